```python
import math
import jax
import jax.numpy as jnp
from jax import lax
import numpy as np

D_MODEL = 1024
BATCH = 16
SEQ = 256
DEPTH = 2
DEC_BATCH = 8
DEC_SEQ = 4096
PAST_LEN = 512

GRID_W = 64
N_EVEN = (DEPTH + 1) // 2
N_ODD = DEPTH // 2
D_MIX = D_MODEL
D_A = D_MIX // 2
NB_A = 8
BS_A = D_A // NB_A
CONV_W = 4
LRU_C = 8.0
D_B = D_MIX - D_A
S5_GROUP = 16
G_B = D_B // S5_GROUP
P_B = 64
H_C = 4
DH_C = D_MIX // H_C
CHUNK = 128
N_EXP = 32
TOP_K = 4
D_FF = D_MODEL
SWIGLU_LIMIT = 7.0
SWIGLU_ALPHA = 1.702
MOE_BLOCK = 128
ALPHA_DN = (2 * DEPTH) ** 0.25
BETA_DN = (8 * DEPTH) ** -0.25
LN_EPS = 1e-5

kernel_name = 'hybrid_rglru_s5_mlstm_moe_diffusion_step'


def layer_norm(x, g, b):
    xf = x.astype(jnp.float32)
    mu = jnp.mean(xf, -1, keepdims=True)
    var = jnp.mean(jnp.square(xf - mu), -1, keepdims=True)
    return ((xf - mu) * lax.rsqrt(var + LN_EPS) * g.astype(jnp.float32) + b.astype(jnp.float32)).astype(x.dtype)


def modulation(cvec, w_mod, b_mod):
    m = jax.nn.silu(cvec) @ w_mod + b_mod
    return jnp.split(m[:, None, :], 6, axis=-1)


def dwconv_centred(x, w, b):
    T = x.shape[1]
    left = CONV_W // 2
    xp = jnp.pad(x, ((0, 0), (left, CONV_W - 1 - left), (0, 0)))
    out = xp[:, 0:T] * w[0]
    for j in range(1, CONV_W):
        out = out + xp[:, j:j + T] * w[j]
    return out + b


def linear_combine(e1, e2):
    a1, b1 = e1
    a2, b2 = e2
    return a1 * a2, a2 * b1 + b2


def to_col_major(x):
    B_, T, C = x.shape
    rows = T // GRID_W
    return x.reshape(B_, rows, GRID_W, C).transpose(0, 2, 1, 3).reshape(B_, T, C)


def to_row_major(x):
    B_, T, C = x.shape
    rows = T // GRID_W
    return x.reshape(B_, GRID_W, rows, C).transpose(0, 2, 1, 3).reshape(B_, T, C)


def rglru_dir(xc, wa, ba, wi, bi, lam, h0, reverse):
    B_, T, _ = xc.shape
    xb = xc.reshape(B_, T, NB_A, BS_A)
    r = jax.nn.sigmoid(jnp.einsum('btnc,ncd->btnd', xb, wa).reshape(B_, T, D_A) + ba).astype(jnp.float32)
    i = jax.nn.sigmoid(jnp.einsum('btnc,ncd->btnd', xb, wi).reshape(B_, T, D_A) + bi)
    log_a = LRU_C * r * jax.nn.log_sigmoid(lam.astype(jnp.float32))
    bx = jnp.sqrt(-jnp.expm1(2.0 * log_a)) * (i * xc).astype(jnp.float32)
    a_cum, h = lax.associative_scan(linear_combine, (jnp.exp(log_a), bx), reverse=reverse, axis=1)
    h = h + a_cum * h0.astype(jnp.float32)[:, None]
    h_fin = h[:, 0] if reverse else h[:, -1]
    return h, h_fin


def s5_dir(ug, a_re, a_im, log_dt, b_re, b_im, c_re, c_im, s0_re, s0_im, reverse):
    f32 = jnp.float32
    lam = lax.complex(a_re.astype(f32), a_im.astype(f32))
    dt = jnp.exp(log_dt.astype(f32))[:, None]
    a_bar = jnp.exp(lam * dt)
    b_bar = ((a_bar - 1.0) / lam)[..., None] * lax.complex(b_re.astype(f32), b_im.astype(f32))
    bu = jnp.einsum('gps,btgs->btgp', b_bar, ug.astype(f32))
    T = ug.shape[1]
    a_seq = jnp.broadcast_to(a_bar, (1, T) + a_bar.shape)
    a_cum, s = lax.associative_scan(linear_combine, (a_seq, bu), reverse=reverse, axis=1)
    s = s + a_cum * lax.complex(s0_re.astype(f32), s0_im.astype(f32))[:, None]
    y = jnp.real(jnp.einsum('gsp,btgp->btgs', lax.complex(c_re.astype(f32), c_im.astype(f32)), s))
    s_fin = s[:, 0] if reverse else s[:, -1]
    return y, s_fin


def mlstm_dir(q, k, v, ig, fg, C0, n0, m0, reverse):
    if reverse:
        q, k, v, ig, fg = [jnp.flip(a, 2) for a in (q, k, v, ig, fg)]
    B_, H, T, DH = q.shape
    nc = T // CHUNK

    def chunks(a):
        return jnp.moveaxis(a.reshape((B_, H, nc, CHUNK) + a.shape[3:]), 2, 0)

    lf = jax.nn.log_sigmoid(fg)
    mask = jnp.tril(jnp.ones((CHUNK, CHUNK), bool))

    def step(carry, inp):
        C, n, m = carry
        qc, kc, vc, lic, lfc = inp
        b = jnp.cumsum(lfc, axis=-1)
        log_d = jnp.where(mask, b[..., :, None] - b[..., None, :] + lic[..., None, :], -jnp.inf)
        m_inter = b + m[..., None]
        m_t = jnp.maximum(m_inter, jnp.max(log_d, axis=-1))
        dmat = jnp.exp(log_d - m_t[..., None])
        w_inter = jnp.exp(m_inter - m_t)
        s = jnp.einsum('bhtd,bhsd->bhts', qc, kc) * dmat
        num = w_inter[..., None] * jnp.einsum('bhvk,bhtk->bhtv', C, qc) + jnp.einsum('bhts,bhsv->bhtv', s, vc)
        den = w_inter * jnp.einsum('bhk,bhtk->bht', n, qc) + jnp.sum(s, axis=-1)
        h = num / jnp.maximum(jnp.abs(den), jnp.exp(-m_t))[..., None]
        m_new = m_t[..., -1]
        w_s = jnp.exp(b[..., -1:] - b + lic - m_new[..., None])
        decay = jnp.exp(b[..., -1] + m - m_new)
        C_new = decay[..., None, None] * C + jnp.einsum('bhs,bhsv,bhsk->bhvk', w_s, vc, kc)
        n_new = decay[..., None] * n + jnp.einsum('bhs,bhsk->bhk', w_s, kc)
        return (C_new, n_new, m_new), h

    init = (C0.astype(jnp.float32), n0.astype(jnp.float32), m0.astype(jnp.float32))
    fin, h = lax.scan(step, init, (chunks(q), chunks(k), chunks(v), chunks(ig), chunks(lf)))
    h = jnp.moveaxis(h, 0, 2).reshape(B_, H, T, DH)
    if reverse:
        h = jnp.flip(h, 2)
    return h, fin


def mixer_even(u, P, j, st):
    h0, s0_re, s0_im = st
    B_, T, _ = u.shape
    z = u @ P['w_in0'][j]
    xa, ga, ub = z[..., :D_A], z[..., D_A:2 * D_A], z[..., 2 * D_A:]
    xc = dwconv_centred(xa, P['conv_a_w'][j], P['conv_a_b'][j])
    hs, hfins = [], []
    for d in range(2):
        h, hf = rglru_dir(xc, P['rg_wa'][j, d], P['rg_ba'][j, d], P['rg_wi'][j, d], P['rg_bi'][j, d],
                          P['rg_lam'][j, d], h0[:, d], d == 1)
        hs.append(h)
        hfins.append(hf)
    ya = (hs[0] + hs[1]).astype(u.dtype) * jax.nn.gelu(ga)
    ug = ub.reshape(B_, T, G_B, S5_GROUP)
    ys, sfins = [], []
    for d in range(2):
        y, sf = s5_dir(ug, P['s5_a_re'][j, d], P['s5_a_im'][j, d], P['s5_log_dt'][j, d], P['s5_b_re'][j, d],
                       P['s5_b_im'][j, d], P['s5_c_re'][j, d], P['s5_c_im'][j, d], s0_re[:, d], s0_im[:, d], d == 1)
        ys.append(y)
        sfins.append(sf)
    yb = (ys[0] + ys[1]).reshape(B_, T, D_B).astype(u.dtype) + P['s5_d'][j] * ub
    g = jax.nn.gelu(yb)
    yb = g * jax.nn.sigmoid(g @ P['glu_w'][j] + P['glu_b'][j])
    out = jnp.concatenate([ya, yb], axis=-1) @ P['w_out0'][j]
    new = (jnp.stack(hfins, 1), jnp.stack([jnp.real(s) for s in sfins], 1), jnp.stack([jnp.imag(s) for s in sfins], 1))
    return out, new


def mixer_odd(u, P, j, st, grid):
    C0, n0, m0 = st
    if grid:
        u = to_col_major(u)
    B_, T, _ = u.shape
    z = u @ P['w_in1'][j]
    qk = jax.nn.silu(dwconv_centred(z[..., :2 * D_MIX], P['conv_c_w'][j], P['conv_c_b'][j]))
    v = z[..., 2 * D_MIX:3 * D_MIX]
    o = z[..., 3 * D_MIX:]

    def heads(a):
        return a.reshape(B_, T, H_C, DH_C).transpose(0, 2, 1, 3).astype(jnp.float32)

    q = heads(qk[..., :D_MIX])
    k = heads(qk[..., D_MIX:]) * (DH_C ** -0.5)
    vh = heads(v)
    gates = (u @ P['w_gate1'][j] + P['b_gate1'][j]).astype(jnp.float32)
    gates = gates.reshape(B_, T, 2, 2, H_C).transpose(2, 3, 0, 4, 1)
    hs, fins = [], []
    for d in range(2):
        h, fin = mlstm_dir(q, k, vh, gates[d, 0], gates[d, 1], C0[:, d], n0[:, d], m0[:, d], d == 1)
        hs.append(h)
        fins.append(fin)
    h = (hs[0] + hs[1]).transpose(0, 2, 1, 3).reshape(B_, T, D_MIX).astype(u.dtype)
    y = (jax.nn.sigmoid(o) * h) @ P['w_out1'][j]
    if grid:
        y = to_row_major(y)
    new = tuple(jnp.stack([fins[0][i], fins[1][i]], 1) for i in range(3))
    return y, new


def moe(x, P, l):
    B_, T, D = x.shape
    xf = x.reshape(-1, D)
    N = xf.shape[0]
    w_gu, b_gu, w_dn, b_dn = P['w_gu'][l], P['b_gu'][l], P['w_down'][l], P['b_down'][l]
    logits = xf.astype(jnp.float32) @ P['w_router'][l].astype(jnp.float32) + P['b_router'][l].astype(jnp.float32)
    top_v, top_i = lax.top_k(logits, TOP_K)
    gates = jax.nn.softmax(top_v, axis=-1)
    M = N * TOP_K
    flat_e = top_i.reshape(M)
    order = jnp.argsort(flat_e)
    sorted_e = flat_e[order]
    counts = jnp.bincount(flat_e, length=N_EXP)
    padded = (counts + MOE_BLOCK - 1) // MOE_BLOCK * MOE_BLOCK
    pad_end = jnp.cumsum(padded)
    pad_start = pad_end - padded
    start = jnp.cumsum(counts) - counts
    dest = pad_start[sorted_e] + jnp.arange(M) - start[sorted_e]
    n_blk = -(-M // MOE_BLOCK) + N_EXP
    slots = n_blk * MOE_BLOCK
    slot_tok = jnp.zeros((slots,), jnp.int32).at[dest].set((order // TOP_K).astype(jnp.int32))
    slot_gate = jnp.zeros((slots,), jnp.float32).at[dest].set(gates.reshape(M)[order])
    blk_e = jnp.minimum(jnp.searchsorted(pad_end, jnp.arange(n_blk) * MOE_BLOCK, side='right'), N_EXP - 1)

    def expert_block(args):
        tok, e, g = args
        hgu = xf[tok] @ w_gu[e] + b_gu[e]
        gt = jnp.minimum(hgu[:, :D_FF], SWIGLU_LIMIT)
        up = jnp.clip(hgu[:, D_FF:], -SWIGLU_LIMIT, SWIGLU_LIMIT)
        act = (up + 1.0) * gt * jax.nn.sigmoid(SWIGLU_ALPHA * gt)
        return (act @ w_dn[e] + b_dn[e]) * g[:, None].astype(x.dtype)

    outs = lax.map(expert_block, (slot_tok.reshape(n_blk, MOE_BLOCK), blk_e, slot_gate.reshape(n_blk, MOE_BLOCK)))
    y = jnp.zeros_like(xf).at[slot_tok].add(outs.reshape(slots, D))
    return y.reshape(B_, T, D)


def trunk(x, cvec, st, P, grid):
    h_rg, s_re, s_im, mC, mn, mm = st
    even_states, odd_states = [], []
    for l in range(DEPTH):
        sh1, sc1, g1, sh2, sc2, g2 = modulation(cvec, P['w_mod'][l], P['b_mod'][l])
        u = x * (1.0 + sc1) + sh1
        j = l // 2
        if l % 2 == 0:
            y, ns = mixer_even(u, P, j, (h_rg[:, j], s_re[:, j], s_im[:, j]))
            even_states.append(ns)
        else:
            y, ns = mixer_odd(u, P, j, (mC[:, j], mn[:, j], mm[:, j]), grid)
            odd_states.append(ns)
        x = layer_norm(ALPHA_DN * x + g1 * y, P['ln_g'][l, 0], P['ln_b'][l, 0])
        u = x * (1.0 + sc2) + sh2
        x = layer_norm(ALPHA_DN * x + g2 * moe(u, P, l), P['ln_g'][l, 1], P['ln_b'][l, 1])
    new_even = tuple(jnp.stack([s[i] for s in even_states], 1) for i in range(3))
    new_odd = tuple(jnp.stack([s[i] for s in odd_states], 1) for i in range(3))
    return x, new_even + new_odd


def setup_inputs(seed: int = 0) -> dict:
    key = jax.random.key(seed)
    ks = iter(jax.random.split(key, 64))
    f32 = jnp.float32

    def nrm(shape, scale):
        return jax.random.normal(next(ks), shape, f32) * scale

    def unif(shape, lo, hi):
        return jax.random.uniform(next(ks), shape, f32, lo, hi)

    d_in = D_MODEL ** -0.5
    x_prompt = nrm((BATCH, SEQ, D_MODEL), 1.0)
    x_sample = nrm((DEC_BATCH, DEC_SEQ, D_MODEL), 1.0)
    c = nrm((DEC_BATCH, D_MODEL), 1.0)
    c_ctx = nrm((D_MODEL,), 1.0)
    state_rglru = nrm((DEC_BATCH, N_EVEN, 2, D_A), 0.5)
    state_s5_re = nrm((DEC_BATCH, N_EVEN, 2, G_B, P_B), 0.5)
    state_s5_im = nrm((DEC_BATCH, N_EVEN, 2, G_B, P_B), 0.5)
    state_mlstm_C = nrm((DEC_BATCH, N_ODD, 2, H_C, DH_C, DH_C), 0.1)
    state_mlstm_n = nrm((DEC_BATCH, N_ODD, 2, H_C, DH_C), 0.1)
    state_mlstm_m = nrm((DEC_BATCH, N_ODD, 2, H_C), 1.0)
    w_mod = nrm((DEPTH, D_MODEL, 6 * D_MODEL), 0.5 * d_in)
    b_mod = nrm((DEPTH, 6 * D_MODEL), 0.01)
    ln_g = 1.0 + nrm((DEPTH, 2, D_MODEL), 0.01)
    ln_b = nrm((DEPTH, 2, D_MODEL), 0.01)
    w_in0 = nrm((N_EVEN, D_MODEL, 2 * D_A + D_B), d_in)
    conv_a_w = nrm((N_EVEN, CONV_W, D_A), CONV_W ** -0.5)
    conv_a_b = nrm((N_EVEN, D_A), 0.01)
    rg_wa = nrm((N_EVEN, 2, NB_A, BS_A, BS_A), BS_A ** -0.5)
    rg_ba = nrm((N_EVEN, 2, D_A), 0.1)
    rg_wi = nrm((N_EVEN, 2, NB_A, BS_A, BS_A), BS_A ** -0.5)
    rg_bi = nrm((N_EVEN, 2, D_A), 0.1)
    a0 = unif((N_EVEN, 2, D_A), 0.9, 0.999)
    s_lam = a0 ** (1.0 / LRU_C)
    rg_lam = jnp.log(s_lam) - jnp.log1p(-s_lam)
    s5_a_re = -0.5 + nrm((N_EVEN, 2, G_B, P_B), 0.01)
    s5_a_im = math.pi * jnp.arange(P_B, dtype=f32) + nrm((N_EVEN, 2, G_B, P_B), 0.01)
    s5_log_dt = unif((N_EVEN, 2, G_B), math.log(1e-3), math.log(1e-1))
    s5_b_re = nrm((N_EVEN, 2, G_B, P_B, S5_GROUP), (2 * S5_GROUP) ** -0.5)
    s5_b_im = nrm((N_EVEN, 2, G_B, P_B, S5_GROUP), (2 * S5_GROUP) ** -0.5)
    s5_c_re = nrm((N_EVEN, 2, G_B, S5_GROUP, P_B), (2 * P_B) ** -0.5)
    s5_c_im = nrm((N_EVEN, 2, G_B, S5_GROUP, P_B), (2 * P_B) ** -0.5)
    s5_d = nrm((N_EVEN, D_B), 1.0)
    glu_w = nrm((N_EVEN, D_B, D_B), D_B ** -0.5)
    glu_b = nrm((N_EVEN, D_B), 0.01)
    w_out0 = nrm((N_EVEN, D_MIX, D_MODEL), BETA_DN * D_MIX ** -0.5)
    w_in1 = nrm((N_ODD, D_MODEL, 4 * D_MIX), d_in)
    w_gate1 = nrm((N_ODD, D_MODEL, 4 * H_C), d_in)
    b_i = nrm((N_ODD, 2, 1, H_C), 0.1)
    b_f = jnp.linspace(3.0, 6.0, H_C, dtype=f32) + nrm((N_ODD, 2, 1, H_C), 0.1)
    b_gate1 = jnp.concatenate([b_i, b_f], axis=2).reshape(N_ODD, 4 * H_C)
    conv_c_w = nrm((N_ODD, CONV_W, 2 * D_MIX), CONV_W ** -0.5)
    conv_c_b = nrm((N_ODD, 2 * D_MIX), 0.01)
    w_out1 = nrm((N_ODD, D_MIX, D_MODEL), BETA_DN * D_MIX ** -0.5)
    w_router = nrm((DEPTH, D_MODEL, N_EXP), d_in)
    b_router = nrm((DEPTH, N_EXP), 0.01)
    w_gu = nrm((DEPTH, N_EXP, D_MODEL, 2 * D_FF), d_in)
    b_gu = nrm((DEPTH, N_EXP, 2 * D_FF), 0.01)
    w_down = nrm((DEPTH, N_EXP, D_FF, D_MODEL), BETA_DN * D_FF ** -0.5)
    b_down = nrm((DEPTH, N_EXP, D_MODEL), 0.01)
    return {'x_prompt': x_prompt, 'x_sample': x_sample, 'c': c, 'c_ctx': c_ctx,
            'state_rglru': state_rglru, 'state_s5_re': state_s5_re, 'state_s5_im': state_s5_im,
            'state_mlstm_C': state_mlstm_C, 'state_mlstm_n': state_mlstm_n, 'state_mlstm_m': state_mlstm_m,
            'w_mod': w_mod, 'b_mod': b_mod, 'ln_g': ln_g, 'ln_b': ln_b,
            'w_in0': w_in0, 'conv_a_w': conv_a_w, 'conv_a_b': conv_a_b,
            'rg_wa': rg_wa, 'rg_ba': rg_ba, 'rg_wi': rg_wi, 'rg_bi': rg_bi, 'rg_lam': rg_lam,
            's5_a_re': s5_a_re, 's5_a_im': s5_a_im, 's5_log_dt': s5_log_dt,
            's5_b_re': s5_b_re, 's5_b_im': s5_b_im, 's5_c_re': s5_c_re, 's5_c_im': s5_c_im,
            's5_d': s5_d, 'glu_w': glu_w, 'glu_b': glu_b, 'w_out0': w_out0,
            'w_in1': w_in1, 'w_gate1': w_gate1, 'b_gate1': b_gate1,
            'conv_c_w': conv_c_w, 'conv_c_b': conv_c_b, 'w_out1': w_out1,
            'w_router': w_router, 'b_router': b_router, 'w_gu': w_gu, 'b_gu': b_gu,
            'w_down': w_down, 'b_down': b_down}


def reference(x_prompt, x_sample, c, c_ctx, state_rglru, state_s5_re, state_s5_im,
              state_mlstm_C, state_mlstm_n, state_mlstm_m,
              w_mod, b_mod, ln_g, ln_b, w_in0, conv_a_w, conv_a_b,
              rg_wa, rg_ba, rg_wi, rg_bi, rg_lam,
              s5_a_re, s5_a_im, s5_log_dt, s5_b_re, s5_b_im, s5_c_re, s5_c_im,
              s5_d, glu_w, glu_b, w_out0, w_in1, w_gate1, b_gate1,
              conv_c_w, conv_c_b, w_out1, w_router, b_router, w_gu, b_gu, w_down, b_down):
    P = dict(w_mod=w_mod, b_mod=b_mod, ln_g=ln_g, ln_b=ln_b, w_in0=w_in0, conv_a_w=conv_a_w,
             conv_a_b=conv_a_b, rg_wa=rg_wa, rg_ba=rg_ba, rg_wi=rg_wi, rg_bi=rg_bi, rg_lam=rg_lam,
             s5_a_re=s5_a_re, s5_a_im=s5_a_im, s5_log_dt=s5_log_dt, s5_b_re=s5_b_re, s5_b_im=s5_b_im,
             s5_c_re=s5_c_re, s5_c_im=s5_c_im, s5_d=s5_d, glu_w=glu_w, glu_b=glu_b, w_out0=w_out0,
             w_in1=w_in1, w_gate1=w_gate1, b_gate1=b_gate1, conv_c_w=conv_c_w, conv_c_b=conv_c_b,
             w_out1=w_out1, w_router=w_router, b_router=b_router, w_gu=w_gu, b_gu=b_gu,
             w_down=w_down, b_down=b_down)
    bp = x_prompt.shape[0]
    f32 = jnp.float32
    zero_state = (jnp.zeros((bp, N_EVEN, 2, D_A), f32),
                  jnp.zeros((bp, N_EVEN, 2, G_B, P_B), f32),
                  jnp.zeros((bp, N_EVEN, 2, G_B, P_B), f32),
                  jnp.zeros((bp, N_ODD, 2, H_C, DH_C, DH_C), f32),
                  jnp.zeros((bp, N_ODD, 2, H_C, DH_C), f32),
                  jnp.zeros((bp, N_ODD, 2, H_C), f32))
    y_prompt, ctx_state = trunk(x_prompt, c_ctx[None, :], zero_state, P, False)
    new_rglru, new_s5_re, new_s5_im, new_mlstm_C, new_mlstm_n, new_mlstm_m = ctx_state
    lat_state = (state_rglru, state_s5_re, state_s5_im, state_mlstm_C, state_mlstm_n, state_mlstm_m)
    y_sample, _ = trunk(x_sample, c, lat_state, P, True)
    return (y_prompt, y_sample, new_rglru, new_s5_re, new_s5_im, new_mlstm_C, new_mlstm_n, new_mlstm_m)
```

```python
import functools
import math

import jax
import jax.numpy as jnp
from jax import lax
from jax.experimental import pallas as pl
from jax.experimental.pallas import tpu as pltpu

F32 = jnp.float32
BF16 = jnp.bfloat16
I32 = jnp.int32
HI = lax.Precision.HIGHEST

D_MODEL = 1024
DEPTH = 2
GRID_W = 64
D_A = 512
NB_A = 8
BS_A = D_A // NB_A
CONV_W = 4
LRU_C = 8.0
D_B = 512
S5_GROUP = 16
G_B = D_B // S5_GROUP
P_B = 64
H_C = 4
DH_C = D_MODEL // H_C
CHUNK = 128
N_EXP = 32
TOP_K = 4
D_FF = D_MODEL
SWIGLU_LIMIT = 7.0
SWIGLU_ALPHA = 1.702
ALPHA_DN = (2 * DEPTH) ** 0.25
LN_EPS = 1e-5

S5_L = 16
S5_W = S5_L * S5_GROUP
TOKEN_TILE = 256
EXPERT_TILE = 256
VMEM_LIMIT = 56 * 1024 * 1024


def _cparams(sem, vmem=VMEM_LIMIT):
    return pltpu.CompilerParams(dimension_semantics=sem, vmem_limit_bytes=vmem)


def _layer_norm(v, g, b):
    mu = jnp.mean(v, axis=-1, keepdims=True)
    c = v - mu
    var = jnp.mean(c * c, axis=-1, keepdims=True)
    return c * lax.rsqrt(var + LN_EPS) * g + b


def _mod_spec(layer, which, row_off, row_stride):
    return pl.BlockSpec((None, None, None, 1, D_MODEL),
                        lambda b, j: (layer, which, row_off + b * row_stride, 0, 0))


def _mod_body(c_ref, w_ref, b_ref, o_ref):
    s = jax.nn.silu(c_ref[...])
    o_ref[...] = jnp.dot(s, w_ref[...], precision=HI, preferred_element_type=F32) + b_ref[...]


def _modulation(cv, w_mod, b_mod):
    R, D = cv.shape
    L, _, N6 = w_mod.shape
    tn = N6 // 6
    out = pl.pallas_call(
        _mod_body,
        grid=(L, N6 // tn),
        in_specs=[pl.BlockSpec((R, D), lambda l, j: (0, 0)),
                  pl.BlockSpec((None, D, tn), lambda l, j: (l, 0, j)),
                  pl.BlockSpec((None, 1, tn), lambda l, j: (l, 0, j))],
        out_specs=pl.BlockSpec((None, R, tn), lambda l, j: (l, 0, j)),
        out_shape=jax.ShapeDtypeStruct((L, R, N6), F32),
        compiler_params=_cparams(("arbitrary", "arbitrary")),
        name="modulation",
    )(cv, w_mod, b_mod.reshape(L, 1, N6))
    return out.reshape(L, R, 6, D).transpose(0, 2, 1, 3)[:, :, :, None, :]


def _inproj0_body(x_ref, sc_ref, sh_ref, w_ref, xa_ref, ga_ref, ub_ref):
    u = x_ref[...] * (1.0 + sc_ref[...]) + sh_ref[...]
    z = jnp.dot(u.astype(BF16), w_ref[...], preferred_element_type=F32)
    xa_ref[...] = z[:, :D_A]
    ga_ref[...] = z[:, D_A:2 * D_A]
    ub_ref[...] = z[:, 2 * D_A:]


def _inproj0(x, mods, w_in0, row_off, row_stride):
    B, T, D = x.shape
    tm = min(TOKEN_TILE, T)
    nz = w_in0.shape[1]
    return pl.pallas_call(
        _inproj0_body,
        grid=(B, T // tm),
        in_specs=[pl.BlockSpec((None, tm, D), lambda b, j: (b, j, 0)),
                  _mod_spec(0, 1, row_off, row_stride),
                  _mod_spec(0, 0, row_off, row_stride),
                  pl.BlockSpec((D, nz), lambda b, j: (0, 0))],
        out_specs=[pl.BlockSpec((tm, D_A), lambda b, j: (j, b)),
                   pl.BlockSpec((None, tm, D_A), lambda b, j: (b, j, 0)),
                   pl.BlockSpec((None, tm, D_B), lambda b, j: (b, j, 0))],
        out_shape=[jax.ShapeDtypeStruct((T, B * D_A), F32),
                   jax.ShapeDtypeStruct((B, T, D_A), F32),
                   jax.ShapeDtypeStruct((B, T, D_B), F32)],
        compiler_params=_cparams(("arbitrary", "arbitrary")),
        name="inproj0",
    )(x, mods, mods, w_in0)


def _rglru_body(cur_ref, prev_ref, next_ref, cw_ref, cb_ref, wa_ref, wi_ref, ba_ref, bi_ref, lam_ref, h0_ref,
                h_ref, hfin_ref, ext_s, a_s, b_s, carry_s, *, tt, n_t):
    d = pl.program_id(0)
    j = pl.program_id(1)
    jj = j + d * (n_t - 1 - 2 * j)
    nb = cur_ref.shape[1]
    half = D_A // 2
    ext_s[0:2] = jnp.where(jj == 0, 0.0, prev_ref[...])
    ext_s[2:tt + 2] = cur_ref[...]
    ext_s[tt + 2:tt + 3] = jnp.where(jj == n_t - 1, 0.0, next_ref[...])
    xc = ext_s[0:tt] * cw_ref[0]
    for k in range(1, CONV_W):
        xc = xc + ext_s[k:k + tt] * cw_ref[k]
    xc = xc + cb_ref[...]
    x2 = xc.reshape(tt * nb, D_A)
    xb = x2.astype(BF16)

    def gate(w_ref, bias_ref):
        lo = jnp.dot(xb[:, :half], w_ref[0], preferred_element_type=F32)
        hi = jnp.dot(xb[:, half:], w_ref[1], preferred_element_type=F32)
        return jax.nn.sigmoid(jnp.concatenate([lo, hi], axis=1) + bias_ref[...])

    r = gate(wa_ref, ba_ref)
    i = gate(wi_ref, bi_ref)
    log_a = LRU_C * r * jax.nn.log_sigmoid(lam_ref[...])
    a = jnp.exp(log_a)
    one_minus_a2 = -jnp.tanh(log_a) * (a * a + 1.0)
    a_s[...] = a.reshape(tt, nb, D_A)
    b_s[...] = (jnp.sqrt(one_minus_a2) * (i * x2)).reshape(tt, nb, D_A)

    @pl.when(j == 0)
    def _():
        carry_s[...] = h0_ref[...]

    def step(t, h):
        h = a_s[t] * h + b_s[t]
        h_ref[t] = h
        return h

    @pl.when(d == 0)
    def _():
        carry_s[...] = lax.fori_loop(0, tt, step, carry_s[...], unroll=8)

    @pl.when(d == 1)
    def _():
        carry_s[...] = lax.fori_loop(0, tt, lambda t, h: step(tt - 1 - t, h), carry_s[...], unroll=8)

    hfin_ref[...] = carry_s[...]


def _block_diag_halves(w):
    nd = w.shape[0]
    per = NB_A // 2
    w = w.reshape(nd, 2, per, BS_A, BS_A)
    eye = jnp.eye(per, dtype=w.dtype)
    out = jnp.einsum('dhncz,nm->dhncmz', w, eye)
    return out.reshape(nd, 2, per * BS_A, per * BS_A)


def _rglru(xa_tm, B, conv_w, conv_b, wa, ba, wi, bi, lam, h0):
    T = xa_tm.shape[0]
    tt = min(T, 1024 // B)
    n_t = T // tt
    x3 = xa_tm.reshape(T, B, D_A)

    def cur_map(d, j):
        return (j + d * (n_t - 1 - 2 * j), 0, 0)

    def prev_map(d, j):
        jj = j + d * (n_t - 1 - 2 * j)
        return (jnp.maximum(jj * (tt // 2) - 1, 0), 0, 0)

    def next_map(d, j):
        jj = j + d * (n_t - 1 - 2 * j)
        return (jnp.minimum((jj + 1) * tt, T - 1), 0, 0)

    dir_spec3 = pl.BlockSpec((None, 1, D_A), lambda d, j: (d, 0, 0))
    h, hfin = pl.pallas_call(
        functools.partial(_rglru_body, tt=tt, n_t=n_t),
        grid=(2, n_t),
        in_specs=[pl.BlockSpec((tt, B, D_A), cur_map),
                  pl.BlockSpec((2, B, D_A), prev_map),
                  pl.BlockSpec((1, B, D_A), next_map),
                  pl.BlockSpec((CONV_W, 1, D_A), lambda d, j: (0, 0, 0)),
                  pl.BlockSpec((1, 1, D_A), lambda d, j: (0, 0, 0)),
                  pl.BlockSpec((None, 2, D_A // 2, D_A // 2), lambda d, j: (d, 0, 0, 0)),
                  pl.BlockSpec((None, 2, D_A // 2, D_A // 2), lambda d, j: (d, 0, 0, 0)),
                  dir_spec3, dir_spec3, dir_spec3,
                  pl.BlockSpec((None, B, D_A), lambda d, j: (d, 0, 0))],
        out_specs=[pl.BlockSpec((None, tt, B, D_A), lambda d, j: (d,) + cur_map(d, j)),
                   pl.BlockSpec((None, B, D_A), lambda d, j: (d, 0, 0))],
        out_shape=[jax.ShapeDtypeStruct((2, T, B, D_A), F32),
                   jax.ShapeDtypeStruct((2, B, D_A), F32)],
        scratch_shapes=[pltpu.VMEM((tt + 3, B, D_A), F32),
                        pltpu.VMEM((tt, B, D_A), F32),
                        pltpu.VMEM((tt, B, D_A), F32),
                        pltpu.VMEM((B, D_A), F32)],
        compiler_params=_cparams(("arbitrary", "arbitrary")),
        name="rglru",
    )(x3, x3, x3, conv_w.reshape(CONV_W, 1, D_A), conv_b.reshape(1, 1, D_A),
      _block_diag_halves(wa).astype(BF16), _block_diag_halves(wi).astype(BF16),
      ba.reshape(2, 1, D_A), bi.reshape(2, 1, D_A), lam.reshape(2, 1, D_A), h0)
    return h.reshape(2, T, B * D_A), hfin


def _s5_matrices(a_re, a_im, log_dt, b_re, b_im, c_re, c_im):
    L = S5_L
    lam = lax.complex(a_re.astype(F32), a_im.astype(F32))
    dt = jnp.exp(log_dt.astype(F32))[..., None]
    ldt = lam * dt
    a_bar = jnp.exp(ldt)
    b_bar = ((a_bar - 1.0) / lam)[..., None] * lax.complex(b_re.astype(F32), b_im.astype(F32))
    cc = lax.complex(c_re.astype(F32), c_im.astype(F32))
    ks = jnp.arange(L + 1, dtype=F32)
    pw = jnp.exp(ldt[:, :, None, :] * ks[None, None, :, None].astype(jnp.complex64))
    kern = jnp.real(jnp.einsum('dgjp,dgkp,dgpi->dgkji', cc, pw[:, :, :L], b_bar, precision=HI))
    s_idx = jnp.arange(L)[:, None]
    t_idx = jnp.arange(L)[None, :]
    lag_f = jnp.clip(t_idx - s_idx, 0, L - 1)
    lag_b = jnp.clip(s_idx - t_idx, 0, L - 1)
    m_f = jnp.where((t_idx >= s_idx)[None, :, :, None, None], kern[0][:, lag_f], 0.0)
    m_b = jnp.where((s_idx >= t_idx)[None, :, :, None, None], kern[1][:, lag_b], 0.0)
    m = (m_f + m_b).transpose(0, 1, 4, 2, 3).reshape(G_B, S5_W, S5_W)
    g_f = pw[0][:, ::-1][:, 1:, :, None] * b_bar[0][:, None]
    g_b = pw[1][:, :L, :, None] * b_bar[1][:, None]

    def g_cols(x):
        return x.transpose(0, 1, 3, 2).reshape(G_B, S5_W, P_B)

    gs = jnp.concatenate([g_cols(jnp.real(g_f)), g_cols(jnp.real(g_b)),
                          g_cols(jnp.imag(g_f)), g_cols(jnp.imag(g_b))], axis=-1)
    e_f = cc[0][:, None] * pw[0][:, 1:, None, :]
    e_b = cc[1][:, None] * pw[1][:, ::-1][:, :L, None, :]

    def e_rows(x):
        return x.transpose(0, 3, 1, 2).reshape(G_B, P_B, S5_W)

    e = jnp.concatenate([e_rows(jnp.real(e_f)), e_rows(jnp.real(e_b)),
                         -e_rows(jnp.imag(e_f)), -e_rows(jnp.imag(e_b))], axis=1)
    a_l = pw[:, :, L]
    al_re = jnp.concatenate([jnp.real(a_l[0]), jnp.real(a_l[1])], axis=-1)
    al_im = jnp.concatenate([jnp.imag(a_l[0]), jnp.imag(a_l[1])], axis=-1)
    mg = jnp.concatenate([m, gs], axis=-1).astype(BF16)
    return mg, e.astype(BF16), al_re[:, None, :], al_im[:, None, :]


def _s5_local_body(u_ref, mg_ref, y_ref, f_ref):
    r = jnp.dot(u_ref[...], mg_ref[...], preferred_element_type=F32)
    y_ref[...] = r[:, :S5_W]
    f_ref[...] = r[:, S5_W:]


def _s5_scan_body(f_ref, are_ref, aim_ref, s0_ref, sin_ref, sfin_ref, *, n_c):
    hw = 2 * P_B
    a_re = are_ref[...]
    a_im = aim_ref[...]
    lane = lax.broadcasted_iota(I32, s0_ref.shape[:2] + (hw,), 2)
    is_fwd = lane < P_B

    def advance(c, s_re, s_im):
        f = f_ref[:, c]
        n_re = a_re * s_re - a_im * s_im + f[..., :hw]
        n_im = a_re * s_im + a_im * s_re + f[..., hw:]
        return n_re, n_im

    def fwd(c, carry):
        s_re, s_im = carry
        sin_ref[:, c, :, 0:hw] = s_re
        sin_ref[:, c, :, hw:2 * hw] = s_im
        return advance(c, s_re, s_im)

    s0 = s0_ref[...]
    init = (s0[..., :hw], s0[..., hw:])
    f_re, f_im = lax.fori_loop(0, n_c, fwd, init)

    def bwd(k, carry):
        c = n_c - 1 - k
        s_re, s_im = carry
        cur = sin_ref[:, c]
        sin_ref[:, c, :, 0:hw] = jnp.where(is_fwd, cur[..., :hw], s_re)
        sin_ref[:, c, :, hw:2 * hw] = jnp.where(is_fwd, cur[..., hw:], s_im)
        return advance(c, s_re, s_im)

    b_re, b_im = lax.fori_loop(0, n_c, bwd, init)
    sfin_ref[:, :, 0:hw] = jnp.where(is_fwd, f_re, b_re)
    sfin_ref[:, :, hw:2 * hw] = jnp.where(is_fwd, f_im, b_im)


def _s5_out_body(y_ref, sin_ref, e_ref, o_ref):
    o_ref[...] = y_ref[...] + jnp.dot(sin_ref[...].astype(BF16), e_ref[...], preferred_element_type=F32)


def _s5(ub, mats, s0):
    mg, e, al_re, al_im = mats
    B, T, _ = ub.shape
    n_c = T // S5_L
    R = n_c * B
    u = ub.reshape(B, n_c, S5_L, G_B, S5_GROUP).transpose(3, 1, 0, 2, 4).reshape(G_B, R, S5_W).astype(BF16)
    tr = min(R, 512)
    y_loc, f_loc = pl.pallas_call(
        _s5_local_body,
        grid=(G_B, R // tr),
        in_specs=[pl.BlockSpec((None, tr, S5_W), lambda g, i: (g, i, 0)),
                  pl.BlockSpec((None, S5_W, 2 * S5_W), lambda g, i: (g, 0, 0))],
        out_specs=[pl.BlockSpec((None, tr, S5_W), lambda g, i: (g, i, 0)),
                   pl.BlockSpec((None, tr, S5_W), lambda g, i: (g, i, 0))],
        out_shape=[jax.ShapeDtypeStruct((G_B, R, S5_W), F32)] * 2,
        compiler_params=_cparams(("arbitrary", "arbitrary")),
        name="s5_local",
    )(u, mg)
    gb = 2
    sin, sfin = pl.pallas_call(
        functools.partial(_s5_scan_body, n_c=n_c),
        grid=(G_B // gb,),
        in_specs=[pl.BlockSpec((gb, n_c, B, S5_W), lambda g: (g, 0, 0, 0)),
                  pl.BlockSpec((gb, 1, 2 * P_B), lambda g: (g, 0, 0)),
                  pl.BlockSpec((gb, 1, 2 * P_B), lambda g: (g, 0, 0)),
                  pl.BlockSpec((gb, B, S5_W), lambda g: (g, 0, 0))],
        out_specs=[pl.BlockSpec((gb, n_c, B, S5_W), lambda g: (g, 0, 0, 0)),
                   pl.BlockSpec((gb, B, S5_W), lambda g: (g, 0, 0))],
        out_shape=[jax.ShapeDtypeStruct((G_B, n_c, B, S5_W), F32),
                   jax.ShapeDtypeStruct((G_B, B, S5_W), F32)],
        compiler_params=_cparams(("arbitrary",)),
        name="s5_scan",
    )(f_loc.reshape(G_B, n_c, B, S5_W), al_re, al_im, s0)
    y = pl.pallas_call(
        _s5_out_body,
        grid=(G_B, R // tr),
        in_specs=[pl.BlockSpec((None, tr, S5_W), lambda g, i: (g, i, 0)),
                  pl.BlockSpec((None, tr, S5_W), lambda g, i: (g, i, 0)),
                  pl.BlockSpec((None, S5_W, S5_W), lambda g, i: (g, 0, 0))],
        out_specs=pl.BlockSpec((None, tr, S5_W), lambda g, i: (g, i, 0)),
        out_shape=jax.ShapeDtypeStruct((G_B, R, S5_W), F32),
        compiler_params=_cparams(("arbitrary", "arbitrary")),
        name="s5_out",
    )(y_loc, sin.reshape(G_B, R, S5_W), e)
    y = y.reshape(G_B, n_c, B, S5_L, S5_GROUP).transpose(2, 1, 3, 0, 4).reshape(B, T, D_B)
    return y, sfin


def _s5_state_to_lanes(s_re, s_im):
    parts = [s_re[:, 0], s_re[:, 1], s_im[:, 0], s_im[:, 1]]
    return jnp.concatenate(parts, axis=-1).transpose(1, 0, 2)


def _s5_lanes_to_state(s):
    s = s.transpose(1, 0, 2).reshape(s.shape[1], G_B, 4, P_B)
    return jnp.stack([s[:, :, 0], s[:, :, 1]], axis=1), jnp.stack([s[:, :, 2], s[:, :, 3]], axis=1)


def _outproj0_body(x_ref, hf_ref, hb_ref, ga_ref, y_ref, ub_ref, g1_ref, d_ref, gw_ref, gb_ref, wo_ref,
                   lg_ref, lb_ref, o_ref):
    ya = (hf_ref[...] + hb_ref[...]) * jax.nn.gelu(ga_ref[...])
    yb = y_ref[...] + d_ref[...] * ub_ref[...]
    g = jax.nn.gelu(yb)
    gate = jax.nn.sigmoid(jnp.dot(g.astype(BF16), gw_ref[...], preferred_element_type=F32) + gb_ref[...])
    yb = g * gate
    out = (jnp.dot(ya.astype(BF16), wo_ref[:D_A], preferred_element_type=F32)
           + jnp.dot(yb.astype(BF16), wo_ref[D_A:], preferred_element_type=F32))
    v = ALPHA_DN * x_ref[...] + g1_ref[...] * out
    o_ref[...] = _layer_norm(v, lg_ref[...], lb_ref[...])


def _outproj0(x, h_tm, ga, y5, ub, mods, s5_d, glu_w, glu_b, w_out0, ln_g, ln_b, row_off, row_stride):
    B, T, D = x.shape
    tm = min(TOKEN_TILE, T)
    tok = lambda w: pl.BlockSpec((None, tm, w), lambda b, j: (b, j, 0))
    vec = lambda w: pl.BlockSpec((1, w), lambda b, j: (0, 0))
    return pl.pallas_call(
        _outproj0_body,
        grid=(B, T // tm),
        in_specs=[tok(D),
                  pl.BlockSpec((None, tm, D_A), lambda b, j: (0, j, b)),
                  pl.BlockSpec((None, tm, D_A), lambda b, j: (1, j, b)),
                  tok(D_A), tok(D_B), tok(D_B),
                  _mod_spec(0, 2, row_off, row_stride),
                  vec(D_B),
                  pl.BlockSpec((D_B, D_B), lambda b, j: (0, 0)),
                  vec(D_B),
                  pl.BlockSpec((D, D), lambda b, j: (0, 0)),
                  vec(D), vec(D)],
        out_specs=tok(D),
        out_shape=jax.ShapeDtypeStruct((B, T, D), F32),
        compiler_params=_cparams(("arbitrary", "arbitrary")),
        name="outproj0",
    )(x, h_tm, h_tm, ga, y5, ub, mods, s5_d.reshape(1, D_B), glu_w.astype(BF16), glu_b.reshape(1, D_B),
      w_out0.astype(BF16), ln_g.reshape(1, D), ln_b.reshape(1, D))


def _router_body(x_ref, sc_ref, sh_ref, wr_ref, br_ref, u_ref, topi_ref, gate_ref, rank_ref, cnt_ref, cnt_s):
    first = jnp.logical_and(pl.program_id(0) == 0, pl.program_id(1) == 0)

    @pl.when(first)
    def _():
        cnt_s[...] = jnp.zeros_like(cnt_s)

    u = x_ref[...] * (1.0 + sc_ref[...]) + sh_ref[...]
    u_ref[...] = u.astype(BF16)
    tm = u.shape[0]
    logits = lax.dot_general(wr_ref[...], u, (((1,), (1,)), ((), ())), precision=HI,
                             preferred_element_type=F32) + br_ref[...]
    e_iota = lax.broadcasted_iota(I32, logits.shape, 0)
    work = logits
    vals, idxs, hots = [], [], []
    for _ in range(TOP_K):
        m = jnp.max(work, axis=0, keepdims=True)
        idx = jnp.min(jnp.where(work == m, e_iota, N_EXP), axis=0, keepdims=True)
        hot = e_iota == idx
        vals.append(m)
        idxs.append(idx)
        hots.append(hot)
        work = jnp.where(hot, -jnp.inf, work)
    ex = [jnp.exp(v - vals[0]) for v in vals]
    den = ex[0] + ex[1] + ex[2] + ex[3]
    gate_ref[...] = jnp.concatenate([e / den for e in ex], axis=0)
    topi_ref[...] = jnp.concatenate(idxs, axis=0)
    hot_sum = jnp.zeros(logits.shape, F32)
    for hot in hots:
        hot_sum = hot_sum + hot.astype(F32)
    before = lax.broadcasted_iota(I32, (tm, tm), 0) < lax.broadcasted_iota(I32, (tm, tm), 1)
    excl = jnp.dot(hot_sum.astype(BF16), before.astype(BF16), preferred_element_type=F32)
    base = excl + cnt_s[...]
    ranks = [jnp.sum(jnp.where(hot, base, 0.0), axis=0, keepdims=True) for hot in hots]
    rank_ref[...] = jnp.concatenate(ranks, axis=0).astype(I32)
    cnt_s[...] = cnt_s[...] + jnp.sum(hot_sum, axis=1, keepdims=True)
    cnt_ref[...] = cnt_s[...]


def _router(x, mods, layer, w_router, b_router, row_off, row_stride):
    B, T, D = x.shape
    tm = min(TOKEN_TILE, T)
    n_j = T // tm
    N = B * T
    lane_spec = pl.BlockSpec((TOP_K, tm), lambda b, j: (0, b * n_j + j))
    return pl.pallas_call(
        _router_body,
        grid=(B, n_j),
        in_specs=[pl.BlockSpec((None, tm, D), lambda b, j: (b, j, 0)),
                  _mod_spec(layer, 4, row_off, row_stride),
                  _mod_spec(layer, 3, row_off, row_stride),
                  pl.BlockSpec((N_EXP, D), lambda b, j: (0, 0)),
                  pl.BlockSpec((N_EXP, 1), lambda b, j: (0, 0))],
        out_specs=[pl.BlockSpec((None, tm, D), lambda b, j: (b, j, 0)),
                   lane_spec, lane_spec, lane_spec,
                   pl.BlockSpec((N_EXP, 1), lambda b, j: (0, 0))],
        out_shape=[jax.ShapeDtypeStruct((B, T, D), BF16),
                   jax.ShapeDtypeStruct((TOP_K, N), I32),
                   jax.ShapeDtypeStruct((TOP_K, N), F32),
                   jax.ShapeDtypeStruct((TOP_K, N), I32),
                   jax.ShapeDtypeStruct((N_EXP, 1), F32)],
        scratch_shapes=[pltpu.VMEM((N_EXP, 1), F32)],
        compiler_params=_cparams(("arbitrary", "arbitrary")),
        name="router",
    )(x, mods, mods, w_router.T, b_router.reshape(N_EXP, 1))


def _expert_body(blk_e_ref, n_used_ref, xs_ref, wgu_ref, bgu_ref, wdn_ref, bdn_ref, o_ref, wgu_s, wdn_s):
    i = pl.program_id(0)
    prev = blk_e_ref[jnp.maximum(i - 1, 0)]
    changed = jnp.logical_or(i == 0, blk_e_ref[i] != prev)

    @pl.when(changed)
    def _():
        wgu_s[...] = wgu_ref[...].astype(BF16)
        wdn_s[...] = wdn_ref[...].astype(BF16)

    @pl.when(i < n_used_ref[0])
    def _():
        h = jnp.dot(xs_ref[...], wgu_s[...], preferred_element_type=F32) + bgu_ref[...]
        gt = jnp.minimum(h[:, :D_FF], SWIGLU_LIMIT)
        up = jnp.clip(h[:, D_FF:], -SWIGLU_LIMIT, SWIGLU_LIMIT)
        act = (up + 1.0) * gt * jax.nn.sigmoid(SWIGLU_ALPHA * gt)
        o_ref[...] = jnp.dot(act.astype(BF16), wdn_s[...], preferred_element_type=F32) + bdn_ref[...]

    @pl.when(i >= n_used_ref[0])
    def _():
        o_ref[...] = jnp.zeros_like(o_ref)


def _experts(xs, blk_e, n_used, w_gu, b_gu, w_dn, b_dn):
    slots, D = xs.shape
    te = EXPERT_TILE
    n_blk = slots // te
    grid_spec = pltpu.PrefetchScalarGridSpec(
        num_scalar_prefetch=2,
        grid=(n_blk,),
        in_specs=[pl.BlockSpec((te, D), lambda i, be, nu: (i, 0)),
                  pl.BlockSpec((None, D, 2 * D_FF), lambda i, be, nu: (be[i], 0, 0)),
                  pl.BlockSpec((None, 1, 2 * D_FF), lambda i, be, nu: (be[i], 0, 0)),
                  pl.BlockSpec((None, D_FF, D), lambda i, be, nu: (be[i], 0, 0)),
                  pl.BlockSpec((None, 1, D), lambda i, be, nu: (be[i], 0, 0))],
        out_specs=pl.BlockSpec((te, D), lambda i, be, nu: (i, 0)),
        scratch_shapes=[pltpu.VMEM((D, 2 * D_FF), BF16), pltpu.VMEM((D_FF, D), BF16)],
    )
    return pl.pallas_call(
        _expert_body,
        grid_spec=grid_spec,
        out_shape=jax.ShapeDtypeStruct((slots, D), F32),
        compiler_params=_cparams(("arbitrary",)),
        name="experts",
    )(blk_e, n_used, xs, w_gu, b_gu.reshape(N_EXP, 1, 2 * D_FF), w_dn, b_dn.reshape(N_EXP, 1, D))


def _combine_body(x_ref, y_ref, gate_ref, g2_ref, lg_ref, lb_ref, o_ref):
    gates = gate_ref[...]
    acc = y_ref[0] * gates[:, 0:1]
    for k in range(1, TOP_K):
        acc = acc + y_ref[k] * gates[:, k:k + 1]
    v = ALPHA_DN * x_ref[...] + g2_ref[...] * acc
    o_ref[...] = _layer_norm(v, lg_ref[...], lb_ref[...])


def _combine(x, gathered, gates_nk, mods, layer, ln_g, ln_b, row_off, row_stride):
    B, T, D = x.shape
    tm = min(TOKEN_TILE, T)
    vec = pl.BlockSpec((1, D), lambda b, j: (0, 0))
    return pl.pallas_call(
        _combine_body,
        grid=(B, T // tm),
        in_specs=[pl.BlockSpec((None, tm, D), lambda b, j: (b, j, 0)),
                  pl.BlockSpec((TOP_K, None, tm, D), lambda b, j: (0, b, j, 0)),
                  pl.BlockSpec((None, tm, TOP_K), lambda b, j: (b, j, 0)),
                  _mod_spec(layer, 5, row_off, row_stride),
                  vec, vec],
        out_specs=pl.BlockSpec((None, tm, D), lambda b, j: (b, j, 0)),
        out_shape=jax.ShapeDtypeStruct((B, T, D), F32),
        compiler_params=_cparams(("arbitrary", "arbitrary")),
        name="moe_combine",
    )(x, gathered, gates_nk, mods, ln_g.reshape(1, D), ln_b.reshape(1, D))


def _moe_layer(x, mods, layer, w_router, b_router, w_gu, b_gu, w_dn, b_dn, ln_g, ln_b, row_off, row_stride):
    B, T, D = x.shape
    N = B * T
    te = EXPERT_TILE
    u2, topi, gates, rank, cnt = _router(x, mods, layer, w_router, b_router, row_off, row_stride)
    counts = cnt[:, 0].astype(I32)
    padded = (counts + te - 1) // te * te
    pad_end = jnp.cumsum(padded)
    pad_start = pad_end - padded
    dest = pad_start[topi] + rank
    n_blk = (N * TOP_K) // te + N_EXP
    blk_e = jnp.minimum(jnp.searchsorted(pad_end, jnp.arange(n_blk, dtype=I32) * te, side='right'),
                        N_EXP - 1).astype(I32)
    n_used = (pad_end[-1] // te).astype(I32).reshape(1)
    tok = jnp.broadcast_to(jnp.arange(N, dtype=I32)[None, :], (TOP_K, N))
    slot_tok = jnp.zeros((n_blk * te,), I32).at[dest.reshape(-1)].set(tok.reshape(-1))
    xs = jnp.take(u2.reshape(N, D), slot_tok, axis=0)
    ys = _experts(xs, blk_e, n_used, w_gu, b_gu, w_dn, b_dn)
    gathered = jnp.take(ys, dest, axis=0).reshape(TOP_K, B, T, D)
    return _combine(x, gathered, gates.T.reshape(B, T, TOP_K), mods, layer, ln_g, ln_b, row_off, row_stride)


def _inproj1_body(x_ref, sc_ref, sh_ref, w_ref, wg_ref, wgt_ref, bg_ref, bgt_ref,
                  qk_ref, v_ref, o_ref, gcol_ref, grow_ref):
    u = x_ref[...] * (1.0 + sc_ref[...]) + sh_ref[...]
    z = jnp.dot(u.astype(BF16), w_ref[...], preferred_element_type=F32)
    qk_ref[...] = z[:, :2 * D_MODEL]
    v_ref[...] = z[:, 2 * D_MODEL:3 * D_MODEL].astype(BF16)
    o_ref[...] = z[:, 3 * D_MODEL:]
    tm = u.shape[0]
    nh = 4 * H_C
    gc = jnp.dot(u, wg_ref[...], precision=HI, preferred_element_type=F32) + bg_ref[...]
    gr = lax.dot_general(wgt_ref[...], u, (((1,), (1,)), ((), ())), precision=HI,
                         preferred_element_type=F32) + bgt_ref[...]
    r_i = lax.broadcasted_iota(I32, (tm, tm), 0)
    c_i = lax.broadcasted_iota(I32, (tm, tm), 1)
    same = (r_i // CHUNK) == (c_i // CHUNK)
    tri_f = jnp.logical_and(same, c_i <= r_i).astype(F32)
    tri_b = jnp.logical_and(same, c_i >= r_i).astype(F32)
    col_i = lax.broadcasted_iota(I32, (tm, nh), 1)
    lf_c = jax.nn.log_sigmoid(gc)
    cum_c = jnp.where(col_i < 2 * H_C,
                      jnp.dot(tri_f, lf_c, precision=HI, preferred_element_type=F32),
                      jnp.dot(tri_b, lf_c, precision=HI, preferred_element_type=F32))
    is_f_col = (col_i // H_C) % 2 == 1
    gcol_ref[...] = jnp.where(is_f_col, cum_c, gc)
    row_i = lax.broadcasted_iota(I32, (nh, tm), 0)
    lf_r = jax.nn.log_sigmoid(gr)
    cum_r = jnp.where(row_i < 2 * H_C,
                      jnp.dot(lf_r, tri_b, precision=HI, preferred_element_type=F32),
                      jnp.dot(lf_r, tri_f, precision=HI, preferred_element_type=F32))
    is_f_row = (row_i // H_C) % 2 == 1
    grow_ref[...] = jnp.where(is_f_row, cum_r, gr)


def _inproj1(x, mods, w_in1, w_gate1, b_gate1, row_off, row_stride):
    B, T, D = x.shape
    tm = min(TOKEN_TILE, T)
    n_j = T // tm
    nh = 4 * H_C
    tok = lambda w: pl.BlockSpec((None, tm, w), lambda b, j: (b, j, 0))
    return pl.pallas_call(
        _inproj1_body,
        grid=(B, n_j),
        in_specs=[tok(D),
                  _mod_spec(1, 1, row_off, row_stride),
                  _mod_spec(1, 0, row_off, row_stride),
                  pl.BlockSpec((D, 4 * D), lambda b, j: (0, 0)),
                  pl.BlockSpec((D, nh), lambda b, j: (0, 0)),
                  pl.BlockSpec((nh, D), lambda b, j: (0, 0)),
                  pl.BlockSpec((1, nh), lambda b, j: (0, 0)),
                  pl.BlockSpec((nh, 1), lambda b, j: (0, 0))],
        out_specs=[tok(2 * D), tok(D), tok(D), tok(nh),
                   pl.BlockSpec((nh, tm), lambda b, j: (0, b * n_j + j))],
        out_shape=[jax.ShapeDtypeStruct((B, T, 2 * D), F32),
                   jax.ShapeDtypeStruct((B, T, D), BF16),
                   jax.ShapeDtypeStruct((B, T, D), F32),
                   jax.ShapeDtypeStruct((B, T, nh), F32),
                   jax.ShapeDtypeStruct((nh, B * T), F32)],
        compiler_params=_cparams(("arbitrary", "arbitrary")),
        name="inproj1",
    )(x, mods, mods, w_in1.astype(BF16), w_gate1, w_gate1.T, b_gate1.reshape(1, nh), b_gate1.reshape(nh, 1))


def _mlstm_body(q_ref, k_ref, v_ref, gcol_ref, grow_ref, cw_ref, cb_ref, c0_ref, n0_ref, m0_ref,
                h_ref, cfin_ref, nfin_ref, mfin_ref, xpad_s, q_s, k_s, c_s, n_s, m_s, *, T):
    hd = pl.program_id(1)
    n_c = T // CHUNK
    L = CHUNK

    rb = min(T, 512)
    pad = 8

    def conv_silu_into(src_ref, w, bias, dst_s, scale):
        xpad_s[0:pad] = jnp.zeros((pad, DH_C), F32)
        xpad_s[T + pad:T + 2 * pad] = jnp.zeros((pad, DH_C), F32)
        xpad_s[pad:T + pad] = src_ref[...]

        def blk(r, _):
            off = pl.multiple_of(r * rb, pad)
            win = xpad_s[pl.ds(off, rb + 2 * pad), :]
            acc = (pltpu.roll(win, 2, 0) * w[0:1] + pltpu.roll(win, 1, 0) * w[1:2] + win * w[2:3]
                   + pltpu.roll(win, rb + 2 * pad - 1, 0) * w[3:4])[pad:rb + pad] + bias
            dst_s[pl.ds(off, rb), :] = (jax.nn.silu(acc) * scale).astype(BF16)
            return 0

        lax.fori_loop(0, T // rb, blk, 0)

    conv_silu_into(q_ref, cw_ref[0], cb_ref[0], q_s, 1.0)
    conv_silu_into(k_ref, cw_ref[1], cb_ref[1], k_s, DH_C ** -0.5)

    t_i = lax.broadcasted_iota(I32, (L, L), 0)
    s_i = lax.broadcasted_iota(I32, (L, L), 1)

    for d in range(2):
        c_s[...] = c0_ref[d]
        n_s[...] = n0_ref[d]
        m_s[...] = m0_ref[d]
        mask = (s_i <= t_i) if d == 0 else (s_i >= t_i)
        last = L - 1 if d == 0 else 0

        def chunk(ci, _, d=d, mask=mask, last=last):
            c = ci if d == 0 else n_c - 1 - ci
            off = pl.multiple_of(c * L, L)
            qc = q_s[pl.ds(off, L), :]
            kc = k_s[pl.ds(off, L), :]
            vc = v_ref[pl.ds(off, L), :]
            gcol = gcol_ref[pl.ds(off, L), :]
            grow = grow_ref[:, pl.ds(off, L)]
            li_c = gcol[:, 2 * d:2 * d + 1]
            b_c = gcol[:, 2 * d + 1:2 * d + 2]
            li_r = grow[2 * d:2 * d + 1, :]
            b_r = grow[2 * d + 1:2 * d + 2, :]
            m_prev = m_s[...]
            log_d = jnp.where(mask, b_c - b_r + li_r, -jnp.inf)
            m_inter = b_c + m_prev
            m_t = jnp.maximum(m_inter, jnp.max(log_d, axis=-1, keepdims=True))
            dmat = jnp.exp(log_d - m_t)
            w_inter = jnp.exp(m_inter - m_t)
            c_b = c_s[...].astype(BF16)
            s = lax.dot_general(qc, kc, (((1,), (1,)), ((), ())), preferred_element_type=F32) * dmat
            inter = lax.dot_general(qc, c_b, (((1,), (1,)), ((), ())), preferred_element_type=F32)
            num = w_inter * inter + jnp.dot(s.astype(BF16), vc, preferred_element_type=F32)
            qn = jnp.sum(qc.astype(F32) * n_s[...], axis=-1, keepdims=True)
            den = w_inter * qn + jnp.sum(s, axis=-1, keepdims=True)
            h = num / jnp.maximum(jnp.abs(den), jnp.exp(-m_t))
            if d == 0:
                h_ref[pl.ds(off, L), :] = h
            else:
                h_ref[pl.ds(off, L), :] = h_ref[pl.ds(off, L), :] + h
            m_new = m_t[last:last + 1, :]
            b_last = b_c[last:last + 1, :]
            w_s = jnp.exp(b_last - b_c + li_c - m_new)
            decay = jnp.exp(b_last + m_prev - m_new)
            wv = (w_s * vc.astype(F32)).astype(BF16)
            kc32 = kc.astype(F32)
            c_s[...] = decay * c_s[...] + lax.dot_general(wv, kc, (((0,), (0,)), ((), ())),
                                                          preferred_element_type=F32)
            n_s[...] = decay * n_s[...] + jnp.sum(w_s * kc32, axis=0, keepdims=True)
            m_s[...] = m_new
            return 0

        lax.fori_loop(0, n_c, chunk, 0)
        cfin_ref[d] = c_s[...]
        nfin_ref[d] = n_s[...]
        mfin_ref[d] = m_s[...]


def _mlstm(qk, v, gcol, grow, conv_w, conv_b, c0, n0, m0):
    B, T, _ = v.shape
    DH = DH_C
    cw = conv_w.reshape(CONV_W, 2, H_C, DH).transpose(2, 1, 0, 3)
    cb = conv_b.reshape(2, H_C, 1, DH).transpose(1, 0, 2, 3)
    st = lambda *tail: pl.BlockSpec((None, 2, None) + tail, lambda b, h: (b, 0, h) + (0,) * len(tail))
    return pl.pallas_call(
        functools.partial(_mlstm_body, T=T),
        grid=(B, H_C),
        in_specs=[pl.BlockSpec((None, T, DH), lambda b, h: (b, 0, h)),
                  pl.BlockSpec((None, T, DH), lambda b, h: (b, 0, H_C + h)),
                  pl.BlockSpec((None, T, DH), lambda b, h: (b, 0, h)),
                  pl.BlockSpec((None, None, T, 4), lambda b, h: (b, h, 0, 0)),
                  pl.BlockSpec((None, None, 4, T), lambda b, h: (b, h, 0, 0)),
                  pl.BlockSpec((None, 2, CONV_W, DH), lambda b, h: (h, 0, 0, 0)),
                  pl.BlockSpec((None, 2, 1, DH), lambda b, h: (h, 0, 0, 0)),
                  st(DH, DH), st(1, DH), st(1, 1)],
        out_specs=[pl.BlockSpec((None, T, DH), lambda b, h: (b, 0, h)),
                   st(DH, DH), st(1, DH), st(1, 1)],
        out_shape=[jax.ShapeDtypeStruct((B, T, D_MODEL), F32),
                   jax.ShapeDtypeStruct((B, 2, H_C, DH, DH), F32),
                   jax.ShapeDtypeStruct((B, 2, H_C, 1, DH), F32),
                   jax.ShapeDtypeStruct((B, 2, H_C, 1, 1), F32)],
        scratch_shapes=[pltpu.VMEM((T + 16, DH), F32), pltpu.VMEM((T, DH), BF16), pltpu.VMEM((T, DH), BF16),
                        pltpu.VMEM((DH, DH), F32), pltpu.VMEM((1, DH), F32), pltpu.VMEM((1, 1), F32)],
        compiler_params=_cparams(("arbitrary", "arbitrary")),
        name="mlstm",
    )(qk, qk, v, gcol, grow, cw, cb, c0, n0, m0)


def _outproj1_body(x_ref, h_ref, o_ref, g1_ref, w_ref, lg_ref, lb_ref, out_ref):
    y = jax.nn.sigmoid(o_ref[...]) * h_ref[...]
    out = jnp.dot(y.astype(BF16), w_ref[...], preferred_element_type=F32)
    v = ALPHA_DN * x_ref[...] + g1_ref[...] * out
    out_ref[...] = _layer_norm(v, lg_ref[...], lb_ref[...])


def _outproj1(x, h, o, mods, w_out1, ln_g, ln_b, row_off, row_stride):
    B, T, D = x.shape
    tm = min(TOKEN_TILE, T)
    tok = pl.BlockSpec((None, tm, D), lambda b, j: (b, j, 0))
    vec = pl.BlockSpec((1, D), lambda b, j: (0, 0))
    return pl.pallas_call(
        _outproj1_body,
        grid=(B, T // tm),
        in_specs=[tok, tok, tok, _mod_spec(1, 2, row_off, row_stride),
                  pl.BlockSpec((D, D), lambda b, j: (0, 0)), vec, vec],
        out_specs=tok,
        out_shape=jax.ShapeDtypeStruct((B, T, D), F32),
        compiler_params=_cparams(("arbitrary", "arbitrary")),
        name="outproj1",
    )(x, h, o, mods, w_out1.astype(BF16), ln_g.reshape(1, D), ln_b.reshape(1, D))


def _to_col_major(x):
    B, T, C = x.shape
    rows = T // GRID_W
    return x.reshape(B, rows, GRID_W, C).transpose(0, 2, 1, 3).reshape(B, T, C)


def _to_row_major(x):
    B, T, C = x.shape
    rows = T // GRID_W
    return x.reshape(B, GRID_W, rows, C).transpose(0, 2, 1, 3).reshape(B, T, C)


def _trunk(x, mods, row_off, row_stride, st, P, s5_mats, grid):
    h_rg, s_re, s_im, m_c, m_n, m_m = st
    B, T, D = x.shape
    xa_tm, ga, ub = _inproj0(x, mods, P['w_in0'][0].astype(BF16), row_off, row_stride)
    h_tm, h_fin = _rglru(xa_tm, B, P['conv_a_w'][0], P['conv_a_b'][0], P['rg_wa'][0], P['rg_ba'][0],
                         P['rg_wi'][0], P['rg_bi'][0], P['rg_lam'][0], h_rg[:, 0].transpose(1, 0, 2))
    y5, s_fin = _s5(ub, s5_mats, _s5_state_to_lanes(s_re[:, 0], s_im[:, 0]))
    x = _outproj0(x, h_tm, ga, y5, ub, mods, P['s5_d'][0], P['glu_w'][0], P['glu_b'][0], P['w_out0'][0],
                  P['ln_g'][0, 0], P['ln_b'][0, 0], row_off, row_stride)
    x = _moe_layer(x, mods, 0, P['w_router'][0], P['b_router'][0], P['w_gu'][0], P['b_gu'][0],
                   P['w_down'][0], P['b_down'][0], P['ln_g'][0, 1], P['ln_b'][0, 1], row_off, row_stride)
    if grid:
        x = _to_col_major(x)
    qk, v, o, gcol, grow = _inproj1(x, mods, P['w_in1'][0], P['w_gate1'][0], P['b_gate1'][0], row_off, row_stride)
    gcol = gcol.reshape(B, T, 2, 2, H_C).transpose(0, 4, 1, 2, 3).reshape(B, H_C, T, 4)
    grow = grow.reshape(2, 2, H_C, B, T).transpose(3, 2, 0, 1, 4).reshape(B, H_C, 4, T)
    h, c_fin, n_fin, m_fin = _mlstm(qk, v, gcol, grow, P['conv_c_w'][0], P['conv_c_b'][0],
                                    m_c[:, 0], m_n[:, 0][:, :, :, None, :], m_m[:, 0][:, :, :, None, None])
    x = _outproj1(x, h, o, mods, P['w_out1'][0], P['ln_g'][1, 0], P['ln_b'][1, 0], row_off, row_stride)
    x = _moe_layer(x, mods, 1, P['w_router'][1], P['b_router'][1], P['w_gu'][1], P['b_gu'][1],
                   P['w_down'][1], P['b_down'][1], P['ln_g'][1, 1], P['ln_b'][1, 1], row_off, row_stride)
    if grid:
        x = _to_row_major(x)
    new_re, new_im = _s5_lanes_to_state(s_fin)
    new = (h_fin.transpose(1, 0, 2)[:, None], new_re[:, None], new_im[:, None],
           c_fin[:, None], n_fin[:, None, :, :, 0, :], m_fin[:, None, :, :, 0, 0])
    return x, new


def _forward(x_prompt, x_sample, c, c_ctx, states, P):
    bp = x_prompt.shape[0]
    bs = x_sample.shape[0]
    rows = 1 + bs
    rpad = -(-rows // 8) * 8
    cv = jnp.concatenate([c_ctx[None, :], c, jnp.zeros((rpad - rows, D_MODEL), F32)], axis=0)
    mods = _modulation(cv, P['w_mod'], P['b_mod'])
    s5_mats = _s5_matrices(P['s5_a_re'][0], P['s5_a_im'][0], P['s5_log_dt'][0], P['s5_b_re'][0], P['s5_b_im'][0],
                           P['s5_c_re'][0], P['s5_c_im'][0])
    zero_state = (jnp.zeros((bp, 1, 2, D_A), F32),
                  jnp.zeros((bp, 1, 2, G_B, P_B), F32),
                  jnp.zeros((bp, 1, 2, G_B, P_B), F32),
                  jnp.zeros((bp, 1, 2, H_C, DH_C, DH_C), F32),
                  jnp.zeros((bp, 1, 2, H_C, DH_C), F32),
                  jnp.zeros((bp, 1, 2, H_C), F32))
    y_prompt, new = _trunk(x_prompt, mods, 0, 0, zero_state, P, s5_mats, False)
    y_sample, _ = _trunk(x_sample, mods, 1, 1, states, P, s5_mats, True)
    return (y_prompt, y_sample) + tuple(new)


def kernel(x_prompt, x_sample, c, c_ctx, state_rglru, state_s5_re, state_s5_im, state_mlstm_C, state_mlstm_n, state_mlstm_m, w_mod, b_mod, ln_g, ln_b, w_in0, conv_a_w, conv_a_b, rg_wa, rg_ba, rg_wi, rg_bi, rg_lam, s5_a_re, s5_a_im, s5_log_dt, s5_b_re, s5_b_im, s5_c_re, s5_c_im, s5_d, glu_w, glu_b, w_out0, w_in1, w_gate1, b_gate1, conv_c_w, conv_c_b, w_out1, w_router, b_router, w_gu, b_gu, w_down, b_down):
    P = dict(w_mod=w_mod, b_mod=b_mod, ln_g=ln_g, ln_b=ln_b, w_in0=w_in0, conv_a_w=conv_a_w,
             conv_a_b=conv_a_b, rg_wa=rg_wa, rg_ba=rg_ba, rg_wi=rg_wi, rg_bi=rg_bi, rg_lam=rg_lam,
             s5_a_re=s5_a_re, s5_a_im=s5_a_im, s5_log_dt=s5_log_dt, s5_b_re=s5_b_re, s5_b_im=s5_b_im,
             s5_c_re=s5_c_re, s5_c_im=s5_c_im, s5_d=s5_d, glu_w=glu_w, glu_b=glu_b, w_out0=w_out0,
             w_in1=w_in1, w_gate1=w_gate1, b_gate1=b_gate1, conv_c_w=conv_c_w, conv_c_b=conv_c_b,
             w_out1=w_out1, w_router=w_router, b_router=b_router, w_gu=w_gu, b_gu=b_gu,
             w_down=w_down, b_down=b_down)
    states = (state_rglru, state_s5_re, state_s5_im, state_mlstm_C, state_mlstm_n, state_mlstm_m)
    return _forward(x_prompt, x_sample, c, c_ctx, states, P)
```

```python
import functools
import math

import jax
import jax.numpy as jnp
from jax import lax
from jax.experimental import pallas as pl
from jax.experimental.pallas import tpu as pltpu

F32 = jnp.float32
BF16 = jnp.bfloat16
I32 = jnp.int32
HI = lax.Precision.HIGHEST

D_MODEL = 1024
DEPTH = 2
GRID_W = 64
D_A = 512
NB_A = 8
BS_A = D_A // NB_A
CONV_W = 4
LRU_C = 8.0
D_B = 512
S5_GROUP = 16
G_B = D_B // S5_GROUP
P_B = 64
H_C = 4
DH_C = D_MODEL // H_C
CHUNK = 128
N_EXP = 32
TOP_K = 4
D_FF = D_MODEL
SWIGLU_LIMIT = 7.0
SWIGLU_ALPHA = 1.702
ALPHA_DN = (2 * DEPTH) ** 0.25
LN_EPS = 1e-5

S5_L = 16
S5_W = S5_L * S5_GROUP
TOKEN_TILE = 256
EXPERT_TILE = 256
VMEM_LIMIT = 56 * 1024 * 1024


def _cparams(sem, vmem=VMEM_LIMIT):
    return pltpu.CompilerParams(dimension_semantics=sem, vmem_limit_bytes=vmem)


def _layer_norm(v, g, b):
    mu = jnp.mean(v, axis=-1, keepdims=True)
    c = v - mu
    var = jnp.mean(c * c, axis=-1, keepdims=True)
    return c * lax.rsqrt(var + LN_EPS) * g + b


def _mod_spec(layer, which, row_off, row_stride):
    return pl.BlockSpec((None, None, None, 1, D_MODEL),
                        lambda b, j: (layer, which, row_off + b * row_stride, 0, 0))


def _mod_body(c_ref, w_ref, b_ref, o_ref):
    s = jax.nn.silu(c_ref[...])
    o_ref[...] = jnp.dot(s, w_ref[...], precision=HI, preferred_element_type=F32) + b_ref[...]


def _modulation(cv, w_mod, b_mod):
    R, D = cv.shape
    L, _, N6 = w_mod.shape
    tn = N6 // 6
    out = pl.pallas_call(
        _mod_body,
        grid=(L, N6 // tn),
        in_specs=[pl.BlockSpec((R, D), lambda l, j: (0, 0)),
                  pl.BlockSpec((None, D, tn), lambda l, j: (l, 0, j)),
                  pl.BlockSpec((None, 1, tn), lambda l, j: (l, 0, j))],
        out_specs=pl.BlockSpec((None, R, tn), lambda l, j: (l, 0, j)),
        out_shape=jax.ShapeDtypeStruct((L, R, N6), F32),
        compiler_params=_cparams(("arbitrary", "arbitrary")),
        name="modulation",
    )(cv, w_mod, b_mod.reshape(L, 1, N6))
    return out.reshape(L, R, 6, D).transpose(0, 2, 1, 3)[:, :, :, None, :]


def _inproj0_body(x_ref, sc_ref, sh_ref, w_ref, xa_ref, ga_ref, ub_ref):
    u = x_ref[...] * (1.0 + sc_ref[...]) + sh_ref[...]
    z = jnp.dot(u.astype(BF16), w_ref[...], preferred_element_type=F32)
    xa_ref[...] = z[:, :D_A]
    ga_ref[...] = z[:, D_A:2 * D_A]
    ub_ref[...] = z[:, 2 * D_A:]


def _inproj0(x, mods, w_in0, row_off, row_stride):
    B, T, D = x.shape
    tm = min(TOKEN_TILE, T)
    nz = w_in0.shape[1]
    return pl.pallas_call(
        _inproj0_body,
        grid=(B, T // tm),
        in_specs=[pl.BlockSpec((None, tm, D), lambda b, j: (b, j, 0)),
                  _mod_spec(0, 1, row_off, row_stride),
                  _mod_spec(0, 0, row_off, row_stride),
                  pl.BlockSpec((D, nz), lambda b, j: (0, 0))],
        out_specs=[pl.BlockSpec((tm, D_A), lambda b, j: (j, b)),
                   pl.BlockSpec((None, tm, D_A), lambda b, j: (b, j, 0)),
                   pl.BlockSpec((None, tm, D_B), lambda b, j: (b, j, 0))],
        out_shape=[jax.ShapeDtypeStruct((T, B * D_A), F32),
                   jax.ShapeDtypeStruct((B, T, D_A), F32),
                   jax.ShapeDtypeStruct((B, T, D_B), F32)],
        compiler_params=_cparams(("arbitrary", "arbitrary")),
        name="inproj0",
    )(x, mods, mods, w_in0)


def _rglru_body(cur_ref, prev_ref, next_ref, cw_ref, cb_ref, wa_ref, wi_ref, ba_ref, bi_ref, lam_ref, h0_ref,
                h_ref, hfin_ref, ext_s, a_s, b_s, carry_s, *, tt, n_t):
    d = pl.program_id(0)
    j = pl.program_id(1)
    jj = j + d * (n_t - 1 - 2 * j)
    nb = cur_ref.shape[1]
    half = D_A // 2
    ext_s[0:2] = jnp.where(jj == 0, 0.0, prev_ref[...])
    ext_s[2:tt + 2] = cur_ref[...]
    ext_s[tt + 2:tt + 3] = jnp.where(jj == n_t - 1, 0.0, next_ref[...])
    xc = ext_s[0:tt] * cw_ref[0]
    for k in range(1, CONV_W):
        xc = xc + ext_s[k:k + tt] * cw_ref[k]
    xc = xc + cb_ref[...]
    x2 = xc.reshape(tt * nb, D_A)
    xb = x2.astype(BF16)

    def gate(w_ref, bias_ref):
        lo = jnp.dot(xb[:, :half], w_ref[0], preferred_element_type=F32)
        hi = jnp.dot(xb[:, half:], w_ref[1], preferred_element_type=F32)
        return jax.nn.sigmoid(jnp.concatenate([lo, hi], axis=1) + bias_ref[...])

    r = gate(wa_ref, ba_ref)
    i = gate(wi_ref, bi_ref)
    log_a = LRU_C * r * jax.nn.log_sigmoid(lam_ref[...])
    a = jnp.exp(log_a)
    one_minus_a2 = -jnp.tanh(log_a) * (a * a + 1.0)
    a_s[...] = a.reshape(tt, nb, D_A)
    b_s[...] = (jnp.sqrt(one_minus_a2) * (i * x2)).reshape(tt, nb, D_A)

    @pl.when(j == 0)
    def _():
        carry_s[...] = h0_ref[...]

    def step(t, h):
        h = a_s[t] * h + b_s[t]
        h_ref[t] = h
        return h

    @pl.when(d == 0)
    def _():
        carry_s[...] = lax.fori_loop(0, tt, step, carry_s[...], unroll=8)

    @pl.when(d == 1)
    def _():
        carry_s[...] = lax.fori_loop(0, tt, lambda t, h: step(tt - 1 - t, h), carry_s[...], unroll=8)

    hfin_ref[...] = carry_s[...]


def _block_diag_halves(w):
    nd = w.shape[0]
    per = NB_A // 2
    w = w.reshape(nd, 2, per, BS_A, BS_A)
    eye = jnp.eye(per, dtype=w.dtype)
    out = jnp.einsum('dhncz,nm->dhncmz', w, eye)
    return out.reshape(nd, 2, per * BS_A, per * BS_A)


def _rglru(xa_tm, B, conv_w, conv_b, wa, ba, wi, bi, lam, h0):
    T = xa_tm.shape[0]
    tt = min(T, 1024 // B)
    n_t = T // tt
    x3 = xa_tm.reshape(T, B, D_A)

    def cur_map(d, j):
        return (j + d * (n_t - 1 - 2 * j), 0, 0)

    def prev_map(d, j):
        jj = j + d * (n_t - 1 - 2 * j)
        return (jnp.maximum(jj * (tt // 2) - 1, 0), 0, 0)

    def next_map(d, j):
        jj = j + d * (n_t - 1 - 2 * j)
        return (jnp.minimum((jj + 1) * tt, T - 1), 0, 0)

    dir_spec3 = pl.BlockSpec((None, 1, D_A), lambda d, j: (d, 0, 0))
    h, hfin = pl.pallas_call(
        functools.partial(_rglru_body, tt=tt, n_t=n_t),
        grid=(2, n_t),
        in_specs=[pl.BlockSpec((tt, B, D_A), cur_map),
                  pl.BlockSpec((2, B, D_A), prev_map),
                  pl.BlockSpec((1, B, D_A), next_map),
                  pl.BlockSpec((CONV_W, 1, D_A), lambda d, j: (0, 0, 0)),
                  pl.BlockSpec((1, 1, D_A), lambda d, j: (0, 0, 0)),
                  pl.BlockSpec((None, 2, D_A // 2, D_A // 2), lambda d, j: (d, 0, 0, 0)),
                  pl.BlockSpec((None, 2, D_A // 2, D_A // 2), lambda d, j: (d, 0, 0, 0)),
                  dir_spec3, dir_spec3, dir_spec3,
                  pl.BlockSpec((None, B, D_A), lambda d, j: (d, 0, 0))],
        out_specs=[pl.BlockSpec((None, tt, B, D_A), lambda d, j: (d,) + cur_map(d, j)),
                   pl.BlockSpec((None, B, D_A), lambda d, j: (d, 0, 0))],
        out_shape=[jax.ShapeDtypeStruct((2, T, B, D_A), F32),
                   jax.ShapeDtypeStruct((2, B, D_A), F32)],
        scratch_shapes=[pltpu.VMEM((tt + 3, B, D_A), F32),
                        pltpu.VMEM((tt, B, D_A), F32),
                        pltpu.VMEM((tt, B, D_A), F32),
                        pltpu.VMEM((B, D_A), F32)],
        compiler_params=_cparams(("arbitrary", "arbitrary")),
        name="rglru",
    )(x3, x3, x3, conv_w.reshape(CONV_W, 1, D_A), conv_b.reshape(1, 1, D_A),
      _block_diag_halves(wa).astype(BF16), _block_diag_halves(wi).astype(BF16),
      ba.reshape(2, 1, D_A), bi.reshape(2, 1, D_A), lam.reshape(2, 1, D_A), h0)
    return h.reshape(2, T, B * D_A), hfin


def _s5_matrices(a_re, a_im, log_dt, b_re, b_im, c_re, c_im):
    L = S5_L
    lam = lax.complex(a_re.astype(F32), a_im.astype(F32))
    dt = jnp.exp(log_dt.astype(F32))[..., None]
    ldt = lam * dt
    a_bar = jnp.exp(ldt)
    b_bar = ((a_bar - 1.0) / lam)[..., None] * lax.complex(b_re.astype(F32), b_im.astype(F32))
    cc = lax.complex(c_re.astype(F32), c_im.astype(F32))
    ks = jnp.arange(L + 1, dtype=F32)
    pw = jnp.exp(ldt[:, :, None, :] * ks[None, None, :, None].astype(jnp.complex64))
    kern = jnp.real(jnp.einsum('dgjp,dgkp,dgpi->dgkji', cc, pw[:, :, :L], b_bar, precision=HI))
    s_idx = jnp.arange(L)[:, None]
    t_idx = jnp.arange(L)[None, :]
    lag_f = jnp.clip(t_idx - s_idx, 0, L - 1)
    lag_b = jnp.clip(s_idx - t_idx, 0, L - 1)
    m_f = jnp.where((t_idx >= s_idx)[None, :, :, None, None], kern[0][:, lag_f], 0.0)
    m_b = jnp.where((s_idx >= t_idx)[None, :, :, None, None], kern[1][:, lag_b], 0.0)
    m = (m_f + m_b).transpose(0, 1, 4, 2, 3).reshape(G_B, S5_W, S5_W)
    g_f = pw[0][:, ::-1][:, 1:, :, None] * b_bar[0][:, None]
    g_b = pw[1][:, :L, :, None] * b_bar[1][:, None]

    def g_cols(x):
        return x.transpose(0, 1, 3, 2).reshape(G_B, S5_W, P_B)

    gs = jnp.concatenate([g_cols(jnp.real(g_f)), g_cols(jnp.real(g_b)),
                          g_cols(jnp.imag(g_f)), g_cols(jnp.imag(g_b))], axis=-1)
    e_f = cc[0][:, None] * pw[0][:, 1:, None, :]
    e_b = cc[1][:, None] * pw[1][:, ::-1][:, :L, None, :]

    def e_rows(x):
        return x.transpose(0, 3, 1, 2).reshape(G_B, P_B, S5_W)

    e = jnp.concatenate([e_rows(jnp.real(e_f)), e_rows(jnp.real(e_b)),
                         -e_rows(jnp.imag(e_f)), -e_rows(jnp.imag(e_b))], axis=1)
    a_l = pw[:, :, L]
    al_re = jnp.concatenate([jnp.real(a_l[0]), jnp.real(a_l[1])], axis=-1)
    al_im = jnp.concatenate([jnp.imag(a_l[0]), jnp.imag(a_l[1])], axis=-1)
    mg = jnp.concatenate([m, gs], axis=-1).astype(BF16)
    return mg, e.astype(BF16), al_re[:, None, :], al_im[:, None, :]


def _s5_local_body(u_ref, mg_ref, y_ref, f_ref):
    r = jnp.dot(u_ref[...], mg_ref[...], preferred_element_type=F32)
    y_ref[...] = r[:, :S5_W]
    f_ref[...] = r[:, S5_W:]


def _s5_scan_body(f_ref, are_ref, aim_ref, s0_ref, sin_ref, sfin_ref, *, n_c):
    hw = 2 * P_B
    a_re = are_ref[...]
    a_im = aim_ref[...]
    lane = lax.broadcasted_iota(I32, s0_ref.shape[:2] + (hw,), 2)
    is_fwd = lane < P_B

    def advance(c, s_re, s_im):
        f = f_ref[:, c]
        n_re = a_re * s_re - a_im * s_im + f[..., :hw]
        n_im = a_re * s_im + a_im * s_re + f[..., hw:]
        return n_re, n_im

    def fwd(c, carry):
        s_re, s_im = carry
        sin_ref[:, c, :, 0:hw] = s_re
        sin_ref[:, c, :, hw:2 * hw] = s_im
        return advance(c, s_re, s_im)

    s0 = s0_ref[...]
    init = (s0[..., :hw], s0[..., hw:])
    f_re, f_im = lax.fori_loop(0, n_c, fwd, init)

    def bwd(k, carry):
        c = n_c - 1 - k
        s_re, s_im = carry
        cur = sin_ref[:, c]
        sin_ref[:, c, :, 0:hw] = jnp.where(is_fwd, cur[..., :hw], s_re)
        sin_ref[:, c, :, hw:2 * hw] = jnp.where(is_fwd, cur[..., hw:], s_im)
        return advance(c, s_re, s_im)

    b_re, b_im = lax.fori_loop(0, n_c, bwd, init)
    sfin_ref[:, :, 0:hw] = jnp.where(is_fwd, f_re, b_re)
    sfin_ref[:, :, hw:2 * hw] = jnp.where(is_fwd, f_im, b_im)


def _s5_out_body(y_ref, sin_ref, e_ref, o_ref):
    o_ref[...] = y_ref[...] + jnp.dot(sin_ref[...].astype(BF16), e_ref[...], preferred_element_type=F32)


def _s5(ub, mats, s0):
    mg, e, al_re, al_im = mats
    B, T, _ = ub.shape
    n_c = T // S5_L
    R = n_c * B
    u = ub.reshape(B, n_c, S5_L, G_B, S5_GROUP).transpose(3, 1, 0, 2, 4).reshape(G_B, R, S5_W).astype(BF16)
    tr = min(R, 512)
    y_loc, f_loc = pl.pallas_call(
        _s5_local_body,
        grid=(G_B, R // tr),
        in_specs=[pl.BlockSpec((None, tr, S5_W), lambda g, i: (g, i, 0)),
                  pl.BlockSpec((None, S5_W, 2 * S5_W), lambda g, i: (g, 0, 0))],
        out_specs=[pl.BlockSpec((None, tr, S5_W), lambda g, i: (g, i, 0)),
                   pl.BlockSpec((None, tr, S5_W), lambda g, i: (g, i, 0))],
        out_shape=[jax.ShapeDtypeStruct((G_B, R, S5_W), F32)] * 2,
        compiler_params=_cparams(("arbitrary", "arbitrary")),
        name="s5_local",
    )(u, mg)
    gb = 2
    sin, sfin = pl.pallas_call(
        functools.partial(_s5_scan_body, n_c=n_c),
        grid=(G_B // gb,),
        in_specs=[pl.BlockSpec((gb, n_c, B, S5_W), lambda g: (g, 0, 0, 0)),
                  pl.BlockSpec((gb, 1, 2 * P_B), lambda g: (g, 0, 0)),
                  pl.BlockSpec((gb, 1, 2 * P_B), lambda g: (g, 0, 0)),
                  pl.BlockSpec((gb, B, S5_W), lambda g: (g, 0, 0))],
        out_specs=[pl.BlockSpec((gb, n_c, B, S5_W), lambda g: (g, 0, 0, 0)),
                   pl.BlockSpec((gb, B, S5_W), lambda g: (g, 0, 0))],
        out_shape=[jax.ShapeDtypeStruct((G_B, n_c, B, S5_W), F32),
                   jax.ShapeDtypeStruct((G_B, B, S5_W), F32)],
        compiler_params=_cparams(("arbitrary",)),
        name="s5_scan",
    )(f_loc.reshape(G_B, n_c, B, S5_W), al_re, al_im, s0)
    y = pl.pallas_call(
        _s5_out_body,
        grid=(G_B, R // tr),
        in_specs=[pl.BlockSpec((None, tr, S5_W), lambda g, i: (g, i, 0)),
                  pl.BlockSpec((None, tr, S5_W), lambda g, i: (g, i, 0)),
                  pl.BlockSpec((None, S5_W, S5_W), lambda g, i: (g, 0, 0))],
        out_specs=pl.BlockSpec((None, tr, S5_W), lambda g, i: (g, i, 0)),
        out_shape=jax.ShapeDtypeStruct((G_B, R, S5_W), F32),
        compiler_params=_cparams(("arbitrary", "arbitrary")),
        name="s5_out",
    )(y_loc, sin.reshape(G_B, R, S5_W), e)
    y = y.reshape(G_B, n_c, B, S5_L, S5_GROUP).transpose(2, 1, 3, 0, 4).reshape(B, T, D_B)
    return y, sfin


def _s5_state_to_lanes(s_re, s_im):
    parts = [s_re[:, 0], s_re[:, 1], s_im[:, 0], s_im[:, 1]]
    return jnp.concatenate(parts, axis=-1).transpose(1, 0, 2)


def _s5_lanes_to_state(s):
    s = s.transpose(1, 0, 2).reshape(s.shape[1], G_B, 4, P_B)
    return jnp.stack([s[:, :, 0], s[:, :, 1]], axis=1), jnp.stack([s[:, :, 2], s[:, :, 3]], axis=1)


def _outproj0_body(x_ref, hf_ref, hb_ref, ga_ref, y_ref, ub_ref, g1_ref, d_ref, gw_ref, gb_ref, wo_ref,
                   lg_ref, lb_ref, o_ref):
    ya = (hf_ref[...] + hb_ref[...]) * jax.nn.gelu(ga_ref[...])
    yb = y_ref[...] + d_ref[...] * ub_ref[...]
    g = jax.nn.gelu(yb)
    gate = jax.nn.sigmoid(jnp.dot(g.astype(BF16), gw_ref[...], preferred_element_type=F32) + gb_ref[...])
    yb = g * gate
    out = (jnp.dot(ya.astype(BF16), wo_ref[:D_A], preferred_element_type=F32)
           + jnp.dot(yb.astype(BF16), wo_ref[D_A:], preferred_element_type=F32))
    v = ALPHA_DN * x_ref[...] + g1_ref[...] * out
    o_ref[...] = _layer_norm(v, lg_ref[...], lb_ref[...])


def _outproj0(x, h_tm, ga, y5, ub, mods, s5_d, glu_w, glu_b, w_out0, ln_g, ln_b, row_off, row_stride):
    B, T, D = x.shape
    tm = min(TOKEN_TILE, T)
    tok = lambda w: pl.BlockSpec((None, tm, w), lambda b, j: (b, j, 0))
    vec = lambda w: pl.BlockSpec((1, w), lambda b, j: (0, 0))
    return pl.pallas_call(
        _outproj0_body,
        grid=(B, T // tm),
        in_specs=[tok(D),
                  pl.BlockSpec((None, tm, D_A), lambda b, j: (0, j, b)),
                  pl.BlockSpec((None, tm, D_A), lambda b, j: (1, j, b)),
                  tok(D_A), tok(D_B), tok(D_B),
                  _mod_spec(0, 2, row_off, row_stride),
                  vec(D_B),
                  pl.BlockSpec((D_B, D_B), lambda b, j: (0, 0)),
                  vec(D_B),
                  pl.BlockSpec((D, D), lambda b, j: (0, 0)),
                  vec(D), vec(D)],
        out_specs=tok(D),
        out_shape=jax.ShapeDtypeStruct((B, T, D), F32),
        compiler_params=_cparams(("arbitrary", "arbitrary")),
        name="outproj0",
    )(x, h_tm, h_tm, ga, y5, ub, mods, s5_d.reshape(1, D_B), glu_w.astype(BF16), glu_b.reshape(1, D_B),
      w_out0.astype(BF16), ln_g.reshape(1, D), ln_b.reshape(1, D))


def _router_body(x_ref, sc_ref, sh_ref, wr_ref, br_ref, u_ref, topi_ref, gate_ref, rank_ref, cnt_ref, cnt_s):
    first = jnp.logical_and(pl.program_id(0) == 0, pl.program_id(1) == 0)

    @pl.when(first)
    def _():
        cnt_s[...] = jnp.zeros_like(cnt_s)

    u = x_ref[...] * (1.0 + sc_ref[...]) + sh_ref[...]
    u_ref[...] = u.astype(BF16)
    tm = u.shape[0]
    logits = lax.dot_general(wr_ref[...], u, (((1,), (1,)), ((), ())), precision=HI,
                             preferred_element_type=F32) + br_ref[...]
    e_iota = lax.broadcasted_iota(I32, logits.shape, 0)
    work = logits
    vals, idxs, hots = [], [], []
    for _ in range(TOP_K):
        m = jnp.max(work, axis=0, keepdims=True)
        idx = jnp.min(jnp.where(work == m, e_iota, N_EXP), axis=0, keepdims=True)
        hot = e_iota == idx
        vals.append(m)
        idxs.append(idx)
        hots.append(hot)
        work = jnp.where(hot, -jnp.inf, work)
    ex = [jnp.exp(v - vals[0]) for v in vals]
    den = ex[0] + ex[1] + ex[2] + ex[3]
    gate_ref[...] = jnp.concatenate([e / den for e in ex], axis=0)
    topi_ref[...] = jnp.concatenate(idxs, axis=0)
    hot_sum = jnp.zeros(logits.shape, F32)
    for hot in hots:
        hot_sum = hot_sum + hot.astype(F32)
    before = lax.broadcasted_iota(I32, (tm, tm), 0) < lax.broadcasted_iota(I32, (tm, tm), 1)
    excl = jnp.dot(hot_sum.astype(BF16), before.astype(BF16), preferred_element_type=F32)
    base = excl + cnt_s[...]
    ranks = [jnp.sum(jnp.where(hot, base, 0.0), axis=0, keepdims=True) for hot in hots]
    rank_ref[...] = jnp.concatenate(ranks, axis=0).astype(I32)
    cnt_s[...] = cnt_s[...] + jnp.sum(hot_sum, axis=1, keepdims=True)
    cnt_ref[...] = cnt_s[...]


def _router(x, mods, layer, w_router, b_router, row_off, row_stride):
    B, T, D = x.shape
    tm = min(TOKEN_TILE, T)
    n_j = T // tm
    N = B * T
    lane_spec = pl.BlockSpec((TOP_K, tm), lambda b, j: (0, b * n_j + j))
    return pl.pallas_call(
        _router_body,
        grid=(B, n_j),
        in_specs=[pl.BlockSpec((None, tm, D), lambda b, j: (b, j, 0)),
                  _mod_spec(layer, 4, row_off, row_stride),
                  _mod_spec(layer, 3, row_off, row_stride),
                  pl.BlockSpec((N_EXP, D), lambda b, j: (0, 0)),
                  pl.BlockSpec((N_EXP, 1), lambda b, j: (0, 0))],
        out_specs=[pl.BlockSpec((None, tm, D), lambda b, j: (b, j, 0)),
                   lane_spec, lane_spec, lane_spec,
                   pl.BlockSpec((N_EXP, 1), lambda b, j: (0, 0))],
        out_shape=[jax.ShapeDtypeStruct((B, T, D), BF16),
                   jax.ShapeDtypeStruct((TOP_K, N), I32),
                   jax.ShapeDtypeStruct((TOP_K, N), F32),
                   jax.ShapeDtypeStruct((TOP_K, N), I32),
                   jax.ShapeDtypeStruct((N_EXP, 1), F32)],
        scratch_shapes=[pltpu.VMEM((N_EXP, 1), F32)],
        compiler_params=_cparams(("arbitrary", "arbitrary")),
        name="router",
    )(x, mods, mods, w_router.T, b_router.reshape(N_EXP, 1))


def _expert_body(blk_e_ref, n_used_ref, xs_ref, wgu_ref, bgu_ref, wdn_ref, bdn_ref, o_ref, wgu_s, wdn_s):
    i = pl.program_id(0)
    prev = blk_e_ref[jnp.maximum(i - 1, 0)]
    changed = jnp.logical_or(i == 0, blk_e_ref[i] != prev)

    @pl.when(changed)
    def _():
        wgu_s[...] = wgu_ref[...].astype(BF16)
        wdn_s[...] = wdn_ref[...].astype(BF16)

    @pl.when(i < n_used_ref[0])
    def _():
        h = jnp.dot(xs_ref[...], wgu_s[...], preferred_element_type=F32) + bgu_ref[...]
        gt = jnp.minimum(h[:, :D_FF], SWIGLU_LIMIT)
        up = jnp.clip(h[:, D_FF:], -SWIGLU_LIMIT, SWIGLU_LIMIT)
        act = (up + 1.0) * gt * jax.nn.sigmoid(SWIGLU_ALPHA * gt)
        o_ref[...] = jnp.dot(act.astype(BF16), wdn_s[...], preferred_element_type=F32) + bdn_ref[...]

    @pl.when(i >= n_used_ref[0])
    def _():
        o_ref[...] = jnp.zeros_like(o_ref)


def _experts(xs, blk_e, n_used, layer, w_gu, b_gu, w_dn, b_dn):
    slots, D = xs.shape
    te = EXPERT_TILE
    n_blk = slots // te
    grid_spec = pltpu.PrefetchScalarGridSpec(
        num_scalar_prefetch=2,
        grid=(n_blk,),
        in_specs=[pl.BlockSpec((te, D), lambda i, be, nu: (i, 0)),
                  pl.BlockSpec((None, None, D, 2 * D_FF), lambda i, be, nu: (layer, be[i], 0, 0)),
                  pl.BlockSpec((None, None, 1, 2 * D_FF), lambda i, be, nu: (layer, be[i], 0, 0)),
                  pl.BlockSpec((None, None, D_FF, D), lambda i, be, nu: (layer, be[i], 0, 0)),
                  pl.BlockSpec((None, None, 1, D), lambda i, be, nu: (layer, be[i], 0, 0))],
        out_specs=pl.BlockSpec((te, D), lambda i, be, nu: (i, 0)),
        scratch_shapes=[pltpu.VMEM((D, 2 * D_FF), BF16), pltpu.VMEM((D_FF, D), BF16)],
    )
    return pl.pallas_call(
        _expert_body,
        grid_spec=grid_spec,
        out_shape=jax.ShapeDtypeStruct((slots, D), F32),
        compiler_params=_cparams(("arbitrary",)),
        name="experts",
    )(blk_e, n_used, xs, w_gu, b_gu.reshape(DEPTH, N_EXP, 1, 2 * D_FF), w_dn, b_dn.reshape(DEPTH, N_EXP, 1, D))


def _combine_body(x_ref, y_ref, gate_ref, g2_ref, lg_ref, lb_ref, o_ref):
    gates = gate_ref[...]
    acc = y_ref[0] * gates[:, 0:1]
    for k in range(1, TOP_K):
        acc = acc + y_ref[k] * gates[:, k:k + 1]
    v = ALPHA_DN * x_ref[...] + g2_ref[...] * acc
    o_ref[...] = _layer_norm(v, lg_ref[...], lb_ref[...])


def _combine(x, gathered, gates_nk, mods, layer, ln_g, ln_b, row_off, row_stride):
    B, T, D = x.shape
    tm = min(TOKEN_TILE, T)
    vec = pl.BlockSpec((1, D), lambda b, j: (0, 0))
    return pl.pallas_call(
        _combine_body,
        grid=(B, T // tm),
        in_specs=[pl.BlockSpec((None, tm, D), lambda b, j: (b, j, 0)),
                  pl.BlockSpec((TOP_K, None, tm, D), lambda b, j: (0, b, j, 0)),
                  pl.BlockSpec((None, tm, TOP_K), lambda b, j: (b, j, 0)),
                  _mod_spec(layer, 5, row_off, row_stride),
                  vec, vec],
        out_specs=pl.BlockSpec((None, tm, D), lambda b, j: (b, j, 0)),
        out_shape=jax.ShapeDtypeStruct((B, T, D), F32),
        compiler_params=_cparams(("arbitrary", "arbitrary")),
        name="moe_combine",
    )(x, gathered, gates_nk, mods, ln_g.reshape(1, D), ln_b.reshape(1, D))


def _moe_layer(x, mods, layer, w_router, b_router, w_gu, b_gu, w_dn, b_dn, ln_g, ln_b, row_off, row_stride):
    B, T, D = x.shape
    N = B * T
    te = EXPERT_TILE
    u2, topi, gates, rank, cnt = _router(x, mods, layer, w_router, b_router, row_off, row_stride)
    counts = cnt[:, 0].astype(I32)
    padded = (counts + te - 1) // te * te
    pad_end = jnp.cumsum(padded)
    pad_start = pad_end - padded
    e_ids = jnp.arange(N_EXP, dtype=I32)
    start_of = jnp.sum(jnp.where(topi[None] == e_ids[:, None, None], pad_start[:, None, None], 0), axis=0)
    dest = start_of + rank
    n_blk = (N * TOP_K) // te + N_EXP
    blk_pos = jnp.arange(n_blk, dtype=I32) * te
    blk_e = jnp.minimum(jnp.sum((pad_end[None, :] <= blk_pos[:, None]).astype(I32), axis=1), N_EXP - 1)
    n_used = (pad_end[-1] // te).astype(I32).reshape(1)
    tok = jnp.broadcast_to(jnp.arange(N, dtype=I32)[None, :], (TOP_K, N))
    slot_tok = jnp.zeros((n_blk * te,), I32).at[dest.reshape(-1)].set(
        tok.reshape(-1), mode='promise_in_bounds', unique_indices=True)
    xs = u2.reshape(N, D).at[slot_tok].get(mode='promise_in_bounds')
    ys = _experts(xs, blk_e, n_used, layer, w_gu, b_gu, w_dn, b_dn)
    gathered = ys.at[dest].get(mode='promise_in_bounds').reshape(TOP_K, B, T, D)
    return _combine(x, gathered, gates.T.reshape(B, T, TOP_K), mods, layer, ln_g, ln_b, row_off, row_stride)


def _inproj1_body(x_ref, sc_ref, sh_ref, w_ref, wg_ref, wgt_ref, bg_ref, bgt_ref,
                  qk_ref, v_ref, o_ref, gcol_ref, grow_ref):
    u = x_ref[...] * (1.0 + sc_ref[...]) + sh_ref[...]
    z = jnp.dot(u.astype(BF16), w_ref[...], preferred_element_type=F32)
    qk_ref[...] = z[:, :2 * D_MODEL]
    v_ref[...] = z[:, 2 * D_MODEL:3 * D_MODEL].astype(BF16)
    o_ref[...] = z[:, 3 * D_MODEL:]
    tm = u.shape[0]
    nh = 4 * H_C
    gc = jnp.dot(u, wg_ref[...], precision=HI, preferred_element_type=F32) + bg_ref[...]
    gr = lax.dot_general(wgt_ref[...], u, (((1,), (1,)), ((), ())), precision=HI,
                         preferred_element_type=F32) + bgt_ref[...]
    r_i = lax.broadcasted_iota(I32, (tm, tm), 0)
    c_i = lax.broadcasted_iota(I32, (tm, tm), 1)
    same = (r_i // CHUNK) == (c_i // CHUNK)
    tri_f = jnp.logical_and(same, c_i <= r_i).astype(F32)
    tri_b = jnp.logical_and(same, c_i >= r_i).astype(F32)
    col_i = lax.broadcasted_iota(I32, (tm, nh), 1)
    lf_c = jax.nn.log_sigmoid(gc)
    cum_c = jnp.where(col_i < 2 * H_C,
                      jnp.dot(tri_f, lf_c, precision=HI, preferred_element_type=F32),
                      jnp.dot(tri_b, lf_c, precision=HI, preferred_element_type=F32))
    is_f_col = (col_i // H_C) % 2 == 1
    gcol_ref[...] = jnp.where(is_f_col, cum_c, gc)
    row_i = lax.broadcasted_iota(I32, (nh, tm), 0)
    lf_r = jax.nn.log_sigmoid(gr)
    cum_r = jnp.where(row_i < 2 * H_C,
                      jnp.dot(lf_r, tri_b, precision=HI, preferred_element_type=F32),
                      jnp.dot(lf_r, tri_f, precision=HI, preferred_element_type=F32))
    is_f_row = (row_i // H_C) % 2 == 1
    grow_ref[...] = jnp.where(is_f_row, cum_r, gr)


def _inproj1(x, mods, w_in1, w_gate1, b_gate1, row_off, row_stride):
    B, T, D = x.shape
    tm = min(TOKEN_TILE, T)
    n_j = T // tm
    nh = 4 * H_C
    tok = lambda w: pl.BlockSpec((None, tm, w), lambda b, j: (b, j, 0))
    return pl.pallas_call(
        _inproj1_body,
        grid=(B, n_j),
        in_specs=[tok(D),
                  _mod_spec(1, 1, row_off, row_stride),
                  _mod_spec(1, 0, row_off, row_stride),
                  pl.BlockSpec((D, 4 * D), lambda b, j: (0, 0)),
                  pl.BlockSpec((D, nh), lambda b, j: (0, 0)),
                  pl.BlockSpec((nh, D), lambda b, j: (0, 0)),
                  pl.BlockSpec((1, nh), lambda b, j: (0, 0)),
                  pl.BlockSpec((nh, 1), lambda b, j: (0, 0))],
        out_specs=[tok(2 * D), tok(D), tok(D), tok(nh),
                   pl.BlockSpec((nh, tm), lambda b, j: (0, b * n_j + j))],
        out_shape=[jax.ShapeDtypeStruct((B, T, 2 * D), F32),
                   jax.ShapeDtypeStruct((B, T, D), BF16),
                   jax.ShapeDtypeStruct((B, T, D), F32),
                   jax.ShapeDtypeStruct((B, T, nh), F32),
                   jax.ShapeDtypeStruct((nh, B * T), F32)],
        compiler_params=_cparams(("arbitrary", "arbitrary")),
        name="inproj1",
    )(x, mods, mods, w_in1.astype(BF16), w_gate1, w_gate1.T, b_gate1.reshape(1, nh), b_gate1.reshape(nh, 1))


def _mlstm_body(q_ref, k_ref, v_ref, gcol_ref, grow_ref, cw_ref, cb_ref, c0_ref, n0_ref, m0_ref,
                h_ref, cfin_ref, nfin_ref, mfin_ref, xpad_s, q_s, k_s, c_s, n_s, m_s, *, T):
    hd = pl.program_id(1)
    n_c = T // CHUNK
    L = CHUNK

    rb = min(T, 512)
    pad = 8

    def conv_silu_into(src_ref, w, bias, dst_s, scale):
        xpad_s[0:pad] = jnp.zeros((pad, DH_C), F32)
        xpad_s[T + pad:T + 2 * pad] = jnp.zeros((pad, DH_C), F32)
        xpad_s[pad:T + pad] = src_ref[...]

        def blk(r, _):
            off = pl.multiple_of(r * rb, pad)
            win = xpad_s[pl.ds(off, rb + 2 * pad), :]
            acc = (pltpu.roll(win, 2, 0) * w[0:1] + pltpu.roll(win, 1, 0) * w[1:2] + win * w[2:3]
                   + pltpu.roll(win, rb + 2 * pad - 1, 0) * w[3:4])[pad:rb + pad] + bias
            dst_s[pl.ds(off, rb), :] = (jax.nn.silu(acc) * scale).astype(BF16)
            return 0

        lax.fori_loop(0, T // rb, blk, 0)

    conv_silu_into(q_ref, cw_ref[0], cb_ref[0], q_s, 1.0)
    conv_silu_into(k_ref, cw_ref[1], cb_ref[1], k_s, DH_C ** -0.5)

    t_i = lax.broadcasted_iota(I32, (L, L), 0)
    s_i = lax.broadcasted_iota(I32, (L, L), 1)

    for d in range(2):
        c_s[...] = c0_ref[d]
        n_s[...] = n0_ref[d]
        m_s[...] = m0_ref[d]
        mask = (s_i <= t_i) if d == 0 else (s_i >= t_i)
        last = L - 1 if d == 0 else 0

        def chunk(ci, _, d=d, mask=mask, last=last):
            c = ci if d == 0 else n_c - 1 - ci
            off = pl.multiple_of(c * L, L)
            qc = q_s[pl.ds(off, L), :]
            kc = k_s[pl.ds(off, L), :]
            vc = v_ref[pl.ds(off, L), :]
            gcol = gcol_ref[pl.ds(off, L), :]
            grow = grow_ref[:, pl.ds(off, L)]
            li_c = gcol[:, 2 * d:2 * d + 1]
            b_c = gcol[:, 2 * d + 1:2 * d + 2]
            li_r = grow[2 * d:2 * d + 1, :]
            b_r = grow[2 * d + 1:2 * d + 2, :]
            m_prev = m_s[...]
            log_d = jnp.where(mask, b_c - b_r + li_r, -jnp.inf)
            m_inter = b_c + m_prev
            m_t = jnp.maximum(m_inter, jnp.max(log_d, axis=-1, keepdims=True))
            dmat = jnp.exp(log_d - m_t)
            w_inter = jnp.exp(m_inter - m_t)
            c_b = c_s[...].astype(BF16)
            s = lax.dot_general(qc, kc, (((1,), (1,)), ((), ())), preferred_element_type=F32) * dmat
            inter = lax.dot_general(qc, c_b, (((1,), (1,)), ((), ())), preferred_element_type=F32)
            num = w_inter * inter + jnp.dot(s.astype(BF16), vc, preferred_element_type=F32)
            qn = jnp.sum(qc.astype(F32) * n_s[...], axis=-1, keepdims=True)
            den = w_inter * qn + jnp.sum(s, axis=-1, keepdims=True)
            h = num / jnp.maximum(jnp.abs(den), jnp.exp(-m_t))
            if d == 0:
                h_ref[pl.ds(off, L), :] = h
            else:
                h_ref[pl.ds(off, L), :] = h_ref[pl.ds(off, L), :] + h
            m_new = m_t[last:last + 1, :]
            b_last = b_c[last:last + 1, :]
            w_s = jnp.exp(b_last - b_c + li_c - m_new)
            decay = jnp.exp(b_last + m_prev - m_new)
            wv = (w_s * vc.astype(F32)).astype(BF16)
            kc32 = kc.astype(F32)
            c_s[...] = decay * c_s[...] + lax.dot_general(wv, kc, (((0,), (0,)), ((), ())),
                                                          preferred_element_type=F32)
            n_s[...] = decay * n_s[...] + jnp.sum(w_s * kc32, axis=0, keepdims=True)
            m_s[...] = m_new
            return 0

        lax.fori_loop(0, n_c, chunk, 0)
        cfin_ref[d] = c_s[...]
        nfin_ref[d] = n_s[...]
        mfin_ref[d] = m_s[...]


def _mlstm(qk, v, gcol, grow, conv_w, conv_b, c0, n0, m0):
    B, T, _ = v.shape
    DH = DH_C
    cw = conv_w.reshape(CONV_W, 2, H_C, DH).transpose(2, 1, 0, 3)
    cb = conv_b.reshape(2, H_C, 1, DH).transpose(1, 0, 2, 3)
    st = lambda *tail: pl.BlockSpec((None, 2, None) + tail, lambda b, h: (b, 0, h) + (0,) * len(tail))
    return pl.pallas_call(
        functools.partial(_mlstm_body, T=T),
        grid=(B, H_C),
        in_specs=[pl.BlockSpec((None, T, DH), lambda b, h: (b, 0, h)),
                  pl.BlockSpec((None, T, DH), lambda b, h: (b, 0, H_C + h)),
                  pl.BlockSpec((None, T, DH), lambda b, h: (b, 0, h)),
                  pl.BlockSpec((None, None, T, 4), lambda b, h: (b, h, 0, 0)),
                  pl.BlockSpec((None, None, 4, T), lambda b, h: (b, h, 0, 0)),
                  pl.BlockSpec((None, 2, CONV_W, DH), lambda b, h: (h, 0, 0, 0)),
                  pl.BlockSpec((None, 2, 1, DH), lambda b, h: (h, 0, 0, 0)),
                  st(DH, DH), st(1, DH), st(1, 1)],
        out_specs=[pl.BlockSpec((None, T, DH), lambda b, h: (b, 0, h)),
                   st(DH, DH), st(1, DH), st(1, 1)],
        out_shape=[jax.ShapeDtypeStruct((B, T, D_MODEL), F32),
                   jax.ShapeDtypeStruct((B, 2, H_C, DH, DH), F32),
                   jax.ShapeDtypeStruct((B, 2, H_C, 1, DH), F32),
                   jax.ShapeDtypeStruct((B, 2, H_C, 1, 1), F32)],
        scratch_shapes=[pltpu.VMEM((T + 16, DH), F32), pltpu.VMEM((T, DH), BF16), pltpu.VMEM((T, DH), BF16),
                        pltpu.VMEM((DH, DH), F32), pltpu.VMEM((1, DH), F32), pltpu.VMEM((1, 1), F32)],
        compiler_params=_cparams(("arbitrary", "arbitrary")),
        name="mlstm",
    )(qk, qk, v, gcol, grow, cw, cb, c0, n0, m0)


def _outproj1_body(x_ref, h_ref, o_ref, g1_ref, w_ref, lg_ref, lb_ref, out_ref):
    y = jax.nn.sigmoid(o_ref[...]) * h_ref[...]
    out = jnp.dot(y.astype(BF16), w_ref[...], preferred_element_type=F32)
    v = ALPHA_DN * x_ref[...] + g1_ref[...] * out
    out_ref[...] = _layer_norm(v, lg_ref[...], lb_ref[...])


def _outproj1(x, h, o, mods, w_out1, ln_g, ln_b, row_off, row_stride):
    B, T, D = x.shape
    tm = min(TOKEN_TILE, T)
    tok = pl.BlockSpec((None, tm, D), lambda b, j: (b, j, 0))
    vec = pl.BlockSpec((1, D), lambda b, j: (0, 0))
    return pl.pallas_call(
        _outproj1_body,
        grid=(B, T // tm),
        in_specs=[tok, tok, tok, _mod_spec(1, 2, row_off, row_stride),
                  pl.BlockSpec((D, D), lambda b, j: (0, 0)), vec, vec],
        out_specs=tok,
        out_shape=jax.ShapeDtypeStruct((B, T, D), F32),
        compiler_params=_cparams(("arbitrary", "arbitrary")),
        name="outproj1",
    )(x, h, o, mods, w_out1.astype(BF16), ln_g.reshape(1, D), ln_b.reshape(1, D))


def _to_col_major(x):
    B, T, C = x.shape
    rows = T // GRID_W
    return x.reshape(B, rows, GRID_W, C).transpose(0, 2, 1, 3).reshape(B, T, C)


def _to_row_major(x):
    B, T, C = x.shape
    rows = T // GRID_W
    return x.reshape(B, GRID_W, rows, C).transpose(0, 2, 1, 3).reshape(B, T, C)


def _trunk(x, mods, row_off, row_stride, st, P, s5_mats, grid):
    h_rg, s_re, s_im, m_c, m_n, m_m = st
    B, T, D = x.shape
    xa_tm, ga, ub = _inproj0(x, mods, P['w_in0'][0].astype(BF16), row_off, row_stride)
    h_tm, h_fin = _rglru(xa_tm, B, P['conv_a_w'][0], P['conv_a_b'][0], P['rg_wa'][0], P['rg_ba'][0],
                         P['rg_wi'][0], P['rg_bi'][0], P['rg_lam'][0], h_rg[:, 0].transpose(1, 0, 2))
    y5, s_fin = _s5(ub, s5_mats, _s5_state_to_lanes(s_re[:, 0], s_im[:, 0]))
    x = _outproj0(x, h_tm, ga, y5, ub, mods, P['s5_d'][0], P['glu_w'][0], P['glu_b'][0], P['w_out0'][0],
                  P['ln_g'][0, 0], P['ln_b'][0, 0], row_off, row_stride)
    x = _moe_layer(x, mods, 0, P['w_router'][0], P['b_router'][0], P['w_gu'], P['b_gu'],
                   P['w_down'], P['b_down'], P['ln_g'][0, 1], P['ln_b'][0, 1], row_off, row_stride)
    if grid:
        x = _to_col_major(x)
    qk, v, o, gcol, grow = _inproj1(x, mods, P['w_in1'][0], P['w_gate1'][0], P['b_gate1'][0], row_off, row_stride)
    gcol = gcol.reshape(B, T, 2, 2, H_C).transpose(0, 4, 1, 2, 3).reshape(B, H_C, T, 4)
    grow = grow.reshape(2, 2, H_C, B, T).transpose(3, 2, 0, 1, 4).reshape(B, H_C, 4, T)
    h, c_fin, n_fin, m_fin = _mlstm(qk, v, gcol, grow, P['conv_c_w'][0], P['conv_c_b'][0],
                                    m_c[:, 0], m_n[:, 0][:, :, :, None, :], m_m[:, 0][:, :, :, None, None])
    x = _outproj1(x, h, o, mods, P['w_out1'][0], P['ln_g'][1, 0], P['ln_b'][1, 0], row_off, row_stride)
    x = _moe_layer(x, mods, 1, P['w_router'][1], P['b_router'][1], P['w_gu'], P['b_gu'],
                   P['w_down'], P['b_down'], P['ln_g'][1, 1], P['ln_b'][1, 1], row_off, row_stride)
    if grid:
        x = _to_row_major(x)
    new_re, new_im = _s5_lanes_to_state(s_fin)
    new = (h_fin.transpose(1, 0, 2)[:, None], new_re[:, None], new_im[:, None],
           c_fin[:, None], n_fin[:, None, :, :, 0, :], m_fin[:, None, :, :, 0, 0])
    return x, new


def _forward(x_prompt, x_sample, c, c_ctx, states, P):
    bp = x_prompt.shape[0]
    bs = x_sample.shape[0]
    rows = 1 + bs
    rpad = -(-rows // 8) * 8
    cv = jnp.concatenate([c_ctx[None, :], c, jnp.zeros((rpad - rows, D_MODEL), F32)], axis=0)
    mods = _modulation(cv, P['w_mod'], P['b_mod'])
    s5_mats = _s5_matrices(P['s5_a_re'][0], P['s5_a_im'][0], P['s5_log_dt'][0], P['s5_b_re'][0], P['s5_b_im'][0],
                           P['s5_c_re'][0], P['s5_c_im'][0])
    zero_state = (jnp.zeros((bp, 1, 2, D_A), F32),
                  jnp.zeros((bp, 1, 2, G_B, P_B), F32),
                  jnp.zeros((bp, 1, 2, G_B, P_B), F32),
                  jnp.zeros((bp, 1, 2, H_C, DH_C, DH_C), F32),
                  jnp.zeros((bp, 1, 2, H_C, DH_C), F32),
                  jnp.zeros((bp, 1, 2, H_C), F32))
    y_prompt, new = _trunk(x_prompt, mods, 0, 0, zero_state, P, s5_mats, False)
    y_sample, _ = _trunk(x_sample, mods, 1, 1, states, P, s5_mats, True)
    return (y_prompt, y_sample) + tuple(new)


def kernel(x_prompt, x_sample, c, c_ctx, state_rglru, state_s5_re, state_s5_im, state_mlstm_C, state_mlstm_n, state_mlstm_m, w_mod, b_mod, ln_g, ln_b, w_in0, conv_a_w, conv_a_b, rg_wa, rg_ba, rg_wi, rg_bi, rg_lam, s5_a_re, s5_a_im, s5_log_dt, s5_b_re, s5_b_im, s5_c_re, s5_c_im, s5_d, glu_w, glu_b, w_out0, w_in1, w_gate1, b_gate1, conv_c_w, conv_c_b, w_out1, w_router, b_router, w_gu, b_gu, w_down, b_down):
    P = dict(w_mod=w_mod, b_mod=b_mod, ln_g=ln_g, ln_b=ln_b, w_in0=w_in0, conv_a_w=conv_a_w,
             conv_a_b=conv_a_b, rg_wa=rg_wa, rg_ba=rg_ba, rg_wi=rg_wi, rg_bi=rg_bi, rg_lam=rg_lam,
             s5_a_re=s5_a_re, s5_a_im=s5_a_im, s5_log_dt=s5_log_dt, s5_b_re=s5_b_re, s5_b_im=s5_b_im,
             s5_c_re=s5_c_re, s5_c_im=s5_c_im, s5_d=s5_d, glu_w=glu_w, glu_b=glu_b, w_out0=w_out0,
             w_in1=w_in1, w_gate1=w_gate1, b_gate1=b_gate1, conv_c_w=conv_c_w, conv_c_b=conv_c_b,
             w_out1=w_out1, w_router=w_router, b_router=b_router, w_gu=w_gu, b_gu=b_gu,
             w_down=w_down, b_down=b_down)
    states = (state_rglru, state_s5_re, state_s5_im, state_mlstm_C, state_mlstm_n, state_mlstm_m)
    return _forward(x_prompt, x_sample, c, c_ctx, states, P)
```

```python
import functools
import math

import jax
import jax.numpy as jnp
from jax import lax
from jax.experimental import pallas as pl
from jax.experimental.pallas import tpu as pltpu

F32 = jnp.float32
BF16 = jnp.bfloat16
I32 = jnp.int32
HI = lax.Precision.HIGHEST

D_MODEL = 1024
DEPTH = 2
GRID_W = 64
D_A = 512
NB_A = 8
BS_A = D_A // NB_A
CONV_W = 4
LRU_C = 8.0
D_B = 512
S5_GROUP = 16
G_B = D_B // S5_GROUP
P_B = 64
H_C = 4
DH_C = D_MODEL // H_C
CHUNK = 128
N_EXP = 32
TOP_K = 4
D_FF = D_MODEL
SWIGLU_LIMIT = 7.0
SWIGLU_ALPHA = 1.702
ALPHA_DN = (2 * DEPTH) ** 0.25
LN_EPS = 1e-5

S5_L = 16
S5_W = S5_L * S5_GROUP
S5_OCT = 8
TOKEN_TILE = 256
EXPERT_TILE = 256
VMEM_LIMIT = 56 * 1024 * 1024


def _cparams(sem, vmem=VMEM_LIMIT):
    return pltpu.CompilerParams(dimension_semantics=sem, vmem_limit_bytes=vmem)


def _layer_norm(v, g, b):
    mu = jnp.mean(v, axis=-1, keepdims=True)
    c = v - mu
    var = jnp.mean(c * c, axis=-1, keepdims=True)
    return c * lax.rsqrt(var + LN_EPS) * g + b


def _mod_spec(layer, which, row_off, row_stride):
    return pl.BlockSpec((None, None, None, 1, D_MODEL),
                        lambda b, j: (layer, which, row_off + b * row_stride, 0, 0))


def _mod_spec_all(layer, which, nb):
    return pl.BlockSpec((None, None, nb, 1, D_MODEL), lambda i: (layer, which, 0, 0, 0))


def _row_permutation(n_out_major, n_out_minor):
    n = n_out_major * n_out_minor
    r_out = lax.broadcasted_iota(I32, (n, n), 0)
    r_in = lax.broadcasted_iota(I32, (n, n), 1)
    hit = jnp.logical_and(r_out // n_out_minor == r_in % n_out_major, r_out % n_out_minor == r_in // n_out_major)
    return jnp.where(hit, 1.0, 0.0).astype(BF16)


def _mod_body(c_ref, w_ref, b_ref, o_ref):
    s = jax.nn.silu(c_ref[...])
    o_ref[...] = jnp.dot(s, w_ref[...], precision=HI, preferred_element_type=F32) + b_ref[...]


def _modulation(cv, w_mod, b_mod):
    R, D = cv.shape
    L, _, N6 = w_mod.shape
    tn = N6 // 6
    out = pl.pallas_call(
        _mod_body,
        grid=(L, N6 // tn),
        in_specs=[pl.BlockSpec((R, D), lambda l, j: (0, 0)),
                  pl.BlockSpec((None, D, tn), lambda l, j: (l, 0, j)),
                  pl.BlockSpec((None, 1, tn), lambda l, j: (l, 0, j))],
        out_specs=pl.BlockSpec((None, R, tn), lambda l, j: (l, 0, j)),
        out_shape=jax.ShapeDtypeStruct((L, R, N6), F32),
        compiler_params=_cparams(("arbitrary", "arbitrary")),
        name="modulation",
    )(cv, w_mod, b_mod.reshape(L, 1, N6))
    return out.reshape(L, R, 6, D).transpose(0, 2, 1, 3)[:, :, :, None, :]


def _inproj0_body(x_ref, sc_ref, sh_ref, w_ref, xa_ref, ga_ref, ub_ref):
    nb, tq, d = x_ref.shape
    u = x_ref[...] * (1.0 + sc_ref[...]) + sh_ref[...]
    ub = u.reshape(nb * tq, d).astype(BF16)
    ut = jnp.dot(_row_permutation(tq, nb), ub, preferred_element_type=F32).astype(BF16)
    z = jnp.dot(ut, w_ref[...], preferred_element_type=F32).reshape(tq, nb, w_ref.shape[1])
    xa_ref[...] = z[:, :, :D_A]
    ga_ref[...] = z[:, :, D_A:2 * D_A]
    ub_ref[...] = z[:, :, 2 * D_A:]


def _inproj0(x, pmods, w_in0):
    B, T, D = x.shape
    tq = TOKEN_TILE // B
    nz = w_in0.shape[1]
    tmaj = pl.BlockSpec((tq, B, D_A), lambda i: (i, 0, 0))
    return pl.pallas_call(
        _inproj0_body,
        grid=(T // tq,),
        in_specs=[pl.BlockSpec((B, tq, D), lambda i: (0, i, 0)),
                  _mod_spec_all(0, 1, B), _mod_spec_all(0, 0, B),
                  pl.BlockSpec((D, nz), lambda i: (0, 0))],
        out_specs=[tmaj, tmaj, tmaj],
        out_shape=[jax.ShapeDtypeStruct((T, B, D_A), F32)] * 3,
        compiler_params=_cparams(("arbitrary",)),
        name="inproj0",
    )(x, pmods, pmods, w_in0)


def _rglru_body(cur_ref, prev_ref, next_ref, cw_ref, cb_ref, wa_ref, wi_ref, ba_ref, bi_ref, lam_ref, h0_ref,
                h_ref, hfin_ref, ext_s, a_s, b_s, carry_s, *, tt, n_t):
    d = pl.program_id(0)
    j = pl.program_id(1)
    jj = j + d * (n_t - 1 - 2 * j)
    nb = cur_ref.shape[1]
    half = D_A // 2
    ext_s[0:2] = jnp.where(jj == 0, 0.0, prev_ref[...])
    ext_s[2:tt + 2] = cur_ref[...]
    ext_s[tt + 2:tt + 3] = jnp.where(jj == n_t - 1, 0.0, next_ref[...])
    xc = ext_s[0:tt] * cw_ref[0]
    for k in range(1, CONV_W):
        xc = xc + ext_s[k:k + tt] * cw_ref[k]
    xc = xc + cb_ref[...]
    x2 = xc.reshape(tt * nb, D_A)
    xb = x2.astype(BF16)

    def gate(w_ref, bias_ref):
        lo = jnp.dot(xb[:, :half], w_ref[0], preferred_element_type=F32)
        hi = jnp.dot(xb[:, half:], w_ref[1], preferred_element_type=F32)
        return jax.nn.sigmoid(jnp.concatenate([lo, hi], axis=1) + bias_ref[...])

    r = gate(wa_ref, ba_ref)
    i = gate(wi_ref, bi_ref)
    log_a = LRU_C * r * jax.nn.log_sigmoid(lam_ref[...])
    a = jnp.exp(log_a)
    one_minus_a2 = -jnp.tanh(log_a) * (a * a + 1.0)
    a_s[...] = a.reshape(tt, nb, D_A)
    b_s[...] = (jnp.sqrt(one_minus_a2) * (i * x2)).reshape(tt, nb, D_A)

    @pl.when(j == 0)
    def _():
        carry_s[...] = h0_ref[...]

    def step(t, h):
        h = a_s[t] * h + b_s[t]
        h_ref[t] = h
        return h

    @pl.when(d == 0)
    def _():
        carry_s[...] = lax.fori_loop(0, tt, step, carry_s[...], unroll=8)

    @pl.when(d == 1)
    def _():
        carry_s[...] = lax.fori_loop(0, tt, lambda t, h: step(tt - 1 - t, h), carry_s[...], unroll=8)

    hfin_ref[...] = carry_s[...]


def _block_diag_halves(w):
    nd = w.shape[0]
    per = NB_A // 2
    w = w.reshape(nd, 2, per, BS_A, BS_A)
    eye = jnp.eye(per, dtype=w.dtype)
    out = jnp.einsum('dhncz,nm->dhncmz', w, eye)
    return out.reshape(nd, 2, per * BS_A, per * BS_A)


def _rglru(x3, conv_w, conv_b, wa, ba, wi, bi, lam, h0):
    T, B, _ = x3.shape
    tt = min(T, 1024 // B)
    n_t = T // tt

    def cur_map(d, j):
        return (j + d * (n_t - 1 - 2 * j), 0, 0)

    def prev_map(d, j):
        jj = j + d * (n_t - 1 - 2 * j)
        return (jnp.maximum(jj * (tt // 2) - 1, 0), 0, 0)

    def next_map(d, j):
        jj = j + d * (n_t - 1 - 2 * j)
        return (jnp.minimum((jj + 1) * tt, T - 1), 0, 0)

    dir_spec3 = pl.BlockSpec((None, 1, D_A), lambda d, j: (d, 0, 0))
    h, hfin = pl.pallas_call(
        functools.partial(_rglru_body, tt=tt, n_t=n_t),
        grid=(2, n_t),
        in_specs=[pl.BlockSpec((tt, B, D_A), cur_map),
                  pl.BlockSpec((2, B, D_A), prev_map),
                  pl.BlockSpec((1, B, D_A), next_map),
                  pl.BlockSpec((CONV_W, 1, D_A), lambda d, j: (0, 0, 0)),
                  pl.BlockSpec((1, 1, D_A), lambda d, j: (0, 0, 0)),
                  pl.BlockSpec((None, 2, D_A // 2, D_A // 2), lambda d, j: (d, 0, 0, 0)),
                  pl.BlockSpec((None, 2, D_A // 2, D_A // 2), lambda d, j: (d, 0, 0, 0)),
                  dir_spec3, dir_spec3, dir_spec3,
                  pl.BlockSpec((None, B, D_A), lambda d, j: (d, 0, 0))],
        out_specs=[pl.BlockSpec((None, tt, B, D_A), lambda d, j: (d,) + cur_map(d, j)),
                   pl.BlockSpec((None, B, D_A), lambda d, j: (d, 0, 0))],
        out_shape=[jax.ShapeDtypeStruct((2, T, B, D_A), F32),
                   jax.ShapeDtypeStruct((2, B, D_A), F32)],
        scratch_shapes=[pltpu.VMEM((tt + 3, B, D_A), F32),
                        pltpu.VMEM((tt, B, D_A), F32),
                        pltpu.VMEM((tt, B, D_A), F32),
                        pltpu.VMEM((B, D_A), F32)],
        compiler_params=_cparams(("arbitrary", "arbitrary")),
        name="rglru",
    )(x3, x3, x3, conv_w.reshape(CONV_W, 1, D_A), conv_b.reshape(1, 1, D_A),
      _block_diag_halves(wa).astype(BF16), _block_diag_halves(wi).astype(BF16),
      ba.reshape(2, 1, D_A), bi.reshape(2, 1, D_A), lam.reshape(2, 1, D_A), h0)
    return h, hfin


def _s5_matrices(a_re, a_im, log_dt, b_re, b_im, c_re, c_im):
    L = S5_L
    lam = lax.complex(a_re.astype(F32), a_im.astype(F32))
    dt = jnp.exp(log_dt.astype(F32))[..., None]
    ldt = lam * dt
    a_bar = jnp.exp(ldt)
    b_bar = ((a_bar - 1.0) / lam)[..., None] * lax.complex(b_re.astype(F32), b_im.astype(F32))
    cc = lax.complex(c_re.astype(F32), c_im.astype(F32))
    ks = jnp.arange(L + 1, dtype=F32)
    pw = jnp.exp(ldt[:, :, None, :] * ks[None, None, :, None].astype(jnp.complex64))
    kern = jnp.real(jnp.einsum('dgjp,dgkp,dgpi->dgkji', cc, pw[:, :, :L], b_bar, precision=HI))
    s_idx = jnp.arange(L)[:, None]
    t_idx = jnp.arange(L)[None, :]
    lag_f = jnp.clip(t_idx - s_idx, 0, L - 1)
    lag_b = jnp.clip(s_idx - t_idx, 0, L - 1)
    m_f = jnp.where((t_idx >= s_idx)[None, :, :, None, None], kern[0][:, lag_f], 0.0)
    m_b = jnp.where((s_idx >= t_idx)[None, :, :, None, None], kern[1][:, lag_b], 0.0)
    m = (m_f + m_b).transpose(0, 1, 4, 2, 3).reshape(G_B, S5_W, S5_W)
    down = (L - ks[:L])[None, :, None].astype(jnp.complex64)
    pw_down_b = jnp.exp(ldt[1][:, None, :] * down)
    g_f = jnp.exp(ldt[0][:, None, :] * (down - 1.0))[..., None] * b_bar[0][:, None]
    g_b = pw[1][:, :L, :, None] * b_bar[1][:, None]

    def g_cols(x):
        return x.transpose(0, 1, 3, 2).reshape(G_B, S5_W, P_B)

    gs = jnp.concatenate([g_cols(jnp.real(g_f)), g_cols(jnp.real(g_b)),
                          g_cols(jnp.imag(g_f)), g_cols(jnp.imag(g_b))], axis=-1)
    e_f = cc[0][:, None] * pw[0][:, 1:, None, :]
    e_b = cc[1][:, None] * pw_down_b[:, :, None, :]

    def e_rows(x):
        return x.transpose(0, 3, 1, 2).reshape(G_B, P_B, S5_W)

    e = jnp.concatenate([e_rows(jnp.real(e_f)), e_rows(jnp.real(e_b)),
                         -e_rows(jnp.imag(e_f)), -e_rows(jnp.imag(e_b))], axis=1)
    a_l = pw[:, :, L]
    al = jnp.concatenate([jnp.real(a_l[0]), jnp.real(a_l[1]), jnp.imag(a_l[0]), jnp.imag(a_l[1])], axis=-1)
    no = G_B // S5_OCT
    eye = jnp.eye(S5_OCT, dtype=F32)
    m_o = jnp.einsum('qgsitj,gh->qsgithj', m.reshape(no, S5_OCT, L, S5_GROUP, L, S5_GROUP), eye)
    g_o = jnp.einsum('qgsil,gh->qsgihl', gs.reshape(no, S5_OCT, L, S5_GROUP, S5_W), eye)
    e_o = jnp.einsum('qgltj,gh->qglthj', e.reshape(no, S5_OCT, S5_W, L, S5_GROUP), eye)
    ow = S5_OCT * S5_W
    return (m_o.reshape(no, ow, ow).astype(BF16), g_o.reshape(no, ow, ow).astype(BF16),
            e_o.reshape(no, ow, ow).astype(BF16), al.reshape(1, G_B * S5_W))


def _s5_fill_lhs(u_ref, lhs_s):
    tc, _, nb, lanes = u_ref.shape
    for s in range(S5_L):
        lhs_s[:, s * lanes:(s + 1) * lanes] = u_ref[:, s].reshape(tc * nb, lanes).astype(BF16)


def _s5_state_body(u_ref, g_ref, f_ref, lhs_s):
    _s5_fill_lhs(u_ref, lhs_s)
    f_ref[...] = jnp.dot(lhs_s[...], g_ref[...], preferred_element_type=F32).reshape(f_ref.shape)


def _s5_scan_body(f_ref, a_ref, s0_ref, sin_ref, sfin_ref, *, n_c):
    nb, width = s0_ref.shape
    hw = 2 * P_B
    n_g = width // S5_W
    a = a_ref[...]
    a_re = [a[:, k * S5_W:k * S5_W + hw] for k in range(n_g)]
    a_im = [a[:, k * S5_W + hw:(k + 1) * S5_W] for k in range(n_g)]
    is_fwd = (lax.broadcasted_iota(I32, (nb, width), 1) % hw) < P_B

    def split(x):
        return tuple(x[:, k * hw:(k + 1) * hw] for k in range(2 * n_g))

    def merge(parts):
        return jnp.concatenate(parts, axis=-1)

    def advance(c, parts):
        f = split(f_ref[c])
        out = []
        for k in range(n_g):
            s_re, s_im = parts[2 * k], parts[2 * k + 1]
            out.append(a_re[k] * s_re - a_im[k] * s_im + f[2 * k])
            out.append(a_re[k] * s_im + a_im[k] * s_re + f[2 * k + 1])
        return tuple(out)

    def fwd(c, parts):
        sin_ref[c] = merge(parts)
        return advance(c, parts)

    init = split(s0_ref[...])
    fin_f = lax.fori_loop(0, n_c, fwd, init)

    def bwd(k, parts):
        c = n_c - 1 - k
        sin_ref[c] = jnp.where(is_fwd, sin_ref[c], merge(parts))
        return advance(c, parts)

    fin_b = lax.fori_loop(0, n_c, bwd, init)
    sfin_ref[...] = jnp.where(is_fwd, merge(fin_f), merge(fin_b))


def _s5_out_body(u_ref, sin_ref, m_ref, e_ref, y_ref, lhs_s):
    _s5_fill_lhs(u_ref, lhs_s)
    tc, _, nb, lanes = u_ref.shape
    sin = sin_ref[...].reshape(tc * nb, sin_ref.shape[-1]).astype(BF16)
    y = (jnp.dot(lhs_s[...], m_ref[...], preferred_element_type=F32)
         + jnp.dot(sin, e_ref[...], preferred_element_type=F32))
    for s in range(S5_L):
        y_ref[:, s] = y[:, s * lanes:(s + 1) * lanes].reshape(tc, nb, lanes)


def _s5(ub, mats, s0):
    m_o, g_o, e_o, al = mats
    T, B, _ = ub.shape
    n_c = T // S5_L
    no = G_B // S5_OCT
    lanes = S5_OCT * S5_GROUP
    ow = S5_OCT * S5_W
    tc = min(n_c, TOKEN_TILE // B)
    u4 = ub.reshape(n_c, S5_L, B, D_B)
    u_spec = pl.BlockSpec((tc, S5_L, B, lanes), lambda q, i: (i, 0, 0, q))
    w_spec = pl.BlockSpec((None, ow, ow), lambda q, i: (q, 0, 0))
    st_spec = pl.BlockSpec((tc, B, ow), lambda q, i: (i, 0, q))
    f_loc = pl.pallas_call(
        _s5_state_body,
        grid=(no, n_c // tc),
        in_specs=[u_spec, w_spec],
        out_specs=st_spec,
        out_shape=jax.ShapeDtypeStruct((n_c, B, G_B * S5_W), F32),
        scratch_shapes=[pltpu.VMEM((tc * B, S5_L * lanes), BF16)],
        compiler_params=_cparams(("arbitrary", "arbitrary")),
        name="s5_state",
    )(u4, g_o)
    sw = 4 * S5_W
    sin, sfin = pl.pallas_call(
        functools.partial(_s5_scan_body, n_c=n_c),
        grid=(G_B * S5_W // sw,),
        in_specs=[pl.BlockSpec((n_c, B, sw), lambda g: (0, 0, g)),
                  pl.BlockSpec((1, sw), lambda g: (0, g)),
                  pl.BlockSpec((B, sw), lambda g: (0, g))],
        out_specs=[pl.BlockSpec((n_c, B, sw), lambda g: (0, 0, g)),
                   pl.BlockSpec((B, sw), lambda g: (0, g))],
        out_shape=[jax.ShapeDtypeStruct((n_c, B, G_B * S5_W), F32),
                   jax.ShapeDtypeStruct((B, G_B * S5_W), F32)],
        compiler_params=_cparams(("arbitrary",)),
        name="s5_scan",
    )(f_loc, al, s0)
    y = pl.pallas_call(
        _s5_out_body,
        grid=(no, n_c // tc),
        in_specs=[u_spec, st_spec, w_spec, w_spec],
        out_specs=u_spec,
        out_shape=jax.ShapeDtypeStruct((n_c, S5_L, B, D_B), F32),
        scratch_shapes=[pltpu.VMEM((tc * B, S5_L * lanes), BF16)],
        compiler_params=_cparams(("arbitrary", "arbitrary")),
        name="s5_out",
    )(u4, sin, m_o, e_o)
    return y.reshape(T, B, D_B), sfin


def _s5_state_to_lanes(s_re, s_im):
    parts = [s_re[:, 0], s_re[:, 1], s_im[:, 0], s_im[:, 1]]
    return jnp.concatenate(parts, axis=-1).reshape(s_re.shape[0], G_B * S5_W)


def _s5_lanes_to_state(s):
    s = s.reshape(s.shape[0], G_B, 4, P_B)
    return jnp.stack([s[:, :, 0], s[:, :, 1]], axis=1), jnp.stack([s[:, :, 2], s[:, :, 3]], axis=1)


def _outproj0_body(x_ref, h_ref, ga_ref, y_ref, ub_ref, g1_ref, d_ref, gw_ref, gb_ref, wo_ref,
                   lg_ref, lb_ref, o_ref):
    nb, tq, d = x_ref.shape
    rows = tq * nb

    def flat(v):
        return v.reshape(rows, v.shape[-1])

    ya = flat(h_ref[0] + h_ref[1]) * jax.nn.gelu(flat(ga_ref[...]))
    yb = flat(y_ref[...]) + d_ref[...] * flat(ub_ref[...])
    g = jax.nn.gelu(yb)
    gate = jax.nn.sigmoid(jnp.dot(g.astype(BF16), gw_ref[...], preferred_element_type=F32) + gb_ref[...])
    cat = jnp.concatenate([ya, g * gate], axis=1).astype(BF16)
    cat = jnp.dot(_row_permutation(nb, tq), cat, preferred_element_type=F32).astype(BF16)
    out = jnp.dot(cat, wo_ref[...], preferred_element_type=F32).reshape(nb, tq, d)
    v = ALPHA_DN * x_ref[...] + g1_ref[...] * out
    o_ref[...] = _layer_norm(v, lg_ref[...], lb_ref[...])


def _outproj0(x, h, ga, y5, ub, pmods, s5_d, glu_w, glu_b, w_out0, ln_g, ln_b):
    B, T, D = x.shape
    tq = TOKEN_TILE // B
    tmaj = pl.BlockSpec((tq, B, D_A), lambda i: (i, 0, 0))
    vec = lambda w: pl.BlockSpec((1, w), lambda i: (0, 0))
    return pl.pallas_call(
        _outproj0_body,
        grid=(T // tq,),
        in_specs=[pl.BlockSpec((B, tq, D), lambda i: (0, i, 0)),
                  pl.BlockSpec((2, tq, B, D_A), lambda i: (0, i, 0, 0)),
                  tmaj, tmaj, tmaj,
                  _mod_spec_all(0, 2, B),
                  vec(D_B),
                  pl.BlockSpec((D_B, D_B), lambda i: (0, 0)),
                  vec(D_B),
                  pl.BlockSpec((D, D), lambda i: (0, 0)),
                  vec(D), vec(D)],
        out_specs=pl.BlockSpec((B, tq, D), lambda i: (0, i, 0)),
        out_shape=jax.ShapeDtypeStruct((B, T, D), F32),
        compiler_params=_cparams(("arbitrary",)),
        name="outproj0",
    )(x, h, ga, y5, ub, pmods, s5_d.reshape(1, D_B), glu_w.astype(BF16), glu_b.reshape(1, D_B),
      w_out0.astype(BF16), ln_g.reshape(1, D), ln_b.reshape(1, D))


def _router_body(x_ref, sc_ref, sh_ref, wr_ref, br_ref, u_ref, topi_ref, gate_ref, rank_ref, cnt_ref, cnt_s):
    first = jnp.logical_and(pl.program_id(0) == 0, pl.program_id(1) == 0)

    @pl.when(first)
    def _():
        cnt_s[...] = jnp.zeros_like(cnt_s)

    u = x_ref[...] * (1.0 + sc_ref[...]) + sh_ref[...]
    u_ref[...] = u.astype(BF16)
    tm = u.shape[0]
    logits = lax.dot_general(wr_ref[...], u, (((1,), (1,)), ((), ())), precision=HI,
                             preferred_element_type=F32) + br_ref[...]
    e_iota = lax.broadcasted_iota(I32, logits.shape, 0)
    work = logits
    vals, idxs, hots = [], [], []
    for _ in range(TOP_K):
        m = jnp.max(work, axis=0, keepdims=True)
        idx = jnp.min(jnp.where(work == m, e_iota, N_EXP), axis=0, keepdims=True)
        hot = e_iota == idx
        vals.append(m)
        idxs.append(idx)
        hots.append(hot)
        work = jnp.where(hot, -jnp.inf, work)
    ex = [jnp.exp(v - vals[0]) for v in vals]
    den = ex[0] + ex[1] + ex[2] + ex[3]
    gate_ref[...] = jnp.concatenate([e / den for e in ex], axis=0)
    topi_ref[...] = jnp.concatenate(idxs, axis=0)
    hot_sum = jnp.zeros(logits.shape, F32)
    for hot in hots:
        hot_sum = hot_sum + hot.astype(F32)
    before = lax.broadcasted_iota(I32, (tm, tm), 0) < lax.broadcasted_iota(I32, (tm, tm), 1)
    excl = jnp.dot(hot_sum.astype(BF16), before.astype(BF16), preferred_element_type=F32)
    base = excl + cnt_s[...]
    ranks = [jnp.sum(jnp.where(hot, base, 0.0), axis=0, keepdims=True) for hot in hots]
    rank_ref[...] = jnp.concatenate(ranks, axis=0).astype(I32)
    cnt_s[...] = cnt_s[...] + jnp.sum(hot_sum, axis=1, keepdims=True)
    cnt_ref[...] = cnt_s[...]


def _router(x, mods, layer, w_router, b_router, row_off, row_stride):
    B, T, D = x.shape
    tm = min(TOKEN_TILE, T)
    n_j = T // tm
    N = B * T
    lane_spec = pl.BlockSpec((TOP_K, tm), lambda b, j: (0, b * n_j + j))
    return pl.pallas_call(
        _router_body,
        grid=(B, n_j),
        in_specs=[pl.BlockSpec((None, tm, D), lambda b, j: (b, j, 0)),
                  _mod_spec(layer, 4, row_off, row_stride),
                  _mod_spec(layer, 3, row_off, row_stride),
                  pl.BlockSpec((N_EXP, D), lambda b, j: (0, 0)),
                  pl.BlockSpec((N_EXP, 1), lambda b, j: (0, 0))],
        out_specs=[pl.BlockSpec((None, tm, D), lambda b, j: (b, j, 0)),
                   lane_spec, lane_spec, lane_spec,
                   pl.BlockSpec((N_EXP, 1), lambda b, j: (0, 0))],
        out_shape=[jax.ShapeDtypeStruct((B, T, D), BF16),
                   jax.ShapeDtypeStruct((TOP_K, N), I32),
                   jax.ShapeDtypeStruct((TOP_K, N), F32),
                   jax.ShapeDtypeStruct((TOP_K, N), I32),
                   jax.ShapeDtypeStruct((N_EXP, 1), F32)],
        scratch_shapes=[pltpu.VMEM((N_EXP, 1), F32)],
        compiler_params=_cparams(("arbitrary", "arbitrary")),
        name="router",
    )(x, mods, mods, w_router.T, b_router.reshape(N_EXP, 1))


def _expert_body(blk_e_ref, n_used_ref, xs_ref, wgu_ref, bgu_ref, wdn_ref, bdn_ref, o_ref, wgu_s, wdn_s):
    i = pl.program_id(0)
    prev = blk_e_ref[jnp.maximum(i - 1, 0)]
    changed = jnp.logical_or(i == 0, blk_e_ref[i] != prev)

    @pl.when(changed)
    def _():
        wgu_s[...] = wgu_ref[...].astype(BF16)
        wdn_s[...] = wdn_ref[...].astype(BF16)

    @pl.when(i < n_used_ref[0])
    def _():
        h = jnp.dot(xs_ref[...], wgu_s[...], preferred_element_type=F32) + bgu_ref[...]
        gt = jnp.minimum(h[:, :D_FF], SWIGLU_LIMIT)
        up = jnp.clip(h[:, D_FF:], -SWIGLU_LIMIT, SWIGLU_LIMIT)
        act = (up + 1.0) * gt * jax.nn.sigmoid(SWIGLU_ALPHA * gt)
        o_ref[...] = jnp.dot(act.astype(BF16), wdn_s[...], preferred_element_type=F32) + bdn_ref[...]

    @pl.when(i >= n_used_ref[0])
    def _():
        o_ref[...] = jnp.zeros_like(o_ref)


def _experts(xs, blk_e, n_used, layer, w_gu, b_gu, w_dn, b_dn):
    slots, D = xs.shape
    te = EXPERT_TILE
    n_blk = slots // te
    grid_spec = pltpu.PrefetchScalarGridSpec(
        num_scalar_prefetch=2,
        grid=(n_blk,),
        in_specs=[pl.BlockSpec((te, D), lambda i, be, nu: (i, 0)),
                  pl.BlockSpec((None, None, D, 2 * D_FF), lambda i, be, nu: (layer, be[i], 0, 0)),
                  pl.BlockSpec((None, None, 1, 2 * D_FF), lambda i, be, nu: (layer, be[i], 0, 0)),
                  pl.BlockSpec((None, None, D_FF, D), lambda i, be, nu: (layer, be[i], 0, 0)),
                  pl.BlockSpec((None, None, 1, D), lambda i, be, nu: (layer, be[i], 0, 0))],
        out_specs=pl.BlockSpec((te, D), lambda i, be, nu: (i, 0)),
        scratch_shapes=[pltpu.VMEM((D, 2 * D_FF), BF16), pltpu.VMEM((D_FF, D), BF16)],
    )
    return pl.pallas_call(
        _expert_body,
        grid_spec=grid_spec,
        out_shape=jax.ShapeDtypeStruct((slots, D), F32),
        compiler_params=_cparams(("arbitrary",)),
        name="experts",
    )(blk_e, n_used, xs, w_gu, b_gu.reshape(DEPTH, N_EXP, 1, 2 * D_FF), w_dn, b_dn.reshape(DEPTH, N_EXP, 1, D))


def _combine_body(x_ref, y_ref, gate_ref, g2_ref, lg_ref, lb_ref, o_ref):
    gates = gate_ref[...]
    acc = y_ref[0] * gates[:, 0:1]
    for k in range(1, TOP_K):
        acc = acc + y_ref[k] * gates[:, k:k + 1]
    v = ALPHA_DN * x_ref[...] + g2_ref[...] * acc
    o_ref[...] = _layer_norm(v, lg_ref[...], lb_ref[...])


def _combine(x, gathered, gates_nk, mods, layer, ln_g, ln_b, row_off, row_stride):
    B, T, D = x.shape
    tm = min(TOKEN_TILE, T)
    vec = pl.BlockSpec((1, D), lambda b, j: (0, 0))
    return pl.pallas_call(
        _combine_body,
        grid=(B, T // tm),
        in_specs=[pl.BlockSpec((None, tm, D), lambda b, j: (b, j, 0)),
                  pl.BlockSpec((TOP_K, None, tm, D), lambda b, j: (0, b, j, 0)),
                  pl.BlockSpec((None, tm, TOP_K), lambda b, j: (b, j, 0)),
                  _mod_spec(layer, 5, row_off, row_stride),
                  vec, vec],
        out_specs=pl.BlockSpec((None, tm, D), lambda b, j: (b, j, 0)),
        out_shape=jax.ShapeDtypeStruct((B, T, D), F32),
        compiler_params=_cparams(("arbitrary", "arbitrary")),
        name="moe_combine",
    )(x, gathered, gates_nk, mods, ln_g.reshape(1, D), ln_b.reshape(1, D))


def _moe_layer(x, mods, layer, w_router, b_router, w_gu, b_gu, w_dn, b_dn, ln_g, ln_b, row_off, row_stride):
    B, T, D = x.shape
    N = B * T
    te = EXPERT_TILE
    u2, topi, gates, rank, cnt = _router(x, mods, layer, w_router, b_router, row_off, row_stride)
    counts = cnt[:, 0].astype(I32)
    padded = (counts + te - 1) // te * te
    pad_end = jnp.cumsum(padded)
    pad_start = pad_end - padded
    e_ids = jnp.arange(N_EXP, dtype=I32)
    start_of = jnp.sum(jnp.where(topi[None] == e_ids[:, None, None], pad_start[:, None, None], 0), axis=0)
    dest = start_of + rank
    n_blk = (N * TOP_K) // te + N_EXP
    blk_pos = jnp.arange(n_blk, dtype=I32) * te
    blk_e = jnp.minimum(jnp.sum((pad_end[None, :] <= blk_pos[:, None]).astype(I32), axis=1), N_EXP - 1)
    n_used = (pad_end[-1] // te).astype(I32).reshape(1)
    tok = jnp.broadcast_to(jnp.arange(N, dtype=I32)[None, :], (TOP_K, N))
    slot_tok = jnp.zeros((n_blk * te,), I32).at[dest.reshape(-1)].set(
        tok.reshape(-1), mode='promise_in_bounds', unique_indices=True)
    xs = u2.reshape(N, D).at[slot_tok].get(mode='promise_in_bounds')
    ys = _experts(xs, blk_e, n_used, layer, w_gu, b_gu, w_dn, b_dn)
    gathered = ys.at[dest].get(mode='promise_in_bounds').reshape(TOP_K, B, T, D)
    return _combine(x, gathered, gates.T.reshape(B, T, TOP_K), mods, layer, ln_g, ln_b, row_off, row_stride)


def _inproj1_body(x_ref, sc_ref, sh_ref, w_ref, wg_ref, wgt_ref, bg_ref, bgt_ref,
                  qk_ref, v_ref, o_ref, gcol_ref, grow_ref):
    u = x_ref[...] * (1.0 + sc_ref[...]) + sh_ref[...]
    z = jnp.dot(u.astype(BF16), w_ref[...], preferred_element_type=F32)
    qk_ref[...] = z[:, :2 * D_MODEL]
    v_ref[...] = z[:, 2 * D_MODEL:3 * D_MODEL].astype(BF16)
    o_ref[...] = z[:, 3 * D_MODEL:]
    tm = u.shape[0]
    nh = 4 * H_C
    gc = jnp.dot(u, wg_ref[...], precision=HI, preferred_element_type=F32) + bg_ref[...]
    gr = lax.dot_general(wgt_ref[...], u, (((1,), (1,)), ((), ())), precision=HI,
                         preferred_element_type=F32) + bgt_ref[...]
    r_i = lax.broadcasted_iota(I32, (tm, tm), 0)
    c_i = lax.broadcasted_iota(I32, (tm, tm), 1)
    same = (r_i // CHUNK) == (c_i // CHUNK)
    tri_f = jnp.logical_and(same, c_i <= r_i).astype(F32)
    tri_b = jnp.logical_and(same, c_i >= r_i).astype(F32)
    col_i = lax.broadcasted_iota(I32, (tm, nh), 1)
    lf_c = jax.nn.log_sigmoid(gc)
    cum_c = jnp.where(col_i < 2 * H_C,
                      jnp.dot(tri_f, lf_c, precision=HI, preferred_element_type=F32),
                      jnp.dot(tri_b, lf_c, precision=HI, preferred_element_type=F32))
    is_f_col = (col_i // H_C) % 2 == 1
    gcol_ref[...] = jnp.where(is_f_col, cum_c, gc)
    row_i = lax.broadcasted_iota(I32, (nh, tm), 0)
    lf_r = jax.nn.log_sigmoid(gr)
    cum_r = jnp.where(row_i < 2 * H_C,
                      jnp.dot(lf_r, tri_b, precision=HI, preferred_element_type=F32),
                      jnp.dot(lf_r, tri_f, precision=HI, preferred_element_type=F32))
    is_f_row = (row_i // H_C) % 2 == 1
    grow_ref[...] = jnp.where(is_f_row, cum_r, gr)


def _inproj1(x, mods, w_in1, w_gate1, b_gate1, row_off, row_stride):
    B, T, D = x.shape
    tm = min(TOKEN_TILE, T)
    n_j = T // tm
    nh = 4 * H_C
    tok = lambda w: pl.BlockSpec((None, tm, w), lambda b, j: (b, j, 0))
    return pl.pallas_call(
        _inproj1_body,
        grid=(B, n_j),
        in_specs=[tok(D),
                  _mod_spec(1, 1, row_off, row_stride),
                  _mod_spec(1, 0, row_off, row_stride),
                  pl.BlockSpec((D, 4 * D), lambda b, j: (0, 0)),
                  pl.BlockSpec((D, nh), lambda b, j: (0, 0)),
                  pl.BlockSpec((nh, D), lambda b, j: (0, 0)),
                  pl.BlockSpec((1, nh), lambda b, j: (0, 0)),
                  pl.BlockSpec((nh, 1), lambda b, j: (0, 0))],
        out_specs=[tok(2 * D), tok(D), tok(D), tok(nh),
                   pl.BlockSpec((nh, tm), lambda b, j: (0, b * n_j + j))],
        out_shape=[jax.ShapeDtypeStruct((B, T, 2 * D), F32),
                   jax.ShapeDtypeStruct((B, T, D), BF16),
                   jax.ShapeDtypeStruct((B, T, D), F32),
                   jax.ShapeDtypeStruct((B, T, nh), F32),
                   jax.ShapeDtypeStruct((nh, B * T), F32)],
        compiler_params=_cparams(("arbitrary", "arbitrary")),
        name="inproj1",
    )(x, mods, mods, w_in1.astype(BF16), w_gate1, w_gate1.T, b_gate1.reshape(1, nh), b_gate1.reshape(nh, 1))


def _mlstm_body(q_ref, k_ref, v_ref, gcol_ref, grow_ref, cw_ref, cb_ref, c0_ref, n0_ref, m0_ref,
                h_ref, cfin_ref, nfin_ref, mfin_ref, xpad_s, q_s, k_s, c_s, n_s, m_s, *, T):
    hd = pl.program_id(1)
    n_c = T // CHUNK
    L = CHUNK

    rb = min(T, 512)
    pad = 8

    def conv_silu_into(src_ref, w, bias, dst_s, scale):
        xpad_s[0:pad] = jnp.zeros((pad, DH_C), F32)
        xpad_s[T + pad:T + 2 * pad] = jnp.zeros((pad, DH_C), F32)
        xpad_s[pad:T + pad] = src_ref[...]

        def blk(r, _):
            off = pl.multiple_of(r * rb, pad)
            win = xpad_s[pl.ds(off, rb + 2 * pad), :]
            acc = (pltpu.roll(win, 2, 0) * w[0:1] + pltpu.roll(win, 1, 0) * w[1:2] + win * w[2:3]
                   + pltpu.roll(win, rb + 2 * pad - 1, 0) * w[3:4])[pad:rb + pad] + bias
            dst_s[pl.ds(off, rb), :] = (jax.nn.silu(acc) * scale).astype(BF16)
            return 0

        lax.fori_loop(0, T // rb, blk, 0)

    conv_silu_into(q_ref, cw_ref[0], cb_ref[0], q_s, 1.0)
    conv_silu_into(k_ref, cw_ref[1], cb_ref[1], k_s, DH_C ** -0.5)

    t_i = lax.broadcasted_iota(I32, (L, L), 0)
    s_i = lax.broadcasted_iota(I32, (L, L), 1)

    for d in range(2):
        c_s[...] = c0_ref[d]
        n_s[...] = n0_ref[d]
        m_s[...] = m0_ref[d]
        mask = (s_i <= t_i) if d == 0 else (s_i >= t_i)
        last = L - 1 if d == 0 else 0

        def chunk(ci, _, d=d, mask=mask, last=last):
            c = ci if d == 0 else n_c - 1 - ci
            off = pl.multiple_of(c * L, L)
            qc = q_s[pl.ds(off, L), :]
            kc = k_s[pl.ds(off, L), :]
            vc = v_ref[pl.ds(off, L), :]
            gcol = gcol_ref[pl.ds(off, L), :]
            grow = grow_ref[:, pl.ds(off, L)]
            li_c = gcol[:, 2 * d:2 * d + 1]
            b_c = gcol[:, 2 * d + 1:2 * d + 2]
            li_r = grow[2 * d:2 * d + 1, :]
            b_r = grow[2 * d + 1:2 * d + 2, :]
            m_prev = m_s[...]
            log_d = jnp.where(mask, b_c - b_r + li_r, -jnp.inf)
            m_inter = b_c + m_prev
            m_t = jnp.maximum(m_inter, jnp.max(log_d, axis=-1, keepdims=True))
            dmat = jnp.exp(log_d - m_t)
            w_inter = jnp.exp(m_inter - m_t)
            c_b = c_s[...].astype(BF16)
            s = lax.dot_general(qc, kc, (((1,), (1,)), ((), ())), preferred_element_type=F32) * dmat
            inter = lax.dot_general(qc, c_b, (((1,), (1,)), ((), ())), preferred_element_type=F32)
            num = w_inter * inter + jnp.dot(s.astype(BF16), vc, preferred_element_type=F32)
            qn = jnp.sum(qc.astype(F32) * n_s[...], axis=-1, keepdims=True)
            den = w_inter * qn + jnp.sum(s, axis=-1, keepdims=True)
            h = num / jnp.maximum(jnp.abs(den), jnp.exp(-m_t))
            if d == 0:
                h_ref[pl.ds(off, L), :] = h
            else:
                h_ref[pl.ds(off, L), :] = h_ref[pl.ds(off, L), :] + h
            m_new = m_t[last:last + 1, :]
            b_last = b_c[last:last + 1, :]
            w_s = jnp.exp(b_last - b_c + li_c - m_new)
            decay = jnp.exp(b_last + m_prev - m_new)
            wv = (w_s * vc.astype(F32)).astype(BF16)
            kc32 = kc.astype(F32)
            c_s[...] = decay * c_s[...] + lax.dot_general(wv, kc, (((0,), (0,)), ((), ())),
                                                          preferred_element_type=F32)
            n_s[...] = decay * n_s[...] + jnp.sum(w_s * kc32, axis=0, keepdims=True)
            m_s[...] = m_new
            return 0

        lax.fori_loop(0, n_c, chunk, 0)
        cfin_ref[d] = c_s[...]
        nfin_ref[d] = n_s[...]
        mfin_ref[d] = m_s[...]


def _mlstm(qk, v, gcol, grow, conv_w, conv_b, c0, n0, m0):
    B, T, _ = v.shape
    DH = DH_C
    cw = conv_w.reshape(CONV_W, 2, H_C, DH).transpose(2, 1, 0, 3)
    cb = conv_b.reshape(2, H_C, 1, DH).transpose(1, 0, 2, 3)
    st = lambda *tail: pl.BlockSpec((None, 2, None) + tail, lambda b, h: (b, 0, h) + (0,) * len(tail))
    return pl.pallas_call(
        functools.partial(_mlstm_body, T=T),
        grid=(B, H_C),
        in_specs=[pl.BlockSpec((None, T, DH), lambda b, h: (b, 0, h)),
                  pl.BlockSpec((None, T, DH), lambda b, h: (b, 0, H_C + h)),
                  pl.BlockSpec((None, T, DH), lambda b, h: (b, 0, h)),
                  pl.BlockSpec((None, None, T, 4), lambda b, h: (b, h, 0, 0)),
                  pl.BlockSpec((None, None, 4, T), lambda b, h: (b, h, 0, 0)),
                  pl.BlockSpec((None, 2, CONV_W, DH), lambda b, h: (h, 0, 0, 0)),
                  pl.BlockSpec((None, 2, 1, DH), lambda b, h: (h, 0, 0, 0)),
                  st(DH, DH), st(1, DH), st(1, 1)],
        out_specs=[pl.BlockSpec((None, T, DH), lambda b, h: (b, 0, h)),
                   st(DH, DH), st(1, DH), st(1, 1)],
        out_shape=[jax.ShapeDtypeStruct((B, T, D_MODEL), F32),
                   jax.ShapeDtypeStruct((B, 2, H_C, DH, DH), F32),
                   jax.ShapeDtypeStruct((B, 2, H_C, 1, DH), F32),
                   jax.ShapeDtypeStruct((B, 2, H_C, 1, 1), F32)],
        scratch_shapes=[pltpu.VMEM((T + 16, DH), F32), pltpu.VMEM((T, DH), BF16), pltpu.VMEM((T, DH), BF16),
                        pltpu.VMEM((DH, DH), F32), pltpu.VMEM((1, DH), F32), pltpu.VMEM((1, 1), F32)],
        compiler_params=_cparams(("arbitrary", "arbitrary")),
        name="mlstm",
    )(qk, qk, v, gcol, grow, cw, cb, c0, n0, m0)


def _outproj1_body(x_ref, h_ref, o_ref, g1_ref, w_ref, lg_ref, lb_ref, out_ref):
    y = jax.nn.sigmoid(o_ref[...]) * h_ref[...]
    out = jnp.dot(y.astype(BF16), w_ref[...], preferred_element_type=F32)
    v = ALPHA_DN * x_ref[...] + g1_ref[...] * out
    out_ref[...] = _layer_norm(v, lg_ref[...], lb_ref[...])


def _outproj1(x, h, o, mods, w_out1, ln_g, ln_b, row_off, row_stride):
    B, T, D = x.shape
    tm = min(TOKEN_TILE, T)
    tok = pl.BlockSpec((None, tm, D), lambda b, j: (b, j, 0))
    vec = pl.BlockSpec((1, D), lambda b, j: (0, 0))
    return pl.pallas_call(
        _outproj1_body,
        grid=(B, T // tm),
        in_specs=[tok, tok, tok, _mod_spec(1, 2, row_off, row_stride),
                  pl.BlockSpec((D, D), lambda b, j: (0, 0)), vec, vec],
        out_specs=tok,
        out_shape=jax.ShapeDtypeStruct((B, T, D), F32),
        compiler_params=_cparams(("arbitrary", "arbitrary")),
        name="outproj1",
    )(x, h, o, mods, w_out1.astype(BF16), ln_g.reshape(1, D), ln_b.reshape(1, D))


def _to_col_major(x):
    B, T, C = x.shape
    rows = T // GRID_W
    return x.reshape(B, rows, GRID_W, C).transpose(0, 2, 1, 3).reshape(B, T, C)


def _to_row_major(x):
    B, T, C = x.shape
    rows = T // GRID_W
    return x.reshape(B, GRID_W, rows, C).transpose(0, 2, 1, 3).reshape(B, T, C)


def _trunk(x, mods, st, P, s5_mats, grid):
    h_rg, s_re, s_im, m_c, m_n, m_m = st
    B, T, D = x.shape
    row_off, row_stride = 0, 1
    xa, ga, ub = _inproj0(x, mods, P['w_in0'][0].astype(BF16))
    h, h_fin = _rglru(xa, P['conv_a_w'][0], P['conv_a_b'][0], P['rg_wa'][0], P['rg_ba'][0],
                      P['rg_wi'][0], P['rg_bi'][0], P['rg_lam'][0], h_rg[:, 0].transpose(1, 0, 2))
    y5, s_fin = _s5(ub, s5_mats, _s5_state_to_lanes(s_re[:, 0], s_im[:, 0]))
    x = _outproj0(x, h, ga, y5, ub, mods, P['s5_d'][0], P['glu_w'][0], P['glu_b'][0], P['w_out0'][0],
                  P['ln_g'][0, 0], P['ln_b'][0, 0])
    x = _moe_layer(x, mods, 0, P['w_router'][0], P['b_router'][0], P['w_gu'], P['b_gu'],
                   P['w_down'], P['b_down'], P['ln_g'][0, 1], P['ln_b'][0, 1], row_off, row_stride)
    if grid:
        x = _to_col_major(x)
    qk, v, o, gcol, grow = _inproj1(x, mods, P['w_in1'][0], P['w_gate1'][0], P['b_gate1'][0], row_off, row_stride)
    gcol = gcol.reshape(B, T, 2, 2, H_C).transpose(0, 4, 1, 2, 3).reshape(B, H_C, T, 4)
    grow = grow.reshape(2, 2, H_C, B, T).transpose(3, 2, 0, 1, 4).reshape(B, H_C, 4, T)
    h, c_fin, n_fin, m_fin = _mlstm(qk, v, gcol, grow, P['conv_c_w'][0], P['conv_c_b'][0],
                                    m_c[:, 0], m_n[:, 0][:, :, :, None, :], m_m[:, 0][:, :, :, None, None])
    x = _outproj1(x, h, o, mods, P['w_out1'][0], P['ln_g'][1, 0], P['ln_b'][1, 0], row_off, row_stride)
    x = _moe_layer(x, mods, 1, P['w_router'][1], P['b_router'][1], P['w_gu'], P['b_gu'],
                   P['w_down'], P['b_down'], P['ln_g'][1, 1], P['ln_b'][1, 1], row_off, row_stride)
    if grid:
        x = _to_row_major(x)
    new_re, new_im = _s5_lanes_to_state(s_fin)
    new = (h_fin.transpose(1, 0, 2)[:, None], new_re[:, None], new_im[:, None],
           c_fin[:, None], n_fin[:, None, :, :, 0, :], m_fin[:, None, :, :, 0, 0])
    return x, new


def _forward(x_prompt, x_sample, c, c_ctx, states, P):
    bp = x_prompt.shape[0]
    bs = x_sample.shape[0]
    rows = 1 + bs
    rpad = -(-rows // 8) * 8
    cv = jnp.concatenate([c_ctx[None, :], c, jnp.zeros((rpad - rows, D_MODEL), F32)], axis=0)
    mods = _modulation(cv, P['w_mod'], P['b_mod'])
    s5_mats = _s5_matrices(P['s5_a_re'][0], P['s5_a_im'][0], P['s5_log_dt'][0], P['s5_b_re'][0], P['s5_b_im'][0],
                           P['s5_c_re'][0], P['s5_c_im'][0])
    zero_state = (jnp.zeros((bp, 1, 2, D_A), F32),
                  jnp.zeros((bp, 1, 2, G_B, P_B), F32),
                  jnp.zeros((bp, 1, 2, G_B, P_B), F32),
                  jnp.zeros((bp, 1, 2, H_C, DH_C, DH_C), F32),
                  jnp.zeros((bp, 1, 2, H_C, DH_C), F32),
                  jnp.zeros((bp, 1, 2, H_C), F32))
    mods_ctx = jnp.broadcast_to(mods[:, :, 0:1], mods.shape[:2] + (bp,) + mods.shape[3:])
    mods_lat = mods[:, :, 1:1 + bs]
    y_prompt, new = _trunk(x_prompt, mods_ctx, zero_state, P, s5_mats, False)
    y_sample, _ = _trunk(x_sample, mods_lat, states, P, s5_mats, True)
    return (y_prompt, y_sample) + tuple(new)


def kernel(x_prompt, x_sample, c, c_ctx, state_rglru, state_s5_re, state_s5_im, state_mlstm_C, state_mlstm_n, state_mlstm_m, w_mod, b_mod, ln_g, ln_b, w_in0, conv_a_w, conv_a_b, rg_wa, rg_ba, rg_wi, rg_bi, rg_lam, s5_a_re, s5_a_im, s5_log_dt, s5_b_re, s5_b_im, s5_c_re, s5_c_im, s5_d, glu_w, glu_b, w_out0, w_in1, w_gate1, b_gate1, conv_c_w, conv_c_b, w_out1, w_router, b_router, w_gu, b_gu, w_down, b_down):
    P = dict(w_mod=w_mod, b_mod=b_mod, ln_g=ln_g, ln_b=ln_b, w_in0=w_in0, conv_a_w=conv_a_w,
             conv_a_b=conv_a_b, rg_wa=rg_wa, rg_ba=rg_ba, rg_wi=rg_wi, rg_bi=rg_bi, rg_lam=rg_lam,
             s5_a_re=s5_a_re, s5_a_im=s5_a_im, s5_log_dt=s5_log_dt, s5_b_re=s5_b_re, s5_b_im=s5_b_im,
             s5_c_re=s5_c_re, s5_c_im=s5_c_im, s5_d=s5_d, glu_w=glu_w, glu_b=glu_b, w_out0=w_out0,
             w_in1=w_in1, w_gate1=w_gate1, b_gate1=b_gate1, conv_c_w=conv_c_w, conv_c_b=conv_c_b,
             w_out1=w_out1, w_router=w_router, b_router=b_router, w_gu=w_gu, b_gu=b_gu,
             w_down=w_down, b_down=b_down)
    states = (state_rglru, state_s5_re, state_s5_im, state_mlstm_C, state_mlstm_n, state_mlstm_m)
    return _forward(x_prompt, x_sample, c, c_ctx, states, P)
```

```python
import functools
import math

import jax
import jax.numpy as jnp
from jax import lax
from jax.experimental import pallas as pl
from jax.experimental.pallas import tpu as pltpu

F32 = jnp.float32
BF16 = jnp.bfloat16
I32 = jnp.int32
HI = lax.Precision.HIGHEST

D_MODEL = 1024
DEPTH = 2
GRID_W = 64
D_A = 512
NB_A = 8
BS_A = D_A // NB_A
CONV_W = 4
LRU_C = 8.0
D_B = 512
S5_GROUP = 16
G_B = D_B // S5_GROUP
P_B = 64
H_C = 4
DH_C = D_MODEL // H_C
CHUNK = 128
N_EXP = 32
TOP_K = 4
D_FF = D_MODEL
SWIGLU_LIMIT = 7.0
SWIGLU_ALPHA = 1.702
ALPHA_DN = (2 * DEPTH) ** 0.25
LN_EPS = 1e-5

S5_L = 16
S5_W = S5_L * S5_GROUP
S5_OCT = 8
TOKEN_TILE = 256
EXPERT_TILE = 256
VMEM_LIMIT = 56 * 1024 * 1024


def _cparams(sem, vmem=VMEM_LIMIT):
    return pltpu.CompilerParams(dimension_semantics=sem, vmem_limit_bytes=vmem)


def _layer_norm(v, g, b):
    mu = jnp.mean(v, axis=-1, keepdims=True)
    c = v - mu
    var = jnp.mean(c * c, axis=-1, keepdims=True)
    return c * lax.rsqrt(var + LN_EPS) * g + b


def _mod_spec(layer, which, row_off, row_stride):
    return pl.BlockSpec((None, None, None, 1, D_MODEL),
                        lambda b, j: (layer, which, row_off + b * row_stride, 0, 0))


def _mod_spec_all(layer, which, nb):
    return pl.BlockSpec((None, None, nb, 1, D_MODEL), lambda i: (layer, which, 0, 0, 0))


def _row_permutation(n_out_major, n_out_minor):
    n = n_out_major * n_out_minor
    r_out = lax.broadcasted_iota(I32, (n, n), 0)
    r_in = lax.broadcasted_iota(I32, (n, n), 1)
    hit = jnp.logical_and(r_out // n_out_minor == r_in % n_out_major, r_out % n_out_minor == r_in // n_out_major)
    return jnp.where(hit, 1.0, 0.0).astype(BF16)


def _mod_body(c_ref, w_ref, b_ref, o_ref):
    s = jax.nn.silu(c_ref[...])
    o_ref[...] = jnp.dot(s, w_ref[...], precision=HI, preferred_element_type=F32) + b_ref[...]


def _modulation(cv, w_mod, b_mod):
    R, D = cv.shape
    L, _, N6 = w_mod.shape
    tn = N6 // 6
    out = pl.pallas_call(
        _mod_body,
        grid=(L, N6 // tn),
        in_specs=[pl.BlockSpec((R, D), lambda l, j: (0, 0)),
                  pl.BlockSpec((None, D, tn), lambda l, j: (l, 0, j)),
                  pl.BlockSpec((None, 1, tn), lambda l, j: (l, 0, j))],
        out_specs=pl.BlockSpec((None, R, tn), lambda l, j: (l, 0, j)),
        out_shape=jax.ShapeDtypeStruct((L, R, N6), F32),
        compiler_params=_cparams(("arbitrary", "arbitrary")),
        name="modulation",
    )(cv, w_mod, b_mod.reshape(L, 1, N6))
    return out.reshape(L, R, 6, D).transpose(0, 2, 1, 3)[:, :, :, None, :]


def _inproj0_body(x_ref, sc_ref, sh_ref, w_ref, xa_ref, ga_ref, ub_ref):
    nb, tq, d = x_ref.shape
    u = x_ref[...] * (1.0 + sc_ref[...]) + sh_ref[...]
    ub = u.reshape(nb * tq, d).astype(BF16)
    ut = jnp.dot(_row_permutation(tq, nb), ub, preferred_element_type=F32).astype(BF16)
    z = jnp.dot(ut, w_ref[...], preferred_element_type=F32).reshape(tq, nb, w_ref.shape[1])
    xa_ref[...] = z[:, :, :D_A]
    ga_ref[...] = z[:, :, D_A:2 * D_A]
    ub_ref[...] = z[:, :, 2 * D_A:]


def _inproj0(x, pmods, w_in0):
    B, T, D = x.shape
    tq = TOKEN_TILE // B
    nz = w_in0.shape[1]
    tmaj = pl.BlockSpec((tq, B, D_A), lambda i: (i, 0, 0))
    return pl.pallas_call(
        _inproj0_body,
        grid=(T // tq,),
        in_specs=[pl.BlockSpec((B, tq, D), lambda i: (0, i, 0)),
                  _mod_spec_all(0, 1, B), _mod_spec_all(0, 0, B),
                  pl.BlockSpec((D, nz), lambda i: (0, 0))],
        out_specs=[tmaj, tmaj, tmaj],
        out_shape=[jax.ShapeDtypeStruct((T, B, D_A), F32)] * 3,
        compiler_params=_cparams(("arbitrary",)),
        name="inproj0",
    )(x, pmods, pmods, w_in0)


def _rglru_body(cur_ref, prev_ref, next_ref, cw_ref, cb_ref, wa_ref, wi_ref, ba_ref, bi_ref, lam_ref, h0_ref,
                h_ref, hfin_ref, ext_s, a_s, b_s, carry_s, *, tt, n_t):
    d = pl.program_id(0)
    j = pl.program_id(1)
    jj = j + d * (n_t - 1 - 2 * j)
    nb = cur_ref.shape[1]
    half = D_A // 2
    ext_s[0:2] = jnp.where(jj == 0, 0.0, prev_ref[...])
    ext_s[2:tt + 2] = cur_ref[...]
    ext_s[tt + 2:tt + 3] = jnp.where(jj == n_t - 1, 0.0, next_ref[...])
    xc = ext_s[0:tt] * cw_ref[0]
    for k in range(1, CONV_W):
        xc = xc + ext_s[k:k + tt] * cw_ref[k]
    xc = xc + cb_ref[...]
    x2 = xc.reshape(tt * nb, D_A)
    xb = x2.astype(BF16)

    def gate(w_ref, bias_ref):
        lo = jnp.dot(xb[:, :half], w_ref[0], preferred_element_type=F32)
        hi = jnp.dot(xb[:, half:], w_ref[1], preferred_element_type=F32)
        return jax.nn.sigmoid(jnp.concatenate([lo, hi], axis=1) + bias_ref[...])

    r = gate(wa_ref, ba_ref)
    i = gate(wi_ref, bi_ref)
    log_a = LRU_C * r * jax.nn.log_sigmoid(lam_ref[...])
    a = jnp.exp(log_a)
    one_minus_a2 = -jnp.tanh(log_a) * (a * a + 1.0)
    a_s[...] = a.reshape(tt, nb, D_A)
    b_s[...] = (jnp.sqrt(one_minus_a2) * (i * x2)).reshape(tt, nb, D_A)

    @pl.when(j == 0)
    def _():
        carry_s[...] = h0_ref[...]

    def step(t, h):
        h = a_s[t] * h + b_s[t]
        h_ref[t] = h
        return h

    @pl.when(d == 0)
    def _():
        carry_s[...] = lax.fori_loop(0, tt, step, carry_s[...], unroll=8)

    @pl.when(d == 1)
    def _():
        carry_s[...] = lax.fori_loop(0, tt, lambda t, h: step(tt - 1 - t, h), carry_s[...], unroll=8)

    hfin_ref[...] = carry_s[...]


def _block_diag_halves(w):
    nd = w.shape[0]
    per = NB_A // 2
    w = w.reshape(nd, 2, per, BS_A, BS_A)
    eye = jnp.eye(per, dtype=w.dtype)
    out = jnp.einsum('dhncz,nm->dhncmz', w, eye)
    return out.reshape(nd, 2, per * BS_A, per * BS_A)


def _rglru(x3, conv_w, conv_b, wa, ba, wi, bi, lam, h0):
    T, B, _ = x3.shape
    tt = min(T, 1024 // B)
    n_t = T // tt

    def cur_map(d, j):
        return (j + d * (n_t - 1 - 2 * j), 0, 0)

    def prev_map(d, j):
        jj = j + d * (n_t - 1 - 2 * j)
        return (jnp.maximum(jj * (tt // 2) - 1, 0), 0, 0)

    def next_map(d, j):
        jj = j + d * (n_t - 1 - 2 * j)
        return (jnp.minimum((jj + 1) * tt, T - 1), 0, 0)

    dir_spec3 = pl.BlockSpec((None, 1, D_A), lambda d, j: (d, 0, 0))
    h, hfin = pl.pallas_call(
        functools.partial(_rglru_body, tt=tt, n_t=n_t),
        grid=(2, n_t),
        in_specs=[pl.BlockSpec((tt, B, D_A), cur_map),
                  pl.BlockSpec((2, B, D_A), prev_map),
                  pl.BlockSpec((1, B, D_A), next_map),
                  pl.BlockSpec((CONV_W, 1, D_A), lambda d, j: (0, 0, 0)),
                  pl.BlockSpec((1, 1, D_A), lambda d, j: (0, 0, 0)),
                  pl.BlockSpec((None, 2, D_A // 2, D_A // 2), lambda d, j: (d, 0, 0, 0)),
                  pl.BlockSpec((None, 2, D_A // 2, D_A // 2), lambda d, j: (d, 0, 0, 0)),
                  dir_spec3, dir_spec3, dir_spec3,
                  pl.BlockSpec((None, B, D_A), lambda d, j: (d, 0, 0))],
        out_specs=[pl.BlockSpec((None, tt, B, D_A), lambda d, j: (d,) + cur_map(d, j)),
                   pl.BlockSpec((None, B, D_A), lambda d, j: (d, 0, 0))],
        out_shape=[jax.ShapeDtypeStruct((2, T, B, D_A), F32),
                   jax.ShapeDtypeStruct((2, B, D_A), F32)],
        scratch_shapes=[pltpu.VMEM((tt + 3, B, D_A), F32),
                        pltpu.VMEM((tt, B, D_A), F32),
                        pltpu.VMEM((tt, B, D_A), F32),
                        pltpu.VMEM((B, D_A), F32)],
        compiler_params=_cparams(("arbitrary", "arbitrary")),
        name="rglru",
    )(x3, x3, x3, conv_w.reshape(CONV_W, 1, D_A), conv_b.reshape(1, 1, D_A),
      _block_diag_halves(wa).astype(BF16), _block_diag_halves(wi).astype(BF16),
      ba.reshape(2, 1, D_A), bi.reshape(2, 1, D_A), lam.reshape(2, 1, D_A), h0)
    return h, hfin


def _s5_matrices(a_re, a_im, log_dt, b_re, b_im, c_re, c_im):
    L = S5_L
    lam = lax.complex(a_re.astype(F32), a_im.astype(F32))
    dt = jnp.exp(log_dt.astype(F32))[..., None]
    ldt = lam * dt
    a_bar = jnp.exp(ldt)
    b_bar = ((a_bar - 1.0) / lam)[..., None] * lax.complex(b_re.astype(F32), b_im.astype(F32))
    cc = lax.complex(c_re.astype(F32), c_im.astype(F32))
    ks = jnp.arange(L + 1, dtype=F32)
    pw = jnp.exp(ldt[:, :, None, :] * ks[None, None, :, None].astype(jnp.complex64))
    kern = jnp.real(jnp.einsum('dgjp,dgkp,dgpi->dgkji', cc, pw[:, :, :L], b_bar, precision=HI))
    s_idx = jnp.arange(L)[:, None]
    t_idx = jnp.arange(L)[None, :]
    lag_f = jnp.clip(t_idx - s_idx, 0, L - 1)
    lag_b = jnp.clip(s_idx - t_idx, 0, L - 1)
    m_f = jnp.where((t_idx >= s_idx)[None, :, :, None, None], kern[0][:, lag_f], 0.0)
    m_b = jnp.where((s_idx >= t_idx)[None, :, :, None, None], kern[1][:, lag_b], 0.0)
    m = (m_f + m_b).transpose(0, 1, 4, 2, 3).reshape(G_B, S5_W, S5_W)
    down = (L - ks[:L])[None, :, None].astype(jnp.complex64)
    pw_down_b = jnp.exp(ldt[1][:, None, :] * down)
    g_f = jnp.exp(ldt[0][:, None, :] * (down - 1.0))[..., None] * b_bar[0][:, None]
    g_b = pw[1][:, :L, :, None] * b_bar[1][:, None]

    def g_cols(x):
        return x.transpose(0, 1, 3, 2).reshape(G_B, S5_W, P_B)

    gs = jnp.concatenate([g_cols(jnp.real(g_f)), g_cols(jnp.real(g_b)),
                          g_cols(jnp.imag(g_f)), g_cols(jnp.imag(g_b))], axis=-1)
    e_f = cc[0][:, None] * pw[0][:, 1:, None, :]
    e_b = cc[1][:, None] * pw_down_b[:, :, None, :]

    def e_rows(x):
        return x.transpose(0, 3, 1, 2).reshape(G_B, P_B, S5_W)

    e = jnp.concatenate([e_rows(jnp.real(e_f)), e_rows(jnp.real(e_b)),
                         -e_rows(jnp.imag(e_f)), -e_rows(jnp.imag(e_b))], axis=1)
    a_l = pw[:, :, L]
    al = jnp.concatenate([jnp.real(a_l[0]), jnp.real(a_l[1]), jnp.imag(a_l[0]), jnp.imag(a_l[1])], axis=-1)
    no = G_B // S5_OCT
    ow = S5_OCT * S5_W
    rows_sgi = lambda x: x.reshape(no, S5_OCT, L, S5_GROUP, S5_W).transpose(0, 2, 1, 3, 4).reshape(no, ow, S5_W)
    src = jnp.stack([rows_sgi(m), rows_sgi(gs), e.reshape(no, ow, S5_W)]).astype(BF16)
    r = jnp.arange(ow, dtype=I32)
    c = jnp.arange(S5_W, dtype=I32)
    grp_sgi = (r // S5_GROUP) % S5_OCT
    grp_gl = r // S5_W
    src_sgi = (r // (S5_OCT * S5_GROUP)) * S5_GROUP + r % S5_GROUP
    src_gl = r % S5_W
    spread = jnp.stack([src_sgi, src_gl, src_sgi])[:, None, :] == c[None, :, None]
    row_grp = jnp.stack([grp_sgi, grp_sgi, grp_gl])[:, :, None]
    col_grp = jnp.stack([grp_sgi, grp_gl, grp_sgi])[:, None, :]
    tr = 512
    out = pl.pallas_call(
        _s5_expand_body,
        grid=(3, no, ow // tr),
        in_specs=[pl.BlockSpec((None, None, tr, S5_W), lambda k, q, i: (k, q, i, 0)),
                  pl.BlockSpec((None, S5_W, ow), lambda k, q, i: (k, 0, 0)),
                  pl.BlockSpec((None, tr, 1), lambda k, q, i: (k, i, 0)),
                  pl.BlockSpec((None, 1, ow), lambda k, q, i: (k, 0, 0))],
        out_specs=pl.BlockSpec((None, None, tr, ow), lambda k, q, i: (k, q, i, 0)),
        out_shape=jax.ShapeDtypeStruct((3, no, ow, ow), BF16),
        compiler_params=_cparams(("arbitrary", "arbitrary", "arbitrary")),
        name="s5_expand",
    )(src, spread.astype(BF16), row_grp, col_grp)
    return out, al.reshape(1, G_B * S5_W)


def _s5_expand_body(src_ref, spread_ref, rg_ref, cg_ref, o_ref):
    wide = jnp.dot(src_ref[...], spread_ref[...], preferred_element_type=F32)
    o_ref[...] = jnp.where(rg_ref[...] == cg_ref[...], wide, 0.0).astype(BF16)


def _s5_fill_lhs(u_ref, lhs_s):
    tc, _, nb, lanes = u_ref.shape
    for s in range(S5_L):
        lhs_s[:, s * lanes:(s + 1) * lanes] = u_ref[:, s].reshape(tc * nb, lanes).astype(BF16)


def _s5_state_body(u_ref, g_ref, f_ref, lhs_s):
    _s5_fill_lhs(u_ref, lhs_s)
    f_ref[...] = jnp.dot(lhs_s[...], g_ref[...], preferred_element_type=F32).reshape(f_ref.shape)


def _s5_scan_body(f_ref, a_ref, s0_ref, sin_ref, sfin_ref, *, n_c):
    nb, width = s0_ref.shape
    hw = 2 * P_B
    n_g = width // S5_W
    a = a_ref[...]
    a_re = [a[:, k * S5_W:k * S5_W + hw] for k in range(n_g)]
    a_im = [a[:, k * S5_W + hw:(k + 1) * S5_W] for k in range(n_g)]
    is_fwd = (lax.broadcasted_iota(I32, (nb, width), 1) % hw) < P_B

    def split(x):
        return tuple(x[:, k * hw:(k + 1) * hw] for k in range(2 * n_g))

    def merge(parts):
        return jnp.concatenate(parts, axis=-1)

    def advance(c, parts):
        f = split(f_ref[c])
        out = []
        for k in range(n_g):
            s_re, s_im = parts[2 * k], parts[2 * k + 1]
            out.append(a_re[k] * s_re - a_im[k] * s_im + f[2 * k])
            out.append(a_re[k] * s_im + a_im[k] * s_re + f[2 * k + 1])
        return tuple(out)

    def fwd(c, parts):
        sin_ref[c] = merge(parts)
        return advance(c, parts)

    init = split(s0_ref[...])
    fin_f = lax.fori_loop(0, n_c, fwd, init)

    def bwd(k, parts):
        c = n_c - 1 - k
        sin_ref[c] = jnp.where(is_fwd, sin_ref[c], merge(parts))
        return advance(c, parts)

    fin_b = lax.fori_loop(0, n_c, bwd, init)
    sfin_ref[...] = jnp.where(is_fwd, merge(fin_f), merge(fin_b))


def _s5_out_body(u_ref, sin_ref, m_ref, e_ref, y_ref, lhs_s):
    _s5_fill_lhs(u_ref, lhs_s)
    tc, _, nb, lanes = u_ref.shape
    sin = sin_ref[...].reshape(tc * nb, sin_ref.shape[-1]).astype(BF16)
    y = (jnp.dot(lhs_s[...], m_ref[...], preferred_element_type=F32)
         + jnp.dot(sin, e_ref[...], preferred_element_type=F32))
    for s in range(S5_L):
        y_ref[:, s] = y[:, s * lanes:(s + 1) * lanes].reshape(tc, nb, lanes)


def _s5(ub, mats, s0):
    mge, al = mats
    T, B, _ = ub.shape
    n_c = T // S5_L
    no = G_B // S5_OCT
    lanes = S5_OCT * S5_GROUP
    ow = S5_OCT * S5_W
    tc = min(n_c, TOKEN_TILE // B)
    u4 = ub.reshape(n_c, S5_L, B, D_B)
    u_spec = pl.BlockSpec((tc, S5_L, B, lanes), lambda q, i: (i, 0, 0, q))
    w_spec = lambda kind: pl.BlockSpec((None, None, ow, ow), lambda q, i: (kind, q, 0, 0))
    st_spec = pl.BlockSpec((tc, B, ow), lambda q, i: (i, 0, q))
    f_loc = pl.pallas_call(
        _s5_state_body,
        grid=(no, n_c // tc),
        in_specs=[u_spec, w_spec(1)],
        out_specs=st_spec,
        out_shape=jax.ShapeDtypeStruct((n_c, B, G_B * S5_W), F32),
        scratch_shapes=[pltpu.VMEM((tc * B, S5_L * lanes), BF16)],
        compiler_params=_cparams(("arbitrary", "arbitrary")),
        name="s5_state",
    )(u4, mge)
    sw = 4 * S5_W
    sin, sfin = pl.pallas_call(
        functools.partial(_s5_scan_body, n_c=n_c),
        grid=(G_B * S5_W // sw,),
        in_specs=[pl.BlockSpec((n_c, B, sw), lambda g: (0, 0, g)),
                  pl.BlockSpec((1, sw), lambda g: (0, g)),
                  pl.BlockSpec((B, sw), lambda g: (0, g))],
        out_specs=[pl.BlockSpec((n_c, B, sw), lambda g: (0, 0, g)),
                   pl.BlockSpec((B, sw), lambda g: (0, g))],
        out_shape=[jax.ShapeDtypeStruct((n_c, B, G_B * S5_W), F32),
                   jax.ShapeDtypeStruct((B, G_B * S5_W), F32)],
        compiler_params=_cparams(("arbitrary",)),
        name="s5_scan",
    )(f_loc, al, s0)
    y = pl.pallas_call(
        _s5_out_body,
        grid=(no, n_c // tc),
        in_specs=[u_spec, st_spec, w_spec(0), w_spec(2)],
        out_specs=u_spec,
        out_shape=jax.ShapeDtypeStruct((n_c, S5_L, B, D_B), F32),
        scratch_shapes=[pltpu.VMEM((tc * B, S5_L * lanes), BF16)],
        compiler_params=_cparams(("arbitrary", "arbitrary")),
        name="s5_out",
    )(u4, sin, mge, mge)
    return y.reshape(T, B, D_B), sfin


def _s5_state_to_lanes(s_re, s_im):
    parts = [s_re[:, 0], s_re[:, 1], s_im[:, 0], s_im[:, 1]]
    return jnp.concatenate(parts, axis=-1).reshape(s_re.shape[0], G_B * S5_W)


def _s5_lanes_to_state(s):
    s = s.reshape(s.shape[0], G_B, 4, P_B)
    return jnp.stack([s[:, :, 0], s[:, :, 1]], axis=1), jnp.stack([s[:, :, 2], s[:, :, 3]], axis=1)


def _outproj0_body(x_ref, h_ref, ga_ref, y_ref, ub_ref, g1_ref, d_ref, gw_ref, gb_ref, wo_ref,
                   lg_ref, lb_ref, o_ref):
    nb, tq, d = x_ref.shape
    rows = tq * nb

    def flat(v):
        return v.reshape(rows, v.shape[-1])

    ya = flat(h_ref[0] + h_ref[1]) * jax.nn.gelu(flat(ga_ref[...]))
    yb = flat(y_ref[...]) + d_ref[...] * flat(ub_ref[...])
    g = jax.nn.gelu(yb)
    gate = jax.nn.sigmoid(jnp.dot(g.astype(BF16), gw_ref[...], preferred_element_type=F32) + gb_ref[...])
    cat = jnp.concatenate([ya, g * gate], axis=1).astype(BF16)
    cat = jnp.dot(_row_permutation(nb, tq), cat, preferred_element_type=F32).astype(BF16)
    out = jnp.dot(cat, wo_ref[...], preferred_element_type=F32).reshape(nb, tq, d)
    v = ALPHA_DN * x_ref[...] + g1_ref[...] * out
    o_ref[...] = _layer_norm(v, lg_ref[...], lb_ref[...])


def _outproj0(x, h, ga, y5, ub, pmods, s5_d, glu_w, glu_b, w_out0, ln_g, ln_b):
    B, T, D = x.shape
    tq = TOKEN_TILE // B
    tmaj = pl.BlockSpec((tq, B, D_A), lambda i: (i, 0, 0))
    vec = lambda w: pl.BlockSpec((1, w), lambda i: (0, 0))
    return pl.pallas_call(
        _outproj0_body,
        grid=(T // tq,),
        in_specs=[pl.BlockSpec((B, tq, D), lambda i: (0, i, 0)),
                  pl.BlockSpec((2, tq, B, D_A), lambda i: (0, i, 0, 0)),
                  tmaj, tmaj, tmaj,
                  _mod_spec_all(0, 2, B),
                  vec(D_B),
                  pl.BlockSpec((D_B, D_B), lambda i: (0, 0)),
                  vec(D_B),
                  pl.BlockSpec((D, D), lambda i: (0, 0)),
                  vec(D), vec(D)],
        out_specs=pl.BlockSpec((B, tq, D), lambda i: (0, i, 0)),
        out_shape=jax.ShapeDtypeStruct((B, T, D), F32),
        compiler_params=_cparams(("arbitrary",)),
        name="outproj0",
    )(x, h, ga, y5, ub, pmods, s5_d.reshape(1, D_B), glu_w.astype(BF16), glu_b.reshape(1, D_B),
      w_out0.astype(BF16), ln_g.reshape(1, D), ln_b.reshape(1, D))


def _router_body(x_ref, sc_ref, sh_ref, wr_ref, br_ref, u_ref, topi_ref, gate_ref, rank_ref, cnt_ref, cnt_s):
    first = jnp.logical_and(pl.program_id(0) == 0, pl.program_id(1) == 0)

    @pl.when(first)
    def _():
        cnt_s[...] = jnp.zeros_like(cnt_s)

    u = x_ref[...] * (1.0 + sc_ref[...]) + sh_ref[...]
    u_ref[...] = u.astype(BF16)
    tm = u.shape[0]
    logits = lax.dot_general(wr_ref[...], u, (((1,), (1,)), ((), ())), precision=HI,
                             preferred_element_type=F32) + br_ref[...]
    e_iota = lax.broadcasted_iota(I32, logits.shape, 0)
    work = logits
    vals, idxs, hots = [], [], []
    for _ in range(TOP_K):
        m = jnp.max(work, axis=0, keepdims=True)
        idx = jnp.min(jnp.where(work == m, e_iota, N_EXP), axis=0, keepdims=True)
        hot = e_iota == idx
        vals.append(m)
        idxs.append(idx)
        hots.append(hot)
        work = jnp.where(hot, -jnp.inf, work)
    ex = [jnp.exp(v - vals[0]) for v in vals]
    den = ex[0] + ex[1] + ex[2] + ex[3]
    gate_ref[...] = jnp.concatenate([e / den for e in ex], axis=0)
    topi_ref[...] = jnp.concatenate(idxs, axis=0)
    hot_sum = jnp.zeros(logits.shape, F32)
    for hot in hots:
        hot_sum = hot_sum + hot.astype(F32)
    before = lax.broadcasted_iota(I32, (tm, tm), 0) < lax.broadcasted_iota(I32, (tm, tm), 1)
    excl = jnp.dot(hot_sum.astype(BF16), before.astype(BF16), preferred_element_type=F32)
    base = excl + cnt_s[...]
    ranks = [jnp.sum(jnp.where(hot, base, 0.0), axis=0, keepdims=True) for hot in hots]
    rank_ref[...] = jnp.concatenate(ranks, axis=0).astype(I32)
    cnt_s[...] = cnt_s[...] + jnp.sum(hot_sum, axis=1, keepdims=True)
    cnt_ref[...] = cnt_s[...]


def _router(x, mods, layer, w_router, b_router, row_off, row_stride):
    B, T, D = x.shape
    tm = min(TOKEN_TILE, T)
    n_j = T // tm
    N = B * T
    lane_spec = pl.BlockSpec((TOP_K, tm), lambda b, j: (0, b * n_j + j))
    return pl.pallas_call(
        _router_body,
        grid=(B, n_j),
        in_specs=[pl.BlockSpec((None, tm, D), lambda b, j: (b, j, 0)),
                  _mod_spec(layer, 4, row_off, row_stride),
                  _mod_spec(layer, 3, row_off, row_stride),
                  pl.BlockSpec((N_EXP, D), lambda b, j: (0, 0)),
                  pl.BlockSpec((N_EXP, 1), lambda b, j: (0, 0))],
        out_specs=[pl.BlockSpec((None, tm, D), lambda b, j: (b, j, 0)),
                   lane_spec, lane_spec, lane_spec,
                   pl.BlockSpec((N_EXP, 1), lambda b, j: (0, 0))],
        out_shape=[jax.ShapeDtypeStruct((B, T, D), BF16),
                   jax.ShapeDtypeStruct((TOP_K, N), I32),
                   jax.ShapeDtypeStruct((TOP_K, N), F32),
                   jax.ShapeDtypeStruct((TOP_K, N), I32),
                   jax.ShapeDtypeStruct((N_EXP, 1), F32)],
        scratch_shapes=[pltpu.VMEM((N_EXP, 1), F32)],
        compiler_params=_cparams(("arbitrary", "arbitrary")),
        name="router",
    )(x, mods, mods, w_router.T, b_router.reshape(N_EXP, 1))


def _expert_body(blk_e_ref, n_used_ref, xs_ref, wgu_ref, bgu_ref, wdn_ref, bdn_ref, o_ref, wgu_s, wdn_s):
    i = pl.program_id(0)
    prev = blk_e_ref[jnp.maximum(i - 1, 0)]
    changed = jnp.logical_or(i == 0, blk_e_ref[i] != prev)

    @pl.when(changed)
    def _():
        wgu_s[...] = wgu_ref[...].astype(BF16)
        wdn_s[...] = wdn_ref[...].astype(BF16)

    @pl.when(i < n_used_ref[0])
    def _():
        h = jnp.dot(xs_ref[...], wgu_s[...], preferred_element_type=F32) + bgu_ref[...]
        gt = jnp.minimum(h[:, :D_FF], SWIGLU_LIMIT)
        up = jnp.clip(h[:, D_FF:], -SWIGLU_LIMIT, SWIGLU_LIMIT)
        act = (up + 1.0) * gt * jax.nn.sigmoid(SWIGLU_ALPHA * gt)
        o_ref[...] = jnp.dot(act.astype(BF16), wdn_s[...], preferred_element_type=F32) + bdn_ref[...]

    @pl.when(i >= n_used_ref[0])
    def _():
        o_ref[...] = jnp.zeros_like(o_ref)


def _experts(xs, blk_e, n_used, layer, w_gu, b_gu, w_dn, b_dn):
    slots, D = xs.shape
    te = EXPERT_TILE
    n_blk = slots // te
    grid_spec = pltpu.PrefetchScalarGridSpec(
        num_scalar_prefetch=2,
        grid=(n_blk,),
        in_specs=[pl.BlockSpec((te, D), lambda i, be, nu: (i, 0)),
                  pl.BlockSpec((None, None, D, 2 * D_FF), lambda i, be, nu: (layer, be[i], 0, 0)),
                  pl.BlockSpec((None, None, 1, 2 * D_FF), lambda i, be, nu: (layer, be[i], 0, 0)),
                  pl.BlockSpec((None, None, D_FF, D), lambda i, be, nu: (layer, be[i], 0, 0)),
                  pl.BlockSpec((None, None, 1, D), lambda i, be, nu: (layer, be[i], 0, 0))],
        out_specs=pl.BlockSpec((te, D), lambda i, be, nu: (i, 0)),
        scratch_shapes=[pltpu.VMEM((D, 2 * D_FF), BF16), pltpu.VMEM((D_FF, D), BF16)],
    )
    return pl.pallas_call(
        _expert_body,
        grid_spec=grid_spec,
        out_shape=jax.ShapeDtypeStruct((slots, D), F32),
        compiler_params=_cparams(("arbitrary",)),
        name="experts",
    )(blk_e, n_used, xs, w_gu, b_gu.reshape(DEPTH, N_EXP, 1, 2 * D_FF), w_dn, b_dn.reshape(DEPTH, N_EXP, 1, D))


def _combine_body(x_ref, y_ref, gate_ref, g2_ref, lg_ref, lb_ref, o_ref):
    gates = gate_ref[...]
    acc = y_ref[0] * gates[:, 0:1]
    for k in range(1, TOP_K):
        acc = acc + y_ref[k] * gates[:, k:k + 1]
    v = ALPHA_DN * x_ref[...] + g2_ref[...] * acc
    o_ref[...] = _layer_norm(v, lg_ref[...], lb_ref[...])


def _combine(x, gathered, gates_nk, mods, layer, ln_g, ln_b, row_off, row_stride):
    B, T, D = x.shape
    tm = min(TOKEN_TILE, T)
    vec = pl.BlockSpec((1, D), lambda b, j: (0, 0))
    return pl.pallas_call(
        _combine_body,
        grid=(B, T // tm),
        in_specs=[pl.BlockSpec((None, tm, D), lambda b, j: (b, j, 0)),
                  pl.BlockSpec((TOP_K, None, tm, D), lambda b, j: (0, b, j, 0)),
                  pl.BlockSpec((None, tm, TOP_K), lambda b, j: (b, j, 0)),
                  _mod_spec(layer, 5, row_off, row_stride),
                  vec, vec],
        out_specs=pl.BlockSpec((None, tm, D), lambda b, j: (b, j, 0)),
        out_shape=jax.ShapeDtypeStruct((B, T, D), F32),
        compiler_params=_cparams(("arbitrary", "arbitrary")),
        name="moe_combine",
    )(x, gathered, gates_nk, mods, ln_g.reshape(1, D), ln_b.reshape(1, D))


def _moe_layer(x, mods, layer, w_router, b_router, w_gu, b_gu, w_dn, b_dn, ln_g, ln_b, row_off, row_stride):
    B, T, D = x.shape
    N = B * T
    te = EXPERT_TILE
    u2, topi, gates, rank, cnt = _router(x, mods, layer, w_router, b_router, row_off, row_stride)
    counts = cnt[:, 0].astype(I32)
    padded = (counts + te - 1) // te * te
    pad_end = jnp.cumsum(padded)
    pad_start = pad_end - padded
    e_ids = jnp.arange(N_EXP, dtype=I32)
    start_of = jnp.sum(jnp.where(topi[None] == e_ids[:, None, None], pad_start[:, None, None], 0), axis=0)
    dest = start_of + rank
    n_blk = (N * TOP_K) // te + N_EXP
    blk_pos = jnp.arange(n_blk, dtype=I32) * te
    blk_e = jnp.minimum(jnp.sum((pad_end[None, :] <= blk_pos[:, None]).astype(I32), axis=1), N_EXP - 1)
    n_used = (pad_end[-1] // te).astype(I32).reshape(1)
    tok = jnp.broadcast_to(jnp.arange(N, dtype=I32)[None, :], (TOP_K, N))
    slot_tok = jnp.zeros((n_blk * te,), I32).at[dest.reshape(-1)].set(
        tok.reshape(-1), mode='promise_in_bounds', unique_indices=True)
    xs = u2.reshape(N, D).at[slot_tok].get(mode='promise_in_bounds')
    ys = _experts(xs, blk_e, n_used, layer, w_gu, b_gu, w_dn, b_dn)
    gathered = ys.at[dest].get(mode='promise_in_bounds').reshape(TOP_K, B, T, D)
    return _combine(x, gathered, gates.T.reshape(B, T, TOP_K), mods, layer, ln_g, ln_b, row_off, row_stride)


def _split3(x):
    hi = x.astype(BF16)
    r1 = x - hi.astype(F32)
    mid = r1.astype(BF16)
    lo = (r1 - mid.astype(F32)).astype(BF16)
    return jnp.concatenate([hi, mid, lo], axis=1)


def _inproj1_body(x_ref, sc_ref, sh_ref, w_ref, wg2_ref, wgh_ref, bg_ref,
                  qk_ref, v_ref, o_ref, gcol_ref, grow_ref):
    u = x_ref[...] * (1.0 + sc_ref[...]) + sh_ref[...]
    ub = u.astype(BF16)
    z = jnp.dot(ub, w_ref[...], preferred_element_type=F32)
    qk_ref[...] = z[:, :2 * D_MODEL]
    v_ref[...] = z[:, 2 * D_MODEL:3 * D_MODEL].astype(BF16)
    o_ref[...] = z[:, 3 * D_MODEL:]
    tm = u.shape[0]
    nh = 4 * H_C
    u_lo = (u - ub.astype(F32)).astype(BF16)
    g2 = jnp.dot(ub, wg2_ref[...], preferred_element_type=F32)
    gc = g2[:, :nh] + g2[:, nh:] + jnp.dot(u_lo, wgh_ref[...], preferred_element_type=F32) + bg_ref[...]
    r_i = lax.broadcasted_iota(I32, (tm, tm), 0)
    c_i = lax.broadcasted_iota(I32, (tm, tm), 1)
    same = (r_i // CHUNK) == (c_i // CHUNK)
    tri_f = jnp.where(jnp.logical_and(same, c_i <= r_i), 1.0, 0.0).astype(BF16)
    tri_b = jnp.where(jnp.logical_and(same, c_i >= r_i), 1.0, 0.0).astype(BF16)
    lf3 = _split3(jax.nn.log_sigmoid(gc))

    def sum3(p, axis):
        if axis == 1:
            return p[:, :nh] + p[:, nh:2 * nh] + p[:, 2 * nh:]
        return p[:nh] + p[nh:2 * nh] + p[2 * nh:]

    col_i = lax.broadcasted_iota(I32, (tm, nh), 1)
    cum = jnp.where(col_i < 2 * H_C,
                    sum3(jnp.dot(tri_f, lf3, preferred_element_type=F32), 1),
                    sum3(jnp.dot(tri_b, lf3, preferred_element_type=F32), 1))
    gcol = jnp.where((col_i // H_C) % 2 == 1, cum, gc)
    gcol_ref[...] = gcol
    eye = jnp.where(r_i == c_i, 1.0, 0.0).astype(BF16)
    gt3 = lax.dot_general(_split3(gcol), eye, (((0,), (0,)), ((), ())), preferred_element_type=F32)
    grow_ref[...] = sum3(gt3, 0)


def _inproj1(x, mods, w_in1, w_gate1, b_gate1, row_off, row_stride):
    B, T, D = x.shape
    tm = min(TOKEN_TILE, T)
    n_j = T // tm
    nh = 4 * H_C
    wg_hi = w_gate1.astype(BF16)
    wg_lo = (w_gate1 - wg_hi.astype(F32)).astype(BF16)
    tok = lambda w: pl.BlockSpec((None, tm, w), lambda b, j: (b, j, 0))
    return pl.pallas_call(
        _inproj1_body,
        grid=(B, n_j),
        in_specs=[tok(D),
                  _mod_spec(1, 1, row_off, row_stride),
                  _mod_spec(1, 0, row_off, row_stride),
                  pl.BlockSpec((D, 4 * D), lambda b, j: (0, 0)),
                  pl.BlockSpec((D, 2 * nh), lambda b, j: (0, 0)),
                  pl.BlockSpec((D, nh), lambda b, j: (0, 0)),
                  pl.BlockSpec((1, nh), lambda b, j: (0, 0))],
        out_specs=[tok(2 * D), tok(D), tok(D), tok(nh),
                   pl.BlockSpec((nh, tm), lambda b, j: (0, b * n_j + j))],
        out_shape=[jax.ShapeDtypeStruct((B, T, 2 * D), F32),
                   jax.ShapeDtypeStruct((B, T, D), BF16),
                   jax.ShapeDtypeStruct((B, T, D), F32),
                   jax.ShapeDtypeStruct((B, T, nh), F32),
                   jax.ShapeDtypeStruct((nh, B * T), F32)],
        compiler_params=_cparams(("arbitrary", "arbitrary")),
        name="inproj1",
    )(x, mods, mods, w_in1.astype(BF16), jnp.concatenate([wg_hi, wg_lo], axis=1), wg_hi, b_gate1.reshape(1, nh))


def _mlstm_body(q_ref, k_ref, v_ref, gcol_ref, grow_ref, cw_ref, cb_ref, c0_ref, n0_ref, m0_ref,
                h_ref, cfin_ref, nfin_ref, mfin_ref, xpad_s, q_s, k_s, *, T):
    n_c = T // CHUNK
    L = CHUNK

    rb = min(T, 512)
    pad = 8

    def conv_silu_into(src_ref, w, bias, dst_s, scale):
        xpad_s[0:pad] = jnp.zeros((pad, DH_C), F32)
        xpad_s[T + pad:T + 2 * pad] = jnp.zeros((pad, DH_C), F32)
        xpad_s[pad:T + pad] = src_ref[...]

        def blk(r, _):
            off = pl.multiple_of(r * rb, pad)
            win = xpad_s[pl.ds(off, rb + 2 * pad), :]
            acc = (pltpu.roll(win, 2, 0) * w[0:1] + pltpu.roll(win, 1, 0) * w[1:2] + win * w[2:3]
                   + pltpu.roll(win, rb + 2 * pad - 1, 0) * w[3:4])[pad:rb + pad] + bias
            dst_s[pl.ds(off, rb), :] = (jax.nn.silu(acc) * scale).astype(BF16)
            return 0

        lax.fori_loop(0, T // rb, blk, 0)

    conv_silu_into(q_ref, cw_ref[0], cb_ref[0], q_s, 1.0)
    conv_silu_into(k_ref, cw_ref[1], cb_ref[1], k_s, DH_C ** -0.5)

    t_i = lax.broadcasted_iota(I32, (L, L), 0)
    s_i = lax.broadcasted_iota(I32, (L, L), 1)

    cfin_ref[...] = c0_ref[...]
    nfin_ref[...] = n0_ref[...]
    mfin_ref[...] = m0_ref[...]

    def chunk(d, c):
        mask = (s_i <= t_i) if d == 0 else (s_i >= t_i)
        last = L - 1 if d == 0 else 0
        off = pl.multiple_of(c * L, L)
        qc = q_s[pl.ds(off, L), :]
        kc = k_s[pl.ds(off, L), :]
        vc = v_ref[pl.ds(off, L), :]
        gcol = gcol_ref[pl.ds(off, L), :]
        grow = grow_ref[:, pl.ds(off, L)]
        li_c = gcol[:, 2 * d:2 * d + 1]
        b_c = gcol[:, 2 * d + 1:2 * d + 2]
        li_r = grow[2 * d:2 * d + 1, :]
        b_r = grow[2 * d + 1:2 * d + 2, :]
        m_prev = mfin_ref[d]
        log_d = jnp.where(mask, b_c - b_r + li_r, -jnp.inf)
        m_inter = b_c + m_prev
        m_t = jnp.maximum(m_inter, jnp.max(log_d, axis=-1, keepdims=True))
        dmat = jnp.exp(log_d - m_t)
        w_inter = jnp.exp(m_inter - m_t)
        c_old = cfin_ref[d]
        s = lax.dot_general(qc, kc, (((1,), (1,)), ((), ())), preferred_element_type=F32) * dmat
        inter = lax.dot_general(qc, c_old.astype(BF16), (((1,), (1,)), ((), ())), preferred_element_type=F32)
        num = w_inter * inter + jnp.dot(s.astype(BF16), vc, preferred_element_type=F32)
        qn = jnp.sum(qc.astype(F32) * nfin_ref[d], axis=-1, keepdims=True)
        den = w_inter * qn + jnp.sum(s, axis=-1, keepdims=True)
        h = num / jnp.maximum(jnp.abs(den), jnp.exp(-m_t))
        m_new = m_t[last:last + 1, :]
        b_last = b_c[last:last + 1, :]
        w_s = jnp.exp(b_last - b_c + li_c - m_new)
        decay = jnp.exp(b_last + m_prev - m_new)
        wv = (w_s * vc.astype(F32)).astype(BF16)
        cfin_ref[d] = decay * c_old + lax.dot_general(wv, kc, (((0,), (0,)), ((), ())),
                                                      preferred_element_type=F32)
        nfin_ref[d] = decay * nfin_ref[d] + jnp.sum(w_s * kc.astype(F32), axis=0, keepdims=True)
        mfin_ref[d] = m_new
        return off, h

    def both(ci, _):
        off_f, h_f = chunk(0, ci)
        off_b, h_b = chunk(1, n_c - 1 - ci)
        h_ref[pl.ds(off_f, L), :] = h_f
        xpad_s[pl.ds(off_b, L), :] = h_b
        return 0

    lax.fori_loop(0, n_c, both, 0)

    def add_bwd(r, _):
        off = pl.multiple_of(r * rb, pad)
        h_ref[pl.ds(off, rb), :] = h_ref[pl.ds(off, rb), :] + xpad_s[pl.ds(off, rb), :]
        return 0

    lax.fori_loop(0, T // rb, add_bwd, 0)


def _mlstm(qk, v, gcol, grow, conv_w, conv_b, c0, n0, m0):
    B, T, _ = v.shape
    DH = DH_C
    cw = conv_w.reshape(CONV_W, 2, H_C, DH).transpose(2, 1, 0, 3)
    cb = conv_b.reshape(2, H_C, 1, DH).transpose(1, 0, 2, 3)
    st = lambda *tail: pl.BlockSpec((None, 2, None) + tail, lambda b, h: (b, 0, h) + (0,) * len(tail))
    return pl.pallas_call(
        functools.partial(_mlstm_body, T=T),
        grid=(B, H_C),
        in_specs=[pl.BlockSpec((None, T, DH), lambda b, h: (b, 0, h)),
                  pl.BlockSpec((None, T, DH), lambda b, h: (b, 0, H_C + h)),
                  pl.BlockSpec((None, T, DH), lambda b, h: (b, 0, h)),
                  pl.BlockSpec((None, None, T, 4), lambda b, h: (b, h, 0, 0)),
                  pl.BlockSpec((None, None, 4, T), lambda b, h: (b, h, 0, 0)),
                  pl.BlockSpec((None, 2, CONV_W, DH), lambda b, h: (h, 0, 0, 0)),
                  pl.BlockSpec((None, 2, 1, DH), lambda b, h: (h, 0, 0, 0)),
                  st(DH, DH), st(1, DH), st(1, 1)],
        out_specs=[pl.BlockSpec((None, T, DH), lambda b, h: (b, 0, h)),
                   st(DH, DH), st(1, DH), st(1, 1)],
        out_shape=[jax.ShapeDtypeStruct((B, T, D_MODEL), F32),
                   jax.ShapeDtypeStruct((B, 2, H_C, DH, DH), F32),
                   jax.ShapeDtypeStruct((B, 2, H_C, 1, DH), F32),
                   jax.ShapeDtypeStruct((B, 2, H_C, 1, 1), F32)],
        scratch_shapes=[pltpu.VMEM((T + 16, DH), F32), pltpu.VMEM((T, DH), BF16), pltpu.VMEM((T, DH), BF16)],
        compiler_params=_cparams(("arbitrary", "arbitrary")),
        name="mlstm",
    )(qk, qk, v, gcol, grow, cw, cb, c0, n0, m0)


def _outproj1_body(x_ref, h_ref, o_ref, g1_ref, w_ref, lg_ref, lb_ref, out_ref):
    y = jax.nn.sigmoid(o_ref[...]) * h_ref[...]
    out = jnp.dot(y.astype(BF16), w_ref[...], preferred_element_type=F32)
    v = ALPHA_DN * x_ref[...] + g1_ref[...] * out
    out_ref[...] = _layer_norm(v, lg_ref[...], lb_ref[...])


def _outproj1(x, h, o, mods, w_out1, ln_g, ln_b, row_off, row_stride):
    B, T, D = x.shape
    tm = min(TOKEN_TILE, T)
    tok = pl.BlockSpec((None, tm, D), lambda b, j: (b, j, 0))
    vec = pl.BlockSpec((1, D), lambda b, j: (0, 0))
    return pl.pallas_call(
        _outproj1_body,
        grid=(B, T // tm),
        in_specs=[tok, tok, tok, _mod_spec(1, 2, row_off, row_stride),
                  pl.BlockSpec((D, D), lambda b, j: (0, 0)), vec, vec],
        out_specs=tok,
        out_shape=jax.ShapeDtypeStruct((B, T, D), F32),
        compiler_params=_cparams(("arbitrary", "arbitrary")),
        name="outproj1",
    )(x, h, o, mods, w_out1.astype(BF16), ln_g.reshape(1, D), ln_b.reshape(1, D))


def _to_col_major(x):
    B, T, C = x.shape
    rows = T // GRID_W
    return x.reshape(B, rows, GRID_W, C).transpose(0, 2, 1, 3).reshape(B, T, C)


def _to_row_major(x):
    B, T, C = x.shape
    rows = T // GRID_W
    return x.reshape(B, GRID_W, rows, C).transpose(0, 2, 1, 3).reshape(B, T, C)


def _trunk(x, mods, st, P, s5_mats, grid):
    h_rg, s_re, s_im, m_c, m_n, m_m = st
    B, T, D = x.shape
    row_off, row_stride = 0, 1
    xa, ga, ub = _inproj0(x, mods, P['w_in0'][0].astype(BF16))
    h, h_fin = _rglru(xa, P['conv_a_w'][0], P['conv_a_b'][0], P['rg_wa'][0], P['rg_ba'][0],
                      P['rg_wi'][0], P['rg_bi'][0], P['rg_lam'][0], h_rg[:, 0].transpose(1, 0, 2))
    y5, s_fin = _s5(ub, s5_mats, _s5_state_to_lanes(s_re[:, 0], s_im[:, 0]))
    x = _outproj0(x, h, ga, y5, ub, mods, P['s5_d'][0], P['glu_w'][0], P['glu_b'][0], P['w_out0'][0],
                  P['ln_g'][0, 0], P['ln_b'][0, 0])
    x = _moe_layer(x, mods, 0, P['w_router'][0], P['b_router'][0], P['w_gu'], P['b_gu'],
                   P['w_down'], P['b_down'], P['ln_g'][0, 1], P['ln_b'][0, 1], row_off, row_stride)
    if grid:
        x = _to_col_major(x)
    qk, v, o, gcol, grow = _inproj1(x, mods, P['w_in1'][0], P['w_gate1'][0], P['b_gate1'][0], row_off, row_stride)
    gcol = gcol.reshape(B, T, 2, 2, H_C).transpose(0, 4, 1, 2, 3).reshape(B, H_C, T, 4)
    grow = grow.reshape(2, 2, H_C, B, T).transpose(3, 2, 0, 1, 4).reshape(B, H_C, 4, T)
    h, c_fin, n_fin, m_fin = _mlstm(qk, v, gcol, grow, P['conv_c_w'][0], P['conv_c_b'][0],
                                    m_c[:, 0], m_n[:, 0][:, :, :, None, :], m_m[:, 0][:, :, :, None, None])
    x = _outproj1(x, h, o, mods, P['w_out1'][0], P['ln_g'][1, 0], P['ln_b'][1, 0], row_off, row_stride)
    x = _moe_layer(x, mods, 1, P['w_router'][1], P['b_router'][1], P['w_gu'], P['b_gu'],
                   P['w_down'], P['b_down'], P['ln_g'][1, 1], P['ln_b'][1, 1], row_off, row_stride)
    if grid:
        x = _to_row_major(x)
    new_re, new_im = _s5_lanes_to_state(s_fin)
    new = (h_fin.transpose(1, 0, 2)[:, None], new_re[:, None], new_im[:, None],
           c_fin[:, None], n_fin[:, None, :, :, 0, :], m_fin[:, None, :, :, 0, 0])
    return x, new


def _forward(x_prompt, x_sample, c, c_ctx, states, P):
    bp = x_prompt.shape[0]
    bs = x_sample.shape[0]
    rows = 1 + bs
    rpad = -(-rows // 8) * 8
    cv = jnp.concatenate([c_ctx[None, :], c, jnp.zeros((rpad - rows, D_MODEL), F32)], axis=0)
    mods = _modulation(cv, P['w_mod'], P['b_mod'])
    s5_mats = _s5_matrices(P['s5_a_re'][0], P['s5_a_im'][0], P['s5_log_dt'][0], P['s5_b_re'][0], P['s5_b_im'][0],
                           P['s5_c_re'][0], P['s5_c_im'][0])
    zero_state = (jnp.zeros((bp, 1, 2, D_A), F32),
                  jnp.zeros((bp, 1, 2, G_B, P_B), F32),
                  jnp.zeros((bp, 1, 2, G_B, P_B), F32),
                  jnp.zeros((bp, 1, 2, H_C, DH_C, DH_C), F32),
                  jnp.zeros((bp, 1, 2, H_C, DH_C), F32),
                  jnp.zeros((bp, 1, 2, H_C), F32))
    mods_ctx = jnp.broadcast_to(mods[:, :, 0:1], mods.shape[:2] + (bp,) + mods.shape[3:])
    mods_lat = mods[:, :, 1:1 + bs]
    y_prompt, new = _trunk(x_prompt, mods_ctx, zero_state, P, s5_mats, False)
    y_sample, _ = _trunk(x_sample, mods_lat, states, P, s5_mats, True)
    return (y_prompt, y_sample) + tuple(new)


def kernel(x_prompt, x_sample, c, c_ctx, state_rglru, state_s5_re, state_s5_im, state_mlstm_C, state_mlstm_n, state_mlstm_m, w_mod, b_mod, ln_g, ln_b, w_in0, conv_a_w, conv_a_b, rg_wa, rg_ba, rg_wi, rg_bi, rg_lam, s5_a_re, s5_a_im, s5_log_dt, s5_b_re, s5_b_im, s5_c_re, s5_c_im, s5_d, glu_w, glu_b, w_out0, w_in1, w_gate1, b_gate1, conv_c_w, conv_c_b, w_out1, w_router, b_router, w_gu, b_gu, w_down, b_down):
    P = dict(w_mod=w_mod, b_mod=b_mod, ln_g=ln_g, ln_b=ln_b, w_in0=w_in0, conv_a_w=conv_a_w,
             conv_a_b=conv_a_b, rg_wa=rg_wa, rg_ba=rg_ba, rg_wi=rg_wi, rg_bi=rg_bi, rg_lam=rg_lam,
             s5_a_re=s5_a_re, s5_a_im=s5_a_im, s5_log_dt=s5_log_dt, s5_b_re=s5_b_re, s5_b_im=s5_b_im,
             s5_c_re=s5_c_re, s5_c_im=s5_c_im, s5_d=s5_d, glu_w=glu_w, glu_b=glu_b, w_out0=w_out0,
             w_in1=w_in1, w_gate1=w_gate1, b_gate1=b_gate1, conv_c_w=conv_c_w, conv_c_b=conv_c_b,
             w_out1=w_out1, w_router=w_router, b_router=b_router, w_gu=w_gu, b_gu=b_gu,
             w_down=w_down, b_down=b_down)
    states = (state_rglru, state_s5_re, state_s5_im, state_mlstm_C, state_mlstm_n, state_mlstm_m)
    return _forward(x_prompt, x_sample, c, c_ctx, states, P)
```

```python
import functools
import math

import jax
import jax.numpy as jnp
from jax import lax
from jax.experimental import pallas as pl
from jax.experimental.pallas import tpu as pltpu

F32 = jnp.float32
BF16 = jnp.bfloat16
I32 = jnp.int32
HI = lax.Precision.HIGHEST

D_MODEL = 1024
DEPTH = 2
GRID_W = 64
D_A = 512
NB_A = 8
BS_A = D_A // NB_A
CONV_W = 4
LRU_C = 8.0
D_B = 512
S5_GROUP = 16
G_B = D_B // S5_GROUP
P_B = 64
H_C = 4
DH_C = D_MODEL // H_C
CHUNK = 128
N_EXP = 32
TOP_K = 4
D_FF = D_MODEL
SWIGLU_LIMIT = 7.0
SWIGLU_ALPHA = 1.702
ALPHA_DN = (2 * DEPTH) ** 0.25
LN_EPS = 1e-5

S5_L = 16
S5_W = S5_L * S5_GROUP
S5_OCT = 8
TOKEN_TILE = 256
EXPERT_TILE = 256
MOE_TILE = 512
RUN_ALIGN = 16
VMEM_LIMIT = 56 * 1024 * 1024


def _cparams(sem, vmem=VMEM_LIMIT):
    return pltpu.CompilerParams(dimension_semantics=sem, vmem_limit_bytes=vmem)


def _layer_norm(v, g, b):
    mu = jnp.mean(v, axis=-1, keepdims=True)
    c = v - mu
    var = jnp.mean(c * c, axis=-1, keepdims=True)
    return c * lax.rsqrt(var + LN_EPS) * g + b


def _mod_spec(layer, which, row_off, row_stride):
    return pl.BlockSpec((None, None, None, 1, D_MODEL),
                        lambda b, j: (layer, which, row_off + b * row_stride, 0, 0))


def _mod_spec_all(layer, which, nb):
    return pl.BlockSpec((None, None, nb, 1, D_MODEL), lambda i: (layer, which, 0, 0, 0))


def _row_permutation(n_out_major, n_out_minor):
    n = n_out_major * n_out_minor
    r_out = lax.broadcasted_iota(I32, (n, n), 0)
    r_in = lax.broadcasted_iota(I32, (n, n), 1)
    hit = jnp.logical_and(r_out // n_out_minor == r_in % n_out_major, r_out % n_out_minor == r_in // n_out_major)
    return jnp.where(hit, 1.0, 0.0).astype(BF16)


def _mod_body(c_ref, w_ref, b_ref, o_ref):
    s = jax.nn.silu(c_ref[...])
    o_ref[...] = jnp.dot(s, w_ref[...], precision=HI, preferred_element_type=F32) + b_ref[...]


def _modulation(cv, w_mod, b_mod):
    R, D = cv.shape
    L, _, N6 = w_mod.shape
    tn = N6 // 6
    out = pl.pallas_call(
        _mod_body,
        grid=(L, N6 // tn),
        in_specs=[pl.BlockSpec((R, D), lambda l, j: (0, 0)),
                  pl.BlockSpec((None, D, tn), lambda l, j: (l, 0, j)),
                  pl.BlockSpec((None, 1, tn), lambda l, j: (l, 0, j))],
        out_specs=pl.BlockSpec((None, R, tn), lambda l, j: (l, 0, j)),
        out_shape=jax.ShapeDtypeStruct((L, R, N6), F32),
        compiler_params=_cparams(("arbitrary", "arbitrary")),
        name="modulation",
    )(cv, w_mod, b_mod.reshape(L, 1, N6))
    return out.reshape(L, R, 6, D).transpose(0, 2, 1, 3)[:, :, :, None, :]


def _inproj0_body(x_ref, sc_ref, sh_ref, w_ref, xa_ref, ga_ref, ub_ref):
    nb, tq, d = x_ref.shape
    u = x_ref[...] * (1.0 + sc_ref[...]) + sh_ref[...]
    ub = u.reshape(nb * tq, d).astype(BF16)
    ut = jnp.dot(_row_permutation(tq, nb), ub, preferred_element_type=F32).astype(BF16)
    z = jnp.dot(ut, w_ref[...], preferred_element_type=F32).reshape(tq, nb, w_ref.shape[1])
    xa_ref[...] = z[:, :, :D_A]
    ga_ref[...] = z[:, :, D_A:2 * D_A]
    ub_ref[...] = z[:, :, 2 * D_A:]


def _inproj0(x, pmods, w_in0):
    B, T, D = x.shape
    tq = TOKEN_TILE // B
    nz = w_in0.shape[1]
    tmaj = pl.BlockSpec((tq, B, D_A), lambda i: (i, 0, 0))
    return pl.pallas_call(
        _inproj0_body,
        grid=(T // tq,),
        in_specs=[pl.BlockSpec((B, tq, D), lambda i: (0, i, 0)),
                  _mod_spec_all(0, 1, B), _mod_spec_all(0, 0, B),
                  pl.BlockSpec((D, nz), lambda i: (0, 0))],
        out_specs=[tmaj, tmaj, tmaj],
        out_shape=[jax.ShapeDtypeStruct((T, B, D_A), F32)] * 3,
        compiler_params=_cparams(("arbitrary",)),
        name="inproj0",
    )(x, pmods, pmods, w_in0)


def _rglru_body(cur_ref, prev_ref, next_ref, cw_ref, cb_ref, wa_ref, wi_ref, ba_ref, bi_ref, lam_ref, h0_ref,
                h_ref, hfin_ref, ext_s, a_s, b_s, carry_s, *, tt, n_t):
    d = pl.program_id(0)
    j = pl.program_id(1)
    jj = j + d * (n_t - 1 - 2 * j)
    nb = cur_ref.shape[1]
    half = D_A // 2
    ext_s[0:2] = jnp.where(jj == 0, 0.0, prev_ref[...])
    ext_s[2:tt + 2] = cur_ref[...]
    ext_s[tt + 2:tt + 3] = jnp.where(jj == n_t - 1, 0.0, next_ref[...])
    xc = ext_s[0:tt] * cw_ref[0]
    for k in range(1, CONV_W):
        xc = xc + ext_s[k:k + tt] * cw_ref[k]
    xc = xc + cb_ref[...]
    x2 = xc.reshape(tt * nb, D_A)
    xb = x2.astype(BF16)

    def gate(w_ref, bias_ref):
        lo = jnp.dot(xb[:, :half], w_ref[0], preferred_element_type=F32)
        hi = jnp.dot(xb[:, half:], w_ref[1], preferred_element_type=F32)
        return jax.nn.sigmoid(jnp.concatenate([lo, hi], axis=1) + bias_ref[...])

    r = gate(wa_ref, ba_ref)
    i = gate(wi_ref, bi_ref)
    log_a = LRU_C * r * jax.nn.log_sigmoid(lam_ref[...])
    a = jnp.exp(log_a)
    one_minus_a2 = -jnp.tanh(log_a) * (a * a + 1.0)
    a_s[...] = a.reshape(tt, nb, D_A)
    b_s[...] = (jnp.sqrt(one_minus_a2) * (i * x2)).reshape(tt, nb, D_A)

    @pl.when(j == 0)
    def _():
        carry_s[...] = h0_ref[...]

    def step(t, h):
        h = a_s[t] * h + b_s[t]
        h_ref[t] = h
        return h

    @pl.when(d == 0)
    def _():
        carry_s[...] = lax.fori_loop(0, tt, step, carry_s[...], unroll=8)

    @pl.when(d == 1)
    def _():
        carry_s[...] = lax.fori_loop(0, tt, lambda t, h: step(tt - 1 - t, h), carry_s[...], unroll=8)

    hfin_ref[...] = carry_s[...]


def _block_diag_halves(w):
    nd = w.shape[0]
    per = NB_A // 2
    w = w.reshape(nd, 2, per, BS_A, BS_A)
    eye = jnp.eye(per, dtype=w.dtype)
    out = jnp.einsum('dhncz,nm->dhncmz', w, eye)
    return out.reshape(nd, 2, per * BS_A, per * BS_A)


def _rglru(x3, conv_w, conv_b, wa, ba, wi, bi, lam, h0):
    T, B, _ = x3.shape
    tt = min(T, 1024 // B)
    n_t = T // tt

    def cur_map(d, j):
        return (j + d * (n_t - 1 - 2 * j), 0, 0)

    def prev_map(d, j):
        jj = j + d * (n_t - 1 - 2 * j)
        return (jnp.maximum(jj * (tt // 2) - 1, 0), 0, 0)

    def next_map(d, j):
        jj = j + d * (n_t - 1 - 2 * j)
        return (jnp.minimum((jj + 1) * tt, T - 1), 0, 0)

    dir_spec3 = pl.BlockSpec((None, 1, D_A), lambda d, j: (d, 0, 0))
    h, hfin = pl.pallas_call(
        functools.partial(_rglru_body, tt=tt, n_t=n_t),
        grid=(2, n_t),
        in_specs=[pl.BlockSpec((tt, B, D_A), cur_map),
                  pl.BlockSpec((2, B, D_A), prev_map),
                  pl.BlockSpec((1, B, D_A), next_map),
                  pl.BlockSpec((CONV_W, 1, D_A), lambda d, j: (0, 0, 0)),
                  pl.BlockSpec((1, 1, D_A), lambda d, j: (0, 0, 0)),
                  pl.BlockSpec((None, 2, D_A // 2, D_A // 2), lambda d, j: (d, 0, 0, 0)),
                  pl.BlockSpec((None, 2, D_A // 2, D_A // 2), lambda d, j: (d, 0, 0, 0)),
                  dir_spec3, dir_spec3, dir_spec3,
                  pl.BlockSpec((None, B, D_A), lambda d, j: (d, 0, 0))],
        out_specs=[pl.BlockSpec((None, tt, B, D_A), lambda d, j: (d,) + cur_map(d, j)),
                   pl.BlockSpec((None, B, D_A), lambda d, j: (d, 0, 0))],
        out_shape=[jax.ShapeDtypeStruct((2, T, B, D_A), F32),
                   jax.ShapeDtypeStruct((2, B, D_A), F32)],
        scratch_shapes=[pltpu.VMEM((tt + 3, B, D_A), F32),
                        pltpu.VMEM((tt, B, D_A), F32),
                        pltpu.VMEM((tt, B, D_A), F32),
                        pltpu.VMEM((B, D_A), F32)],
        compiler_params=_cparams(("arbitrary", "arbitrary")),
        name="rglru",
    )(x3, x3, x3, conv_w.reshape(CONV_W, 1, D_A), conv_b.reshape(1, 1, D_A),
      _block_diag_halves(wa).astype(BF16), _block_diag_halves(wi).astype(BF16),
      ba.reshape(2, 1, D_A), bi.reshape(2, 1, D_A), lam.reshape(2, 1, D_A), h0)
    return h, hfin


def _s5_matrices(a_re, a_im, log_dt, b_re, b_im, c_re, c_im):
    L = S5_L
    lam = lax.complex(a_re.astype(F32), a_im.astype(F32))
    dt = jnp.exp(log_dt.astype(F32))[..., None]
    ldt = lam * dt
    a_bar = jnp.exp(ldt)
    b_bar = ((a_bar - 1.0) / lam)[..., None] * lax.complex(b_re.astype(F32), b_im.astype(F32))
    cc = lax.complex(c_re.astype(F32), c_im.astype(F32))
    ks = jnp.arange(L + 1, dtype=F32)
    pw = jnp.exp(ldt[:, :, None, :] * ks[None, None, :, None].astype(jnp.complex64))
    kern = jnp.real(jnp.einsum('dgjp,dgkp,dgpi->dgkji', cc, pw[:, :, :L], b_bar, precision=HI))
    s_idx = jnp.arange(L)[:, None]
    t_idx = jnp.arange(L)[None, :]
    lag_f = jnp.clip(t_idx - s_idx, 0, L - 1)
    lag_b = jnp.clip(s_idx - t_idx, 0, L - 1)
    m_f = jnp.where((t_idx >= s_idx)[None, :, :, None, None], kern[0][:, lag_f], 0.0)
    m_b = jnp.where((s_idx >= t_idx)[None, :, :, None, None], kern[1][:, lag_b], 0.0)
    m = (m_f + m_b).transpose(0, 1, 4, 2, 3).reshape(G_B, S5_W, S5_W)
    down = (L - ks[:L])[None, :, None].astype(jnp.complex64)
    pw_down_b = jnp.exp(ldt[1][:, None, :] * down)
    g_f = jnp.exp(ldt[0][:, None, :] * (down - 1.0))[..., None] * b_bar[0][:, None]
    g_b = pw[1][:, :L, :, None] * b_bar[1][:, None]

    def g_cols(x):
        return x.transpose(0, 1, 3, 2).reshape(G_B, S5_W, P_B)

    gs = jnp.concatenate([g_cols(jnp.real(g_f)), g_cols(jnp.real(g_b)),
                          g_cols(jnp.imag(g_f)), g_cols(jnp.imag(g_b))], axis=-1)
    e_f = cc[0][:, None] * pw[0][:, 1:, None, :]
    e_b = cc[1][:, None] * pw_down_b[:, :, None, :]

    def e_rows(x):
        return x.transpose(0, 3, 1, 2).reshape(G_B, P_B, S5_W)

    e = jnp.concatenate([e_rows(jnp.real(e_f)), e_rows(jnp.real(e_b)),
                         -e_rows(jnp.imag(e_f)), -e_rows(jnp.imag(e_b))], axis=1)
    a_l = pw[:, :, L]
    al = jnp.concatenate([jnp.real(a_l[0]), jnp.real(a_l[1]), jnp.imag(a_l[0]), jnp.imag(a_l[1])], axis=-1)
    no = G_B // S5_OCT
    ow = S5_OCT * S5_W
    rows_sgi = lambda x: x.reshape(no, S5_OCT, L, S5_GROUP, S5_W).transpose(0, 2, 1, 3, 4).reshape(no, ow, S5_W)
    src = jnp.stack([rows_sgi(m), rows_sgi(gs), e.reshape(no, ow, S5_W)]).astype(BF16)
    r = jnp.arange(ow, dtype=I32)
    c = jnp.arange(S5_W, dtype=I32)
    grp_sgi = (r // S5_GROUP) % S5_OCT
    grp_gl = r // S5_W
    src_sgi = (r // (S5_OCT * S5_GROUP)) * S5_GROUP + r % S5_GROUP
    src_gl = r % S5_W
    spread = jnp.stack([src_sgi, src_gl, src_sgi])[:, None, :] == c[None, :, None]
    row_grp = jnp.stack([grp_sgi, grp_sgi, grp_gl])[:, :, None]
    col_grp = jnp.stack([grp_sgi, grp_gl, grp_sgi])[:, None, :]
    tr = 512
    out = pl.pallas_call(
        _s5_expand_body,
        grid=(3, no, ow // tr),
        in_specs=[pl.BlockSpec((None, None, tr, S5_W), lambda k, q, i: (k, q, i, 0)),
                  pl.BlockSpec((None, S5_W, ow), lambda k, q, i: (k, 0, 0)),
                  pl.BlockSpec((None, tr, 1), lambda k, q, i: (k, i, 0)),
                  pl.BlockSpec((None, 1, ow), lambda k, q, i: (k, 0, 0))],
        out_specs=pl.BlockSpec((None, None, tr, ow), lambda k, q, i: (k, q, i, 0)),
        out_shape=jax.ShapeDtypeStruct((3, no, ow, ow), BF16),
        compiler_params=_cparams(("arbitrary", "arbitrary", "arbitrary")),
        name="s5_expand",
    )(src, spread.astype(BF16), row_grp, col_grp)
    return out, al.reshape(1, G_B * S5_W)


def _s5_expand_body(src_ref, spread_ref, rg_ref, cg_ref, o_ref):
    wide = jnp.dot(src_ref[...], spread_ref[...], preferred_element_type=F32)
    o_ref[...] = jnp.where(rg_ref[...] == cg_ref[...], wide, 0.0).astype(BF16)


def _s5_fill_lhs(u_ref, lhs_s):
    tc, _, nb, lanes = u_ref.shape
    for s in range(S5_L):
        lhs_s[:, s * lanes:(s + 1) * lanes] = u_ref[:, s].reshape(tc * nb, lanes).astype(BF16)


def _s5_state_body(u_ref, g_ref, f_ref, lhs_s):
    _s5_fill_lhs(u_ref, lhs_s)
    f_ref[...] = jnp.dot(lhs_s[...], g_ref[...], preferred_element_type=F32).reshape(f_ref.shape)


def _s5_scan_body(f_ref, a_ref, s0_ref, sin_ref, sfin_ref, *, n_c):
    nb, width = s0_ref.shape
    hw = 2 * P_B
    n_g = width // S5_W
    a = a_ref[...]
    a_re = [a[:, k * S5_W:k * S5_W + hw] for k in range(n_g)]
    a_im = [a[:, k * S5_W + hw:(k + 1) * S5_W] for k in range(n_g)]
    is_fwd = (lax.broadcasted_iota(I32, (nb, width), 1) % hw) < P_B

    def split(x):
        return tuple(x[:, k * hw:(k + 1) * hw] for k in range(2 * n_g))

    def merge(parts):
        return jnp.concatenate(parts, axis=-1)

    def advance(c, parts):
        f = split(f_ref[c])
        out = []
        for k in range(n_g):
            s_re, s_im = parts[2 * k], parts[2 * k + 1]
            out.append(a_re[k] * s_re - a_im[k] * s_im + f[2 * k])
            out.append(a_re[k] * s_im + a_im[k] * s_re + f[2 * k + 1])
        return tuple(out)

    def fwd(c, parts):
        sin_ref[c] = merge(parts)
        return advance(c, parts)

    init = split(s0_ref[...])
    fin_f = lax.fori_loop(0, n_c, fwd, init)

    def bwd(k, parts):
        c = n_c - 1 - k
        sin_ref[c] = jnp.where(is_fwd, sin_ref[c], merge(parts))
        return advance(c, parts)

    fin_b = lax.fori_loop(0, n_c, bwd, init)
    sfin_ref[...] = jnp.where(is_fwd, merge(fin_f), merge(fin_b))


def _s5_out_body(u_ref, sin_ref, m_ref, e_ref, y_ref, lhs_s):
    _s5_fill_lhs(u_ref, lhs_s)
    tc, _, nb, lanes = u_ref.shape
    sin = sin_ref[...].reshape(tc * nb, sin_ref.shape[-1]).astype(BF16)
    y = (jnp.dot(lhs_s[...], m_ref[...], preferred_element_type=F32)
         + jnp.dot(sin, e_ref[...], preferred_element_type=F32))
    for s in range(S5_L):
        y_ref[:, s] = y[:, s * lanes:(s + 1) * lanes].reshape(tc, nb, lanes)


def _s5(ub, mats, s0):
    mge, al = mats
    T, B, _ = ub.shape
    n_c = T // S5_L
    no = G_B // S5_OCT
    lanes = S5_OCT * S5_GROUP
    ow = S5_OCT * S5_W
    tc = min(n_c, TOKEN_TILE // B)
    u4 = ub.reshape(n_c, S5_L, B, D_B)
    u_spec = pl.BlockSpec((tc, S5_L, B, lanes), lambda q, i: (i, 0, 0, q))
    w_spec = lambda kind: pl.BlockSpec((None, None, ow, ow), lambda q, i: (kind, q, 0, 0))
    st_spec = pl.BlockSpec((tc, B, ow), lambda q, i: (i, 0, q))
    f_loc = pl.pallas_call(
        _s5_state_body,
        grid=(no, n_c // tc),
        in_specs=[u_spec, w_spec(1)],
        out_specs=st_spec,
        out_shape=jax.ShapeDtypeStruct((n_c, B, G_B * S5_W), F32),
        scratch_shapes=[pltpu.VMEM((tc * B, S5_L * lanes), BF16)],
        compiler_params=_cparams(("arbitrary", "arbitrary")),
        name="s5_state",
    )(u4, mge)
    sw = 4 * S5_W
    sin, sfin = pl.pallas_call(
        functools.partial(_s5_scan_body, n_c=n_c),
        grid=(G_B * S5_W // sw,),
        in_specs=[pl.BlockSpec((n_c, B, sw), lambda g: (0, 0, g)),
                  pl.BlockSpec((1, sw), lambda g: (0, g)),
                  pl.BlockSpec((B, sw), lambda g: (0, g))],
        out_specs=[pl.BlockSpec((n_c, B, sw), lambda g: (0, 0, g)),
                   pl.BlockSpec((B, sw), lambda g: (0, g))],
        out_shape=[jax.ShapeDtypeStruct((n_c, B, G_B * S5_W), F32),
                   jax.ShapeDtypeStruct((B, G_B * S5_W), F32)],
        compiler_params=_cparams(("arbitrary",)),
        name="s5_scan",
    )(f_loc, al, s0)
    y = pl.pallas_call(
        _s5_out_body,
        grid=(no, n_c // tc),
        in_specs=[u_spec, st_spec, w_spec(0), w_spec(2)],
        out_specs=u_spec,
        out_shape=jax.ShapeDtypeStruct((n_c, S5_L, B, D_B), F32),
        scratch_shapes=[pltpu.VMEM((tc * B, S5_L * lanes), BF16)],
        compiler_params=_cparams(("arbitrary", "arbitrary")),
        name="s5_out",
    )(u4, sin, mge, mge)
    return y.reshape(T, B, D_B), sfin


def _s5_state_to_lanes(s_re, s_im):
    parts = [s_re[:, 0], s_re[:, 1], s_im[:, 0], s_im[:, 1]]
    return jnp.concatenate(parts, axis=-1).reshape(s_re.shape[0], G_B * S5_W)


def _s5_lanes_to_state(s):
    s = s.reshape(s.shape[0], G_B, 4, P_B)
    return jnp.stack([s[:, :, 0], s[:, :, 1]], axis=1), jnp.stack([s[:, :, 2], s[:, :, 3]], axis=1)


def _outproj0_body(x_ref, h_ref, ga_ref, y_ref, ub_ref, g1_ref, d_ref, gw_ref, gb_ref, wo_ref,
                   lg_ref, lb_ref, o_ref):
    nb, tq, d = x_ref.shape
    rows = tq * nb

    def flat(v):
        return v.reshape(rows, v.shape[-1])

    ya = flat(h_ref[0] + h_ref[1]) * jax.nn.gelu(flat(ga_ref[...]))
    yb = flat(y_ref[...]) + d_ref[...] * flat(ub_ref[...])
    g = jax.nn.gelu(yb)
    gate = jax.nn.sigmoid(jnp.dot(g.astype(BF16), gw_ref[...], preferred_element_type=F32) + gb_ref[...])
    cat = jnp.concatenate([ya, g * gate], axis=1).astype(BF16)
    cat = jnp.dot(_row_permutation(nb, tq), cat, preferred_element_type=F32).astype(BF16)
    out = jnp.dot(cat, wo_ref[...], preferred_element_type=F32).reshape(nb, tq, d)
    v = ALPHA_DN * x_ref[...] + g1_ref[...] * out
    o_ref[...] = _layer_norm(v, lg_ref[...], lb_ref[...])


def _outproj0(x, h, ga, y5, ub, pmods, s5_d, glu_w, glu_b, w_out0, ln_g, ln_b):
    B, T, D = x.shape
    tq = TOKEN_TILE // B
    tmaj = pl.BlockSpec((tq, B, D_A), lambda i: (i, 0, 0))
    vec = lambda w: pl.BlockSpec((1, w), lambda i: (0, 0))
    return pl.pallas_call(
        _outproj0_body,
        grid=(T // tq,),
        in_specs=[pl.BlockSpec((B, tq, D), lambda i: (0, i, 0)),
                  pl.BlockSpec((2, tq, B, D_A), lambda i: (0, i, 0, 0)),
                  tmaj, tmaj, tmaj,
                  _mod_spec_all(0, 2, B),
                  vec(D_B),
                  pl.BlockSpec((D_B, D_B), lambda i: (0, 0)),
                  vec(D_B),
                  pl.BlockSpec((D, D), lambda i: (0, 0)),
                  vec(D), vec(D)],
        out_specs=pl.BlockSpec((B, tq, D), lambda i: (0, i, 0)),
        out_shape=jax.ShapeDtypeStruct((B, T, D), F32),
        compiler_params=_cparams(("arbitrary",)),
        name="outproj0",
    )(x, h, ga, y5, ub, pmods, s5_d.reshape(1, D_B), glu_w.astype(BF16), glu_b.reshape(1, D_B),
      w_out0.astype(BF16), ln_g.reshape(1, D), ln_b.reshape(1, D))


def _router_body(x_ref, sc_ref, sh_ref, wr_ref, br_ref, u_ref, ldest_ref, gate_ref, cnt_ref):
    u = x_ref[...] * (1.0 + sc_ref[...]) + sh_ref[...]
    u_ref[...] = u.astype(BF16)
    tm = u.shape[0]
    logits = lax.dot_general(wr_ref[...], u, (((1,), (1,)), ((), ())), precision=HI,
                             preferred_element_type=F32) + br_ref[...]
    e_iota = lax.broadcasted_iota(I32, logits.shape, 0)
    work = logits
    vals, hots = [], []
    for _ in range(TOP_K):
        m = jnp.max(work, axis=0, keepdims=True)
        idx = jnp.min(jnp.where(work == m, e_iota, N_EXP), axis=0, keepdims=True)
        hot = e_iota == idx
        vals.append(m)
        hots.append(hot)
        work = jnp.where(hot, -jnp.inf, work)
    ex = [jnp.exp(v - vals[0]) for v in vals]
    den = ex[0] + ex[1] + ex[2] + ex[3]
    gate_ref[...] = jnp.concatenate([e / den for e in ex], axis=0)
    hot_sum = jnp.zeros(logits.shape, F32)
    for hot in hots:
        hot_sum = hot_sum + hot.astype(F32)
    hot_b = hot_sum.astype(BF16)
    before = lax.broadcasted_iota(I32, (tm, tm), 0) < lax.broadcasted_iota(I32, (tm, tm), 1)
    excl = jnp.dot(hot_b, jnp.where(before, 1.0, 0.0).astype(BF16), preferred_element_type=F32)
    cnt_row = lax.dot_general(jnp.ones((8, tm), BF16), hot_b, (((1,), (1,)), ((), ())),
                              preferred_element_type=F32)[0:1]
    cnt_ref[...] = cnt_row.astype(I32)
    run_len = jnp.ceil(cnt_row * (1.0 / RUN_ALIGN)) * RUN_ALIGN
    lower = lax.broadcasted_iota(I32, (N_EXP, N_EXP), 1) < lax.broadcasted_iota(I32, (N_EXP, N_EXP), 0)
    run_off = jnp.sum(jnp.where(lower, run_len, 0.0), axis=1, keepdims=True)
    base = excl + run_off
    rows = [jnp.sum(jnp.where(hot, base, 0.0), axis=0, keepdims=True) for hot in hots]
    ldest_ref[...] = jnp.concatenate(rows, axis=0).astype(I32)


def _router(x2, pmods, layer, rows_per_mod, tm, w_router, b_router):
    N, D = x2.shape
    n_t = N // tm
    mod = lambda which: pl.BlockSpec((None, None, None, 1, D),
                                     lambda i: (layer, which, (i * tm) // rows_per_mod, 0, 0))
    lane_spec = pl.BlockSpec((TOP_K, tm), lambda i: (0, i))
    return pl.pallas_call(
        _router_body,
        grid=(n_t,),
        in_specs=[pl.BlockSpec((tm, D), lambda i: (i, 0)),
                  mod(4), mod(3),
                  pl.BlockSpec((N_EXP, D), lambda i: (0, 0)),
                  pl.BlockSpec((N_EXP, 1), lambda i: (0, 0))],
        out_specs=[pl.BlockSpec((tm, D), lambda i: (i, 0)),
                   lane_spec, lane_spec,
                   pl.BlockSpec((None, 1, N_EXP), lambda i: (i, 0, 0))],
        out_shape=[jax.ShapeDtypeStruct((N, D), BF16),
                   jax.ShapeDtypeStruct((TOP_K, N), I32),
                   jax.ShapeDtypeStruct((TOP_K, N), F32),
                   jax.ShapeDtypeStruct((n_t, 1, N_EXP), I32)],
        compiler_params=_cparams(("arbitrary",)),
        name="router",
    )(x2, pmods, pmods, w_router.T, b_router.reshape(N_EXP, 1))


def _expert_body(blk_e_ref, n_used_ref, xs_ref, wgu_ref, bgu_ref, wdn_ref, bdn_ref, o_ref, wgu_s, wdn_s):
    i = pl.program_id(0)
    prev = blk_e_ref[jnp.maximum(i - 1, 0)]
    changed = jnp.logical_or(i == 0, blk_e_ref[i] != prev)

    @pl.when(changed)
    def _():
        wgu_s[...] = wgu_ref[...].astype(BF16)
        wdn_s[...] = wdn_ref[...].astype(BF16)

    @pl.when(i < n_used_ref[0])
    def _():
        h = jnp.dot(xs_ref[...], wgu_s[...], preferred_element_type=F32) + bgu_ref[...]
        gt = jnp.minimum(h[:, :D_FF], SWIGLU_LIMIT)
        up = jnp.clip(h[:, D_FF:], -SWIGLU_LIMIT, SWIGLU_LIMIT)
        act = (up + 1.0) * gt * jax.nn.sigmoid(SWIGLU_ALPHA * gt)
        y = jnp.dot(act.astype(BF16), wdn_s[...], preferred_element_type=F32) + bdn_ref[...]
        o_ref[...] = y.astype(o_ref.dtype)

    @pl.when(i >= n_used_ref[0])
    def _():
        o_ref[...] = jnp.zeros_like(o_ref)


def _experts(xs, blk_e, n_used, layer, w_gu, b_gu, w_dn, b_dn):
    slots, D = xs.shape
    te = EXPERT_TILE
    n_blk = slots // te
    grid_spec = pltpu.PrefetchScalarGridSpec(
        num_scalar_prefetch=2,
        grid=(n_blk,),
        in_specs=[pl.BlockSpec((te, D), lambda i, be, nu: (i, 0)),
                  pl.BlockSpec((None, None, D, 2 * D_FF), lambda i, be, nu: (layer, be[i], 0, 0)),
                  pl.BlockSpec((None, None, 1, 2 * D_FF), lambda i, be, nu: (layer, be[i], 0, 0)),
                  pl.BlockSpec((None, None, D_FF, D), lambda i, be, nu: (layer, be[i], 0, 0)),
                  pl.BlockSpec((None, None, 1, D), lambda i, be, nu: (layer, be[i], 0, 0))],
        out_specs=pl.BlockSpec((te, D), lambda i, be, nu: (i, 0)),
        scratch_shapes=[pltpu.VMEM((D, 2 * D_FF), BF16), pltpu.VMEM((D_FF, D), BF16)],
    )
    return pl.pallas_call(
        _expert_body,
        grid_spec=grid_spec,
        out_shape=jax.ShapeDtypeStruct((slots, D), BF16),
        compiler_params=_cparams(("arbitrary",)),
        name="experts",
    )(blk_e, n_used, xs, w_gu, b_gu.reshape(DEPTH, N_EXP, 1, 2 * D_FF), w_dn, b_dn.reshape(DEPTH, N_EXP, 1, D))


def _chunk_copy(vmem_ref, hbm_ref, sem, k, dst_chunk, to_hbm):
    local = vmem_ref.at[pl.ds(pl.multiple_of(k * RUN_ALIGN, RUN_ALIGN), RUN_ALIGN)]
    remote = hbm_ref.at[pl.ds(pl.multiple_of(dst_chunk * RUN_ALIGN, RUN_ALIGN), RUN_ALIGN)]
    return pltpu.make_async_copy(local, remote, sem) if to_hbm else pltpu.make_async_copy(remote, local, sem)


def _dispatch_body(dst_ref, u_ref, ld_ref, xs_in_ref, xs_ref, buf_s, sem, *, n_chunks):
    del xs_in_ref
    i = pl.program_id(0)
    n_t = pl.num_programs(0)
    slot = i % 2
    tm = u_ref.shape[0]
    lb = n_chunks * RUN_ALIGN
    rb = 256
    ld = ld_ref[...]
    u = u_ref[...]
    for j in range(lb // rb):
        r = lax.broadcasted_iota(I32, (rb, tm), 0) + j * rb
        p = jnp.where(r == ld[0:1], 1.0, jnp.where(r == ld[1:2], 1.0, jnp.where(r == ld[2:3], 1.0,
                                                                                    jnp.where(r == ld[3:4], 1.0, 0.0))))
        buf_s[slot, j * rb:(j + 1) * rb, :] = jnp.dot(p.astype(BF16), u, preferred_element_type=F32).astype(BF16)

    def start(k, _):
        @pl.when(dst_ref[i, k] >= 0)
        def _():
            _chunk_copy(buf_s.at[slot], xs_ref, sem.at[slot], k, dst_ref[i, k], True).start()
        return 0

    lax.fori_loop(0, n_chunks, start, 0)

    def drain(step, s):
        def wait(k, _):
            @pl.when(dst_ref[step, k] >= 0)
            def _():
                _chunk_copy(buf_s.at[s], xs_ref, sem.at[s], k, dst_ref[step, k], True).wait()
            return 0
        lax.fori_loop(0, n_chunks, wait, 0)

    @pl.when(i > 0)
    def _():
        drain(i - 1, 1 - slot)

    @pl.when(i == n_t - 1)
    def _():
        drain(i, slot)


def _dispatch(u2, ldest, dst, xs, tm):
    N, D = u2.shape
    n_t, n_chunks = dst.shape
    lb = n_chunks * RUN_ALIGN
    grid_spec = pltpu.PrefetchScalarGridSpec(
        num_scalar_prefetch=1,
        grid=(n_t,),
        in_specs=[pl.BlockSpec((tm, D), lambda i, d: (i, 0)),
                  pl.BlockSpec((TOP_K, tm), lambda i, d: (0, i)),
                  pl.BlockSpec(memory_space=pl.ANY)],
        out_specs=pl.BlockSpec(memory_space=pl.ANY),
        scratch_shapes=[pltpu.VMEM((2, lb, D), BF16), pltpu.SemaphoreType.DMA((2,))],
    )
    return pl.pallas_call(
        functools.partial(_dispatch_body, n_chunks=n_chunks),
        grid_spec=grid_spec,
        out_shape=jax.ShapeDtypeStruct(xs.shape, xs.dtype),
        input_output_aliases={3: 0},
        compiler_params=_cparams(("arbitrary",)),
        name="moe_dispatch",
    )(dst, u2, ldest, xs)


def _combine_body(dst_ref, x_ref, ld_ref, gate_ref, g2_ref, lg_ref, lb_ref, ys_ref, o_ref, buf_s, w_s, sem,
                  *, n_chunks):
    i = pl.program_id(0)
    tm = x_ref.shape[0]
    lb = n_chunks * RUN_ALIGN

    def fetch(k, _):
        d = dst_ref[i, k]

        @pl.when(d >= 0)
        def _():
            _chunk_copy(buf_s, ys_ref, sem.at[0], k, d, False).start()

        @pl.when(d < 0)
        def _():
            buf_s[pl.ds(pl.multiple_of(k * RUN_ALIGN, RUN_ALIGN), RUN_ALIGN), :] = jnp.zeros(
                (RUN_ALIGN, buf_s.shape[1]), buf_s.dtype)
        return 0

    lax.fori_loop(0, n_chunks, fetch, 0)
    ld = ld_ref[...]
    gates = gate_ref[...]
    cb = 256
    for j in range(lb // cb):
        c = lax.broadcasted_iota(I32, (tm, cb), 1) + j * cb
        w = jnp.zeros((tm, cb), F32)
        for k in range(TOP_K):
            w = jnp.where(c == ld[:, k:k + 1], gates[:, k:k + 1], w)
        w_s[:, j * cb:(j + 1) * cb] = w.astype(BF16)

    def wait(k, _):
        @pl.when(dst_ref[i, k] >= 0)
        def _():
            _chunk_copy(buf_s, ys_ref, sem.at[0], k, dst_ref[i, k], False).wait()
        return 0

    lax.fori_loop(0, n_chunks, wait, 0)
    y = jnp.dot(w_s[...], buf_s[...], preferred_element_type=F32)
    v = ALPHA_DN * x_ref[...] + g2_ref[...] * y
    o_ref[...] = _layer_norm(v, lg_ref[...], lb_ref[...])


def _combine(x2, ldest_col, gates_col, dst, ys, pmods, layer, rows_per_mod, tm, ln_g, ln_b):
    N, D = x2.shape
    n_t, n_chunks = dst.shape
    lb = n_chunks * RUN_ALIGN
    vec = pl.BlockSpec((1, D), lambda i, d: (0, 0))
    grid_spec = pltpu.PrefetchScalarGridSpec(
        num_scalar_prefetch=1,
        grid=(n_t,),
        in_specs=[pl.BlockSpec((tm, D), lambda i, d: (i, 0)),
                  pl.BlockSpec((tm, TOP_K), lambda i, d: (i, 0)),
                  pl.BlockSpec((tm, TOP_K), lambda i, d: (i, 0)),
                  pl.BlockSpec((None, None, None, 1, D), lambda i, d: (layer, 5, (i * tm) // rows_per_mod, 0, 0)),
                  vec, vec,
                  pl.BlockSpec(memory_space=pl.ANY)],
        out_specs=pl.BlockSpec((tm, D), lambda i, d: (i, 0)),
        scratch_shapes=[pltpu.VMEM((lb, D), BF16), pltpu.VMEM((tm, lb), BF16), pltpu.SemaphoreType.DMA((1,))],
    )
    return pl.pallas_call(
        functools.partial(_combine_body, n_chunks=n_chunks),
        grid_spec=grid_spec,
        out_shape=jax.ShapeDtypeStruct((N, D), F32),
        compiler_params=_cparams(("arbitrary",)),
        name="moe_combine",
    )(dst, x2, ldest_col, gates_col, pmods, ln_g.reshape(1, D), ln_b.reshape(1, D), ys)


def _moe_plan(cnts, n_chunks, te):
    a = RUN_ALIGN
    cnt = jnp.concatenate(cnts, axis=0)
    pc = (cnt + a - 1) // a * a
    seg = (jnp.sum(pc, axis=0) + te - 1) // te * te
    pad_end = jnp.cumsum(seg)
    run_start = (pad_end - seg)[None, :] + jnp.cumsum(pc, axis=0) - pc
    lo = jnp.cumsum(pc, axis=1) - pc
    tables, t0 = [], 0
    for c, nc in zip(cnts, n_chunks):
        sl = slice(t0, t0 + c.shape[0])
        t0 += c.shape[0]
        pos = jnp.arange(nc, dtype=I32) * a
        owner = jnp.sum(((lo[sl] + pc[sl])[:, None, :] <= pos[None, :, None]).astype(I32), axis=-1)
        mine = owner[..., None] == jnp.arange(N_EXP, dtype=I32)
        base = jnp.sum(jnp.where(mine, (run_start[sl] - lo[sl])[:, None, :], 0), axis=-1)
        valid = pos[None, :] < jnp.sum(pc[sl], axis=1, keepdims=True)
        tables.append(jnp.where(valid, (base + pos[None, :]) // a, -1).astype(I32))
    return tables, pad_end


def _moe(passes, layer, P):
    te = EXPERT_TILE
    routed = []
    for x, pmods, shared_mod in passes:
        B, T, D = x.shape
        N = B * T
        tm = min(MOE_TILE, N if shared_mod else T)
        x2 = x.reshape(N, D)
        u2, ldest, gates, cnt = _router(x2, pmods, layer, T, tm, P['w_router'][layer], P['b_router'][layer])
        routed.append((x2, pmods, T, tm, u2, ldest, gates, cnt[:, 0, :], x.shape))
    n_chunks = [r[3] * TOP_K // RUN_ALIGN + N_EXP for r in routed]
    n_tiles = [r[7].shape[0] for r in routed]
    dsts, pad_end = _moe_plan([r[7] for r in routed], n_chunks, te)
    n_assign = sum(r[0].shape[0] for r in routed) * TOP_K
    slots = -(-(n_assign + sum(n_tiles) * N_EXP * RUN_ALIGN + N_EXP * te) // te) * te
    n_blk = slots // te
    blk_pos = jnp.arange(n_blk, dtype=I32) * te
    blk_e = jnp.minimum(jnp.sum((pad_end[None, :] <= blk_pos[:, None]).astype(I32), axis=1), N_EXP - 1)
    n_used = (pad_end[-1] // te).astype(I32).reshape(1)
    xs = jnp.zeros((slots, D_MODEL), BF16)
    for r, dst in zip(routed, dsts):
        xs = _dispatch(r[4], r[5], dst, xs, r[3])
    ys = _experts(xs, blk_e, n_used, layer, P['w_gu'], P['b_gu'], P['w_down'], P['b_down'])
    outs = []
    for r, dst in zip(routed, dsts):
        x2, pmods, T, tm, _, ldest, gates, _, shape = r
        y = _combine(x2, ldest.T, gates.T, dst, ys, pmods, layer, T, tm, P['ln_g'][layer, 1], P['ln_b'][layer, 1])
        outs.append(y.reshape(shape))
    return outs


def _split3(x):
    hi = x.astype(BF16)
    r1 = x - hi.astype(F32)
    mid = r1.astype(BF16)
    lo = (r1 - mid.astype(F32)).astype(BF16)
    return jnp.concatenate([hi, mid, lo], axis=1)


def _inproj1_body(x_ref, sc_ref, sh_ref, w_ref, wg2_ref, wgh_ref, bg_ref,
                  qk_ref, v_ref, o_ref, gcol_ref, grow_ref):
    u = x_ref[...] * (1.0 + sc_ref[...]) + sh_ref[...]
    ub = u.astype(BF16)
    z = jnp.dot(ub, w_ref[...], preferred_element_type=F32)
    qk_ref[...] = z[:, :2 * D_MODEL]
    v_ref[...] = z[:, 2 * D_MODEL:3 * D_MODEL].astype(BF16)
    o_ref[...] = z[:, 3 * D_MODEL:]
    tm = u.shape[0]
    nh = 4 * H_C
    u_lo = (u - ub.astype(F32)).astype(BF16)
    g2 = jnp.dot(ub, wg2_ref[...], preferred_element_type=F32)
    gc = g2[:, :nh] + g2[:, nh:] + jnp.dot(u_lo, wgh_ref[...], preferred_element_type=F32) + bg_ref[...]
    r_i = lax.broadcasted_iota(I32, (tm, tm), 0)
    c_i = lax.broadcasted_iota(I32, (tm, tm), 1)
    same = (r_i // CHUNK) == (c_i // CHUNK)
    tri_f = jnp.where(jnp.logical_and(same, c_i <= r_i), 1.0, 0.0).astype(BF16)
    tri_b = jnp.where(jnp.logical_and(same, c_i >= r_i), 1.0, 0.0).astype(BF16)
    lf3 = _split3(jax.nn.log_sigmoid(gc))

    def sum3(p, axis):
        if axis == 1:
            return p[:, :nh] + p[:, nh:2 * nh] + p[:, 2 * nh:]
        return p[:nh] + p[nh:2 * nh] + p[2 * nh:]

    col_i = lax.broadcasted_iota(I32, (tm, nh), 1)
    cum = jnp.where(col_i < 2 * H_C,
                    sum3(jnp.dot(tri_f, lf3, preferred_element_type=F32), 1),
                    sum3(jnp.dot(tri_b, lf3, preferred_element_type=F32), 1))
    gcol = jnp.where((col_i // H_C) % 2 == 1, cum, gc)
    gcol_ref[...] = gcol
    eye = jnp.where(r_i == c_i, 1.0, 0.0).astype(BF16)
    gt3 = lax.dot_general(_split3(gcol), eye, (((0,), (0,)), ((), ())), preferred_element_type=F32)
    grow_ref[...] = sum3(gt3, 0)


def _inproj1(x, mods, w_in1, w_gate1, b_gate1, row_off, row_stride):
    B, T, D = x.shape
    tm = min(TOKEN_TILE, T)
    n_j = T // tm
    nh = 4 * H_C
    wg_hi = w_gate1.astype(BF16)
    wg_lo = (w_gate1 - wg_hi.astype(F32)).astype(BF16)
    tok = lambda w: pl.BlockSpec((None, tm, w), lambda b, j: (b, j, 0))
    return pl.pallas_call(
        _inproj1_body,
        grid=(B, n_j),
        in_specs=[tok(D),
                  _mod_spec(1, 1, row_off, row_stride),
                  _mod_spec(1, 0, row_off, row_stride),
                  pl.BlockSpec((D, 4 * D), lambda b, j: (0, 0)),
                  pl.BlockSpec((D, 2 * nh), lambda b, j: (0, 0)),
                  pl.BlockSpec((D, nh), lambda b, j: (0, 0)),
                  pl.BlockSpec((1, nh), lambda b, j: (0, 0))],
        out_specs=[tok(2 * D), tok(D), tok(D), tok(nh),
                   pl.BlockSpec((nh, tm), lambda b, j: (0, b * n_j + j))],
        out_shape=[jax.ShapeDtypeStruct((B, T, 2 * D), F32),
                   jax.ShapeDtypeStruct((B, T, D), BF16),
                   jax.ShapeDtypeStruct((B, T, D), F32),
                   jax.ShapeDtypeStruct((B, T, nh), F32),
                   jax.ShapeDtypeStruct((nh, B * T), F32)],
        compiler_params=_cparams(("arbitrary", "arbitrary")),
        name="inproj1",
    )(x, mods, mods, w_in1.astype(BF16), jnp.concatenate([wg_hi, wg_lo], axis=1), wg_hi, b_gate1.reshape(1, nh))


def _mlstm_body(q_ref, k_ref, v_ref, gcol_ref, grow_ref, cw_ref, cb_ref, c0_ref, n0_ref, m0_ref,
                h_ref, cfin_ref, nfin_ref, mfin_ref, xpad_s, q_s, k_s, *, T):
    n_c = T // CHUNK
    L = CHUNK

    rb = min(T, 512)
    pad = 8

    def conv_silu_into(src_ref, w, bias, dst_s, scale):
        xpad_s[0:pad] = jnp.zeros((pad, DH_C), F32)
        xpad_s[T + pad:T + 2 * pad] = jnp.zeros((pad, DH_C), F32)
        xpad_s[pad:T + pad] = src_ref[...]

        def blk(r, _):
            off = pl.multiple_of(r * rb, pad)
            win = xpad_s[pl.ds(off, rb + 2 * pad), :]
            acc = (pltpu.roll(win, 2, 0) * w[0:1] + pltpu.roll(win, 1, 0) * w[1:2] + win * w[2:3]
                   + pltpu.roll(win, rb + 2 * pad - 1, 0) * w[3:4])[pad:rb + pad] + bias
            dst_s[pl.ds(off, rb), :] = (jax.nn.silu(acc) * scale).astype(BF16)
            return 0

        lax.fori_loop(0, T // rb, blk, 0)

    conv_silu_into(q_ref, cw_ref[0], cb_ref[0], q_s, 1.0)
    conv_silu_into(k_ref, cw_ref[1], cb_ref[1], k_s, DH_C ** -0.5)

    t_i = lax.broadcasted_iota(I32, (L, L), 0)
    s_i = lax.broadcasted_iota(I32, (L, L), 1)

    cfin_ref[...] = c0_ref[...]
    nfin_ref[...] = n0_ref[...]
    mfin_ref[...] = m0_ref[...]

    def chunk(d, c):
        mask = (s_i <= t_i) if d == 0 else (s_i >= t_i)
        last = L - 1 if d == 0 else 0
        off = pl.multiple_of(c * L, L)
        qc = q_s[pl.ds(off, L), :]
        kc = k_s[pl.ds(off, L), :]
        vc = v_ref[pl.ds(off, L), :]
        gcol = gcol_ref[pl.ds(off, L), :]
        grow = grow_ref[:, pl.ds(off, L)]
        li_c = gcol[:, 2 * d:2 * d + 1]
        b_c = gcol[:, 2 * d + 1:2 * d + 2]
        li_r = grow[2 * d:2 * d + 1, :]
        b_r = grow[2 * d + 1:2 * d + 2, :]
        m_prev = mfin_ref[d]
        log_d = jnp.where(mask, b_c - b_r + li_r, -jnp.inf)
        m_inter = b_c + m_prev
        m_t = jnp.maximum(m_inter, jnp.max(log_d, axis=-1, keepdims=True))
        dmat = jnp.exp(log_d - m_t)
        w_inter = jnp.exp(m_inter - m_t)
        c_old = cfin_ref[d]
        s = lax.dot_general(qc, kc, (((1,), (1,)), ((), ())), preferred_element_type=F32) * dmat
        inter = lax.dot_general(qc, c_old.astype(BF16), (((1,), (1,)), ((), ())), preferred_element_type=F32)
        num = w_inter * inter + jnp.dot(s.astype(BF16), vc, preferred_element_type=F32)
        qn = jnp.sum(qc.astype(F32) * nfin_ref[d], axis=-1, keepdims=True)
        den = w_inter * qn + jnp.sum(s, axis=-1, keepdims=True)
        h = num / jnp.maximum(jnp.abs(den), jnp.exp(-m_t))
        m_new = m_t[last:last + 1, :]
        b_last = b_c[last:last + 1, :]
        w_s = jnp.exp(b_last - b_c + li_c - m_new)
        decay = jnp.exp(b_last + m_prev - m_new)
        wv = (w_s * vc.astype(F32)).astype(BF16)
        cfin_ref[d] = decay * c_old + lax.dot_general(wv, kc, (((0,), (0,)), ((), ())),
                                                      preferred_element_type=F32)
        nfin_ref[d] = decay * nfin_ref[d] + jnp.sum(w_s * kc.astype(F32), axis=0, keepdims=True)
        mfin_ref[d] = m_new
        return off, h

    def both(ci, _):
        off_f, h_f = chunk(0, ci)
        off_b, h_b = chunk(1, n_c - 1 - ci)
        h_ref[pl.ds(off_f, L), :] = h_f
        xpad_s[pl.ds(off_b, L), :] = h_b
        return 0

    lax.fori_loop(0, n_c, both, 0)

    def add_bwd(r, _):
        off = pl.multiple_of(r * rb, pad)
        h_ref[pl.ds(off, rb), :] = h_ref[pl.ds(off, rb), :] + xpad_s[pl.ds(off, rb), :]
        return 0

    lax.fori_loop(0, T // rb, add_bwd, 0)


def _mlstm(qk, v, gcol, grow, conv_w, conv_b, c0, n0, m0):
    B, T, _ = v.shape
    DH = DH_C
    cw = conv_w.reshape(CONV_W, 2, H_C, DH).transpose(2, 1, 0, 3)
    cb = conv_b.reshape(2, H_C, 1, DH).transpose(1, 0, 2, 3)
    st = lambda *tail: pl.BlockSpec((None, 2, None) + tail, lambda b, h: (b, 0, h) + (0,) * len(tail))
    return pl.pallas_call(
        functools.partial(_mlstm_body, T=T),
        grid=(B, H_C),
        in_specs=[pl.BlockSpec((None, T, DH), lambda b, h: (b, 0, h)),
                  pl.BlockSpec((None, T, DH), lambda b, h: (b, 0, H_C + h)),
                  pl.BlockSpec((None, T, DH), lambda b, h: (b, 0, h)),
                  pl.BlockSpec((None, None, T, 4), lambda b, h: (b, h, 0, 0)),
                  pl.BlockSpec((None, None, 4, T), lambda b, h: (b, h, 0, 0)),
                  pl.BlockSpec((None, 2, CONV_W, DH), lambda b, h: (h, 0, 0, 0)),
                  pl.BlockSpec((None, 2, 1, DH), lambda b, h: (h, 0, 0, 0)),
                  st(DH, DH), st(1, DH), st(1, 1)],
        out_specs=[pl.BlockSpec((None, T, DH), lambda b, h: (b, 0, h)),
                   st(DH, DH), st(1, DH), st(1, 1)],
        out_shape=[jax.ShapeDtypeStruct((B, T, D_MODEL), F32),
                   jax.ShapeDtypeStruct((B, 2, H_C, DH, DH), F32),
                   jax.ShapeDtypeStruct((B, 2, H_C, 1, DH), F32),
                   jax.ShapeDtypeStruct((B, 2, H_C, 1, 1), F32)],
        scratch_shapes=[pltpu.VMEM((T + 16, DH), F32), pltpu.VMEM((T, DH), BF16), pltpu.VMEM((T, DH), BF16)],
        compiler_params=_cparams(("arbitrary", "arbitrary")),
        name="mlstm",
    )(qk, qk, v, gcol, grow, cw, cb, c0, n0, m0)


def _outproj1_body(x_ref, h_ref, o_ref, g1_ref, w_ref, lg_ref, lb_ref, out_ref):
    y = jax.nn.sigmoid(o_ref[...]) * h_ref[...]
    out = jnp.dot(y.astype(BF16), w_ref[...], preferred_element_type=F32)
    v = ALPHA_DN * x_ref[...] + g1_ref[...] * out
    out_ref[...] = _layer_norm(v, lg_ref[...], lb_ref[...])


def _outproj1(x, h, o, mods, w_out1, ln_g, ln_b, row_off, row_stride):
    B, T, D = x.shape
    tm = min(TOKEN_TILE, T)
    tok = pl.BlockSpec((None, tm, D), lambda b, j: (b, j, 0))
    vec = pl.BlockSpec((1, D), lambda b, j: (0, 0))
    return pl.pallas_call(
        _outproj1_body,
        grid=(B, T // tm),
        in_specs=[tok, tok, tok, _mod_spec(1, 2, row_off, row_stride),
                  pl.BlockSpec((D, D), lambda b, j: (0, 0)), vec, vec],
        out_specs=tok,
        out_shape=jax.ShapeDtypeStruct((B, T, D), F32),
        compiler_params=_cparams(("arbitrary", "arbitrary")),
        name="outproj1",
    )(x, h, o, mods, w_out1.astype(BF16), ln_g.reshape(1, D), ln_b.reshape(1, D))


def _to_col_major(x):
    B, T, C = x.shape
    rows = T // GRID_W
    return x.reshape(B, rows, GRID_W, C).transpose(0, 2, 1, 3).reshape(B, T, C)


def _to_row_major(x):
    B, T, C = x.shape
    rows = T // GRID_W
    return x.reshape(B, GRID_W, rows, C).transpose(0, 2, 1, 3).reshape(B, T, C)


def _mixer0(x, mods, st, P, s5_mats):
    h_rg, s_re, s_im = st
    xa, ga, ub = _inproj0(x, mods, P['w_in0'][0].astype(BF16))
    h, h_fin = _rglru(xa, P['conv_a_w'][0], P['conv_a_b'][0], P['rg_wa'][0], P['rg_ba'][0],
                      P['rg_wi'][0], P['rg_bi'][0], P['rg_lam'][0], h_rg[:, 0].transpose(1, 0, 2))
    y5, s_fin = _s5(ub, s5_mats, _s5_state_to_lanes(s_re[:, 0], s_im[:, 0]))
    x = _outproj0(x, h, ga, y5, ub, mods, P['s5_d'][0], P['glu_w'][0], P['glu_b'][0], P['w_out0'][0],
                  P['ln_g'][0, 0], P['ln_b'][0, 0])
    new_re, new_im = _s5_lanes_to_state(s_fin)
    return x, (h_fin.transpose(1, 0, 2)[:, None], new_re[:, None], new_im[:, None])


def _mixer1(x, mods, st, P):
    m_c, m_n, m_m = st
    B, T, D = x.shape
    qk, v, o, gcol, grow = _inproj1(x, mods, P['w_in1'][0], P['w_gate1'][0], P['b_gate1'][0], 0, 1)
    gcol = gcol.reshape(B, T, 2, 2, H_C).transpose(0, 4, 1, 2, 3).reshape(B, H_C, T, 4)
    grow = grow.reshape(2, 2, H_C, B, T).transpose(3, 2, 0, 1, 4).reshape(B, H_C, 4, T)
    h, c_fin, n_fin, m_fin = _mlstm(qk, v, gcol, grow, P['conv_c_w'][0], P['conv_c_b'][0],
                                    m_c[:, 0], m_n[:, 0][:, :, :, None, :], m_m[:, 0][:, :, :, None, None])
    x = _outproj1(x, h, o, mods, P['w_out1'][0], P['ln_g'][1, 0], P['ln_b'][1, 0], 0, 1)
    return x, (c_fin[:, None], n_fin[:, None, :, :, 0, :], m_fin[:, None, :, :, 0, 0])


def _forward(x_prompt, x_sample, c, c_ctx, states, P):
    bp = x_prompt.shape[0]
    bs = x_sample.shape[0]
    rows = 1 + bs
    rpad = -(-rows // 8) * 8
    cv = jnp.concatenate([c_ctx[None, :], c, jnp.zeros((rpad - rows, D_MODEL), F32)], axis=0)
    mods = _modulation(cv, P['w_mod'], P['b_mod'])
    s5_mats = _s5_matrices(P['s5_a_re'][0], P['s5_a_im'][0], P['s5_log_dt'][0], P['s5_b_re'][0], P['s5_b_im'][0],
                           P['s5_c_re'][0], P['s5_c_im'][0])
    zero_state = (jnp.zeros((bp, 1, 2, D_A), F32),
                  jnp.zeros((bp, 1, 2, G_B, P_B), F32),
                  jnp.zeros((bp, 1, 2, G_B, P_B), F32),
                  jnp.zeros((bp, 1, 2, H_C, DH_C, DH_C), F32),
                  jnp.zeros((bp, 1, 2, H_C, DH_C), F32),
                  jnp.zeros((bp, 1, 2, H_C), F32))
    mods_ctx = jnp.broadcast_to(mods[:, :, 0:1], mods.shape[:2] + (bp,) + mods.shape[3:])
    mods_lat = mods[:, :, 1:1 + bs]
    xc, new_even = _mixer0(x_prompt, mods_ctx, zero_state[:3], P, s5_mats)
    xl, _ = _mixer0(x_sample, mods_lat, states[:3], P, s5_mats)
    xc, xl = _moe([(xc, mods_ctx, True), (xl, mods_lat, False)], 0, P)
    xl = _to_col_major(xl)
    xc, new_odd = _mixer1(xc, mods_ctx, zero_state[3:], P)
    xl, _ = _mixer1(xl, mods_lat, states[3:], P)
    xc, xl = _moe([(xc, mods_ctx, True), (xl, mods_lat, False)], 1, P)
    return (xc, _to_row_major(xl)) + tuple(new_even) + tuple(new_odd)


def kernel(x_prompt, x_sample, c, c_ctx, state_rglru, state_s5_re, state_s5_im, state_mlstm_C, state_mlstm_n, state_mlstm_m, w_mod, b_mod, ln_g, ln_b, w_in0, conv_a_w, conv_a_b, rg_wa, rg_ba, rg_wi, rg_bi, rg_lam, s5_a_re, s5_a_im, s5_log_dt, s5_b_re, s5_b_im, s5_c_re, s5_c_im, s5_d, glu_w, glu_b, w_out0, w_in1, w_gate1, b_gate1, conv_c_w, conv_c_b, w_out1, w_router, b_router, w_gu, b_gu, w_down, b_down):
    P = dict(w_mod=w_mod, b_mod=b_mod, ln_g=ln_g, ln_b=ln_b, w_in0=w_in0, conv_a_w=conv_a_w,
             conv_a_b=conv_a_b, rg_wa=rg_wa, rg_ba=rg_ba, rg_wi=rg_wi, rg_bi=rg_bi, rg_lam=rg_lam,
             s5_a_re=s5_a_re, s5_a_im=s5_a_im, s5_log_dt=s5_log_dt, s5_b_re=s5_b_re, s5_b_im=s5_b_im,
             s5_c_re=s5_c_re, s5_c_im=s5_c_im, s5_d=s5_d, glu_w=glu_w, glu_b=glu_b, w_out0=w_out0,
             w_in1=w_in1, w_gate1=w_gate1, b_gate1=b_gate1, conv_c_w=conv_c_w, conv_c_b=conv_c_b,
             w_out1=w_out1, w_router=w_router, b_router=b_router, w_gu=w_gu, b_gu=b_gu,
             w_down=w_down, b_down=b_down)
    states = (state_rglru, state_s5_re, state_s5_im, state_mlstm_C, state_mlstm_n, state_mlstm_m)
    return _forward(x_prompt, x_sample, c, c_ctx, states, P)
```

```python
import functools
import math

import jax
import jax.numpy as jnp
from jax import lax
from jax.experimental import pallas as pl
from jax.experimental.pallas import tpu as pltpu

F32 = jnp.float32
BF16 = jnp.bfloat16
I32 = jnp.int32
HI = lax.Precision.HIGHEST

D_MODEL = 1024
DEPTH = 2
GRID_W = 64
D_A = 512
NB_A = 8
BS_A = D_A // NB_A
CONV_W = 4
LRU_C = 8.0
D_B = 512
S5_GROUP = 16
G_B = D_B // S5_GROUP
P_B = 64
H_C = 4
DH_C = D_MODEL // H_C
CHUNK = 128
N_EXP = 32
TOP_K = 4
D_FF = D_MODEL
SWIGLU_LIMIT = 7.0
SWIGLU_ALPHA = 1.702
ALPHA_DN = (2 * DEPTH) ** 0.25
LN_EPS = 1e-5

S5_L = 16
S5_W = S5_L * S5_GROUP
S5_OCT = 8
TOKEN_TILE = 256
EXPERT_TILE = 256
MOE_TILE = 512
RUN_ALIGN = 16
VMEM_LIMIT = 56 * 1024 * 1024


def _cparams(sem, vmem=VMEM_LIMIT):
    return pltpu.CompilerParams(dimension_semantics=sem, vmem_limit_bytes=vmem)


def _layer_norm(v, g, b):
    mu = jnp.mean(v, axis=-1, keepdims=True)
    c = v - mu
    var = jnp.mean(c * c, axis=-1, keepdims=True)
    return c * lax.rsqrt(var + LN_EPS) * g + b


def _mod_spec(layer, which, row_off, row_stride):
    return pl.BlockSpec((None, None, None, 1, D_MODEL),
                        lambda b, j: (layer, which, row_off + b * row_stride, 0, 0))


def _mod_spec_all(layer, which, nb):
    return pl.BlockSpec((None, None, nb, 1, D_MODEL), lambda i: (layer, which, 0, 0, 0))


def _row_permutation(n_out_major, n_out_minor):
    n = n_out_major * n_out_minor
    r_out = lax.broadcasted_iota(I32, (n, n), 0)
    r_in = lax.broadcasted_iota(I32, (n, n), 1)
    hit = jnp.logical_and(r_out // n_out_minor == r_in % n_out_major, r_out % n_out_minor == r_in // n_out_major)
    return jnp.where(hit, 1.0, 0.0).astype(BF16)


def _mod_body(c_ref, w_ref, b_ref, o_ref):
    s = jax.nn.silu(c_ref[...])
    o_ref[...] = jnp.dot(s, w_ref[...], precision=HI, preferred_element_type=F32) + b_ref[...]


def _modulation(cv, w_mod, b_mod):
    R, D = cv.shape
    L, _, N6 = w_mod.shape
    tn = N6 // 6
    out = pl.pallas_call(
        _mod_body,
        grid=(L, N6 // tn),
        in_specs=[pl.BlockSpec((R, D), lambda l, j: (0, 0)),
                  pl.BlockSpec((None, D, tn), lambda l, j: (l, 0, j)),
                  pl.BlockSpec((None, 1, tn), lambda l, j: (l, 0, j))],
        out_specs=pl.BlockSpec((None, R, tn), lambda l, j: (l, 0, j)),
        out_shape=jax.ShapeDtypeStruct((L, R, N6), F32),
        compiler_params=_cparams(("arbitrary", "arbitrary")),
        name="modulation",
    )(cv, w_mod, b_mod.reshape(L, 1, N6))
    return out.reshape(L, R, 6, D).transpose(0, 2, 1, 3)[:, :, :, None, :]


def _inproj0_body(x_ref, sc_ref, sh_ref, w_ref, xa_ref, ga_ref, ub_ref):
    nb, tq, d = x_ref.shape
    u = x_ref[...] * (1.0 + sc_ref[...]) + sh_ref[...]
    ub = u.reshape(nb * tq, d).astype(BF16)
    ut = jnp.dot(_row_permutation(tq, nb), ub, preferred_element_type=F32).astype(BF16)
    z = jnp.dot(ut, w_ref[...], preferred_element_type=F32).reshape(tq, nb, w_ref.shape[1])
    xa_ref[...] = z[:, :, :D_A]
    ga_ref[...] = z[:, :, D_A:2 * D_A]
    ub_ref[...] = z[:, :, 2 * D_A:]


def _inproj0(x, pmods, w_in0):
    B, T, D = x.shape
    tq = TOKEN_TILE // B
    nz = w_in0.shape[1]
    tmaj = pl.BlockSpec((tq, B, D_A), lambda i: (i, 0, 0))
    return pl.pallas_call(
        _inproj0_body,
        grid=(T // tq,),
        in_specs=[pl.BlockSpec((B, tq, D), lambda i: (0, i, 0)),
                  _mod_spec_all(0, 1, B), _mod_spec_all(0, 0, B),
                  pl.BlockSpec((D, nz), lambda i: (0, 0))],
        out_specs=[tmaj, tmaj, tmaj],
        out_shape=[jax.ShapeDtypeStruct((T, B, D_A), F32)] * 3,
        compiler_params=_cparams(("arbitrary",)),
        name="inproj0",
    )(x, pmods, pmods, w_in0)


def _rglru_body(cur_ref, prev_ref, next_ref, cw_ref, cb_ref, wa_ref, wi_ref, ba_ref, bi_ref, lam_ref, h0_ref,
                h_ref, hfin_ref, ext_s, a_s, b_s, carry_s, *, tt, n_t):
    d = pl.program_id(0)
    j = pl.program_id(1)
    jj = j + d * (n_t - 1 - 2 * j)
    nb = cur_ref.shape[1]
    half = D_A // 2
    ext_s[0:2] = jnp.where(jj == 0, 0.0, prev_ref[...])
    ext_s[2:tt + 2] = cur_ref[...]
    ext_s[tt + 2:tt + 3] = jnp.where(jj == n_t - 1, 0.0, next_ref[...])
    xc = ext_s[0:tt] * cw_ref[0]
    for k in range(1, CONV_W):
        xc = xc + ext_s[k:k + tt] * cw_ref[k]
    xc = xc + cb_ref[...]
    x2 = xc.reshape(tt * nb, D_A)
    xb = x2.astype(BF16)

    def gate(w_ref, bias_ref):
        lo = jnp.dot(xb[:, :half], w_ref[0], preferred_element_type=F32)
        hi = jnp.dot(xb[:, half:], w_ref[1], preferred_element_type=F32)
        return jax.nn.sigmoid(jnp.concatenate([lo, hi], axis=1) + bias_ref[...])

    r = gate(wa_ref, ba_ref)
    i = gate(wi_ref, bi_ref)
    log_a = LRU_C * r * jax.nn.log_sigmoid(lam_ref[...])
    a = jnp.exp(log_a)
    one_minus_a2 = -jnp.tanh(log_a) * (a * a + 1.0)
    a_s[...] = a.reshape(tt, nb, D_A)
    b_s[...] = (jnp.sqrt(one_minus_a2) * (i * x2)).reshape(tt, nb, D_A)

    @pl.when(j == 0)
    def _():
        carry_s[...] = h0_ref[...]

    def step(t, h):
        h = a_s[t] * h + b_s[t]
        h_ref[t] = h
        return h

    @pl.when(d == 0)
    def _():
        carry_s[...] = lax.fori_loop(0, tt, step, carry_s[...], unroll=8)

    @pl.when(d == 1)
    def _():
        carry_s[...] = lax.fori_loop(0, tt, lambda t, h: step(tt - 1 - t, h), carry_s[...], unroll=8)

    hfin_ref[...] = carry_s[...]


def _block_diag_halves(w):
    nd = w.shape[0]
    per = NB_A // 2
    w = w.reshape(nd, 2, per, BS_A, BS_A)
    eye = jnp.eye(per, dtype=w.dtype)
    out = jnp.einsum('dhncz,nm->dhncmz', w, eye)
    return out.reshape(nd, 2, per * BS_A, per * BS_A)


def _rglru(x3, conv_w, conv_b, wa, ba, wi, bi, lam, h0):
    T, B, _ = x3.shape
    tt = min(T, 1024 // B)
    n_t = T // tt

    def cur_map(d, j):
        return (j + d * (n_t - 1 - 2 * j), 0, 0)

    def prev_map(d, j):
        jj = j + d * (n_t - 1 - 2 * j)
        return (jnp.maximum(jj * (tt // 2) - 1, 0), 0, 0)

    def next_map(d, j):
        jj = j + d * (n_t - 1 - 2 * j)
        return (jnp.minimum((jj + 1) * tt, T - 1), 0, 0)

    dir_spec3 = pl.BlockSpec((None, 1, D_A), lambda d, j: (d, 0, 0))
    h, hfin = pl.pallas_call(
        functools.partial(_rglru_body, tt=tt, n_t=n_t),
        grid=(2, n_t),
        in_specs=[pl.BlockSpec((tt, B, D_A), cur_map),
                  pl.BlockSpec((2, B, D_A), prev_map),
                  pl.BlockSpec((1, B, D_A), next_map),
                  pl.BlockSpec((CONV_W, 1, D_A), lambda d, j: (0, 0, 0)),
                  pl.BlockSpec((1, 1, D_A), lambda d, j: (0, 0, 0)),
                  pl.BlockSpec((None, 2, D_A // 2, D_A // 2), lambda d, j: (d, 0, 0, 0)),
                  pl.BlockSpec((None, 2, D_A // 2, D_A // 2), lambda d, j: (d, 0, 0, 0)),
                  dir_spec3, dir_spec3, dir_spec3,
                  pl.BlockSpec((None, B, D_A), lambda d, j: (d, 0, 0))],
        out_specs=[pl.BlockSpec((None, tt, B, D_A), lambda d, j: (d,) + cur_map(d, j)),
                   pl.BlockSpec((None, B, D_A), lambda d, j: (d, 0, 0))],
        out_shape=[jax.ShapeDtypeStruct((2, T, B, D_A), F32),
                   jax.ShapeDtypeStruct((2, B, D_A), F32)],
        scratch_shapes=[pltpu.VMEM((tt + 3, B, D_A), F32),
                        pltpu.VMEM((tt, B, D_A), F32),
                        pltpu.VMEM((tt, B, D_A), F32),
                        pltpu.VMEM((B, D_A), F32)],
        compiler_params=_cparams(("arbitrary", "arbitrary")),
        name="rglru",
    )(x3, x3, x3, conv_w.reshape(CONV_W, 1, D_A), conv_b.reshape(1, 1, D_A),
      _block_diag_halves(wa).astype(BF16), _block_diag_halves(wi).astype(BF16),
      ba.reshape(2, 1, D_A), bi.reshape(2, 1, D_A), lam.reshape(2, 1, D_A), h0)
    return h, hfin


def _s5_matrices(a_re, a_im, log_dt, b_re, b_im, c_re, c_im):
    L = S5_L
    lam = lax.complex(a_re.astype(F32), a_im.astype(F32))
    dt = jnp.exp(log_dt.astype(F32))[..., None]
    ldt = lam * dt
    a_bar = jnp.exp(ldt)
    b_bar = ((a_bar - 1.0) / lam)[..., None] * lax.complex(b_re.astype(F32), b_im.astype(F32))
    cc = lax.complex(c_re.astype(F32), c_im.astype(F32))
    ks = jnp.arange(L + 1, dtype=F32)
    pw = jnp.exp(ldt[:, :, None, :] * ks[None, None, :, None].astype(jnp.complex64))
    kern = jnp.real(jnp.einsum('dgjp,dgkp,dgpi->dgkji', cc, pw[:, :, :L], b_bar, precision=HI))
    s_idx = jnp.arange(L)[:, None]
    t_idx = jnp.arange(L)[None, :]
    lag_f = jnp.clip(t_idx - s_idx, 0, L - 1)
    lag_b = jnp.clip(s_idx - t_idx, 0, L - 1)
    m_f = jnp.where((t_idx >= s_idx)[None, :, :, None, None], kern[0][:, lag_f], 0.0)
    m_b = jnp.where((s_idx >= t_idx)[None, :, :, None, None], kern[1][:, lag_b], 0.0)
    m = (m_f + m_b).transpose(0, 1, 4, 2, 3).reshape(G_B, S5_W, S5_W)
    down = (L - ks[:L])[None, :, None].astype(jnp.complex64)
    pw_down_b = jnp.exp(ldt[1][:, None, :] * down)
    g_f = jnp.exp(ldt[0][:, None, :] * (down - 1.0))[..., None] * b_bar[0][:, None]
    g_b = pw[1][:, :L, :, None] * b_bar[1][:, None]

    def g_cols(x):
        return x.transpose(0, 1, 3, 2).reshape(G_B, S5_W, P_B)

    gs = jnp.concatenate([g_cols(jnp.real(g_f)), g_cols(jnp.real(g_b)),
                          g_cols(jnp.imag(g_f)), g_cols(jnp.imag(g_b))], axis=-1)
    e_f = cc[0][:, None] * pw[0][:, 1:, None, :]
    e_b = cc[1][:, None] * pw_down_b[:, :, None, :]

    def e_rows(x):
        return x.transpose(0, 3, 1, 2).reshape(G_B, P_B, S5_W)

    e = jnp.concatenate([e_rows(jnp.real(e_f)), e_rows(jnp.real(e_b)),
                         -e_rows(jnp.imag(e_f)), -e_rows(jnp.imag(e_b))], axis=1)
    a_l = pw[:, :, L]
    al = jnp.concatenate([jnp.real(a_l[0]), jnp.real(a_l[1]), jnp.imag(a_l[0]), jnp.imag(a_l[1])], axis=-1)
    no = G_B // S5_OCT
    ow = S5_OCT * S5_W
    rows_sgi = lambda x: x.reshape(no, S5_OCT, L, S5_GROUP, S5_W).transpose(0, 2, 1, 3, 4).reshape(no, ow, S5_W)
    src = jnp.stack([rows_sgi(m), rows_sgi(gs), e.reshape(no, ow, S5_W)]).astype(BF16)
    r = jnp.arange(ow, dtype=I32)
    c = jnp.arange(S5_W, dtype=I32)
    grp_sgi = (r // S5_GROUP) % S5_OCT
    grp_gl = r // S5_W
    src_sgi = (r // (S5_OCT * S5_GROUP)) * S5_GROUP + r % S5_GROUP
    src_gl = r % S5_W
    spread = jnp.stack([src_sgi, src_gl, src_sgi])[:, None, :] == c[None, :, None]
    row_grp = jnp.stack([grp_sgi, grp_sgi, grp_gl])[:, :, None]
    col_grp = jnp.stack([grp_sgi, grp_gl, grp_sgi])[:, None, :]
    tr = 512
    out = pl.pallas_call(
        _s5_expand_body,
        grid=(3, no, ow // tr),
        in_specs=[pl.BlockSpec((None, None, tr, S5_W), lambda k, q, i: (k, q, i, 0)),
                  pl.BlockSpec((None, S5_W, ow), lambda k, q, i: (k, 0, 0)),
                  pl.BlockSpec((None, tr, 1), lambda k, q, i: (k, i, 0)),
                  pl.BlockSpec((None, 1, ow), lambda k, q, i: (k, 0, 0))],
        out_specs=pl.BlockSpec((None, None, tr, ow), lambda k, q, i: (k, q, i, 0)),
        out_shape=jax.ShapeDtypeStruct((3, no, ow, ow), BF16),
        compiler_params=_cparams(("arbitrary", "arbitrary", "arbitrary")),
        name="s5_expand",
    )(src, spread.astype(BF16), row_grp, col_grp)
    return out, al.reshape(1, G_B * S5_W)


def _s5_expand_body(src_ref, spread_ref, rg_ref, cg_ref, o_ref):
    wide = jnp.dot(src_ref[...], spread_ref[...], preferred_element_type=F32)
    o_ref[...] = jnp.where(rg_ref[...] == cg_ref[...], wide, 0.0).astype(BF16)


def _s5_fill_lhs(u_ref, lhs_s):
    tc, _, nb, lanes = u_ref.shape
    for s in range(S5_L):
        lhs_s[:, s * lanes:(s + 1) * lanes] = u_ref[:, s].reshape(tc * nb, lanes).astype(BF16)


def _s5_state_body(u_ref, g_ref, f_ref, lhs_s):
    _s5_fill_lhs(u_ref, lhs_s)
    f_ref[...] = jnp.dot(lhs_s[...], g_ref[...], preferred_element_type=F32).reshape(f_ref.shape)


def _s5_scan_body(f_ref, a_ref, s0_ref, sin_ref, sfin_ref, *, n_c):
    nb, width = s0_ref.shape
    hw = 2 * P_B
    n_g = width // S5_W
    a = a_ref[...]
    a_re = [a[:, k * S5_W:k * S5_W + hw] for k in range(n_g)]
    a_im = [a[:, k * S5_W + hw:(k + 1) * S5_W] for k in range(n_g)]
    is_fwd = (lax.broadcasted_iota(I32, (nb, width), 1) % hw) < P_B

    def split(x):
        return tuple(x[:, k * hw:(k + 1) * hw] for k in range(2 * n_g))

    def merge(parts):
        return jnp.concatenate(parts, axis=-1)

    def advance(c, parts):
        f = split(f_ref[c])
        out = []
        for k in range(n_g):
            s_re, s_im = parts[2 * k], parts[2 * k + 1]
            out.append(a_re[k] * s_re - a_im[k] * s_im + f[2 * k])
            out.append(a_re[k] * s_im + a_im[k] * s_re + f[2 * k + 1])
        return tuple(out)

    def fwd(c, parts):
        sin_ref[c] = merge(parts)
        return advance(c, parts)

    init = split(s0_ref[...])
    fin_f = lax.fori_loop(0, n_c, fwd, init)

    def bwd(k, parts):
        c = n_c - 1 - k
        sin_ref[c] = jnp.where(is_fwd, sin_ref[c], merge(parts))
        return advance(c, parts)

    fin_b = lax.fori_loop(0, n_c, bwd, init)
    sfin_ref[...] = jnp.where(is_fwd, merge(fin_f), merge(fin_b))


def _s5_out_body(u_ref, sin_ref, m_ref, e_ref, y_ref, lhs_s):
    _s5_fill_lhs(u_ref, lhs_s)
    tc, _, nb, lanes = u_ref.shape
    sin = sin_ref[...].reshape(tc * nb, sin_ref.shape[-1]).astype(BF16)
    y = (jnp.dot(lhs_s[...], m_ref[...], preferred_element_type=F32)
         + jnp.dot(sin, e_ref[...], preferred_element_type=F32))
    for s in range(S5_L):
        y_ref[:, s] = y[:, s * lanes:(s + 1) * lanes].reshape(tc, nb, lanes)


def _s5(ub, mats, s0):
    mge, al = mats
    T, B, _ = ub.shape
    n_c = T // S5_L
    no = G_B // S5_OCT
    lanes = S5_OCT * S5_GROUP
    ow = S5_OCT * S5_W
    tc = min(n_c, TOKEN_TILE // B)
    u4 = ub.reshape(n_c, S5_L, B, D_B)
    u_spec = pl.BlockSpec((tc, S5_L, B, lanes), lambda q, i: (i, 0, 0, q))
    w_spec = lambda kind: pl.BlockSpec((None, None, ow, ow), lambda q, i: (kind, q, 0, 0))
    st_spec = pl.BlockSpec((tc, B, ow), lambda q, i: (i, 0, q))
    f_loc = pl.pallas_call(
        _s5_state_body,
        grid=(no, n_c // tc),
        in_specs=[u_spec, w_spec(1)],
        out_specs=st_spec,
        out_shape=jax.ShapeDtypeStruct((n_c, B, G_B * S5_W), F32),
        scratch_shapes=[pltpu.VMEM((tc * B, S5_L * lanes), BF16)],
        compiler_params=_cparams(("arbitrary", "arbitrary")),
        name="s5_state",
    )(u4, mge)
    sw = 4 * S5_W
    sin, sfin = pl.pallas_call(
        functools.partial(_s5_scan_body, n_c=n_c),
        grid=(G_B * S5_W // sw,),
        in_specs=[pl.BlockSpec((n_c, B, sw), lambda g: (0, 0, g)),
                  pl.BlockSpec((1, sw), lambda g: (0, g)),
                  pl.BlockSpec((B, sw), lambda g: (0, g))],
        out_specs=[pl.BlockSpec((n_c, B, sw), lambda g: (0, 0, g)),
                   pl.BlockSpec((B, sw), lambda g: (0, g))],
        out_shape=[jax.ShapeDtypeStruct((n_c, B, G_B * S5_W), F32),
                   jax.ShapeDtypeStruct((B, G_B * S5_W), F32)],
        compiler_params=_cparams(("arbitrary",)),
        name="s5_scan",
    )(f_loc, al, s0)
    y = pl.pallas_call(
        _s5_out_body,
        grid=(no, n_c // tc),
        in_specs=[u_spec, st_spec, w_spec(0), w_spec(2)],
        out_specs=u_spec,
        out_shape=jax.ShapeDtypeStruct((n_c, S5_L, B, D_B), F32),
        scratch_shapes=[pltpu.VMEM((tc * B, S5_L * lanes), BF16)],
        compiler_params=_cparams(("arbitrary", "arbitrary")),
        name="s5_out",
    )(u4, sin, mge, mge)
    return y.reshape(T, B, D_B), sfin


def _s5_state_to_lanes(s_re, s_im):
    parts = [s_re[:, 0], s_re[:, 1], s_im[:, 0], s_im[:, 1]]
    return jnp.concatenate(parts, axis=-1).reshape(s_re.shape[0], G_B * S5_W)


def _s5_lanes_to_state(s):
    s = s.reshape(s.shape[0], G_B, 4, P_B)
    return jnp.stack([s[:, :, 0], s[:, :, 1]], axis=1), jnp.stack([s[:, :, 2], s[:, :, 3]], axis=1)


def _outproj0_body(x_ref, h_ref, ga_ref, y_ref, ub_ref, g1_ref, d_ref, gw_ref, gb_ref, wo_ref,
                   lg_ref, lb_ref, o_ref):
    nb, tq, d = x_ref.shape
    rows = tq * nb

    def flat(v):
        return v.reshape(rows, v.shape[-1])

    ya = flat(h_ref[0] + h_ref[1]) * jax.nn.gelu(flat(ga_ref[...]))
    yb = flat(y_ref[...]) + d_ref[...] * flat(ub_ref[...])
    g = jax.nn.gelu(yb)
    gate = jax.nn.sigmoid(jnp.dot(g.astype(BF16), gw_ref[...], preferred_element_type=F32) + gb_ref[...])
    cat = jnp.concatenate([ya, g * gate], axis=1).astype(BF16)
    cat = jnp.dot(_row_permutation(nb, tq), cat, preferred_element_type=F32).astype(BF16)
    out = jnp.dot(cat, wo_ref[...], preferred_element_type=F32).reshape(nb, tq, d)
    v = ALPHA_DN * x_ref[...] + g1_ref[...] * out
    o_ref[...] = _layer_norm(v, lg_ref[...], lb_ref[...])


def _outproj0(x, h, ga, y5, ub, pmods, s5_d, glu_w, glu_b, w_out0, ln_g, ln_b):
    B, T, D = x.shape
    tq = TOKEN_TILE // B
    tmaj = pl.BlockSpec((tq, B, D_A), lambda i: (i, 0, 0))
    vec = lambda w: pl.BlockSpec((1, w), lambda i: (0, 0))
    return pl.pallas_call(
        _outproj0_body,
        grid=(T // tq,),
        in_specs=[pl.BlockSpec((B, tq, D), lambda i: (0, i, 0)),
                  pl.BlockSpec((2, tq, B, D_A), lambda i: (0, i, 0, 0)),
                  tmaj, tmaj, tmaj,
                  _mod_spec_all(0, 2, B),
                  vec(D_B),
                  pl.BlockSpec((D_B, D_B), lambda i: (0, 0)),
                  vec(D_B),
                  pl.BlockSpec((D, D), lambda i: (0, 0)),
                  vec(D), vec(D)],
        out_specs=pl.BlockSpec((B, tq, D), lambda i: (0, i, 0)),
        out_shape=jax.ShapeDtypeStruct((B, T, D), F32),
        compiler_params=_cparams(("arbitrary",)),
        name="outproj0",
    )(x, h, ga, y5, ub, pmods, s5_d.reshape(1, D_B), glu_w.astype(BF16), glu_b.reshape(1, D_B),
      w_out0.astype(BF16), ln_g.reshape(1, D), ln_b.reshape(1, D))


def _router_body(x_ref, sc_ref, sh_ref, wr_ref, br_ref, u_ref, ldest_ref, gate_ref, cnt_ref):
    u = x_ref[...] * (1.0 + sc_ref[...]) + sh_ref[...]
    u_ref[...] = u.astype(BF16)
    tm = u.shape[0]
    logits = lax.dot_general(wr_ref[...], u, (((1,), (1,)), ((), ())), precision=HI,
                             preferred_element_type=F32) + br_ref[...]
    e_iota = lax.broadcasted_iota(I32, logits.shape, 0)
    work = logits
    vals, hots = [], []
    for _ in range(TOP_K):
        m = jnp.max(work, axis=0, keepdims=True)
        idx = jnp.min(jnp.where(work == m, e_iota, N_EXP), axis=0, keepdims=True)
        hot = e_iota == idx
        vals.append(m)
        hots.append(hot)
        work = jnp.where(hot, -jnp.inf, work)
    ex = [jnp.exp(v - vals[0]) for v in vals]
    den = ex[0] + ex[1] + ex[2] + ex[3]
    gate_ref[...] = jnp.concatenate([e / den for e in ex], axis=0)
    hot_sum = jnp.zeros(logits.shape, F32)
    for hot in hots:
        hot_sum = hot_sum + hot.astype(F32)
    hot_b = hot_sum.astype(BF16)
    before = lax.broadcasted_iota(I32, (tm, tm), 0) < lax.broadcasted_iota(I32, (tm, tm), 1)
    excl = jnp.dot(hot_b, jnp.where(before, 1.0, 0.0).astype(BF16), preferred_element_type=F32)
    cnt_row = lax.dot_general(jnp.ones((8, tm), BF16), hot_b, (((1,), (1,)), ((), ())),
                              preferred_element_type=F32)[0:1]
    cnt_ref[...] = cnt_row.astype(I32)
    run_len = jnp.ceil(cnt_row * (1.0 / RUN_ALIGN)) * RUN_ALIGN
    lower = lax.broadcasted_iota(I32, (N_EXP, N_EXP), 1) < lax.broadcasted_iota(I32, (N_EXP, N_EXP), 0)
    run_off = jnp.sum(jnp.where(lower, run_len, 0.0), axis=1, keepdims=True)
    base = excl + run_off
    rows = [jnp.sum(jnp.where(hot, base, 0.0), axis=0, keepdims=True) for hot in hots]
    ldest_ref[...] = jnp.concatenate(rows, axis=0).astype(I32)


def _router(x2, pmods, layer, rows_per_mod, tm, w_router, b_router):
    N, D = x2.shape
    n_t = N // tm
    mod = lambda which: pl.BlockSpec((None, None, None, 1, D),
                                     lambda i: (layer, which, (i * tm) // rows_per_mod, 0, 0))
    lane_spec = pl.BlockSpec((TOP_K, tm), lambda i: (0, i))
    return pl.pallas_call(
        _router_body,
        grid=(n_t,),
        in_specs=[pl.BlockSpec((tm, D), lambda i: (i, 0)),
                  mod(4), mod(3),
                  pl.BlockSpec((N_EXP, D), lambda i: (0, 0)),
                  pl.BlockSpec((N_EXP, 1), lambda i: (0, 0))],
        out_specs=[pl.BlockSpec((tm, D), lambda i: (i, 0)),
                   lane_spec, lane_spec,
                   pl.BlockSpec((None, 1, N_EXP), lambda i: (i, 0, 0))],
        out_shape=[jax.ShapeDtypeStruct((N, D), BF16),
                   jax.ShapeDtypeStruct((TOP_K, N), I32),
                   jax.ShapeDtypeStruct((TOP_K, N), F32),
                   jax.ShapeDtypeStruct((n_t, 1, N_EXP), I32)],
        compiler_params=_cparams(("arbitrary",)),
        name="router",
    )(x2, pmods, pmods, w_router.T, b_router.reshape(N_EXP, 1))


def _expert_body(blk_e_ref, n_used_ref, xs_ref, wgu_ref, bgu_ref, wdn_ref, bdn_ref, o_ref, wgu_s, wdn_s):
    i = pl.program_id(0)
    prev = blk_e_ref[jnp.maximum(i - 1, 0)]
    changed = jnp.logical_or(i == 0, blk_e_ref[i] != prev)

    @pl.when(changed)
    def _():
        wgu_s[...] = wgu_ref[...].astype(BF16)
        wdn_s[...] = wdn_ref[...].astype(BF16)

    @pl.when(i < n_used_ref[0])
    def _():
        h = jnp.dot(xs_ref[...], wgu_s[...], preferred_element_type=F32) + bgu_ref[...]
        gt = jnp.minimum(h[:, :D_FF], SWIGLU_LIMIT)
        up = jnp.clip(h[:, D_FF:], -SWIGLU_LIMIT, SWIGLU_LIMIT)
        act = (up + 1.0) * gt * jax.nn.sigmoid(SWIGLU_ALPHA * gt)
        y = jnp.dot(act.astype(BF16), wdn_s[...], preferred_element_type=F32) + bdn_ref[...]
        o_ref[...] = y.astype(o_ref.dtype)

    @pl.when(i >= n_used_ref[0])
    def _():
        o_ref[...] = jnp.zeros_like(o_ref)


def _experts(xs, blk_e, n_used, layer, w_gu, b_gu, w_dn, b_dn):
    slots, D = xs.shape
    te = EXPERT_TILE
    n_blk = slots // te
    grid_spec = pltpu.PrefetchScalarGridSpec(
        num_scalar_prefetch=2,
        grid=(n_blk,),
        in_specs=[pl.BlockSpec((te, D), lambda i, be, nu: (i, 0)),
                  pl.BlockSpec((None, None, D, 2 * D_FF), lambda i, be, nu: (layer, be[i], 0, 0)),
                  pl.BlockSpec((None, None, 1, 2 * D_FF), lambda i, be, nu: (layer, be[i], 0, 0)),
                  pl.BlockSpec((None, None, D_FF, D), lambda i, be, nu: (layer, be[i], 0, 0)),
                  pl.BlockSpec((None, None, 1, D), lambda i, be, nu: (layer, be[i], 0, 0))],
        out_specs=pl.BlockSpec((te, D), lambda i, be, nu: (i, 0)),
        scratch_shapes=[pltpu.VMEM((D, 2 * D_FF), BF16), pltpu.VMEM((D_FF, D), BF16)],
    )
    return pl.pallas_call(
        _expert_body,
        grid_spec=grid_spec,
        out_shape=jax.ShapeDtypeStruct((slots, D), BF16),
        compiler_params=_cparams(("arbitrary",)),
        name="experts",
    )(blk_e, n_used, xs, w_gu, b_gu.reshape(DEPTH, N_EXP, 1, 2 * D_FF), w_dn, b_dn.reshape(DEPTH, N_EXP, 1, D))


def _chunk_copy(vmem_ref, hbm_ref, sem, k, dst_chunk, to_hbm):
    local = vmem_ref.at[pl.ds(pl.multiple_of(k * RUN_ALIGN, RUN_ALIGN), RUN_ALIGN)]
    remote = hbm_ref.at[pl.ds(pl.multiple_of(dst_chunk * RUN_ALIGN, RUN_ALIGN), RUN_ALIGN)]
    return pltpu.make_async_copy(local, remote, sem) if to_hbm else pltpu.make_async_copy(remote, local, sem)


def _dispatch_body(dst_ref, u_ref, ld_ref, xs_in_ref, xs_ref, buf_s, sem, *, n_chunks):
    del xs_in_ref
    i = pl.program_id(0)
    n_t = pl.num_programs(0)
    slot = i % 2
    tm = u_ref.shape[0]
    lb = n_chunks * RUN_ALIGN
    rb = 256
    ld = ld_ref[...]
    u = u_ref[...]
    for j in range(lb // rb):
        r = lax.broadcasted_iota(I32, (rb, tm), 0) + j * rb
        p = jnp.where(r == ld[0:1], 1.0, jnp.where(r == ld[1:2], 1.0, jnp.where(r == ld[2:3], 1.0,
                                                                                    jnp.where(r == ld[3:4], 1.0, 0.0))))
        buf_s[slot, j * rb:(j + 1) * rb, :] = jnp.dot(p.astype(BF16), u, preferred_element_type=F32).astype(BF16)

    def start(k, _):
        _chunk_copy(buf_s.at[slot], xs_ref, sem.at[slot], k, dst_ref[i, k], True).start()
        return 0

    lax.fori_loop(0, n_chunks, start, 0, unroll=8)

    def drain(s):
        pltpu.make_async_copy(buf_s.at[s], xs_ref.at[pl.ds(0, lb)], sem.at[s]).wait()

    @pl.when(i > 0)
    def _():
        drain(1 - slot)

    @pl.when(i == n_t - 1)
    def _():
        drain(slot)


def _dispatch(u2, ldest, dst, xs, tm):
    N, D = u2.shape
    n_t, n_chunks = dst.shape
    lb = n_chunks * RUN_ALIGN
    grid_spec = pltpu.PrefetchScalarGridSpec(
        num_scalar_prefetch=1,
        grid=(n_t,),
        in_specs=[pl.BlockSpec((tm, D), lambda i, d: (i, 0)),
                  pl.BlockSpec((TOP_K, tm), lambda i, d: (0, i)),
                  pl.BlockSpec(memory_space=pl.ANY)],
        out_specs=pl.BlockSpec(memory_space=pl.ANY),
        scratch_shapes=[pltpu.VMEM((2, lb, D), BF16), pltpu.SemaphoreType.DMA((2,))],
    )
    return pl.pallas_call(
        functools.partial(_dispatch_body, n_chunks=n_chunks),
        grid_spec=grid_spec,
        out_shape=jax.ShapeDtypeStruct(xs.shape, xs.dtype),
        input_output_aliases={3: 0},
        compiler_params=_cparams(("arbitrary",)),
        name="moe_dispatch",
    )(dst, u2, ldest, xs)


def _combine_body(dst_ref, x_ref, ld_ref, gate_ref, g2_ref, lg_ref, lb_ref, ys_ref, o_ref, buf_s, sem, *, n_chunks):
    i = pl.program_id(0)
    n_t = pl.num_programs(0)
    slot = i % 2
    tm = x_ref.shape[0]
    lb = n_chunks * RUN_ALIGN

    def fetch(step, s):
        def body(k, _):
            _chunk_copy(buf_s.at[s], ys_ref, sem.at[s], k, dst_ref[step, k], False).start()
            return 0
        lax.fori_loop(0, n_chunks, body, 0, unroll=8)

    @pl.when(i == 0)
    def _():
        fetch(0, 0)

    @pl.when(i + 1 < n_t)
    def _():
        fetch(i + 1, 1 - slot)

    pltpu.make_async_copy(ys_ref.at[pl.ds(0, lb)], buf_s.at[slot], sem.at[slot]).wait()
    ld = ld_ref[...]
    gates = gate_ref[...]
    cb = 512
    y = jnp.zeros((tm, x_ref.shape[1]), F32)
    for j in range(lb // cb):
        c = lax.broadcasted_iota(I32, (tm, cb), 1) + j * cb
        w = jnp.zeros((tm, cb), F32)
        for k in range(TOP_K):
            w = jnp.where(c == ld[:, k:k + 1], gates[:, k:k + 1], w)
        y = y + jnp.dot(w.astype(BF16), buf_s[slot, j * cb:(j + 1) * cb, :], preferred_element_type=F32)
    v = ALPHA_DN * x_ref[...] + g2_ref[...] * y
    o_ref[...] = _layer_norm(v, lg_ref[...], lb_ref[...])


def _combine(x2, ldest_col, gates_col, dst, ys, pmods, layer, rows_per_mod, tm, ln_g, ln_b):
    N, D = x2.shape
    n_t, n_chunks = dst.shape
    lb = n_chunks * RUN_ALIGN
    vec = pl.BlockSpec((1, D), lambda i, d: (0, 0))
    grid_spec = pltpu.PrefetchScalarGridSpec(
        num_scalar_prefetch=1,
        grid=(n_t,),
        in_specs=[pl.BlockSpec((tm, D), lambda i, d: (i, 0)),
                  pl.BlockSpec((tm, TOP_K), lambda i, d: (i, 0)),
                  pl.BlockSpec((tm, TOP_K), lambda i, d: (i, 0)),
                  pl.BlockSpec((None, None, None, 1, D), lambda i, d: (layer, 5, (i * tm) // rows_per_mod, 0, 0)),
                  vec, vec,
                  pl.BlockSpec(memory_space=pl.ANY)],
        out_specs=pl.BlockSpec((tm, D), lambda i, d: (i, 0)),
        scratch_shapes=[pltpu.VMEM((2, lb, D), BF16), pltpu.SemaphoreType.DMA((2,))],
    )
    return pl.pallas_call(
        functools.partial(_combine_body, n_chunks=n_chunks),
        grid_spec=grid_spec,
        out_shape=jax.ShapeDtypeStruct((N, D), F32),
        compiler_params=_cparams(("arbitrary",)),
        name="moe_combine",
    )(dst, x2, ldest_col, gates_col, pmods, ln_g.reshape(1, D), ln_b.reshape(1, D), ys)


def _moe_plan(cnts, n_chunks, te, spare_chunk):
    a = RUN_ALIGN
    cnt = jnp.concatenate(cnts, axis=0)
    pc = (cnt + a - 1) // a * a
    seg = (jnp.sum(pc, axis=0) + te - 1) // te * te
    pad_end = jnp.cumsum(seg)
    run_start = (pad_end - seg)[None, :] + jnp.cumsum(pc, axis=0) - pc
    lo = jnp.cumsum(pc, axis=1) - pc
    tables, t0 = [], 0
    for c, nc in zip(cnts, n_chunks):
        sl = slice(t0, t0 + c.shape[0])
        t0 += c.shape[0]
        pos = jnp.arange(nc, dtype=I32) * a
        owner = jnp.sum(((lo[sl] + pc[sl])[:, None, :] <= pos[None, :, None]).astype(I32), axis=-1)
        mine = owner[..., None] == jnp.arange(N_EXP, dtype=I32)
        base = jnp.sum(jnp.where(mine, (run_start[sl] - lo[sl])[:, None, :], 0), axis=-1)
        valid = pos[None, :] < jnp.sum(pc[sl], axis=1, keepdims=True)
        parity = (jnp.arange(c.shape[0], dtype=I32) % 2)[:, None]
        spare = spare_chunk + parity * nc + jnp.arange(nc, dtype=I32)[None, :]
        tables.append(jnp.where(valid, (base + pos[None, :]) // a, spare).astype(I32))
    return tables, pad_end


def _moe(passes, layer, P):
    te = EXPERT_TILE
    routed = []
    for x, pmods, shared_mod in passes:
        B, T, D = x.shape
        N = B * T
        tm = min(MOE_TILE, N if shared_mod else T)
        x2 = x.reshape(N, D)
        u2, ldest, gates, cnt = _router(x2, pmods, layer, T, tm, P['w_router'][layer], P['b_router'][layer])
        routed.append((x2, pmods, T, tm, u2, ldest, gates, cnt[:, 0, :], x.shape))
    n_chunks = [r[3] * TOP_K // RUN_ALIGN + N_EXP for r in routed]
    n_tiles = [r[7].shape[0] for r in routed]
    n_assign = sum(r[0].shape[0] for r in routed) * TOP_K
    seg_rows = -(-(n_assign + sum(n_tiles) * N_EXP * RUN_ALIGN + N_EXP * te) // te) * te
    slots = seg_rows + -(-(2 * max(n_chunks) * RUN_ALIGN) // te) * te
    dsts, pad_end = _moe_plan([r[7] for r in routed], n_chunks, te, seg_rows // RUN_ALIGN)
    n_blk = slots // te
    blk_pos = jnp.arange(n_blk, dtype=I32) * te
    blk_e = jnp.minimum(jnp.sum((pad_end[None, :] <= blk_pos[:, None]).astype(I32), axis=1), N_EXP - 1)
    n_used = (pad_end[-1] // te).astype(I32).reshape(1)
    xs = jnp.zeros((slots, D_MODEL), BF16)
    for r, dst in zip(routed, dsts):
        xs = _dispatch(r[4], r[5], dst, xs, r[3])
    ys = _experts(xs, blk_e, n_used, layer, P['w_gu'], P['b_gu'], P['w_down'], P['b_down'])
    outs = []
    for r, dst in zip(routed, dsts):
        x2, pmods, T, tm, _, ldest, gates, _, shape = r
        y = _combine(x2, ldest.T, gates.T, dst, ys, pmods, layer, T, tm, P['ln_g'][layer, 1], P['ln_b'][layer, 1])
        outs.append(y.reshape(shape))
    return outs


def _split3(x):
    hi = x.astype(BF16)
    r1 = x - hi.astype(F32)
    mid = r1.astype(BF16)
    lo = (r1 - mid.astype(F32)).astype(BF16)
    return jnp.concatenate([hi, mid, lo], axis=1)


def _inproj1_body(x_ref, sc_ref, sh_ref, w_ref, wg2_ref, wgh_ref, bg_ref,
                  qk_ref, v_ref, o_ref, gcol_ref, grow_ref):
    u = x_ref[...] * (1.0 + sc_ref[...]) + sh_ref[...]
    ub = u.astype(BF16)
    z = jnp.dot(ub, w_ref[...], preferred_element_type=F32)
    qk_ref[...] = z[:, :2 * D_MODEL]
    v_ref[...] = z[:, 2 * D_MODEL:3 * D_MODEL].astype(BF16)
    o_ref[...] = z[:, 3 * D_MODEL:]
    tm = u.shape[0]
    nh = 4 * H_C
    u_lo = (u - ub.astype(F32)).astype(BF16)
    g2 = jnp.dot(ub, wg2_ref[...], preferred_element_type=F32)
    gc = g2[:, :nh] + g2[:, nh:] + jnp.dot(u_lo, wgh_ref[...], preferred_element_type=F32) + bg_ref[...]
    r_i = lax.broadcasted_iota(I32, (tm, tm), 0)
    c_i = lax.broadcasted_iota(I32, (tm, tm), 1)
    same = (r_i // CHUNK) == (c_i // CHUNK)
    tri_f = jnp.where(jnp.logical_and(same, c_i <= r_i), 1.0, 0.0).astype(BF16)
    tri_b = jnp.where(jnp.logical_and(same, c_i >= r_i), 1.0, 0.0).astype(BF16)
    lf3 = _split3(jax.nn.log_sigmoid(gc))

    def sum3(p, axis):
        if axis == 1:
            return p[:, :nh] + p[:, nh:2 * nh] + p[:, 2 * nh:]
        return p[:nh] + p[nh:2 * nh] + p[2 * nh:]

    col_i = lax.broadcasted_iota(I32, (tm, nh), 1)
    cum = jnp.where(col_i < 2 * H_C,
                    sum3(jnp.dot(tri_f, lf3, preferred_element_type=F32), 1),
                    sum3(jnp.dot(tri_b, lf3, preferred_element_type=F32), 1))
    gcol = jnp.where((col_i // H_C) % 2 == 1, cum, gc)
    gcol_ref[...] = gcol
    eye = jnp.where(r_i == c_i, 1.0, 0.0).astype(BF16)
    gt3 = lax.dot_general(_split3(gcol), eye, (((0,), (0,)), ((), ())), preferred_element_type=F32)
    grow_ref[...] = sum3(gt3, 0)


def _inproj1(x, mods, w_in1, w_gate1, b_gate1, row_off, row_stride):
    B, T, D = x.shape
    tm = min(TOKEN_TILE, T)
    n_j = T // tm
    nh = 4 * H_C
    wg_hi = w_gate1.astype(BF16)
    wg_lo = (w_gate1 - wg_hi.astype(F32)).astype(BF16)
    tok = lambda w: pl.BlockSpec((None, tm, w), lambda b, j: (b, j, 0))
    return pl.pallas_call(
        _inproj1_body,
        grid=(B, n_j),
        in_specs=[tok(D),
                  _mod_spec(1, 1, row_off, row_stride),
                  _mod_spec(1, 0, row_off, row_stride),
                  pl.BlockSpec((D, 4 * D), lambda b, j: (0, 0)),
                  pl.BlockSpec((D, 2 * nh), lambda b, j: (0, 0)),
                  pl.BlockSpec((D, nh), lambda b, j: (0, 0)),
                  pl.BlockSpec((1, nh), lambda b, j: (0, 0))],
        out_specs=[tok(2 * D), tok(D), tok(D), tok(nh),
                   pl.BlockSpec((nh, tm), lambda b, j: (0, b * n_j + j))],
        out_shape=[jax.ShapeDtypeStruct((B, T, 2 * D), F32),
                   jax.ShapeDtypeStruct((B, T, D), BF16),
                   jax.ShapeDtypeStruct((B, T, D), F32),
                   jax.ShapeDtypeStruct((B, T, nh), F32),
                   jax.ShapeDtypeStruct((nh, B * T), F32)],
        compiler_params=_cparams(("arbitrary", "arbitrary")),
        name="inproj1",
    )(x, mods, mods, w_in1.astype(BF16), jnp.concatenate([wg_hi, wg_lo], axis=1), wg_hi, b_gate1.reshape(1, nh))


def _mlstm_body(q_ref, k_ref, v_ref, gcol_ref, grow_ref, cw_ref, cb_ref, c0_ref, n0_ref, m0_ref,
                h_ref, cfin_ref, nfin_ref, mfin_ref, xpad_s, q_s, k_s, *, T):
    n_c = T // CHUNK
    L = CHUNK

    rb = min(T, 512)
    pad = 8

    def conv_silu_into(src_ref, w, bias, dst_s, scale):
        xpad_s[0:pad] = jnp.zeros((pad, DH_C), F32)
        xpad_s[T + pad:T + 2 * pad] = jnp.zeros((pad, DH_C), F32)
        xpad_s[pad:T + pad] = src_ref[...]

        def blk(r, _):
            off = pl.multiple_of(r * rb, pad)
            win = xpad_s[pl.ds(off, rb + 2 * pad), :]
            acc = (pltpu.roll(win, 2, 0) * w[0:1] + pltpu.roll(win, 1, 0) * w[1:2] + win * w[2:3]
                   + pltpu.roll(win, rb + 2 * pad - 1, 0) * w[3:4])[pad:rb + pad] + bias
            dst_s[pl.ds(off, rb), :] = (jax.nn.silu(acc) * scale).astype(BF16)
            return 0

        lax.fori_loop(0, T // rb, blk, 0)

    conv_silu_into(q_ref, cw_ref[0], cb_ref[0], q_s, 1.0)
    conv_silu_into(k_ref, cw_ref[1], cb_ref[1], k_s, DH_C ** -0.5)

    t_i = lax.broadcasted_iota(I32, (L, L), 0)
    s_i = lax.broadcasted_iota(I32, (L, L), 1)

    cfin_ref[...] = c0_ref[...]
    nfin_ref[...] = n0_ref[...]
    mfin_ref[...] = m0_ref[...]

    def chunk(d, c):
        mask = (s_i <= t_i) if d == 0 else (s_i >= t_i)
        last = L - 1 if d == 0 else 0
        off = pl.multiple_of(c * L, L)
        qc = q_s[pl.ds(off, L), :]
        kc = k_s[pl.ds(off, L), :]
        vc = v_ref[pl.ds(off, L), :]
        gcol = gcol_ref[pl.ds(off, L), :]
        grow = grow_ref[:, pl.ds(off, L)]
        li_c = gcol[:, 2 * d:2 * d + 1]
        b_c = gcol[:, 2 * d + 1:2 * d + 2]
        li_r = grow[2 * d:2 * d + 1, :]
        b_r = grow[2 * d + 1:2 * d + 2, :]
        m_prev = mfin_ref[d]
        log_d = jnp.where(mask, b_c - b_r + li_r, -jnp.inf)
        m_inter = b_c + m_prev
        m_t = jnp.maximum(m_inter, jnp.max(log_d, axis=-1, keepdims=True))
        dmat = jnp.exp(log_d - m_t)
        w_inter = jnp.exp(m_inter - m_t)
        c_old = cfin_ref[d]
        s = lax.dot_general(qc, kc, (((1,), (1,)), ((), ())), preferred_element_type=F32) * dmat
        inter = lax.dot_general(qc, c_old.astype(BF16), (((1,), (1,)), ((), ())), preferred_element_type=F32)
        num = w_inter * inter + jnp.dot(s.astype(BF16), vc, preferred_element_type=F32)
        qn = jnp.sum(qc.astype(F32) * nfin_ref[d], axis=-1, keepdims=True)
        den = w_inter * qn + jnp.sum(s, axis=-1, keepdims=True)
        h = num / jnp.maximum(jnp.abs(den), jnp.exp(-m_t))
        m_new = m_t[last:last + 1, :]
        b_last = b_c[last:last + 1, :]
        w_s = jnp.exp(b_last - b_c + li_c - m_new)
        decay = jnp.exp(b_last + m_prev - m_new)
        wv = (w_s * vc.astype(F32)).astype(BF16)
        cfin_ref[d] = decay * c_old + lax.dot_general(wv, kc, (((0,), (0,)), ((), ())),
                                                      preferred_element_type=F32)
        nfin_ref[d] = decay * nfin_ref[d] + jnp.sum(w_s * kc.astype(F32), axis=0, keepdims=True)
        mfin_ref[d] = m_new
        return off, h

    def both(ci, _):
        off_f, h_f = chunk(0, ci)
        off_b, h_b = chunk(1, n_c - 1 - ci)
        h_ref[pl.ds(off_f, L), :] = h_f
        xpad_s[pl.ds(off_b, L), :] = h_b
        return 0

    lax.fori_loop(0, n_c, both, 0)

    def add_bwd(r, _):
        off = pl.multiple_of(r * rb, pad)
        h_ref[pl.ds(off, rb), :] = h_ref[pl.ds(off, rb), :] + xpad_s[pl.ds(off, rb), :]
        return 0

    lax.fori_loop(0, T // rb, add_bwd, 0)


def _mlstm(qk, v, gcol, grow, conv_w, conv_b, c0, n0, m0):
    B, T, _ = v.shape
    DH = DH_C
    cw = conv_w.reshape(CONV_W, 2, H_C, DH).transpose(2, 1, 0, 3)
    cb = conv_b.reshape(2, H_C, 1, DH).transpose(1, 0, 2, 3)
    st = lambda *tail: pl.BlockSpec((None, 2, None) + tail, lambda b, h: (b, 0, h) + (0,) * len(tail))
    return pl.pallas_call(
        functools.partial(_mlstm_body, T=T),
        grid=(B, H_C),
        in_specs=[pl.BlockSpec((None, T, DH), lambda b, h: (b, 0, h)),
                  pl.BlockSpec((None, T, DH), lambda b, h: (b, 0, H_C + h)),
                  pl.BlockSpec((None, T, DH), lambda b, h: (b, 0, h)),
                  pl.BlockSpec((None, None, T, 4), lambda b, h: (b, h, 0, 0)),
                  pl.BlockSpec((None, None, 4, T), lambda b, h: (b, h, 0, 0)),
                  pl.BlockSpec((None, 2, CONV_W, DH), lambda b, h: (h, 0, 0, 0)),
                  pl.BlockSpec((None, 2, 1, DH), lambda b, h: (h, 0, 0, 0)),
                  st(DH, DH), st(1, DH), st(1, 1)],
        out_specs=[pl.BlockSpec((None, T, DH), lambda b, h: (b, 0, h)),
                   st(DH, DH), st(1, DH), st(1, 1)],
        out_shape=[jax.ShapeDtypeStruct((B, T, D_MODEL), F32),
                   jax.ShapeDtypeStruct((B, 2, H_C, DH, DH), F32),
                   jax.ShapeDtypeStruct((B, 2, H_C, 1, DH), F32),
                   jax.ShapeDtypeStruct((B, 2, H_C, 1, 1), F32)],
        scratch_shapes=[pltpu.VMEM((T + 16, DH), F32), pltpu.VMEM((T, DH), BF16), pltpu.VMEM((T, DH), BF16)],
        compiler_params=_cparams(("arbitrary", "arbitrary")),
        name="mlstm",
    )(qk, qk, v, gcol, grow, cw, cb, c0, n0, m0)


def _outproj1_body(x_ref, h_ref, o_ref, g1_ref, w_ref, lg_ref, lb_ref, out_ref):
    y = jax.nn.sigmoid(o_ref[...]) * h_ref[...]
    out = jnp.dot(y.astype(BF16), w_ref[...], preferred_element_type=F32)
    v = ALPHA_DN * x_ref[...] + g1_ref[...] * out
    out_ref[...] = _layer_norm(v, lg_ref[...], lb_ref[...])


def _outproj1(x, h, o, mods, w_out1, ln_g, ln_b, row_off, row_stride):
    B, T, D = x.shape
    tm = min(TOKEN_TILE, T)
    tok = pl.BlockSpec((None, tm, D), lambda b, j: (b, j, 0))
    vec = pl.BlockSpec((1, D), lambda b, j: (0, 0))
    return pl.pallas_call(
        _outproj1_body,
        grid=(B, T // tm),
        in_specs=[tok, tok, tok, _mod_spec(1, 2, row_off, row_stride),
                  pl.BlockSpec((D, D), lambda b, j: (0, 0)), vec, vec],
        out_specs=tok,
        out_shape=jax.ShapeDtypeStruct((B, T, D), F32),
        compiler_params=_cparams(("arbitrary", "arbitrary")),
        name="outproj1",
    )(x, h, o, mods, w_out1.astype(BF16), ln_g.reshape(1, D), ln_b.reshape(1, D))


def _to_col_major(x):
    B, T, C = x.shape
    rows = T // GRID_W
    return x.reshape(B, rows, GRID_W, C).transpose(0, 2, 1, 3).reshape(B, T, C)


def _to_row_major(x):
    B, T, C = x.shape
    rows = T // GRID_W
    return x.reshape(B, GRID_W, rows, C).transpose(0, 2, 1, 3).reshape(B, T, C)


def _mixer0(x, mods, st, P, s5_mats):
    h_rg, s_re, s_im = st
    xa, ga, ub = _inproj0(x, mods, P['w_in0'][0].astype(BF16))
    h, h_fin = _rglru(xa, P['conv_a_w'][0], P['conv_a_b'][0], P['rg_wa'][0], P['rg_ba'][0],
                      P['rg_wi'][0], P['rg_bi'][0], P['rg_lam'][0], h_rg[:, 0].transpose(1, 0, 2))
    y5, s_fin = _s5(ub, s5_mats, _s5_state_to_lanes(s_re[:, 0], s_im[:, 0]))
    x = _outproj0(x, h, ga, y5, ub, mods, P['s5_d'][0], P['glu_w'][0], P['glu_b'][0], P['w_out0'][0],
                  P['ln_g'][0, 0], P['ln_b'][0, 0])
    new_re, new_im = _s5_lanes_to_state(s_fin)
    return x, (h_fin.transpose(1, 0, 2)[:, None], new_re[:, None], new_im[:, None])


def _mixer1(x, mods, st, P):
    m_c, m_n, m_m = st
    B, T, D = x.shape
    qk, v, o, gcol, grow = _inproj1(x, mods, P['w_in1'][0], P['w_gate1'][0], P['b_gate1'][0], 0, 1)
    gcol = gcol.reshape(B, T, 2, 2, H_C).transpose(0, 4, 1, 2, 3).reshape(B, H_C, T, 4)
    grow = grow.reshape(2, 2, H_C, B, T).transpose(3, 2, 0, 1, 4).reshape(B, H_C, 4, T)
    h, c_fin, n_fin, m_fin = _mlstm(qk, v, gcol, grow, P['conv_c_w'][0], P['conv_c_b'][0],
                                    m_c[:, 0], m_n[:, 0][:, :, :, None, :], m_m[:, 0][:, :, :, None, None])
    x = _outproj1(x, h, o, mods, P['w_out1'][0], P['ln_g'][1, 0], P['ln_b'][1, 0], 0, 1)
    return x, (c_fin[:, None], n_fin[:, None, :, :, 0, :], m_fin[:, None, :, :, 0, 0])


def _forward(x_prompt, x_sample, c, c_ctx, states, P):
    bp = x_prompt.shape[0]
    bs = x_sample.shape[0]
    rows = 1 + bs
    rpad = -(-rows // 8) * 8
    cv = jnp.concatenate([c_ctx[None, :], c, jnp.zeros((rpad - rows, D_MODEL), F32)], axis=0)
    mods = _modulation(cv, P['w_mod'], P['b_mod'])
    s5_mats = _s5_matrices(P['s5_a_re'][0], P['s5_a_im'][0], P['s5_log_dt'][0], P['s5_b_re'][0], P['s5_b_im'][0],
                           P['s5_c_re'][0], P['s5_c_im'][0])
    zero_state = (jnp.zeros((bp, 1, 2, D_A), F32),
                  jnp.zeros((bp, 1, 2, G_B, P_B), F32),
                  jnp.zeros((bp, 1, 2, G_B, P_B), F32),
                  jnp.zeros((bp, 1, 2, H_C, DH_C, DH_C), F32),
                  jnp.zeros((bp, 1, 2, H_C, DH_C), F32),
                  jnp.zeros((bp, 1, 2, H_C), F32))
    mods_ctx = jnp.broadcast_to(mods[:, :, 0:1], mods.shape[:2] + (bp,) + mods.shape[3:])
    mods_lat = mods[:, :, 1:1 + bs]
    xc, new_even = _mixer0(x_prompt, mods_ctx, zero_state[:3], P, s5_mats)
    xl, _ = _mixer0(x_sample, mods_lat, states[:3], P, s5_mats)
    xc, xl = _moe([(xc, mods_ctx, True), (xl, mods_lat, False)], 0, P)
    xl = _to_col_major(xl)
    xc, new_odd = _mixer1(xc, mods_ctx, zero_state[3:], P)
    xl, _ = _mixer1(xl, mods_lat, states[3:], P)
    xc, xl = _moe([(xc, mods_ctx, True), (xl, mods_lat, False)], 1, P)
    return (xc, _to_row_major(xl)) + tuple(new_even) + tuple(new_odd)


def kernel(x_prompt, x_sample, c, c_ctx, state_rglru, state_s5_re, state_s5_im, state_mlstm_C, state_mlstm_n, state_mlstm_m, w_mod, b_mod, ln_g, ln_b, w_in0, conv_a_w, conv_a_b, rg_wa, rg_ba, rg_wi, rg_bi, rg_lam, s5_a_re, s5_a_im, s5_log_dt, s5_b_re, s5_b_im, s5_c_re, s5_c_im, s5_d, glu_w, glu_b, w_out0, w_in1, w_gate1, b_gate1, conv_c_w, conv_c_b, w_out1, w_router, b_router, w_gu, b_gu, w_down, b_down):
    P = dict(w_mod=w_mod, b_mod=b_mod, ln_g=ln_g, ln_b=ln_b, w_in0=w_in0, conv_a_w=conv_a_w,
             conv_a_b=conv_a_b, rg_wa=rg_wa, rg_ba=rg_ba, rg_wi=rg_wi, rg_bi=rg_bi, rg_lam=rg_lam,
             s5_a_re=s5_a_re, s5_a_im=s5_a_im, s5_log_dt=s5_log_dt, s5_b_re=s5_b_re, s5_b_im=s5_b_im,
             s5_c_re=s5_c_re, s5_c_im=s5_c_im, s5_d=s5_d, glu_w=glu_w, glu_b=glu_b, w_out0=w_out0,
             w_in1=w_in1, w_gate1=w_gate1, b_gate1=b_gate1, conv_c_w=conv_c_w, conv_c_b=conv_c_b,
             w_out1=w_out1, w_router=w_router, b_router=b_router, w_gu=w_gu, b_gu=b_gu,
             w_down=w_down, b_down=b_down)
    states = (state_rglru, state_s5_re, state_s5_im, state_mlstm_C, state_mlstm_n, state_mlstm_m)
    return _forward(x_prompt, x_sample, c, c_ctx, states, P)
```

```python
import functools
import math

import jax
import jax.numpy as jnp
from jax import lax
from jax.experimental import pallas as pl
from jax.experimental.pallas import tpu as pltpu

F32 = jnp.float32
BF16 = jnp.bfloat16
I32 = jnp.int32
HI = lax.Precision.HIGHEST

D_MODEL = 1024
DEPTH = 2
GRID_W = 64
D_A = 512
NB_A = 8
BS_A = D_A // NB_A
CONV_W = 4
LRU_C = 8.0
D_B = 512
S5_GROUP = 16
G_B = D_B // S5_GROUP
P_B = 64
H_C = 4
DH_C = D_MODEL // H_C
CHUNK = 128
N_EXP = 32
TOP_K = 4
D_FF = D_MODEL
SWIGLU_LIMIT = 7.0
SWIGLU_ALPHA = 1.702
ALPHA_DN = (2 * DEPTH) ** 0.25
LN_EPS = 1e-5

S5_L = 16
S5_W = S5_L * S5_GROUP
S5_OCT = 8
TOKEN_TILE = 256
EXPERT_TILE = 512
MOE_TILE = 512
RUN_ALIGN = 16
VMEM_LIMIT = 56 * 1024 * 1024


def _cparams(sem, vmem=VMEM_LIMIT):
    return pltpu.CompilerParams(dimension_semantics=sem, vmem_limit_bytes=vmem)


def _layer_norm(v, g, b):
    mu = jnp.mean(v, axis=-1, keepdims=True)
    c = v - mu
    var = jnp.mean(c * c, axis=-1, keepdims=True)
    return c * lax.rsqrt(var + LN_EPS) * g + b


def _mod_spec(layer, which, row_off, row_stride):
    return pl.BlockSpec((None, None, None, 1, D_MODEL),
                        lambda b, j: (layer, which, row_off + b * row_stride, 0, 0))


def _mod_spec_all(layer, which, nb):
    return pl.BlockSpec((None, None, nb, 1, D_MODEL), lambda i: (layer, which, 0, 0, 0))


def _row_permutation(n_out_major, n_out_minor):
    n = n_out_major * n_out_minor
    r_out = lax.broadcasted_iota(I32, (n, n), 0)
    r_in = lax.broadcasted_iota(I32, (n, n), 1)
    hit = jnp.logical_and(r_out // n_out_minor == r_in % n_out_major, r_out % n_out_minor == r_in // n_out_major)
    return jnp.where(hit, 1.0, 0.0).astype(BF16)


def _mod_body(c_ref, w_ref, b_ref, o_ref):
    s = jax.nn.silu(c_ref[...])
    o_ref[...] = jnp.dot(s, w_ref[...], precision=HI, preferred_element_type=F32) + b_ref[...]


def _modulation(cv, w_mod, b_mod):
    R, D = cv.shape
    L, _, N6 = w_mod.shape
    tn = N6 // 6
    out = pl.pallas_call(
        _mod_body,
        grid=(L, N6 // tn),
        in_specs=[pl.BlockSpec((R, D), lambda l, j: (0, 0)),
                  pl.BlockSpec((None, D, tn), lambda l, j: (l, 0, j)),
                  pl.BlockSpec((None, 1, tn), lambda l, j: (l, 0, j))],
        out_specs=pl.BlockSpec((None, R, tn), lambda l, j: (l, 0, j)),
        out_shape=jax.ShapeDtypeStruct((L, R, N6), F32),
        compiler_params=_cparams(("arbitrary", "arbitrary")),
        name="modulation",
    )(cv, w_mod, b_mod.reshape(L, 1, N6))
    return out.reshape(L, R, 6, D).transpose(0, 2, 1, 3)[:, :, :, None, :]


def _inproj0_body(x_ref, sc_ref, sh_ref, w_ref, xa_ref, ga_ref, ub_ref):
    nb, tq, d = x_ref.shape
    u = x_ref[...] * (1.0 + sc_ref[...]) + sh_ref[...]
    ub = u.reshape(nb * tq, d).astype(BF16)
    ut = jnp.dot(_row_permutation(tq, nb), ub, preferred_element_type=F32).astype(BF16)
    z = jnp.dot(ut, w_ref[...], preferred_element_type=F32).reshape(tq, nb, w_ref.shape[1])
    xa_ref[...] = z[:, :, :D_A]
    ga_ref[...] = z[:, :, D_A:2 * D_A]
    ub_ref[...] = z[:, :, 2 * D_A:]


def _inproj0(x, pmods, w_in0):
    B, T, D = x.shape
    tq = TOKEN_TILE // B
    nz = w_in0.shape[1]
    tmaj = pl.BlockSpec((tq, B, D_A), lambda i: (i, 0, 0))
    return pl.pallas_call(
        _inproj0_body,
        grid=(T // tq,),
        in_specs=[pl.BlockSpec((B, tq, D), lambda i: (0, i, 0)),
                  _mod_spec_all(0, 1, B), _mod_spec_all(0, 0, B),
                  pl.BlockSpec((D, nz), lambda i: (0, 0))],
        out_specs=[tmaj, tmaj, tmaj],
        out_shape=[jax.ShapeDtypeStruct((T, B, D_A), F32)] * 3,
        compiler_params=_cparams(("arbitrary",)),
        name="inproj0",
    )(x, pmods, pmods, w_in0)


def _rglru_body(cur_ref, prev_ref, next_ref, cw_ref, cb_ref, wa_ref, wi_ref, ba_ref, bi_ref, lam_ref, h0_ref,
                h_ref, hfin_ref, ext_s, a_s, b_s, carry_s, *, tt, n_t):
    d = pl.program_id(0)
    j = pl.program_id(1)
    jj = j + d * (n_t - 1 - 2 * j)
    nb = cur_ref.shape[1]
    half = D_A // 2
    ext_s[0:2] = jnp.where(jj == 0, 0.0, prev_ref[...])
    ext_s[2:tt + 2] = cur_ref[...]
    ext_s[tt + 2:tt + 3] = jnp.where(jj == n_t - 1, 0.0, next_ref[...])
    xc = ext_s[0:tt] * cw_ref[0]
    for k in range(1, CONV_W):
        xc = xc + ext_s[k:k + tt] * cw_ref[k]
    xc = xc + cb_ref[...]
    x2 = xc.reshape(tt * nb, D_A)
    xb = x2.astype(BF16)

    def gate(w_ref, bias_ref):
        lo = jnp.dot(xb[:, :half], w_ref[0], preferred_element_type=F32)
        hi = jnp.dot(xb[:, half:], w_ref[1], preferred_element_type=F32)
        return jax.nn.sigmoid(jnp.concatenate([lo, hi], axis=1) + bias_ref[...])

    r = gate(wa_ref, ba_ref)
    i = gate(wi_ref, bi_ref)
    log_a = LRU_C * r * jax.nn.log_sigmoid(lam_ref[...])
    a = jnp.exp(log_a)
    one_minus_a2 = -jnp.tanh(log_a) * (a * a + 1.0)
    a_s[...] = a.reshape(tt, nb, D_A)
    b_s[...] = (jnp.sqrt(one_minus_a2) * (i * x2)).reshape(tt, nb, D_A)

    @pl.when(j == 0)
    def _():
        carry_s[...] = h0_ref[...]

    def step(t, h):
        h = a_s[t] * h + b_s[t]
        h_ref[t] = h
        return h

    @pl.when(d == 0)
    def _():
        carry_s[...] = lax.fori_loop(0, tt, step, carry_s[...], unroll=8)

    @pl.when(d == 1)
    def _():
        carry_s[...] = lax.fori_loop(0, tt, lambda t, h: step(tt - 1 - t, h), carry_s[...], unroll=8)

    hfin_ref[...] = carry_s[...]


def _block_diag_halves(w):
    nd = w.shape[0]
    per = NB_A // 2
    w = w.reshape(nd, 2, per, BS_A, BS_A)
    eye = jnp.eye(per, dtype=w.dtype)
    out = jnp.einsum('dhncz,nm->dhncmz', w, eye)
    return out.reshape(nd, 2, per * BS_A, per * BS_A)


def _rglru(x3, conv_w, conv_b, wa, ba, wi, bi, lam, h0):
    T, B, _ = x3.shape
    tt = min(T, 1024 // B)
    n_t = T // tt

    def cur_map(d, j):
        return (j + d * (n_t - 1 - 2 * j), 0, 0)

    def prev_map(d, j):
        jj = j + d * (n_t - 1 - 2 * j)
        return (jnp.maximum(jj * (tt // 2) - 1, 0), 0, 0)

    def next_map(d, j):
        jj = j + d * (n_t - 1 - 2 * j)
        return (jnp.minimum((jj + 1) * tt, T - 1), 0, 0)

    dir_spec3 = pl.BlockSpec((None, 1, D_A), lambda d, j: (d, 0, 0))
    h, hfin = pl.pallas_call(
        functools.partial(_rglru_body, tt=tt, n_t=n_t),
        grid=(2, n_t),
        in_specs=[pl.BlockSpec((tt, B, D_A), cur_map),
                  pl.BlockSpec((2, B, D_A), prev_map),
                  pl.BlockSpec((1, B, D_A), next_map),
                  pl.BlockSpec((CONV_W, 1, D_A), lambda d, j: (0, 0, 0)),
                  pl.BlockSpec((1, 1, D_A), lambda d, j: (0, 0, 0)),
                  pl.BlockSpec((None, 2, D_A // 2, D_A // 2), lambda d, j: (d, 0, 0, 0)),
                  pl.BlockSpec((None, 2, D_A // 2, D_A // 2), lambda d, j: (d, 0, 0, 0)),
                  dir_spec3, dir_spec3, dir_spec3,
                  pl.BlockSpec((None, B, D_A), lambda d, j: (d, 0, 0))],
        out_specs=[pl.BlockSpec((None, tt, B, D_A), lambda d, j: (d,) + cur_map(d, j)),
                   pl.BlockSpec((None, B, D_A), lambda d, j: (d, 0, 0))],
        out_shape=[jax.ShapeDtypeStruct((2, T, B, D_A), F32),
                   jax.ShapeDtypeStruct((2, B, D_A), F32)],
        scratch_shapes=[pltpu.VMEM((tt + 3, B, D_A), F32),
                        pltpu.VMEM((tt, B, D_A), F32),
                        pltpu.VMEM((tt, B, D_A), F32),
                        pltpu.VMEM((B, D_A), F32)],
        compiler_params=_cparams(("arbitrary", "arbitrary")),
        name="rglru",
    )(x3, x3, x3, conv_w.reshape(CONV_W, 1, D_A), conv_b.reshape(1, 1, D_A),
      _block_diag_halves(wa).astype(BF16), _block_diag_halves(wi).astype(BF16),
      ba.reshape(2, 1, D_A), bi.reshape(2, 1, D_A), lam.reshape(2, 1, D_A), h0)
    return h, hfin


def _s5_matrices(a_re, a_im, log_dt, b_re, b_im, c_re, c_im):
    L = S5_L
    lam = lax.complex(a_re.astype(F32), a_im.astype(F32))
    dt = jnp.exp(log_dt.astype(F32))[..., None]
    ldt = lam * dt
    a_bar = jnp.exp(ldt)
    b_bar = ((a_bar - 1.0) / lam)[..., None] * lax.complex(b_re.astype(F32), b_im.astype(F32))
    cc = lax.complex(c_re.astype(F32), c_im.astype(F32))
    ks = jnp.arange(L + 1, dtype=F32)
    pw = jnp.exp(ldt[:, :, None, :] * ks[None, None, :, None].astype(jnp.complex64))
    kern = jnp.real(jnp.einsum('dgjp,dgkp,dgpi->dgkji', cc, pw[:, :, :L], b_bar, precision=HI))
    s_idx = jnp.arange(L)[:, None]
    t_idx = jnp.arange(L)[None, :]
    lag_f = jnp.clip(t_idx - s_idx, 0, L - 1)
    lag_b = jnp.clip(s_idx - t_idx, 0, L - 1)
    m_f = jnp.where((t_idx >= s_idx)[None, :, :, None, None], kern[0][:, lag_f], 0.0)
    m_b = jnp.where((s_idx >= t_idx)[None, :, :, None, None], kern[1][:, lag_b], 0.0)
    m = (m_f + m_b).transpose(0, 1, 4, 2, 3).reshape(G_B, S5_W, S5_W)
    down = (L - ks[:L])[None, :, None].astype(jnp.complex64)
    pw_down_b = jnp.exp(ldt[1][:, None, :] * down)
    g_f = jnp.exp(ldt[0][:, None, :] * (down - 1.0))[..., None] * b_bar[0][:, None]
    g_b = pw[1][:, :L, :, None] * b_bar[1][:, None]

    def g_cols(x):
        return x.transpose(0, 1, 3, 2).reshape(G_B, S5_W, P_B)

    gs = jnp.concatenate([g_cols(jnp.real(g_f)), g_cols(jnp.real(g_b)),
                          g_cols(jnp.imag(g_f)), g_cols(jnp.imag(g_b))], axis=-1)
    e_f = cc[0][:, None] * pw[0][:, 1:, None, :]
    e_b = cc[1][:, None] * pw_down_b[:, :, None, :]

    def e_rows(x):
        return x.transpose(0, 3, 1, 2).reshape(G_B, P_B, S5_W)

    e = jnp.concatenate([e_rows(jnp.real(e_f)), e_rows(jnp.real(e_b)),
                         -e_rows(jnp.imag(e_f)), -e_rows(jnp.imag(e_b))], axis=1)
    a_l = pw[:, :, L]
    al = jnp.concatenate([jnp.real(a_l[0]), jnp.real(a_l[1]), jnp.imag(a_l[0]), jnp.imag(a_l[1])], axis=-1)
    no = G_B // S5_OCT
    ow = S5_OCT * S5_W
    rows_sgi = lambda x: x.reshape(no, S5_OCT, L, S5_GROUP, S5_W).transpose(0, 2, 1, 3, 4).reshape(no, ow, S5_W)
    src = jnp.stack([rows_sgi(m), rows_sgi(gs), e.reshape(no, ow, S5_W)]).astype(BF16)
    r = jnp.arange(ow, dtype=I32)
    c = jnp.arange(S5_W, dtype=I32)
    grp_sgi = (r // S5_GROUP) % S5_OCT
    grp_gl = r // S5_W
    src_sgi = (r // (S5_OCT * S5_GROUP)) * S5_GROUP + r % S5_GROUP
    src_gl = r % S5_W
    spread = jnp.stack([src_sgi, src_gl, src_sgi])[:, None, :] == c[None, :, None]
    row_grp = jnp.stack([grp_sgi, grp_sgi, grp_gl])[:, :, None]
    col_grp = jnp.stack([grp_sgi, grp_gl, grp_sgi])[:, None, :]
    tr = 512
    out = pl.pallas_call(
        _s5_expand_body,
        grid=(3, no, ow // tr),
        in_specs=[pl.BlockSpec((None, None, tr, S5_W), lambda k, q, i: (k, q, i, 0)),
                  pl.BlockSpec((None, S5_W, ow), lambda k, q, i: (k, 0, 0)),
                  pl.BlockSpec((None, tr, 1), lambda k, q, i: (k, i, 0)),
                  pl.BlockSpec((None, 1, ow), lambda k, q, i: (k, 0, 0))],
        out_specs=pl.BlockSpec((None, None, tr, ow), lambda k, q, i: (k, q, i, 0)),
        out_shape=jax.ShapeDtypeStruct((3, no, ow, ow), BF16),
        compiler_params=_cparams(("arbitrary", "arbitrary", "arbitrary")),
        name="s5_expand",
    )(src, spread.astype(BF16), row_grp, col_grp)
    return out, al.reshape(1, G_B * S5_W)


def _s5_expand_body(src_ref, spread_ref, rg_ref, cg_ref, o_ref):
    wide = jnp.dot(src_ref[...], spread_ref[...], preferred_element_type=F32)
    o_ref[...] = jnp.where(rg_ref[...] == cg_ref[...], wide, 0.0).astype(BF16)


def _s5_fill_lhs(u_ref, lhs_s):
    tc, _, nb, lanes = u_ref.shape
    for s in range(S5_L):
        lhs_s[:, s * lanes:(s + 1) * lanes] = u_ref[:, s].reshape(tc * nb, lanes).astype(BF16)


def _s5_state_body(u_ref, g_ref, f_ref, lhs_s):
    _s5_fill_lhs(u_ref, lhs_s)
    f_ref[...] = jnp.dot(lhs_s[...], g_ref[...], preferred_element_type=F32).reshape(f_ref.shape)


def _s5_scan_body(f_ref, a_ref, s0_ref, sin_ref, sfin_ref, *, n_c):
    nb, width = s0_ref.shape
    hw = 2 * P_B
    n_g = width // S5_W
    a = a_ref[...]
    a_re = [a[:, k * S5_W:k * S5_W + hw] for k in range(n_g)]
    a_im = [a[:, k * S5_W + hw:(k + 1) * S5_W] for k in range(n_g)]
    is_fwd = (lax.broadcasted_iota(I32, (nb, width), 1) % hw) < P_B

    def split(x):
        return tuple(x[:, k * hw:(k + 1) * hw] for k in range(2 * n_g))

    def merge(parts):
        return jnp.concatenate(parts, axis=-1)

    def advance(c, parts):
        f = split(f_ref[c])
        out = []
        for k in range(n_g):
            s_re, s_im = parts[2 * k], parts[2 * k + 1]
            out.append(a_re[k] * s_re - a_im[k] * s_im + f[2 * k])
            out.append(a_re[k] * s_im + a_im[k] * s_re + f[2 * k + 1])
        return tuple(out)

    def fwd(c, parts):
        sin_ref[c] = merge(parts)
        return advance(c, parts)

    init = split(s0_ref[...])
    fin_f = lax.fori_loop(0, n_c, fwd, init)

    def bwd(k, parts):
        c = n_c - 1 - k
        sin_ref[c] = jnp.where(is_fwd, sin_ref[c], merge(parts))
        return advance(c, parts)

    fin_b = lax.fori_loop(0, n_c, bwd, init)
    sfin_ref[...] = jnp.where(is_fwd, merge(fin_f), merge(fin_b))


def _s5_out_body(u_ref, sin_ref, m_ref, e_ref, y_ref, lhs_s):
    _s5_fill_lhs(u_ref, lhs_s)
    tc, _, nb, lanes = u_ref.shape
    sin = sin_ref[...].reshape(tc * nb, sin_ref.shape[-1]).astype(BF16)
    y = (jnp.dot(lhs_s[...], m_ref[...], preferred_element_type=F32)
         + jnp.dot(sin, e_ref[...], preferred_element_type=F32))
    for s in range(S5_L):
        y_ref[:, s] = y[:, s * lanes:(s + 1) * lanes].reshape(tc, nb, lanes)


def _s5(ub, mats, s0):
    mge, al = mats
    T, B, _ = ub.shape
    n_c = T // S5_L
    no = G_B // S5_OCT
    lanes = S5_OCT * S5_GROUP
    ow = S5_OCT * S5_W
    tc = min(n_c, TOKEN_TILE // B)
    u4 = ub.reshape(n_c, S5_L, B, D_B)
    u_spec = pl.BlockSpec((tc, S5_L, B, lanes), lambda q, i: (i, 0, 0, q))
    w_spec = lambda kind: pl.BlockSpec((None, None, ow, ow), lambda q, i: (kind, q, 0, 0))
    st_spec = pl.BlockSpec((tc, B, ow), lambda q, i: (i, 0, q))
    f_loc = pl.pallas_call(
        _s5_state_body,
        grid=(no, n_c // tc),
        in_specs=[u_spec, w_spec(1)],
        out_specs=st_spec,
        out_shape=jax.ShapeDtypeStruct((n_c, B, G_B * S5_W), F32),
        scratch_shapes=[pltpu.VMEM((tc * B, S5_L * lanes), BF16)],
        compiler_params=_cparams(("arbitrary", "arbitrary")),
        name="s5_state",
    )(u4, mge)
    sw = 4 * S5_W
    sin, sfin = pl.pallas_call(
        functools.partial(_s5_scan_body, n_c=n_c),
        grid=(G_B * S5_W // sw,),
        in_specs=[pl.BlockSpec((n_c, B, sw), lambda g: (0, 0, g)),
                  pl.BlockSpec((1, sw), lambda g: (0, g)),
                  pl.BlockSpec((B, sw), lambda g: (0, g))],
        out_specs=[pl.BlockSpec((n_c, B, sw), lambda g: (0, 0, g)),
                   pl.BlockSpec((B, sw), lambda g: (0, g))],
        out_shape=[jax.ShapeDtypeStruct((n_c, B, G_B * S5_W), F32),
                   jax.ShapeDtypeStruct((B, G_B * S5_W), F32)],
        compiler_params=_cparams(("arbitrary",)),
        name="s5_scan",
    )(f_loc, al, s0)
    y = pl.pallas_call(
        _s5_out_body,
        grid=(no, n_c // tc),
        in_specs=[u_spec, st_spec, w_spec(0), w_spec(2)],
        out_specs=u_spec,
        out_shape=jax.ShapeDtypeStruct((n_c, S5_L, B, D_B), F32),
        scratch_shapes=[pltpu.VMEM((tc * B, S5_L * lanes), BF16)],
        compiler_params=_cparams(("arbitrary", "arbitrary")),
        name="s5_out",
    )(u4, sin, mge, mge)
    return y.reshape(T, B, D_B), sfin


def _s5_state_to_lanes(s_re, s_im):
    parts = [s_re[:, 0], s_re[:, 1], s_im[:, 0], s_im[:, 1]]
    return jnp.concatenate(parts, axis=-1).reshape(s_re.shape[0], G_B * S5_W)


def _s5_lanes_to_state(s):
    s = s.reshape(s.shape[0], G_B, 4, P_B)
    return jnp.stack([s[:, :, 0], s[:, :, 1]], axis=1), jnp.stack([s[:, :, 2], s[:, :, 3]], axis=1)


def _outproj0_body(x_ref, h_ref, ga_ref, y_ref, ub_ref, g1_ref, d_ref, gw_ref, gb_ref, wo_ref,
                   lg_ref, lb_ref, o_ref):
    nb, tq, d = x_ref.shape
    rows = tq * nb

    def flat(v):
        return v.reshape(rows, v.shape[-1])

    ya = flat(h_ref[0] + h_ref[1]) * jax.nn.gelu(flat(ga_ref[...]))
    yb = flat(y_ref[...]) + d_ref[...] * flat(ub_ref[...])
    g = jax.nn.gelu(yb)
    gate = jax.nn.sigmoid(jnp.dot(g.astype(BF16), gw_ref[...], preferred_element_type=F32) + gb_ref[...])
    cat = jnp.concatenate([ya, g * gate], axis=1).astype(BF16)
    cat = jnp.dot(_row_permutation(nb, tq), cat, preferred_element_type=F32).astype(BF16)
    out = jnp.dot(cat, wo_ref[...], preferred_element_type=F32).reshape(nb, tq, d)
    v = ALPHA_DN * x_ref[...] + g1_ref[...] * out
    o_ref[...] = _layer_norm(v, lg_ref[...], lb_ref[...])


def _outproj0(x, h, ga, y5, ub, pmods, s5_d, glu_w, glu_b, w_out0, ln_g, ln_b):
    B, T, D = x.shape
    tq = TOKEN_TILE // B
    tmaj = pl.BlockSpec((tq, B, D_A), lambda i: (i, 0, 0))
    vec = lambda w: pl.BlockSpec((1, w), lambda i: (0, 0))
    return pl.pallas_call(
        _outproj0_body,
        grid=(T // tq,),
        in_specs=[pl.BlockSpec((B, tq, D), lambda i: (0, i, 0)),
                  pl.BlockSpec((2, tq, B, D_A), lambda i: (0, i, 0, 0)),
                  tmaj, tmaj, tmaj,
                  _mod_spec_all(0, 2, B),
                  vec(D_B),
                  pl.BlockSpec((D_B, D_B), lambda i: (0, 0)),
                  vec(D_B),
                  pl.BlockSpec((D, D), lambda i: (0, 0)),
                  vec(D), vec(D)],
        out_specs=pl.BlockSpec((B, tq, D), lambda i: (0, i, 0)),
        out_shape=jax.ShapeDtypeStruct((B, T, D), F32),
        compiler_params=_cparams(("arbitrary",)),
        name="outproj0",
    )(x, h, ga, y5, ub, pmods, s5_d.reshape(1, D_B), glu_w.astype(BF16), glu_b.reshape(1, D_B),
      w_out0.astype(BF16), ln_g.reshape(1, D), ln_b.reshape(1, D))


def _router_body(x_ref, sc_ref, sh_ref, wr_ref, br_ref, u_ref, ldest_ref, gate_ref, cnt_ref):
    u = x_ref[...] * (1.0 + sc_ref[...]) + sh_ref[...]
    u_ref[...] = u.astype(BF16)
    tm = u.shape[0]
    logits = lax.dot_general(wr_ref[...], u, (((1,), (1,)), ((), ())), precision=HI,
                             preferred_element_type=F32) + br_ref[...]
    e_iota = lax.broadcasted_iota(I32, logits.shape, 0)
    work = logits
    vals, hots = [], []
    for _ in range(TOP_K):
        m = jnp.max(work, axis=0, keepdims=True)
        idx = jnp.min(jnp.where(work == m, e_iota, N_EXP), axis=0, keepdims=True)
        hot = e_iota == idx
        vals.append(m)
        hots.append(hot)
        work = jnp.where(hot, -jnp.inf, work)
    ex = [jnp.exp(v - vals[0]) for v in vals]
    den = ex[0] + ex[1] + ex[2] + ex[3]
    gate_ref[...] = jnp.concatenate([e / den for e in ex], axis=0)
    hot_sum = jnp.zeros(logits.shape, F32)
    for hot in hots:
        hot_sum = hot_sum + hot.astype(F32)
    hot_b = hot_sum.astype(BF16)
    before = lax.broadcasted_iota(I32, (tm, tm), 0) < lax.broadcasted_iota(I32, (tm, tm), 1)
    excl = jnp.dot(hot_b, jnp.where(before, 1.0, 0.0).astype(BF16), preferred_element_type=F32)
    cnt_row = lax.dot_general(jnp.ones((8, tm), BF16), hot_b, (((1,), (1,)), ((), ())),
                              preferred_element_type=F32)[0:1]
    cnt_ref[...] = cnt_row.astype(I32)
    run_len = jnp.ceil(cnt_row * (1.0 / RUN_ALIGN)) * RUN_ALIGN
    lower = lax.broadcasted_iota(I32, (N_EXP, N_EXP), 1) < lax.broadcasted_iota(I32, (N_EXP, N_EXP), 0)
    run_off = jnp.sum(jnp.where(lower, run_len, 0.0), axis=1, keepdims=True)
    base = excl + run_off
    rows = [jnp.sum(jnp.where(hot, base, 0.0), axis=0, keepdims=True) for hot in hots]
    ldest_ref[...] = jnp.concatenate(rows, axis=0).astype(I32)


def _router(x2, pmods, layer, rows_per_mod, tm, w_router, b_router):
    N, D = x2.shape
    n_t = N // tm
    mod = lambda which: pl.BlockSpec((None, None, None, 1, D),
                                     lambda i: (layer, which, (i * tm) // rows_per_mod, 0, 0))
    lane_spec = pl.BlockSpec((TOP_K, tm), lambda i: (0, i))
    return pl.pallas_call(
        _router_body,
        grid=(n_t,),
        in_specs=[pl.BlockSpec((tm, D), lambda i: (i, 0)),
                  mod(4), mod(3),
                  pl.BlockSpec((N_EXP, D), lambda i: (0, 0)),
                  pl.BlockSpec((N_EXP, 1), lambda i: (0, 0))],
        out_specs=[pl.BlockSpec((tm, D), lambda i: (i, 0)),
                   lane_spec, lane_spec,
                   pl.BlockSpec((None, 1, N_EXP), lambda i: (i, 0, 0))],
        out_shape=[jax.ShapeDtypeStruct((N, D), BF16),
                   jax.ShapeDtypeStruct((TOP_K, N), I32),
                   jax.ShapeDtypeStruct((TOP_K, N), F32),
                   jax.ShapeDtypeStruct((n_t, 1, N_EXP), I32)],
        compiler_params=_cparams(("arbitrary",)),
        name="router",
    )(x2, pmods, pmods, w_router.T, b_router.reshape(N_EXP, 1))


def _expert_body(blk_e_ref, n_used_ref, xs_ref, wgu_ref, bgu_ref, wdn_ref, bdn_ref, o_ref, wgu_s, wdn_s):
    i = pl.program_id(0)
    prev = blk_e_ref[jnp.maximum(i - 1, 0)]
    changed = jnp.logical_or(i == 0, blk_e_ref[i] != prev)

    @pl.when(changed)
    def _():
        wgu_s[...] = wgu_ref[...].astype(BF16)
        wdn_s[...] = wdn_ref[...].astype(BF16)

    @pl.when(i < n_used_ref[0])
    def _():
        h = jnp.dot(xs_ref[...], wgu_s[...], preferred_element_type=F32) + bgu_ref[...]
        gt = jnp.minimum(h[:, :D_FF], SWIGLU_LIMIT)
        up = jnp.clip(h[:, D_FF:], -SWIGLU_LIMIT, SWIGLU_LIMIT)
        act = (up + 1.0) * gt * jax.nn.sigmoid(SWIGLU_ALPHA * gt)
        y = jnp.dot(act.astype(BF16), wdn_s[...], preferred_element_type=F32) + bdn_ref[...]
        o_ref[...] = y.astype(o_ref.dtype)

    @pl.when(i >= n_used_ref[0])
    def _():
        o_ref[...] = jnp.zeros_like(o_ref)


def _experts(xs, blk_e, n_used, layer, w_gu, b_gu, w_dn, b_dn):
    slots, D = xs.shape
    te = EXPERT_TILE
    n_blk = slots // te
    grid_spec = pltpu.PrefetchScalarGridSpec(
        num_scalar_prefetch=2,
        grid=(n_blk,),
        in_specs=[pl.BlockSpec((te, D), lambda i, be, nu: (i, 0)),
                  pl.BlockSpec((None, None, D, 2 * D_FF), lambda i, be, nu: (layer, be[i], 0, 0)),
                  pl.BlockSpec((None, None, 1, 2 * D_FF), lambda i, be, nu: (layer, be[i], 0, 0)),
                  pl.BlockSpec((None, None, D_FF, D), lambda i, be, nu: (layer, be[i], 0, 0)),
                  pl.BlockSpec((None, None, 1, D), lambda i, be, nu: (layer, be[i], 0, 0))],
        out_specs=pl.BlockSpec((te, D), lambda i, be, nu: (i, 0)),
        scratch_shapes=[pltpu.VMEM((D, 2 * D_FF), BF16), pltpu.VMEM((D_FF, D), BF16)],
    )
    return pl.pallas_call(
        _expert_body,
        grid_spec=grid_spec,
        out_shape=jax.ShapeDtypeStruct((slots, D), BF16),
        compiler_params=_cparams(("arbitrary",)),
        name="experts",
    )(blk_e, n_used, xs, w_gu, b_gu.reshape(DEPTH, N_EXP, 1, 2 * D_FF), w_dn, b_dn.reshape(DEPTH, N_EXP, 1, D))


def _chunk_copy(vmem_ref, hbm_ref, sem, k, dst_chunk, to_hbm):
    local = vmem_ref.at[pl.ds(pl.multiple_of(k * RUN_ALIGN, RUN_ALIGN), RUN_ALIGN)]
    remote = hbm_ref.at[pl.ds(pl.multiple_of(dst_chunk * RUN_ALIGN, RUN_ALIGN), RUN_ALIGN)]
    return pltpu.make_async_copy(local, remote, sem) if to_hbm else pltpu.make_async_copy(remote, local, sem)


def _dispatch_body(dst_ref, u_ref, ld_ref, xs_in_ref, xs_ref, buf_s, sem, *, n_chunks):
    del xs_in_ref
    i = pl.program_id(0)
    n_t = pl.num_programs(0)
    slot = i % 2
    tm = u_ref.shape[0]
    lb = n_chunks * RUN_ALIGN
    rb = 256
    ld = ld_ref[...]
    u = u_ref[...]
    for j in range(lb // rb):
        r = lax.broadcasted_iota(I32, (rb, tm), 0) + j * rb
        p = jnp.where(r == ld[0:1], 1.0, jnp.where(r == ld[1:2], 1.0, jnp.where(r == ld[2:3], 1.0,
                                                                                    jnp.where(r == ld[3:4], 1.0, 0.0))))
        buf_s[slot, j * rb:(j + 1) * rb, :] = jnp.dot(p.astype(BF16), u, preferred_element_type=F32).astype(BF16)

    def start(k, _):
        _chunk_copy(buf_s.at[slot], xs_ref, sem.at[slot], k, dst_ref[i, k], True).start()
        return 0

    lax.fori_loop(0, n_chunks, start, 0, unroll=8)

    def drain(s):
        pltpu.make_async_copy(buf_s.at[s], xs_ref.at[pl.ds(0, lb)], sem.at[s]).wait()

    @pl.when(i > 0)
    def _():
        drain(1 - slot)

    @pl.when(i == n_t - 1)
    def _():
        drain(slot)


def _dispatch(u2, ldest, dst, xs, tm):
    N, D = u2.shape
    n_t, n_chunks = dst.shape
    lb = n_chunks * RUN_ALIGN
    grid_spec = pltpu.PrefetchScalarGridSpec(
        num_scalar_prefetch=1,
        grid=(n_t,),
        in_specs=[pl.BlockSpec((tm, D), lambda i, d: (i, 0)),
                  pl.BlockSpec((TOP_K, tm), lambda i, d: (0, i)),
                  pl.BlockSpec(memory_space=pl.ANY)],
        out_specs=pl.BlockSpec(memory_space=pl.ANY),
        scratch_shapes=[pltpu.VMEM((2, lb, D), BF16), pltpu.SemaphoreType.DMA((2,))],
    )
    return pl.pallas_call(
        functools.partial(_dispatch_body, n_chunks=n_chunks),
        grid_spec=grid_spec,
        out_shape=jax.ShapeDtypeStruct(xs.shape, xs.dtype),
        input_output_aliases={3: 0},
        compiler_params=_cparams(("arbitrary",)),
        name="moe_dispatch",
    )(dst, u2, ldest, xs)


def _combine_body(dst_ref, x_ref, ld_ref, gate_ref, g2_ref, lg_ref, lb_ref, ys_ref, o_ref, buf_s, sem, *, n_chunks):
    i = pl.program_id(0)
    n_t = pl.num_programs(0)
    slot = i % 2
    tm = x_ref.shape[0]
    lb = n_chunks * RUN_ALIGN

    def fetch(step, s):
        def body(k, _):
            _chunk_copy(buf_s.at[s], ys_ref, sem.at[s], k, dst_ref[step, k], False).start()
            return 0
        lax.fori_loop(0, n_chunks, body, 0, unroll=8)

    @pl.when(i == 0)
    def _():
        fetch(0, 0)

    @pl.when(i + 1 < n_t)
    def _():
        fetch(i + 1, 1 - slot)

    pltpu.make_async_copy(ys_ref.at[pl.ds(0, lb)], buf_s.at[slot], sem.at[slot]).wait()
    ld = ld_ref[...]
    gates = gate_ref[...]
    cb = 512
    y = jnp.zeros((tm, x_ref.shape[1]), F32)
    for j in range(lb // cb):
        c = lax.broadcasted_iota(I32, (tm, cb), 1) + j * cb
        w = jnp.zeros((tm, cb), F32)
        for k in range(TOP_K):
            w = jnp.where(c == ld[:, k:k + 1], gates[:, k:k + 1], w)
        y = y + jnp.dot(w.astype(BF16), buf_s[slot, j * cb:(j + 1) * cb, :], preferred_element_type=F32)
    v = ALPHA_DN * x_ref[...] + g2_ref[...] * y
    o_ref[...] = _layer_norm(v, lg_ref[...], lb_ref[...])


def _combine(x2, ldest_col, gates_col, dst, ys, pmods, layer, rows_per_mod, tm, ln_g, ln_b):
    N, D = x2.shape
    n_t, n_chunks = dst.shape
    lb = n_chunks * RUN_ALIGN
    vec = pl.BlockSpec((1, D), lambda i, d: (0, 0))
    grid_spec = pltpu.PrefetchScalarGridSpec(
        num_scalar_prefetch=1,
        grid=(n_t,),
        in_specs=[pl.BlockSpec((tm, D), lambda i, d: (i, 0)),
                  pl.BlockSpec((tm, TOP_K), lambda i, d: (i, 0)),
                  pl.BlockSpec((tm, TOP_K), lambda i, d: (i, 0)),
                  pl.BlockSpec((None, None, None, 1, D), lambda i, d: (layer, 5, (i * tm) // rows_per_mod, 0, 0)),
                  vec, vec,
                  pl.BlockSpec(memory_space=pl.ANY)],
        out_specs=pl.BlockSpec((tm, D), lambda i, d: (i, 0)),
        scratch_shapes=[pltpu.VMEM((2, lb, D), BF16), pltpu.SemaphoreType.DMA((2,))],
    )
    return pl.pallas_call(
        functools.partial(_combine_body, n_chunks=n_chunks),
        grid_spec=grid_spec,
        out_shape=jax.ShapeDtypeStruct((N, D), F32),
        compiler_params=_cparams(("arbitrary",)),
        name="moe_combine",
    )(dst, x2, ldest_col, gates_col, pmods, ln_g.reshape(1, D), ln_b.reshape(1, D), ys)


def _moe_plan(cnts, n_chunks, te, spare_chunk):
    a = RUN_ALIGN
    cnt = jnp.concatenate(cnts, axis=0)
    pc = (cnt + a - 1) // a * a
    seg = (jnp.sum(pc, axis=0) + te - 1) // te * te
    pad_end = jnp.cumsum(seg)
    run_start = (pad_end - seg)[None, :] + jnp.cumsum(pc, axis=0) - pc
    lo = jnp.cumsum(pc, axis=1) - pc
    tables, t0 = [], 0
    for c, nc in zip(cnts, n_chunks):
        sl = slice(t0, t0 + c.shape[0])
        t0 += c.shape[0]
        pos = jnp.arange(nc, dtype=I32) * a
        owner = jnp.sum(((lo[sl] + pc[sl])[:, None, :] <= pos[None, :, None]).astype(I32), axis=-1)
        mine = owner[..., None] == jnp.arange(N_EXP, dtype=I32)
        base = jnp.sum(jnp.where(mine, (run_start[sl] - lo[sl])[:, None, :], 0), axis=-1)
        valid = pos[None, :] < jnp.sum(pc[sl], axis=1, keepdims=True)
        parity = (jnp.arange(c.shape[0], dtype=I32) % 2)[:, None]
        spare = spare_chunk + parity * nc + jnp.arange(nc, dtype=I32)[None, :]
        tables.append(jnp.where(valid, (base + pos[None, :]) // a, spare).astype(I32))
    return tables, pad_end


def _moe(passes, layer, P):
    te = EXPERT_TILE
    routed = []
    for x, pmods, shared_mod in passes:
        B, T, D = x.shape
        N = B * T
        tm = min(MOE_TILE, N if shared_mod else T)
        x2 = x.reshape(N, D)
        u2, ldest, gates, cnt = _router(x2, pmods, layer, T, tm, P['w_router'][layer], P['b_router'][layer])
        routed.append((x2, pmods, T, tm, u2, ldest, gates, cnt[:, 0, :], x.shape))
    n_chunks = [r[3] * TOP_K // RUN_ALIGN + N_EXP for r in routed]
    n_tiles = [r[7].shape[0] for r in routed]
    n_assign = sum(r[0].shape[0] for r in routed) * TOP_K
    seg_rows = -(-(n_assign + sum(n_tiles) * N_EXP * RUN_ALIGN + N_EXP * te) // te) * te
    slots = seg_rows + -(-(2 * max(n_chunks) * RUN_ALIGN) // te) * te
    dsts, pad_end = _moe_plan([r[7] for r in routed], n_chunks, te, seg_rows // RUN_ALIGN)
    n_blk = slots // te
    blk_pos = jnp.arange(n_blk, dtype=I32) * te
    blk_e = jnp.minimum(jnp.sum((pad_end[None, :] <= blk_pos[:, None]).astype(I32), axis=1), N_EXP - 1)
    n_used = (pad_end[-1] // te).astype(I32).reshape(1)
    xs = jnp.zeros((slots, D_MODEL), BF16)
    for r, dst in zip(routed, dsts):
        xs = _dispatch(r[4], r[5], dst, xs, r[3])
    ys = _experts(xs, blk_e, n_used, layer, P['w_gu'], P['b_gu'], P['w_down'], P['b_down'])
    outs = []
    for r, dst in zip(routed, dsts):
        x2, pmods, T, tm, _, ldest, gates, _, shape = r
        y = _combine(x2, ldest.T, gates.T, dst, ys, pmods, layer, T, tm, P['ln_g'][layer, 1], P['ln_b'][layer, 1])
        outs.append(y.reshape(shape))
    return outs


def _split3(x):
    hi = x.astype(BF16)
    r1 = x - hi.astype(F32)
    mid = r1.astype(BF16)
    lo = (r1 - mid.astype(F32)).astype(BF16)
    return jnp.concatenate([hi, mid, lo], axis=1)


def _inproj1_body(x_ref, sc_ref, sh_ref, w_ref, wg2_ref, wgh_ref, bg_ref,
                  qk_ref, v_ref, o_ref, gcol_ref, grow_ref):
    u = x_ref[...] * (1.0 + sc_ref[...]) + sh_ref[...]
    ub = u.astype(BF16)
    z = jnp.dot(ub, w_ref[...], preferred_element_type=F32)
    qk_ref[...] = z[:, :2 * D_MODEL]
    v_ref[...] = z[:, 2 * D_MODEL:3 * D_MODEL].astype(BF16)
    o_ref[...] = z[:, 3 * D_MODEL:]
    tm = u.shape[0]
    nh = 4 * H_C
    u_lo = (u - ub.astype(F32)).astype(BF16)
    g2 = jnp.dot(ub, wg2_ref[...], preferred_element_type=F32)
    gc = g2[:, :nh] + g2[:, nh:] + jnp.dot(u_lo, wgh_ref[...], preferred_element_type=F32) + bg_ref[...]
    r_i = lax.broadcasted_iota(I32, (tm, tm), 0)
    c_i = lax.broadcasted_iota(I32, (tm, tm), 1)
    same = (r_i // CHUNK) == (c_i // CHUNK)
    tri_f = jnp.where(jnp.logical_and(same, c_i <= r_i), 1.0, 0.0).astype(BF16)
    tri_b = jnp.where(jnp.logical_and(same, c_i >= r_i), 1.0, 0.0).astype(BF16)
    lf3 = _split3(jax.nn.log_sigmoid(gc))

    def sum3(p, axis):
        if axis == 1:
            return p[:, :nh] + p[:, nh:2 * nh] + p[:, 2 * nh:]
        return p[:nh] + p[nh:2 * nh] + p[2 * nh:]

    col_i = lax.broadcasted_iota(I32, (tm, nh), 1)
    cum = jnp.where(col_i < 2 * H_C,
                    sum3(jnp.dot(tri_f, lf3, preferred_element_type=F32), 1),
                    sum3(jnp.dot(tri_b, lf3, preferred_element_type=F32), 1))
    gcol = jnp.where((col_i // H_C) % 2 == 1, cum, gc)
    gcol_ref[...] = gcol
    eye = jnp.where(r_i == c_i, 1.0, 0.0).astype(BF16)
    gt3 = lax.dot_general(_split3(gcol), eye, (((0,), (0,)), ((), ())), preferred_element_type=F32)
    grow_ref[...] = sum3(gt3, 0)


def _inproj1(x, mods, w_in1, w_gate1, b_gate1, row_off, row_stride):
    B, T, D = x.shape
    tm = min(TOKEN_TILE, T)
    n_j = T // tm
    nh = 4 * H_C
    wg_hi = w_gate1.astype(BF16)
    wg_lo = (w_gate1 - wg_hi.astype(F32)).astype(BF16)
    tok = lambda w: pl.BlockSpec((None, tm, w), lambda b, j: (b, j, 0))
    return pl.pallas_call(
        _inproj1_body,
        grid=(B, n_j),
        in_specs=[tok(D),
                  _mod_spec(1, 1, row_off, row_stride),
                  _mod_spec(1, 0, row_off, row_stride),
                  pl.BlockSpec((D, 4 * D), lambda b, j: (0, 0)),
                  pl.BlockSpec((D, 2 * nh), lambda b, j: (0, 0)),
                  pl.BlockSpec((D, nh), lambda b, j: (0, 0)),
                  pl.BlockSpec((1, nh), lambda b, j: (0, 0))],
        out_specs=[tok(2 * D), tok(D), tok(D), tok(nh),
                   pl.BlockSpec((nh, tm), lambda b, j: (0, b * n_j + j))],
        out_shape=[jax.ShapeDtypeStruct((B, T, 2 * D), F32),
                   jax.ShapeDtypeStruct((B, T, D), BF16),
                   jax.ShapeDtypeStruct((B, T, D), F32),
                   jax.ShapeDtypeStruct((B, T, nh), F32),
                   jax.ShapeDtypeStruct((nh, B * T), F32)],
        compiler_params=_cparams(("arbitrary", "arbitrary")),
        name="inproj1",
    )(x, mods, mods, w_in1.astype(BF16), jnp.concatenate([wg_hi, wg_lo], axis=1), wg_hi, b_gate1.reshape(1, nh))


def _mlstm_body(q_ref, k_ref, v_ref, gcol_ref, grow_ref, cw_ref, cb_ref, c0_ref, n0_ref, m0_ref,
                h_ref, cfin_ref, nfin_ref, mfin_ref, xpad_s, q_s, k_s, *, T):
    n_c = T // CHUNK
    L = CHUNK

    rb = min(T, 512)
    pad = 8

    def conv_silu_into(src_ref, w, bias, dst_s, scale):
        xpad_s[0:pad] = jnp.zeros((pad, DH_C), F32)
        xpad_s[T + pad:T + 2 * pad] = jnp.zeros((pad, DH_C), F32)
        xpad_s[pad:T + pad] = src_ref[...]

        def blk(r, _):
            off = pl.multiple_of(r * rb, pad)
            win = xpad_s[pl.ds(off, rb + 2 * pad), :]
            acc = (pltpu.roll(win, 2, 0) * w[0:1] + pltpu.roll(win, 1, 0) * w[1:2] + win * w[2:3]
                   + pltpu.roll(win, rb + 2 * pad - 1, 0) * w[3:4])[pad:rb + pad] + bias
            dst_s[pl.ds(off, rb), :] = (jax.nn.silu(acc) * scale).astype(BF16)
            return 0

        lax.fori_loop(0, T // rb, blk, 0)

    conv_silu_into(q_ref, cw_ref[0], cb_ref[0], q_s, 1.0)
    conv_silu_into(k_ref, cw_ref[1], cb_ref[1], k_s, DH_C ** -0.5)

    t_i = lax.broadcasted_iota(I32, (L, L), 0)
    s_i = lax.broadcasted_iota(I32, (L, L), 1)

    cfin_ref[...] = c0_ref[...]
    nfin_ref[...] = n0_ref[...]
    mfin_ref[...] = m0_ref[...]

    def chunk(d, c):
        mask = (s_i <= t_i) if d == 0 else (s_i >= t_i)
        last = L - 1 if d == 0 else 0
        off = pl.multiple_of(c * L, L)
        qc = q_s[pl.ds(off, L), :]
        kc = k_s[pl.ds(off, L), :]
        vc = v_ref[pl.ds(off, L), :]
        gcol = gcol_ref[pl.ds(off, L), :]
        grow = grow_ref[:, pl.ds(off, L)]
        li_c = gcol[:, 2 * d:2 * d + 1]
        b_c = gcol[:, 2 * d + 1:2 * d + 2]
        li_r = grow[2 * d:2 * d + 1, :]
        b_r = grow[2 * d + 1:2 * d + 2, :]
        m_prev = mfin_ref[d]
        log_d = jnp.where(mask, b_c - b_r + li_r, -jnp.inf)
        m_inter = b_c + m_prev
        m_t = jnp.maximum(m_inter, jnp.max(log_d, axis=-1, keepdims=True))
        dmat = jnp.exp(log_d - m_t)
        w_inter = jnp.exp(m_inter - m_t)
        c_old = cfin_ref[d]
        s = lax.dot_general(qc, kc, (((1,), (1,)), ((), ())), preferred_element_type=F32) * dmat
        inter = lax.dot_general(qc, c_old.astype(BF16), (((1,), (1,)), ((), ())), preferred_element_type=F32)
        num = w_inter * inter + jnp.dot(s.astype(BF16), vc, preferred_element_type=F32)
        qn = jnp.sum(qc.astype(F32) * nfin_ref[d], axis=-1, keepdims=True)
        den = w_inter * qn + jnp.sum(s, axis=-1, keepdims=True)
        h = num / jnp.maximum(jnp.abs(den), jnp.exp(-m_t))
        m_new = m_t[last:last + 1, :]
        b_last = b_c[last:last + 1, :]
        w_s = jnp.exp(b_last - b_c + li_c - m_new)
        decay = jnp.exp(b_last + m_prev - m_new)
        wv = (w_s * vc.astype(F32)).astype(BF16)
        cfin_ref[d] = decay * c_old + lax.dot_general(wv, kc, (((0,), (0,)), ((), ())),
                                                      preferred_element_type=F32)
        nfin_ref[d] = decay * nfin_ref[d] + jnp.sum(w_s * kc.astype(F32), axis=0, keepdims=True)
        mfin_ref[d] = m_new
        return off, h

    def both(ci, _):
        off_f, h_f = chunk(0, ci)
        off_b, h_b = chunk(1, n_c - 1 - ci)
        h_ref[pl.ds(off_f, L), :] = h_f
        xpad_s[pl.ds(off_b, L), :] = h_b
        return 0

    lax.fori_loop(0, n_c, both, 0)

    def add_bwd(r, _):
        off = pl.multiple_of(r * rb, pad)
        h_ref[pl.ds(off, rb), :] = h_ref[pl.ds(off, rb), :] + xpad_s[pl.ds(off, rb), :]
        return 0

    lax.fori_loop(0, T // rb, add_bwd, 0)


def _mlstm(qk, v, gcol, grow, conv_w, conv_b, c0, n0, m0):
    B, T, _ = v.shape
    DH = DH_C
    cw = conv_w.reshape(CONV_W, 2, H_C, DH).transpose(2, 1, 0, 3)
    cb = conv_b.reshape(2, H_C, 1, DH).transpose(1, 0, 2, 3)
    st = lambda *tail: pl.BlockSpec((None, 2, None) + tail, lambda b, h: (b, 0, h) + (0,) * len(tail))
    return pl.pallas_call(
        functools.partial(_mlstm_body, T=T),
        grid=(B, H_C),
        in_specs=[pl.BlockSpec((None, T, DH), lambda b, h: (b, 0, h)),
                  pl.BlockSpec((None, T, DH), lambda b, h: (b, 0, H_C + h)),
                  pl.BlockSpec((None, T, DH), lambda b, h: (b, 0, h)),
                  pl.BlockSpec((None, None, T, 4), lambda b, h: (b, h, 0, 0)),
                  pl.BlockSpec((None, None, 4, T), lambda b, h: (b, h, 0, 0)),
                  pl.BlockSpec((None, 2, CONV_W, DH), lambda b, h: (h, 0, 0, 0)),
                  pl.BlockSpec((None, 2, 1, DH), lambda b, h: (h, 0, 0, 0)),
                  st(DH, DH), st(1, DH), st(1, 1)],
        out_specs=[pl.BlockSpec((None, T, DH), lambda b, h: (b, 0, h)),
                   st(DH, DH), st(1, DH), st(1, 1)],
        out_shape=[jax.ShapeDtypeStruct((B, T, D_MODEL), F32),
                   jax.ShapeDtypeStruct((B, 2, H_C, DH, DH), F32),
                   jax.ShapeDtypeStruct((B, 2, H_C, 1, DH), F32),
                   jax.ShapeDtypeStruct((B, 2, H_C, 1, 1), F32)],
        scratch_shapes=[pltpu.VMEM((T + 16, DH), F32), pltpu.VMEM((T, DH), BF16), pltpu.VMEM((T, DH), BF16)],
        compiler_params=_cparams(("arbitrary", "arbitrary")),
        name="mlstm",
    )(qk, qk, v, gcol, grow, cw, cb, c0, n0, m0)


def _outproj1_body(x_ref, h_ref, o_ref, g1_ref, w_ref, lg_ref, lb_ref, out_ref):
    y = jax.nn.sigmoid(o_ref[...]) * h_ref[...]
    out = jnp.dot(y.astype(BF16), w_ref[...], preferred_element_type=F32)
    v = ALPHA_DN * x_ref[...] + g1_ref[...] * out
    out_ref[...] = _layer_norm(v, lg_ref[...], lb_ref[...])


def _outproj1(x, h, o, mods, w_out1, ln_g, ln_b, row_off, row_stride):
    B, T, D = x.shape
    tm = min(TOKEN_TILE, T)
    tok = pl.BlockSpec((None, tm, D), lambda b, j: (b, j, 0))
    vec = pl.BlockSpec((1, D), lambda b, j: (0, 0))
    return pl.pallas_call(
        _outproj1_body,
        grid=(B, T // tm),
        in_specs=[tok, tok, tok, _mod_spec(1, 2, row_off, row_stride),
                  pl.BlockSpec((D, D), lambda b, j: (0, 0)), vec, vec],
        out_specs=tok,
        out_shape=jax.ShapeDtypeStruct((B, T, D), F32),
        compiler_params=_cparams(("arbitrary", "arbitrary")),
        name="outproj1",
    )(x, h, o, mods, w_out1.astype(BF16), ln_g.reshape(1, D), ln_b.reshape(1, D))


def _to_col_major(x):
    B, T, C = x.shape
    rows = T // GRID_W
    return x.reshape(B, rows, GRID_W, C).transpose(0, 2, 1, 3).reshape(B, T, C)


def _to_row_major(x):
    B, T, C = x.shape
    rows = T // GRID_W
    return x.reshape(B, GRID_W, rows, C).transpose(0, 2, 1, 3).reshape(B, T, C)


def _mixer0(x, mods, st, P, s5_mats):
    h_rg, s_re, s_im = st
    xa, ga, ub = _inproj0(x, mods, P['w_in0'][0].astype(BF16))
    h, h_fin = _rglru(xa, P['conv_a_w'][0], P['conv_a_b'][0], P['rg_wa'][0], P['rg_ba'][0],
                      P['rg_wi'][0], P['rg_bi'][0], P['rg_lam'][0], h_rg[:, 0].transpose(1, 0, 2))
    y5, s_fin = _s5(ub, s5_mats, _s5_state_to_lanes(s_re[:, 0], s_im[:, 0]))
    x = _outproj0(x, h, ga, y5, ub, mods, P['s5_d'][0], P['glu_w'][0], P['glu_b'][0], P['w_out0'][0],
                  P['ln_g'][0, 0], P['ln_b'][0, 0])
    new_re, new_im = _s5_lanes_to_state(s_fin)
    return x, (h_fin.transpose(1, 0, 2)[:, None], new_re[:, None], new_im[:, None])


def _mixer1(x, mods, st, P):
    m_c, m_n, m_m = st
    B, T, D = x.shape
    qk, v, o, gcol, grow = _inproj1(x, mods, P['w_in1'][0], P['w_gate1'][0], P['b_gate1'][0], 0, 1)
    gcol = gcol.reshape(B, T, 2, 2, H_C).transpose(0, 4, 1, 2, 3).reshape(B, H_C, T, 4)
    grow = grow.reshape(2, 2, H_C, B, T).transpose(3, 2, 0, 1, 4).reshape(B, H_C, 4, T)
    h, c_fin, n_fin, m_fin = _mlstm(qk, v, gcol, grow, P['conv_c_w'][0], P['conv_c_b'][0],
                                    m_c[:, 0], m_n[:, 0][:, :, :, None, :], m_m[:, 0][:, :, :, None, None])
    x = _outproj1(x, h, o, mods, P['w_out1'][0], P['ln_g'][1, 0], P['ln_b'][1, 0], 0, 1)
    return x, (c_fin[:, None], n_fin[:, None, :, :, 0, :], m_fin[:, None, :, :, 0, 0])


def _forward(x_prompt, x_sample, c, c_ctx, states, P):
    bp = x_prompt.shape[0]
    bs = x_sample.shape[0]
    rows = 1 + bs
    rpad = -(-rows // 8) * 8
    cv = jnp.concatenate([c_ctx[None, :], c, jnp.zeros((rpad - rows, D_MODEL), F32)], axis=0)
    mods = _modulation(cv, P['w_mod'], P['b_mod'])
    s5_mats = _s5_matrices(P['s5_a_re'][0], P['s5_a_im'][0], P['s5_log_dt'][0], P['s5_b_re'][0], P['s5_b_im'][0],
                           P['s5_c_re'][0], P['s5_c_im'][0])
    zero_state = (jnp.zeros((bp, 1, 2, D_A), F32),
                  jnp.zeros((bp, 1, 2, G_B, P_B), F32),
                  jnp.zeros((bp, 1, 2, G_B, P_B), F32),
                  jnp.zeros((bp, 1, 2, H_C, DH_C, DH_C), F32),
                  jnp.zeros((bp, 1, 2, H_C, DH_C), F32),
                  jnp.zeros((bp, 1, 2, H_C), F32))
    mods_ctx = jnp.broadcast_to(mods[:, :, 0:1], mods.shape[:2] + (bp,) + mods.shape[3:])
    mods_lat = mods[:, :, 1:1 + bs]
    xc, new_even = _mixer0(x_prompt, mods_ctx, zero_state[:3], P, s5_mats)
    xl, _ = _mixer0(x_sample, mods_lat, states[:3], P, s5_mats)
    xc, xl = _moe([(xc, mods_ctx, True), (xl, mods_lat, False)], 0, P)
    xl = _to_col_major(xl)
    xc, new_odd = _mixer1(xc, mods_ctx, zero_state[3:], P)
    xl, _ = _mixer1(xl, mods_lat, states[3:], P)
    xc, xl = _moe([(xc, mods_ctx, True), (xl, mods_lat, False)], 1, P)
    return (xc, _to_row_major(xl)) + tuple(new_even) + tuple(new_odd)


def kernel(x_prompt, x_sample, c, c_ctx, state_rglru, state_s5_re, state_s5_im, state_mlstm_C, state_mlstm_n, state_mlstm_m, w_mod, b_mod, ln_g, ln_b, w_in0, conv_a_w, conv_a_b, rg_wa, rg_ba, rg_wi, rg_bi, rg_lam, s5_a_re, s5_a_im, s5_log_dt, s5_b_re, s5_b_im, s5_c_re, s5_c_im, s5_d, glu_w, glu_b, w_out0, w_in1, w_gate1, b_gate1, conv_c_w, conv_c_b, w_out1, w_router, b_router, w_gu, b_gu, w_down, b_down):
    P = dict(w_mod=w_mod, b_mod=b_mod, ln_g=ln_g, ln_b=ln_b, w_in0=w_in0, conv_a_w=conv_a_w,
             conv_a_b=conv_a_b, rg_wa=rg_wa, rg_ba=rg_ba, rg_wi=rg_wi, rg_bi=rg_bi, rg_lam=rg_lam,
             s5_a_re=s5_a_re, s5_a_im=s5_a_im, s5_log_dt=s5_log_dt, s5_b_re=s5_b_re, s5_b_im=s5_b_im,
             s5_c_re=s5_c_re, s5_c_im=s5_c_im, s5_d=s5_d, glu_w=glu_w, glu_b=glu_b, w_out0=w_out0,
             w_in1=w_in1, w_gate1=w_gate1, b_gate1=b_gate1, conv_c_w=conv_c_w, conv_c_b=conv_c_b,
             w_out1=w_out1, w_router=w_router, b_router=b_router, w_gu=w_gu, b_gu=b_gu,
             w_down=w_down, b_down=b_down)
    states = (state_rglru, state_s5_re, state_s5_im, state_mlstm_C, state_mlstm_n, state_mlstm_m)
    return _forward(x_prompt, x_sample, c, c_ctx, states, P)
```

```python
import functools
import math

import jax
import jax.numpy as jnp
from jax import lax
from jax.experimental import pallas as pl
from jax.experimental.pallas import tpu as pltpu

F32 = jnp.float32
BF16 = jnp.bfloat16
I32 = jnp.int32
HI = lax.Precision.HIGHEST

D_MODEL = 1024
DEPTH = 2
GRID_W = 64
D_A = 512
NB_A = 8
BS_A = D_A // NB_A
CONV_W = 4
LRU_C = 8.0
D_B = 512
S5_GROUP = 16
G_B = D_B // S5_GROUP
P_B = 64
H_C = 4
DH_C = D_MODEL // H_C
CHUNK = 128
N_EXP = 32
TOP_K = 4
D_FF = D_MODEL
SWIGLU_LIMIT = 7.0
SWIGLU_ALPHA = 1.702
ALPHA_DN = (2 * DEPTH) ** 0.25
LN_EPS = 1e-5

S5_L = 16
S5_W = S5_L * S5_GROUP
S5_OCT = 8
TOKEN_TILE = 256
EXPERT_TILE = 512
MOE_TILE = 512
RUN_ALIGN = 16
VMEM_LIMIT = 56 * 1024 * 1024


def _cparams(sem, vmem=VMEM_LIMIT):
    return pltpu.CompilerParams(dimension_semantics=sem, vmem_limit_bytes=vmem)


def _layer_norm(v, g, b):
    mu = jnp.mean(v, axis=-1, keepdims=True)
    c = v - mu
    var = jnp.mean(c * c, axis=-1, keepdims=True)
    return c * lax.rsqrt(var + LN_EPS) * g + b


def _mod_spec(layer, which, row_off, row_stride):
    return pl.BlockSpec((None, None, None, 1, D_MODEL),
                        lambda b, j: (layer, which, row_off + b * row_stride, 0, 0))


def _mod_spec_all(layer, which, nb):
    return pl.BlockSpec((None, None, nb, 1, D_MODEL), lambda i: (layer, which, 0, 0, 0))


def _row_permutation(n_out_major, n_out_minor):
    n = n_out_major * n_out_minor
    r_out = lax.broadcasted_iota(I32, (n, n), 0)
    r_in = lax.broadcasted_iota(I32, (n, n), 1)
    hit = jnp.logical_and(r_out // n_out_minor == r_in % n_out_major, r_out % n_out_minor == r_in // n_out_major)
    return jnp.where(hit, 1.0, 0.0).astype(BF16)


def _mod_body(c_ref, w_ref, b_ref, o_ref):
    s = jax.nn.silu(c_ref[...])
    o_ref[...] = jnp.dot(s, w_ref[...], precision=HI, preferred_element_type=F32) + b_ref[...]


def _modulation(cv, w_mod, b_mod):
    R, D = cv.shape
    L, _, N6 = w_mod.shape
    tn = N6 // 6
    out = pl.pallas_call(
        _mod_body,
        grid=(L, N6 // tn),
        in_specs=[pl.BlockSpec((R, D), lambda l, j: (0, 0)),
                  pl.BlockSpec((None, D, tn), lambda l, j: (l, 0, j)),
                  pl.BlockSpec((None, 1, tn), lambda l, j: (l, 0, j))],
        out_specs=pl.BlockSpec((None, R, tn), lambda l, j: (l, 0, j)),
        out_shape=jax.ShapeDtypeStruct((L, R, N6), F32),
        compiler_params=_cparams(("arbitrary", "arbitrary")),
        name="modulation",
    )(cv, w_mod, b_mod.reshape(L, 1, N6))
    return out.reshape(L, R, 6, D).transpose(0, 2, 1, 3)[:, :, :, None, :]


def _inproj0_body(x_ref, sc_ref, sh_ref, w_ref, xa_ref, ga_ref, ub_ref):
    nb, tq, d = x_ref.shape
    u = x_ref[...] * (1.0 + sc_ref[...]) + sh_ref[...]
    ub = u.reshape(nb * tq, d).astype(BF16)
    ut = jnp.dot(_row_permutation(tq, nb), ub, preferred_element_type=F32).astype(BF16)
    z = jnp.dot(ut, w_ref[...], preferred_element_type=F32).reshape(tq, nb, w_ref.shape[1])
    xa_ref[...] = z[:, :, :D_A]
    ga_ref[...] = z[:, :, D_A:2 * D_A]
    ub_ref[...] = z[:, :, 2 * D_A:]


def _inproj0(x, pmods, w_in0):
    B, T, D = x.shape
    tq = TOKEN_TILE // B
    nz = w_in0.shape[1]
    tmaj = pl.BlockSpec((tq, B, D_A), lambda i: (i, 0, 0))
    return pl.pallas_call(
        _inproj0_body,
        grid=(T // tq,),
        in_specs=[pl.BlockSpec((B, tq, D), lambda i: (0, i, 0)),
                  _mod_spec_all(0, 1, B), _mod_spec_all(0, 0, B),
                  pl.BlockSpec((D, nz), lambda i: (0, 0))],
        out_specs=[tmaj, tmaj, tmaj],
        out_shape=[jax.ShapeDtypeStruct((T, B, D_A), F32)] * 3,
        compiler_params=_cparams(("arbitrary",)),
        name="inproj0",
    )(x, pmods, pmods, w_in0)


def _rglru_body(cur_ref, prev_ref, next_ref, cw_ref, cb_ref, wa_ref, wi_ref, ba_ref, bi_ref, lam_ref, h0_ref,
                h_ref, hfin_ref, ext_s, a_s, b_s, carry_s, *, tt, n_t):
    d = pl.program_id(0)
    j = pl.program_id(1)
    jj = j + d * (n_t - 1 - 2 * j)
    nb = cur_ref.shape[1]
    half = D_A // 2
    ext_s[0:2] = jnp.where(jj == 0, 0.0, prev_ref[...])
    ext_s[2:tt + 2] = cur_ref[...]
    ext_s[tt + 2:tt + 3] = jnp.where(jj == n_t - 1, 0.0, next_ref[...])
    xc = ext_s[0:tt] * cw_ref[0]
    for k in range(1, CONV_W):
        xc = xc + ext_s[k:k + tt] * cw_ref[k]
    xc = xc + cb_ref[...]
    x2 = xc.reshape(tt * nb, D_A)
    xb = x2.astype(BF16)

    def gate(w_ref, bias_ref):
        lo = jnp.dot(xb[:, :half], w_ref[0], preferred_element_type=F32)
        hi = jnp.dot(xb[:, half:], w_ref[1], preferred_element_type=F32)
        return jax.nn.sigmoid(jnp.concatenate([lo, hi], axis=1) + bias_ref[...])

    r = gate(wa_ref, ba_ref)
    i = gate(wi_ref, bi_ref)
    log_a = LRU_C * r * jax.nn.log_sigmoid(lam_ref[...])
    a = jnp.exp(log_a)
    one_minus_a2 = -jnp.tanh(log_a) * (a * a + 1.0)
    a_s[...] = a.reshape(tt, nb, D_A)
    b_s[...] = (jnp.sqrt(one_minus_a2) * (i * x2)).reshape(tt, nb, D_A)

    @pl.when(j == 0)
    def _():
        carry_s[...] = h0_ref[...]

    def step(t, h):
        h = a_s[t] * h + b_s[t]
        h_ref[t] = h
        return h

    @pl.when(d == 0)
    def _():
        carry_s[...] = lax.fori_loop(0, tt, step, carry_s[...], unroll=8)

    @pl.when(d == 1)
    def _():
        carry_s[...] = lax.fori_loop(0, tt, lambda t, h: step(tt - 1 - t, h), carry_s[...], unroll=8)

    hfin_ref[...] = carry_s[...]


def _block_diag_halves(w):
    nd = w.shape[0]
    per = NB_A // 2
    w = w.reshape(nd, 2, per, BS_A, BS_A)
    eye = jnp.eye(per, dtype=w.dtype)
    out = jnp.einsum('dhncz,nm->dhncmz', w, eye)
    return out.reshape(nd, 2, per * BS_A, per * BS_A)


def _rglru(x3, conv_w, conv_b, wa, ba, wi, bi, lam, h0):
    T, B, _ = x3.shape
    tt = min(T, 1024 // B)
    n_t = T // tt

    def cur_map(d, j):
        return (j + d * (n_t - 1 - 2 * j), 0, 0)

    def prev_map(d, j):
        jj = j + d * (n_t - 1 - 2 * j)
        return (jnp.maximum(jj * (tt // 2) - 1, 0), 0, 0)

    def next_map(d, j):
        jj = j + d * (n_t - 1 - 2 * j)
        return (jnp.minimum((jj + 1) * tt, T - 1), 0, 0)

    dir_spec3 = pl.BlockSpec((None, 1, D_A), lambda d, j: (d, 0, 0))
    h, hfin = pl.pallas_call(
        functools.partial(_rglru_body, tt=tt, n_t=n_t),
        grid=(2, n_t),
        in_specs=[pl.BlockSpec((tt, B, D_A), cur_map),
                  pl.BlockSpec((2, B, D_A), prev_map),
                  pl.BlockSpec((1, B, D_A), next_map),
                  pl.BlockSpec((CONV_W, 1, D_A), lambda d, j: (0, 0, 0)),
                  pl.BlockSpec((1, 1, D_A), lambda d, j: (0, 0, 0)),
                  pl.BlockSpec((None, 2, D_A // 2, D_A // 2), lambda d, j: (d, 0, 0, 0)),
                  pl.BlockSpec((None, 2, D_A // 2, D_A // 2), lambda d, j: (d, 0, 0, 0)),
                  dir_spec3, dir_spec3, dir_spec3,
                  pl.BlockSpec((None, B, D_A), lambda d, j: (d, 0, 0))],
        out_specs=[pl.BlockSpec((None, tt, B, D_A), lambda d, j: (d,) + cur_map(d, j)),
                   pl.BlockSpec((None, B, D_A), lambda d, j: (d, 0, 0))],
        out_shape=[jax.ShapeDtypeStruct((2, T, B, D_A), F32),
                   jax.ShapeDtypeStruct((2, B, D_A), F32)],
        scratch_shapes=[pltpu.VMEM((tt + 3, B, D_A), F32),
                        pltpu.VMEM((tt, B, D_A), F32),
                        pltpu.VMEM((tt, B, D_A), F32),
                        pltpu.VMEM((B, D_A), F32)],
        compiler_params=_cparams(("arbitrary", "arbitrary")),
        name="rglru",
    )(x3, x3, x3, conv_w.reshape(CONV_W, 1, D_A), conv_b.reshape(1, 1, D_A),
      _block_diag_halves(wa).astype(BF16), _block_diag_halves(wi).astype(BF16),
      ba.reshape(2, 1, D_A), bi.reshape(2, 1, D_A), lam.reshape(2, 1, D_A), h0)
    return h, hfin


def _s5_matrices(a_re, a_im, log_dt, b_re, b_im, c_re, c_im):
    L = S5_L
    lam = lax.complex(a_re.astype(F32), a_im.astype(F32))
    dt = jnp.exp(log_dt.astype(F32))[..., None]
    ldt = lam * dt
    a_bar = jnp.exp(ldt)
    b_bar = ((a_bar - 1.0) / lam)[..., None] * lax.complex(b_re.astype(F32), b_im.astype(F32))
    cc = lax.complex(c_re.astype(F32), c_im.astype(F32))
    ks = jnp.arange(L + 1, dtype=F32)
    pw = jnp.exp(ldt[:, :, None, :] * ks[None, None, :, None].astype(jnp.complex64))
    kern = jnp.real(jnp.einsum('dgjp,dgkp,dgpi->dgkji', cc, pw[:, :, :L], b_bar, precision=HI))
    s_idx = jnp.arange(L)[:, None]
    t_idx = jnp.arange(L)[None, :]
    lag_f = jnp.clip(t_idx - s_idx, 0, L - 1)
    lag_b = jnp.clip(s_idx - t_idx, 0, L - 1)
    m_f = jnp.where((t_idx >= s_idx)[None, :, :, None, None], kern[0][:, lag_f], 0.0)
    m_b = jnp.where((s_idx >= t_idx)[None, :, :, None, None], kern[1][:, lag_b], 0.0)
    m = (m_f + m_b).transpose(0, 1, 4, 2, 3).reshape(G_B, S5_W, S5_W)
    down = (L - ks[:L])[None, :, None].astype(jnp.complex64)
    pw_down_b = jnp.exp(ldt[1][:, None, :] * down)
    g_f = jnp.exp(ldt[0][:, None, :] * (down - 1.0))[..., None] * b_bar[0][:, None]
    g_b = pw[1][:, :L, :, None] * b_bar[1][:, None]

    def g_cols(x):
        return x.transpose(0, 1, 3, 2).reshape(G_B, S5_W, P_B)

    gs = jnp.concatenate([g_cols(jnp.real(g_f)), g_cols(jnp.real(g_b)),
                          g_cols(jnp.imag(g_f)), g_cols(jnp.imag(g_b))], axis=-1)
    e_f = cc[0][:, None] * pw[0][:, 1:, None, :]
    e_b = cc[1][:, None] * pw_down_b[:, :, None, :]

    def e_rows(x):
        return x.transpose(0, 3, 1, 2).reshape(G_B, P_B, S5_W)

    e = jnp.concatenate([e_rows(jnp.real(e_f)), e_rows(jnp.real(e_b)),
                         -e_rows(jnp.imag(e_f)), -e_rows(jnp.imag(e_b))], axis=1)
    a_l = pw[:, :, L]
    al = jnp.concatenate([jnp.real(a_l[0]), jnp.real(a_l[1]), jnp.imag(a_l[0]), jnp.imag(a_l[1])], axis=-1)
    no = G_B // S5_OCT
    ow = S5_OCT * S5_W
    rows_sgi = lambda x: x.reshape(no, S5_OCT, L, S5_GROUP, S5_W).transpose(0, 2, 1, 3, 4).reshape(no, ow, S5_W)
    src = jnp.stack([rows_sgi(m), rows_sgi(gs), e.reshape(no, ow, S5_W)]).astype(BF16)
    r = jnp.arange(ow, dtype=I32)
    c = jnp.arange(S5_W, dtype=I32)
    grp_sgi = (r // S5_GROUP) % S5_OCT
    grp_gl = r // S5_W
    src_sgi = (r // (S5_OCT * S5_GROUP)) * S5_GROUP + r % S5_GROUP
    src_gl = r % S5_W
    spread = jnp.stack([src_sgi, src_gl, src_sgi])[:, None, :] == c[None, :, None]
    row_grp = jnp.stack([grp_sgi, grp_sgi, grp_gl])[:, :, None]
    col_grp = jnp.stack([grp_sgi, grp_gl, grp_sgi])[:, None, :]
    tr = 512
    out = pl.pallas_call(
        _s5_expand_body,
        grid=(3, no, ow // tr),
        in_specs=[pl.BlockSpec((None, None, tr, S5_W), lambda k, q, i: (k, q, i, 0)),
                  pl.BlockSpec((None, S5_W, ow), lambda k, q, i: (k, 0, 0)),
                  pl.BlockSpec((None, tr, 1), lambda k, q, i: (k, i, 0)),
                  pl.BlockSpec((None, 1, ow), lambda k, q, i: (k, 0, 0))],
        out_specs=pl.BlockSpec((None, None, tr, ow), lambda k, q, i: (k, q, i, 0)),
        out_shape=jax.ShapeDtypeStruct((3, no, ow, ow), BF16),
        compiler_params=_cparams(("arbitrary", "arbitrary", "arbitrary")),
        name="s5_expand",
    )(src, spread.astype(BF16), row_grp, col_grp)
    return out, al.reshape(1, G_B * S5_W)


def _s5_expand_body(src_ref, spread_ref, rg_ref, cg_ref, o_ref):
    wide = jnp.dot(src_ref[...], spread_ref[...], preferred_element_type=F32)
    o_ref[...] = jnp.where(rg_ref[...] == cg_ref[...], wide, 0.0).astype(BF16)


def _s5_fill_lhs(u_ref, lhs_s):
    tc, _, nb, lanes = u_ref.shape
    for s in range(S5_L):
        lhs_s[:, s * lanes:(s + 1) * lanes] = u_ref[:, s].reshape(tc * nb, lanes).astype(BF16)


def _s5_state_body(u_ref, g_ref, f_ref, lhs_s):
    _s5_fill_lhs(u_ref, lhs_s)
    f_ref[...] = jnp.dot(lhs_s[...], g_ref[...], preferred_element_type=F32).reshape(f_ref.shape)


def _s5_scan_body(f_ref, a_ref, s0_ref, sin_ref, sfin_ref, *, n_c):
    nb, width = s0_ref.shape
    hw = 2 * P_B
    n_g = width // S5_W
    a = a_ref[...]
    a_re = [a[:, k * S5_W:k * S5_W + hw] for k in range(n_g)]
    a_im = [a[:, k * S5_W + hw:(k + 1) * S5_W] for k in range(n_g)]
    is_fwd = (lax.broadcasted_iota(I32, (nb, width), 1) % hw) < P_B

    def split(x):
        return tuple(x[:, k * hw:(k + 1) * hw] for k in range(2 * n_g))

    def merge(parts):
        return jnp.concatenate(parts, axis=-1)

    def advance(c, parts):
        f = split(f_ref[c])
        out = []
        for k in range(n_g):
            s_re, s_im = parts[2 * k], parts[2 * k + 1]
            out.append(a_re[k] * s_re - a_im[k] * s_im + f[2 * k])
            out.append(a_re[k] * s_im + a_im[k] * s_re + f[2 * k + 1])
        return tuple(out)

    def fwd(c, parts):
        sin_ref[c] = merge(parts)
        return advance(c, parts)

    init = split(s0_ref[...])
    fin_f = lax.fori_loop(0, n_c, fwd, init)

    def bwd(k, parts):
        c = n_c - 1 - k
        sin_ref[c] = jnp.where(is_fwd, sin_ref[c], merge(parts))
        return advance(c, parts)

    fin_b = lax.fori_loop(0, n_c, bwd, init)
    sfin_ref[...] = jnp.where(is_fwd, merge(fin_f), merge(fin_b))


def _s5_out_body(u_ref, sin_ref, m_ref, e_ref, y_ref, lhs_s):
    _s5_fill_lhs(u_ref, lhs_s)
    tc, _, nb, lanes = u_ref.shape
    sin = sin_ref[...].reshape(tc * nb, sin_ref.shape[-1]).astype(BF16)
    y = (jnp.dot(lhs_s[...], m_ref[...], preferred_element_type=F32)
         + jnp.dot(sin, e_ref[...], preferred_element_type=F32))
    for s in range(S5_L):
        y_ref[:, s] = y[:, s * lanes:(s + 1) * lanes].reshape(tc, nb, lanes)


def _s5(ub, mats, s0):
    mge, al = mats
    T, B, _ = ub.shape
    n_c = T // S5_L
    no = G_B // S5_OCT
    lanes = S5_OCT * S5_GROUP
    ow = S5_OCT * S5_W
    tc = min(n_c, TOKEN_TILE // B)
    u4 = ub.reshape(n_c, S5_L, B, D_B)
    u_spec = pl.BlockSpec((tc, S5_L, B, lanes), lambda q, i: (i, 0, 0, q))
    w_spec = lambda kind: pl.BlockSpec((None, None, ow, ow), lambda q, i: (kind, q, 0, 0))
    st_spec = pl.BlockSpec((tc, B, ow), lambda q, i: (i, 0, q))
    f_loc = pl.pallas_call(
        _s5_state_body,
        grid=(no, n_c // tc),
        in_specs=[u_spec, w_spec(1)],
        out_specs=st_spec,
        out_shape=jax.ShapeDtypeStruct((n_c, B, G_B * S5_W), F32),
        scratch_shapes=[pltpu.VMEM((tc * B, S5_L * lanes), BF16)],
        compiler_params=_cparams(("arbitrary", "arbitrary")),
        name="s5_state",
    )(u4, mge)
    sw = 4 * S5_W
    sin, sfin = pl.pallas_call(
        functools.partial(_s5_scan_body, n_c=n_c),
        grid=(G_B * S5_W // sw,),
        in_specs=[pl.BlockSpec((n_c, B, sw), lambda g: (0, 0, g)),
                  pl.BlockSpec((1, sw), lambda g: (0, g)),
                  pl.BlockSpec((B, sw), lambda g: (0, g))],
        out_specs=[pl.BlockSpec((n_c, B, sw), lambda g: (0, 0, g)),
                   pl.BlockSpec((B, sw), lambda g: (0, g))],
        out_shape=[jax.ShapeDtypeStruct((n_c, B, G_B * S5_W), F32),
                   jax.ShapeDtypeStruct((B, G_B * S5_W), F32)],
        compiler_params=_cparams(("arbitrary",)),
        name="s5_scan",
    )(f_loc, al, s0)
    y = pl.pallas_call(
        _s5_out_body,
        grid=(no, n_c // tc),
        in_specs=[u_spec, st_spec, w_spec(0), w_spec(2)],
        out_specs=u_spec,
        out_shape=jax.ShapeDtypeStruct((n_c, S5_L, B, D_B), F32),
        scratch_shapes=[pltpu.VMEM((tc * B, S5_L * lanes), BF16)],
        compiler_params=_cparams(("arbitrary", "arbitrary")),
        name="s5_out",
    )(u4, sin, mge, mge)
    return y.reshape(T, B, D_B), sfin


def _s5_state_to_lanes(s_re, s_im):
    parts = [s_re[:, 0], s_re[:, 1], s_im[:, 0], s_im[:, 1]]
    return jnp.concatenate(parts, axis=-1).reshape(s_re.shape[0], G_B * S5_W)


def _s5_lanes_to_state(s):
    s = s.reshape(s.shape[0], G_B, 4, P_B)
    return jnp.stack([s[:, :, 0], s[:, :, 1]], axis=1), jnp.stack([s[:, :, 2], s[:, :, 3]], axis=1)


def _outproj0_body(x_ref, h_ref, ga_ref, y_ref, ub_ref, g1_ref, d_ref, gw_ref, gb_ref, wo_ref,
                   lg_ref, lb_ref, o_ref):
    nb, tq, d = x_ref.shape
    rows = tq * nb

    def flat(v):
        return v.reshape(rows, v.shape[-1])

    ya = flat(h_ref[0] + h_ref[1]) * jax.nn.gelu(flat(ga_ref[...]))
    yb = flat(y_ref[...]) + d_ref[...] * flat(ub_ref[...])
    g = jax.nn.gelu(yb)
    gate = jax.nn.sigmoid(jnp.dot(g.astype(BF16), gw_ref[...], preferred_element_type=F32) + gb_ref[...])
    cat = jnp.concatenate([ya, g * gate], axis=1).astype(BF16)
    cat = jnp.dot(_row_permutation(nb, tq), cat, preferred_element_type=F32).astype(BF16)
    out = jnp.dot(cat, wo_ref[...], preferred_element_type=F32).reshape(nb, tq, d)
    v = ALPHA_DN * x_ref[...] + g1_ref[...] * out
    o_ref[...] = _layer_norm(v, lg_ref[...], lb_ref[...])


def _outproj0(x, h, ga, y5, ub, pmods, s5_d, glu_w, glu_b, w_out0, ln_g, ln_b):
    B, T, D = x.shape
    tq = TOKEN_TILE // B
    tmaj = pl.BlockSpec((tq, B, D_A), lambda i: (i, 0, 0))
    vec = lambda w: pl.BlockSpec((1, w), lambda i: (0, 0))
    return pl.pallas_call(
        _outproj0_body,
        grid=(T // tq,),
        in_specs=[pl.BlockSpec((B, tq, D), lambda i: (0, i, 0)),
                  pl.BlockSpec((2, tq, B, D_A), lambda i: (0, i, 0, 0)),
                  tmaj, tmaj, tmaj,
                  _mod_spec_all(0, 2, B),
                  vec(D_B),
                  pl.BlockSpec((D_B, D_B), lambda i: (0, 0)),
                  vec(D_B),
                  pl.BlockSpec((D, D), lambda i: (0, 0)),
                  vec(D), vec(D)],
        out_specs=pl.BlockSpec((B, tq, D), lambda i: (0, i, 0)),
        out_shape=jax.ShapeDtypeStruct((B, T, D), F32),
        compiler_params=_cparams(("arbitrary",)),
        name="outproj0",
    )(x, h, ga, y5, ub, pmods, s5_d.reshape(1, D_B), glu_w.astype(BF16), glu_b.reshape(1, D_B),
      w_out0.astype(BF16), ln_g.reshape(1, D), ln_b.reshape(1, D))


def _router_body(x_ref, sc_ref, sh_ref, wr_ref, br_ref, u_ref, ldest_ref, gate_ref, cnt_ref):
    u = x_ref[...] * (1.0 + sc_ref[...]) + sh_ref[...]
    u_ref[...] = u.astype(BF16)
    tm = u.shape[0]
    logits = lax.dot_general(wr_ref[...], u, (((1,), (1,)), ((), ())), precision=HI,
                             preferred_element_type=F32) + br_ref[...]
    e_iota = lax.broadcasted_iota(I32, logits.shape, 0)
    work = logits
    vals, hots = [], []
    for _ in range(TOP_K):
        m = jnp.max(work, axis=0, keepdims=True)
        idx = jnp.min(jnp.where(work == m, e_iota, N_EXP), axis=0, keepdims=True)
        hot = e_iota == idx
        vals.append(m)
        hots.append(hot)
        work = jnp.where(hot, -jnp.inf, work)
    ex = [jnp.exp(v - vals[0]) for v in vals]
    den = ex[0] + ex[1] + ex[2] + ex[3]
    gate_ref[...] = jnp.concatenate([e / den for e in ex], axis=0)
    hot_sum = jnp.zeros(logits.shape, F32)
    for hot in hots:
        hot_sum = hot_sum + hot.astype(F32)
    hot_b = hot_sum.astype(BF16)
    before = lax.broadcasted_iota(I32, (tm, tm), 0) < lax.broadcasted_iota(I32, (tm, tm), 1)
    excl = jnp.dot(hot_b, jnp.where(before, 1.0, 0.0).astype(BF16), preferred_element_type=F32)
    cnt_row = lax.dot_general(jnp.ones((8, tm), BF16), hot_b, (((1,), (1,)), ((), ())),
                              preferred_element_type=F32)[0:1]
    cnt_ref[...] = cnt_row.astype(I32)
    run_len = jnp.ceil(cnt_row * (1.0 / RUN_ALIGN)) * RUN_ALIGN
    lower = lax.broadcasted_iota(I32, (N_EXP, N_EXP), 1) < lax.broadcasted_iota(I32, (N_EXP, N_EXP), 0)
    run_off = jnp.sum(jnp.where(lower, run_len, 0.0), axis=1, keepdims=True)
    base = excl + run_off
    rows = [jnp.sum(jnp.where(hot, base, 0.0), axis=0, keepdims=True) for hot in hots]
    ldest_ref[...] = jnp.concatenate(rows, axis=0).astype(I32)


def _router(x2, pmods, layer, rows_per_mod, tm, w_router, b_router):
    N, D = x2.shape
    n_t = N // tm
    mod = lambda which: pl.BlockSpec((None, None, None, 1, D),
                                     lambda i: (layer, which, (i * tm) // rows_per_mod, 0, 0))
    lane_spec = pl.BlockSpec((TOP_K, tm), lambda i: (0, i))
    return pl.pallas_call(
        _router_body,
        grid=(n_t,),
        in_specs=[pl.BlockSpec((tm, D), lambda i: (i, 0)),
                  mod(4), mod(3),
                  pl.BlockSpec((N_EXP, D), lambda i: (0, 0)),
                  pl.BlockSpec((N_EXP, 1), lambda i: (0, 0))],
        out_specs=[pl.BlockSpec((tm, D), lambda i: (i, 0)),
                   lane_spec, lane_spec,
                   pl.BlockSpec((None, 1, N_EXP), lambda i: (i, 0, 0))],
        out_shape=[jax.ShapeDtypeStruct((N, D), BF16),
                   jax.ShapeDtypeStruct((TOP_K, N), I32),
                   jax.ShapeDtypeStruct((TOP_K, N), F32),
                   jax.ShapeDtypeStruct((n_t, 1, N_EXP), I32)],
        compiler_params=_cparams(("arbitrary",)),
        name="router",
    )(x2, pmods, pmods, w_router.T, b_router.reshape(N_EXP, 1))


def _expert_body(blk_e_ref, n_used_ref, xs_ref, wgu_ref, bgu_ref, wdn_ref, bdn_ref, o_ref, wgu_s, wdn_s):
    i = pl.program_id(0)
    prev = blk_e_ref[jnp.maximum(i - 1, 0)]
    changed = jnp.logical_or(i == 0, blk_e_ref[i] != prev)

    @pl.when(changed)
    def _():
        wgu_s[...] = wgu_ref[...].astype(BF16)
        wdn_s[...] = wdn_ref[...].astype(BF16)

    @pl.when(i < n_used_ref[0])
    def _():
        h = jnp.dot(xs_ref[...], wgu_s[...], preferred_element_type=F32) + bgu_ref[...]
        gt = jnp.minimum(h[:, :D_FF], SWIGLU_LIMIT)
        up = jnp.clip(h[:, D_FF:], -SWIGLU_LIMIT, SWIGLU_LIMIT)
        act = (up + 1.0) * gt * jax.nn.sigmoid(SWIGLU_ALPHA * gt)
        y = jnp.dot(act.astype(BF16), wdn_s[...], preferred_element_type=F32) + bdn_ref[...]
        o_ref[...] = y.astype(o_ref.dtype)

    @pl.when(i >= n_used_ref[0])
    def _():
        o_ref[...] = jnp.zeros_like(o_ref)


def _experts(xs, blk_e, n_used, layer, w_gu, b_gu, w_dn, b_dn):
    slots, D = xs.shape
    te = EXPERT_TILE
    n_blk = slots // te
    grid_spec = pltpu.PrefetchScalarGridSpec(
        num_scalar_prefetch=2,
        grid=(n_blk,),
        in_specs=[pl.BlockSpec((te, D), lambda i, be, nu: (jnp.minimum(i, nu[0] - 1), 0)),
                  pl.BlockSpec((None, None, D, 2 * D_FF), lambda i, be, nu: (layer, be[i], 0, 0)),
                  pl.BlockSpec((None, None, 1, 2 * D_FF), lambda i, be, nu: (layer, be[i], 0, 0)),
                  pl.BlockSpec((None, None, D_FF, D), lambda i, be, nu: (layer, be[i], 0, 0)),
                  pl.BlockSpec((None, None, 1, D), lambda i, be, nu: (layer, be[i], 0, 0))],
        out_specs=pl.BlockSpec((te, D), lambda i, be, nu: (i, 0)),
        scratch_shapes=[pltpu.VMEM((D, 2 * D_FF), BF16), pltpu.VMEM((D_FF, D), BF16)],
    )
    return pl.pallas_call(
        _expert_body,
        grid_spec=grid_spec,
        out_shape=jax.ShapeDtypeStruct((slots, D), BF16),
        compiler_params=_cparams(("arbitrary",)),
        name="experts",
    )(blk_e, n_used, xs, w_gu, b_gu.reshape(DEPTH, N_EXP, 1, 2 * D_FF), w_dn, b_dn.reshape(DEPTH, N_EXP, 1, D))


def _chunk_copy(vmem_ref, hbm_ref, sem, k, dst_chunk, to_hbm):
    row = k * RUN_ALIGN if isinstance(k, int) else pl.multiple_of(k * RUN_ALIGN, RUN_ALIGN)
    local = vmem_ref.at[pl.ds(row, RUN_ALIGN)]
    remote = hbm_ref.at[pl.ds(pl.multiple_of(dst_chunk * RUN_ALIGN, RUN_ALIGN), RUN_ALIGN)]
    return pltpu.make_async_copy(local, remote, sem) if to_hbm else pltpu.make_async_copy(remote, local, sem)


def _dispatch_body(dst_ref, tail_ref, nu_ref, *rest, n_chunks, tiles, n_blk):
    n_p = len(tiles)
    xs_ref, buf_s, zero_s, sem = rest[2 * n_p:]
    i = pl.program_id(0)
    n_t = pl.num_programs(0)
    slot = i % 2
    lb = n_chunks * RUN_ALIGN
    te = zero_s.shape[0]
    rb = 256

    def sort_rows(u_ref, ld_ref):
        tm = u_ref.shape[0]
        ld = ld_ref[...]
        u = u_ref[...]
        for j in range(lb // rb):
            r = lax.broadcasted_iota(I32, (rb, tm), 0) + j * rb
            p = jnp.where(r == ld[0:1], 1.0, jnp.where(r == ld[1:2], 1.0, jnp.where(
                r == ld[2:3], 1.0, jnp.where(r == ld[3:4], 1.0, 0.0))))
            buf_s[slot, j * rb:(j + 1) * rb, :] = jnp.dot(p.astype(BF16), u,
                                                          preferred_element_type=F32).astype(BF16)

    for p, (first, count) in enumerate(tiles):
        @pl.when(jnp.logical_and(i >= first, i < first + count))
        def _(p=p):
            sort_rows(rest[2 * p], rest[2 * p + 1])

    def start(k, _):
        _chunk_copy(buf_s.at[slot], xs_ref, sem.at[slot], k, dst_ref[i, k], True).start()
        return 0

    lax.fori_loop(0, n_chunks, start, 0, unroll=8)

    def drain(s):
        pltpu.make_async_copy(buf_s.at[s], xs_ref.at[pl.ds(0, lb)], sem.at[s]).wait()

    @pl.when(i > 0)
    def _():
        drain(1 - slot)

    @pl.when(i == n_t - 1)
    def _():
        drain(slot)
        zero_s[...] = jnp.zeros_like(zero_s)
        n_tail = tail_ref.shape[0]

        def tail_copy(t):
            return _chunk_copy(zero_s, xs_ref, sem.at[2], 0, tail_ref[t], True)

        def block_copy(k):
            return pltpu.make_async_copy(zero_s, xs_ref.at[pl.ds(pl.multiple_of(k * te, te), te)], sem.at[2])

        def each(n, pred, copy, wait):
            def body(t, _):
                @pl.when(pred(t))
                def _():
                    copy(t).wait() if wait else copy(t).start()
                return 0
            lax.fori_loop(0, n, body, 0)

        for wait in (False, True):
            each(n_tail, lambda t: tail_ref[t] >= 0, tail_copy, wait)
            each(n_blk, lambda k: k >= nu_ref[0], block_copy, wait)


def _dispatch(u2s, ldests, dst, tails, n_used, slots, tm, te):
    D = u2s[0].shape[1]
    n_t, n_chunks = dst.shape
    lb = n_chunks * RUN_ALIGN
    tiles, first = [], 0
    for u2 in u2s:
        tiles.append((first, u2.shape[0] // tm))
        first += u2.shape[0] // tm
    assert first == n_t

    def own_tile(first, count):
        return lambda i, d, t, nu: jnp.clip(i - first, 0, count - 1)

    in_specs, operands = [], []
    for (first, count), u2, ld in zip(tiles, u2s, ldests):
        tile = own_tile(first, count)
        in_specs += [pl.BlockSpec((tm, D), lambda i, d, t, nu, tile=tile: (tile(i, d, t, nu), 0)),
                     pl.BlockSpec((TOP_K, tm), lambda i, d, t, nu, tile=tile: (0, tile(i, d, t, nu)))]
        operands += [u2, ld]
    grid_spec = pltpu.PrefetchScalarGridSpec(
        num_scalar_prefetch=3,
        grid=(n_t,),
        in_specs=in_specs,
        out_specs=pl.BlockSpec(memory_space=pl.ANY),
        scratch_shapes=[pltpu.VMEM((2, lb, D), BF16), pltpu.VMEM((te, D), BF16), pltpu.SemaphoreType.DMA((3,))],
    )
    return pl.pallas_call(
        functools.partial(_dispatch_body, n_chunks=n_chunks, tiles=tuple(tiles), n_blk=slots // te),
        grid_spec=grid_spec,
        out_shape=jax.ShapeDtypeStruct((slots, D), BF16),
        compiler_params=_cparams(("arbitrary",)),
        name="moe_dispatch",
    )(dst, tails, n_used, *operands)


def _combine_body(dst_ref, x_ref, ld_ref, gate_ref, g2_ref, lg_ref, lb_ref, ys_ref, o_ref, buf_s, sem, *, n_chunks):
    i = pl.program_id(0)
    n_t = pl.num_programs(0)
    slot = i % 2
    tm = x_ref.shape[0]
    lb = n_chunks * RUN_ALIGN

    def fetch(step, s):
        def body(k, _):
            _chunk_copy(buf_s.at[s], ys_ref, sem.at[s], k, dst_ref[step, k], False).start()
            return 0
        lax.fori_loop(0, n_chunks, body, 0, unroll=8)

    @pl.when(i == 0)
    def _():
        fetch(0, 0)

    @pl.when(i + 1 < n_t)
    def _():
        fetch(i + 1, 1 - slot)

    pltpu.make_async_copy(ys_ref.at[pl.ds(0, lb)], buf_s.at[slot], sem.at[slot]).wait()
    ld = ld_ref[...]
    gates = gate_ref[...]
    cb = 512
    y = jnp.zeros((tm, x_ref.shape[1]), F32)
    for j in range(lb // cb):
        c = lax.broadcasted_iota(I32, (tm, cb), 1) + j * cb
        w = jnp.zeros((tm, cb), F32)
        for k in range(TOP_K):
            w = jnp.where(c == ld[:, k:k + 1], gates[:, k:k + 1], w)
        y = y + jnp.dot(w.astype(BF16), buf_s[slot, j * cb:(j + 1) * cb, :], preferred_element_type=F32)
    v = ALPHA_DN * x_ref[...] + g2_ref[...] * y
    o_ref[...] = _layer_norm(v, lg_ref[...], lb_ref[...])


def _combine(x2, ldest_col, gates_col, dst, ys, pmods, layer, rows_per_mod, tm, ln_g, ln_b):
    N, D = x2.shape
    n_t, n_chunks = dst.shape
    lb = n_chunks * RUN_ALIGN
    vec = pl.BlockSpec((1, D), lambda i, d: (0, 0))
    grid_spec = pltpu.PrefetchScalarGridSpec(
        num_scalar_prefetch=1,
        grid=(n_t,),
        in_specs=[pl.BlockSpec((tm, D), lambda i, d: (i, 0)),
                  pl.BlockSpec((tm, TOP_K), lambda i, d: (i, 0)),
                  pl.BlockSpec((tm, TOP_K), lambda i, d: (i, 0)),
                  pl.BlockSpec((None, None, None, 1, D), lambda i, d: (layer, 5, (i * tm) // rows_per_mod, 0, 0)),
                  vec, vec,
                  pl.BlockSpec(memory_space=pl.ANY)],
        out_specs=pl.BlockSpec((tm, D), lambda i, d: (i, 0)),
        scratch_shapes=[pltpu.VMEM((2, lb, D), BF16), pltpu.SemaphoreType.DMA((2,))],
    )
    return pl.pallas_call(
        functools.partial(_combine_body, n_chunks=n_chunks),
        grid_spec=grid_spec,
        out_shape=jax.ShapeDtypeStruct((N, D), F32),
        compiler_params=_cparams(("arbitrary",)),
        name="moe_combine",
    )(dst, x2, ldest_col, gates_col, pmods, ln_g.reshape(1, D), ln_b.reshape(1, D), ys)


def _moe_plan(cnts, n_chunks, te, spare_chunk):
    a = RUN_ALIGN
    cnt = jnp.concatenate(cnts, axis=0)
    pc = (cnt + a - 1) // a * a
    seg = (jnp.sum(pc, axis=0) + te - 1) // te * te
    pad_end = jnp.cumsum(seg)
    run_start = (pad_end - seg)[None, :] + jnp.cumsum(pc, axis=0) - pc
    lo = jnp.cumsum(pc, axis=1) - pc
    tables, t0 = [], 0
    for c, nc in zip(cnts, n_chunks):
        sl = slice(t0, t0 + c.shape[0])
        parity = ((t0 + jnp.arange(c.shape[0], dtype=I32)) % 2)[:, None]
        t0 += c.shape[0]
        pos = jnp.arange(nc, dtype=I32) * a
        owner = jnp.sum(((lo[sl] + pc[sl])[:, None, :] <= pos[None, :, None]).astype(I32), axis=-1)
        mine = owner[..., None] == jnp.arange(N_EXP, dtype=I32)
        base = jnp.sum(jnp.where(mine, (run_start[sl] - lo[sl])[:, None, :], 0), axis=-1)
        valid = pos[None, :] < jnp.sum(pc[sl], axis=1, keepdims=True)
        spare = spare_chunk + parity * nc + jnp.arange(nc, dtype=I32)[None, :]
        tables.append(jnp.where(valid, (base + pos[None, :]) // a, spare).astype(I32))
    tail_row = (pad_end - seg + jnp.sum(pc, axis=0))[:, None] + jnp.arange(te // a, dtype=I32)[None, :] * a
    tails = jnp.where(tail_row < pad_end[:, None], tail_row // a, -1).astype(I32).reshape(-1)
    return tables, tails, pad_end


def _moe(passes, layer, P):
    te = EXPERT_TILE
    routed = []
    for x, pmods, shared_mod in passes:
        B, T, D = x.shape
        N = B * T
        tm = min(MOE_TILE, N if shared_mod else T)
        x2 = x.reshape(N, D)
        u2, ldest, gates, cnt = _router(x2, pmods, layer, T, tm, P['w_router'][layer], P['b_router'][layer])
        routed.append((x2, pmods, T, tm, u2, ldest, gates, cnt[:, 0, :], x.shape))
    n_chunks = [r[3] * TOP_K // RUN_ALIGN + N_EXP for r in routed]
    n_tiles = [r[7].shape[0] for r in routed]
    n_assign = sum(r[0].shape[0] for r in routed) * TOP_K
    seg_rows = -(-(n_assign + sum(n_tiles) * N_EXP * RUN_ALIGN + N_EXP * te) // te) * te
    slots = seg_rows + -(-(2 * max(n_chunks) * RUN_ALIGN) // te) * te
    dsts, tails, pad_end = _moe_plan([r[7] for r in routed], n_chunks, te, seg_rows // RUN_ALIGN)
    n_blk = slots // te
    blk_pos = jnp.arange(n_blk, dtype=I32) * te
    blk_e = jnp.minimum(jnp.sum((pad_end[None, :] <= blk_pos[:, None]).astype(I32), axis=1), N_EXP - 1)
    n_used = (pad_end[-1] // te).astype(I32).reshape(1)
    assert len(set(r[3] for r in routed)) == 1, "all passes must use the same MoE tile"
    xs = _dispatch([r[4] for r in routed], [r[5] for r in routed], jnp.concatenate(dsts, axis=0), tails, n_used,
                   slots, routed[0][3], te)
    ys = _experts(xs, blk_e, n_used, layer, P['w_gu'], P['b_gu'], P['w_down'], P['b_down'])
    outs = []
    for r, dst in zip(routed, dsts):
        x2, pmods, T, tm, _, ldest, gates, _, shape = r
        y = _combine(x2, ldest.T, gates.T, dst, ys, pmods, layer, T, tm, P['ln_g'][layer, 1], P['ln_b'][layer, 1])
        outs.append(y.reshape(shape))
    return outs


def _split3(x):
    hi = x.astype(BF16)
    r1 = x - hi.astype(F32)
    mid = r1.astype(BF16)
    lo = (r1 - mid.astype(F32)).astype(BF16)
    return jnp.concatenate([hi, mid, lo], axis=1)


def _inproj1_body(x_ref, sc_ref, sh_ref, w_ref, wg2_ref, wgh_ref, bg_ref,
                  qk_ref, v_ref, o_ref, gcol_ref, grow_ref):
    u = x_ref[...] * (1.0 + sc_ref[...]) + sh_ref[...]
    ub = u.astype(BF16)
    z = jnp.dot(ub, w_ref[...], preferred_element_type=F32)
    qk_ref[...] = z[:, :2 * D_MODEL]
    v_ref[...] = z[:, 2 * D_MODEL:3 * D_MODEL].astype(BF16)
    o_ref[...] = z[:, 3 * D_MODEL:]
    tm = u.shape[0]
    nh = 4 * H_C
    u_lo = (u - ub.astype(F32)).astype(BF16)
    g2 = jnp.dot(ub, wg2_ref[...], preferred_element_type=F32)
    gc = g2[:, :nh] + g2[:, nh:] + jnp.dot(u_lo, wgh_ref[...], preferred_element_type=F32) + bg_ref[...]
    r_i = lax.broadcasted_iota(I32, (tm, tm), 0)
    c_i = lax.broadcasted_iota(I32, (tm, tm), 1)
    same = (r_i // CHUNK) == (c_i // CHUNK)
    tri_f = jnp.where(jnp.logical_and(same, c_i <= r_i), 1.0, 0.0).astype(BF16)
    tri_b = jnp.where(jnp.logical_and(same, c_i >= r_i), 1.0, 0.0).astype(BF16)
    lf3 = _split3(jax.nn.log_sigmoid(gc))

    def sum3(p, axis):
        if axis == 1:
            return p[:, :nh] + p[:, nh:2 * nh] + p[:, 2 * nh:]
        return p[:nh] + p[nh:2 * nh] + p[2 * nh:]

    col_i = lax.broadcasted_iota(I32, (tm, nh), 1)
    cum = jnp.where(col_i < 2 * H_C,
                    sum3(jnp.dot(tri_f, lf3, preferred_element_type=F32), 1),
                    sum3(jnp.dot(tri_b, lf3, preferred_element_type=F32), 1))
    gcol = jnp.where((col_i // H_C) % 2 == 1, cum, gc)
    gcol_ref[...] = gcol
    eye = jnp.where(r_i == c_i, 1.0, 0.0).astype(BF16)
    gt3 = lax.dot_general(_split3(gcol), eye, (((0,), (0,)), ((), ())), preferred_element_type=F32)
    grow_ref[...] = sum3(gt3, 0)


def _inproj1(x, mods, w_in1, w_gate1, b_gate1, row_off, row_stride):
    B, T, D = x.shape
    tm = min(TOKEN_TILE, T)
    n_j = T // tm
    nh = 4 * H_C
    wg_hi = w_gate1.astype(BF16)
    wg_lo = (w_gate1 - wg_hi.astype(F32)).astype(BF16)
    tok = lambda w: pl.BlockSpec((None, tm, w), lambda b, j: (b, j, 0))
    return pl.pallas_call(
        _inproj1_body,
        grid=(B, n_j),
        in_specs=[tok(D),
                  _mod_spec(1, 1, row_off, row_stride),
                  _mod_spec(1, 0, row_off, row_stride),
                  pl.BlockSpec((D, 4 * D), lambda b, j: (0, 0)),
                  pl.BlockSpec((D, 2 * nh), lambda b, j: (0, 0)),
                  pl.BlockSpec((D, nh), lambda b, j: (0, 0)),
                  pl.BlockSpec((1, nh), lambda b, j: (0, 0))],
        out_specs=[tok(2 * D), tok(D), tok(D), tok(nh),
                   pl.BlockSpec((nh, tm), lambda b, j: (0, b * n_j + j))],
        out_shape=[jax.ShapeDtypeStruct((B, T, 2 * D), F32),
                   jax.ShapeDtypeStruct((B, T, D), BF16),
                   jax.ShapeDtypeStruct((B, T, D), F32),
                   jax.ShapeDtypeStruct((B, T, nh), F32),
                   jax.ShapeDtypeStruct((nh, B * T), F32)],
        compiler_params=_cparams(("arbitrary", "arbitrary")),
        name="inproj1",
    )(x, mods, mods, w_in1.astype(BF16), jnp.concatenate([wg_hi, wg_lo], axis=1), wg_hi, b_gate1.reshape(1, nh))


def _mlstm_body(q_ref, k_ref, v_ref, gcol_ref, grow_ref, cw_ref, cb_ref, c0_ref, n0_ref, m0_ref,
                h_ref, cfin_ref, nfin_ref, mfin_ref, xpad_s, q_s, k_s, *, T):
    n_c = T // CHUNK
    L = CHUNK

    rb = min(T, 512)
    pad = 8

    def conv_silu_into(src_ref, w, bias, dst_s, scale):
        xpad_s[0:pad] = jnp.zeros((pad, DH_C), F32)
        xpad_s[T + pad:T + 2 * pad] = jnp.zeros((pad, DH_C), F32)
        xpad_s[pad:T + pad] = src_ref[...]

        def blk(r, _):
            off = pl.multiple_of(r * rb, pad)
            win = xpad_s[pl.ds(off, rb + 2 * pad), :]
            acc = (pltpu.roll(win, 2, 0) * w[0:1] + pltpu.roll(win, 1, 0) * w[1:2] + win * w[2:3]
                   + pltpu.roll(win, rb + 2 * pad - 1, 0) * w[3:4])[pad:rb + pad] + bias
            dst_s[pl.ds(off, rb), :] = (jax.nn.silu(acc) * scale).astype(BF16)
            return 0

        lax.fori_loop(0, T // rb, blk, 0)

    conv_silu_into(q_ref, cw_ref[0], cb_ref[0], q_s, 1.0)
    conv_silu_into(k_ref, cw_ref[1], cb_ref[1], k_s, DH_C ** -0.5)

    t_i = lax.broadcasted_iota(I32, (L, L), 0)
    s_i = lax.broadcasted_iota(I32, (L, L), 1)

    cfin_ref[...] = c0_ref[...]
    nfin_ref[...] = n0_ref[...]
    mfin_ref[...] = m0_ref[...]

    def chunk(d, c):
        mask = (s_i <= t_i) if d == 0 else (s_i >= t_i)
        last = L - 1 if d == 0 else 0
        off = pl.multiple_of(c * L, L)
        qc = q_s[pl.ds(off, L), :]
        kc = k_s[pl.ds(off, L), :]
        vc = v_ref[pl.ds(off, L), :]
        gcol = gcol_ref[pl.ds(off, L), :]
        grow = grow_ref[:, pl.ds(off, L)]
        li_c = gcol[:, 2 * d:2 * d + 1]
        b_c = gcol[:, 2 * d + 1:2 * d + 2]
        li_r = grow[2 * d:2 * d + 1, :]
        b_r = grow[2 * d + 1:2 * d + 2, :]
        m_prev = mfin_ref[d]
        log_d = jnp.where(mask, b_c - b_r + li_r, -jnp.inf)
        m_inter = b_c + m_prev
        m_t = jnp.maximum(m_inter, jnp.max(log_d, axis=-1, keepdims=True))
        dmat = jnp.exp(log_d - m_t)
        w_inter = jnp.exp(m_inter - m_t)
        c_old = cfin_ref[d]
        s = lax.dot_general(qc, kc, (((1,), (1,)), ((), ())), preferred_element_type=F32) * dmat
        inter = lax.dot_general(qc, c_old.astype(BF16), (((1,), (1,)), ((), ())), preferred_element_type=F32)
        num = w_inter * inter + jnp.dot(s.astype(BF16), vc, preferred_element_type=F32)
        qn = jnp.sum(qc.astype(F32) * nfin_ref[d], axis=-1, keepdims=True)
        den = w_inter * qn + jnp.sum(s, axis=-1, keepdims=True)
        h = num / jnp.maximum(jnp.abs(den), jnp.exp(-m_t))
        m_new = m_t[last:last + 1, :]
        b_last = b_c[last:last + 1, :]
        w_s = jnp.exp(b_last - b_c + li_c - m_new)
        decay = jnp.exp(b_last + m_prev - m_new)
        wv = (w_s * vc.astype(F32)).astype(BF16)
        cfin_ref[d] = decay * c_old + lax.dot_general(wv, kc, (((0,), (0,)), ((), ())),
                                                      preferred_element_type=F32)
        nfin_ref[d] = decay * nfin_ref[d] + jnp.sum(w_s * kc.astype(F32), axis=0, keepdims=True)
        mfin_ref[d] = m_new
        return off, h

    def both(ci, _):
        off_f, h_f = chunk(0, ci)
        off_b, h_b = chunk(1, n_c - 1 - ci)
        h_ref[pl.ds(off_f, L), :] = h_f
        xpad_s[pl.ds(off_b, L), :] = h_b
        return 0

    lax.fori_loop(0, n_c, both, 0)

    def add_bwd(r, _):
        off = pl.multiple_of(r * rb, pad)
        h_ref[pl.ds(off, rb), :] = h_ref[pl.ds(off, rb), :] + xpad_s[pl.ds(off, rb), :]
        return 0

    lax.fori_loop(0, T // rb, add_bwd, 0)


def _mlstm(qk, v, gcol, grow, conv_w, conv_b, c0, n0, m0):
    B, T, _ = v.shape
    DH = DH_C
    cw = conv_w.reshape(CONV_W, 2, H_C, DH).transpose(2, 1, 0, 3)
    cb = conv_b.reshape(2, H_C, 1, DH).transpose(1, 0, 2, 3)
    st = lambda *tail: pl.BlockSpec((None, 2, None) + tail, lambda b, h: (b, 0, h) + (0,) * len(tail))
    return pl.pallas_call(
        functools.partial(_mlstm_body, T=T),
        grid=(B, H_C),
        in_specs=[pl.BlockSpec((None, T, DH), lambda b, h: (b, 0, h)),
                  pl.BlockSpec((None, T, DH), lambda b, h: (b, 0, H_C + h)),
                  pl.BlockSpec((None, T, DH), lambda b, h: (b, 0, h)),
                  pl.BlockSpec((None, None, T, 4), lambda b, h: (b, h, 0, 0)),
                  pl.BlockSpec((None, None, 4, T), lambda b, h: (b, h, 0, 0)),
                  pl.BlockSpec((None, 2, CONV_W, DH), lambda b, h: (h, 0, 0, 0)),
                  pl.BlockSpec((None, 2, 1, DH), lambda b, h: (h, 0, 0, 0)),
                  st(DH, DH), st(1, DH), st(1, 1)],
        out_specs=[pl.BlockSpec((None, T, DH), lambda b, h: (b, 0, h)),
                   st(DH, DH), st(1, DH), st(1, 1)],
        out_shape=[jax.ShapeDtypeStruct((B, T, D_MODEL), F32),
                   jax.ShapeDtypeStruct((B, 2, H_C, DH, DH), F32),
                   jax.ShapeDtypeStruct((B, 2, H_C, 1, DH), F32),
                   jax.ShapeDtypeStruct((B, 2, H_C, 1, 1), F32)],
        scratch_shapes=[pltpu.VMEM((T + 16, DH), F32), pltpu.VMEM((T, DH), BF16), pltpu.VMEM((T, DH), BF16)],
        compiler_params=_cparams(("arbitrary", "arbitrary")),
        name="mlstm",
    )(qk, qk, v, gcol, grow, cw, cb, c0, n0, m0)


def _outproj1_body(x_ref, h_ref, o_ref, g1_ref, w_ref, lg_ref, lb_ref, out_ref):
    y = jax.nn.sigmoid(o_ref[...]) * h_ref[...]
    out = jnp.dot(y.astype(BF16), w_ref[...], preferred_element_type=F32)
    v = ALPHA_DN * x_ref[...] + g1_ref[...] * out
    out_ref[...] = _layer_norm(v, lg_ref[...], lb_ref[...])


def _outproj1(x, h, o, mods, w_out1, ln_g, ln_b, row_off, row_stride):
    B, T, D = x.shape
    tm = min(TOKEN_TILE, T)
    tok = pl.BlockSpec((None, tm, D), lambda b, j: (b, j, 0))
    vec = pl.BlockSpec((1, D), lambda b, j: (0, 0))
    return pl.pallas_call(
        _outproj1_body,
        grid=(B, T // tm),
        in_specs=[tok, tok, tok, _mod_spec(1, 2, row_off, row_stride),
                  pl.BlockSpec((D, D), lambda b, j: (0, 0)), vec, vec],
        out_specs=tok,
        out_shape=jax.ShapeDtypeStruct((B, T, D), F32),
        compiler_params=_cparams(("arbitrary", "arbitrary")),
        name="outproj1",
    )(x, h, o, mods, w_out1.astype(BF16), ln_g.reshape(1, D), ln_b.reshape(1, D))


def _to_col_major(x):
    B, T, C = x.shape
    rows = T // GRID_W
    return x.reshape(B, rows, GRID_W, C).transpose(0, 2, 1, 3).reshape(B, T, C)


def _to_row_major(x):
    B, T, C = x.shape
    rows = T // GRID_W
    return x.reshape(B, GRID_W, rows, C).transpose(0, 2, 1, 3).reshape(B, T, C)


def _mixer0(x, mods, st, P, s5_mats):
    h_rg, s_re, s_im = st
    xa, ga, ub = _inproj0(x, mods, P['w_in0'][0].astype(BF16))
    h, h_fin = _rglru(xa, P['conv_a_w'][0], P['conv_a_b'][0], P['rg_wa'][0], P['rg_ba'][0],
                      P['rg_wi'][0], P['rg_bi'][0], P['rg_lam'][0], h_rg[:, 0].transpose(1, 0, 2))
    y5, s_fin = _s5(ub, s5_mats, _s5_state_to_lanes(s_re[:, 0], s_im[:, 0]))
    x = _outproj0(x, h, ga, y5, ub, mods, P['s5_d'][0], P['glu_w'][0], P['glu_b'][0], P['w_out0'][0],
                  P['ln_g'][0, 0], P['ln_b'][0, 0])
    new_re, new_im = _s5_lanes_to_state(s_fin)
    return x, (h_fin.transpose(1, 0, 2)[:, None], new_re[:, None], new_im[:, None])


def _mixer1(x, mods, st, P):
    m_c, m_n, m_m = st
    B, T, D = x.shape
    qk, v, o, gcol, grow = _inproj1(x, mods, P['w_in1'][0], P['w_gate1'][0], P['b_gate1'][0], 0, 1)
    gcol = gcol.reshape(B, T, 2, 2, H_C).transpose(0, 4, 1, 2, 3).reshape(B, H_C, T, 4)
    grow = grow.reshape(2, 2, H_C, B, T).transpose(3, 2, 0, 1, 4).reshape(B, H_C, 4, T)
    h, c_fin, n_fin, m_fin = _mlstm(qk, v, gcol, grow, P['conv_c_w'][0], P['conv_c_b'][0],
                                    m_c[:, 0], m_n[:, 0][:, :, :, None, :], m_m[:, 0][:, :, :, None, None])
    x = _outproj1(x, h, o, mods, P['w_out1'][0], P['ln_g'][1, 0], P['ln_b'][1, 0], 0, 1)
    return x, (c_fin[:, None], n_fin[:, None, :, :, 0, :], m_fin[:, None, :, :, 0, 0])


def _forward(x_prompt, x_sample, c, c_ctx, states, P):
    bp = x_prompt.shape[0]
    bs = x_sample.shape[0]
    rows = 1 + bs
    rpad = -(-rows // 8) * 8
    cv = jnp.concatenate([c_ctx[None, :], c, jnp.zeros((rpad - rows, D_MODEL), F32)], axis=0)
    mods = _modulation(cv, P['w_mod'], P['b_mod'])
    s5_mats = _s5_matrices(P['s5_a_re'][0], P['s5_a_im'][0], P['s5_log_dt'][0], P['s5_b_re'][0], P['s5_b_im'][0],
                           P['s5_c_re'][0], P['s5_c_im'][0])
    zero_state = (jnp.zeros((bp, 1, 2, D_A), F32),
                  jnp.zeros((bp, 1, 2, G_B, P_B), F32),
                  jnp.zeros((bp, 1, 2, G_B, P_B), F32),
                  jnp.zeros((bp, 1, 2, H_C, DH_C, DH_C), F32),
                  jnp.zeros((bp, 1, 2, H_C, DH_C), F32),
                  jnp.zeros((bp, 1, 2, H_C), F32))
    mods_ctx = jnp.broadcast_to(mods[:, :, 0:1], mods.shape[:2] + (bp,) + mods.shape[3:])
    mods_lat = mods[:, :, 1:1 + bs]
    xc, new_even = _mixer0(x_prompt, mods_ctx, zero_state[:3], P, s5_mats)
    xl, _ = _mixer0(x_sample, mods_lat, states[:3], P, s5_mats)
    xc, xl = _moe([(xc, mods_ctx, True), (xl, mods_lat, False)], 0, P)
    xl = _to_col_major(xl)
    xc, new_odd = _mixer1(xc, mods_ctx, zero_state[3:], P)
    xl, _ = _mixer1(xl, mods_lat, states[3:], P)
    xc, xl = _moe([(xc, mods_ctx, True), (xl, mods_lat, False)], 1, P)
    return (xc, _to_row_major(xl)) + tuple(new_even) + tuple(new_odd)


def kernel(x_prompt, x_sample, c, c_ctx, state_rglru, state_s5_re, state_s5_im, state_mlstm_C, state_mlstm_n, state_mlstm_m, w_mod, b_mod, ln_g, ln_b, w_in0, conv_a_w, conv_a_b, rg_wa, rg_ba, rg_wi, rg_bi, rg_lam, s5_a_re, s5_a_im, s5_log_dt, s5_b_re, s5_b_im, s5_c_re, s5_c_im, s5_d, glu_w, glu_b, w_out0, w_in1, w_gate1, b_gate1, conv_c_w, conv_c_b, w_out1, w_router, b_router, w_gu, b_gu, w_down, b_down):
    P = dict(w_mod=w_mod, b_mod=b_mod, ln_g=ln_g, ln_b=ln_b, w_in0=w_in0, conv_a_w=conv_a_w,
             conv_a_b=conv_a_b, rg_wa=rg_wa, rg_ba=rg_ba, rg_wi=rg_wi, rg_bi=rg_bi, rg_lam=rg_lam,
             s5_a_re=s5_a_re, s5_a_im=s5_a_im, s5_log_dt=s5_log_dt, s5_b_re=s5_b_re, s5_b_im=s5_b_im,
             s5_c_re=s5_c_re, s5_c_im=s5_c_im, s5_d=s5_d, glu_w=glu_w, glu_b=glu_b, w_out0=w_out0,
             w_in1=w_in1, w_gate1=w_gate1, b_gate1=b_gate1, conv_c_w=conv_c_w, conv_c_b=conv_c_b,
             w_out1=w_out1, w_router=w_router, b_router=b_router, w_gu=w_gu, b_gu=b_gu,
             w_down=w_down, b_down=b_down)
    states = (state_rglru, state_s5_re, state_s5_im, state_mlstm_C, state_mlstm_n, state_mlstm_m)
    return _forward(x_prompt, x_sample, c, c_ctx, states, P)
```

```python
import functools
import math

import jax
import jax.numpy as jnp
from jax import lax
from jax.experimental import pallas as pl
from jax.experimental.pallas import tpu as pltpu

F32 = jnp.float32
BF16 = jnp.bfloat16
I32 = jnp.int32
HI = lax.Precision.HIGHEST

D_MODEL = 1024
DEPTH = 2
GRID_W = 64
D_A = 512
NB_A = 8
BS_A = D_A // NB_A
CONV_W = 4
LRU_C = 8.0
D_B = 512
S5_GROUP = 16
G_B = D_B // S5_GROUP
P_B = 64
H_C = 4
DH_C = D_MODEL // H_C
CHUNK = 128
N_EXP = 32
TOP_K = 4
D_FF = D_MODEL
SWIGLU_LIMIT = 7.0
SWIGLU_ALPHA = 1.702
ALPHA_DN = (2 * DEPTH) ** 0.25
LN_EPS = 1e-5

S5_L = 16
S5_W = S5_L * S5_GROUP
S5_OCT = 8
TOKEN_TILE = 256
EXPERT_TILE = 512
MOE_TILE = 512
RUN_ALIGN = 16
VMEM_LIMIT = 56 * 1024 * 1024


def _cparams(sem, vmem=VMEM_LIMIT):
    return pltpu.CompilerParams(dimension_semantics=sem, vmem_limit_bytes=vmem)


def _layer_norm(v, g, b):
    mu = jnp.mean(v, axis=-1, keepdims=True)
    c = v - mu
    var = jnp.mean(c * c, axis=-1, keepdims=True)
    return c * lax.rsqrt(var + LN_EPS) * g + b


def _mod_spec(layer, which, row_off, row_stride):
    return pl.BlockSpec((None, None, None, 1, D_MODEL),
                        lambda b, j: (layer, which, row_off + b * row_stride, 0, 0))


def _mod_spec_all(layer, which, nb):
    return pl.BlockSpec((None, None, nb, 1, D_MODEL), lambda i: (layer, which, 0, 0, 0))


def _row_permutation(n_out_major, n_out_minor):
    n = n_out_major * n_out_minor
    r_out = lax.broadcasted_iota(I32, (n, n), 0)
    r_in = lax.broadcasted_iota(I32, (n, n), 1)
    hit = jnp.logical_and(r_out // n_out_minor == r_in % n_out_major, r_out % n_out_minor == r_in // n_out_major)
    return jnp.where(hit, 1.0, 0.0).astype(BF16)


def _mod_body(c_ref, w_ref, b_ref, o_ref):
    s = jax.nn.silu(c_ref[...])
    o_ref[...] = jnp.dot(s, w_ref[...], precision=HI, preferred_element_type=F32) + b_ref[...]


def _modulation(cv, w_mod, b_mod):
    R, D = cv.shape
    L, _, N6 = w_mod.shape
    tn = N6 // 6
    out = pl.pallas_call(
        _mod_body,
        grid=(L, N6 // tn),
        in_specs=[pl.BlockSpec((R, D), lambda l, j: (0, 0)),
                  pl.BlockSpec((None, D, tn), lambda l, j: (l, 0, j)),
                  pl.BlockSpec((None, 1, tn), lambda l, j: (l, 0, j))],
        out_specs=pl.BlockSpec((None, R, tn), lambda l, j: (l, 0, j)),
        out_shape=jax.ShapeDtypeStruct((L, R, N6), F32),
        compiler_params=_cparams(("arbitrary", "arbitrary")),
        name="modulation",
    )(cv, w_mod, b_mod.reshape(L, 1, N6))
    return out.reshape(L, R, 6, D).transpose(0, 2, 1, 3)[:, :, :, None, :]


def _inproj0_body(x_ref, sc_ref, sh_ref, w_ref, xa_ref, ga_ref, ub_ref):
    nb, tq, d = x_ref.shape
    u = x_ref[...] * (1.0 + sc_ref[...]) + sh_ref[...]
    ub = u.reshape(nb * tq, d).astype(BF16)
    ut = jnp.dot(_row_permutation(tq, nb), ub, preferred_element_type=F32).astype(BF16)
    z = jnp.dot(ut, w_ref[...], preferred_element_type=F32).reshape(tq, nb, w_ref.shape[1])
    xa_ref[...] = z[:, :, :D_A]
    ga_ref[...] = z[:, :, D_A:2 * D_A]
    ub_ref[...] = z[:, :, 2 * D_A:]


def _inproj0(x, pmods, w_in0):
    B, T, D = x.shape
    tq = TOKEN_TILE // B
    nz = w_in0.shape[1]
    tmaj = pl.BlockSpec((tq, B, D_A), lambda i: (i, 0, 0))
    return pl.pallas_call(
        _inproj0_body,
        grid=(T // tq,),
        in_specs=[pl.BlockSpec((B, tq, D), lambda i: (0, i, 0)),
                  _mod_spec_all(0, 1, B), _mod_spec_all(0, 0, B),
                  pl.BlockSpec((D, nz), lambda i: (0, 0))],
        out_specs=[tmaj, tmaj, tmaj],
        out_shape=[jax.ShapeDtypeStruct((T, B, D_A), F32)] * 3,
        compiler_params=_cparams(("arbitrary",)),
        name="inproj0",
    )(x, pmods, pmods, w_in0)


def _rglru_body(cur_ref, prev_ref, next_ref, cw_ref, cb_ref, wa_ref, wi_ref, ba_ref, bi_ref, lam_ref, h0_ref,
                h_ref, hfin_ref, ext_s, a_s, b_s, carry_s, *, tt, n_t):
    d = pl.program_id(0)
    j = pl.program_id(1)
    jj = j + d * (n_t - 1 - 2 * j)
    nb = cur_ref.shape[1]
    half = D_A // 2
    ext_s[0:2] = jnp.where(jj == 0, 0.0, prev_ref[...])
    ext_s[2:tt + 2] = cur_ref[...]
    ext_s[tt + 2:tt + 3] = jnp.where(jj == n_t - 1, 0.0, next_ref[...])
    xc = ext_s[0:tt] * cw_ref[0]
    for k in range(1, CONV_W):
        xc = xc + ext_s[k:k + tt] * cw_ref[k]
    xc = xc + cb_ref[...]
    x2 = xc.reshape(tt * nb, D_A)
    xb = x2.astype(BF16)

    def gate(w_ref, bias_ref):
        lo = jnp.dot(xb[:, :half], w_ref[0], preferred_element_type=F32)
        hi = jnp.dot(xb[:, half:], w_ref[1], preferred_element_type=F32)
        return jax.nn.sigmoid(jnp.concatenate([lo, hi], axis=1) + bias_ref[...])

    r = gate(wa_ref, ba_ref)
    i = gate(wi_ref, bi_ref)
    log_a = LRU_C * r * jax.nn.log_sigmoid(lam_ref[...])
    a = jnp.exp(log_a)
    one_minus_a2 = -jnp.tanh(log_a) * (a * a + 1.0)
    a_s[...] = a.reshape(tt, nb, D_A)
    b_s[...] = (jnp.sqrt(one_minus_a2) * (i * x2)).reshape(tt, nb, D_A)

    @pl.when(j == 0)
    def _():
        carry_s[...] = h0_ref[...]

    def step(t, h):
        h = a_s[t] * h + b_s[t]
        h_ref[t] = h
        return h

    @pl.when(d == 0)
    def _():
        carry_s[...] = lax.fori_loop(0, tt, step, carry_s[...], unroll=8)

    @pl.when(d == 1)
    def _():
        carry_s[...] = lax.fori_loop(0, tt, lambda t, h: step(tt - 1 - t, h), carry_s[...], unroll=8)

    hfin_ref[...] = carry_s[...]


def _block_diag_halves(w):
    nd = w.shape[0]
    per = NB_A // 2
    w = w.reshape(nd, 2, per, BS_A, BS_A)
    eye = jnp.eye(per, dtype=w.dtype)
    out = jnp.einsum('dhncz,nm->dhncmz', w, eye)
    return out.reshape(nd, 2, per * BS_A, per * BS_A)


def _rglru(x3, conv_w, conv_b, wa, ba, wi, bi, lam, h0):
    T, B, _ = x3.shape
    tt = min(T, 1024 // B)
    n_t = T // tt

    def cur_map(d, j):
        return (j + d * (n_t - 1 - 2 * j), 0, 0)

    def prev_map(d, j):
        jj = j + d * (n_t - 1 - 2 * j)
        return (jnp.maximum(jj * (tt // 2) - 1, 0), 0, 0)

    def next_map(d, j):
        jj = j + d * (n_t - 1 - 2 * j)
        return (jnp.minimum((jj + 1) * tt, T - 1), 0, 0)

    dir_spec3 = pl.BlockSpec((None, 1, D_A), lambda d, j: (d, 0, 0))
    h, hfin = pl.pallas_call(
        functools.partial(_rglru_body, tt=tt, n_t=n_t),
        grid=(2, n_t),
        in_specs=[pl.BlockSpec((tt, B, D_A), cur_map),
                  pl.BlockSpec((2, B, D_A), prev_map),
                  pl.BlockSpec((1, B, D_A), next_map),
                  pl.BlockSpec((CONV_W, 1, D_A), lambda d, j: (0, 0, 0)),
                  pl.BlockSpec((1, 1, D_A), lambda d, j: (0, 0, 0)),
                  pl.BlockSpec((None, 2, D_A // 2, D_A // 2), lambda d, j: (d, 0, 0, 0)),
                  pl.BlockSpec((None, 2, D_A // 2, D_A // 2), lambda d, j: (d, 0, 0, 0)),
                  dir_spec3, dir_spec3, dir_spec3,
                  pl.BlockSpec((None, B, D_A), lambda d, j: (d, 0, 0))],
        out_specs=[pl.BlockSpec((None, tt, B, D_A), lambda d, j: (d,) + cur_map(d, j)),
                   pl.BlockSpec((None, B, D_A), lambda d, j: (d, 0, 0))],
        out_shape=[jax.ShapeDtypeStruct((2, T, B, D_A), F32),
                   jax.ShapeDtypeStruct((2, B, D_A), F32)],
        scratch_shapes=[pltpu.VMEM((tt + 3, B, D_A), F32),
                        pltpu.VMEM((tt, B, D_A), F32),
                        pltpu.VMEM((tt, B, D_A), F32),
                        pltpu.VMEM((B, D_A), F32)],
        compiler_params=_cparams(("arbitrary", "arbitrary")),
        name="rglru",
    )(x3, x3, x3, conv_w.reshape(CONV_W, 1, D_A), conv_b.reshape(1, 1, D_A),
      _block_diag_halves(wa).astype(BF16), _block_diag_halves(wi).astype(BF16),
      ba.reshape(2, 1, D_A), bi.reshape(2, 1, D_A), lam.reshape(2, 1, D_A), h0)
    return h, hfin


def _s5_matrices(a_re, a_im, log_dt, b_re, b_im, c_re, c_im):
    L = S5_L
    lam = lax.complex(a_re.astype(F32), a_im.astype(F32))
    dt = jnp.exp(log_dt.astype(F32))[..., None]
    ldt = lam * dt
    a_bar = jnp.exp(ldt)
    b_bar = ((a_bar - 1.0) / lam)[..., None] * lax.complex(b_re.astype(F32), b_im.astype(F32))
    cc = lax.complex(c_re.astype(F32), c_im.astype(F32))
    ks = jnp.arange(L + 1, dtype=F32)
    pw = jnp.exp(ldt[:, :, None, :] * ks[None, None, :, None].astype(jnp.complex64))
    kern = jnp.real(jnp.einsum('dgjp,dgkp,dgpi->dgkji', cc, pw[:, :, :L], b_bar, precision=HI))
    s_idx = jnp.arange(L)[:, None]
    t_idx = jnp.arange(L)[None, :]
    lag_f = jnp.clip(t_idx - s_idx, 0, L - 1)
    lag_b = jnp.clip(s_idx - t_idx, 0, L - 1)
    m_f = jnp.where((t_idx >= s_idx)[None, :, :, None, None], kern[0][:, lag_f], 0.0)
    m_b = jnp.where((s_idx >= t_idx)[None, :, :, None, None], kern[1][:, lag_b], 0.0)
    m = (m_f + m_b).transpose(0, 1, 4, 2, 3).reshape(G_B, S5_W, S5_W)
    down = (L - ks[:L])[None, :, None].astype(jnp.complex64)
    pw_down_b = jnp.exp(ldt[1][:, None, :] * down)
    g_f = jnp.exp(ldt[0][:, None, :] * (down - 1.0))[..., None] * b_bar[0][:, None]
    g_b = pw[1][:, :L, :, None] * b_bar[1][:, None]

    def g_cols(x):
        return x.transpose(0, 1, 3, 2).reshape(G_B, S5_W, P_B)

    gs = jnp.concatenate([g_cols(jnp.real(g_f)), g_cols(jnp.real(g_b)),
                          g_cols(jnp.imag(g_f)), g_cols(jnp.imag(g_b))], axis=-1)
    e_f = cc[0][:, None] * pw[0][:, 1:, None, :]
    e_b = cc[1][:, None] * pw_down_b[:, :, None, :]

    def e_rows(x):
        return x.transpose(0, 3, 1, 2).reshape(G_B, P_B, S5_W)

    e = jnp.concatenate([e_rows(jnp.real(e_f)), e_rows(jnp.real(e_b)),
                         -e_rows(jnp.imag(e_f)), -e_rows(jnp.imag(e_b))], axis=1)
    a_l = pw[:, :, L]
    al = jnp.concatenate([jnp.real(a_l[0]), jnp.real(a_l[1]), jnp.imag(a_l[0]), jnp.imag(a_l[1])], axis=-1)
    no = G_B // S5_OCT
    ow = S5_OCT * S5_W
    rows_sgi = lambda x: x.reshape(no, S5_OCT, L, S5_GROUP, S5_W).transpose(0, 2, 1, 3, 4).reshape(no, ow, S5_W)
    src = jnp.stack([rows_sgi(m), rows_sgi(gs), e.reshape(no, ow, S5_W)]).astype(BF16)
    r = jnp.arange(ow, dtype=I32)
    c = jnp.arange(S5_W, dtype=I32)
    grp_sgi = (r // S5_GROUP) % S5_OCT
    grp_gl = r // S5_W
    src_sgi = (r // (S5_OCT * S5_GROUP)) * S5_GROUP + r % S5_GROUP
    src_gl = r % S5_W
    spread = jnp.stack([src_sgi, src_gl, src_sgi])[:, None, :] == c[None, :, None]
    row_grp = jnp.stack([grp_sgi, grp_sgi, grp_gl])[:, :, None]
    col_grp = jnp.stack([grp_sgi, grp_gl, grp_sgi])[:, None, :]
    tr = 512
    out = pl.pallas_call(
        _s5_expand_body,
        grid=(3, no, ow // tr),
        in_specs=[pl.BlockSpec((None, None, tr, S5_W), lambda k, q, i: (k, q, i, 0)),
                  pl.BlockSpec((None, S5_W, ow), lambda k, q, i: (k, 0, 0)),
                  pl.BlockSpec((None, tr, 1), lambda k, q, i: (k, i, 0)),
                  pl.BlockSpec((None, 1, ow), lambda k, q, i: (k, 0, 0))],
        out_specs=pl.BlockSpec((None, None, tr, ow), lambda k, q, i: (k, q, i, 0)),
        out_shape=jax.ShapeDtypeStruct((3, no, ow, ow), BF16),
        compiler_params=_cparams(("arbitrary", "arbitrary", "arbitrary")),
        name="s5_expand",
    )(src, spread.astype(BF16), row_grp, col_grp)
    return out, al.reshape(1, G_B * S5_W)


def _s5_expand_body(src_ref, spread_ref, rg_ref, cg_ref, o_ref):
    wide = jnp.dot(src_ref[...], spread_ref[...], preferred_element_type=F32)
    o_ref[...] = jnp.where(rg_ref[...] == cg_ref[...], wide, 0.0).astype(BF16)


def _s5_fill_lhs(u_ref, lhs_s):
    tc, _, nb, lanes = u_ref.shape
    for s in range(S5_L):
        lhs_s[:, s * lanes:(s + 1) * lanes] = u_ref[:, s].reshape(tc * nb, lanes).astype(BF16)


def _s5_state_body(u_ref, g_ref, f_ref, lhs_s):
    _s5_fill_lhs(u_ref, lhs_s)
    f_ref[...] = jnp.dot(lhs_s[...], g_ref[...], preferred_element_type=F32).reshape(f_ref.shape)


def _s5_scan_body(f_ref, a_ref, s0_ref, sin_ref, sfin_ref, *, n_c):
    nb, width = s0_ref.shape
    hw = 2 * P_B
    n_g = width // S5_W
    a = a_ref[...]
    a_re = [a[:, k * S5_W:k * S5_W + hw] for k in range(n_g)]
    a_im = [a[:, k * S5_W + hw:(k + 1) * S5_W] for k in range(n_g)]
    is_fwd = (lax.broadcasted_iota(I32, (nb, width), 1) % hw) < P_B

    def split(x):
        return tuple(x[:, k * hw:(k + 1) * hw] for k in range(2 * n_g))

    def merge(parts):
        return jnp.concatenate(parts, axis=-1)

    def advance(c, parts):
        f = split(f_ref[c])
        out = []
        for k in range(n_g):
            s_re, s_im = parts[2 * k], parts[2 * k + 1]
            out.append(a_re[k] * s_re - a_im[k] * s_im + f[2 * k])
            out.append(a_re[k] * s_im + a_im[k] * s_re + f[2 * k + 1])
        return tuple(out)

    def fwd(c, parts):
        sin_ref[c] = merge(parts)
        return advance(c, parts)

    init = split(s0_ref[...])
    fin_f = lax.fori_loop(0, n_c, fwd, init)

    def bwd(k, parts):
        c = n_c - 1 - k
        sin_ref[c] = jnp.where(is_fwd, sin_ref[c], merge(parts))
        return advance(c, parts)

    fin_b = lax.fori_loop(0, n_c, bwd, init)
    sfin_ref[...] = jnp.where(is_fwd, merge(fin_f), merge(fin_b))


def _s5_out_body(u_ref, sin_ref, m_ref, e_ref, y_ref, lhs_s):
    _s5_fill_lhs(u_ref, lhs_s)
    tc, _, nb, lanes = u_ref.shape
    sin = sin_ref[...].reshape(tc * nb, sin_ref.shape[-1]).astype(BF16)
    y = (jnp.dot(lhs_s[...], m_ref[...], preferred_element_type=F32)
         + jnp.dot(sin, e_ref[...], preferred_element_type=F32))
    for s in range(S5_L):
        y_ref[:, s] = y[:, s * lanes:(s + 1) * lanes].reshape(tc, nb, lanes)


def _s5(ub, mats, s0):
    mge, al = mats
    T, B, _ = ub.shape
    n_c = T // S5_L
    no = G_B // S5_OCT
    lanes = S5_OCT * S5_GROUP
    ow = S5_OCT * S5_W
    tc = min(n_c, TOKEN_TILE // B)
    u4 = ub.reshape(n_c, S5_L, B, D_B)
    u_spec = pl.BlockSpec((tc, S5_L, B, lanes), lambda q, i: (i, 0, 0, q))
    w_spec = lambda kind: pl.BlockSpec((None, None, ow, ow), lambda q, i: (kind, q, 0, 0))
    st_spec = pl.BlockSpec((tc, B, ow), lambda q, i: (i, 0, q))
    f_loc = pl.pallas_call(
        _s5_state_body,
        grid=(no, n_c // tc),
        in_specs=[u_spec, w_spec(1)],
        out_specs=st_spec,
        out_shape=jax.ShapeDtypeStruct((n_c, B, G_B * S5_W), F32),
        scratch_shapes=[pltpu.VMEM((tc * B, S5_L * lanes), BF16)],
        compiler_params=_cparams(("arbitrary", "arbitrary")),
        name="s5_state",
    )(u4, mge)
    sw = 4 * S5_W
    sin, sfin = pl.pallas_call(
        functools.partial(_s5_scan_body, n_c=n_c),
        grid=(G_B * S5_W // sw,),
        in_specs=[pl.BlockSpec((n_c, B, sw), lambda g: (0, 0, g)),
                  pl.BlockSpec((1, sw), lambda g: (0, g)),
                  pl.BlockSpec((B, sw), lambda g: (0, g))],
        out_specs=[pl.BlockSpec((n_c, B, sw), lambda g: (0, 0, g)),
                   pl.BlockSpec((B, sw), lambda g: (0, g))],
        out_shape=[jax.ShapeDtypeStruct((n_c, B, G_B * S5_W), F32),
                   jax.ShapeDtypeStruct((B, G_B * S5_W), F32)],
        compiler_params=_cparams(("arbitrary",)),
        name="s5_scan",
    )(f_loc, al, s0)
    y = pl.pallas_call(
        _s5_out_body,
        grid=(no, n_c // tc),
        in_specs=[u_spec, st_spec, w_spec(0), w_spec(2)],
        out_specs=u_spec,
        out_shape=jax.ShapeDtypeStruct((n_c, S5_L, B, D_B), F32),
        scratch_shapes=[pltpu.VMEM((tc * B, S5_L * lanes), BF16)],
        compiler_params=_cparams(("arbitrary", "arbitrary")),
        name="s5_out",
    )(u4, sin, mge, mge)
    return y.reshape(T, B, D_B), sfin


def _s5_state_to_lanes(s_re, s_im):
    parts = [s_re[:, 0], s_re[:, 1], s_im[:, 0], s_im[:, 1]]
    return jnp.concatenate(parts, axis=-1).reshape(s_re.shape[0], G_B * S5_W)


def _s5_lanes_to_state(s):
    s = s.reshape(s.shape[0], G_B, 4, P_B)
    return jnp.stack([s[:, :, 0], s[:, :, 1]], axis=1), jnp.stack([s[:, :, 2], s[:, :, 3]], axis=1)


def _outproj0_body(x_ref, h_ref, ga_ref, y_ref, ub_ref, g1_ref, d_ref, gw_ref, gb_ref, wo_ref,
                   lg_ref, lb_ref, o_ref):
    nb, tq, d = x_ref.shape
    rows = tq * nb

    def flat(v):
        return v.reshape(rows, v.shape[-1])

    ya = flat(h_ref[0] + h_ref[1]) * jax.nn.gelu(flat(ga_ref[...]))
    yb = flat(y_ref[...]) + d_ref[...] * flat(ub_ref[...])
    g = jax.nn.gelu(yb)
    gate = jax.nn.sigmoid(jnp.dot(g.astype(BF16), gw_ref[...], preferred_element_type=F32) + gb_ref[...])
    cat = jnp.concatenate([ya, g * gate], axis=1).astype(BF16)
    cat = jnp.dot(_row_permutation(nb, tq), cat, preferred_element_type=F32).astype(BF16)
    out = jnp.dot(cat, wo_ref[...], preferred_element_type=F32).reshape(nb, tq, d)
    v = ALPHA_DN * x_ref[...] + g1_ref[...] * out
    o_ref[...] = _layer_norm(v, lg_ref[...], lb_ref[...])


def _outproj0(x, h, ga, y5, ub, pmods, s5_d, glu_w, glu_b, w_out0, ln_g, ln_b):
    B, T, D = x.shape
    tq = TOKEN_TILE // B
    tmaj = pl.BlockSpec((tq, B, D_A), lambda i: (i, 0, 0))
    vec = lambda w: pl.BlockSpec((1, w), lambda i: (0, 0))
    return pl.pallas_call(
        _outproj0_body,
        grid=(T // tq,),
        in_specs=[pl.BlockSpec((B, tq, D), lambda i: (0, i, 0)),
                  pl.BlockSpec((2, tq, B, D_A), lambda i: (0, i, 0, 0)),
                  tmaj, tmaj, tmaj,
                  _mod_spec_all(0, 2, B),
                  vec(D_B),
                  pl.BlockSpec((D_B, D_B), lambda i: (0, 0)),
                  vec(D_B),
                  pl.BlockSpec((D, D), lambda i: (0, 0)),
                  vec(D), vec(D)],
        out_specs=pl.BlockSpec((B, tq, D), lambda i: (0, i, 0)),
        out_shape=jax.ShapeDtypeStruct((B, T, D), F32),
        compiler_params=_cparams(("arbitrary",)),
        name="outproj0",
    )(x, h, ga, y5, ub, pmods, s5_d.reshape(1, D_B), glu_w.astype(BF16), glu_b.reshape(1, D_B),
      w_out0.astype(BF16), ln_g.reshape(1, D), ln_b.reshape(1, D))


def _router_body(x_ref, sc_ref, sh_ref, wr_ref, br_ref, u_ref, ldest_ref, gate_ref, cnt_ref):
    u = x_ref[...] * (1.0 + sc_ref[...]) + sh_ref[...]
    u_ref[...] = u.astype(BF16)
    tm = u.shape[0]
    logits = lax.dot_general(wr_ref[...], u, (((1,), (1,)), ((), ())), precision=HI,
                             preferred_element_type=F32) + br_ref[...]
    e_iota = lax.broadcasted_iota(I32, logits.shape, 0)
    work = logits
    vals, hots = [], []
    for _ in range(TOP_K):
        m = jnp.max(work, axis=0, keepdims=True)
        idx = jnp.min(jnp.where(work == m, e_iota, N_EXP), axis=0, keepdims=True)
        hot = e_iota == idx
        vals.append(m)
        hots.append(hot)
        work = jnp.where(hot, -jnp.inf, work)
    ex = [jnp.exp(v - vals[0]) for v in vals]
    den = ex[0] + ex[1] + ex[2] + ex[3]
    gate_ref[...] = jnp.concatenate([e / den for e in ex], axis=0)
    hot_sum = jnp.zeros(logits.shape, F32)
    for hot in hots:
        hot_sum = hot_sum + hot.astype(F32)
    hot_b = hot_sum.astype(BF16)
    before = lax.broadcasted_iota(I32, (tm, tm), 0) < lax.broadcasted_iota(I32, (tm, tm), 1)
    excl = jnp.dot(hot_b, jnp.where(before, 1.0, 0.0).astype(BF16), preferred_element_type=F32)
    cnt_row = lax.dot_general(jnp.ones((8, tm), BF16), hot_b, (((1,), (1,)), ((), ())),
                              preferred_element_type=F32)[0:1]
    cnt_ref[...] = cnt_row.astype(I32)
    run_len = jnp.ceil(cnt_row * (1.0 / RUN_ALIGN)) * RUN_ALIGN
    lower = lax.broadcasted_iota(I32, (N_EXP, N_EXP), 1) < lax.broadcasted_iota(I32, (N_EXP, N_EXP), 0)
    run_off = jnp.sum(jnp.where(lower, run_len, 0.0), axis=1, keepdims=True)
    base = excl + run_off
    rows = [jnp.sum(jnp.where(hot, base, 0.0), axis=0, keepdims=True) for hot in hots]
    ldest_ref[...] = jnp.concatenate(rows, axis=0).astype(I32)


def _router(x2, pmods, layer, rows_per_mod, tm, w_router, b_router):
    N, D = x2.shape
    n_t = N // tm
    mod = lambda which: pl.BlockSpec((None, None, None, 1, D),
                                     lambda i: (layer, which, (i * tm) // rows_per_mod, 0, 0))
    lane_spec = pl.BlockSpec((TOP_K, tm), lambda i: (0, i))
    return pl.pallas_call(
        _router_body,
        grid=(n_t,),
        in_specs=[pl.BlockSpec((tm, D), lambda i: (i, 0)),
                  mod(4), mod(3),
                  pl.BlockSpec((N_EXP, D), lambda i: (0, 0)),
                  pl.BlockSpec((N_EXP, 1), lambda i: (0, 0))],
        out_specs=[pl.BlockSpec((tm, D), lambda i: (i, 0)),
                   lane_spec, lane_spec,
                   pl.BlockSpec((None, 1, N_EXP), lambda i: (i, 0, 0))],
        out_shape=[jax.ShapeDtypeStruct((N, D), BF16),
                   jax.ShapeDtypeStruct((TOP_K, N), I32),
                   jax.ShapeDtypeStruct((TOP_K, N), F32),
                   jax.ShapeDtypeStruct((n_t, 1, N_EXP), I32)],
        compiler_params=_cparams(("arbitrary",)),
        name="router",
    )(x2, pmods, pmods, w_router.T, b_router.reshape(N_EXP, 1))


def _expert_body(blk_e_ref, n_used_ref, xs_ref, wgu_ref, bgu_ref, wdn_ref, bdn_ref, o_ref, wgu_s, wdn_s):
    i = pl.program_id(0)
    prev = blk_e_ref[jnp.maximum(i - 1, 0)]
    changed = jnp.logical_or(i == 0, blk_e_ref[i] != prev)

    @pl.when(changed)
    def _():
        wgu_s[...] = wgu_ref[...].astype(BF16)
        wdn_s[...] = wdn_ref[...].astype(BF16)

    @pl.when(i < n_used_ref[0])
    def _():
        h = jnp.dot(xs_ref[...], wgu_s[...], preferred_element_type=F32) + bgu_ref[...]
        gt = jnp.minimum(h[:, :D_FF], SWIGLU_LIMIT)
        up = jnp.clip(h[:, D_FF:], -SWIGLU_LIMIT, SWIGLU_LIMIT)
        act = (up + 1.0) * gt * jax.nn.sigmoid(SWIGLU_ALPHA * gt)
        y = jnp.dot(act.astype(BF16), wdn_s[...], preferred_element_type=F32) + bdn_ref[...]
        o_ref[...] = y.astype(o_ref.dtype)

    @pl.when(i >= n_used_ref[0])
    def _():
        o_ref[...] = jnp.zeros_like(o_ref)


def _experts(xs, blk_e, n_used, layer, w_gu, b_gu, w_dn, b_dn):
    slots, D = xs.shape
    te = EXPERT_TILE
    n_blk = slots // te
    grid_spec = pltpu.PrefetchScalarGridSpec(
        num_scalar_prefetch=2,
        grid=(n_blk,),
        in_specs=[pl.BlockSpec((te, D), lambda i, be, nu: (jnp.minimum(i, nu[0] - 1), 0)),
                  pl.BlockSpec((None, None, D, 2 * D_FF), lambda i, be, nu: (layer, be[i], 0, 0)),
                  pl.BlockSpec((None, None, 1, 2 * D_FF), lambda i, be, nu: (layer, be[i], 0, 0)),
                  pl.BlockSpec((None, None, D_FF, D), lambda i, be, nu: (layer, be[i], 0, 0)),
                  pl.BlockSpec((None, None, 1, D), lambda i, be, nu: (layer, be[i], 0, 0))],
        out_specs=pl.BlockSpec((te, D), lambda i, be, nu: (i, 0)),
        scratch_shapes=[pltpu.VMEM((D, 2 * D_FF), BF16), pltpu.VMEM((D_FF, D), BF16)],
    )
    return pl.pallas_call(
        _expert_body,
        grid_spec=grid_spec,
        out_shape=jax.ShapeDtypeStruct((slots, D), BF16),
        compiler_params=_cparams(("arbitrary",)),
        name="experts",
    )(blk_e, n_used, xs, w_gu, b_gu.reshape(DEPTH, N_EXP, 1, 2 * D_FF), w_dn, b_dn.reshape(DEPTH, N_EXP, 1, D))


def _chunk_copy(vmem_ref, hbm_ref, sem, k, dst_chunk, to_hbm):
    row = k * RUN_ALIGN if isinstance(k, int) else pl.multiple_of(k * RUN_ALIGN, RUN_ALIGN)
    local = vmem_ref.at[pl.ds(row, RUN_ALIGN)]
    remote = hbm_ref.at[pl.ds(pl.multiple_of(dst_chunk * RUN_ALIGN, RUN_ALIGN), RUN_ALIGN)]
    return pltpu.make_async_copy(local, remote, sem) if to_hbm else pltpu.make_async_copy(remote, local, sem)


def _dispatch_body(dst_ref, tail_ref, nu_ref, *rest, n_chunks, tiles, n_blk):
    n_p = len(tiles)
    xs_ref, buf_s, zero_s, sem = rest[2 * n_p:]
    i = pl.program_id(0)
    n_t = pl.num_programs(0)
    slot = i % 2
    lb = n_chunks * RUN_ALIGN
    te = zero_s.shape[0]
    rb = 256

    def sort_rows(u_ref, ld_ref):
        tm = u_ref.shape[0]
        ld = ld_ref[...]
        u = u_ref[...]
        for j in range(lb // rb):
            r = lax.broadcasted_iota(I32, (rb, tm), 0) + j * rb
            p = jnp.where(r == ld[0:1], 1.0, jnp.where(r == ld[1:2], 1.0, jnp.where(
                r == ld[2:3], 1.0, jnp.where(r == ld[3:4], 1.0, 0.0))))
            buf_s[slot, j * rb:(j + 1) * rb, :] = jnp.dot(p.astype(BF16), u,
                                                          preferred_element_type=F32).astype(BF16)

    for p, (first, count) in enumerate(tiles):
        @pl.when(jnp.logical_and(i >= first, i < first + count))
        def _(p=p):
            sort_rows(rest[2 * p], rest[2 * p + 1])

    def start(k, _):
        _chunk_copy(buf_s.at[slot], xs_ref, sem.at[slot], k, dst_ref[i, k], True).start()
        return 0

    lax.fori_loop(0, n_chunks, start, 0, unroll=8)

    def drain(s):
        pltpu.make_async_copy(buf_s.at[s], xs_ref.at[pl.ds(0, lb)], sem.at[s]).wait()

    @pl.when(i > 0)
    def _():
        drain(1 - slot)

    @pl.when(i == n_t - 1)
    def _():
        drain(slot)
        zero_s[...] = jnp.zeros_like(zero_s)
        n_tail = tail_ref.shape[0]

        def tail_copy(t):
            return _chunk_copy(zero_s, xs_ref, sem.at[2], 0, tail_ref[t], True)

        def block_copy(k):
            return pltpu.make_async_copy(zero_s, xs_ref.at[pl.ds(pl.multiple_of(k * te, te), te)], sem.at[2])

        def each(n, pred, copy, wait):
            def body(t, _):
                @pl.when(pred(t))
                def _():
                    copy(t).wait() if wait else copy(t).start()
                return 0
            lax.fori_loop(0, n, body, 0)

        for wait in (False, True):
            each(n_tail, lambda t: tail_ref[t] >= 0, tail_copy, wait)
            each(n_blk, lambda k: k >= nu_ref[0], block_copy, wait)


def _dispatch(u2s, ldests, dst, tails, n_used, slots, tm, te):
    D = u2s[0].shape[1]
    n_t, n_chunks = dst.shape
    lb = n_chunks * RUN_ALIGN
    tiles, first = [], 0
    for u2 in u2s:
        tiles.append((first, u2.shape[0] // tm))
        first += u2.shape[0] // tm
    assert first == n_t

    def own_tile(first, count):
        return lambda i, d, t, nu: jnp.clip(i - first, 0, count - 1)

    in_specs, operands = [], []
    for (first, count), u2, ld in zip(tiles, u2s, ldests):
        tile = own_tile(first, count)
        in_specs += [pl.BlockSpec((tm, D), lambda i, d, t, nu, tile=tile: (tile(i, d, t, nu), 0)),
                     pl.BlockSpec((TOP_K, tm), lambda i, d, t, nu, tile=tile: (0, tile(i, d, t, nu)))]
        operands += [u2, ld]
    grid_spec = pltpu.PrefetchScalarGridSpec(
        num_scalar_prefetch=3,
        grid=(n_t,),
        in_specs=in_specs,
        out_specs=pl.BlockSpec(memory_space=pl.ANY),
        scratch_shapes=[pltpu.VMEM((2, lb, D), BF16), pltpu.VMEM((te, D), BF16), pltpu.SemaphoreType.DMA((3,))],
    )
    return pl.pallas_call(
        functools.partial(_dispatch_body, n_chunks=n_chunks, tiles=tuple(tiles), n_blk=slots // te),
        grid_spec=grid_spec,
        out_shape=jax.ShapeDtypeStruct((slots, D), BF16),
        compiler_params=_cparams(("arbitrary",)),
        name="moe_dispatch",
    )(dst, tails, n_used, *operands)


def _combine_body(dst_ref, x_ref, ld_ref, gate_ref, g2_ref, lg_ref, lb_ref, ys_ref, o_ref, buf_s, sem, *, n_chunks):
    i = pl.program_id(0)
    n_t = pl.num_programs(0)
    slot = i % 2
    tm = x_ref.shape[0]
    lb = n_chunks * RUN_ALIGN

    def fetch(step, s):
        def body(k, _):
            _chunk_copy(buf_s.at[s], ys_ref, sem.at[s], k, dst_ref[step, k], False).start()
            return 0
        lax.fori_loop(0, n_chunks, body, 0, unroll=8)

    @pl.when(i == 0)
    def _():
        fetch(0, 0)

    @pl.when(i + 1 < n_t)
    def _():
        fetch(i + 1, 1 - slot)

    pltpu.make_async_copy(ys_ref.at[pl.ds(0, lb)], buf_s.at[slot], sem.at[slot]).wait()
    ld = ld_ref[...]
    gates = gate_ref[...]
    cb = 512
    y = jnp.zeros((tm, x_ref.shape[1]), F32)
    for j in range(lb // cb):
        c = lax.broadcasted_iota(I32, (tm, cb), 1) + j * cb
        w = jnp.zeros((tm, cb), F32)
        for k in range(TOP_K):
            w = jnp.where(c == ld[:, k:k + 1], gates[:, k:k + 1], w)
        y = y + jnp.dot(w.astype(BF16), buf_s[slot, j * cb:(j + 1) * cb, :], preferred_element_type=F32)
    v = ALPHA_DN * x_ref[...] + g2_ref[...] * y
    o_ref[...] = _layer_norm(v, lg_ref[...], lb_ref[...])


def _combine(x2, ldest_col, gates_col, dst, ys, pmods, layer, rows_per_mod, tm, ln_g, ln_b):
    N, D = x2.shape
    n_t, n_chunks = dst.shape
    lb = n_chunks * RUN_ALIGN
    vec = pl.BlockSpec((1, D), lambda i, d: (0, 0))
    grid_spec = pltpu.PrefetchScalarGridSpec(
        num_scalar_prefetch=1,
        grid=(n_t,),
        in_specs=[pl.BlockSpec((tm, D), lambda i, d: (i, 0)),
                  pl.BlockSpec((tm, TOP_K), lambda i, d: (i, 0)),
                  pl.BlockSpec((tm, TOP_K), lambda i, d: (i, 0)),
                  pl.BlockSpec((None, None, None, 1, D), lambda i, d: (layer, 5, (i * tm) // rows_per_mod, 0, 0)),
                  vec, vec,
                  pl.BlockSpec(memory_space=pl.ANY)],
        out_specs=pl.BlockSpec((tm, D), lambda i, d: (i, 0)),
        scratch_shapes=[pltpu.VMEM((2, lb, D), BF16), pltpu.SemaphoreType.DMA((2,))],
    )
    return pl.pallas_call(
        functools.partial(_combine_body, n_chunks=n_chunks),
        grid_spec=grid_spec,
        out_shape=jax.ShapeDtypeStruct((N, D), F32),
        compiler_params=_cparams(("arbitrary",)),
        name="moe_combine",
    )(dst, x2, ldest_col, gates_col, pmods, ln_g.reshape(1, D), ln_b.reshape(1, D), ys)


def _moe_plan(cnts, n_chunks, te, spare_chunk):
    a = RUN_ALIGN
    cnt = jnp.concatenate(cnts, axis=0)
    pc = (cnt + a - 1) // a * a
    seg = (jnp.sum(pc, axis=0) + te - 1) // te * te
    pad_end = jnp.cumsum(seg)
    run_start = (pad_end - seg)[None, :] + jnp.cumsum(pc, axis=0) - pc
    lo = jnp.cumsum(pc, axis=1) - pc
    tables, t0 = [], 0
    for c, nc in zip(cnts, n_chunks):
        sl = slice(t0, t0 + c.shape[0])
        parity = ((t0 + jnp.arange(c.shape[0], dtype=I32)) % 2)[:, None]
        t0 += c.shape[0]
        pos = jnp.arange(nc, dtype=I32) * a
        owner = jnp.sum(((lo[sl] + pc[sl])[:, None, :] <= pos[None, :, None]).astype(I32), axis=-1)
        mine = owner[..., None] == jnp.arange(N_EXP, dtype=I32)
        base = jnp.sum(jnp.where(mine, (run_start[sl] - lo[sl])[:, None, :], 0), axis=-1)
        valid = pos[None, :] < jnp.sum(pc[sl], axis=1, keepdims=True)
        spare = spare_chunk + parity * nc + jnp.arange(nc, dtype=I32)[None, :]
        tables.append(jnp.where(valid, (base + pos[None, :]) // a, spare).astype(I32))
    tail_row = (pad_end - seg + jnp.sum(pc, axis=0))[:, None] + jnp.arange(te // a, dtype=I32)[None, :] * a
    tails = jnp.where(tail_row < pad_end[:, None], tail_row // a, -1).astype(I32).reshape(-1)
    return tables, tails, pad_end


def _moe(passes, layer, P):
    te = EXPERT_TILE
    routed = []
    for x, pmods, shared_mod in passes:
        B, T, D = x.shape
        N = B * T
        tm = min(MOE_TILE, N if shared_mod else T)
        x2 = x.reshape(N, D)
        u2, ldest, gates, cnt = _router(x2, pmods, layer, T, tm, P['w_router'][layer], P['b_router'][layer])
        routed.append((x2, pmods, T, tm, u2, ldest, gates, cnt[:, 0, :], x.shape))
    n_chunks = [r[3] * TOP_K // RUN_ALIGN + N_EXP for r in routed]
    n_tiles = [r[7].shape[0] for r in routed]
    n_assign = sum(r[0].shape[0] for r in routed) * TOP_K
    seg_rows = -(-(n_assign + sum(n_tiles) * N_EXP * RUN_ALIGN + N_EXP * te) // te) * te
    slots = seg_rows + -(-(2 * max(n_chunks) * RUN_ALIGN) // te) * te
    dsts, tails, pad_end = _moe_plan([r[7] for r in routed], n_chunks, te, seg_rows // RUN_ALIGN)
    n_blk = slots // te
    blk_pos = jnp.arange(n_blk, dtype=I32) * te
    blk_e = jnp.minimum(jnp.sum((pad_end[None, :] <= blk_pos[:, None]).astype(I32), axis=1), N_EXP - 1)
    n_used = (pad_end[-1] // te).astype(I32).reshape(1)
    assert len(set(r[3] for r in routed)) == 1, "all passes must use the same MoE tile"
    xs = _dispatch([r[4] for r in routed], [r[5] for r in routed], jnp.concatenate(dsts, axis=0), tails, n_used,
                   slots, routed[0][3], te)
    ys = _experts(xs, blk_e, n_used, layer, P['w_gu'], P['b_gu'], P['w_down'], P['b_down'])
    outs = []
    for r, dst in zip(routed, dsts):
        x2, pmods, T, tm, _, ldest, gates, _, shape = r
        y = _combine(x2, ldest.T, gates.T, dst, ys, pmods, layer, T, tm, P['ln_g'][layer, 1], P['ln_b'][layer, 1])
        outs.append(y.reshape(shape))
    return outs


def _split3(x):
    hi = x.astype(BF16)
    r1 = x - hi.astype(F32)
    mid = r1.astype(BF16)
    lo = (r1 - mid.astype(F32)).astype(BF16)
    return jnp.concatenate([hi, mid, lo], axis=1)


def _inproj1_body(x_ref, xp_ref, xn_ref, sc_ref, sh_ref, w_ref, cw_ref, cb_ref, qs_ref, wg2_ref, wgh_ref, bg_ref,
                  qk_ref, v_ref, o_ref, gcol_ref, grow_ref):
    j = pl.program_id(1)
    n_j = pl.num_programs(1)
    halo = xp_ref.shape[0]
    tm = x_ref.shape[0]
    nh = 4 * H_C

    def adaln(v):
        return v * (1.0 + sc_ref[...]) + sh_ref[...]

    u = adaln(x_ref[...])
    ub = u.astype(BF16)
    zvo = jnp.dot(ub, w_ref[:, 2 * D_MODEL:], preferred_element_type=F32)
    v_ref[...] = zvo[:, :D_MODEL].astype(BF16)
    o_ref[...] = zvo[:, D_MODEL:]
    u_ext = jnp.concatenate([adaln(xp_ref[...]).astype(BF16), ub, adaln(xn_ref[...]).astype(BF16)], axis=0)
    n_ext = tm + 2 * halo
    row = lax.broadcasted_iota(I32, (n_ext, 1), 0)
    inside = jnp.logical_and(jnp.logical_or(row >= halo, j > 0), jnp.logical_or(row < tm + halo, j < n_j - 1))
    cblk = 512
    for c0 in range(0, 2 * D_MODEL, cblk):
        cols = slice(c0, c0 + cblk)
        zqk = jnp.where(inside, jnp.dot(u_ext, w_ref[:, cols], preferred_element_type=F32), 0.0)
        acc = (pltpu.roll(zqk, 2, 0) * cw_ref[0:1, cols] + pltpu.roll(zqk, 1, 0) * cw_ref[1:2, cols]
               + zqk * cw_ref[2:3, cols] + pltpu.roll(zqk, n_ext - 1, 0) * cw_ref[3:4, cols])[halo:tm + halo]
        acc = acc + cb_ref[:, cols]
        qk_ref[:, cols] = (jax.nn.silu(acc) * qs_ref[:, cols]).astype(BF16)
    u_lo = (u - ub.astype(F32)).astype(BF16)
    g2 = jnp.dot(ub, wg2_ref[...], preferred_element_type=F32)
    gc = g2[:, :nh] + g2[:, nh:] + jnp.dot(u_lo, wgh_ref[...], preferred_element_type=F32) + bg_ref[...]
    r_i = lax.broadcasted_iota(I32, (tm, tm), 0)
    c_i = lax.broadcasted_iota(I32, (tm, tm), 1)
    same = (r_i // CHUNK) == (c_i // CHUNK)
    tri_f = jnp.where(jnp.logical_and(same, c_i <= r_i), 1.0, 0.0).astype(BF16)
    tri_b = jnp.where(jnp.logical_and(same, c_i >= r_i), 1.0, 0.0).astype(BF16)
    lf3 = _split3(jax.nn.log_sigmoid(gc))

    def sum3(p, axis):
        if axis == 1:
            return p[:, :nh] + p[:, nh:2 * nh] + p[:, 2 * nh:]
        return p[:nh] + p[nh:2 * nh] + p[2 * nh:]

    col_i = lax.broadcasted_iota(I32, (tm, nh), 1)
    cum = jnp.where(col_i < 2 * H_C,
                    sum3(jnp.dot(tri_f, lf3, preferred_element_type=F32), 1),
                    sum3(jnp.dot(tri_b, lf3, preferred_element_type=F32), 1))
    gcol = jnp.where((col_i // H_C) % 2 == 1, cum, gc)
    gcol_ref[...] = gcol
    eye = jnp.where(r_i == c_i, 1.0, 0.0).astype(BF16)
    gt3 = lax.dot_general(_split3(gcol), eye, (((0,), (0,)), ((), ())), preferred_element_type=F32)
    grow_ref[...] = sum3(gt3, 0)


def _inproj1(x, mods, w_in1, w_gate1, b_gate1, conv_w, conv_b, row_off, row_stride):
    B, T, D = x.shape
    tm = min(TOKEN_TILE, T)
    n_j = T // tm
    nh = 4 * H_C
    halo = 8
    wg_hi = w_gate1.astype(BF16)
    wg_lo = (w_gate1 - wg_hi.astype(F32)).astype(BF16)
    qk_scale = jnp.concatenate([jnp.ones((1, D), F32), jnp.full((1, D), DH_C ** -0.5, F32)], axis=1)
    tok = lambda w: pl.BlockSpec((None, tm, w), lambda b, j: (b, j, 0))
    full = lambda r, c: pl.BlockSpec((r, c), lambda b, j: (0, 0))
    return pl.pallas_call(
        _inproj1_body,
        grid=(B, n_j),
        in_specs=[tok(D),
                  pl.BlockSpec((None, halo, D), lambda b, j: (b, jnp.maximum(j * (tm // halo) - 1, 0), 0)),
                  pl.BlockSpec((None, halo, D), lambda b, j: (b, jnp.minimum((j + 1) * (tm // halo), T // halo - 1), 0)),
                  _mod_spec(1, 1, row_off, row_stride),
                  _mod_spec(1, 0, row_off, row_stride),
                  full(D, 4 * D), full(CONV_W, 2 * D), full(1, 2 * D), full(1, 2 * D),
                  full(D, 2 * nh), full(D, nh), full(1, nh)],
        out_specs=[tok(2 * D), tok(D), tok(D), tok(nh),
                   pl.BlockSpec((nh, tm), lambda b, j: (0, b * n_j + j))],
        out_shape=[jax.ShapeDtypeStruct((B, T, 2 * D), BF16),
                   jax.ShapeDtypeStruct((B, T, D), BF16),
                   jax.ShapeDtypeStruct((B, T, D), F32),
                   jax.ShapeDtypeStruct((B, T, nh), F32),
                   jax.ShapeDtypeStruct((nh, B * T), F32)],
        compiler_params=_cparams(("arbitrary", "arbitrary")),
        name="inproj1",
    )(x, x, x, mods, mods, w_in1.astype(BF16), conv_w, conv_b.reshape(1, 2 * D), qk_scale,
      jnp.concatenate([wg_hi, wg_lo], axis=1), wg_hi, b_gate1.reshape(1, nh))


def _mlstm_body(q_ref, k_ref, v_ref, gcol_ref, grow_ref, c0_ref, n0_ref, m0_ref,
                h_ref, cfin_ref, nfin_ref, mfin_ref, hb_s, *, T):
    n_c = T // CHUNK
    L = CHUNK

    rb = min(T, 512)
    t_i = lax.broadcasted_iota(I32, (L, L), 0)
    s_i = lax.broadcasted_iota(I32, (L, L), 1)

    cfin_ref[...] = c0_ref[...]
    nfin_ref[...] = n0_ref[...]
    mfin_ref[...] = m0_ref[...]

    def chunk(d, c):
        mask = (s_i <= t_i) if d == 0 else (s_i >= t_i)
        last = L - 1 if d == 0 else 0
        off = pl.multiple_of(c * L, L)
        qc = q_ref[pl.ds(off, L), :]
        kc = k_ref[pl.ds(off, L), :]
        vc = v_ref[pl.ds(off, L), :]
        gcol = gcol_ref[pl.ds(off, L), :]
        grow = grow_ref[:, pl.ds(off, L)]
        li_c = gcol[:, 2 * d:2 * d + 1]
        b_c = gcol[:, 2 * d + 1:2 * d + 2]
        li_r = grow[2 * d:2 * d + 1, :]
        b_r = grow[2 * d + 1:2 * d + 2, :]
        m_prev = mfin_ref[d]
        log_d = jnp.where(mask, b_c - b_r + li_r, -jnp.inf)
        m_inter = b_c + m_prev
        m_t = jnp.maximum(m_inter, jnp.max(log_d, axis=-1, keepdims=True))
        dmat = jnp.exp(log_d - m_t)
        w_inter = jnp.exp(m_inter - m_t)
        c_old = cfin_ref[d]
        s = lax.dot_general(qc, kc, (((1,), (1,)), ((), ())), preferred_element_type=F32) * dmat
        inter = lax.dot_general(qc, c_old.astype(BF16), (((1,), (1,)), ((), ())), preferred_element_type=F32)
        num = w_inter * inter + jnp.dot(s.astype(BF16), vc, preferred_element_type=F32)
        qn = jnp.sum(qc.astype(F32) * nfin_ref[d], axis=-1, keepdims=True)
        den = w_inter * qn + jnp.sum(s, axis=-1, keepdims=True)
        h = num / jnp.maximum(jnp.abs(den), jnp.exp(-m_t))
        m_new = m_t[last:last + 1, :]
        b_last = b_c[last:last + 1, :]
        w_s = jnp.exp(b_last - b_c + li_c - m_new)
        decay = jnp.exp(b_last + m_prev - m_new)
        wv = (w_s * vc.astype(F32)).astype(BF16)
        cfin_ref[d] = decay * c_old + lax.dot_general(wv, kc, (((0,), (0,)), ((), ())),
                                                      preferred_element_type=F32)
        nfin_ref[d] = decay * nfin_ref[d] + jnp.sum(w_s * kc.astype(F32), axis=0, keepdims=True)
        mfin_ref[d] = m_new
        return off, h

    def both(ci, _):
        off_f, h_f = chunk(0, ci)
        off_b, h_b = chunk(1, n_c - 1 - ci)
        h_ref[pl.ds(off_f, L), :] = h_f
        hb_s[pl.ds(off_b, L), :] = h_b
        return 0

    lax.fori_loop(0, n_c, both, 0)

    def add_bwd(r, _):
        off = pl.multiple_of(r * rb, rb)
        h_ref[pl.ds(off, rb), :] = h_ref[pl.ds(off, rb), :] + hb_s[pl.ds(off, rb), :]
        return 0

    lax.fori_loop(0, T // rb, add_bwd, 0)


def _mlstm(qk, v, gcol, grow, c0, n0, m0):
    B, T, _ = v.shape
    DH = DH_C
    st = lambda *tail: pl.BlockSpec((None, 2, None) + tail, lambda b, h: (b, 0, h) + (0,) * len(tail))
    return pl.pallas_call(
        functools.partial(_mlstm_body, T=T),
        grid=(B, H_C),
        in_specs=[pl.BlockSpec((None, T, DH), lambda b, h: (b, 0, h)),
                  pl.BlockSpec((None, T, DH), lambda b, h: (b, 0, H_C + h)),
                  pl.BlockSpec((None, T, DH), lambda b, h: (b, 0, h)),
                  pl.BlockSpec((None, None, T, 4), lambda b, h: (b, h, 0, 0)),
                  pl.BlockSpec((None, None, 4, T), lambda b, h: (b, h, 0, 0)),
                  st(DH, DH), st(1, DH), st(1, 1)],
        out_specs=[pl.BlockSpec((None, T, DH), lambda b, h: (b, 0, h)),
                   st(DH, DH), st(1, DH), st(1, 1)],
        out_shape=[jax.ShapeDtypeStruct((B, T, D_MODEL), F32),
                   jax.ShapeDtypeStruct((B, 2, H_C, DH, DH), F32),
                   jax.ShapeDtypeStruct((B, 2, H_C, 1, DH), F32),
                   jax.ShapeDtypeStruct((B, 2, H_C, 1, 1), F32)],
        scratch_shapes=[pltpu.VMEM((T, DH), F32)],
        compiler_params=_cparams(("arbitrary", "arbitrary")),
        name="mlstm",
    )(qk, qk, v, gcol, grow, c0, n0, m0)


def _outproj1_body(x_ref, h_ref, o_ref, g1_ref, w_ref, lg_ref, lb_ref, out_ref):
    y = jax.nn.sigmoid(o_ref[...]) * h_ref[...]
    out = jnp.dot(y.astype(BF16), w_ref[...], preferred_element_type=F32)
    v = ALPHA_DN * x_ref[...] + g1_ref[...] * out
    out_ref[...] = _layer_norm(v, lg_ref[...], lb_ref[...])


def _outproj1(x, h, o, mods, w_out1, ln_g, ln_b, row_off, row_stride):
    B, T, D = x.shape
    tm = min(TOKEN_TILE, T)
    tok = pl.BlockSpec((None, tm, D), lambda b, j: (b, j, 0))
    vec = pl.BlockSpec((1, D), lambda b, j: (0, 0))
    return pl.pallas_call(
        _outproj1_body,
        grid=(B, T // tm),
        in_specs=[tok, tok, tok, _mod_spec(1, 2, row_off, row_stride),
                  pl.BlockSpec((D, D), lambda b, j: (0, 0)), vec, vec],
        out_specs=tok,
        out_shape=jax.ShapeDtypeStruct((B, T, D), F32),
        compiler_params=_cparams(("arbitrary", "arbitrary")),
        name="outproj1",
    )(x, h, o, mods, w_out1.astype(BF16), ln_g.reshape(1, D), ln_b.reshape(1, D))


def _to_col_major(x):
    B, T, C = x.shape
    rows = T // GRID_W
    return x.reshape(B, rows, GRID_W, C).transpose(0, 2, 1, 3).reshape(B, T, C)


def _to_row_major(x):
    B, T, C = x.shape
    rows = T // GRID_W
    return x.reshape(B, GRID_W, rows, C).transpose(0, 2, 1, 3).reshape(B, T, C)


def _mixer0(x, mods, st, P, s5_mats):
    h_rg, s_re, s_im = st
    xa, ga, ub = _inproj0(x, mods, P['w_in0'][0].astype(BF16))
    h, h_fin = _rglru(xa, P['conv_a_w'][0], P['conv_a_b'][0], P['rg_wa'][0], P['rg_ba'][0],
                      P['rg_wi'][0], P['rg_bi'][0], P['rg_lam'][0], h_rg[:, 0].transpose(1, 0, 2))
    y5, s_fin = _s5(ub, s5_mats, _s5_state_to_lanes(s_re[:, 0], s_im[:, 0]))
    x = _outproj0(x, h, ga, y5, ub, mods, P['s5_d'][0], P['glu_w'][0], P['glu_b'][0], P['w_out0'][0],
                  P['ln_g'][0, 0], P['ln_b'][0, 0])
    new_re, new_im = _s5_lanes_to_state(s_fin)
    return x, (h_fin.transpose(1, 0, 2)[:, None], new_re[:, None], new_im[:, None])


def _mixer1(x, mods, st, P):
    m_c, m_n, m_m = st
    B, T, D = x.shape
    qk, v, o, gcol, grow = _inproj1(x, mods, P['w_in1'][0], P['w_gate1'][0], P['b_gate1'][0],
                                    P['conv_c_w'][0], P['conv_c_b'][0], 0, 1)
    gcol = gcol.reshape(B, T, 2, 2, H_C).transpose(0, 4, 1, 2, 3).reshape(B, H_C, T, 4)
    grow = grow.reshape(2, 2, H_C, B, T).transpose(3, 2, 0, 1, 4).reshape(B, H_C, 4, T)
    h, c_fin, n_fin, m_fin = _mlstm(qk, v, gcol, grow,
                                    m_c[:, 0], m_n[:, 0][:, :, :, None, :], m_m[:, 0][:, :, :, None, None])
    x = _outproj1(x, h, o, mods, P['w_out1'][0], P['ln_g'][1, 0], P['ln_b'][1, 0], 0, 1)
    return x, (c_fin[:, None], n_fin[:, None, :, :, 0, :], m_fin[:, None, :, :, 0, 0])


def _forward(x_prompt, x_sample, c, c_ctx, states, P):
    bp = x_prompt.shape[0]
    bs = x_sample.shape[0]
    rows = 1 + bs
    rpad = -(-rows // 8) * 8
    cv = jnp.concatenate([c_ctx[None, :], c, jnp.zeros((rpad - rows, D_MODEL), F32)], axis=0)
    mods = _modulation(cv, P['w_mod'], P['b_mod'])
    s5_mats = _s5_matrices(P['s5_a_re'][0], P['s5_a_im'][0], P['s5_log_dt'][0], P['s5_b_re'][0], P['s5_b_im'][0],
                           P['s5_c_re'][0], P['s5_c_im'][0])
    zero_state = (jnp.zeros((bp, 1, 2, D_A), F32),
                  jnp.zeros((bp, 1, 2, G_B, P_B), F32),
                  jnp.zeros((bp, 1, 2, G_B, P_B), F32),
                  jnp.zeros((bp, 1, 2, H_C, DH_C, DH_C), F32),
                  jnp.zeros((bp, 1, 2, H_C, DH_C), F32),
                  jnp.zeros((bp, 1, 2, H_C), F32))
    mods_ctx = jnp.broadcast_to(mods[:, :, 0:1], mods.shape[:2] + (bp,) + mods.shape[3:])
    mods_lat = mods[:, :, 1:1 + bs]
    xc, new_even = _mixer0(x_prompt, mods_ctx, zero_state[:3], P, s5_mats)
    xl, _ = _mixer0(x_sample, mods_lat, states[:3], P, s5_mats)
    xc, xl = _moe([(xc, mods_ctx, True), (xl, mods_lat, False)], 0, P)
    xl = _to_col_major(xl)
    xc, new_odd = _mixer1(xc, mods_ctx, zero_state[3:], P)
    xl, _ = _mixer1(xl, mods_lat, states[3:], P)
    xc, xl = _moe([(xc, mods_ctx, True), (xl, mods_lat, False)], 1, P)
    return (xc, _to_row_major(xl)) + tuple(new_even) + tuple(new_odd)


def kernel(x_prompt, x_sample, c, c_ctx, state_rglru, state_s5_re, state_s5_im, state_mlstm_C, state_mlstm_n, state_mlstm_m, w_mod, b_mod, ln_g, ln_b, w_in0, conv_a_w, conv_a_b, rg_wa, rg_ba, rg_wi, rg_bi, rg_lam, s5_a_re, s5_a_im, s5_log_dt, s5_b_re, s5_b_im, s5_c_re, s5_c_im, s5_d, glu_w, glu_b, w_out0, w_in1, w_gate1, b_gate1, conv_c_w, conv_c_b, w_out1, w_router, b_router, w_gu, b_gu, w_down, b_down):
    P = dict(w_mod=w_mod, b_mod=b_mod, ln_g=ln_g, ln_b=ln_b, w_in0=w_in0, conv_a_w=conv_a_w,
             conv_a_b=conv_a_b, rg_wa=rg_wa, rg_ba=rg_ba, rg_wi=rg_wi, rg_bi=rg_bi, rg_lam=rg_lam,
             s5_a_re=s5_a_re, s5_a_im=s5_a_im, s5_log_dt=s5_log_dt, s5_b_re=s5_b_re, s5_b_im=s5_b_im,
             s5_c_re=s5_c_re, s5_c_im=s5_c_im, s5_d=s5_d, glu_w=glu_w, glu_b=glu_b, w_out0=w_out0,
             w_in1=w_in1, w_gate1=w_gate1, b_gate1=b_gate1, conv_c_w=conv_c_w, conv_c_b=conv_c_b,
             w_out1=w_out1, w_router=w_router, b_router=b_router, w_gu=w_gu, b_gu=b_gu,
             w_down=w_down, b_down=b_down)
    states = (state_rglru, state_s5_re, state_s5_im, state_mlstm_C, state_mlstm_n, state_mlstm_m)
    return _forward(x_prompt, x_sample, c, c_ctx, states, P)
```

```python
import functools
import math

import jax
import jax.numpy as jnp
from jax import lax
from jax.experimental import pallas as pl
from jax.experimental.pallas import tpu as pltpu

F32 = jnp.float32
BF16 = jnp.bfloat16
I32 = jnp.int32
HI = lax.Precision.HIGHEST

D_MODEL = 1024
DEPTH = 2
GRID_W = 64
D_A = 512
NB_A = 8
BS_A = D_A // NB_A
CONV_W = 4
LRU_C = 8.0
D_B = 512
S5_GROUP = 16
G_B = D_B // S5_GROUP
P_B = 64
H_C = 4
DH_C = D_MODEL // H_C
CHUNK = 128
N_EXP = 32
TOP_K = 4
D_FF = D_MODEL
SWIGLU_LIMIT = 7.0
SWIGLU_ALPHA = 1.702
ALPHA_DN = (2 * DEPTH) ** 0.25
LN_EPS = 1e-5

S5_L = 16
S5_W = S5_L * S5_GROUP
S5_OCT = 8
TOKEN_TILE = 256
EXPERT_TILE = 512
MOE_TILE = 512
RUN_ALIGN = 16
VMEM_LIMIT = 56 * 1024 * 1024


def _cparams(sem, vmem=VMEM_LIMIT):
    return pltpu.CompilerParams(dimension_semantics=sem, vmem_limit_bytes=vmem)


def _layer_norm(v, g, b):
    mu = jnp.mean(v, axis=-1, keepdims=True)
    c = v - mu
    var = jnp.mean(c * c, axis=-1, keepdims=True)
    return c * lax.rsqrt(var + LN_EPS) * g + b


def _mod_spec(layer, which, row_off, row_stride):
    return pl.BlockSpec((None, None, None, 1, D_MODEL),
                        lambda b, j: (layer, which, row_off + b * row_stride, 0, 0))


def _mod_spec_all(layer, which, nb):
    return pl.BlockSpec((None, None, nb, 1, D_MODEL), lambda i: (layer, which, 0, 0, 0))


def _row_permutation(n_out_major, n_out_minor):
    n = n_out_major * n_out_minor
    r_out = lax.broadcasted_iota(I32, (n, n), 0)
    r_in = lax.broadcasted_iota(I32, (n, n), 1)
    hit = jnp.logical_and(r_out // n_out_minor == r_in % n_out_major, r_out % n_out_minor == r_in // n_out_major)
    return jnp.where(hit, 1.0, 0.0).astype(BF16)


def _mod_body(c_ref, w_ref, b_ref, o_ref):
    s = jax.nn.silu(c_ref[...])
    o_ref[...] = jnp.dot(s, w_ref[...], precision=HI, preferred_element_type=F32) + b_ref[...]


def _modulation(cv, w_mod, b_mod):
    R, D = cv.shape
    L, _, N6 = w_mod.shape
    tn = N6 // 6
    out = pl.pallas_call(
        _mod_body,
        grid=(L, N6 // tn),
        in_specs=[pl.BlockSpec((R, D), lambda l, j: (0, 0)),
                  pl.BlockSpec((None, D, tn), lambda l, j: (l, 0, j)),
                  pl.BlockSpec((None, 1, tn), lambda l, j: (l, 0, j))],
        out_specs=pl.BlockSpec((None, R, tn), lambda l, j: (l, 0, j)),
        out_shape=jax.ShapeDtypeStruct((L, R, N6), F32),
        compiler_params=_cparams(("arbitrary", "arbitrary")),
        name="modulation",
    )(cv, w_mod, b_mod.reshape(L, 1, N6))
    return out.reshape(L, R, 6, D).transpose(0, 2, 1, 3)[:, :, :, None, :]


def _inproj0_body(x_ref, sc_ref, sh_ref, w_ref, xa_ref, ga_ref, ub_ref):
    nb, tq, d = x_ref.shape
    u = x_ref[...] * (1.0 + sc_ref[...]) + sh_ref[...]
    ub = u.reshape(nb * tq, d).astype(BF16)
    ut = jnp.dot(_row_permutation(tq, nb), ub, preferred_element_type=F32).astype(BF16)
    z = jnp.dot(ut, w_ref[...], preferred_element_type=F32).reshape(tq, nb, w_ref.shape[1])
    xa_ref[...] = z[:, :, :D_A]
    ga_ref[...] = z[:, :, D_A:2 * D_A]
    ub_ref[...] = z[:, :, 2 * D_A:]


def _inproj0(x, pmods, w_in0):
    B, T, D = x.shape
    tq = TOKEN_TILE // B
    nz = w_in0.shape[1]
    tmaj = pl.BlockSpec((tq, B, D_A), lambda i: (i, 0, 0))
    return pl.pallas_call(
        _inproj0_body,
        grid=(T // tq,),
        in_specs=[pl.BlockSpec((B, tq, D), lambda i: (0, i, 0)),
                  _mod_spec_all(0, 1, B), _mod_spec_all(0, 0, B),
                  pl.BlockSpec((D, nz), lambda i: (0, 0))],
        out_specs=[tmaj, tmaj, tmaj],
        out_shape=[jax.ShapeDtypeStruct((T, B, D_A), F32)] * 3,
        compiler_params=_cparams(("arbitrary",)),
        name="inproj0",
    )(x, pmods, pmods, w_in0)


def _rglru_body(cur_ref, prev_ref, next_ref, cw_ref, cb_ref, wa_ref, wi_ref, ba_ref, bi_ref, lam_ref, h0_ref,
                h_ref, hfin_ref, ext_s, a_s, b_s, carry_s, *, tt, n_t):
    d = pl.program_id(0)
    j = pl.program_id(1)
    jj = j + d * (n_t - 1 - 2 * j)
    nb = cur_ref.shape[1]
    half = D_A // 2
    ext_s[0:2] = jnp.where(jj == 0, 0.0, prev_ref[...])
    ext_s[2:tt + 2] = cur_ref[...]
    ext_s[tt + 2:tt + 3] = jnp.where(jj == n_t - 1, 0.0, next_ref[...])
    xc = ext_s[0:tt] * cw_ref[0]
    for k in range(1, CONV_W):
        xc = xc + ext_s[k:k + tt] * cw_ref[k]
    xc = xc + cb_ref[...]
    x2 = xc.reshape(tt * nb, D_A)
    xb = x2.astype(BF16)

    def gate(w_ref, bias_ref):
        lo = jnp.dot(xb[:, :half], w_ref[0], preferred_element_type=F32)
        hi = jnp.dot(xb[:, half:], w_ref[1], preferred_element_type=F32)
        return jax.nn.sigmoid(jnp.concatenate([lo, hi], axis=1) + bias_ref[...])

    r = gate(wa_ref, ba_ref)
    i = gate(wi_ref, bi_ref)
    log_a = LRU_C * r * jax.nn.log_sigmoid(lam_ref[...])
    a = jnp.exp(log_a)
    one_minus_a2 = -jnp.tanh(log_a) * (a * a + 1.0)
    a_s[...] = a.reshape(tt, nb, D_A)
    b_s[...] = (jnp.sqrt(one_minus_a2) * (i * x2)).reshape(tt, nb, D_A)

    @pl.when(j == 0)
    def _():
        carry_s[...] = h0_ref[...]

    def step(t, h):
        h = a_s[t] * h + b_s[t]
        h_ref[t] = h
        return h

    @pl.when(d == 0)
    def _():
        carry_s[...] = lax.fori_loop(0, tt, step, carry_s[...], unroll=8)

    @pl.when(d == 1)
    def _():
        carry_s[...] = lax.fori_loop(0, tt, lambda t, h: step(tt - 1 - t, h), carry_s[...], unroll=8)

    hfin_ref[...] = carry_s[...]


def _block_diag_halves(w):
    nd = w.shape[0]
    per = NB_A // 2
    w = w.reshape(nd, 2, per, BS_A, BS_A)
    eye = jnp.eye(per, dtype=w.dtype)
    out = jnp.einsum('dhncz,nm->dhncmz', w, eye)
    return out.reshape(nd, 2, per * BS_A, per * BS_A)


def _rglru(x3, conv_w, conv_b, wa, ba, wi, bi, lam, h0):
    T, B, _ = x3.shape
    tt = min(T, 1024 // B)
    n_t = T // tt

    def cur_map(d, j):
        return (j + d * (n_t - 1 - 2 * j), 0, 0)

    def prev_map(d, j):
        jj = j + d * (n_t - 1 - 2 * j)
        return (jnp.maximum(jj * (tt // 2) - 1, 0), 0, 0)

    def next_map(d, j):
        jj = j + d * (n_t - 1 - 2 * j)
        return (jnp.minimum((jj + 1) * tt, T - 1), 0, 0)

    dir_spec3 = pl.BlockSpec((None, 1, D_A), lambda d, j: (d, 0, 0))
    h, hfin = pl.pallas_call(
        functools.partial(_rglru_body, tt=tt, n_t=n_t),
        grid=(2, n_t),
        in_specs=[pl.BlockSpec((tt, B, D_A), cur_map),
                  pl.BlockSpec((2, B, D_A), prev_map),
                  pl.BlockSpec((1, B, D_A), next_map),
                  pl.BlockSpec((CONV_W, 1, D_A), lambda d, j: (0, 0, 0)),
                  pl.BlockSpec((1, 1, D_A), lambda d, j: (0, 0, 0)),
                  pl.BlockSpec((None, 2, D_A // 2, D_A // 2), lambda d, j: (d, 0, 0, 0)),
                  pl.BlockSpec((None, 2, D_A // 2, D_A // 2), lambda d, j: (d, 0, 0, 0)),
                  dir_spec3, dir_spec3, dir_spec3,
                  pl.BlockSpec((None, B, D_A), lambda d, j: (d, 0, 0))],
        out_specs=[pl.BlockSpec((None, tt, B, D_A), lambda d, j: (d,) + cur_map(d, j)),
                   pl.BlockSpec((None, B, D_A), lambda d, j: (d, 0, 0))],
        out_shape=[jax.ShapeDtypeStruct((2, T, B, D_A), F32),
                   jax.ShapeDtypeStruct((2, B, D_A), F32)],
        scratch_shapes=[pltpu.VMEM((tt + 3, B, D_A), F32),
                        pltpu.VMEM((tt, B, D_A), F32),
                        pltpu.VMEM((tt, B, D_A), F32),
                        pltpu.VMEM((B, D_A), F32)],
        compiler_params=_cparams(("arbitrary", "arbitrary")),
        name="rglru",
    )(x3, x3, x3, conv_w.reshape(CONV_W, 1, D_A), conv_b.reshape(1, 1, D_A),
      _block_diag_halves(wa).astype(BF16), _block_diag_halves(wi).astype(BF16),
      ba.reshape(2, 1, D_A), bi.reshape(2, 1, D_A), lam.reshape(2, 1, D_A), h0)
    return h, hfin


def _s5_matrices(a_re, a_im, log_dt, b_re, b_im, c_re, c_im):
    L = S5_L
    lam = lax.complex(a_re.astype(F32), a_im.astype(F32))
    dt = jnp.exp(log_dt.astype(F32))[..., None]
    ldt = lam * dt
    a_bar = jnp.exp(ldt)
    b_bar = ((a_bar - 1.0) / lam)[..., None] * lax.complex(b_re.astype(F32), b_im.astype(F32))
    cc = lax.complex(c_re.astype(F32), c_im.astype(F32))
    ks = jnp.arange(L + 1, dtype=F32)
    pw = jnp.exp(ldt[:, :, None, :] * ks[None, None, :, None].astype(jnp.complex64))
    kern = jnp.real(jnp.einsum('dgjp,dgkp,dgpi->dgkji', cc, pw[:, :, :L], b_bar, precision=HI))
    s_idx = jnp.arange(L)[:, None]
    t_idx = jnp.arange(L)[None, :]
    lag_f = jnp.clip(t_idx - s_idx, 0, L - 1)
    lag_b = jnp.clip(s_idx - t_idx, 0, L - 1)
    m_f = jnp.where((t_idx >= s_idx)[None, :, :, None, None], kern[0][:, lag_f], 0.0)
    m_b = jnp.where((s_idx >= t_idx)[None, :, :, None, None], kern[1][:, lag_b], 0.0)
    m = (m_f + m_b).transpose(0, 1, 4, 2, 3).reshape(G_B, S5_W, S5_W)
    down = (L - ks[:L])[None, :, None].astype(jnp.complex64)
    pw_down_b = jnp.exp(ldt[1][:, None, :] * down)
    g_f = jnp.exp(ldt[0][:, None, :] * (down - 1.0))[..., None] * b_bar[0][:, None]
    g_b = pw[1][:, :L, :, None] * b_bar[1][:, None]

    def g_cols(x):
        return x.transpose(0, 1, 3, 2).reshape(G_B, S5_W, P_B)

    gs = jnp.concatenate([g_cols(jnp.real(g_f)), g_cols(jnp.real(g_b)),
                          g_cols(jnp.imag(g_f)), g_cols(jnp.imag(g_b))], axis=-1)
    e_f = cc[0][:, None] * pw[0][:, 1:, None, :]
    e_b = cc[1][:, None] * pw_down_b[:, :, None, :]

    def e_rows(x):
        return x.transpose(0, 3, 1, 2).reshape(G_B, P_B, S5_W)

    e = jnp.concatenate([e_rows(jnp.real(e_f)), e_rows(jnp.real(e_b)),
                         -e_rows(jnp.imag(e_f)), -e_rows(jnp.imag(e_b))], axis=1)
    a_l = pw[:, :, L]
    al = jnp.concatenate([jnp.real(a_l[0]), jnp.real(a_l[1]), jnp.imag(a_l[0]), jnp.imag(a_l[1])], axis=-1)
    no = G_B // S5_OCT
    ow = S5_OCT * S5_W
    rows_sgi = lambda x: x.reshape(no, S5_OCT, L, S5_GROUP, S5_W).transpose(0, 2, 1, 3, 4).reshape(no, ow, S5_W)
    src = jnp.stack([rows_sgi(m), rows_sgi(gs), e.reshape(no, ow, S5_W)]).astype(BF16)
    r = jnp.arange(ow, dtype=I32)
    c = jnp.arange(S5_W, dtype=I32)
    grp_sgi = (r // S5_GROUP) % S5_OCT
    grp_gl = r // S5_W
    src_sgi = (r // (S5_OCT * S5_GROUP)) * S5_GROUP + r % S5_GROUP
    src_gl = r % S5_W
    spread = jnp.stack([src_sgi, src_gl, src_sgi])[:, None, :] == c[None, :, None]
    row_grp = jnp.stack([grp_sgi, grp_sgi, grp_gl])[:, :, None]
    col_grp = jnp.stack([grp_sgi, grp_gl, grp_sgi])[:, None, :]
    tr = 512
    out = pl.pallas_call(
        _s5_expand_body,
        grid=(3, no, ow // tr),
        in_specs=[pl.BlockSpec((None, None, tr, S5_W), lambda k, q, i: (k, q, i, 0)),
                  pl.BlockSpec((None, S5_W, ow), lambda k, q, i: (k, 0, 0)),
                  pl.BlockSpec((None, tr, 1), lambda k, q, i: (k, i, 0)),
                  pl.BlockSpec((None, 1, ow), lambda k, q, i: (k, 0, 0))],
        out_specs=pl.BlockSpec((None, None, tr, ow), lambda k, q, i: (k, q, i, 0)),
        out_shape=jax.ShapeDtypeStruct((3, no, ow, ow), BF16),
        compiler_params=_cparams(("arbitrary", "arbitrary", "arbitrary")),
        name="s5_expand",
    )(src, spread.astype(BF16), row_grp, col_grp)
    return out, al.reshape(1, G_B * S5_W)


def _s5_expand_body(src_ref, spread_ref, rg_ref, cg_ref, o_ref):
    wide = jnp.dot(src_ref[...], spread_ref[...], preferred_element_type=F32)
    o_ref[...] = jnp.where(rg_ref[...] == cg_ref[...], wide, 0.0).astype(BF16)


def _s5_fill_lhs(u_ref, lhs_s):
    tc, _, nb, lanes = u_ref.shape
    for s in range(S5_L):
        lhs_s[:, s * lanes:(s + 1) * lanes] = u_ref[:, s].reshape(tc * nb, lanes).astype(BF16)


def _s5_state_body(u_ref, g_ref, f_ref, lhs_s):
    _s5_fill_lhs(u_ref, lhs_s)
    f_ref[...] = jnp.dot(lhs_s[...], g_ref[...], preferred_element_type=F32).reshape(f_ref.shape)


def _s5_scan_body(f_ref, a_ref, s0_ref, sin_ref, sfin_ref, *, n_c):
    nb, width = s0_ref.shape
    hw = 2 * P_B
    n_g = width // S5_W
    a = a_ref[...]
    a_re = [a[:, k * S5_W:k * S5_W + hw] for k in range(n_g)]
    a_im = [a[:, k * S5_W + hw:(k + 1) * S5_W] for k in range(n_g)]
    is_fwd = (lax.broadcasted_iota(I32, (nb, width), 1) % hw) < P_B

    def split(x):
        return tuple(x[:, k * hw:(k + 1) * hw] for k in range(2 * n_g))

    def merge(parts):
        return jnp.concatenate(parts, axis=-1)

    def advance(c, parts):
        f = split(f_ref[c])
        out = []
        for k in range(n_g):
            s_re, s_im = parts[2 * k], parts[2 * k + 1]
            out.append(a_re[k] * s_re - a_im[k] * s_im + f[2 * k])
            out.append(a_re[k] * s_im + a_im[k] * s_re + f[2 * k + 1])
        return tuple(out)

    def fwd(c, parts):
        sin_ref[c] = merge(parts)
        return advance(c, parts)

    init = split(s0_ref[...])
    fin_f = lax.fori_loop(0, n_c, fwd, init)

    def bwd(k, parts):
        c = n_c - 1 - k
        sin_ref[c] = jnp.where(is_fwd, sin_ref[c], merge(parts))
        return advance(c, parts)

    fin_b = lax.fori_loop(0, n_c, bwd, init)
    sfin_ref[...] = jnp.where(is_fwd, merge(fin_f), merge(fin_b))


def _s5_out_body(u_ref, sin_ref, m_ref, e_ref, y_ref, lhs_s):
    _s5_fill_lhs(u_ref, lhs_s)
    tc, _, nb, lanes = u_ref.shape
    sin = sin_ref[...].reshape(tc * nb, sin_ref.shape[-1]).astype(BF16)
    y = (jnp.dot(lhs_s[...], m_ref[...], preferred_element_type=F32)
         + jnp.dot(sin, e_ref[...], preferred_element_type=F32))
    for s in range(S5_L):
        y_ref[:, s] = y[:, s * lanes:(s + 1) * lanes].reshape(tc, nb, lanes)


def _s5(ub, mats, s0):
    mge, al = mats
    T, B, _ = ub.shape
    n_c = T // S5_L
    no = G_B // S5_OCT
    lanes = S5_OCT * S5_GROUP
    ow = S5_OCT * S5_W
    tc = min(n_c, TOKEN_TILE // B)
    u4 = ub.reshape(n_c, S5_L, B, D_B)
    u_spec = pl.BlockSpec((tc, S5_L, B, lanes), lambda q, i: (i, 0, 0, q))
    w_spec = lambda kind: pl.BlockSpec((None, None, ow, ow), lambda q, i: (kind, q, 0, 0))
    st_spec = pl.BlockSpec((tc, B, ow), lambda q, i: (i, 0, q))
    f_loc = pl.pallas_call(
        _s5_state_body,
        grid=(no, n_c // tc),
        in_specs=[u_spec, w_spec(1)],
        out_specs=st_spec,
        out_shape=jax.ShapeDtypeStruct((n_c, B, G_B * S5_W), F32),
        scratch_shapes=[pltpu.VMEM((tc * B, S5_L * lanes), BF16)],
        compiler_params=_cparams(("arbitrary", "arbitrary")),
        name="s5_state",
    )(u4, mge)
    sw = 4 * S5_W
    sin, sfin = pl.pallas_call(
        functools.partial(_s5_scan_body, n_c=n_c),
        grid=(G_B * S5_W // sw,),
        in_specs=[pl.BlockSpec((n_c, B, sw), lambda g: (0, 0, g)),
                  pl.BlockSpec((1, sw), lambda g: (0, g)),
                  pl.BlockSpec((B, sw), lambda g: (0, g))],
        out_specs=[pl.BlockSpec((n_c, B, sw), lambda g: (0, 0, g)),
                   pl.BlockSpec((B, sw), lambda g: (0, g))],
        out_shape=[jax.ShapeDtypeStruct((n_c, B, G_B * S5_W), F32),
                   jax.ShapeDtypeStruct((B, G_B * S5_W), F32)],
        compiler_params=_cparams(("arbitrary",)),
        name="s5_scan",
    )(f_loc, al, s0)
    y = pl.pallas_call(
        _s5_out_body,
        grid=(no, n_c // tc),
        in_specs=[u_spec, st_spec, w_spec(0), w_spec(2)],
        out_specs=u_spec,
        out_shape=jax.ShapeDtypeStruct((n_c, S5_L, B, D_B), F32),
        scratch_shapes=[pltpu.VMEM((tc * B, S5_L * lanes), BF16)],
        compiler_params=_cparams(("arbitrary", "arbitrary")),
        name="s5_out",
    )(u4, sin, mge, mge)
    return y.reshape(T, B, D_B), sfin


def _s5_state_to_lanes(s_re, s_im):
    parts = [s_re[:, 0], s_re[:, 1], s_im[:, 0], s_im[:, 1]]
    return jnp.concatenate(parts, axis=-1).reshape(s_re.shape[0], G_B * S5_W)


def _s5_lanes_to_state(s):
    s = s.reshape(s.shape[0], G_B, 4, P_B)
    return jnp.stack([s[:, :, 0], s[:, :, 1]], axis=1), jnp.stack([s[:, :, 2], s[:, :, 3]], axis=1)


def _outproj0_body(x_ref, h_ref, ga_ref, y_ref, ub_ref, g1_ref, d_ref, gw_ref, gb_ref, wo_ref,
                   lg_ref, lb_ref, o_ref):
    nb, tq, d = x_ref.shape
    rows = tq * nb

    def flat(v):
        return v.reshape(rows, v.shape[-1])

    ya = flat(h_ref[0] + h_ref[1]) * jax.nn.gelu(flat(ga_ref[...]))
    yb = flat(y_ref[...]) + d_ref[...] * flat(ub_ref[...])
    g = jax.nn.gelu(yb)
    gate = jax.nn.sigmoid(jnp.dot(g.astype(BF16), gw_ref[...], preferred_element_type=F32) + gb_ref[...])
    cat = jnp.concatenate([ya, g * gate], axis=1).astype(BF16)
    cat = jnp.dot(_row_permutation(nb, tq), cat, preferred_element_type=F32).astype(BF16)
    out = jnp.dot(cat, wo_ref[...], preferred_element_type=F32).reshape(nb, tq, d)
    v = ALPHA_DN * x_ref[...] + g1_ref[...] * out
    o_ref[...] = _layer_norm(v, lg_ref[...], lb_ref[...])


def _outproj0(x, h, ga, y5, ub, pmods, s5_d, glu_w, glu_b, w_out0, ln_g, ln_b):
    B, T, D = x.shape
    tq = TOKEN_TILE // B
    tmaj = pl.BlockSpec((tq, B, D_A), lambda i: (i, 0, 0))
    vec = lambda w: pl.BlockSpec((1, w), lambda i: (0, 0))
    return pl.pallas_call(
        _outproj0_body,
        grid=(T // tq,),
        in_specs=[pl.BlockSpec((B, tq, D), lambda i: (0, i, 0)),
                  pl.BlockSpec((2, tq, B, D_A), lambda i: (0, i, 0, 0)),
                  tmaj, tmaj, tmaj,
                  _mod_spec_all(0, 2, B),
                  vec(D_B),
                  pl.BlockSpec((D_B, D_B), lambda i: (0, 0)),
                  vec(D_B),
                  pl.BlockSpec((D, D), lambda i: (0, 0)),
                  vec(D), vec(D)],
        out_specs=pl.BlockSpec((B, tq, D), lambda i: (0, i, 0)),
        out_shape=jax.ShapeDtypeStruct((B, T, D), F32),
        compiler_params=_cparams(("arbitrary",)),
        name="outproj0",
    )(x, h, ga, y5, ub, pmods, s5_d.reshape(1, D_B), glu_w.astype(BF16), glu_b.reshape(1, D_B),
      w_out0.astype(BF16), ln_g.reshape(1, D), ln_b.reshape(1, D))


def _router_body(x_ref, sc_ref, sh_ref, wr_ref, br_ref, u_ref, ldest_ref, gate_ref, cnt_ref):
    u = x_ref[...] * (1.0 + sc_ref[...]) + sh_ref[...]
    u_ref[...] = u.astype(BF16)
    tm = u.shape[0]
    logits = lax.dot_general(wr_ref[...], u, (((1,), (1,)), ((), ())), precision=HI,
                             preferred_element_type=F32) + br_ref[...]
    e_iota = lax.broadcasted_iota(I32, logits.shape, 0)
    work = logits
    vals, hots = [], []
    for _ in range(TOP_K):
        m = jnp.max(work, axis=0, keepdims=True)
        idx = jnp.min(jnp.where(work == m, e_iota, N_EXP), axis=0, keepdims=True)
        hot = e_iota == idx
        vals.append(m)
        hots.append(hot)
        work = jnp.where(hot, -jnp.inf, work)
    ex = [jnp.exp(v - vals[0]) for v in vals]
    den = ex[0] + ex[1] + ex[2] + ex[3]
    gate_ref[...] = jnp.concatenate([e / den for e in ex], axis=0)
    hot_sum = jnp.zeros(logits.shape, F32)
    for hot in hots:
        hot_sum = hot_sum + hot.astype(F32)
    hot_b = hot_sum.astype(BF16)
    before = lax.broadcasted_iota(I32, (tm, tm), 0) < lax.broadcasted_iota(I32, (tm, tm), 1)
    excl = jnp.dot(hot_b, jnp.where(before, 1.0, 0.0).astype(BF16), preferred_element_type=F32)
    cnt_row = lax.dot_general(jnp.ones((8, tm), BF16), hot_b, (((1,), (1,)), ((), ())),
                              preferred_element_type=F32)[0:1]
    cnt_ref[...] = cnt_row.astype(I32)
    run_len = jnp.ceil(cnt_row * (1.0 / RUN_ALIGN)) * RUN_ALIGN
    lower = lax.broadcasted_iota(I32, (N_EXP, N_EXP), 1) < lax.broadcasted_iota(I32, (N_EXP, N_EXP), 0)
    run_off = jnp.sum(jnp.where(lower, run_len, 0.0), axis=1, keepdims=True)
    base = excl + run_off
    rows = [jnp.sum(jnp.where(hot, base, 0.0), axis=0, keepdims=True) for hot in hots]
    ldest_ref[...] = jnp.concatenate(rows, axis=0).astype(I32)


def _router(x2, pmods, layer, rows_per_mod, tm, w_router, b_router):
    N, D = x2.shape
    n_t = N // tm
    mod = lambda which: pl.BlockSpec((None, None, None, 1, D),
                                     lambda i: (layer, which, (i * tm) // rows_per_mod, 0, 0))
    lane_spec = pl.BlockSpec((TOP_K, tm), lambda i: (0, i))
    return pl.pallas_call(
        _router_body,
        grid=(n_t,),
        in_specs=[pl.BlockSpec((tm, D), lambda i: (i, 0)),
                  mod(4), mod(3),
                  pl.BlockSpec((N_EXP, D), lambda i: (0, 0)),
                  pl.BlockSpec((N_EXP, 1), lambda i: (0, 0))],
        out_specs=[pl.BlockSpec((tm, D), lambda i: (i, 0)),
                   lane_spec, lane_spec,
                   pl.BlockSpec((None, 1, N_EXP), lambda i: (i, 0, 0))],
        out_shape=[jax.ShapeDtypeStruct((N, D), BF16),
                   jax.ShapeDtypeStruct((TOP_K, N), I32),
                   jax.ShapeDtypeStruct((TOP_K, N), F32),
                   jax.ShapeDtypeStruct((n_t, 1, N_EXP), I32)],
        compiler_params=_cparams(("arbitrary",)),
        name="router",
    )(x2, pmods, pmods, w_router.T, b_router.reshape(N_EXP, 1))


def _expert_body(blk_e_ref, n_used_ref, xs_ref, wgu_ref, bgu_ref, wdn_ref, bdn_ref, o_ref, wgu_s, wdn_s):
    i = pl.program_id(0)
    prev = blk_e_ref[jnp.maximum(i - 1, 0)]
    changed = jnp.logical_or(i == 0, blk_e_ref[i] != prev)

    @pl.when(changed)
    def _():
        wgu_s[...] = wgu_ref[...].astype(BF16)
        wdn_s[...] = wdn_ref[...].astype(BF16)

    @pl.when(i < n_used_ref[0])
    def _():
        h = jnp.dot(xs_ref[...], wgu_s[...], preferred_element_type=F32) + bgu_ref[...]
        gt = jnp.minimum(h[:, :D_FF], SWIGLU_LIMIT)
        up = jnp.clip(h[:, D_FF:], -SWIGLU_LIMIT, SWIGLU_LIMIT)
        act = (up + 1.0) * gt * jax.nn.sigmoid(SWIGLU_ALPHA * gt)
        y = jnp.dot(act.astype(BF16), wdn_s[...], preferred_element_type=F32) + bdn_ref[...]
        o_ref[...] = y.astype(o_ref.dtype)

    @pl.when(i >= n_used_ref[0])
    def _():
        o_ref[...] = jnp.zeros_like(o_ref)


def _experts(xs, blk_e, n_used, layer, w_gu, b_gu, w_dn, b_dn):
    slots, D = xs.shape
    te = EXPERT_TILE
    n_blk = slots // te
    grid_spec = pltpu.PrefetchScalarGridSpec(
        num_scalar_prefetch=2,
        grid=(n_blk,),
        in_specs=[pl.BlockSpec((te, D), lambda i, be, nu: (jnp.minimum(i, nu[0] - 1), 0)),
                  pl.BlockSpec((None, None, D, 2 * D_FF), lambda i, be, nu: (layer, be[i], 0, 0)),
                  pl.BlockSpec((None, None, 1, 2 * D_FF), lambda i, be, nu: (layer, be[i], 0, 0)),
                  pl.BlockSpec((None, None, D_FF, D), lambda i, be, nu: (layer, be[i], 0, 0)),
                  pl.BlockSpec((None, None, 1, D), lambda i, be, nu: (layer, be[i], 0, 0))],
        out_specs=pl.BlockSpec((te, D), lambda i, be, nu: (i, 0)),
        scratch_shapes=[pltpu.VMEM((D, 2 * D_FF), BF16), pltpu.VMEM((D_FF, D), BF16)],
    )
    return pl.pallas_call(
        _expert_body,
        grid_spec=grid_spec,
        out_shape=jax.ShapeDtypeStruct((slots, D), BF16),
        compiler_params=_cparams(("arbitrary",)),
        name="experts",
    )(blk_e, n_used, xs, w_gu, b_gu.reshape(DEPTH, N_EXP, 1, 2 * D_FF), w_dn, b_dn.reshape(DEPTH, N_EXP, 1, D))


def _chunk_copy(vmem_ref, hbm_ref, sem, k, dst_chunk, to_hbm):
    row = k * RUN_ALIGN if isinstance(k, int) else pl.multiple_of(k * RUN_ALIGN, RUN_ALIGN)
    local = vmem_ref.at[pl.ds(row, RUN_ALIGN)]
    remote = hbm_ref.at[pl.ds(pl.multiple_of(dst_chunk * RUN_ALIGN, RUN_ALIGN), RUN_ALIGN)]
    return pltpu.make_async_copy(local, remote, sem) if to_hbm else pltpu.make_async_copy(remote, local, sem)


def _dispatch_body(dst_ref, tail_ref, nu_ref, *rest, n_chunks, tiles, n_blk):
    n_p = len(tiles)
    xs_ref, buf_s, zero_s, sem = rest[2 * n_p:]
    i = pl.program_id(0)
    n_t = pl.num_programs(0)
    slot = i % 2
    lb = n_chunks * RUN_ALIGN
    te = zero_s.shape[0]
    rb = 256

    def sort_rows(u_ref, ld_ref):
        tm = u_ref.shape[0]
        ld = ld_ref[...]
        u = u_ref[...]
        for j in range(lb // rb):
            r = lax.broadcasted_iota(I32, (rb, tm), 0) + j * rb
            p = jnp.where(r == ld[0:1], 1.0, jnp.where(r == ld[1:2], 1.0, jnp.where(
                r == ld[2:3], 1.0, jnp.where(r == ld[3:4], 1.0, 0.0))))
            buf_s[slot, j * rb:(j + 1) * rb, :] = jnp.dot(p.astype(BF16), u,
                                                          preferred_element_type=F32).astype(BF16)

    for p, (first, count) in enumerate(tiles):
        @pl.when(jnp.logical_and(i >= first, i < first + count))
        def _(p=p):
            sort_rows(rest[2 * p], rest[2 * p + 1])

    def start(k, _):
        _chunk_copy(buf_s.at[slot], xs_ref, sem.at[slot], k, dst_ref[i, k], True).start()
        return 0

    lax.fori_loop(0, n_chunks, start, 0, unroll=8)

    def drain(s):
        pltpu.make_async_copy(buf_s.at[s], xs_ref.at[pl.ds(0, lb)], sem.at[s]).wait()

    @pl.when(i > 0)
    def _():
        drain(1 - slot)

    @pl.when(i == n_t - 1)
    def _():
        drain(slot)
        zero_s[...] = jnp.zeros_like(zero_s)
        n_tail = tail_ref.shape[0]

        def tail_copy(t):
            return _chunk_copy(zero_s, xs_ref, sem.at[2], 0, tail_ref[t], True)

        def block_copy(k):
            return pltpu.make_async_copy(zero_s, xs_ref.at[pl.ds(pl.multiple_of(k * te, te), te)], sem.at[2])

        def each(n, pred, copy, wait):
            def body(t, _):
                @pl.when(pred(t))
                def _():
                    copy(t).wait() if wait else copy(t).start()
                return 0
            lax.fori_loop(0, n, body, 0)

        for wait in (False, True):
            each(n_tail, lambda t: tail_ref[t] >= 0, tail_copy, wait)
            each(n_blk, lambda k: k >= nu_ref[0], block_copy, wait)


def _dispatch(u2s, ldests, dst, tails, n_used, slots, tm, te):
    D = u2s[0].shape[1]
    n_t, n_chunks = dst.shape
    lb = n_chunks * RUN_ALIGN
    tiles, first = [], 0
    for u2 in u2s:
        tiles.append((first, u2.shape[0] // tm))
        first += u2.shape[0] // tm
    assert first == n_t

    def own_tile(first, count):
        return lambda i, d, t, nu: jnp.clip(i - first, 0, count - 1)

    in_specs, operands = [], []
    for (first, count), u2, ld in zip(tiles, u2s, ldests):
        tile = own_tile(first, count)
        in_specs += [pl.BlockSpec((tm, D), lambda i, d, t, nu, tile=tile: (tile(i, d, t, nu), 0)),
                     pl.BlockSpec((TOP_K, tm), lambda i, d, t, nu, tile=tile: (0, tile(i, d, t, nu)))]
        operands += [u2, ld]
    grid_spec = pltpu.PrefetchScalarGridSpec(
        num_scalar_prefetch=3,
        grid=(n_t,),
        in_specs=in_specs,
        out_specs=pl.BlockSpec(memory_space=pl.ANY),
        scratch_shapes=[pltpu.VMEM((2, lb, D), BF16), pltpu.VMEM((te, D), BF16), pltpu.SemaphoreType.DMA((3,))],
    )
    return pl.pallas_call(
        functools.partial(_dispatch_body, n_chunks=n_chunks, tiles=tuple(tiles), n_blk=slots // te),
        grid_spec=grid_spec,
        out_shape=jax.ShapeDtypeStruct((slots, D), BF16),
        compiler_params=_cparams(("arbitrary",)),
        name="moe_dispatch",
    )(dst, tails, n_used, *operands)


def _combine_body(dst_ref, x_ref, ld_ref, gate_ref, g2_ref, lg_ref, lb_ref, ys_ref, o_ref, buf_s, sem, *, n_chunks):
    i = pl.program_id(0)
    n_t = pl.num_programs(0)
    slot = i % 2
    tm = x_ref.shape[0]
    lb = n_chunks * RUN_ALIGN

    def fetch(step, s):
        def body(k, _):
            _chunk_copy(buf_s.at[s], ys_ref, sem.at[s], k, dst_ref[step, k], False).start()
            return 0
        lax.fori_loop(0, n_chunks, body, 0, unroll=8)

    @pl.when(i == 0)
    def _():
        fetch(0, 0)

    @pl.when(i + 1 < n_t)
    def _():
        fetch(i + 1, 1 - slot)

    pltpu.make_async_copy(ys_ref.at[pl.ds(0, lb)], buf_s.at[slot], sem.at[slot]).wait()
    ld = ld_ref[...]
    gates = gate_ref[...]
    cb = 512
    y = jnp.zeros((tm, x_ref.shape[1]), F32)
    for j in range(lb // cb):
        c = lax.broadcasted_iota(I32, (tm, cb), 1) + j * cb
        w = jnp.zeros((tm, cb), F32)
        for k in range(TOP_K):
            w = jnp.where(c == ld[:, k:k + 1], gates[:, k:k + 1], w)
        y = y + jnp.dot(w.astype(BF16), buf_s[slot, j * cb:(j + 1) * cb, :], preferred_element_type=F32)
    v = ALPHA_DN * x_ref[...] + g2_ref[...] * y
    o_ref[...] = _layer_norm(v, lg_ref[...], lb_ref[...])


def _combine(x2, ldest_col, gates_col, dst, ys, pmods, layer, rows_per_mod, tm, ln_g, ln_b):
    N, D = x2.shape
    n_t, n_chunks = dst.shape
    lb = n_chunks * RUN_ALIGN
    vec = pl.BlockSpec((1, D), lambda i, d: (0, 0))
    grid_spec = pltpu.PrefetchScalarGridSpec(
        num_scalar_prefetch=1,
        grid=(n_t,),
        in_specs=[pl.BlockSpec((tm, D), lambda i, d: (i, 0)),
                  pl.BlockSpec((tm, TOP_K), lambda i, d: (i, 0)),
                  pl.BlockSpec((tm, TOP_K), lambda i, d: (i, 0)),
                  pl.BlockSpec((None, None, None, 1, D), lambda i, d: (layer, 5, (i * tm) // rows_per_mod, 0, 0)),
                  vec, vec,
                  pl.BlockSpec(memory_space=pl.ANY)],
        out_specs=pl.BlockSpec((tm, D), lambda i, d: (i, 0)),
        scratch_shapes=[pltpu.VMEM((2, lb, D), BF16), pltpu.SemaphoreType.DMA((2,))],
    )
    return pl.pallas_call(
        functools.partial(_combine_body, n_chunks=n_chunks),
        grid_spec=grid_spec,
        out_shape=jax.ShapeDtypeStruct((N, D), F32),
        compiler_params=_cparams(("arbitrary",)),
        name="moe_combine",
    )(dst, x2, ldest_col, gates_col, pmods, ln_g.reshape(1, D), ln_b.reshape(1, D), ys)


def _moe_plan(cnts, n_chunks, te, spare_chunk):
    a = RUN_ALIGN
    cnt = jnp.concatenate(cnts, axis=0)
    pc = (cnt + a - 1) // a * a
    seg = (jnp.sum(pc, axis=0) + te - 1) // te * te
    pad_end = jnp.cumsum(seg)
    run_start = (pad_end - seg)[None, :] + jnp.cumsum(pc, axis=0) - pc
    lo = jnp.cumsum(pc, axis=1) - pc
    tables, t0 = [], 0
    for c, nc in zip(cnts, n_chunks):
        sl = slice(t0, t0 + c.shape[0])
        parity = ((t0 + jnp.arange(c.shape[0], dtype=I32)) % 2)[:, None]
        t0 += c.shape[0]
        pos = jnp.arange(nc, dtype=I32) * a
        owner = jnp.sum(((lo[sl] + pc[sl])[:, None, :] <= pos[None, :, None]).astype(I32), axis=-1)
        mine = owner[..., None] == jnp.arange(N_EXP, dtype=I32)
        base = jnp.sum(jnp.where(mine, (run_start[sl] - lo[sl])[:, None, :], 0), axis=-1)
        valid = pos[None, :] < jnp.sum(pc[sl], axis=1, keepdims=True)
        spare = spare_chunk + parity * nc + jnp.arange(nc, dtype=I32)[None, :]
        tables.append(jnp.where(valid, (base + pos[None, :]) // a, spare).astype(I32))
    tail_row = (pad_end - seg + jnp.sum(pc, axis=0))[:, None] + jnp.arange(te // a, dtype=I32)[None, :] * a
    tails = jnp.where(tail_row < pad_end[:, None], tail_row // a, -1).astype(I32).reshape(-1)
    return tables, tails, pad_end


def _moe(passes, layer, P):
    te = EXPERT_TILE
    routed = []
    for x, pmods, shared_mod in passes:
        B, T, D = x.shape
        N = B * T
        tm = min(MOE_TILE, N if shared_mod else T)
        x2 = x.reshape(N, D)
        u2, ldest, gates, cnt = _router(x2, pmods, layer, T, tm, P['w_router'][layer], P['b_router'][layer])
        routed.append((x2, pmods, T, tm, u2, ldest, gates, cnt[:, 0, :], x.shape))
    n_chunks = [r[3] * TOP_K // RUN_ALIGN + N_EXP for r in routed]
    n_tiles = [r[7].shape[0] for r in routed]
    n_assign = sum(r[0].shape[0] for r in routed) * TOP_K
    seg_rows = -(-(n_assign + sum(n_tiles) * N_EXP * RUN_ALIGN + N_EXP * te) // te) * te
    slots = seg_rows + -(-(2 * max(n_chunks) * RUN_ALIGN) // te) * te
    dsts, tails, pad_end = _moe_plan([r[7] for r in routed], n_chunks, te, seg_rows // RUN_ALIGN)
    n_blk = slots // te
    blk_pos = jnp.arange(n_blk, dtype=I32) * te
    blk_e = jnp.minimum(jnp.sum((pad_end[None, :] <= blk_pos[:, None]).astype(I32), axis=1), N_EXP - 1)
    n_used = (pad_end[-1] // te).astype(I32).reshape(1)
    assert len(set(r[3] for r in routed)) == 1, "all passes must use the same MoE tile"
    xs = _dispatch([r[4] for r in routed], [r[5] for r in routed], jnp.concatenate(dsts, axis=0), tails, n_used,
                   slots, routed[0][3], te)
    ys = _experts(xs, blk_e, n_used, layer, P['w_gu'], P['b_gu'], P['w_down'], P['b_down'])
    outs = []
    for r, dst in zip(routed, dsts):
        x2, pmods, T, tm, _, ldest, gates, _, shape = r
        y = _combine(x2, ldest.T, gates.T, dst, ys, pmods, layer, T, tm, P['ln_g'][layer, 1], P['ln_b'][layer, 1])
        outs.append(y.reshape(shape))
    return outs


def _split3(x):
    hi = x.astype(BF16)
    r1 = x - hi.astype(F32)
    mid = r1.astype(BF16)
    lo = (r1 - mid.astype(F32)).astype(BF16)
    return jnp.concatenate([hi, mid, lo], axis=1)


def _inproj1_body(x_ref, xp_ref, xn_ref, sc_ref, sh_ref, w_ref, wvt_ref, cw_ref, cb_ref, qs_ref, wg2_ref, wgh_ref,
                  bg_ref, qk_ref, vt_ref, o_ref, gcol_ref, grow_ref):
    j = pl.program_id(1)
    n_j = pl.num_programs(1)
    halo = xp_ref.shape[0]
    tm = x_ref.shape[0]
    nh = 4 * H_C

    def adaln(v):
        return v * (1.0 + sc_ref[...]) + sh_ref[...]

    u = adaln(x_ref[...])
    ub = u.astype(BF16)
    u_ext = jnp.concatenate([adaln(xp_ref[...]).astype(BF16), ub, adaln(xn_ref[...]).astype(BF16)], axis=0)
    n_ext = tm + 2 * halo
    row = lax.broadcasted_iota(I32, (n_ext, 1), 0)
    inside = jnp.logical_and(jnp.logical_or(row >= halo, j > 0), jnp.logical_or(row < tm + halo, j < n_j - 1))
    cblk = 256
    n_blk = 2 * D_MODEL // cblk

    def matmuls(b):
        c0 = b * cblk
        zqk = jnp.where(inside, jnp.dot(u_ext, w_ref[:, c0:c0 + cblk], preferred_element_type=F32), 0.0)
        if c0 < D_MODEL:
            vt = lax.dot_general(wvt_ref[c0:c0 + cblk, :], ub, (((1,), (1,)), ((), ())), preferred_element_type=F32)
            vt_ref[c0:c0 + cblk, :] = vt.astype(BF16)
        else:
            o0 = c0 - D_MODEL
            o_ref[:, o0:o0 + cblk] = jnp.dot(ub, w_ref[:, 3 * D_MODEL + o0:3 * D_MODEL + o0 + cblk],
                                             preferred_element_type=F32)
        return zqk

    z_next = matmuls(0)
    for b in range(n_blk):
        zqk = z_next
        if b + 1 < n_blk:
            z_next = matmuls(b + 1)
        cols = slice(b * cblk, (b + 1) * cblk)
        acc = (pltpu.roll(zqk, 2, 0) * cw_ref[0:1, cols] + pltpu.roll(zqk, 1, 0) * cw_ref[1:2, cols]
               + zqk * cw_ref[2:3, cols] + pltpu.roll(zqk, n_ext - 1, 0) * cw_ref[3:4, cols])[halo:tm + halo]
        acc = acc + cb_ref[:, cols]
        qk_ref[:, cols] = (jax.nn.silu(acc) * qs_ref[:, cols]).astype(BF16)
    u_lo = (u - ub.astype(F32)).astype(BF16)
    g2 = jnp.dot(ub, wg2_ref[...], preferred_element_type=F32)
    gc = g2[:, :nh] + g2[:, nh:] + jnp.dot(u_lo, wgh_ref[...], preferred_element_type=F32) + bg_ref[...]
    r_i = lax.broadcasted_iota(I32, (tm, tm), 0)
    c_i = lax.broadcasted_iota(I32, (tm, tm), 1)
    same = (r_i // CHUNK) == (c_i // CHUNK)
    tri_f = jnp.where(jnp.logical_and(same, c_i <= r_i), 1.0, 0.0).astype(BF16)
    tri_b = jnp.where(jnp.logical_and(same, c_i >= r_i), 1.0, 0.0).astype(BF16)
    lf3 = _split3(jax.nn.log_sigmoid(gc))

    def sum3(p, axis):
        if axis == 1:
            return p[:, :nh] + p[:, nh:2 * nh] + p[:, 2 * nh:]
        return p[:nh] + p[nh:2 * nh] + p[2 * nh:]

    col_i = lax.broadcasted_iota(I32, (tm, nh), 1)
    cum = jnp.where(col_i < 2 * H_C,
                    sum3(jnp.dot(tri_f, lf3, preferred_element_type=F32), 1),
                    sum3(jnp.dot(tri_b, lf3, preferred_element_type=F32), 1))
    gcol = jnp.where((col_i // H_C) % 2 == 1, cum, gc)
    gcol_ref[...] = gcol
    eye = jnp.where(r_i == c_i, 1.0, 0.0).astype(BF16)
    gt3 = lax.dot_general(_split3(gcol), eye, (((0,), (0,)), ((), ())), preferred_element_type=F32)
    grow_ref[...] = sum3(gt3, 0)


def _inproj1(x, mods, w_in1, w_gate1, b_gate1, conv_w, conv_b, row_off, row_stride):
    B, T, D = x.shape
    tm = min(TOKEN_TILE, T)
    n_j = T // tm
    nh = 4 * H_C
    halo = 8
    wg_hi = w_gate1.astype(BF16)
    wg_lo = (w_gate1 - wg_hi.astype(F32)).astype(BF16)
    qk_scale = jnp.concatenate([jnp.ones((1, D), F32), jnp.full((1, D), DH_C ** -0.5, F32)], axis=1)
    w_bf = w_in1.astype(BF16)
    tok = lambda w: pl.BlockSpec((None, tm, w), lambda b, j: (b, j, 0))
    full = lambda r, c: pl.BlockSpec((r, c), lambda b, j: (0, 0))
    return pl.pallas_call(
        _inproj1_body,
        grid=(B, n_j),
        in_specs=[tok(D),
                  pl.BlockSpec((None, halo, D), lambda b, j: (b, jnp.maximum(j * (tm // halo) - 1, 0), 0)),
                  pl.BlockSpec((None, halo, D), lambda b, j: (b, jnp.minimum((j + 1) * (tm // halo), T // halo - 1), 0)),
                  _mod_spec(1, 1, row_off, row_stride),
                  _mod_spec(1, 0, row_off, row_stride),
                  full(D, 4 * D), full(D, D), full(CONV_W, 2 * D), full(1, 2 * D), full(1, 2 * D),
                  full(D, 2 * nh), full(D, nh), full(1, nh)],
        out_specs=[tok(2 * D),
                   pl.BlockSpec((None, D, tm), lambda b, j: (b, 0, j)),
                   tok(D), tok(nh),
                   pl.BlockSpec((nh, tm), lambda b, j: (0, b * n_j + j))],
        out_shape=[jax.ShapeDtypeStruct((B, T, 2 * D), BF16),
                   jax.ShapeDtypeStruct((B, D, T), BF16),
                   jax.ShapeDtypeStruct((B, T, D), F32),
                   jax.ShapeDtypeStruct((B, T, nh), F32),
                   jax.ShapeDtypeStruct((nh, B * T), F32)],
        compiler_params=_cparams(("arbitrary", "arbitrary")),
        name="inproj1",
    )(x, x, x, mods, mods, w_bf, w_bf[:, 2 * D:3 * D].T, conv_w, conv_b.reshape(1, 2 * D), qk_scale,
      jnp.concatenate([wg_hi, wg_lo], axis=1), wg_hi, b_gate1.reshape(1, nh))


def _mlstm_body(q_ref, k_ref, vt_ref, gcol_ref, grow_ref, c0_ref, n0_ref, m0_ref,
                h_ref, cfin_ref, nfin_ref, mfin_ref, hb_s, *, T):
    n_c = T // CHUNK
    L = CHUNK

    rb = min(T, 512)
    t_i = lax.broadcasted_iota(I32, (L, L), 0)
    s_i = lax.broadcasted_iota(I32, (L, L), 1)

    cfin_ref[...] = c0_ref[...]
    nfin_ref[...] = n0_ref[...]
    mfin_ref[...] = m0_ref[...]

    def chunk(d, c):
        mask = (s_i <= t_i) if d == 0 else (s_i >= t_i)
        last = L - 1 if d == 0 else 0
        off = pl.multiple_of(c * L, L)
        qc = q_ref[pl.ds(off, L), :]
        kc = k_ref[pl.ds(off, L), :]
        vt = vt_ref[:, pl.ds(off, L)]
        gcol = gcol_ref[pl.ds(off, L), :]
        grow = grow_ref[:, pl.ds(off, L)]
        b_c = gcol[:, 2 * d + 1:2 * d + 2]
        li_r = grow[2 * d:2 * d + 1, :]
        b_r = grow[2 * d + 1:2 * d + 2, :]
        m_prev = mfin_ref[d]
        log_d = jnp.where(mask, b_c - b_r + li_r, -jnp.inf)
        m_inter = b_c + m_prev
        m_t = jnp.maximum(m_inter, jnp.max(log_d, axis=-1, keepdims=True))
        dmat = jnp.exp(log_d - m_t)
        w_inter = jnp.exp(m_inter - m_t)
        c_old = cfin_ref[d]
        s = lax.dot_general(qc, kc, (((1,), (1,)), ((), ())), preferred_element_type=F32) * dmat
        inter = lax.dot_general(qc, c_old.astype(BF16), (((1,), (1,)), ((), ())), preferred_element_type=F32)
        num = w_inter * inter + lax.dot_general(s.astype(BF16), vt, (((1,), (1,)), ((), ())),
                                                preferred_element_type=F32)
        qn = jnp.sum(qc.astype(F32) * nfin_ref[d], axis=-1, keepdims=True)
        den = w_inter * qn + jnp.sum(s, axis=-1, keepdims=True)
        h = num / jnp.maximum(jnp.abs(den), jnp.exp(-m_t))
        m_new = m_t[last:last + 1, :]
        b_last = b_c[last:last + 1, :]
        w_s = jnp.exp(b_last - b_r + li_r - m_new)
        decay = jnp.exp(b_last + m_prev - m_new)
        wvt = (vt.astype(F32) * w_s).astype(BF16)
        cfin_ref[d] = decay * c_old + jnp.dot(wvt, kc, preferred_element_type=F32)
        wk = jnp.dot(jnp.broadcast_to(w_s, (8, L)).astype(BF16), kc, preferred_element_type=F32)[0:1]
        nfin_ref[d] = decay * nfin_ref[d] + wk
        mfin_ref[d] = m_new
        return off, h

    def both(ci, _):
        off_f, h_f = chunk(0, ci)
        off_b, h_b = chunk(1, n_c - 1 - ci)
        h_ref[pl.ds(off_f, L), :] = h_f
        hb_s[pl.ds(off_b, L), :] = h_b
        return 0

    lax.fori_loop(0, n_c, both, 0)

    def add_bwd(r, _):
        off = pl.multiple_of(r * rb, rb)
        h_ref[pl.ds(off, rb), :] = h_ref[pl.ds(off, rb), :] + hb_s[pl.ds(off, rb), :]
        return 0

    lax.fori_loop(0, T // rb, add_bwd, 0)


def _mlstm(qk, vt, gcol, grow, c0, n0, m0):
    B, _, T = vt.shape
    DH = DH_C
    st = lambda *tail: pl.BlockSpec((None, 2, None) + tail, lambda b, h: (b, 0, h) + (0,) * len(tail))
    return pl.pallas_call(
        functools.partial(_mlstm_body, T=T),
        grid=(B, H_C),
        in_specs=[pl.BlockSpec((None, T, DH), lambda b, h: (b, 0, h)),
                  pl.BlockSpec((None, T, DH), lambda b, h: (b, 0, H_C + h)),
                  pl.BlockSpec((None, DH, T), lambda b, h: (b, h, 0)),
                  pl.BlockSpec((None, None, T, 4), lambda b, h: (b, h, 0, 0)),
                  pl.BlockSpec((None, None, 4, T), lambda b, h: (b, h, 0, 0)),
                  st(DH, DH), st(1, DH), st(1, 1)],
        out_specs=[pl.BlockSpec((None, T, DH), lambda b, h: (b, 0, h)),
                   st(DH, DH), st(1, DH), st(1, 1)],
        out_shape=[jax.ShapeDtypeStruct((B, T, D_MODEL), F32),
                   jax.ShapeDtypeStruct((B, 2, H_C, DH, DH), F32),
                   jax.ShapeDtypeStruct((B, 2, H_C, 1, DH), F32),
                   jax.ShapeDtypeStruct((B, 2, H_C, 1, 1), F32)],
        scratch_shapes=[pltpu.VMEM((T, DH), F32)],
        compiler_params=_cparams(("arbitrary", "arbitrary")),
        name="mlstm",
    )(qk, qk, vt, gcol, grow, c0, n0, m0)


def _outproj1_body(x_ref, h_ref, o_ref, g1_ref, w_ref, lg_ref, lb_ref, out_ref):
    y = jax.nn.sigmoid(o_ref[...]) * h_ref[...]
    out = jnp.dot(y.astype(BF16), w_ref[...], preferred_element_type=F32)
    v = ALPHA_DN * x_ref[...] + g1_ref[...] * out
    out_ref[...] = _layer_norm(v, lg_ref[...], lb_ref[...])


def _outproj1(x, h, o, mods, w_out1, ln_g, ln_b, row_off, row_stride):
    B, T, D = x.shape
    tm = min(TOKEN_TILE, T)
    tok = pl.BlockSpec((None, tm, D), lambda b, j: (b, j, 0))
    vec = pl.BlockSpec((1, D), lambda b, j: (0, 0))
    return pl.pallas_call(
        _outproj1_body,
        grid=(B, T // tm),
        in_specs=[tok, tok, tok, _mod_spec(1, 2, row_off, row_stride),
                  pl.BlockSpec((D, D), lambda b, j: (0, 0)), vec, vec],
        out_specs=tok,
        out_shape=jax.ShapeDtypeStruct((B, T, D), F32),
        compiler_params=_cparams(("arbitrary", "arbitrary")),
        name="outproj1",
    )(x, h, o, mods, w_out1.astype(BF16), ln_g.reshape(1, D), ln_b.reshape(1, D))


def _to_col_major(x):
    B, T, C = x.shape
    rows = T // GRID_W
    return x.reshape(B, rows, GRID_W, C).transpose(0, 2, 1, 3).reshape(B, T, C)


def _to_row_major(x):
    B, T, C = x.shape
    rows = T // GRID_W
    return x.reshape(B, GRID_W, rows, C).transpose(0, 2, 1, 3).reshape(B, T, C)


def _mixer0(x, mods, st, P, s5_mats):
    h_rg, s_re, s_im = st
    xa, ga, ub = _inproj0(x, mods, P['w_in0'][0].astype(BF16))
    h, h_fin = _rglru(xa, P['conv_a_w'][0], P['conv_a_b'][0], P['rg_wa'][0], P['rg_ba'][0],
                      P['rg_wi'][0], P['rg_bi'][0], P['rg_lam'][0], h_rg[:, 0].transpose(1, 0, 2))
    y5, s_fin = _s5(ub, s5_mats, _s5_state_to_lanes(s_re[:, 0], s_im[:, 0]))
    x = _outproj0(x, h, ga, y5, ub, mods, P['s5_d'][0], P['glu_w'][0], P['glu_b'][0], P['w_out0'][0],
                  P['ln_g'][0, 0], P['ln_b'][0, 0])
    new_re, new_im = _s5_lanes_to_state(s_fin)
    return x, (h_fin.transpose(1, 0, 2)[:, None], new_re[:, None], new_im[:, None])


def _mixer1(x, mods, st, P):
    m_c, m_n, m_m = st
    B, T, D = x.shape
    qk, v, o, gcol, grow = _inproj1(x, mods, P['w_in1'][0], P['w_gate1'][0], P['b_gate1'][0],
                                    P['conv_c_w'][0], P['conv_c_b'][0], 0, 1)
    gcol = gcol.reshape(B, T, 2, 2, H_C).transpose(0, 4, 1, 2, 3).reshape(B, H_C, T, 4)
    grow = grow.reshape(2, 2, H_C, B, T).transpose(3, 2, 0, 1, 4).reshape(B, H_C, 4, T)
    h, c_fin, n_fin, m_fin = _mlstm(qk, v, gcol, grow,
                                    m_c[:, 0], m_n[:, 0][:, :, :, None, :], m_m[:, 0][:, :, :, None, None])
    x = _outproj1(x, h, o, mods, P['w_out1'][0], P['ln_g'][1, 0], P['ln_b'][1, 0], 0, 1)
    return x, (c_fin[:, None], n_fin[:, None, :, :, 0, :], m_fin[:, None, :, :, 0, 0])


def _forward(x_prompt, x_sample, c, c_ctx, states, P):
    bp = x_prompt.shape[0]
    bs = x_sample.shape[0]
    rows = 1 + bs
    rpad = -(-rows // 8) * 8
    cv = jnp.concatenate([c_ctx[None, :], c, jnp.zeros((rpad - rows, D_MODEL), F32)], axis=0)
    mods = _modulation(cv, P['w_mod'], P['b_mod'])
    s5_mats = _s5_matrices(P['s5_a_re'][0], P['s5_a_im'][0], P['s5_log_dt'][0], P['s5_b_re'][0], P['s5_b_im'][0],
                           P['s5_c_re'][0], P['s5_c_im'][0])
    zero_state = (jnp.zeros((bp, 1, 2, D_A), F32),
                  jnp.zeros((bp, 1, 2, G_B, P_B), F32),
                  jnp.zeros((bp, 1, 2, G_B, P_B), F32),
                  jnp.zeros((bp, 1, 2, H_C, DH_C, DH_C), F32),
                  jnp.zeros((bp, 1, 2, H_C, DH_C), F32),
                  jnp.zeros((bp, 1, 2, H_C), F32))
    mods_ctx = jnp.broadcast_to(mods[:, :, 0:1], mods.shape[:2] + (bp,) + mods.shape[3:])
    mods_lat = mods[:, :, 1:1 + bs]
    xc, new_even = _mixer0(x_prompt, mods_ctx, zero_state[:3], P, s5_mats)
    xl, _ = _mixer0(x_sample, mods_lat, states[:3], P, s5_mats)
    xc, xl = _moe([(xc, mods_ctx, True), (xl, mods_lat, False)], 0, P)
    xl = _to_col_major(xl)
    xc, new_odd = _mixer1(xc, mods_ctx, zero_state[3:], P)
    xl, _ = _mixer1(xl, mods_lat, states[3:], P)
    xc, xl = _moe([(xc, mods_ctx, True), (xl, mods_lat, False)], 1, P)
    return (xc, _to_row_major(xl)) + tuple(new_even) + tuple(new_odd)


def kernel(x_prompt, x_sample, c, c_ctx, state_rglru, state_s5_re, state_s5_im, state_mlstm_C, state_mlstm_n, state_mlstm_m, w_mod, b_mod, ln_g, ln_b, w_in0, conv_a_w, conv_a_b, rg_wa, rg_ba, rg_wi, rg_bi, rg_lam, s5_a_re, s5_a_im, s5_log_dt, s5_b_re, s5_b_im, s5_c_re, s5_c_im, s5_d, glu_w, glu_b, w_out0, w_in1, w_gate1, b_gate1, conv_c_w, conv_c_b, w_out1, w_router, b_router, w_gu, b_gu, w_down, b_down):
    P = dict(w_mod=w_mod, b_mod=b_mod, ln_g=ln_g, ln_b=ln_b, w_in0=w_in0, conv_a_w=conv_a_w,
             conv_a_b=conv_a_b, rg_wa=rg_wa, rg_ba=rg_ba, rg_wi=rg_wi, rg_bi=rg_bi, rg_lam=rg_lam,
             s5_a_re=s5_a_re, s5_a_im=s5_a_im, s5_log_dt=s5_log_dt, s5_b_re=s5_b_re, s5_b_im=s5_b_im,
             s5_c_re=s5_c_re, s5_c_im=s5_c_im, s5_d=s5_d, glu_w=glu_w, glu_b=glu_b, w_out0=w_out0,
             w_in1=w_in1, w_gate1=w_gate1, b_gate1=b_gate1, conv_c_w=conv_c_w, conv_c_b=conv_c_b,
             w_out1=w_out1, w_router=w_router, b_router=b_router, w_gu=w_gu, b_gu=b_gu,
             w_down=w_down, b_down=b_down)
    states = (state_rglru, state_s5_re, state_s5_im, state_mlstm_C, state_mlstm_n, state_mlstm_m)
    return _forward(x_prompt, x_sample, c, c_ctx, states, P)
```

```python
import functools

import jax
import jax.numpy as jnp
from jax import lax
from jax.experimental import pallas as pl
from jax.experimental.pallas import tpu as pltpu

F32 = jnp.float32
BF16 = jnp.bfloat16
I32 = jnp.int32
HI = lax.Precision.HIGHEST

D_MODEL = 1024
DEPTH = 2
GRID_W = 64
D_A = 512
NB_A = 8
BS_A = D_A // NB_A
CONV_W = 4
LRU_C = 8.0
D_B = 512
S5_GROUP = 16
G_B = D_B // S5_GROUP
P_B = 64
H_C = 4
DH_C = D_MODEL // H_C
CHUNK = 128
N_EXP = 32
TOP_K = 4
D_FF = D_MODEL
SWIGLU_LIMIT = 7.0
SWIGLU_ALPHA = 1.702
ALPHA_DN = (2 * DEPTH) ** 0.25
LN_EPS = 1e-5

S5_L = 16
S5_W = S5_L * S5_GROUP
S5_OCT = 8
TOKEN_TILE = 256
EXPERT_TILE = 512
MOE_TILE = 512
RUN_ALIGN = 16
VMEM_LIMIT = 56 * 1024 * 1024


def _cparams(sem, vmem=VMEM_LIMIT):
    return pltpu.CompilerParams(dimension_semantics=sem, vmem_limit_bytes=vmem)


def _layer_norm(v, g, b):
    mu = jnp.mean(v, axis=-1, keepdims=True)
    c = v - mu
    var = jnp.mean(c * c, axis=-1, keepdims=True)
    return c * lax.rsqrt(var + LN_EPS) * g + b


def _mod_spec(layer, which, row_off, row_stride):
    return pl.BlockSpec((None, None, None, 1, D_MODEL),
                        lambda b, j: (layer, which, row_off + b * row_stride, 0, 0))


def _mod_spec_all(layer, which, nb):
    return pl.BlockSpec((None, None, nb, 1, D_MODEL), lambda i: (layer, which, 0, 0, 0))


def _row_permutation(n_out_major, n_out_minor):
    n = n_out_major * n_out_minor
    r_out = lax.broadcasted_iota(I32, (n, n), 0)
    r_in = lax.broadcasted_iota(I32, (n, n), 1)
    hit = jnp.logical_and(r_out // n_out_minor == r_in % n_out_major, r_out % n_out_minor == r_in // n_out_major)
    return jnp.where(hit, 1.0, 0.0).astype(BF16)


def _mod_body(c_ref, w_ref, b_ref, o_ref):
    s = jax.nn.silu(c_ref[...])
    o_ref[...] = jnp.dot(s, w_ref[...], precision=HI, preferred_element_type=F32) + b_ref[...]


def _modulation(cv, w_mod, b_mod):
    R, D = cv.shape
    L, _, N6 = w_mod.shape
    tn = N6 // 6
    out = pl.pallas_call(
        _mod_body,
        grid=(L, N6 // tn),
        in_specs=[pl.BlockSpec((R, D), lambda l, j: (0, 0)),
                  pl.BlockSpec((None, D, tn), lambda l, j: (l, 0, j)),
                  pl.BlockSpec((None, 1, tn), lambda l, j: (l, 0, j))],
        out_specs=pl.BlockSpec((None, R, tn), lambda l, j: (l, 0, j)),
        out_shape=jax.ShapeDtypeStruct((L, R, N6), F32),
        compiler_params=_cparams(("arbitrary", "arbitrary")),
        name="modulation",
    )(cv, w_mod, b_mod.reshape(L, 1, N6))
    return out.reshape(L, R, 6, D).transpose(0, 2, 1, 3)[:, :, :, None, :]


def _inproj0_body(x_ref, sc_ref, sh_ref, w_ref, xa_ref, ga_ref, ub_ref):
    nb, tq, d = x_ref.shape
    u = x_ref[...] * (1.0 + sc_ref[...]) + sh_ref[...]
    ub = u.reshape(nb * tq, d).astype(BF16)
    ut = jnp.dot(_row_permutation(tq, nb), ub, preferred_element_type=F32).astype(BF16)
    z = jnp.dot(ut, w_ref[...], preferred_element_type=F32).reshape(tq, nb, w_ref.shape[1])
    xa_ref[...] = z[:, :, :D_A]
    ga_ref[...] = z[:, :, D_A:2 * D_A]
    ub_ref[...] = z[:, :, 2 * D_A:]


def _inproj0(x, pmods, w_in0):
    B, T, D = x.shape
    tq = TOKEN_TILE // B
    nz = w_in0.shape[1]
    tmaj = pl.BlockSpec((tq, B, D_A), lambda i: (i, 0, 0))
    return pl.pallas_call(
        _inproj0_body,
        grid=(T // tq,),
        in_specs=[pl.BlockSpec((B, tq, D), lambda i: (0, i, 0)),
                  _mod_spec_all(0, 1, B), _mod_spec_all(0, 0, B),
                  pl.BlockSpec((D, nz), lambda i: (0, 0))],
        out_specs=[tmaj, tmaj, tmaj],
        out_shape=[jax.ShapeDtypeStruct((T, B, D_A), F32)] * 3,
        compiler_params=_cparams(("arbitrary",)),
        name="inproj0",
    )(x, pmods, pmods, w_in0)


def _rglru_body(cur_ref, prev_ref, next_ref, cw_ref, cb_ref, wa_ref, wi_ref, ba_ref, bi_ref, lam_ref, h0_ref,
                h_ref, hfin_ref, ext_s, a_s, b_s, carry_s, *, tt, n_t):
    d = pl.program_id(0)
    j = pl.program_id(1)
    jj = j + d * (n_t - 1 - 2 * j)
    nb = cur_ref.shape[1]
    half = D_A // 2
    ext_s[0:2] = jnp.where(jj == 0, 0.0, prev_ref[...])
    ext_s[2:tt + 2] = cur_ref[...]
    ext_s[tt + 2:tt + 3] = jnp.where(jj == n_t - 1, 0.0, next_ref[...])
    xc = ext_s[0:tt] * cw_ref[0]
    for k in range(1, CONV_W):
        xc = xc + ext_s[k:k + tt] * cw_ref[k]
    xc = xc + cb_ref[...]
    x2 = xc.reshape(tt * nb, D_A)
    xb = x2.astype(BF16)

    def gate(w_ref, bias_ref):
        lo = jnp.dot(xb[:, :half], w_ref[0], preferred_element_type=F32)
        hi = jnp.dot(xb[:, half:], w_ref[1], preferred_element_type=F32)
        return jax.nn.sigmoid(jnp.concatenate([lo, hi], axis=1) + bias_ref[...])

    r = gate(wa_ref, ba_ref)
    i = gate(wi_ref, bi_ref)
    log_a = LRU_C * r * jax.nn.log_sigmoid(lam_ref[...])
    a = jnp.exp(log_a)
    one_minus_a2 = -jnp.tanh(log_a) * (a * a + 1.0)
    a_s[...] = a.reshape(tt, nb, D_A)
    b_s[...] = (jnp.sqrt(one_minus_a2) * (i * x2)).reshape(tt, nb, D_A)

    @pl.when(j == 0)
    def _():
        carry_s[...] = h0_ref[...]

    def step(t, h):
        h = a_s[t] * h + b_s[t]
        h_ref[t] = h
        return h

    @pl.when(d == 0)
    def _():
        carry_s[...] = lax.fori_loop(0, tt, step, carry_s[...], unroll=8)

    @pl.when(d == 1)
    def _():
        carry_s[...] = lax.fori_loop(0, tt, lambda t, h: step(tt - 1 - t, h), carry_s[...], unroll=8)

    hfin_ref[...] = carry_s[...]


def _block_diag_halves(w):
    nd = w.shape[0]
    per = NB_A // 2
    w = w.reshape(nd, 2, per, BS_A, BS_A)
    eye = jnp.eye(per, dtype=w.dtype)
    out = jnp.einsum('dhncz,nm->dhncmz', w, eye)
    return out.reshape(nd, 2, per * BS_A, per * BS_A)


def _rglru(x3, conv_w, conv_b, wa, ba, wi, bi, lam, h0):
    T, B, _ = x3.shape
    tt = min(T, 1024 // B)
    n_t = T // tt

    def cur_map(d, j):
        return (j + d * (n_t - 1 - 2 * j), 0, 0)

    def prev_map(d, j):
        jj = j + d * (n_t - 1 - 2 * j)
        return (jnp.maximum(jj * (tt // 2) - 1, 0), 0, 0)

    def next_map(d, j):
        jj = j + d * (n_t - 1 - 2 * j)
        return (jnp.minimum((jj + 1) * tt, T - 1), 0, 0)

    dir_spec3 = pl.BlockSpec((None, 1, D_A), lambda d, j: (d, 0, 0))
    h, hfin = pl.pallas_call(
        functools.partial(_rglru_body, tt=tt, n_t=n_t),
        grid=(2, n_t),
        in_specs=[pl.BlockSpec((tt, B, D_A), cur_map),
                  pl.BlockSpec((2, B, D_A), prev_map),
                  pl.BlockSpec((1, B, D_A), next_map),
                  pl.BlockSpec((CONV_W, 1, D_A), lambda d, j: (0, 0, 0)),
                  pl.BlockSpec((1, 1, D_A), lambda d, j: (0, 0, 0)),
                  pl.BlockSpec((None, 2, D_A // 2, D_A // 2), lambda d, j: (d, 0, 0, 0)),
                  pl.BlockSpec((None, 2, D_A // 2, D_A // 2), lambda d, j: (d, 0, 0, 0)),
                  dir_spec3, dir_spec3, dir_spec3,
                  pl.BlockSpec((None, B, D_A), lambda d, j: (d, 0, 0))],
        out_specs=[pl.BlockSpec((None, tt, B, D_A), lambda d, j: (d,) + cur_map(d, j)),
                   pl.BlockSpec((None, B, D_A), lambda d, j: (d, 0, 0))],
        out_shape=[jax.ShapeDtypeStruct((2, T, B, D_A), F32),
                   jax.ShapeDtypeStruct((2, B, D_A), F32)],
        scratch_shapes=[pltpu.VMEM((tt + 3, B, D_A), F32),
                        pltpu.VMEM((tt, B, D_A), F32),
                        pltpu.VMEM((tt, B, D_A), F32),
                        pltpu.VMEM((B, D_A), F32)],
        compiler_params=_cparams(("arbitrary", "arbitrary")),
        name="rglru",
    )(x3, x3, x3, conv_w.reshape(CONV_W, 1, D_A), conv_b.reshape(1, 1, D_A),
      _block_diag_halves(wa).astype(BF16), _block_diag_halves(wi).astype(BF16),
      ba.reshape(2, 1, D_A), bi.reshape(2, 1, D_A), lam.reshape(2, 1, D_A), h0)
    return h, hfin


def _s5_matrices(a_re, a_im, log_dt, b_re, b_im, c_re, c_im):
    L = S5_L
    lam = lax.complex(a_re.astype(F32), a_im.astype(F32))
    dt = jnp.exp(log_dt.astype(F32))[..., None]
    ldt = lam * dt
    a_bar = jnp.exp(ldt)
    b_bar = ((a_bar - 1.0) / lam)[..., None] * lax.complex(b_re.astype(F32), b_im.astype(F32))
    cc = lax.complex(c_re.astype(F32), c_im.astype(F32))
    ks = jnp.arange(L + 1, dtype=F32)
    pw = jnp.exp(ldt[:, :, None, :] * ks[None, None, :, None].astype(jnp.complex64))
    kern = jnp.real(jnp.einsum('dgjp,dgkp,dgpi->dgkji', cc, pw[:, :, :L], b_bar, precision=HI))
    s_idx = jnp.arange(L)[:, None]
    t_idx = jnp.arange(L)[None, :]
    lag_f = jnp.clip(t_idx - s_idx, 0, L - 1)
    lag_b = jnp.clip(s_idx - t_idx, 0, L - 1)
    m_f = jnp.where((t_idx >= s_idx)[None, :, :, None, None], kern[0][:, lag_f], 0.0)
    m_b = jnp.where((s_idx >= t_idx)[None, :, :, None, None], kern[1][:, lag_b], 0.0)
    m = (m_f + m_b).transpose(0, 1, 4, 2, 3).reshape(G_B, S5_W, S5_W)
    down = (L - ks[:L])[None, :, None].astype(jnp.complex64)
    pw_down_b = jnp.exp(ldt[1][:, None, :] * down)
    g_f = jnp.exp(ldt[0][:, None, :] * (down - 1.0))[..., None] * b_bar[0][:, None]
    g_b = pw[1][:, :L, :, None] * b_bar[1][:, None]

    def g_cols(x):
        return x.transpose(0, 1, 3, 2).reshape(G_B, S5_W, P_B)

    gs = jnp.concatenate([g_cols(jnp.real(g_f)), g_cols(jnp.real(g_b)),
                          g_cols(jnp.imag(g_f)), g_cols(jnp.imag(g_b))], axis=-1)
    e_f = cc[0][:, None] * pw[0][:, 1:, None, :]
    e_b = cc[1][:, None] * pw_down_b[:, :, None, :]

    def e_rows(x):
        return x.transpose(0, 3, 1, 2).reshape(G_B, P_B, S5_W)

    e = jnp.concatenate([e_rows(jnp.real(e_f)), e_rows(jnp.real(e_b)),
                         -e_rows(jnp.imag(e_f)), -e_rows(jnp.imag(e_b))], axis=1)
    a_l = pw[:, :, L]
    al = jnp.concatenate([jnp.real(a_l[0]), jnp.real(a_l[1]), jnp.imag(a_l[0]), jnp.imag(a_l[1])], axis=-1)
    no = G_B // S5_OCT
    ow = S5_OCT * S5_W
    rows_sgi = lambda x: x.reshape(no, S5_OCT, L, S5_GROUP, S5_W).transpose(0, 2, 1, 3, 4).reshape(no, ow, S5_W)
    src = jnp.stack([rows_sgi(m), rows_sgi(gs), e.reshape(no, ow, S5_W)]).astype(BF16)
    r = jnp.arange(ow, dtype=I32)
    c = jnp.arange(S5_W, dtype=I32)
    grp_sgi = (r // S5_GROUP) % S5_OCT
    grp_gl = r // S5_W
    src_sgi = (r // (S5_OCT * S5_GROUP)) * S5_GROUP + r % S5_GROUP
    src_gl = r % S5_W
    spread = jnp.stack([src_sgi, src_gl, src_sgi])[:, None, :] == c[None, :, None]
    row_grp = jnp.stack([grp_sgi, grp_sgi, grp_gl])[:, :, None]
    col_grp = jnp.stack([grp_sgi, grp_gl, grp_sgi])[:, None, :]
    tr = 512
    out = pl.pallas_call(
        _s5_expand_body,
        grid=(3, no, ow // tr),
        in_specs=[pl.BlockSpec((None, None, tr, S5_W), lambda k, q, i: (k, q, i, 0)),
                  pl.BlockSpec((None, S5_W, ow), lambda k, q, i: (k, 0, 0)),
                  pl.BlockSpec((None, tr, 1), lambda k, q, i: (k, i, 0)),
                  pl.BlockSpec((None, 1, ow), lambda k, q, i: (k, 0, 0))],
        out_specs=pl.BlockSpec((None, None, tr, ow), lambda k, q, i: (k, q, i, 0)),
        out_shape=jax.ShapeDtypeStruct((3, no, ow, ow), BF16),
        compiler_params=_cparams(("arbitrary", "arbitrary", "arbitrary")),
        name="s5_expand",
    )(src, spread.astype(BF16), row_grp, col_grp)
    return out, al.reshape(1, G_B * S5_W)


def _s5_expand_body(src_ref, spread_ref, rg_ref, cg_ref, o_ref):
    wide = jnp.dot(src_ref[...], spread_ref[...], preferred_element_type=F32)
    o_ref[...] = jnp.where(rg_ref[...] == cg_ref[...], wide, 0.0).astype(BF16)


def _s5_fill_lhs(u_ref, lhs_s):
    tc, _, nb, lanes = u_ref.shape
    for s in range(S5_L):
        lhs_s[:, s * lanes:(s + 1) * lanes] = u_ref[:, s].reshape(tc * nb, lanes).astype(BF16)


def _s5_state_body(u_ref, g_ref, f_ref, lhs_s):
    _s5_fill_lhs(u_ref, lhs_s)
    f_ref[...] = jnp.dot(lhs_s[...], g_ref[...], preferred_element_type=F32).reshape(f_ref.shape)


def _s5_scan_body(f_ref, a_ref, s0_ref, sin_ref, sfin_ref, *, n_c):
    nb, width = s0_ref.shape
    hw = 2 * P_B
    n_g = width // S5_W
    a = a_ref[...]
    a_re = [a[:, k * S5_W:k * S5_W + hw] for k in range(n_g)]
    a_im = [a[:, k * S5_W + hw:(k + 1) * S5_W] for k in range(n_g)]
    is_fwd = (lax.broadcasted_iota(I32, (nb, width), 1) % hw) < P_B

    def split(x):
        return tuple(x[:, k * hw:(k + 1) * hw] for k in range(2 * n_g))

    def merge(parts):
        return jnp.concatenate(parts, axis=-1)

    def advance(c, parts):
        f = split(f_ref[c])
        out = []
        for k in range(n_g):
            s_re, s_im = parts[2 * k], parts[2 * k + 1]
            out.append(a_re[k] * s_re - a_im[k] * s_im + f[2 * k])
            out.append(a_re[k] * s_im + a_im[k] * s_re + f[2 * k + 1])
        return tuple(out)

    def fwd(c, parts):
        sin_ref[c] = merge(parts)
        return advance(c, parts)

    init = split(s0_ref[...])
    fin_f = lax.fori_loop(0, n_c, fwd, init)

    def bwd(k, parts):
        c = n_c - 1 - k
        sin_ref[c] = jnp.where(is_fwd, sin_ref[c], merge(parts))
        return advance(c, parts)

    fin_b = lax.fori_loop(0, n_c, bwd, init)
    sfin_ref[...] = jnp.where(is_fwd, merge(fin_f), merge(fin_b))


def _s5_out_body(u_ref, sin_ref, m_ref, e_ref, y_ref, lhs_s):
    _s5_fill_lhs(u_ref, lhs_s)
    tc, _, nb, lanes = u_ref.shape
    sin = sin_ref[...].reshape(tc * nb, sin_ref.shape[-1]).astype(BF16)
    y = (jnp.dot(lhs_s[...], m_ref[...], preferred_element_type=F32)
         + jnp.dot(sin, e_ref[...], preferred_element_type=F32))
    for s in range(S5_L):
        y_ref[:, s] = y[:, s * lanes:(s + 1) * lanes].reshape(tc, nb, lanes)


def _s5(ub, mats, s0):
    mge, al = mats
    T, B, _ = ub.shape
    n_c = T // S5_L
    no = G_B // S5_OCT
    lanes = S5_OCT * S5_GROUP
    ow = S5_OCT * S5_W
    tc = min(n_c, TOKEN_TILE // B)
    u4 = ub.reshape(n_c, S5_L, B, D_B)
    u_spec = pl.BlockSpec((tc, S5_L, B, lanes), lambda q, i: (i, 0, 0, q))
    w_spec = lambda kind: pl.BlockSpec((None, None, ow, ow), lambda q, i: (kind, q, 0, 0))
    st_spec = pl.BlockSpec((tc, B, ow), lambda q, i: (i, 0, q))
    f_loc = pl.pallas_call(
        _s5_state_body,
        grid=(no, n_c // tc),
        in_specs=[u_spec, w_spec(1)],
        out_specs=st_spec,
        out_shape=jax.ShapeDtypeStruct((n_c, B, G_B * S5_W), F32),
        scratch_shapes=[pltpu.VMEM((tc * B, S5_L * lanes), BF16)],
        compiler_params=_cparams(("arbitrary", "arbitrary")),
        name="s5_state",
    )(u4, mge)
    sw = 4 * S5_W
    sin, sfin = pl.pallas_call(
        functools.partial(_s5_scan_body, n_c=n_c),
        grid=(G_B * S5_W // sw,),
        in_specs=[pl.BlockSpec((n_c, B, sw), lambda g: (0, 0, g)),
                  pl.BlockSpec((1, sw), lambda g: (0, g)),
                  pl.BlockSpec((B, sw), lambda g: (0, g))],
        out_specs=[pl.BlockSpec((n_c, B, sw), lambda g: (0, 0, g)),
                   pl.BlockSpec((B, sw), lambda g: (0, g))],
        out_shape=[jax.ShapeDtypeStruct((n_c, B, G_B * S5_W), F32),
                   jax.ShapeDtypeStruct((B, G_B * S5_W), F32)],
        compiler_params=_cparams(("arbitrary",)),
        name="s5_scan",
    )(f_loc, al, s0)
    y = pl.pallas_call(
        _s5_out_body,
        grid=(no, n_c // tc),
        in_specs=[u_spec, st_spec, w_spec(0), w_spec(2)],
        out_specs=u_spec,
        out_shape=jax.ShapeDtypeStruct((n_c, S5_L, B, D_B), F32),
        scratch_shapes=[pltpu.VMEM((tc * B, S5_L * lanes), BF16)],
        compiler_params=_cparams(("arbitrary", "arbitrary")),
        name="s5_out",
    )(u4, sin, mge, mge)
    return y.reshape(T, B, D_B), sfin


def _s5_state_to_lanes(s_re, s_im):
    parts = [s_re[:, 0], s_re[:, 1], s_im[:, 0], s_im[:, 1]]
    return jnp.concatenate(parts, axis=-1).reshape(s_re.shape[0], G_B * S5_W)


def _s5_lanes_to_state(s):
    s = s.reshape(s.shape[0], G_B, 4, P_B)
    return jnp.stack([s[:, :, 0], s[:, :, 1]], axis=1), jnp.stack([s[:, :, 2], s[:, :, 3]], axis=1)


def _outproj0_body(x_ref, h_ref, ga_ref, y_ref, ub_ref, g1_ref, d_ref, gw_ref, gb_ref, wo_ref,
                   lg_ref, lb_ref, o_ref):
    nb, tq, d = x_ref.shape
    rows = tq * nb

    def flat(v):
        return v.reshape(rows, v.shape[-1])

    ya = flat(h_ref[0] + h_ref[1]) * jax.nn.gelu(flat(ga_ref[...]))
    yb = flat(y_ref[...]) + d_ref[...] * flat(ub_ref[...])
    g = jax.nn.gelu(yb)
    gate = jax.nn.sigmoid(jnp.dot(g.astype(BF16), gw_ref[...], preferred_element_type=F32) + gb_ref[...])
    cat = jnp.concatenate([ya, g * gate], axis=1).astype(BF16)
    cat = jnp.dot(_row_permutation(nb, tq), cat, preferred_element_type=F32).astype(BF16)
    out = jnp.dot(cat, wo_ref[...], preferred_element_type=F32).reshape(nb, tq, d)
    v = ALPHA_DN * x_ref[...] + g1_ref[...] * out
    o_ref[...] = _layer_norm(v, lg_ref[...], lb_ref[...])


def _outproj0(x, h, ga, y5, ub, pmods, s5_d, glu_w, glu_b, w_out0, ln_g, ln_b):
    B, T, D = x.shape
    tq = TOKEN_TILE // B
    tmaj = pl.BlockSpec((tq, B, D_A), lambda i: (i, 0, 0))
    vec = lambda w: pl.BlockSpec((1, w), lambda i: (0, 0))
    return pl.pallas_call(
        _outproj0_body,
        grid=(T // tq,),
        in_specs=[pl.BlockSpec((B, tq, D), lambda i: (0, i, 0)),
                  pl.BlockSpec((2, tq, B, D_A), lambda i: (0, i, 0, 0)),
                  tmaj, tmaj, tmaj,
                  _mod_spec_all(0, 2, B),
                  vec(D_B),
                  pl.BlockSpec((D_B, D_B), lambda i: (0, 0)),
                  vec(D_B),
                  pl.BlockSpec((D, D), lambda i: (0, 0)),
                  vec(D), vec(D)],
        out_specs=pl.BlockSpec((B, tq, D), lambda i: (0, i, 0)),
        out_shape=jax.ShapeDtypeStruct((B, T, D), F32),
        compiler_params=_cparams(("arbitrary",)),
        name="outproj0",
    )(x, h, ga, y5, ub, pmods, s5_d.reshape(1, D_B), glu_w.astype(BF16), glu_b.reshape(1, D_B),
      w_out0.astype(BF16), ln_g.reshape(1, D), ln_b.reshape(1, D))


def _router_body(x_ref, sc_ref, sh_ref, wr_ref, br_ref, u_ref, ldest_ref, gate_ref, cnt_ref):
    u = x_ref[...] * (1.0 + sc_ref[...]) + sh_ref[...]
    ub = u.astype(BF16)
    u_ref[...] = ub
    tm = u.shape[0]
    u_lo = (u - ub.astype(F32)).astype(BF16)
    nt = (((1,), (1,)), ((), ()))
    both = lax.dot_general(wr_ref[...], ub, nt, preferred_element_type=F32)
    logits = (both[:N_EXP] + both[N_EXP:] + lax.dot_general(wr_ref[:N_EXP], u_lo, nt, preferred_element_type=F32)
              + br_ref[...])
    e_iota = lax.broadcasted_iota(I32, logits.shape, 0)
    work = logits
    vals, hots = [], []
    for _ in range(TOP_K):
        m = jnp.max(work, axis=0, keepdims=True)
        idx = jnp.min(jnp.where(work == m, e_iota, N_EXP), axis=0, keepdims=True)
        hot = e_iota == idx
        vals.append(m)
        hots.append(hot)
        work = jnp.where(hot, -jnp.inf, work)
    ex = [jnp.exp(v - vals[0]) for v in vals]
    den = ex[0] + ex[1] + ex[2] + ex[3]
    gate_ref[...] = jnp.concatenate([e / den for e in ex], axis=0)
    hot_sum = jnp.zeros(logits.shape, F32)
    for hot in hots:
        hot_sum = hot_sum + hot.astype(F32)
    hot_b = hot_sum.astype(BF16)
    before = lax.broadcasted_iota(I32, (tm, tm), 0) < lax.broadcasted_iota(I32, (tm, tm), 1)
    excl = jnp.dot(hot_b, jnp.where(before, 1.0, 0.0).astype(BF16), preferred_element_type=F32)
    cnt_row = lax.dot_general(jnp.ones((8, tm), BF16), hot_b, (((1,), (1,)), ((), ())),
                              preferred_element_type=F32)[0:1]
    cnt_ref[...] = cnt_row.astype(I32)
    run_len = jnp.ceil(cnt_row * (1.0 / RUN_ALIGN)) * RUN_ALIGN
    lower = lax.broadcasted_iota(I32, (N_EXP, N_EXP), 1) < lax.broadcasted_iota(I32, (N_EXP, N_EXP), 0)
    run_off = jnp.sum(jnp.where(lower, run_len, 0.0), axis=1, keepdims=True)
    base = excl + run_off
    rows = [jnp.sum(jnp.where(hot, base, 0.0), axis=0, keepdims=True) for hot in hots]
    ldest_ref[...] = jnp.concatenate(rows, axis=0).astype(I32)


def _router(x2, pmods, layer, rows_per_mod, tm, w_router, b_router):
    N, D = x2.shape
    n_t = N // tm
    wr_hi = w_router.T.astype(BF16)
    wr_lo = (w_router.T - wr_hi.astype(F32)).astype(BF16)
    mod = lambda which: pl.BlockSpec((None, None, None, 1, D),
                                     lambda i: (layer, which, (i * tm) // rows_per_mod, 0, 0))
    lane_spec = pl.BlockSpec((TOP_K, tm), lambda i: (0, i))
    return pl.pallas_call(
        _router_body,
        grid=(n_t,),
        in_specs=[pl.BlockSpec((tm, D), lambda i: (i, 0)),
                  mod(4), mod(3),
                  pl.BlockSpec((2 * N_EXP, D), lambda i: (0, 0)),
                  pl.BlockSpec((N_EXP, 1), lambda i: (0, 0))],
        out_specs=[pl.BlockSpec((tm, D), lambda i: (i, 0)),
                   lane_spec, lane_spec,
                   pl.BlockSpec((None, 1, N_EXP), lambda i: (i, 0, 0))],
        out_shape=[jax.ShapeDtypeStruct((N, D), BF16),
                   jax.ShapeDtypeStruct((TOP_K, N), I32),
                   jax.ShapeDtypeStruct((TOP_K, N), F32),
                   jax.ShapeDtypeStruct((n_t, 1, N_EXP), I32)],
        compiler_params=_cparams(("arbitrary",)),
        name="router",
    )(x2, pmods, pmods, jnp.concatenate([wr_hi, wr_lo], axis=0), b_router.reshape(N_EXP, 1))


def _expert_body(blk_e_ref, n_used_ref, xs_ref, wgu_ref, bgu_ref, wdn_ref, bdn_ref, o_ref, wgu_s, wdn_s):
    i = pl.program_id(0)
    prev = blk_e_ref[jnp.maximum(i - 1, 0)]
    changed = jnp.logical_or(i == 0, blk_e_ref[i] != prev)

    @pl.when(changed)
    def _():
        wgu_s[...] = wgu_ref[...].astype(BF16)
        wdn_s[...] = wdn_ref[...].astype(BF16)

    @pl.when(i < n_used_ref[0])
    def _():
        h = jnp.dot(xs_ref[...], wgu_s[...], preferred_element_type=F32) + bgu_ref[...]
        gt = jnp.minimum(h[:, :D_FF], SWIGLU_LIMIT)
        up = jnp.clip(h[:, D_FF:], -SWIGLU_LIMIT, SWIGLU_LIMIT)
        act = (up + 1.0) * gt * jax.nn.sigmoid(SWIGLU_ALPHA * gt)
        y = jnp.dot(act.astype(BF16), wdn_s[...], preferred_element_type=F32) + bdn_ref[...]
        o_ref[...] = y.astype(o_ref.dtype)

    @pl.when(i >= n_used_ref[0])
    def _():
        o_ref[...] = jnp.zeros_like(o_ref)


def _experts(xs, blk_e, n_used, layer, w_gu, b_gu, w_dn, b_dn):
    slots, D = xs.shape
    te = EXPERT_TILE
    n_blk = slots // te
    grid_spec = pltpu.PrefetchScalarGridSpec(
        num_scalar_prefetch=2,
        grid=(n_blk,),
        in_specs=[pl.BlockSpec((te, D), lambda i, be, nu: (jnp.minimum(i, nu[0] - 1), 0)),
                  pl.BlockSpec((None, None, D, 2 * D_FF), lambda i, be, nu: (layer, be[i], 0, 0)),
                  pl.BlockSpec((None, None, 1, 2 * D_FF), lambda i, be, nu: (layer, be[i], 0, 0)),
                  pl.BlockSpec((None, None, D_FF, D), lambda i, be, nu: (layer, be[i], 0, 0)),
                  pl.BlockSpec((None, None, 1, D), lambda i, be, nu: (layer, be[i], 0, 0))],
        out_specs=pl.BlockSpec((te, D), lambda i, be, nu: (i, 0)),
        scratch_shapes=[pltpu.VMEM((D, 2 * D_FF), BF16), pltpu.VMEM((D_FF, D), BF16)],
    )
    return pl.pallas_call(
        _expert_body,
        grid_spec=grid_spec,
        out_shape=jax.ShapeDtypeStruct((slots, D), BF16),
        compiler_params=_cparams(("arbitrary",)),
        name="experts",
    )(blk_e, n_used, xs, w_gu, b_gu.reshape(DEPTH, N_EXP, 1, 2 * D_FF), w_dn, b_dn.reshape(DEPTH, N_EXP, 1, D))


def _chunk_copy(vmem_ref, hbm_ref, sem, k, dst_chunk, to_hbm):
    row = k * RUN_ALIGN if isinstance(k, int) else pl.multiple_of(k * RUN_ALIGN, RUN_ALIGN)
    local = vmem_ref.at[pl.ds(row, RUN_ALIGN)]
    remote = hbm_ref.at[pl.ds(pl.multiple_of(dst_chunk * RUN_ALIGN, RUN_ALIGN), RUN_ALIGN)]
    return pltpu.make_async_copy(local, remote, sem) if to_hbm else pltpu.make_async_copy(remote, local, sem)


def _dispatch_body(dst_ref, tail_ref, nu_ref, *rest, n_chunks, tiles, n_blk):
    n_p = len(tiles)
    xs_ref, buf_s, zero_s, sem = rest[2 * n_p:]
    i = pl.program_id(0)
    n_t = pl.num_programs(0)
    slot = i % 2
    lb = n_chunks * RUN_ALIGN
    te = zero_s.shape[0]
    rb = 256

    def sort_rows(u_ref, ld_ref):
        tm = u_ref.shape[0]
        ld = ld_ref[...]
        u = u_ref[...]
        for j in range(lb // rb):
            r = lax.broadcasted_iota(I32, (rb, tm), 0) + j * rb
            p = jnp.where(r == ld[0:1], 1.0, jnp.where(r == ld[1:2], 1.0, jnp.where(
                r == ld[2:3], 1.0, jnp.where(r == ld[3:4], 1.0, 0.0))))
            buf_s[slot, j * rb:(j + 1) * rb, :] = jnp.dot(p.astype(BF16), u,
                                                          preferred_element_type=F32).astype(BF16)

    for p, (first, count) in enumerate(tiles):
        @pl.when(jnp.logical_and(i >= first, i < first + count))
        def _(p=p):
            sort_rows(rest[2 * p], rest[2 * p + 1])

    def start(k, _):
        _chunk_copy(buf_s.at[slot], xs_ref, sem.at[slot], k, dst_ref[i, k], True).start()
        return 0

    lax.fori_loop(0, n_chunks, start, 0, unroll=8)

    def drain(s):
        pltpu.make_async_copy(buf_s.at[s], xs_ref.at[pl.ds(0, lb)], sem.at[s]).wait()

    @pl.when(i > 0)
    def _():
        drain(1 - slot)

    @pl.when(i == n_t - 1)
    def _():
        drain(slot)
        zero_s[...] = jnp.zeros_like(zero_s)
        n_tail = tail_ref.shape[0]

        def tail_copy(t):
            return _chunk_copy(zero_s, xs_ref, sem.at[2], 0, tail_ref[t], True)

        def block_copy(k):
            return pltpu.make_async_copy(zero_s, xs_ref.at[pl.ds(pl.multiple_of(k * te, te), te)], sem.at[2])

        def each(n, pred, copy, wait):
            def body(t, _):
                @pl.when(pred(t))
                def _():
                    copy(t).wait() if wait else copy(t).start()
                return 0
            lax.fori_loop(0, n, body, 0)

        for wait in (False, True):
            each(n_tail, lambda t: tail_ref[t] >= 0, tail_copy, wait)
            each(n_blk, lambda k: k >= nu_ref[0], block_copy, wait)


def _dispatch(u2s, ldests, dst, tails, n_used, slots, tm, te):
    D = u2s[0].shape[1]
    n_t, n_chunks = dst.shape
    lb = n_chunks * RUN_ALIGN
    tiles, first = [], 0
    for u2 in u2s:
        tiles.append((first, u2.shape[0] // tm))
        first += u2.shape[0] // tm
    assert first == n_t

    def own_tile(first, count):
        return lambda i, d, t, nu: jnp.clip(i - first, 0, count - 1)

    in_specs, operands = [], []
    for (first, count), u2, ld in zip(tiles, u2s, ldests):
        tile = own_tile(first, count)
        in_specs += [pl.BlockSpec((tm, D), lambda i, d, t, nu, tile=tile: (tile(i, d, t, nu), 0)),
                     pl.BlockSpec((TOP_K, tm), lambda i, d, t, nu, tile=tile: (0, tile(i, d, t, nu)))]
        operands += [u2, ld]
    grid_spec = pltpu.PrefetchScalarGridSpec(
        num_scalar_prefetch=3,
        grid=(n_t,),
        in_specs=in_specs,
        out_specs=pl.BlockSpec(memory_space=pl.ANY),
        scratch_shapes=[pltpu.VMEM((2, lb, D), BF16), pltpu.VMEM((te, D), BF16), pltpu.SemaphoreType.DMA((3,))],
    )
    return pl.pallas_call(
        functools.partial(_dispatch_body, n_chunks=n_chunks, tiles=tuple(tiles), n_blk=slots // te),
        grid_spec=grid_spec,
        out_shape=jax.ShapeDtypeStruct((slots, D), BF16),
        compiler_params=_cparams(("arbitrary",)),
        name="moe_dispatch",
    )(dst, tails, n_used, *operands)


def _combine_body(dst_ref, x_ref, ld_ref, gate_ref, g2_ref, lg_ref, lb_ref, ys_ref, o_ref, buf_s, sem, *, n_chunks):
    i = pl.program_id(0)
    n_t = pl.num_programs(0)
    slot = i % 2
    tm = x_ref.shape[0]
    lb = n_chunks * RUN_ALIGN

    def fetch(step, s):
        def body(k, _):
            _chunk_copy(buf_s.at[s], ys_ref, sem.at[s], k, dst_ref[step, k], False).start()
            return 0
        lax.fori_loop(0, n_chunks, body, 0, unroll=8)

    @pl.when(i == 0)
    def _():
        fetch(0, 0)

    @pl.when(i + 1 < n_t)
    def _():
        fetch(i + 1, 1 - slot)

    pltpu.make_async_copy(ys_ref.at[pl.ds(0, lb)], buf_s.at[slot], sem.at[slot]).wait()
    ld = ld_ref[...]
    gates = gate_ref[...]
    cb = 512
    y = jnp.zeros((tm, x_ref.shape[1]), F32)
    for j in range(lb // cb):
        c = lax.broadcasted_iota(I32, (tm, cb), 1) + j * cb
        w = jnp.zeros((tm, cb), F32)
        for k in range(TOP_K):
            w = jnp.where(c == ld[:, k:k + 1], gates[:, k:k + 1], w)
        y = y + jnp.dot(w.astype(BF16), buf_s[slot, j * cb:(j + 1) * cb, :], preferred_element_type=F32)
    v = ALPHA_DN * x_ref[...] + g2_ref[...] * y
    o_ref[...] = _layer_norm(v, lg_ref[...], lb_ref[...])


def _combine(x2, ldest_col, gates_col, dst, ys, pmods, layer, rows_per_mod, tm, ln_g, ln_b):
    N, D = x2.shape
    n_t, n_chunks = dst.shape
    lb = n_chunks * RUN_ALIGN
    vec = pl.BlockSpec((1, D), lambda i, d: (0, 0))
    grid_spec = pltpu.PrefetchScalarGridSpec(
        num_scalar_prefetch=1,
        grid=(n_t,),
        in_specs=[pl.BlockSpec((tm, D), lambda i, d: (i, 0)),
                  pl.BlockSpec((tm, TOP_K), lambda i, d: (i, 0)),
                  pl.BlockSpec((tm, TOP_K), lambda i, d: (i, 0)),
                  pl.BlockSpec((None, None, None, 1, D), lambda i, d: (layer, 5, (i * tm) // rows_per_mod, 0, 0)),
                  vec, vec,
                  pl.BlockSpec(memory_space=pl.ANY)],
        out_specs=pl.BlockSpec((tm, D), lambda i, d: (i, 0)),
        scratch_shapes=[pltpu.VMEM((2, lb, D), BF16), pltpu.SemaphoreType.DMA((2,))],
    )
    return pl.pallas_call(
        functools.partial(_combine_body, n_chunks=n_chunks),
        grid_spec=grid_spec,
        out_shape=jax.ShapeDtypeStruct((N, D), F32),
        compiler_params=_cparams(("arbitrary",)),
        name="moe_combine",
    )(dst, x2, ldest_col, gates_col, pmods, ln_g.reshape(1, D), ln_b.reshape(1, D), ys)


def _moe_plan(cnts, n_chunks, te, spare_chunk):
    a = RUN_ALIGN
    cnt = jnp.concatenate(cnts, axis=0)
    pc = (cnt + a - 1) // a * a
    seg = (jnp.sum(pc, axis=0) + te - 1) // te * te
    pad_end = jnp.cumsum(seg)
    run_start = (pad_end - seg)[None, :] + jnp.cumsum(pc, axis=0) - pc
    lo = jnp.cumsum(pc, axis=1) - pc
    tables, t0 = [], 0
    for c, nc in zip(cnts, n_chunks):
        sl = slice(t0, t0 + c.shape[0])
        parity = ((t0 + jnp.arange(c.shape[0], dtype=I32)) % 2)[:, None]
        t0 += c.shape[0]
        pos = jnp.arange(nc, dtype=I32) * a
        owner = jnp.sum(((lo[sl] + pc[sl])[:, None, :] <= pos[None, :, None]).astype(I32), axis=-1)
        mine = owner[..., None] == jnp.arange(N_EXP, dtype=I32)
        base = jnp.sum(jnp.where(mine, (run_start[sl] - lo[sl])[:, None, :], 0), axis=-1)
        valid = pos[None, :] < jnp.sum(pc[sl], axis=1, keepdims=True)
        spare = spare_chunk + parity * nc + jnp.arange(nc, dtype=I32)[None, :]
        tables.append(jnp.where(valid, (base + pos[None, :]) // a, spare).astype(I32))
    tail_row = (pad_end - seg + jnp.sum(pc, axis=0))[:, None] + jnp.arange(te // a, dtype=I32)[None, :] * a
    tails = jnp.where(tail_row < pad_end[:, None], tail_row // a, -1).astype(I32).reshape(-1)
    return tables, tails, pad_end


def _moe(passes, layer, P):
    te = EXPERT_TILE
    routed = []
    for x, pmods, shared_mod in passes:
        B, T, D = x.shape
        N = B * T
        tm = min(MOE_TILE, N if shared_mod else T)
        x2 = x.reshape(N, D)
        u2, ldest, gates, cnt = _router(x2, pmods, layer, T, tm, P['w_router'][layer], P['b_router'][layer])
        routed.append((x2, pmods, T, tm, u2, ldest, gates, cnt[:, 0, :], x.shape))
    n_chunks = [r[3] * TOP_K // RUN_ALIGN + N_EXP for r in routed]
    n_tiles = [r[7].shape[0] for r in routed]
    n_assign = sum(r[0].shape[0] for r in routed) * TOP_K
    seg_rows = -(-(n_assign + sum(n_tiles) * N_EXP * RUN_ALIGN + N_EXP * te) // te) * te
    slots = seg_rows + -(-(2 * max(n_chunks) * RUN_ALIGN) // te) * te
    dsts, tails, pad_end = _moe_plan([r[7] for r in routed], n_chunks, te, seg_rows // RUN_ALIGN)
    n_blk = slots // te
    blk_pos = jnp.arange(n_blk, dtype=I32) * te
    blk_e = jnp.minimum(jnp.sum((pad_end[None, :] <= blk_pos[:, None]).astype(I32), axis=1), N_EXP - 1)
    n_used = (pad_end[-1] // te).astype(I32).reshape(1)
    assert len(set(r[3] for r in routed)) == 1, "all passes must use the same MoE tile"
    xs = _dispatch([r[4] for r in routed], [r[5] for r in routed], jnp.concatenate(dsts, axis=0), tails, n_used,
                   slots, routed[0][3], te)
    ys = _experts(xs, blk_e, n_used, layer, P['w_gu'], P['b_gu'], P['w_down'], P['b_down'])
    outs = []
    for r, dst in zip(routed, dsts):
        x2, pmods, T, tm, _, ldest, gates, _, shape = r
        y = _combine(x2, ldest.T, gates.T, dst, ys, pmods, layer, T, tm, P['ln_g'][layer, 1], P['ln_b'][layer, 1])
        outs.append(y.reshape(shape))
    return outs


def _split3(x):
    hi = x.astype(BF16)
    r1 = x - hi.astype(F32)
    mid = r1.astype(BF16)
    lo = (r1 - mid.astype(F32)).astype(BF16)
    return jnp.concatenate([hi, mid, lo], axis=1)


def _inproj1_body(x_ref, xp_ref, xn_ref, sc_ref, sh_ref, w_ref, wvt_ref, cw_ref, cb_ref, qs_ref, wg2_ref, wgh_ref,
                  bg_ref, qk_ref, vt_ref, o_ref, gcol_ref, grow_ref):
    j = pl.program_id(1)
    n_j = pl.num_programs(1)
    halo = xp_ref.shape[0]
    tm = x_ref.shape[0]
    nh = 4 * H_C

    def adaln(v):
        return v * (1.0 + sc_ref[...]) + sh_ref[...]

    u = adaln(x_ref[...])
    ub = u.astype(BF16)
    u_ext = jnp.concatenate([adaln(xp_ref[...]).astype(BF16), ub, adaln(xn_ref[...]).astype(BF16)], axis=0)
    n_ext = tm + 2 * halo
    row = lax.broadcasted_iota(I32, (n_ext, 1), 0)
    inside = jnp.logical_and(jnp.logical_or(row >= halo, j > 0), jnp.logical_or(row < tm + halo, j < n_j - 1))
    cblk = 256
    n_blk = 2 * D_MODEL // cblk

    def matmuls(b):
        c0 = b * cblk
        zqk = jnp.where(inside, jnp.dot(u_ext, w_ref[:, c0:c0 + cblk], preferred_element_type=F32), 0.0)
        if c0 < D_MODEL:
            vt = lax.dot_general(wvt_ref[c0:c0 + cblk, :], ub, (((1,), (1,)), ((), ())), preferred_element_type=F32)
            vt_ref[c0:c0 + cblk, :] = vt.astype(BF16)
        else:
            o0 = c0 - D_MODEL
            o_ref[:, o0:o0 + cblk] = jnp.dot(ub, w_ref[:, 3 * D_MODEL + o0:3 * D_MODEL + o0 + cblk],
                                             preferred_element_type=F32)
        return zqk

    z_next = matmuls(0)
    for b in range(n_blk):
        zqk = z_next
        if b + 1 < n_blk:
            z_next = matmuls(b + 1)
        cols = slice(b * cblk, (b + 1) * cblk)
        acc = (pltpu.roll(zqk, 2, 0) * cw_ref[0:1, cols] + pltpu.roll(zqk, 1, 0) * cw_ref[1:2, cols]
               + zqk * cw_ref[2:3, cols] + pltpu.roll(zqk, n_ext - 1, 0) * cw_ref[3:4, cols])[halo:tm + halo]
        acc = acc + cb_ref[:, cols]
        qk_ref[:, cols] = (jax.nn.silu(acc) * qs_ref[:, cols]).astype(BF16)
    u_lo = (u - ub.astype(F32)).astype(BF16)
    g2 = jnp.dot(ub, wg2_ref[...], preferred_element_type=F32)
    gc = g2[:, :nh] + g2[:, nh:] + jnp.dot(u_lo, wgh_ref[...], preferred_element_type=F32) + bg_ref[...]
    r_i = lax.broadcasted_iota(I32, (tm, tm), 0)
    c_i = lax.broadcasted_iota(I32, (tm, tm), 1)
    same = (r_i // CHUNK) == (c_i // CHUNK)
    tri_f = jnp.where(jnp.logical_and(same, c_i <= r_i), 1.0, 0.0).astype(BF16)
    tri_b = jnp.where(jnp.logical_and(same, c_i >= r_i), 1.0, 0.0).astype(BF16)
    lf3 = _split3(jax.nn.log_sigmoid(gc))

    def sum3(p, axis):
        if axis == 1:
            return p[:, :nh] + p[:, nh:2 * nh] + p[:, 2 * nh:]
        return p[:nh] + p[nh:2 * nh] + p[2 * nh:]

    col_i = lax.broadcasted_iota(I32, (tm, nh), 1)
    cum = jnp.where(col_i < 2 * H_C,
                    sum3(jnp.dot(tri_f, lf3, preferred_element_type=F32), 1),
                    sum3(jnp.dot(tri_b, lf3, preferred_element_type=F32), 1))
    gcol = jnp.where((col_i // H_C) % 2 == 1, cum, gc)
    gcol_ref[...] = gcol
    eye = jnp.where(r_i == c_i, 1.0, 0.0).astype(BF16)
    gt3 = lax.dot_general(_split3(gcol), eye, (((0,), (0,)), ((), ())), preferred_element_type=F32)
    grow_ref[...] = sum3(gt3, 0)


def _inproj1(x, mods, w_in1, w_gate1, b_gate1, conv_w, conv_b, row_off, row_stride):
    B, T, D = x.shape
    tm = min(TOKEN_TILE, T)
    n_j = T // tm
    nh = 4 * H_C
    halo = 8
    wg_hi = w_gate1.astype(BF16)
    wg_lo = (w_gate1 - wg_hi.astype(F32)).astype(BF16)
    qk_scale = jnp.concatenate([jnp.ones((1, D), F32), jnp.full((1, D), DH_C ** -0.5, F32)], axis=1)
    w_bf = w_in1.astype(BF16)
    tok = lambda w: pl.BlockSpec((None, tm, w), lambda b, j: (b, j, 0))
    full = lambda r, c: pl.BlockSpec((r, c), lambda b, j: (0, 0))
    return pl.pallas_call(
        _inproj1_body,
        grid=(B, n_j),
        in_specs=[tok(D),
                  pl.BlockSpec((None, halo, D), lambda b, j: (b, jnp.maximum(j * (tm // halo) - 1, 0), 0)),
                  pl.BlockSpec((None, halo, D), lambda b, j: (b, jnp.minimum((j + 1) * (tm // halo), T // halo - 1), 0)),
                  _mod_spec(1, 1, row_off, row_stride),
                  _mod_spec(1, 0, row_off, row_stride),
                  full(D, 4 * D), full(D, D), full(CONV_W, 2 * D), full(1, 2 * D), full(1, 2 * D),
                  full(D, 2 * nh), full(D, nh), full(1, nh)],
        out_specs=[tok(2 * D),
                   pl.BlockSpec((None, D, tm), lambda b, j: (b, 0, j)),
                   tok(D), tok(nh),
                   pl.BlockSpec((nh, tm), lambda b, j: (0, b * n_j + j))],
        out_shape=[jax.ShapeDtypeStruct((B, T, 2 * D), BF16),
                   jax.ShapeDtypeStruct((B, D, T), BF16),
                   jax.ShapeDtypeStruct((B, T, D), F32),
                   jax.ShapeDtypeStruct((B, T, nh), F32),
                   jax.ShapeDtypeStruct((nh, B * T), F32)],
        compiler_params=_cparams(("arbitrary", "arbitrary")),
        name="inproj1",
    )(x, x, x, mods, mods, w_bf, w_bf[:, 2 * D:3 * D].T, conv_w, conv_b.reshape(1, 2 * D), qk_scale,
      jnp.concatenate([wg_hi, wg_lo], axis=1), wg_hi, b_gate1.reshape(1, nh))


def _mlstm_body(q_ref, k_ref, vt_ref, gcol_ref, grow_ref, c0_ref, n0_ref, m0_ref,
                h_ref, cfin_ref, nfin_ref, mfin_ref, hb_s, *, T):
    n_c = T // CHUNK
    L = CHUNK

    rb = min(T, 512)
    t_i = lax.broadcasted_iota(I32, (L, L), 0)
    s_i = lax.broadcasted_iota(I32, (L, L), 1)

    cfin_ref[...] = c0_ref[...]
    nfin_ref[...] = n0_ref[...]
    mfin_ref[...] = m0_ref[...]

    def chunk(d, c):
        mask = (s_i <= t_i) if d == 0 else (s_i >= t_i)
        last = L - 1 if d == 0 else 0
        off = pl.multiple_of(c * L, L)
        qc = q_ref[pl.ds(off, L), :]
        kc = k_ref[pl.ds(off, L), :]
        vt = vt_ref[:, pl.ds(off, L)]
        gcol = gcol_ref[pl.ds(off, L), :]
        grow = grow_ref[:, pl.ds(off, L)]
        b_c = gcol[:, 2 * d + 1:2 * d + 2]
        li_r = grow[2 * d:2 * d + 1, :]
        b_r = grow[2 * d + 1:2 * d + 2, :]
        m_prev = mfin_ref[d]
        log_d = jnp.where(mask, b_c - b_r + li_r, -jnp.inf)
        m_inter = b_c + m_prev
        m_t = jnp.maximum(m_inter, jnp.max(log_d, axis=-1, keepdims=True))
        dmat = jnp.exp(log_d - m_t)
        w_inter = jnp.exp(m_inter - m_t)
        c_old = cfin_ref[d]
        s = lax.dot_general(qc, kc, (((1,), (1,)), ((), ())), preferred_element_type=F32) * dmat
        inter = lax.dot_general(qc, c_old.astype(BF16), (((1,), (1,)), ((), ())), preferred_element_type=F32)
        num = w_inter * inter + lax.dot_general(s.astype(BF16), vt, (((1,), (1,)), ((), ())),
                                                preferred_element_type=F32)
        qn = jnp.sum(qc.astype(F32) * nfin_ref[d], axis=-1, keepdims=True)
        den = w_inter * qn + jnp.sum(s, axis=-1, keepdims=True)
        h = num / jnp.maximum(jnp.abs(den), jnp.exp(-m_t))
        m_new = m_t[last:last + 1, :]
        b_last = b_c[last:last + 1, :]
        w_s = jnp.exp(b_last - b_r + li_r - m_new)
        decay = jnp.exp(b_last + m_prev - m_new)
        wvt = (vt.astype(F32) * w_s).astype(BF16)
        cfin_ref[d] = decay * c_old + jnp.dot(wvt, kc, preferred_element_type=F32)
        wk = jnp.dot(jnp.broadcast_to(w_s, (8, L)).astype(BF16), kc, preferred_element_type=F32)[0:1]
        nfin_ref[d] = decay * nfin_ref[d] + wk
        mfin_ref[d] = m_new
        return off, h

    def both(ci, _):
        off_f, h_f = chunk(0, ci)
        off_b, h_b = chunk(1, n_c - 1 - ci)
        h_ref[pl.ds(off_f, L), :] = h_f
        hb_s[pl.ds(off_b, L), :] = h_b
        return 0

    lax.fori_loop(0, n_c, both, 0)

    def add_bwd(r, _):
        off = pl.multiple_of(r * rb, rb)
        h_ref[pl.ds(off, rb), :] = h_ref[pl.ds(off, rb), :] + hb_s[pl.ds(off, rb), :]
        return 0

    lax.fori_loop(0, T // rb, add_bwd, 0)


def _mlstm(qk, vt, gcol, grow, c0, n0, m0):
    B, _, T = vt.shape
    DH = DH_C
    st = lambda *tail: pl.BlockSpec((None, 2, None) + tail, lambda b, h: (b, 0, h) + (0,) * len(tail))
    return pl.pallas_call(
        functools.partial(_mlstm_body, T=T),
        grid=(B, H_C),
        in_specs=[pl.BlockSpec((None, T, DH), lambda b, h: (b, 0, h)),
                  pl.BlockSpec((None, T, DH), lambda b, h: (b, 0, H_C + h)),
                  pl.BlockSpec((None, DH, T), lambda b, h: (b, h, 0)),
                  pl.BlockSpec((None, None, T, 4), lambda b, h: (b, h, 0, 0)),
                  pl.BlockSpec((None, None, 4, T), lambda b, h: (b, h, 0, 0)),
                  st(DH, DH), st(1, DH), st(1, 1)],
        out_specs=[pl.BlockSpec((None, T, DH), lambda b, h: (b, 0, h)),
                   st(DH, DH), st(1, DH), st(1, 1)],
        out_shape=[jax.ShapeDtypeStruct((B, T, D_MODEL), F32),
                   jax.ShapeDtypeStruct((B, 2, H_C, DH, DH), F32),
                   jax.ShapeDtypeStruct((B, 2, H_C, 1, DH), F32),
                   jax.ShapeDtypeStruct((B, 2, H_C, 1, 1), F32)],
        scratch_shapes=[pltpu.VMEM((T, DH), F32)],
        compiler_params=_cparams(("arbitrary", "arbitrary")),
        name="mlstm",
    )(qk, qk, vt, gcol, grow, c0, n0, m0)


def _outproj1_body(x_ref, h_ref, o_ref, g1_ref, w_ref, lg_ref, lb_ref, out_ref):
    y = jax.nn.sigmoid(o_ref[...]) * h_ref[...]
    out = jnp.dot(y.astype(BF16), w_ref[...], preferred_element_type=F32)
    v = ALPHA_DN * x_ref[...] + g1_ref[...] * out
    out_ref[...] = _layer_norm(v, lg_ref[...], lb_ref[...])


def _outproj1(x, h, o, mods, w_out1, ln_g, ln_b, row_off, row_stride):
    B, T, D = x.shape
    tm = min(TOKEN_TILE, T)
    tok = pl.BlockSpec((None, tm, D), lambda b, j: (b, j, 0))
    vec = pl.BlockSpec((1, D), lambda b, j: (0, 0))
    return pl.pallas_call(
        _outproj1_body,
        grid=(B, T // tm),
        in_specs=[tok, tok, tok, _mod_spec(1, 2, row_off, row_stride),
                  pl.BlockSpec((D, D), lambda b, j: (0, 0)), vec, vec],
        out_specs=tok,
        out_shape=jax.ShapeDtypeStruct((B, T, D), F32),
        compiler_params=_cparams(("arbitrary", "arbitrary")),
        name="outproj1",
    )(x, h, o, mods, w_out1.astype(BF16), ln_g.reshape(1, D), ln_b.reshape(1, D))


def _to_col_major(x):
    B, T, C = x.shape
    rows = T // GRID_W
    return x.reshape(B, rows, GRID_W, C).transpose(0, 2, 1, 3).reshape(B, T, C)


def _to_row_major(x):
    B, T, C = x.shape
    rows = T // GRID_W
    return x.reshape(B, GRID_W, rows, C).transpose(0, 2, 1, 3).reshape(B, T, C)


def _mixer0(x, mods, st, P, s5_mats):
    h_rg, s_re, s_im = st
    xa, ga, ub = _inproj0(x, mods, P['w_in0'][0].astype(BF16))
    h, h_fin = _rglru(xa, P['conv_a_w'][0], P['conv_a_b'][0], P['rg_wa'][0], P['rg_ba'][0],
                      P['rg_wi'][0], P['rg_bi'][0], P['rg_lam'][0], h_rg[:, 0].transpose(1, 0, 2))
    y5, s_fin = _s5(ub, s5_mats, _s5_state_to_lanes(s_re[:, 0], s_im[:, 0]))
    x = _outproj0(x, h, ga, y5, ub, mods, P['s5_d'][0], P['glu_w'][0], P['glu_b'][0], P['w_out0'][0],
                  P['ln_g'][0, 0], P['ln_b'][0, 0])
    new_re, new_im = _s5_lanes_to_state(s_fin)
    return x, (h_fin.transpose(1, 0, 2)[:, None], new_re[:, None], new_im[:, None])


def _mixer1(x, mods, st, P):
    m_c, m_n, m_m = st
    B, T, D = x.shape
    qk, v, o, gcol, grow = _inproj1(x, mods, P['w_in1'][0], P['w_gate1'][0], P['b_gate1'][0],
                                    P['conv_c_w'][0], P['conv_c_b'][0], 0, 1)
    gcol = gcol.reshape(B, T, 2, 2, H_C).transpose(0, 4, 1, 2, 3).reshape(B, H_C, T, 4)
    grow = grow.reshape(2, 2, H_C, B, T).transpose(3, 2, 0, 1, 4).reshape(B, H_C, 4, T)
    h, c_fin, n_fin, m_fin = _mlstm(qk, v, gcol, grow,
                                    m_c[:, 0], m_n[:, 0][:, :, :, None, :], m_m[:, 0][:, :, :, None, None])
    x = _outproj1(x, h, o, mods, P['w_out1'][0], P['ln_g'][1, 0], P['ln_b'][1, 0], 0, 1)
    return x, (c_fin[:, None], n_fin[:, None, :, :, 0, :], m_fin[:, None, :, :, 0, 0])


def _forward(x_prompt, x_sample, c, c_ctx, states, P):
    bp = x_prompt.shape[0]
    bs = x_sample.shape[0]
    rows = 1 + bs
    rpad = -(-rows // 8) * 8
    cv = jnp.concatenate([c_ctx[None, :], c, jnp.zeros((rpad - rows, D_MODEL), F32)], axis=0)
    mods = _modulation(cv, P['w_mod'], P['b_mod'])
    s5_mats = _s5_matrices(P['s5_a_re'][0], P['s5_a_im'][0], P['s5_log_dt'][0], P['s5_b_re'][0], P['s5_b_im'][0],
                           P['s5_c_re'][0], P['s5_c_im'][0])
    zero_state = (jnp.zeros((bp, 1, 2, D_A), F32),
                  jnp.zeros((bp, 1, 2, G_B, P_B), F32),
                  jnp.zeros((bp, 1, 2, G_B, P_B), F32),
                  jnp.zeros((bp, 1, 2, H_C, DH_C, DH_C), F32),
                  jnp.zeros((bp, 1, 2, H_C, DH_C), F32),
                  jnp.zeros((bp, 1, 2, H_C), F32))
    mods_ctx = jnp.broadcast_to(mods[:, :, 0:1], mods.shape[:2] + (bp,) + mods.shape[3:])
    mods_lat = mods[:, :, 1:1 + bs]
    xc, new_even = _mixer0(x_prompt, mods_ctx, zero_state[:3], P, s5_mats)
    xl, _ = _mixer0(x_sample, mods_lat, states[:3], P, s5_mats)
    xc, xl = _moe([(xc, mods_ctx, True), (xl, mods_lat, False)], 0, P)
    xl = _to_col_major(xl)
    xc, new_odd = _mixer1(xc, mods_ctx, zero_state[3:], P)
    xl, _ = _mixer1(xl, mods_lat, states[3:], P)
    xc, xl = _moe([(xc, mods_ctx, True), (xl, mods_lat, False)], 1, P)
    return (xc, _to_row_major(xl)) + tuple(new_even) + tuple(new_odd)


def kernel(x_prompt, x_sample, c, c_ctx, state_rglru, state_s5_re, state_s5_im, state_mlstm_C, state_mlstm_n, state_mlstm_m, w_mod, b_mod, ln_g, ln_b, w_in0, conv_a_w, conv_a_b, rg_wa, rg_ba, rg_wi, rg_bi, rg_lam, s5_a_re, s5_a_im, s5_log_dt, s5_b_re, s5_b_im, s5_c_re, s5_c_im, s5_d, glu_w, glu_b, w_out0, w_in1, w_gate1, b_gate1, conv_c_w, conv_c_b, w_out1, w_router, b_router, w_gu, b_gu, w_down, b_down):
    P = dict(w_mod=w_mod, b_mod=b_mod, ln_g=ln_g, ln_b=ln_b, w_in0=w_in0, conv_a_w=conv_a_w,
             conv_a_b=conv_a_b, rg_wa=rg_wa, rg_ba=rg_ba, rg_wi=rg_wi, rg_bi=rg_bi, rg_lam=rg_lam,
             s5_a_re=s5_a_re, s5_a_im=s5_a_im, s5_log_dt=s5_log_dt, s5_b_re=s5_b_re, s5_b_im=s5_b_im,
             s5_c_re=s5_c_re, s5_c_im=s5_c_im, s5_d=s5_d, glu_w=glu_w, glu_b=glu_b, w_out0=w_out0,
             w_in1=w_in1, w_gate1=w_gate1, b_gate1=b_gate1, conv_c_w=conv_c_w, conv_c_b=conv_c_b,
             w_out1=w_out1, w_router=w_router, b_router=b_router, w_gu=w_gu, b_gu=b_gu,
             w_down=w_down, b_down=b_down)
    states = (state_rglru, state_s5_re, state_s5_im, state_mlstm_C, state_mlstm_n, state_mlstm_m)
    return _forward(x_prompt, x_sample, c, c_ctx, states, P)
```

```python
import functools

import jax
import jax.numpy as jnp
from jax import lax
from jax.experimental import pallas as pl
from jax.experimental.pallas import tpu as pltpu

F32 = jnp.float32
BF16 = jnp.bfloat16
I32 = jnp.int32
HI = lax.Precision.HIGHEST

D_MODEL = 1024
DEPTH = 2
GRID_W = 64
D_A = 512
NB_A = 8
BS_A = D_A // NB_A
CONV_W = 4
LRU_C = 8.0
D_B = 512
S5_GROUP = 16
G_B = D_B // S5_GROUP
P_B = 64
H_C = 4
DH_C = D_MODEL // H_C
CHUNK = 128
N_EXP = 32
TOP_K = 4
D_FF = D_MODEL
SWIGLU_LIMIT = 7.0
SWIGLU_ALPHA = 1.702
ALPHA_DN = (2 * DEPTH) ** 0.25
LN_EPS = 1e-5

S5_L = 16
S5_W = S5_L * S5_GROUP
S5_OCT = 8
TOKEN_TILE = 256
EXPERT_TILE = 512
MOE_TILE = 512
RUN_ALIGN = 16
VMEM_LIMIT = 56 * 1024 * 1024


def _cparams(sem, vmem=VMEM_LIMIT):
    return pltpu.CompilerParams(dimension_semantics=sem, vmem_limit_bytes=vmem)


def _layer_norm(v, g, b):
    mu = jnp.mean(v, axis=-1, keepdims=True)
    c = v - mu
    var = jnp.mean(c * c, axis=-1, keepdims=True)
    return c * lax.rsqrt(var + LN_EPS) * g + b


def _mod_spec(layer, which, row_off, row_stride):
    return pl.BlockSpec((None, None, None, 1, D_MODEL),
                        lambda b, j: (layer, which, row_off + b * row_stride, 0, 0))


def _mod_spec_all(layer, which, nb):
    return pl.BlockSpec((None, None, nb, 1, D_MODEL), lambda i: (layer, which, 0, 0, 0))


def _row_permutation(n_out_major, n_out_minor):
    n = n_out_major * n_out_minor
    r_out = lax.broadcasted_iota(I32, (n, n), 0)
    r_in = lax.broadcasted_iota(I32, (n, n), 1)
    hit = jnp.logical_and(r_out // n_out_minor == r_in % n_out_major, r_out % n_out_minor == r_in // n_out_major)
    return jnp.where(hit, 1.0, 0.0).astype(BF16)


def _mod_body(c_ref, w_ref, b_ref, o_ref):
    s = jax.nn.silu(c_ref[...])
    o_ref[...] = jnp.dot(s, w_ref[...], precision=HI, preferred_element_type=F32) + b_ref[...]


def _modulation(cv, w_mod, b_mod):
    R, D = cv.shape
    L, _, N6 = w_mod.shape
    tn = N6 // 6
    out = pl.pallas_call(
        _mod_body,
        grid=(L, N6 // tn),
        in_specs=[pl.BlockSpec((R, D), lambda l, j: (0, 0)),
                  pl.BlockSpec((None, D, tn), lambda l, j: (l, 0, j)),
                  pl.BlockSpec((None, 1, tn), lambda l, j: (l, 0, j))],
        out_specs=pl.BlockSpec((None, R, tn), lambda l, j: (l, 0, j)),
        out_shape=jax.ShapeDtypeStruct((L, R, N6), F32),
        compiler_params=_cparams(("arbitrary", "arbitrary")),
        name="modulation",
    )(cv, w_mod, b_mod.reshape(L, 1, N6))
    return out.reshape(L, R, 6, D).transpose(0, 2, 1, 3)[:, :, :, None, :]


def _inproj0_body(x_ref, sc_ref, sh_ref, w_ref, xa_ref, ga_ref, ub_ref):
    nb, tq, d = x_ref.shape
    u = x_ref[...] * (1.0 + sc_ref[...]) + sh_ref[...]
    ub = u.reshape(nb * tq, d).astype(BF16)
    ut = jnp.dot(_row_permutation(tq, nb), ub, preferred_element_type=F32).astype(BF16)
    z = jnp.dot(ut, w_ref[...], preferred_element_type=F32).reshape(tq, nb, w_ref.shape[1])
    xa_ref[...] = z[:, :, :D_A]
    ga_ref[...] = z[:, :, D_A:2 * D_A]
    ub_ref[...] = z[:, :, 2 * D_A:]


def _inproj0(x, pmods, w_in0):
    B, T, D = x.shape
    tq = TOKEN_TILE // B
    nz = w_in0.shape[1]
    tmaj = pl.BlockSpec((tq, B, D_A), lambda i: (i, 0, 0))
    return pl.pallas_call(
        _inproj0_body,
        grid=(T // tq,),
        in_specs=[pl.BlockSpec((B, tq, D), lambda i: (0, i, 0)),
                  _mod_spec_all(0, 1, B), _mod_spec_all(0, 0, B),
                  pl.BlockSpec((D, nz), lambda i: (0, 0))],
        out_specs=[tmaj, tmaj, tmaj],
        out_shape=[jax.ShapeDtypeStruct((T, B, D_A), F32)] * 3,
        compiler_params=_cparams(("arbitrary",)),
        name="inproj0",
    )(x, pmods, pmods, w_in0)


def _rglru_body(cur_ref, prev_ref, next_ref, cw_ref, cb_ref, wa_ref, wi_ref, ba_ref, bi_ref, lam_ref, h0_ref,
                h_ref, hfin_ref, ext_s, a_s, b_s, carry_s, *, tt, n_t):
    d = pl.program_id(0)
    j = pl.program_id(1)
    jj = j + d * (n_t - 1 - 2 * j)
    nb = cur_ref.shape[1]
    half = D_A // 2
    ext_s[0:2] = jnp.where(jj == 0, 0.0, prev_ref[...])
    ext_s[2:tt + 2] = cur_ref[...]
    ext_s[tt + 2:tt + 3] = jnp.where(jj == n_t - 1, 0.0, next_ref[...])
    xc = ext_s[0:tt] * cw_ref[0]
    for k in range(1, CONV_W):
        xc = xc + ext_s[k:k + tt] * cw_ref[k]
    xc = xc + cb_ref[...]
    x2 = xc.reshape(tt * nb, D_A)
    xb = x2.astype(BF16)

    def gate(w_ref, bias_ref):
        lo = jnp.dot(xb[:, :half], w_ref[0], preferred_element_type=F32)
        hi = jnp.dot(xb[:, half:], w_ref[1], preferred_element_type=F32)
        return jax.nn.sigmoid(jnp.concatenate([lo, hi], axis=1) + bias_ref[...])

    r = gate(wa_ref, ba_ref)
    i = gate(wi_ref, bi_ref)
    log_a = LRU_C * r * jax.nn.log_sigmoid(lam_ref[...])
    a = jnp.exp(log_a)
    one_minus_a2 = -jnp.tanh(log_a) * (a * a + 1.0)
    a_s[...] = a.reshape(tt, nb, D_A)
    b_s[...] = (jnp.sqrt(one_minus_a2) * (i * x2)).reshape(tt, nb, D_A)

    @pl.when(j == 0)
    def _():
        carry_s[...] = h0_ref[...]

    def step(t, h):
        h = a_s[t] * h + b_s[t]
        h_ref[t] = h
        return h

    @pl.when(d == 0)
    def _():
        carry_s[...] = lax.fori_loop(0, tt, step, carry_s[...], unroll=8)

    @pl.when(d == 1)
    def _():
        carry_s[...] = lax.fori_loop(0, tt, lambda t, h: step(tt - 1 - t, h), carry_s[...], unroll=8)

    hfin_ref[...] = carry_s[...]


def _block_diag_halves(w):
    nd = w.shape[0]
    per = NB_A // 2
    w = w.reshape(nd, 2, per, BS_A, BS_A)
    eye = jnp.eye(per, dtype=w.dtype)
    out = jnp.einsum('dhncz,nm->dhncmz', w, eye)
    return out.reshape(nd, 2, per * BS_A, per * BS_A)


def _rglru(x3, conv_w, conv_b, wa, ba, wi, bi, lam, h0):
    T, B, _ = x3.shape
    tt = min(T, 1024 // B)
    n_t = T // tt

    def cur_map(d, j):
        return (j + d * (n_t - 1 - 2 * j), 0, 0)

    def prev_map(d, j):
        jj = j + d * (n_t - 1 - 2 * j)
        return (jnp.maximum(jj * (tt // 2) - 1, 0), 0, 0)

    def next_map(d, j):
        jj = j + d * (n_t - 1 - 2 * j)
        return (jnp.minimum((jj + 1) * tt, T - 1), 0, 0)

    dir_spec3 = pl.BlockSpec((None, 1, D_A), lambda d, j: (d, 0, 0))
    h, hfin = pl.pallas_call(
        functools.partial(_rglru_body, tt=tt, n_t=n_t),
        grid=(2, n_t),
        in_specs=[pl.BlockSpec((tt, B, D_A), cur_map),
                  pl.BlockSpec((2, B, D_A), prev_map),
                  pl.BlockSpec((1, B, D_A), next_map),
                  pl.BlockSpec((CONV_W, 1, D_A), lambda d, j: (0, 0, 0)),
                  pl.BlockSpec((1, 1, D_A), lambda d, j: (0, 0, 0)),
                  pl.BlockSpec((None, 2, D_A // 2, D_A // 2), lambda d, j: (d, 0, 0, 0)),
                  pl.BlockSpec((None, 2, D_A // 2, D_A // 2), lambda d, j: (d, 0, 0, 0)),
                  dir_spec3, dir_spec3, dir_spec3,
                  pl.BlockSpec((None, B, D_A), lambda d, j: (d, 0, 0))],
        out_specs=[pl.BlockSpec((None, tt, B, D_A), lambda d, j: (d,) + cur_map(d, j)),
                   pl.BlockSpec((None, B, D_A), lambda d, j: (d, 0, 0))],
        out_shape=[jax.ShapeDtypeStruct((2, T, B, D_A), F32),
                   jax.ShapeDtypeStruct((2, B, D_A), F32)],
        scratch_shapes=[pltpu.VMEM((tt + 3, B, D_A), F32),
                        pltpu.VMEM((tt, B, D_A), F32),
                        pltpu.VMEM((tt, B, D_A), F32),
                        pltpu.VMEM((B, D_A), F32)],
        compiler_params=_cparams(("arbitrary", "arbitrary")),
        name="rglru",
    )(x3, x3, x3, conv_w.reshape(CONV_W, 1, D_A), conv_b.reshape(1, 1, D_A),
      _block_diag_halves(wa).astype(BF16), _block_diag_halves(wi).astype(BF16),
      ba.reshape(2, 1, D_A), bi.reshape(2, 1, D_A), lam.reshape(2, 1, D_A), h0)
    return h, hfin


def _s5_matrices(a_re, a_im, log_dt, b_re, b_im, c_re, c_im):
    L = S5_L
    lam = lax.complex(a_re.astype(F32), a_im.astype(F32))
    dt = jnp.exp(log_dt.astype(F32))[..., None]
    ldt = lam * dt
    a_bar = jnp.exp(ldt)
    b_bar = ((a_bar - 1.0) / lam)[..., None] * lax.complex(b_re.astype(F32), b_im.astype(F32))
    cc = lax.complex(c_re.astype(F32), c_im.astype(F32))
    ks = jnp.arange(L + 1, dtype=F32)
    pw = jnp.exp(ldt[:, :, None, :] * ks[None, None, :, None].astype(jnp.complex64))
    kern = jnp.real(jnp.einsum('dgjp,dgkp,dgpi->dgkji', cc, pw[:, :, :L], b_bar, precision=HI))
    s_idx = jnp.arange(L)[:, None]
    t_idx = jnp.arange(L)[None, :]
    lag_f = jnp.clip(t_idx - s_idx, 0, L - 1)
    lag_b = jnp.clip(s_idx - t_idx, 0, L - 1)
    m_f = jnp.where((t_idx >= s_idx)[None, :, :, None, None], kern[0][:, lag_f], 0.0)
    m_b = jnp.where((s_idx >= t_idx)[None, :, :, None, None], kern[1][:, lag_b], 0.0)
    m = (m_f + m_b).transpose(0, 1, 4, 2, 3).reshape(G_B, S5_W, S5_W)
    down = (L - ks[:L])[None, :, None].astype(jnp.complex64)
    pw_down_b = jnp.exp(ldt[1][:, None, :] * down)
    g_f = jnp.exp(ldt[0][:, None, :] * (down - 1.0))[..., None] * b_bar[0][:, None]
    g_b = pw[1][:, :L, :, None] * b_bar[1][:, None]

    def g_cols(x):
        return x.transpose(0, 1, 3, 2).reshape(G_B, S5_W, P_B)

    gs = jnp.concatenate([g_cols(jnp.real(g_f)), g_cols(jnp.real(g_b)),
                          g_cols(jnp.imag(g_f)), g_cols(jnp.imag(g_b))], axis=-1)
    e_f = cc[0][:, None] * pw[0][:, 1:, None, :]
    e_b = cc[1][:, None] * pw_down_b[:, :, None, :]

    def e_rows(x):
        return x.transpose(0, 3, 1, 2).reshape(G_B, P_B, S5_W)

    e = jnp.concatenate([e_rows(jnp.real(e_f)), e_rows(jnp.real(e_b)),
                         -e_rows(jnp.imag(e_f)), -e_rows(jnp.imag(e_b))], axis=1)
    a_l = pw[:, :, L]
    al = jnp.concatenate([jnp.real(a_l[0]), jnp.real(a_l[1]), jnp.imag(a_l[0]), jnp.imag(a_l[1])], axis=-1)
    no = G_B // S5_OCT
    ow = S5_OCT * S5_W
    rows_sgi = lambda x: x.reshape(no, S5_OCT, L, S5_GROUP, S5_W).transpose(0, 2, 1, 3, 4).reshape(no, ow, S5_W)
    src = jnp.stack([rows_sgi(m), rows_sgi(gs), e.reshape(no, ow, S5_W)]).astype(BF16)
    r = jnp.arange(ow, dtype=I32)
    c = jnp.arange(S5_W, dtype=I32)
    grp_sgi = (r // S5_GROUP) % S5_OCT
    grp_gl = r // S5_W
    src_sgi = (r // (S5_OCT * S5_GROUP)) * S5_GROUP + r % S5_GROUP
    src_gl = r % S5_W
    spread = jnp.stack([src_sgi, src_gl, src_sgi])[:, None, :] == c[None, :, None]
    row_grp = jnp.stack([grp_sgi, grp_sgi, grp_gl])[:, :, None]
    col_grp = jnp.stack([grp_sgi, grp_gl, grp_sgi])[:, None, :]
    tr = 512
    out = pl.pallas_call(
        _s5_expand_body,
        grid=(3, no, ow // tr),
        in_specs=[pl.BlockSpec((None, None, tr, S5_W), lambda k, q, i: (k, q, i, 0)),
                  pl.BlockSpec((None, S5_W, ow), lambda k, q, i: (k, 0, 0)),
                  pl.BlockSpec((None, tr, 1), lambda k, q, i: (k, i, 0)),
                  pl.BlockSpec((None, 1, ow), lambda k, q, i: (k, 0, 0))],
        out_specs=pl.BlockSpec((None, None, tr, ow), lambda k, q, i: (k, q, i, 0)),
        out_shape=jax.ShapeDtypeStruct((3, no, ow, ow), BF16),
        compiler_params=_cparams(("arbitrary", "arbitrary", "arbitrary")),
        name="s5_expand",
    )(src, spread.astype(BF16), row_grp, col_grp)
    return out, al.reshape(1, G_B * S5_W)


def _s5_expand_body(src_ref, spread_ref, rg_ref, cg_ref, o_ref):
    wide = jnp.dot(src_ref[...], spread_ref[...], preferred_element_type=F32)
    o_ref[...] = jnp.where(rg_ref[...] == cg_ref[...], wide, 0.0).astype(BF16)


def _s5_fill_lhs(u_ref, lhs_s):
    tc, _, nb, lanes = u_ref.shape
    for s in range(S5_L):
        lhs_s[:, s * lanes:(s + 1) * lanes] = u_ref[:, s].reshape(tc * nb, lanes).astype(BF16)


def _s5_state_body(u_ref, g_ref, f_ref, lhs_s):
    _s5_fill_lhs(u_ref, lhs_s)
    f_ref[...] = jnp.dot(lhs_s[...], g_ref[...], preferred_element_type=F32).reshape(f_ref.shape)


def _s5_scan_body(f_ref, a_ref, s0_ref, sin_ref, sfin_ref, *, n_c):
    nb, width = s0_ref.shape
    hw = 2 * P_B
    n_g = width // S5_W
    a = a_ref[...]
    a_re = [a[:, k * S5_W:k * S5_W + hw] for k in range(n_g)]
    a_im = [a[:, k * S5_W + hw:(k + 1) * S5_W] for k in range(n_g)]
    is_fwd = (lax.broadcasted_iota(I32, (nb, width), 1) % hw) < P_B

    def split(x):
        return tuple(x[:, k * hw:(k + 1) * hw] for k in range(2 * n_g))

    def merge(parts):
        return jnp.concatenate(parts, axis=-1)

    def advance(c, parts):
        f = split(f_ref[c])
        out = []
        for k in range(n_g):
            s_re, s_im = parts[2 * k], parts[2 * k + 1]
            out.append(a_re[k] * s_re - a_im[k] * s_im + f[2 * k])
            out.append(a_re[k] * s_im + a_im[k] * s_re + f[2 * k + 1])
        return tuple(out)

    def fwd(c, parts):
        sin_ref[c] = merge(parts)
        return advance(c, parts)

    init = split(s0_ref[...])
    fin_f = lax.fori_loop(0, n_c, fwd, init)

    def bwd(k, parts):
        c = n_c - 1 - k
        sin_ref[c] = jnp.where(is_fwd, sin_ref[c], merge(parts))
        return advance(c, parts)

    fin_b = lax.fori_loop(0, n_c, bwd, init)
    sfin_ref[...] = jnp.where(is_fwd, merge(fin_f), merge(fin_b))


def _s5_out_body(u_ref, sin_ref, m_ref, e_ref, y_ref, lhs_s):
    _s5_fill_lhs(u_ref, lhs_s)
    tc, _, nb, lanes = u_ref.shape
    sin = sin_ref[...].reshape(tc * nb, sin_ref.shape[-1]).astype(BF16)
    y = (jnp.dot(lhs_s[...], m_ref[...], preferred_element_type=F32)
         + jnp.dot(sin, e_ref[...], preferred_element_type=F32))
    for s in range(S5_L):
        y_ref[:, s] = y[:, s * lanes:(s + 1) * lanes].reshape(tc, nb, lanes)


def _s5(ub, mats, s0):
    mge, al = mats
    T, B, _ = ub.shape
    n_c = T // S5_L
    no = G_B // S5_OCT
    lanes = S5_OCT * S5_GROUP
    ow = S5_OCT * S5_W
    tc = min(n_c, TOKEN_TILE // B)
    u4 = ub.reshape(n_c, S5_L, B, D_B)
    u_spec = pl.BlockSpec((tc, S5_L, B, lanes), lambda q, i: (i, 0, 0, q))
    w_spec = lambda kind: pl.BlockSpec((None, None, ow, ow), lambda q, i: (kind, q, 0, 0))
    st_spec = pl.BlockSpec((tc, B, ow), lambda q, i: (i, 0, q))
    f_loc = pl.pallas_call(
        _s5_state_body,
        grid=(no, n_c // tc),
        in_specs=[u_spec, w_spec(1)],
        out_specs=st_spec,
        out_shape=jax.ShapeDtypeStruct((n_c, B, G_B * S5_W), F32),
        scratch_shapes=[pltpu.VMEM((tc * B, S5_L * lanes), BF16)],
        compiler_params=_cparams(("arbitrary", "arbitrary")),
        name="s5_state",
    )(u4, mge)
    sw = 4 * S5_W
    sin, sfin = pl.pallas_call(
        functools.partial(_s5_scan_body, n_c=n_c),
        grid=(G_B * S5_W // sw,),
        in_specs=[pl.BlockSpec((n_c, B, sw), lambda g: (0, 0, g)),
                  pl.BlockSpec((1, sw), lambda g: (0, g)),
                  pl.BlockSpec((B, sw), lambda g: (0, g))],
        out_specs=[pl.BlockSpec((n_c, B, sw), lambda g: (0, 0, g)),
                   pl.BlockSpec((B, sw), lambda g: (0, g))],
        out_shape=[jax.ShapeDtypeStruct((n_c, B, G_B * S5_W), F32),
                   jax.ShapeDtypeStruct((B, G_B * S5_W), F32)],
        compiler_params=_cparams(("arbitrary",)),
        name="s5_scan",
    )(f_loc, al, s0)
    y = pl.pallas_call(
        _s5_out_body,
        grid=(no, n_c // tc),
        in_specs=[u_spec, st_spec, w_spec(0), w_spec(2)],
        out_specs=u_spec,
        out_shape=jax.ShapeDtypeStruct((n_c, S5_L, B, D_B), F32),
        scratch_shapes=[pltpu.VMEM((tc * B, S5_L * lanes), BF16)],
        compiler_params=_cparams(("arbitrary", "arbitrary")),
        name="s5_out",
    )(u4, sin, mge, mge)
    return y.reshape(T, B, D_B), sfin


def _s5_state_to_lanes(s_re, s_im):
    parts = [s_re[:, 0], s_re[:, 1], s_im[:, 0], s_im[:, 1]]
    return jnp.concatenate(parts, axis=-1).reshape(s_re.shape[0], G_B * S5_W)


def _s5_lanes_to_state(s):
    s = s.reshape(s.shape[0], G_B, 4, P_B)
    return jnp.stack([s[:, :, 0], s[:, :, 1]], axis=1), jnp.stack([s[:, :, 2], s[:, :, 3]], axis=1)


def _outproj0_body(x_ref, h_ref, ga_ref, y_ref, ub_ref, g1_ref, d_ref, gw_ref, gb_ref, wo_ref,
                   lg_ref, lb_ref, o_ref):
    nb, tq, d = x_ref.shape
    rows = tq * nb

    def flat(v):
        return v.reshape(rows, v.shape[-1])

    ya = flat(h_ref[0] + h_ref[1]) * jax.nn.gelu(flat(ga_ref[...]))
    yb = flat(y_ref[...]) + d_ref[...] * flat(ub_ref[...])
    g = jax.nn.gelu(yb)
    gate = jax.nn.sigmoid(jnp.dot(g.astype(BF16), gw_ref[...], preferred_element_type=F32) + gb_ref[...])
    cat = jnp.concatenate([ya, g * gate], axis=1).astype(BF16)
    cat = jnp.dot(_row_permutation(nb, tq), cat, preferred_element_type=F32).astype(BF16)
    out = jnp.dot(cat, wo_ref[...], preferred_element_type=F32).reshape(nb, tq, d)
    v = ALPHA_DN * x_ref[...] + g1_ref[...] * out
    o_ref[...] = _layer_norm(v, lg_ref[...], lb_ref[...])


def _outproj0(x, h, ga, y5, ub, pmods, s5_d, glu_w, glu_b, w_out0, ln_g, ln_b):
    B, T, D = x.shape
    tq = TOKEN_TILE // B
    tmaj = pl.BlockSpec((tq, B, D_A), lambda i: (i, 0, 0))
    vec = lambda w: pl.BlockSpec((1, w), lambda i: (0, 0))
    return pl.pallas_call(
        _outproj0_body,
        grid=(T // tq,),
        in_specs=[pl.BlockSpec((B, tq, D), lambda i: (0, i, 0)),
                  pl.BlockSpec((2, tq, B, D_A), lambda i: (0, i, 0, 0)),
                  tmaj, tmaj, tmaj,
                  _mod_spec_all(0, 2, B),
                  vec(D_B),
                  pl.BlockSpec((D_B, D_B), lambda i: (0, 0)),
                  vec(D_B),
                  pl.BlockSpec((D, D), lambda i: (0, 0)),
                  vec(D), vec(D)],
        out_specs=pl.BlockSpec((B, tq, D), lambda i: (0, i, 0)),
        out_shape=jax.ShapeDtypeStruct((B, T, D), F32),
        compiler_params=_cparams(("arbitrary",)),
        name="outproj0",
    )(x, h, ga, y5, ub, pmods, s5_d.reshape(1, D_B), glu_w.astype(BF16), glu_b.reshape(1, D_B),
      w_out0.astype(BF16), ln_g.reshape(1, D), ln_b.reshape(1, D))


def _router_body(x_ref, sc_ref, sh_ref, wr_ref, br_ref, u_ref, ldest_ref, gate_ref, cnt_ref):
    u = x_ref[...] * (1.0 + sc_ref[...]) + sh_ref[...]
    ub = u.astype(BF16)
    u_ref[...] = ub
    tm = u.shape[0]
    u_lo = (u - ub.astype(F32)).astype(BF16)
    nt = (((1,), (1,)), ((), ()))
    both = lax.dot_general(wr_ref[...], ub, nt, preferred_element_type=F32)
    logits = (both[:N_EXP] + both[N_EXP:] + lax.dot_general(wr_ref[:N_EXP], u_lo, nt, preferred_element_type=F32)
              + br_ref[...])
    e_iota = lax.broadcasted_iota(I32, logits.shape, 0)
    work = logits
    vals, hots = [], []
    for _ in range(TOP_K):
        m = jnp.max(work, axis=0, keepdims=True)
        idx = jnp.min(jnp.where(work == m, e_iota, N_EXP), axis=0, keepdims=True)
        hot = e_iota == idx
        vals.append(m)
        hots.append(hot)
        work = jnp.where(hot, -jnp.inf, work)
    ex = [jnp.exp(v - vals[0]) for v in vals]
    den = ex[0] + ex[1] + ex[2] + ex[3]
    gate_ref[...] = jnp.concatenate([e / den for e in ex], axis=0)
    hot_sum = jnp.zeros(logits.shape, F32)
    for hot in hots:
        hot_sum = hot_sum + hot.astype(F32)
    hot_b = hot_sum.astype(BF16)
    before = lax.broadcasted_iota(I32, (tm, tm), 0) < lax.broadcasted_iota(I32, (tm, tm), 1)
    excl = jnp.dot(hot_b, jnp.where(before, 1.0, 0.0).astype(BF16), preferred_element_type=F32)
    cnt_row = lax.dot_general(jnp.ones((8, tm), BF16), hot_b, (((1,), (1,)), ((), ())),
                              preferred_element_type=F32)[0:1]
    cnt_ref[...] = cnt_row.astype(I32)
    run_len = jnp.ceil(cnt_row * (1.0 / RUN_ALIGN)) * RUN_ALIGN
    lower = lax.broadcasted_iota(I32, (N_EXP, N_EXP), 1) < lax.broadcasted_iota(I32, (N_EXP, N_EXP), 0)
    run_off = jnp.sum(jnp.where(lower, run_len, 0.0), axis=1, keepdims=True)
    base = excl + run_off
    rows = [jnp.sum(jnp.where(hot, base, 0.0), axis=0, keepdims=True) for hot in hots]
    ldest_ref[...] = jnp.concatenate(rows, axis=0).astype(I32)


def _router(x2, pmods, layer, rows_per_mod, tm, w_router, b_router):
    N, D = x2.shape
    n_t = N // tm
    wr_hi = w_router.T.astype(BF16)
    wr_lo = (w_router.T - wr_hi.astype(F32)).astype(BF16)
    mod = lambda which: pl.BlockSpec((None, None, None, 1, D),
                                     lambda i: (layer, which, (i * tm) // rows_per_mod, 0, 0))
    lane_spec = pl.BlockSpec((TOP_K, tm), lambda i: (0, i))
    return pl.pallas_call(
        _router_body,
        grid=(n_t,),
        in_specs=[pl.BlockSpec((tm, D), lambda i: (i, 0)),
                  mod(4), mod(3),
                  pl.BlockSpec((2 * N_EXP, D), lambda i: (0, 0)),
                  pl.BlockSpec((N_EXP, 1), lambda i: (0, 0))],
        out_specs=[pl.BlockSpec((tm, D), lambda i: (i, 0)),
                   lane_spec, lane_spec,
                   pl.BlockSpec((None, 1, N_EXP), lambda i: (i, 0, 0))],
        out_shape=[jax.ShapeDtypeStruct((N, D), BF16),
                   jax.ShapeDtypeStruct((TOP_K, N), I32),
                   jax.ShapeDtypeStruct((TOP_K, N), F32),
                   jax.ShapeDtypeStruct((n_t, 1, N_EXP), I32)],
        compiler_params=_cparams(("arbitrary",)),
        name="router",
    )(x2, pmods, pmods, jnp.concatenate([wr_hi, wr_lo], axis=0), b_router.reshape(N_EXP, 1))


def _expert_body(blk_e_ref, n_used_ref, xs_ref, wgu_ref, bgu_ref, wdn_ref, bdn_ref, o_ref, wgu_s, wdn_s):
    i = pl.program_id(0)
    prev = blk_e_ref[jnp.maximum(i - 1, 0)]
    changed = jnp.logical_or(i == 0, blk_e_ref[i] != prev)

    @pl.when(changed)
    def _():
        wgu_s[...] = wgu_ref[...].astype(BF16)
        wdn_s[...] = wdn_ref[...].astype(BF16)

    @pl.when(i < n_used_ref[0])
    def _():
        h = jnp.dot(xs_ref[...], wgu_s[...], preferred_element_type=F32) + bgu_ref[...]
        gt = jnp.minimum(h[:, :D_FF], SWIGLU_LIMIT)
        up = jnp.clip(h[:, D_FF:], -SWIGLU_LIMIT, SWIGLU_LIMIT)
        act = (up + 1.0) * gt * jax.nn.sigmoid(SWIGLU_ALPHA * gt)
        y = jnp.dot(act.astype(BF16), wdn_s[...], preferred_element_type=F32) + bdn_ref[...]
        o_ref[...] = y.astype(o_ref.dtype)

    @pl.when(i >= n_used_ref[0])
    def _():
        o_ref[...] = jnp.zeros_like(o_ref)


def _experts(xs, blk_e, n_used, layer, w_gu, b_gu, w_dn, b_dn):
    slots, D = xs.shape
    te = EXPERT_TILE
    n_blk = slots // te
    grid_spec = pltpu.PrefetchScalarGridSpec(
        num_scalar_prefetch=2,
        grid=(n_blk,),
        in_specs=[pl.BlockSpec((te, D), lambda i, be, nu: (jnp.minimum(i, nu[0] - 1), 0)),
                  pl.BlockSpec((None, None, D, 2 * D_FF), lambda i, be, nu: (layer, be[i], 0, 0)),
                  pl.BlockSpec((None, None, 1, 2 * D_FF), lambda i, be, nu: (layer, be[i], 0, 0)),
                  pl.BlockSpec((None, None, D_FF, D), lambda i, be, nu: (layer, be[i], 0, 0)),
                  pl.BlockSpec((None, None, 1, D), lambda i, be, nu: (layer, be[i], 0, 0))],
        out_specs=pl.BlockSpec((te, D), lambda i, be, nu: (i, 0)),
        scratch_shapes=[pltpu.VMEM((D, 2 * D_FF), BF16), pltpu.VMEM((D_FF, D), BF16)],
    )
    return pl.pallas_call(
        _expert_body,
        grid_spec=grid_spec,
        out_shape=jax.ShapeDtypeStruct((slots, D), BF16),
        compiler_params=_cparams(("arbitrary",)),
        name="experts",
    )(blk_e, n_used, xs, w_gu, b_gu.reshape(DEPTH, N_EXP, 1, 2 * D_FF), w_dn, b_dn.reshape(DEPTH, N_EXP, 1, D))


def _chunk_copy(vmem_ref, hbm_ref, sem, k, dst_chunk, to_hbm):
    row = k * RUN_ALIGN if isinstance(k, int) else pl.multiple_of(k * RUN_ALIGN, RUN_ALIGN)
    local = vmem_ref.at[pl.ds(row, RUN_ALIGN)]
    remote = hbm_ref.at[pl.ds(pl.multiple_of(dst_chunk * RUN_ALIGN, RUN_ALIGN), RUN_ALIGN)]
    return pltpu.make_async_copy(local, remote, sem) if to_hbm else pltpu.make_async_copy(remote, local, sem)


def _dispatch_body(dst_ref, tail_ref, nu_ref, *rest, n_chunks, tiles, n_blk):
    n_p = len(tiles)
    xs_ref, buf_s, zero_s, sem = rest[2 * n_p:]
    i = pl.program_id(0)
    n_t = pl.num_programs(0)
    slot = i % 2
    lb = n_chunks * RUN_ALIGN
    te = zero_s.shape[0]
    rb = 256

    def sort_rows(u_ref, ld_ref):
        tm = u_ref.shape[0]
        ld = ld_ref[...]
        u = u_ref[...]
        for j in range(lb // rb):
            r = lax.broadcasted_iota(I32, (rb, tm), 0) + j * rb
            p = jnp.where(r == ld[0:1], 1.0, jnp.where(r == ld[1:2], 1.0, jnp.where(
                r == ld[2:3], 1.0, jnp.where(r == ld[3:4], 1.0, 0.0))))
            buf_s[slot, j * rb:(j + 1) * rb, :] = jnp.dot(p.astype(BF16), u,
                                                          preferred_element_type=F32).astype(BF16)

    for p, (first, count) in enumerate(tiles):
        @pl.when(jnp.logical_and(i >= first, i < first + count))
        def _(p=p):
            sort_rows(rest[2 * p], rest[2 * p + 1])

    def start(k, _):
        _chunk_copy(buf_s.at[slot], xs_ref, sem.at[slot], k, dst_ref[i, k], True).start()
        return 0

    lax.fori_loop(0, n_chunks, start, 0, unroll=8)

    def drain(s):
        pltpu.make_async_copy(buf_s.at[s], xs_ref.at[pl.ds(0, lb)], sem.at[s]).wait()

    @pl.when(i > 0)
    def _():
        drain(1 - slot)

    @pl.when(i == n_t - 1)
    def _():
        drain(slot)
        zero_s[...] = jnp.zeros_like(zero_s)
        n_tail = tail_ref.shape[0]

        def tail_copy(t):
            return _chunk_copy(zero_s, xs_ref, sem.at[2], 0, tail_ref[t], True)

        def block_copy(k):
            return pltpu.make_async_copy(zero_s, xs_ref.at[pl.ds(pl.multiple_of(k * te, te), te)], sem.at[2])

        def each(n, pred, copy, wait):
            def body(t, _):
                @pl.when(pred(t))
                def _():
                    copy(t).wait() if wait else copy(t).start()
                return 0
            lax.fori_loop(0, n, body, 0)

        for wait in (False, True):
            each(n_tail, lambda t: tail_ref[t] >= 0, tail_copy, wait)
            each(n_blk, lambda k: k >= nu_ref[0], block_copy, wait)


def _dispatch(u2s, ldests, dst, tails, n_used, slots, tm, te):
    D = u2s[0].shape[1]
    n_t, n_chunks = dst.shape
    lb = n_chunks * RUN_ALIGN
    tiles, first = [], 0
    for u2 in u2s:
        tiles.append((first, u2.shape[0] // tm))
        first += u2.shape[0] // tm
    assert first == n_t

    def own_tile(first, count):
        return lambda i, d, t, nu: jnp.clip(i - first, 0, count - 1)

    in_specs, operands = [], []
    for (first, count), u2, ld in zip(tiles, u2s, ldests):
        tile = own_tile(first, count)
        in_specs += [pl.BlockSpec((tm, D), lambda i, d, t, nu, tile=tile: (tile(i, d, t, nu), 0)),
                     pl.BlockSpec((TOP_K, tm), lambda i, d, t, nu, tile=tile: (0, tile(i, d, t, nu)))]
        operands += [u2, ld]
    grid_spec = pltpu.PrefetchScalarGridSpec(
        num_scalar_prefetch=3,
        grid=(n_t,),
        in_specs=in_specs,
        out_specs=pl.BlockSpec(memory_space=pl.ANY),
        scratch_shapes=[pltpu.VMEM((2, lb, D), BF16), pltpu.VMEM((te, D), BF16), pltpu.SemaphoreType.DMA((3,))],
    )
    return pl.pallas_call(
        functools.partial(_dispatch_body, n_chunks=n_chunks, tiles=tuple(tiles), n_blk=slots // te),
        grid_spec=grid_spec,
        out_shape=jax.ShapeDtypeStruct((slots, D), BF16),
        compiler_params=_cparams(("arbitrary",)),
        name="moe_dispatch",
    )(dst, tails, n_used, *operands)


def _combine_body(dst_ref, x_ref, ld_ref, gate_ref, g2_ref, lg_ref, lb_ref, ys_ref, o_ref, buf_s, sem, *, n_chunks):
    i = pl.program_id(0)
    n_t = pl.num_programs(0)
    slot = i % 2
    tm = x_ref.shape[0]
    lb = n_chunks * RUN_ALIGN

    def fetch(step, s):
        def body(k, _):
            _chunk_copy(buf_s.at[s], ys_ref, sem.at[s], k, dst_ref[step, k], False).start()
            return 0
        lax.fori_loop(0, n_chunks, body, 0, unroll=8)

    @pl.when(i == 0)
    def _():
        fetch(0, 0)

    @pl.when(i + 1 < n_t)
    def _():
        fetch(i + 1, 1 - slot)

    pltpu.make_async_copy(ys_ref.at[pl.ds(0, lb)], buf_s.at[slot], sem.at[slot]).wait()
    ld = ld_ref[...]
    gates = gate_ref[...]
    cb = 512
    y = jnp.zeros((tm, x_ref.shape[1]), F32)
    for j in range(lb // cb):
        c = lax.broadcasted_iota(I32, (tm, cb), 1) + j * cb
        w = jnp.zeros((tm, cb), F32)
        for k in range(TOP_K):
            w = jnp.where(c == ld[:, k:k + 1], gates[:, k:k + 1], w)
        y = y + jnp.dot(w.astype(BF16), buf_s[slot, j * cb:(j + 1) * cb, :], preferred_element_type=F32)
    v = ALPHA_DN * x_ref[...] + g2_ref[...] * y
    o_ref[...] = _layer_norm(v, lg_ref[...], lb_ref[...])


def _combine(x2, ldest_col, gates_col, dst, ys, pmods, layer, rows_per_mod, tm, ln_g, ln_b):
    N, D = x2.shape
    n_t, n_chunks = dst.shape
    lb = n_chunks * RUN_ALIGN
    vec = pl.BlockSpec((1, D), lambda i, d: (0, 0))
    grid_spec = pltpu.PrefetchScalarGridSpec(
        num_scalar_prefetch=1,
        grid=(n_t,),
        in_specs=[pl.BlockSpec((tm, D), lambda i, d: (i, 0)),
                  pl.BlockSpec((tm, TOP_K), lambda i, d: (i, 0)),
                  pl.BlockSpec((tm, TOP_K), lambda i, d: (i, 0)),
                  pl.BlockSpec((None, None, None, 1, D), lambda i, d: (layer, 5, (i * tm) // rows_per_mod, 0, 0)),
                  vec, vec,
                  pl.BlockSpec(memory_space=pl.ANY)],
        out_specs=pl.BlockSpec((tm, D), lambda i, d: (i, 0)),
        scratch_shapes=[pltpu.VMEM((2, lb, D), BF16), pltpu.SemaphoreType.DMA((2,))],
    )
    return pl.pallas_call(
        functools.partial(_combine_body, n_chunks=n_chunks),
        grid_spec=grid_spec,
        out_shape=jax.ShapeDtypeStruct((N, D), F32),
        compiler_params=_cparams(("arbitrary",)),
        name="moe_combine",
    )(dst, x2, ldest_col, gates_col, pmods, ln_g.reshape(1, D), ln_b.reshape(1, D), ys)


def _moe_plan(cnts, n_chunks, te, spare_chunk):
    a = RUN_ALIGN
    cnt = jnp.concatenate(cnts, axis=0)
    pc = (cnt + a - 1) // a * a
    seg = (jnp.sum(pc, axis=0) + te - 1) // te * te
    pad_end = jnp.cumsum(seg)
    run_start = (pad_end - seg)[None, :] + jnp.cumsum(pc, axis=0) - pc
    lo = jnp.cumsum(pc, axis=1) - pc
    tables, t0 = [], 0
    for c, nc in zip(cnts, n_chunks):
        sl = slice(t0, t0 + c.shape[0])
        parity = ((t0 + jnp.arange(c.shape[0], dtype=I32)) % 2)[:, None]
        t0 += c.shape[0]
        pos = jnp.arange(nc, dtype=I32) * a
        owner = jnp.sum(((lo[sl] + pc[sl])[:, None, :] <= pos[None, :, None]).astype(I32), axis=-1)
        mine = owner[..., None] == jnp.arange(N_EXP, dtype=I32)
        base = jnp.sum(jnp.where(mine, (run_start[sl] - lo[sl])[:, None, :], 0), axis=-1)
        valid = pos[None, :] < jnp.sum(pc[sl], axis=1, keepdims=True)
        spare = spare_chunk + parity * nc + jnp.arange(nc, dtype=I32)[None, :]
        tables.append(jnp.where(valid, (base + pos[None, :]) // a, spare).astype(I32))
    tail_row = (pad_end - seg + jnp.sum(pc, axis=0))[:, None] + jnp.arange(te // a, dtype=I32)[None, :] * a
    tails = jnp.where(tail_row < pad_end[:, None], tail_row // a, -1).astype(I32).reshape(-1)
    return tables, tails, pad_end


def _moe(passes, layer, P):
    te = EXPERT_TILE
    routed = []
    for x, pmods, shared_mod in passes:
        B, T, D = x.shape
        N = B * T
        tm = min(MOE_TILE, N if shared_mod else T)
        x2 = x.reshape(N, D)
        u2, ldest, gates, cnt = _router(x2, pmods, layer, T, tm, P['w_router'][layer], P['b_router'][layer])
        routed.append((x2, pmods, T, tm, u2, ldest, gates, cnt[:, 0, :], x.shape))
    n_chunks = [r[3] * TOP_K // RUN_ALIGN + N_EXP for r in routed]
    n_tiles = [r[7].shape[0] for r in routed]
    n_assign = sum(r[0].shape[0] for r in routed) * TOP_K
    seg_rows = -(-(n_assign + sum(n_tiles) * N_EXP * RUN_ALIGN + N_EXP * te) // te) * te
    slots = seg_rows + -(-(2 * max(n_chunks) * RUN_ALIGN) // te) * te
    dsts, tails, pad_end = _moe_plan([r[7] for r in routed], n_chunks, te, seg_rows // RUN_ALIGN)
    n_blk = slots // te
    blk_pos = jnp.arange(n_blk, dtype=I32) * te
    blk_e = jnp.minimum(jnp.sum((pad_end[None, :] <= blk_pos[:, None]).astype(I32), axis=1), N_EXP - 1)
    n_used = (pad_end[-1] // te).astype(I32).reshape(1)
    assert len(set(r[3] for r in routed)) == 1, "all passes must use the same MoE tile"
    xs = _dispatch([r[4] for r in routed], [r[5] for r in routed], jnp.concatenate(dsts, axis=0), tails, n_used,
                   slots, routed[0][3], te)
    ys = _experts(xs, blk_e, n_used, layer, P['w_gu'], P['b_gu'], P['w_down'], P['b_down'])
    outs = []
    for r, dst in zip(routed, dsts):
        x2, pmods, T, tm, _, ldest, gates, _, shape = r
        y = _combine(x2, ldest.T, gates.T, dst, ys, pmods, layer, T, tm, P['ln_g'][layer, 1], P['ln_b'][layer, 1])
        outs.append(y.reshape(shape))
    return outs


def _split3(x):
    hi = x.astype(BF16)
    r1 = x - hi.astype(F32)
    mid = r1.astype(BF16)
    lo = (r1 - mid.astype(F32)).astype(BF16)
    return jnp.concatenate([hi, mid, lo], axis=1)


def _inproj1_body(x_ref, xp_ref, xn_ref, sc_ref, sh_ref, w_ref, wvt_ref, cw_ref, cb_ref, wg2_ref, wgh_ref,
                  bg_ref, qk_ref, vt_ref, o_ref, gcol_ref, grow_ref):
    j = pl.program_id(1)
    n_j = pl.num_programs(1)
    halo = xp_ref.shape[0]
    tm = x_ref.shape[0]
    nh = 4 * H_C

    def adaln(v):
        return v * (1.0 + sc_ref[...]) + sh_ref[...]

    u = adaln(x_ref[...])
    ub = u.astype(BF16)
    u_prev = jnp.where(j > 0, adaln(xp_ref[...]), 0.0).astype(BF16)
    u_next = jnp.where(j < n_j - 1, adaln(xn_ref[...]), 0.0).astype(BF16)
    u_ext = jnp.concatenate([u_prev, ub, u_next], axis=0)
    n_ext = tm + 2 * halo
    cblk = 256
    n_blk = 2 * D_MODEL // cblk

    def matmuls(b):
        c0 = b * cblk
        zqk = jnp.dot(u_ext, w_ref[:, c0:c0 + cblk], preferred_element_type=F32)
        if c0 < D_MODEL:
            vt = lax.dot_general(wvt_ref[c0:c0 + cblk, :], ub, (((1,), (1,)), ((), ())), preferred_element_type=F32)
            vt_ref[c0:c0 + cblk, :] = vt.astype(BF16)
        else:
            o0 = c0 - D_MODEL
            o_ref[:, o0:o0 + cblk] = jnp.dot(ub, w_ref[:, 3 * D_MODEL + o0:3 * D_MODEL + o0 + cblk],
                                             preferred_element_type=F32)
        return zqk

    z_next = matmuls(0)
    for b in range(n_blk):
        zqk = z_next
        if b + 1 < n_blk:
            z_next = matmuls(b + 1)
        cols = slice(b * cblk, (b + 1) * cblk)
        acc = (pltpu.roll(zqk, 2, 0) * cw_ref[0:1, cols] + pltpu.roll(zqk, 1, 0) * cw_ref[1:2, cols]
               + zqk * cw_ref[2:3, cols] + pltpu.roll(zqk, n_ext - 1, 0) * cw_ref[3:4, cols])[halo:tm + halo]
        act = jax.nn.silu(acc + cb_ref[:, cols])
        if b * cblk >= D_MODEL:
            act = act * (DH_C ** -0.5)
        qk_ref[:, cols] = act.astype(BF16)
    u_lo = (u - ub.astype(F32)).astype(BF16)
    g2 = jnp.dot(ub, wg2_ref[...], preferred_element_type=F32)
    gc = g2[:, :nh] + g2[:, nh:] + jnp.dot(u_lo, wgh_ref[...], preferred_element_type=F32) + bg_ref[...]
    r_i = lax.broadcasted_iota(I32, (tm, tm), 0)
    c_i = lax.broadcasted_iota(I32, (tm, tm), 1)
    same = (r_i // CHUNK) == (c_i // CHUNK)
    tri_f = jnp.where(jnp.logical_and(same, c_i <= r_i), 1.0, 0.0).astype(BF16)
    tri_b = jnp.where(jnp.logical_and(same, c_i >= r_i), 1.0, 0.0).astype(BF16)
    lf3 = _split3(jax.nn.log_sigmoid(gc))

    def sum3(p, axis):
        if axis == 1:
            return p[:, :nh] + p[:, nh:2 * nh] + p[:, 2 * nh:]
        return p[:nh] + p[nh:2 * nh] + p[2 * nh:]

    col_i = lax.broadcasted_iota(I32, (tm, nh), 1)
    cum = jnp.where(col_i < 2 * H_C,
                    sum3(jnp.dot(tri_f, lf3, preferred_element_type=F32), 1),
                    sum3(jnp.dot(tri_b, lf3, preferred_element_type=F32), 1))
    gcol = jnp.where((col_i // H_C) % 2 == 1, cum, gc)
    gcol_ref[...] = gcol
    eye = jnp.where(r_i == c_i, 1.0, 0.0).astype(BF16)
    gt3 = lax.dot_general(_split3(gcol), eye, (((0,), (0,)), ((), ())), preferred_element_type=F32)
    grow_ref[...] = sum3(gt3, 0)


def _inproj1(x, mods, w_in1, w_gate1, b_gate1, conv_w, conv_b, row_off, row_stride):
    B, T, D = x.shape
    tm = min(TOKEN_TILE, T)
    n_j = T // tm
    nh = 4 * H_C
    halo = 8
    wg_hi = w_gate1.astype(BF16)
    wg_lo = (w_gate1 - wg_hi.astype(F32)).astype(BF16)
    w_bf = w_in1.astype(BF16)
    tok = lambda w: pl.BlockSpec((None, tm, w), lambda b, j: (b, j, 0))
    full = lambda r, c: pl.BlockSpec((r, c), lambda b, j: (0, 0))
    return pl.pallas_call(
        _inproj1_body,
        grid=(B, n_j),
        in_specs=[tok(D),
                  pl.BlockSpec((None, halo, D), lambda b, j: (b, jnp.maximum(j * (tm // halo) - 1, 0), 0)),
                  pl.BlockSpec((None, halo, D), lambda b, j: (b, jnp.minimum((j + 1) * (tm // halo), T // halo - 1), 0)),
                  _mod_spec(1, 1, row_off, row_stride),
                  _mod_spec(1, 0, row_off, row_stride),
                  full(D, 4 * D), full(D, D), full(CONV_W, 2 * D), full(1, 2 * D),
                  full(D, 2 * nh), full(D, nh), full(1, nh)],
        out_specs=[tok(2 * D),
                   pl.BlockSpec((None, D, tm), lambda b, j: (b, 0, j)),
                   tok(D), tok(nh),
                   pl.BlockSpec((nh, tm), lambda b, j: (0, b * n_j + j))],
        out_shape=[jax.ShapeDtypeStruct((B, T, 2 * D), BF16),
                   jax.ShapeDtypeStruct((B, D, T), BF16),
                   jax.ShapeDtypeStruct((B, T, D), F32),
                   jax.ShapeDtypeStruct((B, T, nh), F32),
                   jax.ShapeDtypeStruct((nh, B * T), F32)],
        compiler_params=_cparams(("arbitrary", "arbitrary")),
        name="inproj1",
    )(x, x, x, mods, mods, w_bf, w_bf[:, 2 * D:3 * D].T, conv_w, conv_b.reshape(1, 2 * D),
      jnp.concatenate([wg_hi, wg_lo], axis=1), wg_hi, b_gate1.reshape(1, nh))


def _mlstm_body(q_ref, k_ref, vt_ref, gcol_ref, grow_ref, c0_ref, n0_ref, m0_ref,
                h_ref, cfin_ref, nfin_ref, mfin_ref, hb_s, *, T):
    n_c = T // CHUNK
    L = CHUNK

    rb = min(T, 512)
    t_i = lax.broadcasted_iota(I32, (L, L), 0)
    s_i = lax.broadcasted_iota(I32, (L, L), 1)

    cfin_ref[...] = c0_ref[...]
    nfin_ref[...] = n0_ref[...]
    mfin_ref[...] = m0_ref[...]

    def chunk(d, c):
        mask = (s_i <= t_i) if d == 0 else (s_i >= t_i)
        last = L - 1 if d == 0 else 0
        off = pl.multiple_of(c * L, L)
        qc = q_ref[pl.ds(off, L), :]
        kc = k_ref[pl.ds(off, L), :]
        vt = vt_ref[:, pl.ds(off, L)]
        gcol = gcol_ref[pl.ds(off, L), :]
        grow = grow_ref[:, pl.ds(off, L)]
        b_c = gcol[:, 2 * d + 1:2 * d + 2]
        li_r = grow[2 * d:2 * d + 1, :]
        b_r = grow[2 * d + 1:2 * d + 2, :]
        m_prev = mfin_ref[d]
        log_d = jnp.where(mask, b_c - b_r + li_r, -jnp.inf)
        m_inter = b_c + m_prev
        m_t = jnp.maximum(m_inter, jnp.max(log_d, axis=-1, keepdims=True))
        dmat = jnp.exp(log_d - m_t)
        w_inter = jnp.exp(m_inter - m_t)
        c_old = cfin_ref[d]
        s = lax.dot_general(qc, kc, (((1,), (1,)), ((), ())), preferred_element_type=F32) * dmat
        inter = lax.dot_general(qc, c_old.astype(BF16), (((1,), (1,)), ((), ())), preferred_element_type=F32)
        num = w_inter * inter + lax.dot_general(s.astype(BF16), vt, (((1,), (1,)), ((), ())),
                                                preferred_element_type=F32)
        qn = jnp.sum(qc.astype(F32) * nfin_ref[d], axis=-1, keepdims=True)
        den = w_inter * qn + jnp.sum(s, axis=-1, keepdims=True)
        h = num / jnp.maximum(jnp.abs(den), jnp.exp(-m_t))
        m_new = m_t[last:last + 1, :]
        b_last = b_c[last:last + 1, :]
        w_s = jnp.exp(b_last - b_r + li_r - m_new)
        decay = jnp.exp(b_last + m_prev - m_new)
        wvt = (vt.astype(F32) * w_s).astype(BF16)
        cfin_ref[d] = decay * c_old + jnp.dot(wvt, kc, preferred_element_type=F32)
        wk = jnp.dot(jnp.broadcast_to(w_s, (8, L)).astype(BF16), kc, preferred_element_type=F32)[0:1]
        nfin_ref[d] = decay * nfin_ref[d] + wk
        mfin_ref[d] = m_new
        return off, h

    def both(ci, _):
        off_f, h_f = chunk(0, ci)
        off_b, h_b = chunk(1, n_c - 1 - ci)
        h_ref[pl.ds(off_f, L), :] = h_f
        hb_s[pl.ds(off_b, L), :] = h_b
        return 0

    lax.fori_loop(0, n_c, both, 0)

    def add_bwd(r, _):
        off = pl.multiple_of(r * rb, rb)
        h_ref[pl.ds(off, rb), :] = h_ref[pl.ds(off, rb), :] + hb_s[pl.ds(off, rb), :]
        return 0

    lax.fori_loop(0, T // rb, add_bwd, 0)


def _mlstm(qk, vt, gcol, grow, c0, n0, m0):
    B, _, T = vt.shape
    DH = DH_C
    st = lambda *tail: pl.BlockSpec((None, 2, None) + tail, lambda b, h: (b, 0, h) + (0,) * len(tail))
    return pl.pallas_call(
        functools.partial(_mlstm_body, T=T),
        grid=(B, H_C),
        in_specs=[pl.BlockSpec((None, T, DH), lambda b, h: (b, 0, h)),
                  pl.BlockSpec((None, T, DH), lambda b, h: (b, 0, H_C + h)),
                  pl.BlockSpec((None, DH, T), lambda b, h: (b, h, 0)),
                  pl.BlockSpec((None, None, T, 4), lambda b, h: (b, h, 0, 0)),
                  pl.BlockSpec((None, None, 4, T), lambda b, h: (b, h, 0, 0)),
                  st(DH, DH), st(1, DH), st(1, 1)],
        out_specs=[pl.BlockSpec((None, T, DH), lambda b, h: (b, 0, h)),
                   st(DH, DH), st(1, DH), st(1, 1)],
        out_shape=[jax.ShapeDtypeStruct((B, T, D_MODEL), F32),
                   jax.ShapeDtypeStruct((B, 2, H_C, DH, DH), F32),
                   jax.ShapeDtypeStruct((B, 2, H_C, 1, DH), F32),
                   jax.ShapeDtypeStruct((B, 2, H_C, 1, 1), F32)],
        scratch_shapes=[pltpu.VMEM((T, DH), F32)],
        compiler_params=_cparams(("arbitrary", "arbitrary")),
        name="mlstm",
    )(qk, qk, vt, gcol, grow, c0, n0, m0)


def _outproj1_body(x_ref, h_ref, o_ref, g1_ref, w_ref, lg_ref, lb_ref, out_ref):
    y = jax.nn.sigmoid(o_ref[...]) * h_ref[...]
    out = jnp.dot(y.astype(BF16), w_ref[...], preferred_element_type=F32)
    v = ALPHA_DN * x_ref[...] + g1_ref[...] * out
    out_ref[...] = _layer_norm(v, lg_ref[...], lb_ref[...])


def _outproj1(x, h, o, mods, w_out1, ln_g, ln_b, row_off, row_stride):
    B, T, D = x.shape
    tm = min(TOKEN_TILE, T)
    tok = pl.BlockSpec((None, tm, D), lambda b, j: (b, j, 0))
    vec = pl.BlockSpec((1, D), lambda b, j: (0, 0))
    return pl.pallas_call(
        _outproj1_body,
        grid=(B, T // tm),
        in_specs=[tok, tok, tok, _mod_spec(1, 2, row_off, row_stride),
                  pl.BlockSpec((D, D), lambda b, j: (0, 0)), vec, vec],
        out_specs=tok,
        out_shape=jax.ShapeDtypeStruct((B, T, D), F32),
        compiler_params=_cparams(("arbitrary", "arbitrary")),
        name="outproj1",
    )(x, h, o, mods, w_out1.astype(BF16), ln_g.reshape(1, D), ln_b.reshape(1, D))


def _to_col_major(x):
    B, T, C = x.shape
    rows = T // GRID_W
    return x.reshape(B, rows, GRID_W, C).transpose(0, 2, 1, 3).reshape(B, T, C)


def _to_row_major(x):
    B, T, C = x.shape
    rows = T // GRID_W
    return x.reshape(B, GRID_W, rows, C).transpose(0, 2, 1, 3).reshape(B, T, C)


def _mixer0(x, mods, st, P, s5_mats):
    h_rg, s_re, s_im = st
    xa, ga, ub = _inproj0(x, mods, P['w_in0'][0].astype(BF16))
    h, h_fin = _rglru(xa, P['conv_a_w'][0], P['conv_a_b'][0], P['rg_wa'][0], P['rg_ba'][0],
                      P['rg_wi'][0], P['rg_bi'][0], P['rg_lam'][0], h_rg[:, 0].transpose(1, 0, 2))
    y5, s_fin = _s5(ub, s5_mats, _s5_state_to_lanes(s_re[:, 0], s_im[:, 0]))
    x = _outproj0(x, h, ga, y5, ub, mods, P['s5_d'][0], P['glu_w'][0], P['glu_b'][0], P['w_out0'][0],
                  P['ln_g'][0, 0], P['ln_b'][0, 0])
    new_re, new_im = _s5_lanes_to_state(s_fin)
    return x, (h_fin.transpose(1, 0, 2)[:, None], new_re[:, None], new_im[:, None])


def _mixer1(x, mods, st, P):
    m_c, m_n, m_m = st
    B, T, D = x.shape
    qk, v, o, gcol, grow = _inproj1(x, mods, P['w_in1'][0], P['w_gate1'][0], P['b_gate1'][0],
                                    P['conv_c_w'][0], P['conv_c_b'][0], 0, 1)
    gcol = gcol.reshape(B, T, 2, 2, H_C).transpose(0, 4, 1, 2, 3).reshape(B, H_C, T, 4)
    grow = grow.reshape(2, 2, H_C, B, T).transpose(3, 2, 0, 1, 4).reshape(B, H_C, 4, T)
    h, c_fin, n_fin, m_fin = _mlstm(qk, v, gcol, grow,
                                    m_c[:, 0], m_n[:, 0][:, :, :, None, :], m_m[:, 0][:, :, :, None, None])
    x = _outproj1(x, h, o, mods, P['w_out1'][0], P['ln_g'][1, 0], P['ln_b'][1, 0], 0, 1)
    return x, (c_fin[:, None], n_fin[:, None, :, :, 0, :], m_fin[:, None, :, :, 0, 0])


def _forward(x_prompt, x_sample, c, c_ctx, states, P):
    bp = x_prompt.shape[0]
    bs = x_sample.shape[0]
    rows = 1 + bs
    rpad = -(-rows // 8) * 8
    cv = jnp.concatenate([c_ctx[None, :], c, jnp.zeros((rpad - rows, D_MODEL), F32)], axis=0)
    mods = _modulation(cv, P['w_mod'], P['b_mod'])
    s5_mats = _s5_matrices(P['s5_a_re'][0], P['s5_a_im'][0], P['s5_log_dt'][0], P['s5_b_re'][0], P['s5_b_im'][0],
                           P['s5_c_re'][0], P['s5_c_im'][0])
    zero_state = (jnp.zeros((bp, 1, 2, D_A), F32),
                  jnp.zeros((bp, 1, 2, G_B, P_B), F32),
                  jnp.zeros((bp, 1, 2, G_B, P_B), F32),
                  jnp.zeros((bp, 1, 2, H_C, DH_C, DH_C), F32),
                  jnp.zeros((bp, 1, 2, H_C, DH_C), F32),
                  jnp.zeros((bp, 1, 2, H_C), F32))
    mods_ctx = jnp.broadcast_to(mods[:, :, 0:1], mods.shape[:2] + (bp,) + mods.shape[3:])
    mods_lat = mods[:, :, 1:1 + bs]
    xc, new_even = _mixer0(x_prompt, mods_ctx, zero_state[:3], P, s5_mats)
    xl, _ = _mixer0(x_sample, mods_lat, states[:3], P, s5_mats)
    xc, xl = _moe([(xc, mods_ctx, True), (xl, mods_lat, False)], 0, P)
    xl = _to_col_major(xl)
    xc, new_odd = _mixer1(xc, mods_ctx, zero_state[3:], P)
    xl, _ = _mixer1(xl, mods_lat, states[3:], P)
    xc, xl = _moe([(xc, mods_ctx, True), (xl, mods_lat, False)], 1, P)
    return (xc, _to_row_major(xl)) + tuple(new_even) + tuple(new_odd)


def kernel(x_prompt, x_sample, c, c_ctx, state_rglru, state_s5_re, state_s5_im, state_mlstm_C, state_mlstm_n, state_mlstm_m, w_mod, b_mod, ln_g, ln_b, w_in0, conv_a_w, conv_a_b, rg_wa, rg_ba, rg_wi, rg_bi, rg_lam, s5_a_re, s5_a_im, s5_log_dt, s5_b_re, s5_b_im, s5_c_re, s5_c_im, s5_d, glu_w, glu_b, w_out0, w_in1, w_gate1, b_gate1, conv_c_w, conv_c_b, w_out1, w_router, b_router, w_gu, b_gu, w_down, b_down):
    P = dict(w_mod=w_mod, b_mod=b_mod, ln_g=ln_g, ln_b=ln_b, w_in0=w_in0, conv_a_w=conv_a_w,
             conv_a_b=conv_a_b, rg_wa=rg_wa, rg_ba=rg_ba, rg_wi=rg_wi, rg_bi=rg_bi, rg_lam=rg_lam,
             s5_a_re=s5_a_re, s5_a_im=s5_a_im, s5_log_dt=s5_log_dt, s5_b_re=s5_b_re, s5_b_im=s5_b_im,
             s5_c_re=s5_c_re, s5_c_im=s5_c_im, s5_d=s5_d, glu_w=glu_w, glu_b=glu_b, w_out0=w_out0,
             w_in1=w_in1, w_gate1=w_gate1, b_gate1=b_gate1, conv_c_w=conv_c_w, conv_c_b=conv_c_b,
             w_out1=w_out1, w_router=w_router, b_router=b_router, w_gu=w_gu, b_gu=b_gu,
             w_down=w_down, b_down=b_down)
    states = (state_rglru, state_s5_re, state_s5_im, state_mlstm_C, state_mlstm_n, state_mlstm_m)
    return _forward(x_prompt, x_sample, c, c_ctx, states, P)
```

```python
import functools

import jax
import jax.numpy as jnp
from jax import lax
from jax.experimental import pallas as pl
from jax.experimental.pallas import tpu as pltpu

F32 = jnp.float32
BF16 = jnp.bfloat16
I32 = jnp.int32
HI = lax.Precision.HIGHEST

D_MODEL = 1024
DEPTH = 2
GRID_W = 64
D_A = 512
NB_A = 8
BS_A = D_A // NB_A
CONV_W = 4
LRU_C = 8.0
D_B = 512
S5_GROUP = 16
G_B = D_B // S5_GROUP
P_B = 64
H_C = 4
DH_C = D_MODEL // H_C
CHUNK = 128
N_EXP = 32
TOP_K = 4
D_FF = D_MODEL
SWIGLU_LIMIT = 7.0
SWIGLU_ALPHA = 1.702
ALPHA_DN = (2 * DEPTH) ** 0.25
LN_EPS = 1e-5

S5_L = 16
S5_W = S5_L * S5_GROUP
S5_OCT = 8
TOKEN_TILE = 256
EXPERT_TILE = 512
MOE_TILE = 512
RUN_ALIGN = 16
VMEM_LIMIT = 56 * 1024 * 1024


def _cparams(sem, vmem=VMEM_LIMIT):
    return pltpu.CompilerParams(dimension_semantics=sem, vmem_limit_bytes=vmem)


def _layer_norm(v, g, b):
    mu = jnp.mean(v, axis=-1, keepdims=True)
    c = v - mu
    var = jnp.mean(c * c, axis=-1, keepdims=True)
    return c * lax.rsqrt(var + LN_EPS) * g + b


def _mod_spec(layer, which, row_off, row_stride):
    return pl.BlockSpec((None, None, None, 1, D_MODEL),
                        lambda b, j: (layer, which, row_off + b * row_stride, 0, 0))


def _mod_spec_all(layer, which, nb):
    return pl.BlockSpec((None, None, nb, 1, D_MODEL), lambda i: (layer, which, 0, 0, 0))


def _row_permutation(n_out_major, n_out_minor):
    n = n_out_major * n_out_minor
    r_out = lax.broadcasted_iota(I32, (n, n), 0)
    r_in = lax.broadcasted_iota(I32, (n, n), 1)
    hit = jnp.logical_and(r_out // n_out_minor == r_in % n_out_major, r_out % n_out_minor == r_in // n_out_major)
    return jnp.where(hit, 1.0, 0.0).astype(BF16)


def _mod_body(c_ref, w_ref, b_ref, o_ref):
    s = jax.nn.silu(c_ref[...])
    o_ref[...] = jnp.dot(s, w_ref[...], precision=HI, preferred_element_type=F32) + b_ref[...]


def _modulation(cv, w_mod, b_mod):
    R, D = cv.shape
    L, _, N6 = w_mod.shape
    tn = N6 // 6
    out = pl.pallas_call(
        _mod_body,
        grid=(L, N6 // tn),
        in_specs=[pl.BlockSpec((R, D), lambda l, j: (0, 0)),
                  pl.BlockSpec((None, D, tn), lambda l, j: (l, 0, j)),
                  pl.BlockSpec((None, 1, tn), lambda l, j: (l, 0, j))],
        out_specs=pl.BlockSpec((None, R, tn), lambda l, j: (l, 0, j)),
        out_shape=jax.ShapeDtypeStruct((L, R, N6), F32),
        compiler_params=_cparams(("arbitrary", "arbitrary")),
        name="modulation",
    )(cv, w_mod, b_mod.reshape(L, 1, N6))
    return out.reshape(L, R, 6, D).transpose(0, 2, 1, 3)[:, :, :, None, :]


def _inproj0_body(x_ref, sc_ref, sh_ref, w_ref, xa_ref, ga_ref, ub_ref):
    nb, tq, d = x_ref.shape
    u = x_ref[...] * (1.0 + sc_ref[...]) + sh_ref[...]
    ub = u.reshape(nb * tq, d).astype(BF16)
    ut = jnp.dot(_row_permutation(tq, nb), ub, preferred_element_type=F32).astype(BF16)
    z = jnp.dot(ut, w_ref[...], preferred_element_type=F32).reshape(tq, nb, w_ref.shape[1])
    xa_ref[...] = z[:, :, :D_A]
    ga_ref[...] = z[:, :, D_A:2 * D_A]
    ub_ref[...] = z[:, :, 2 * D_A:]


def _inproj0(x, pmods, w_in0):
    B, T, D = x.shape
    tq = TOKEN_TILE // B
    nz = w_in0.shape[1]
    tmaj = pl.BlockSpec((tq, B, D_A), lambda i: (i, 0, 0))
    return pl.pallas_call(
        _inproj0_body,
        grid=(T // tq,),
        in_specs=[pl.BlockSpec((B, tq, D), lambda i: (0, i, 0)),
                  _mod_spec_all(0, 1, B), _mod_spec_all(0, 0, B),
                  pl.BlockSpec((D, nz), lambda i: (0, 0))],
        out_specs=[tmaj, tmaj, tmaj],
        out_shape=[jax.ShapeDtypeStruct((T, B, D_A), F32)] * 3,
        compiler_params=_cparams(("arbitrary",)),
        name="inproj0",
    )(x, pmods, pmods, w_in0)


def _rglru_body(cur_ref, prev_ref, next_ref, cw_ref, cb_ref, wa_ref, wi_ref, ba_ref, bi_ref, lam_ref, h0_ref,
                h_ref, hfin_ref, ext_s, a_s, b_s, carry_s, *, tt, n_t):
    d = pl.program_id(0)
    j = pl.program_id(1)
    jj = j + d * (n_t - 1 - 2 * j)
    nb = cur_ref.shape[1]
    half = D_A // 2
    ext_s[0:2] = jnp.where(jj == 0, 0.0, prev_ref[...])
    ext_s[2:tt + 2] = cur_ref[...]
    ext_s[tt + 2:tt + 3] = jnp.where(jj == n_t - 1, 0.0, next_ref[...])
    xc = ext_s[0:tt] * cw_ref[0]
    for k in range(1, CONV_W):
        xc = xc + ext_s[k:k + tt] * cw_ref[k]
    xc = xc + cb_ref[...]
    x2 = xc.reshape(tt * nb, D_A)
    xb = x2.astype(BF16)

    def gate(w_ref, bias_ref):
        lo = jnp.dot(xb[:, :half], w_ref[0], preferred_element_type=F32)
        hi = jnp.dot(xb[:, half:], w_ref[1], preferred_element_type=F32)
        return jax.nn.sigmoid(jnp.concatenate([lo, hi], axis=1) + bias_ref[...])

    r = gate(wa_ref, ba_ref)
    i = gate(wi_ref, bi_ref)
    log_a = LRU_C * r * jax.nn.log_sigmoid(lam_ref[...])
    a = jnp.exp(log_a)
    one_minus_a2 = -jnp.tanh(log_a) * (a * a + 1.0)
    a_s[...] = a.reshape(tt, nb, D_A)
    b_s[...] = (jnp.sqrt(one_minus_a2) * (i * x2)).reshape(tt, nb, D_A)

    @pl.when(j == 0)
    def _():
        carry_s[...] = h0_ref[...]

    def step(t, h):
        h = a_s[t] * h + b_s[t]
        h_ref[t] = h
        return h

    @pl.when(d == 0)
    def _():
        carry_s[...] = lax.fori_loop(0, tt, step, carry_s[...], unroll=8)

    @pl.when(d == 1)
    def _():
        carry_s[...] = lax.fori_loop(0, tt, lambda t, h: step(tt - 1 - t, h), carry_s[...], unroll=8)

    hfin_ref[...] = carry_s[...]


def _block_diag_halves(w):
    nd = w.shape[0]
    per = NB_A // 2
    w = w.reshape(nd, 2, per, BS_A, BS_A)
    eye = jnp.eye(per, dtype=w.dtype)
    out = jnp.einsum('dhncz,nm->dhncmz', w, eye)
    return out.reshape(nd, 2, per * BS_A, per * BS_A)


def _rglru(x3, conv_w, conv_b, wa, ba, wi, bi, lam, h0):
    T, B, _ = x3.shape
    tt = min(T, 1024 // B)
    n_t = T // tt

    def cur_map(d, j):
        return (j + d * (n_t - 1 - 2 * j), 0, 0)

    def prev_map(d, j):
        jj = j + d * (n_t - 1 - 2 * j)
        return (jnp.maximum(jj * (tt // 2) - 1, 0), 0, 0)

    def next_map(d, j):
        jj = j + d * (n_t - 1 - 2 * j)
        return (jnp.minimum((jj + 1) * tt, T - 1), 0, 0)

    dir_spec3 = pl.BlockSpec((None, 1, D_A), lambda d, j: (d, 0, 0))
    h, hfin = pl.pallas_call(
        functools.partial(_rglru_body, tt=tt, n_t=n_t),
        grid=(2, n_t),
        in_specs=[pl.BlockSpec((tt, B, D_A), cur_map),
                  pl.BlockSpec((2, B, D_A), prev_map),
                  pl.BlockSpec((1, B, D_A), next_map),
                  pl.BlockSpec((CONV_W, 1, D_A), lambda d, j: (0, 0, 0)),
                  pl.BlockSpec((1, 1, D_A), lambda d, j: (0, 0, 0)),
                  pl.BlockSpec((None, 2, D_A // 2, D_A // 2), lambda d, j: (d, 0, 0, 0)),
                  pl.BlockSpec((None, 2, D_A // 2, D_A // 2), lambda d, j: (d, 0, 0, 0)),
                  dir_spec3, dir_spec3, dir_spec3,
                  pl.BlockSpec((None, B, D_A), lambda d, j: (d, 0, 0))],
        out_specs=[pl.BlockSpec((None, tt, B, D_A), lambda d, j: (d,) + cur_map(d, j)),
                   pl.BlockSpec((None, B, D_A), lambda d, j: (d, 0, 0))],
        out_shape=[jax.ShapeDtypeStruct((2, T, B, D_A), F32),
                   jax.ShapeDtypeStruct((2, B, D_A), F32)],
        scratch_shapes=[pltpu.VMEM((tt + 3, B, D_A), F32),
                        pltpu.VMEM((tt, B, D_A), F32),
                        pltpu.VMEM((tt, B, D_A), F32),
                        pltpu.VMEM((B, D_A), F32)],
        compiler_params=_cparams(("arbitrary", "arbitrary")),
        name="rglru",
    )(x3, x3, x3, conv_w.reshape(CONV_W, 1, D_A), conv_b.reshape(1, 1, D_A),
      _block_diag_halves(wa).astype(BF16), _block_diag_halves(wi).astype(BF16),
      ba.reshape(2, 1, D_A), bi.reshape(2, 1, D_A), lam.reshape(2, 1, D_A), h0)
    return h, hfin


def _s5_matrices(a_re, a_im, log_dt, b_re, b_im, c_re, c_im):
    L = S5_L
    lam = lax.complex(a_re.astype(F32), a_im.astype(F32))
    dt = jnp.exp(log_dt.astype(F32))[..., None]
    ldt = lam * dt
    a_bar = jnp.exp(ldt)
    b_bar = ((a_bar - 1.0) / lam)[..., None] * lax.complex(b_re.astype(F32), b_im.astype(F32))
    cc = lax.complex(c_re.astype(F32), c_im.astype(F32))
    ks = jnp.arange(L + 1, dtype=F32)
    pw = jnp.exp(ldt[:, :, None, :] * ks[None, None, :, None].astype(jnp.complex64))
    kern = jnp.real(jnp.einsum('dgjp,dgkp,dgpi->dgkji', cc, pw[:, :, :L], b_bar, precision=HI))
    s_idx = jnp.arange(L)[:, None]
    t_idx = jnp.arange(L)[None, :]
    lag_f = jnp.clip(t_idx - s_idx, 0, L - 1)
    lag_b = jnp.clip(s_idx - t_idx, 0, L - 1)
    m_f = jnp.where((t_idx >= s_idx)[None, :, :, None, None], kern[0][:, lag_f], 0.0)
    m_b = jnp.where((s_idx >= t_idx)[None, :, :, None, None], kern[1][:, lag_b], 0.0)
    m = (m_f + m_b).transpose(0, 1, 4, 2, 3).reshape(G_B, S5_W, S5_W)
    down = (L - ks[:L])[None, :, None].astype(jnp.complex64)
    pw_down_b = jnp.exp(ldt[1][:, None, :] * down)
    g_f = jnp.exp(ldt[0][:, None, :] * (down - 1.0))[..., None] * b_bar[0][:, None]
    g_b = pw[1][:, :L, :, None] * b_bar[1][:, None]

    def g_cols(x):
        return x.transpose(0, 1, 3, 2).reshape(G_B, S5_W, P_B)

    gs = jnp.concatenate([g_cols(jnp.real(g_f)), g_cols(jnp.real(g_b)),
                          g_cols(jnp.imag(g_f)), g_cols(jnp.imag(g_b))], axis=-1)
    e_f = cc[0][:, None] * pw[0][:, 1:, None, :]
    e_b = cc[1][:, None] * pw_down_b[:, :, None, :]

    def e_rows(x):
        return x.transpose(0, 3, 1, 2).reshape(G_B, P_B, S5_W)

    e = jnp.concatenate([e_rows(jnp.real(e_f)), e_rows(jnp.real(e_b)),
                         -e_rows(jnp.imag(e_f)), -e_rows(jnp.imag(e_b))], axis=1)
    a_l = pw[:, :, L]
    al = jnp.concatenate([jnp.real(a_l[0]), jnp.real(a_l[1]), jnp.imag(a_l[0]), jnp.imag(a_l[1])], axis=-1)
    no = G_B // S5_OCT
    ow = S5_OCT * S5_W
    rows_sgi = lambda x: x.reshape(no, S5_OCT, L, S5_GROUP, S5_W).transpose(0, 2, 1, 3, 4).reshape(no, ow, S5_W)
    src = jnp.stack([rows_sgi(m), rows_sgi(gs), e.reshape(no, ow, S5_W)]).astype(BF16)
    r = jnp.arange(ow, dtype=I32)
    c = jnp.arange(S5_W, dtype=I32)
    grp_sgi = (r // S5_GROUP) % S5_OCT
    grp_gl = r // S5_W
    src_sgi = (r // (S5_OCT * S5_GROUP)) * S5_GROUP + r % S5_GROUP
    src_gl = r % S5_W
    spread = jnp.stack([src_sgi, src_gl, src_sgi])[:, None, :] == c[None, :, None]
    row_grp = jnp.stack([grp_sgi, grp_sgi, grp_gl])[:, :, None]
    col_grp = jnp.stack([grp_sgi, grp_gl, grp_sgi])[:, None, :]
    tr = 512
    out = pl.pallas_call(
        _s5_expand_body,
        grid=(3, no, ow // tr),
        in_specs=[pl.BlockSpec((None, None, tr, S5_W), lambda k, q, i: (k, q, i, 0)),
                  pl.BlockSpec((None, S5_W, ow), lambda k, q, i: (k, 0, 0)),
                  pl.BlockSpec((None, tr, 1), lambda k, q, i: (k, i, 0)),
                  pl.BlockSpec((None, 1, ow), lambda k, q, i: (k, 0, 0))],
        out_specs=pl.BlockSpec((None, None, tr, ow), lambda k, q, i: (k, q, i, 0)),
        out_shape=jax.ShapeDtypeStruct((3, no, ow, ow), BF16),
        compiler_params=_cparams(("arbitrary", "arbitrary", "arbitrary")),
        name="s5_expand",
    )(src, spread.astype(BF16), row_grp, col_grp)
    return out, al.reshape(1, G_B * S5_W)


def _s5_expand_body(src_ref, spread_ref, rg_ref, cg_ref, o_ref):
    wide = jnp.dot(src_ref[...], spread_ref[...], preferred_element_type=F32)
    o_ref[...] = jnp.where(rg_ref[...] == cg_ref[...], wide, 0.0).astype(BF16)


def _s5_fill_lhs(u_ref, lhs_s):
    tc, _, nb, lanes = u_ref.shape
    for s in range(S5_L):
        lhs_s[:, s * lanes:(s + 1) * lanes] = u_ref[:, s].reshape(tc * nb, lanes).astype(BF16)


def _s5_state_body(u_ref, g_ref, f_ref, lhs_s):
    _s5_fill_lhs(u_ref, lhs_s)
    f_ref[...] = jnp.dot(lhs_s[...], g_ref[...], preferred_element_type=F32).reshape(f_ref.shape)


def _s5_scan_body(f_ref, a_ref, s0_ref, sin_ref, sfin_ref, *, n_c):
    nb, width = s0_ref.shape
    hw = 2 * P_B
    n_g = width // S5_W
    a = a_ref[...]
    a_re = [a[:, k * S5_W:k * S5_W + hw] for k in range(n_g)]
    a_im = [a[:, k * S5_W + hw:(k + 1) * S5_W] for k in range(n_g)]
    is_fwd = (lax.broadcasted_iota(I32, (nb, width), 1) % hw) < P_B

    def split(x):
        return tuple(x[:, k * hw:(k + 1) * hw] for k in range(2 * n_g))

    def merge(parts):
        return jnp.concatenate(parts, axis=-1)

    def advance(c, parts):
        f = split(f_ref[c])
        out = []
        for k in range(n_g):
            s_re, s_im = parts[2 * k], parts[2 * k + 1]
            out.append(a_re[k] * s_re - a_im[k] * s_im + f[2 * k])
            out.append(a_re[k] * s_im + a_im[k] * s_re + f[2 * k + 1])
        return tuple(out)

    def fwd(c, parts):
        sin_ref[c] = merge(parts)
        return advance(c, parts)

    init = split(s0_ref[...])
    fin_f = lax.fori_loop(0, n_c, fwd, init)

    def bwd(k, parts):
        c = n_c - 1 - k
        sin_ref[c] = jnp.where(is_fwd, sin_ref[c], merge(parts))
        return advance(c, parts)

    fin_b = lax.fori_loop(0, n_c, bwd, init)
    sfin_ref[...] = jnp.where(is_fwd, merge(fin_f), merge(fin_b))


def _s5_out_body(u_ref, sin_ref, m_ref, e_ref, y_ref, lhs_s):
    _s5_fill_lhs(u_ref, lhs_s)
    tc, _, nb, lanes = u_ref.shape
    sin = sin_ref[...].reshape(tc * nb, sin_ref.shape[-1]).astype(BF16)
    y = (jnp.dot(lhs_s[...], m_ref[...], preferred_element_type=F32)
         + jnp.dot(sin, e_ref[...], preferred_element_type=F32))
    for s in range(S5_L):
        y_ref[:, s] = y[:, s * lanes:(s + 1) * lanes].reshape(tc, nb, lanes)


def _s5(ub, mats, s0):
    mge, al = mats
    T, B, _ = ub.shape
    n_c = T // S5_L
    no = G_B // S5_OCT
    lanes = S5_OCT * S5_GROUP
    ow = S5_OCT * S5_W
    tc = min(n_c, TOKEN_TILE // B)
    u4 = ub.reshape(n_c, S5_L, B, D_B)
    u_spec = pl.BlockSpec((tc, S5_L, B, lanes), lambda q, i: (i, 0, 0, q))
    w_spec = lambda kind: pl.BlockSpec((None, None, ow, ow), lambda q, i: (kind, q, 0, 0))
    st_spec = pl.BlockSpec((tc, B, ow), lambda q, i: (i, 0, q))
    f_loc = pl.pallas_call(
        _s5_state_body,
        grid=(no, n_c // tc),
        in_specs=[u_spec, w_spec(1)],
        out_specs=st_spec,
        out_shape=jax.ShapeDtypeStruct((n_c, B, G_B * S5_W), F32),
        scratch_shapes=[pltpu.VMEM((tc * B, S5_L * lanes), BF16)],
        compiler_params=_cparams(("arbitrary", "arbitrary")),
        name="s5_state",
    )(u4, mge)
    sw = 4 * S5_W
    sin, sfin = pl.pallas_call(
        functools.partial(_s5_scan_body, n_c=n_c),
        grid=(G_B * S5_W // sw,),
        in_specs=[pl.BlockSpec((n_c, B, sw), lambda g: (0, 0, g)),
                  pl.BlockSpec((1, sw), lambda g: (0, g)),
                  pl.BlockSpec((B, sw), lambda g: (0, g))],
        out_specs=[pl.BlockSpec((n_c, B, sw), lambda g: (0, 0, g)),
                   pl.BlockSpec((B, sw), lambda g: (0, g))],
        out_shape=[jax.ShapeDtypeStruct((n_c, B, G_B * S5_W), F32),
                   jax.ShapeDtypeStruct((B, G_B * S5_W), F32)],
        compiler_params=_cparams(("arbitrary",)),
        name="s5_scan",
    )(f_loc, al, s0)
    y = pl.pallas_call(
        _s5_out_body,
        grid=(no, n_c // tc),
        in_specs=[u_spec, st_spec, w_spec(0), w_spec(2)],
        out_specs=u_spec,
        out_shape=jax.ShapeDtypeStruct((n_c, S5_L, B, D_B), F32),
        scratch_shapes=[pltpu.VMEM((tc * B, S5_L * lanes), BF16)],
        compiler_params=_cparams(("arbitrary", "arbitrary")),
        name="s5_out",
    )(u4, sin, mge, mge)
    return y.reshape(T, B, D_B), sfin


def _s5_state_to_lanes(s_re, s_im):
    parts = [s_re[:, 0], s_re[:, 1], s_im[:, 0], s_im[:, 1]]
    return jnp.concatenate(parts, axis=-1).reshape(s_re.shape[0], G_B * S5_W)


def _s5_lanes_to_state(s):
    s = s.reshape(s.shape[0], G_B, 4, P_B)
    return jnp.stack([s[:, :, 0], s[:, :, 1]], axis=1), jnp.stack([s[:, :, 2], s[:, :, 3]], axis=1)


def _outproj0_body(x_ref, h_ref, ga_ref, y_ref, ub_ref, g1_ref, d_ref, gw_ref, gb_ref, wo_ref,
                   lg_ref, lb_ref, o_ref):
    nb, tq, d = x_ref.shape
    rows = tq * nb

    def flat(v):
        return v.reshape(rows, v.shape[-1])

    ya = flat(h_ref[0] + h_ref[1]) * jax.nn.gelu(flat(ga_ref[...]))
    yb = flat(y_ref[...]) + d_ref[...] * flat(ub_ref[...])
    g = jax.nn.gelu(yb)
    gate = jax.nn.sigmoid(jnp.dot(g.astype(BF16), gw_ref[...], preferred_element_type=F32) + gb_ref[...])
    cat = jnp.concatenate([ya, g * gate], axis=1).astype(BF16)
    cat = jnp.dot(_row_permutation(nb, tq), cat, preferred_element_type=F32).astype(BF16)
    out = jnp.dot(cat, wo_ref[...], preferred_element_type=F32).reshape(nb, tq, d)
    v = ALPHA_DN * x_ref[...] + g1_ref[...] * out
    o_ref[...] = _layer_norm(v, lg_ref[...], lb_ref[...])


def _outproj0(x, h, ga, y5, ub, pmods, s5_d, glu_w, glu_b, w_out0, ln_g, ln_b):
    B, T, D = x.shape
    tq = TOKEN_TILE // B
    tmaj = pl.BlockSpec((tq, B, D_A), lambda i: (i, 0, 0))
    vec = lambda w: pl.BlockSpec((1, w), lambda i: (0, 0))
    return pl.pallas_call(
        _outproj0_body,
        grid=(T // tq,),
        in_specs=[pl.BlockSpec((B, tq, D), lambda i: (0, i, 0)),
                  pl.BlockSpec((2, tq, B, D_A), lambda i: (0, i, 0, 0)),
                  tmaj, tmaj, tmaj,
                  _mod_spec_all(0, 2, B),
                  vec(D_B),
                  pl.BlockSpec((D_B, D_B), lambda i: (0, 0)),
                  vec(D_B),
                  pl.BlockSpec((D, D), lambda i: (0, 0)),
                  vec(D), vec(D)],
        out_specs=pl.BlockSpec((B, tq, D), lambda i: (0, i, 0)),
        out_shape=jax.ShapeDtypeStruct((B, T, D), F32),
        compiler_params=_cparams(("arbitrary",)),
        name="outproj0",
    )(x, h, ga, y5, ub, pmods, s5_d.reshape(1, D_B), glu_w.astype(BF16), glu_b.reshape(1, D_B),
      w_out0.astype(BF16), ln_g.reshape(1, D), ln_b.reshape(1, D))


def _router_body(x_ref, sc_ref, sh_ref, wr_ref, br_ref, u_ref, ldest_ref, gate_ref, cnt_ref):
    u = x_ref[...] * (1.0 + sc_ref[...]) + sh_ref[...]
    ub = u.astype(BF16)
    u_ref[...] = ub
    tm = u.shape[0]
    u_lo = (u - ub.astype(F32)).astype(BF16)
    nt = (((1,), (1,)), ((), ()))
    both = lax.dot_general(wr_ref[...], ub, nt, preferred_element_type=F32)
    logits = (both[:N_EXP] + both[N_EXP:] + lax.dot_general(wr_ref[:N_EXP], u_lo, nt, preferred_element_type=F32)
              + br_ref[...])
    e_iota = lax.broadcasted_iota(I32, logits.shape, 0)
    work = logits
    vals, hots = [], []
    for _ in range(TOP_K):
        m = jnp.max(work, axis=0, keepdims=True)
        idx = jnp.min(jnp.where(work == m, e_iota, N_EXP), axis=0, keepdims=True)
        hot = e_iota == idx
        vals.append(m)
        hots.append(hot)
        work = jnp.where(hot, -jnp.inf, work)
    ex = [jnp.exp(v - vals[0]) for v in vals]
    den = ex[0] + ex[1] + ex[2] + ex[3]
    gate_ref[...] = jnp.concatenate([e / den for e in ex], axis=0)
    hot_sum = jnp.zeros(logits.shape, F32)
    for hot in hots:
        hot_sum = hot_sum + hot.astype(F32)
    hot_b = hot_sum.astype(BF16)
    before = lax.broadcasted_iota(I32, (tm, tm), 0) < lax.broadcasted_iota(I32, (tm, tm), 1)
    excl = jnp.dot(hot_b, jnp.where(before, 1.0, 0.0).astype(BF16), preferred_element_type=F32)
    cnt_row = lax.dot_general(jnp.ones((8, tm), BF16), hot_b, (((1,), (1,)), ((), ())),
                              preferred_element_type=F32)[0:1]
    cnt_ref[...] = cnt_row.astype(I32)
    run_len = jnp.ceil(cnt_row * (1.0 / RUN_ALIGN)) * RUN_ALIGN
    lower = lax.broadcasted_iota(I32, (N_EXP, N_EXP), 1) < lax.broadcasted_iota(I32, (N_EXP, N_EXP), 0)
    run_off = jnp.sum(jnp.where(lower, run_len, 0.0), axis=1, keepdims=True)
    base = excl + run_off
    rows = [jnp.sum(jnp.where(hot, base, 0.0), axis=0, keepdims=True) for hot in hots]
    ldest_ref[...] = jnp.concatenate(rows, axis=0).astype(I32)


def _router(x2, pmods, layer, rows_per_mod, tm, w_router, b_router):
    N, D = x2.shape
    n_t = N // tm
    wr_hi = w_router.T.astype(BF16)
    wr_lo = (w_router.T - wr_hi.astype(F32)).astype(BF16)
    mod = lambda which: pl.BlockSpec((None, None, None, 1, D),
                                     lambda i: (layer, which, (i * tm) // rows_per_mod, 0, 0))
    lane_spec = pl.BlockSpec((TOP_K, tm), lambda i: (0, i))
    return pl.pallas_call(
        _router_body,
        grid=(n_t,),
        in_specs=[pl.BlockSpec((tm, D), lambda i: (i, 0)),
                  mod(4), mod(3),
                  pl.BlockSpec((2 * N_EXP, D), lambda i: (0, 0)),
                  pl.BlockSpec((N_EXP, 1), lambda i: (0, 0))],
        out_specs=[pl.BlockSpec((tm, D), lambda i: (i, 0)),
                   lane_spec, lane_spec,
                   pl.BlockSpec((None, 1, N_EXP), lambda i: (i, 0, 0))],
        out_shape=[jax.ShapeDtypeStruct((N, D), BF16),
                   jax.ShapeDtypeStruct((TOP_K, N), I32),
                   jax.ShapeDtypeStruct((TOP_K, N), F32),
                   jax.ShapeDtypeStruct((n_t, 1, N_EXP), I32)],
        compiler_params=_cparams(("arbitrary",)),
        name="router",
    )(x2, pmods, pmods, jnp.concatenate([wr_hi, wr_lo], axis=0), b_router.reshape(N_EXP, 1))


def _expert_body(blk_e_ref, n_used_ref, xs_ref, wgu_ref, bgu_ref, wdn_ref, bdn_ref, o_ref, wgu_s, wdn_s):
    i = pl.program_id(0)
    prev = blk_e_ref[jnp.maximum(i - 1, 0)]
    changed = jnp.logical_or(i == 0, blk_e_ref[i] != prev)

    @pl.when(changed)
    def _():
        wgu_s[...] = wgu_ref[...].astype(BF16)
        wdn_s[...] = wdn_ref[...].astype(BF16)

    @pl.when(i < n_used_ref[0])
    def _():
        h = jnp.dot(xs_ref[...], wgu_s[...], preferred_element_type=F32) + bgu_ref[...]
        gt = jnp.minimum(h[:, :D_FF], SWIGLU_LIMIT)
        up = jnp.clip(h[:, D_FF:], -SWIGLU_LIMIT, SWIGLU_LIMIT)
        act = (up + 1.0) * gt * jax.nn.sigmoid(SWIGLU_ALPHA * gt)
        y = jnp.dot(act.astype(BF16), wdn_s[...], preferred_element_type=F32) + bdn_ref[...]
        o_ref[...] = y.astype(o_ref.dtype)

    @pl.when(i >= n_used_ref[0])
    def _():
        o_ref[...] = jnp.zeros_like(o_ref)


def _experts(xs, blk_e, n_used, layer, w_gu, b_gu, w_dn, b_dn):
    slots, D = xs.shape
    te = EXPERT_TILE
    n_blk = slots // te
    grid_spec = pltpu.PrefetchScalarGridSpec(
        num_scalar_prefetch=2,
        grid=(n_blk,),
        in_specs=[pl.BlockSpec((te, D), lambda i, be, nu: (jnp.minimum(i, nu[0] - 1), 0)),
                  pl.BlockSpec((None, None, D, 2 * D_FF), lambda i, be, nu: (layer, be[i], 0, 0)),
                  pl.BlockSpec((None, None, 1, 2 * D_FF), lambda i, be, nu: (layer, be[i], 0, 0)),
                  pl.BlockSpec((None, None, D_FF, D), lambda i, be, nu: (layer, be[i], 0, 0)),
                  pl.BlockSpec((None, None, 1, D), lambda i, be, nu: (layer, be[i], 0, 0))],
        out_specs=pl.BlockSpec((te, D), lambda i, be, nu: (i, 0)),
        scratch_shapes=[pltpu.VMEM((D, 2 * D_FF), BF16), pltpu.VMEM((D_FF, D), BF16)],
    )
    return pl.pallas_call(
        _expert_body,
        grid_spec=grid_spec,
        out_shape=jax.ShapeDtypeStruct((slots, D), BF16),
        compiler_params=_cparams(("arbitrary",)),
        name="experts",
    )(blk_e, n_used, xs, w_gu, b_gu.reshape(DEPTH, N_EXP, 1, 2 * D_FF), w_dn, b_dn.reshape(DEPTH, N_EXP, 1, D))


def _chunk_copy(vmem_ref, hbm_ref, sem, k, dst_chunk, to_hbm):
    row = k * RUN_ALIGN if isinstance(k, int) else pl.multiple_of(k * RUN_ALIGN, RUN_ALIGN)
    local = vmem_ref.at[pl.ds(row, RUN_ALIGN)]
    remote = hbm_ref.at[pl.ds(pl.multiple_of(dst_chunk * RUN_ALIGN, RUN_ALIGN), RUN_ALIGN)]
    return pltpu.make_async_copy(local, remote, sem) if to_hbm else pltpu.make_async_copy(remote, local, sem)


def _dispatch_body(dst_ref, tail_ref, nu_ref, *rest, n_chunks, tiles, n_blk):
    n_p = len(tiles)
    xs_ref, buf_s, zero_s, sem = rest[2 * n_p:]
    i = pl.program_id(0)
    n_t = pl.num_programs(0)
    slot = i % 2
    lb = n_chunks * RUN_ALIGN
    te = zero_s.shape[0]
    rb = 256

    def sort_rows(u_ref, ld_ref):
        tm = u_ref.shape[0]
        ld = ld_ref[...]
        u = u_ref[...]
        for j in range(lb // rb):
            r = lax.broadcasted_iota(I32, (rb, tm), 0) + j * rb
            p = jnp.where(r == ld[0:1], 1.0, jnp.where(r == ld[1:2], 1.0, jnp.where(
                r == ld[2:3], 1.0, jnp.where(r == ld[3:4], 1.0, 0.0))))
            buf_s[slot, j * rb:(j + 1) * rb, :] = jnp.dot(p.astype(BF16), u,
                                                          preferred_element_type=F32).astype(BF16)
            for k in range(j * rb // RUN_ALIGN, (j + 1) * rb // RUN_ALIGN):
                _chunk_copy(buf_s.at[slot], xs_ref, sem.at[slot], k, dst_ref[i, k], True).start()

    for p, (first, count) in enumerate(tiles):
        @pl.when(jnp.logical_and(i >= first, i < first + count))
        def _(p=p):
            sort_rows(rest[2 * p], rest[2 * p + 1])

    def drain(s):
        pltpu.make_async_copy(buf_s.at[s], xs_ref.at[pl.ds(0, lb)], sem.at[s]).wait()

    @pl.when(i > 0)
    def _():
        drain(1 - slot)

    @pl.when(i == n_t - 1)
    def _():
        drain(slot)
        zero_s[...] = jnp.zeros_like(zero_s)
        n_tail = tail_ref.shape[0]

        def tail_copy(t):
            return _chunk_copy(zero_s, xs_ref, sem.at[2], 0, tail_ref[t], True)

        def block_copy(k):
            return pltpu.make_async_copy(zero_s, xs_ref.at[pl.ds(pl.multiple_of(k * te, te), te)], sem.at[2])

        def each(n, pred, copy, wait):
            def body(t, _):
                @pl.when(pred(t))
                def _():
                    copy(t).wait() if wait else copy(t).start()
                return 0
            lax.fori_loop(0, n, body, 0)

        for wait in (False, True):
            each(n_tail, lambda t: tail_ref[t] >= 0, tail_copy, wait)
            each(n_blk, lambda k: k >= nu_ref[0], block_copy, wait)


def _dispatch(u2s, ldests, dst, tails, n_used, slots, tm, te):
    D = u2s[0].shape[1]
    n_t, n_chunks = dst.shape
    lb = n_chunks * RUN_ALIGN
    tiles, first = [], 0
    for u2 in u2s:
        tiles.append((first, u2.shape[0] // tm))
        first += u2.shape[0] // tm
    assert first == n_t

    def own_tile(first, count):
        return lambda i, d, t, nu: jnp.clip(i - first, 0, count - 1)

    in_specs, operands = [], []
    for (first, count), u2, ld in zip(tiles, u2s, ldests):
        tile = own_tile(first, count)
        in_specs += [pl.BlockSpec((tm, D), lambda i, d, t, nu, tile=tile: (tile(i, d, t, nu), 0)),
                     pl.BlockSpec((TOP_K, tm), lambda i, d, t, nu, tile=tile: (0, tile(i, d, t, nu)))]
        operands += [u2, ld]
    grid_spec = pltpu.PrefetchScalarGridSpec(
        num_scalar_prefetch=3,
        grid=(n_t,),
        in_specs=in_specs,
        out_specs=pl.BlockSpec(memory_space=pl.ANY),
        scratch_shapes=[pltpu.VMEM((2, lb, D), BF16), pltpu.VMEM((te, D), BF16), pltpu.SemaphoreType.DMA((3,))],
    )
    return pl.pallas_call(
        functools.partial(_dispatch_body, n_chunks=n_chunks, tiles=tuple(tiles), n_blk=slots // te),
        grid_spec=grid_spec,
        out_shape=jax.ShapeDtypeStruct((slots, D), BF16),
        compiler_params=_cparams(("arbitrary",)),
        name="moe_dispatch",
    )(dst, tails, n_used, *operands)


def _combine_body(dst_ref, x_ref, ld_ref, gate_ref, g2_ref, lg_ref, lb_ref, ys_ref, o_ref, buf_s, sem, *, n_chunks):
    i = pl.program_id(0)
    n_t = pl.num_programs(0)
    slot = i % 2
    tm = x_ref.shape[0]
    lb = n_chunks * RUN_ALIGN

    def wait_all(s):
        pltpu.make_async_copy(ys_ref.at[pl.ds(0, lb)], buf_s.at[s], sem.at[s]).wait()

    @pl.when(i == 0)
    def _():
        def body(k, _):
            _chunk_copy(buf_s.at[0], ys_ref, sem.at[0], k, dst_ref[0, k], False).start()
            return 0
        lax.fori_loop(0, n_chunks, body, 0, unroll=8)

    wait_all(slot)
    nxt = jnp.minimum(i + 1, n_t - 1)
    ld = ld_ref[...]
    gates = gate_ref[...]
    cb = 512
    y = jnp.zeros((tm, x_ref.shape[1]), F32)
    for j in range(lb // cb):
        for k in range(j * cb // RUN_ALIGN, (j + 1) * cb // RUN_ALIGN):
            _chunk_copy(buf_s.at[1 - slot], ys_ref, sem.at[1 - slot], k, dst_ref[nxt, k], False).start()
        c = lax.broadcasted_iota(I32, (tm, cb), 1) + j * cb
        w = jnp.zeros((tm, cb), F32)
        for k in range(TOP_K):
            w = jnp.where(c == ld[:, k:k + 1], gates[:, k:k + 1], w)
        y = y + jnp.dot(w.astype(BF16), buf_s[slot, j * cb:(j + 1) * cb, :], preferred_element_type=F32)
    v = ALPHA_DN * x_ref[...] + g2_ref[...] * y
    o_ref[...] = _layer_norm(v, lg_ref[...], lb_ref[...])

    @pl.when(i == n_t - 1)
    def _():
        wait_all(1 - slot)


def _combine(x2, ldest_col, gates_col, dst, ys, pmods, layer, rows_per_mod, tm, ln_g, ln_b):
    N, D = x2.shape
    n_t, n_chunks = dst.shape
    lb = n_chunks * RUN_ALIGN
    vec = pl.BlockSpec((1, D), lambda i, d: (0, 0))
    grid_spec = pltpu.PrefetchScalarGridSpec(
        num_scalar_prefetch=1,
        grid=(n_t,),
        in_specs=[pl.BlockSpec((tm, D), lambda i, d: (i, 0)),
                  pl.BlockSpec((tm, TOP_K), lambda i, d: (i, 0)),
                  pl.BlockSpec((tm, TOP_K), lambda i, d: (i, 0)),
                  pl.BlockSpec((None, None, None, 1, D), lambda i, d: (layer, 5, (i * tm) // rows_per_mod, 0, 0)),
                  vec, vec,
                  pl.BlockSpec(memory_space=pl.ANY)],
        out_specs=pl.BlockSpec((tm, D), lambda i, d: (i, 0)),
        scratch_shapes=[pltpu.VMEM((2, lb, D), BF16), pltpu.SemaphoreType.DMA((2,))],
    )
    return pl.pallas_call(
        functools.partial(_combine_body, n_chunks=n_chunks),
        grid_spec=grid_spec,
        out_shape=jax.ShapeDtypeStruct((N, D), F32),
        compiler_params=_cparams(("arbitrary",)),
        name="moe_combine",
    )(dst, x2, ldest_col, gates_col, pmods, ln_g.reshape(1, D), ln_b.reshape(1, D), ys)


def _moe_plan(cnts, n_chunks, te, spare_chunk):
    a = RUN_ALIGN
    cnt = jnp.concatenate(cnts, axis=0)
    pc = (cnt + a - 1) // a * a
    seg = (jnp.sum(pc, axis=0) + te - 1) // te * te
    pad_end = jnp.cumsum(seg)
    run_start = (pad_end - seg)[None, :] + jnp.cumsum(pc, axis=0) - pc
    lo = jnp.cumsum(pc, axis=1) - pc
    tables, t0 = [], 0
    for c, nc in zip(cnts, n_chunks):
        sl = slice(t0, t0 + c.shape[0])
        parity = ((t0 + jnp.arange(c.shape[0], dtype=I32)) % 2)[:, None]
        t0 += c.shape[0]
        pos = jnp.arange(nc, dtype=I32) * a
        owner = jnp.sum(((lo[sl] + pc[sl])[:, None, :] <= pos[None, :, None]).astype(I32), axis=-1)
        mine = owner[..., None] == jnp.arange(N_EXP, dtype=I32)
        base = jnp.sum(jnp.where(mine, (run_start[sl] - lo[sl])[:, None, :], 0), axis=-1)
        valid = pos[None, :] < jnp.sum(pc[sl], axis=1, keepdims=True)
        spare = spare_chunk + parity * nc + jnp.arange(nc, dtype=I32)[None, :]
        tables.append(jnp.where(valid, (base + pos[None, :]) // a, spare).astype(I32))
    tail_row = (pad_end - seg + jnp.sum(pc, axis=0))[:, None] + jnp.arange(te // a, dtype=I32)[None, :] * a
    tails = jnp.where(tail_row < pad_end[:, None], tail_row // a, -1).astype(I32).reshape(-1)
    return tables, tails, pad_end


def _moe(passes, layer, P):
    te = EXPERT_TILE
    routed = []
    for x, pmods, shared_mod in passes:
        B, T, D = x.shape
        N = B * T
        tm = min(MOE_TILE, N if shared_mod else T)
        x2 = x.reshape(N, D)
        u2, ldest, gates, cnt = _router(x2, pmods, layer, T, tm, P['w_router'][layer], P['b_router'][layer])
        routed.append((x2, pmods, T, tm, u2, ldest, gates, cnt[:, 0, :], x.shape))
    n_chunks = [r[3] * TOP_K // RUN_ALIGN + N_EXP for r in routed]
    n_tiles = [r[7].shape[0] for r in routed]
    n_assign = sum(r[0].shape[0] for r in routed) * TOP_K
    seg_rows = -(-(n_assign + sum(n_tiles) * N_EXP * RUN_ALIGN + N_EXP * te) // te) * te
    slots = seg_rows + -(-(2 * max(n_chunks) * RUN_ALIGN) // te) * te
    dsts, tails, pad_end = _moe_plan([r[7] for r in routed], n_chunks, te, seg_rows // RUN_ALIGN)
    n_blk = slots // te
    blk_pos = jnp.arange(n_blk, dtype=I32) * te
    blk_e = jnp.minimum(jnp.sum((pad_end[None, :] <= blk_pos[:, None]).astype(I32), axis=1), N_EXP - 1)
    n_used = (pad_end[-1] // te).astype(I32).reshape(1)
    assert len(set(r[3] for r in routed)) == 1, "all passes must use the same MoE tile"
    xs = _dispatch([r[4] for r in routed], [r[5] for r in routed], jnp.concatenate(dsts, axis=0), tails, n_used,
                   slots, routed[0][3], te)
    ys = _experts(xs, blk_e, n_used, layer, P['w_gu'], P['b_gu'], P['w_down'], P['b_down'])
    outs = []
    for r, dst in zip(routed, dsts):
        x2, pmods, T, tm, _, ldest, gates, _, shape = r
        y = _combine(x2, ldest.T, gates.T, dst, ys, pmods, layer, T, tm, P['ln_g'][layer, 1], P['ln_b'][layer, 1])
        outs.append(y.reshape(shape))
    return outs


def _split3(x):
    hi = x.astype(BF16)
    r1 = x - hi.astype(F32)
    mid = r1.astype(BF16)
    lo = (r1 - mid.astype(F32)).astype(BF16)
    return jnp.concatenate([hi, mid, lo], axis=1)


def _inproj1_body(x_ref, xp_ref, xn_ref, sc_ref, sh_ref, w_ref, wvt_ref, cw_ref, cb_ref, wg2_ref, wgh_ref,
                  bg_ref, qk_ref, vt_ref, o_ref, gcol_ref, grow_ref):
    j = pl.program_id(1)
    n_j = pl.num_programs(1)
    halo = xp_ref.shape[0]
    tm = x_ref.shape[0]
    nh = 4 * H_C

    def adaln(v):
        return v * (1.0 + sc_ref[...]) + sh_ref[...]

    u = adaln(x_ref[...])
    ub = u.astype(BF16)
    u_prev = jnp.where(j > 0, adaln(xp_ref[...]), 0.0).astype(BF16)
    u_next = jnp.where(j < n_j - 1, adaln(xn_ref[...]), 0.0).astype(BF16)
    u_ext = jnp.concatenate([u_prev, ub, u_next], axis=0)
    n_ext = tm + 2 * halo
    cblk = 256
    n_blk = 2 * D_MODEL // cblk

    def matmuls(b):
        c0 = b * cblk
        zqk = jnp.dot(u_ext, w_ref[:, c0:c0 + cblk], preferred_element_type=F32)
        if c0 < D_MODEL:
            vt = lax.dot_general(wvt_ref[c0:c0 + cblk, :], ub, (((1,), (1,)), ((), ())), preferred_element_type=F32)
            vt_ref[c0:c0 + cblk, :] = vt.astype(BF16)
        else:
            o0 = c0 - D_MODEL
            o_ref[:, o0:o0 + cblk] = jnp.dot(ub, w_ref[:, 3 * D_MODEL + o0:3 * D_MODEL + o0 + cblk],
                                             preferred_element_type=F32)
        return zqk

    z_next = matmuls(0)
    for b in range(n_blk):
        zqk = z_next
        if b + 1 < n_blk:
            z_next = matmuls(b + 1)
        cols = slice(b * cblk, (b + 1) * cblk)
        acc = (pltpu.roll(zqk, 2, 0) * cw_ref[0:1, cols] + pltpu.roll(zqk, 1, 0) * cw_ref[1:2, cols]
               + zqk * cw_ref[2:3, cols] + pltpu.roll(zqk, n_ext - 1, 0) * cw_ref[3:4, cols])[halo:tm + halo]
        act = jax.nn.silu(acc + cb_ref[:, cols])
        if b * cblk >= D_MODEL:
            act = act * (DH_C ** -0.5)
        qk_ref[:, cols] = act.astype(BF16)
    u_lo = (u - ub.astype(F32)).astype(BF16)
    g2 = jnp.dot(ub, wg2_ref[...], preferred_element_type=F32)
    gc = g2[:, :nh] + g2[:, nh:] + jnp.dot(u_lo, wgh_ref[...], preferred_element_type=F32) + bg_ref[...]
    r_i = lax.broadcasted_iota(I32, (tm, tm), 0)
    c_i = lax.broadcasted_iota(I32, (tm, tm), 1)
    same = (r_i // CHUNK) == (c_i // CHUNK)
    tri_f = jnp.where(jnp.logical_and(same, c_i <= r_i), 1.0, 0.0).astype(BF16)
    tri_b = jnp.where(jnp.logical_and(same, c_i >= r_i), 1.0, 0.0).astype(BF16)
    lf3 = _split3(jax.nn.log_sigmoid(gc))

    def sum3(p, axis):
        if axis == 1:
            return p[:, :nh] + p[:, nh:2 * nh] + p[:, 2 * nh:]
        return p[:nh] + p[nh:2 * nh] + p[2 * nh:]

    col_i = lax.broadcasted_iota(I32, (tm, nh), 1)
    cum = jnp.where(col_i < 2 * H_C,
                    sum3(jnp.dot(tri_f, lf3, preferred_element_type=F32), 1),
                    sum3(jnp.dot(tri_b, lf3, preferred_element_type=F32), 1))
    gcol = jnp.where((col_i // H_C) % 2 == 1, cum, gc)
    gcol_ref[...] = gcol
    eye = jnp.where(r_i == c_i, 1.0, 0.0).astype(BF16)
    gt3 = lax.dot_general(_split3(gcol), eye, (((0,), (0,)), ((), ())), preferred_element_type=F32)
    grow_ref[...] = sum3(gt3, 0)


def _inproj1(x, mods, w_in1, w_gate1, b_gate1, conv_w, conv_b, row_off, row_stride):
    B, T, D = x.shape
    tm = min(TOKEN_TILE, T)
    n_j = T // tm
    nh = 4 * H_C
    halo = 8
    wg_hi = w_gate1.astype(BF16)
    wg_lo = (w_gate1 - wg_hi.astype(F32)).astype(BF16)
    w_bf = w_in1.astype(BF16)
    tok = lambda w: pl.BlockSpec((None, tm, w), lambda b, j: (b, j, 0))
    full = lambda r, c: pl.BlockSpec((r, c), lambda b, j: (0, 0))
    return pl.pallas_call(
        _inproj1_body,
        grid=(B, n_j),
        in_specs=[tok(D),
                  pl.BlockSpec((None, halo, D), lambda b, j: (b, jnp.maximum(j * (tm // halo) - 1, 0), 0)),
                  pl.BlockSpec((None, halo, D), lambda b, j: (b, jnp.minimum((j + 1) * (tm // halo), T // halo - 1), 0)),
                  _mod_spec(1, 1, row_off, row_stride),
                  _mod_spec(1, 0, row_off, row_stride),
                  full(D, 4 * D), full(D, D), full(CONV_W, 2 * D), full(1, 2 * D),
                  full(D, 2 * nh), full(D, nh), full(1, nh)],
        out_specs=[tok(2 * D),
                   pl.BlockSpec((None, D, tm), lambda b, j: (b, 0, j)),
                   tok(D), tok(nh),
                   pl.BlockSpec((nh, tm), lambda b, j: (0, b * n_j + j))],
        out_shape=[jax.ShapeDtypeStruct((B, T, 2 * D), BF16),
                   jax.ShapeDtypeStruct((B, D, T), BF16),
                   jax.ShapeDtypeStruct((B, T, D), F32),
                   jax.ShapeDtypeStruct((B, T, nh), F32),
                   jax.ShapeDtypeStruct((nh, B * T), F32)],
        compiler_params=_cparams(("arbitrary", "arbitrary")),
        name="inproj1",
    )(x, x, x, mods, mods, w_bf, w_bf[:, 2 * D:3 * D].T, conv_w, conv_b.reshape(1, 2 * D),
      jnp.concatenate([wg_hi, wg_lo], axis=1), wg_hi, b_gate1.reshape(1, nh))


def _mlstm_body(q_ref, k_ref, vt_ref, gcol_ref, grow_ref, c0_ref, n0_ref, m0_ref,
                h_ref, cfin_ref, nfin_ref, mfin_ref, hb_s, *, T):
    n_c = T // CHUNK
    L = CHUNK

    rb = min(T, 512)
    t_i = lax.broadcasted_iota(I32, (L, L), 0)
    s_i = lax.broadcasted_iota(I32, (L, L), 1)

    cfin_ref[...] = c0_ref[...]
    nfin_ref[...] = n0_ref[...]
    mfin_ref[...] = m0_ref[...]

    def chunk(d, c):
        mask = (s_i <= t_i) if d == 0 else (s_i >= t_i)
        last = L - 1 if d == 0 else 0
        off = pl.multiple_of(c * L, L)
        qc = q_ref[pl.ds(off, L), :]
        kc = k_ref[pl.ds(off, L), :]
        vt = vt_ref[:, pl.ds(off, L)]
        gcol = gcol_ref[pl.ds(off, L), :]
        grow = grow_ref[:, pl.ds(off, L)]
        b_c = gcol[:, 2 * d + 1:2 * d + 2]
        li_r = grow[2 * d:2 * d + 1, :]
        b_r = grow[2 * d + 1:2 * d + 2, :]
        m_prev = mfin_ref[d]
        log_d = jnp.where(mask, b_c - b_r + li_r, -jnp.inf)
        m_inter = b_c + m_prev
        m_t = jnp.maximum(m_inter, jnp.max(log_d, axis=-1, keepdims=True))
        dmat = jnp.exp(log_d - m_t)
        w_inter = jnp.exp(m_inter - m_t)
        c_old = cfin_ref[d]
        s = lax.dot_general(qc, kc, (((1,), (1,)), ((), ())), preferred_element_type=F32) * dmat
        inter = lax.dot_general(qc, c_old.astype(BF16), (((1,), (1,)), ((), ())), preferred_element_type=F32)
        num = w_inter * inter + lax.dot_general(s.astype(BF16), vt, (((1,), (1,)), ((), ())),
                                                preferred_element_type=F32)
        qn = jnp.sum(qc.astype(F32) * nfin_ref[d], axis=-1, keepdims=True)
        den = w_inter * qn + jnp.sum(s, axis=-1, keepdims=True)
        h = num / jnp.maximum(jnp.abs(den), jnp.exp(-m_t))
        m_new = m_t[last:last + 1, :]
        b_last = b_c[last:last + 1, :]
        w_s = jnp.exp(b_last - b_r + li_r - m_new)
        decay = jnp.exp(b_last + m_prev - m_new)
        wvt = (vt.astype(F32) * w_s).astype(BF16)
        cfin_ref[d] = decay * c_old + jnp.dot(wvt, kc, preferred_element_type=F32)
        wk = jnp.dot(jnp.broadcast_to(w_s, (8, L)).astype(BF16), kc, preferred_element_type=F32)[0:1]
        nfin_ref[d] = decay * nfin_ref[d] + wk
        mfin_ref[d] = m_new
        return off, h

    def both(ci, _):
        off_f, h_f = chunk(0, ci)
        off_b, h_b = chunk(1, n_c - 1 - ci)
        h_ref[pl.ds(off_f, L), :] = h_f
        hb_s[pl.ds(off_b, L), :] = h_b
        return 0

    lax.fori_loop(0, n_c, both, 0)

    def add_bwd(r, _):
        off = pl.multiple_of(r * rb, rb)
        h_ref[pl.ds(off, rb), :] = h_ref[pl.ds(off, rb), :] + hb_s[pl.ds(off, rb), :]
        return 0

    lax.fori_loop(0, T // rb, add_bwd, 0)


def _mlstm(qk, vt, gcol, grow, c0, n0, m0):
    B, _, T = vt.shape
    DH = DH_C
    st = lambda *tail: pl.BlockSpec((None, 2, None) + tail, lambda b, h: (b, 0, h) + (0,) * len(tail))
    return pl.pallas_call(
        functools.partial(_mlstm_body, T=T),
        grid=(B, H_C),
        in_specs=[pl.BlockSpec((None, T, DH), lambda b, h: (b, 0, h)),
                  pl.BlockSpec((None, T, DH), lambda b, h: (b, 0, H_C + h)),
                  pl.BlockSpec((None, DH, T), lambda b, h: (b, h, 0)),
                  pl.BlockSpec((None, None, T, 4), lambda b, h: (b, h, 0, 0)),
                  pl.BlockSpec((None, None, 4, T), lambda b, h: (b, h, 0, 0)),
                  st(DH, DH), st(1, DH), st(1, 1)],
        out_specs=[pl.BlockSpec((None, T, DH), lambda b, h: (b, 0, h)),
                   st(DH, DH), st(1, DH), st(1, 1)],
        out_shape=[jax.ShapeDtypeStruct((B, T, D_MODEL), F32),
                   jax.ShapeDtypeStruct((B, 2, H_C, DH, DH), F32),
                   jax.ShapeDtypeStruct((B, 2, H_C, 1, DH), F32),
                   jax.ShapeDtypeStruct((B, 2, H_C, 1, 1), F32)],
        scratch_shapes=[pltpu.VMEM((T, DH), F32)],
        compiler_params=_cparams(("arbitrary", "arbitrary")),
        name="mlstm",
    )(qk, qk, vt, gcol, grow, c0, n0, m0)


def _outproj1_body(x_ref, h_ref, o_ref, g1_ref, w_ref, lg_ref, lb_ref, out_ref):
    y = jax.nn.sigmoid(o_ref[...]) * h_ref[...]
    out = jnp.dot(y.astype(BF16), w_ref[...], preferred_element_type=F32)
    v = ALPHA_DN * x_ref[...] + g1_ref[...] * out
    out_ref[...] = _layer_norm(v, lg_ref[...], lb_ref[...])


def _outproj1(x, h, o, mods, w_out1, ln_g, ln_b, row_off, row_stride):
    B, T, D = x.shape
    tm = min(TOKEN_TILE, T)
    tok = pl.BlockSpec((None, tm, D), lambda b, j: (b, j, 0))
    vec = pl.BlockSpec((1, D), lambda b, j: (0, 0))
    return pl.pallas_call(
        _outproj1_body,
        grid=(B, T // tm),
        in_specs=[tok, tok, tok, _mod_spec(1, 2, row_off, row_stride),
                  pl.BlockSpec((D, D), lambda b, j: (0, 0)), vec, vec],
        out_specs=tok,
        out_shape=jax.ShapeDtypeStruct((B, T, D), F32),
        compiler_params=_cparams(("arbitrary", "arbitrary")),
        name="outproj1",
    )(x, h, o, mods, w_out1.astype(BF16), ln_g.reshape(1, D), ln_b.reshape(1, D))


def _to_col_major(x):
    B, T, C = x.shape
    rows = T // GRID_W
    return x.reshape(B, rows, GRID_W, C).transpose(0, 2, 1, 3).reshape(B, T, C)


def _to_row_major(x):
    B, T, C = x.shape
    rows = T // GRID_W
    return x.reshape(B, GRID_W, rows, C).transpose(0, 2, 1, 3).reshape(B, T, C)


def _mixer0(x, mods, st, P, s5_mats):
    h_rg, s_re, s_im = st
    xa, ga, ub = _inproj0(x, mods, P['w_in0'][0].astype(BF16))
    h, h_fin = _rglru(xa, P['conv_a_w'][0], P['conv_a_b'][0], P['rg_wa'][0], P['rg_ba'][0],
                      P['rg_wi'][0], P['rg_bi'][0], P['rg_lam'][0], h_rg[:, 0].transpose(1, 0, 2))
    y5, s_fin = _s5(ub, s5_mats, _s5_state_to_lanes(s_re[:, 0], s_im[:, 0]))
    x = _outproj0(x, h, ga, y5, ub, mods, P['s5_d'][0], P['glu_w'][0], P['glu_b'][0], P['w_out0'][0],
                  P['ln_g'][0, 0], P['ln_b'][0, 0])
    new_re, new_im = _s5_lanes_to_state(s_fin)
    return x, (h_fin.transpose(1, 0, 2)[:, None], new_re[:, None], new_im[:, None])


def _mixer1(x, mods, st, P):
    m_c, m_n, m_m = st
    B, T, D = x.shape
    qk, v, o, gcol, grow = _inproj1(x, mods, P['w_in1'][0], P['w_gate1'][0], P['b_gate1'][0],
                                    P['conv_c_w'][0], P['conv_c_b'][0], 0, 1)
    gcol = gcol.reshape(B, T, 2, 2, H_C).transpose(0, 4, 1, 2, 3).reshape(B, H_C, T, 4)
    grow = grow.reshape(2, 2, H_C, B, T).transpose(3, 2, 0, 1, 4).reshape(B, H_C, 4, T)
    h, c_fin, n_fin, m_fin = _mlstm(qk, v, gcol, grow,
                                    m_c[:, 0], m_n[:, 0][:, :, :, None, :], m_m[:, 0][:, :, :, None, None])
    x = _outproj1(x, h, o, mods, P['w_out1'][0], P['ln_g'][1, 0], P['ln_b'][1, 0], 0, 1)
    return x, (c_fin[:, None], n_fin[:, None, :, :, 0, :], m_fin[:, None, :, :, 0, 0])


def _forward(x_prompt, x_sample, c, c_ctx, states, P):
    bp = x_prompt.shape[0]
    bs = x_sample.shape[0]
    rows = 1 + bs
    rpad = -(-rows // 8) * 8
    cv = jnp.concatenate([c_ctx[None, :], c, jnp.zeros((rpad - rows, D_MODEL), F32)], axis=0)
    mods = _modulation(cv, P['w_mod'], P['b_mod'])
    s5_mats = _s5_matrices(P['s5_a_re'][0], P['s5_a_im'][0], P['s5_log_dt'][0], P['s5_b_re'][0], P['s5_b_im'][0],
                           P['s5_c_re'][0], P['s5_c_im'][0])
    zero_state = (jnp.zeros((bp, 1, 2, D_A), F32),
                  jnp.zeros((bp, 1, 2, G_B, P_B), F32),
                  jnp.zeros((bp, 1, 2, G_B, P_B), F32),
                  jnp.zeros((bp, 1, 2, H_C, DH_C, DH_C), F32),
                  jnp.zeros((bp, 1, 2, H_C, DH_C), F32),
                  jnp.zeros((bp, 1, 2, H_C), F32))
    mods_ctx = jnp.broadcast_to(mods[:, :, 0:1], mods.shape[:2] + (bp,) + mods.shape[3:])
    mods_lat = mods[:, :, 1:1 + bs]
    xc, new_even = _mixer0(x_prompt, mods_ctx, zero_state[:3], P, s5_mats)
    xl, _ = _mixer0(x_sample, mods_lat, states[:3], P, s5_mats)
    xc, xl = _moe([(xc, mods_ctx, True), (xl, mods_lat, False)], 0, P)
    xl = _to_col_major(xl)
    xc, new_odd = _mixer1(xc, mods_ctx, zero_state[3:], P)
    xl, _ = _mixer1(xl, mods_lat, states[3:], P)
    xc, xl = _moe([(xc, mods_ctx, True), (xl, mods_lat, False)], 1, P)
    return (xc, _to_row_major(xl)) + tuple(new_even) + tuple(new_odd)


def kernel(x_prompt, x_sample, c, c_ctx, state_rglru, state_s5_re, state_s5_im, state_mlstm_C, state_mlstm_n, state_mlstm_m, w_mod, b_mod, ln_g, ln_b, w_in0, conv_a_w, conv_a_b, rg_wa, rg_ba, rg_wi, rg_bi, rg_lam, s5_a_re, s5_a_im, s5_log_dt, s5_b_re, s5_b_im, s5_c_re, s5_c_im, s5_d, glu_w, glu_b, w_out0, w_in1, w_gate1, b_gate1, conv_c_w, conv_c_b, w_out1, w_router, b_router, w_gu, b_gu, w_down, b_down):
    P = dict(w_mod=w_mod, b_mod=b_mod, ln_g=ln_g, ln_b=ln_b, w_in0=w_in0, conv_a_w=conv_a_w,
             conv_a_b=conv_a_b, rg_wa=rg_wa, rg_ba=rg_ba, rg_wi=rg_wi, rg_bi=rg_bi, rg_lam=rg_lam,
             s5_a_re=s5_a_re, s5_a_im=s5_a_im, s5_log_dt=s5_log_dt, s5_b_re=s5_b_re, s5_b_im=s5_b_im,
             s5_c_re=s5_c_re, s5_c_im=s5_c_im, s5_d=s5_d, glu_w=glu_w, glu_b=glu_b, w_out0=w_out0,
             w_in1=w_in1, w_gate1=w_gate1, b_gate1=b_gate1, conv_c_w=conv_c_w, conv_c_b=conv_c_b,
             w_out1=w_out1, w_router=w_router, b_router=b_router, w_gu=w_gu, b_gu=b_gu,
             w_down=w_down, b_down=b_down)
    states = (state_rglru, state_s5_re, state_s5_im, state_mlstm_C, state_mlstm_n, state_mlstm_m)
    return _forward(x_prompt, x_sample, c, c_ctx, states, P)
```

```python
import functools

import jax
import jax.numpy as jnp
from jax import lax
from jax.experimental import pallas as pl
from jax.experimental.pallas import tpu as pltpu

F32 = jnp.float32
BF16 = jnp.bfloat16
I32 = jnp.int32
HI = lax.Precision.HIGHEST

D_MODEL = 1024
DEPTH = 2
GRID_W = 64
D_A = 512
NB_A = 8
BS_A = D_A // NB_A
CONV_W = 4
LRU_C = 8.0
D_B = 512
S5_GROUP = 16
G_B = D_B // S5_GROUP
P_B = 64
H_C = 4
DH_C = D_MODEL // H_C
CHUNK = 128
N_EXP = 32
TOP_K = 4
D_FF = D_MODEL
SWIGLU_LIMIT = 7.0
SWIGLU_ALPHA = 1.702
ALPHA_DN = (2 * DEPTH) ** 0.25
LN_EPS = 1e-5

S5_L = 16
S5_W = S5_L * S5_GROUP
S5_OCT = 8
TOKEN_TILE = 256
EXPERT_TILE = 512
MOE_TILE = 512
RUN_ALIGN = 16
VMEM_LIMIT = 56 * 1024 * 1024


def _cparams(sem, vmem=VMEM_LIMIT):
    return pltpu.CompilerParams(dimension_semantics=sem, vmem_limit_bytes=vmem)


def _layer_norm(v, g, b):
    mu = jnp.mean(v, axis=-1, keepdims=True)
    c = v - mu
    var = jnp.mean(c * c, axis=-1, keepdims=True)
    return c * lax.rsqrt(var + LN_EPS) * g + b


def _mod_spec(layer, which, row_off, row_stride):
    return pl.BlockSpec((None, None, None, 1, D_MODEL),
                        lambda b, j: (layer, which, row_off + b * row_stride, 0, 0))


def _mod_spec_all(layer, which, nb):
    return pl.BlockSpec((None, None, nb, 1, D_MODEL), lambda i: (layer, which, 0, 0, 0))


def _row_permutation(n_out_major, n_out_minor):
    n = n_out_major * n_out_minor
    r_out = lax.broadcasted_iota(I32, (n, n), 0)
    r_in = lax.broadcasted_iota(I32, (n, n), 1)
    hit = jnp.logical_and(r_out // n_out_minor == r_in % n_out_major, r_out % n_out_minor == r_in // n_out_major)
    return jnp.where(hit, 1.0, 0.0).astype(BF16)


def _mod_body(c_ref, w_ref, b_ref, o_ref):
    s = jax.nn.silu(c_ref[...])
    o_ref[...] = jnp.dot(s, w_ref[...], precision=HI, preferred_element_type=F32) + b_ref[...]


def _modulation(cv, w_mod, b_mod):
    R, D = cv.shape
    L, _, N6 = w_mod.shape
    tn = N6 // 6
    out = pl.pallas_call(
        _mod_body,
        grid=(L, N6 // tn),
        in_specs=[pl.BlockSpec((R, D), lambda l, j: (0, 0)),
                  pl.BlockSpec((None, D, tn), lambda l, j: (l, 0, j)),
                  pl.BlockSpec((None, 1, tn), lambda l, j: (l, 0, j))],
        out_specs=pl.BlockSpec((None, R, tn), lambda l, j: (l, 0, j)),
        out_shape=jax.ShapeDtypeStruct((L, R, N6), F32),
        compiler_params=_cparams(("arbitrary", "arbitrary")),
        name="modulation",
    )(cv, w_mod, b_mod.reshape(L, 1, N6))
    return out.reshape(L, R, 6, D).transpose(0, 2, 1, 3)[:, :, :, None, :]


def _inproj0_body(x_ref, sc_ref, sh_ref, w_ref, xa_ref, ga_ref, ub_ref):
    nb, tq, d = x_ref.shape
    u = x_ref[...] * (1.0 + sc_ref[...]) + sh_ref[...]
    ub = u.reshape(nb * tq, d).astype(BF16)
    ut = jnp.dot(_row_permutation(tq, nb), ub, preferred_element_type=F32).astype(BF16)
    z = jnp.dot(ut, w_ref[...], preferred_element_type=F32).reshape(tq, nb, w_ref.shape[1])
    xa_ref[...] = z[:, :, :D_A]
    ga_ref[...] = z[:, :, D_A:2 * D_A]
    ub_ref[...] = z[:, :, 2 * D_A:]


def _inproj0(x, pmods, w_in0):
    B, T, D = x.shape
    tq = TOKEN_TILE // B
    nz = w_in0.shape[1]
    tmaj = pl.BlockSpec((tq, B, D_A), lambda i: (i, 0, 0))
    return pl.pallas_call(
        _inproj0_body,
        grid=(T // tq,),
        in_specs=[pl.BlockSpec((B, tq, D), lambda i: (0, i, 0)),
                  _mod_spec_all(0, 1, B), _mod_spec_all(0, 0, B),
                  pl.BlockSpec((D, nz), lambda i: (0, 0))],
        out_specs=[tmaj, tmaj, tmaj],
        out_shape=[jax.ShapeDtypeStruct((T, B, D_A), F32)] * 3,
        compiler_params=_cparams(("arbitrary",)),
        name="inproj0",
    )(x, pmods, pmods, w_in0)


def _rglru_body(cur_ref, prev_ref, next_ref, cw_ref, cb_ref, wa_ref, wi_ref, ba_ref, bi_ref, lam_ref, h0_ref,
                h_ref, hfin_ref, ext_s, a_s, b_s, carry_s, *, tt, n_t):
    d = pl.program_id(0)
    j = pl.program_id(1)
    jj = j + d * (n_t - 1 - 2 * j)
    nb = cur_ref.shape[1]
    half = D_A // 2
    ext_s[0:2] = jnp.where(jj == 0, 0.0, prev_ref[...])
    ext_s[2:tt + 2] = cur_ref[...]
    ext_s[tt + 2:tt + 3] = jnp.where(jj == n_t - 1, 0.0, next_ref[...])
    xc = ext_s[0:tt] * cw_ref[0]
    for k in range(1, CONV_W):
        xc = xc + ext_s[k:k + tt] * cw_ref[k]
    xc = xc + cb_ref[...]
    x2 = xc.reshape(tt * nb, D_A)
    xb = x2.astype(BF16)

    def gate(w_ref, bias_ref):
        lo = jnp.dot(xb[:, :half], w_ref[0], preferred_element_type=F32)
        hi = jnp.dot(xb[:, half:], w_ref[1], preferred_element_type=F32)
        return jax.nn.sigmoid(jnp.concatenate([lo, hi], axis=1) + bias_ref[...])

    r = gate(wa_ref, ba_ref)
    i = gate(wi_ref, bi_ref)
    log_a = LRU_C * r * jax.nn.log_sigmoid(lam_ref[...])
    a = jnp.exp(log_a)
    one_minus_a2 = -jnp.tanh(log_a) * (a * a + 1.0)
    a_s[...] = a.reshape(tt, nb, D_A)
    b_s[...] = (jnp.sqrt(one_minus_a2) * (i * x2)).reshape(tt, nb, D_A)

    @pl.when(j == 0)
    def _():
        carry_s[...] = h0_ref[...]

    def step(t, h):
        h = a_s[t] * h + b_s[t]
        h_ref[t] = h
        return h

    @pl.when(d == 0)
    def _():
        carry_s[...] = lax.fori_loop(0, tt, step, carry_s[...], unroll=8)

    @pl.when(d == 1)
    def _():
        carry_s[...] = lax.fori_loop(0, tt, lambda t, h: step(tt - 1 - t, h), carry_s[...], unroll=8)

    hfin_ref[...] = carry_s[...]


def _block_diag_halves(w):
    nd = w.shape[0]
    per = NB_A // 2
    w = w.reshape(nd, 2, per, BS_A, BS_A)
    eye = jnp.eye(per, dtype=w.dtype)
    out = jnp.einsum('dhncz,nm->dhncmz', w, eye)
    return out.reshape(nd, 2, per * BS_A, per * BS_A)


def _rglru(x3, conv_w, conv_b, wa, ba, wi, bi, lam, h0):
    T, B, _ = x3.shape
    tt = min(T, 1024 // B)
    n_t = T // tt

    def cur_map(d, j):
        return (j + d * (n_t - 1 - 2 * j), 0, 0)

    def prev_map(d, j):
        jj = j + d * (n_t - 1 - 2 * j)
        return (jnp.maximum(jj * (tt // 2) - 1, 0), 0, 0)

    def next_map(d, j):
        jj = j + d * (n_t - 1 - 2 * j)
        return (jnp.minimum((jj + 1) * tt, T - 1), 0, 0)

    dir_spec3 = pl.BlockSpec((None, 1, D_A), lambda d, j: (d, 0, 0))
    h, hfin = pl.pallas_call(
        functools.partial(_rglru_body, tt=tt, n_t=n_t),
        grid=(2, n_t),
        in_specs=[pl.BlockSpec((tt, B, D_A), cur_map),
                  pl.BlockSpec((2, B, D_A), prev_map),
                  pl.BlockSpec((1, B, D_A), next_map),
                  pl.BlockSpec((CONV_W, 1, D_A), lambda d, j: (0, 0, 0)),
                  pl.BlockSpec((1, 1, D_A), lambda d, j: (0, 0, 0)),
                  pl.BlockSpec((None, 2, D_A // 2, D_A // 2), lambda d, j: (d, 0, 0, 0)),
                  pl.BlockSpec((None, 2, D_A // 2, D_A // 2), lambda d, j: (d, 0, 0, 0)),
                  dir_spec3, dir_spec3, dir_spec3,
                  pl.BlockSpec((None, B, D_A), lambda d, j: (d, 0, 0))],
        out_specs=[pl.BlockSpec((None, tt, B, D_A), lambda d, j: (d,) + cur_map(d, j)),
                   pl.BlockSpec((None, B, D_A), lambda d, j: (d, 0, 0))],
        out_shape=[jax.ShapeDtypeStruct((2, T, B, D_A), F32),
                   jax.ShapeDtypeStruct((2, B, D_A), F32)],
        scratch_shapes=[pltpu.VMEM((tt + 3, B, D_A), F32),
                        pltpu.VMEM((tt, B, D_A), F32),
                        pltpu.VMEM((tt, B, D_A), F32),
                        pltpu.VMEM((B, D_A), F32)],
        compiler_params=_cparams(("arbitrary", "arbitrary")),
        name="rglru",
    )(x3, x3, x3, conv_w.reshape(CONV_W, 1, D_A), conv_b.reshape(1, 1, D_A),
      _block_diag_halves(wa).astype(BF16), _block_diag_halves(wi).astype(BF16),
      ba.reshape(2, 1, D_A), bi.reshape(2, 1, D_A), lam.reshape(2, 1, D_A), h0)
    return h, hfin


def _s5_matrices(a_re, a_im, log_dt, b_re, b_im, c_re, c_im):
    L = S5_L
    lam = lax.complex(a_re.astype(F32), a_im.astype(F32))
    dt = jnp.exp(log_dt.astype(F32))[..., None]
    ldt = lam * dt
    a_bar = jnp.exp(ldt)
    b_bar = ((a_bar - 1.0) / lam)[..., None] * lax.complex(b_re.astype(F32), b_im.astype(F32))
    cc = lax.complex(c_re.astype(F32), c_im.astype(F32))
    ks = jnp.arange(L + 1, dtype=F32)
    pw = jnp.exp(ldt[:, :, None, :] * ks[None, None, :, None].astype(jnp.complex64))
    kern = jnp.real(jnp.einsum('dgjp,dgkp,dgpi->dgkji', cc, pw[:, :, :L], b_bar, precision=HI))
    s_idx = jnp.arange(L)[:, None]
    t_idx = jnp.arange(L)[None, :]
    lag_f = jnp.clip(t_idx - s_idx, 0, L - 1)
    lag_b = jnp.clip(s_idx - t_idx, 0, L - 1)
    m_f = jnp.where((t_idx >= s_idx)[None, :, :, None, None], kern[0][:, lag_f], 0.0)
    m_b = jnp.where((s_idx >= t_idx)[None, :, :, None, None], kern[1][:, lag_b], 0.0)
    m = (m_f + m_b).transpose(0, 1, 4, 2, 3).reshape(G_B, S5_W, S5_W)
    down = (L - ks[:L])[None, :, None].astype(jnp.complex64)
    pw_down_b = jnp.exp(ldt[1][:, None, :] * down)
    g_f = jnp.exp(ldt[0][:, None, :] * (down - 1.0))[..., None] * b_bar[0][:, None]
    g_b = pw[1][:, :L, :, None] * b_bar[1][:, None]

    def g_cols(x):
        return x.transpose(0, 1, 3, 2).reshape(G_B, S5_W, P_B)

    gs = jnp.concatenate([g_cols(jnp.real(g_f)), g_cols(jnp.real(g_b)),
                          g_cols(jnp.imag(g_f)), g_cols(jnp.imag(g_b))], axis=-1)
    e_f = cc[0][:, None] * pw[0][:, 1:, None, :]
    e_b = cc[1][:, None] * pw_down_b[:, :, None, :]

    def e_rows(x):
        return x.transpose(0, 3, 1, 2).reshape(G_B, P_B, S5_W)

    e = jnp.concatenate([e_rows(jnp.real(e_f)), e_rows(jnp.real(e_b)),
                         -e_rows(jnp.imag(e_f)), -e_rows(jnp.imag(e_b))], axis=1)
    a_l = pw[:, :, L]
    al = jnp.concatenate([jnp.real(a_l[0]), jnp.real(a_l[1]), jnp.imag(a_l[0]), jnp.imag(a_l[1])], axis=-1)
    no = G_B // S5_OCT
    ow = S5_OCT * S5_W
    rows_sgi = lambda x: x.reshape(no, S5_OCT, L, S5_GROUP, S5_W).transpose(0, 2, 1, 3, 4).reshape(no, ow, S5_W)
    src = jnp.stack([rows_sgi(m), rows_sgi(gs), e.reshape(no, ow, S5_W)]).astype(BF16)
    r = jnp.arange(ow, dtype=I32)
    c = jnp.arange(S5_W, dtype=I32)
    grp_sgi = (r // S5_GROUP) % S5_OCT
    grp_gl = r // S5_W
    src_sgi = (r // (S5_OCT * S5_GROUP)) * S5_GROUP + r % S5_GROUP
    src_gl = r % S5_W
    spread = jnp.stack([src_sgi, src_gl, src_sgi])[:, None, :] == c[None, :, None]
    row_grp = jnp.stack([grp_sgi, grp_sgi, grp_gl])[:, :, None]
    col_grp = jnp.stack([grp_sgi, grp_gl, grp_sgi])[:, None, :]
    tr = 512
    out = pl.pallas_call(
        _s5_expand_body,
        grid=(3, no, ow // tr),
        in_specs=[pl.BlockSpec((None, None, tr, S5_W), lambda k, q, i: (k, q, i, 0)),
                  pl.BlockSpec((None, S5_W, ow), lambda k, q, i: (k, 0, 0)),
                  pl.BlockSpec((None, tr, 1), lambda k, q, i: (k, i, 0)),
                  pl.BlockSpec((None, 1, ow), lambda k, q, i: (k, 0, 0))],
        out_specs=pl.BlockSpec((None, None, tr, ow), lambda k, q, i: (k, q, i, 0)),
        out_shape=jax.ShapeDtypeStruct((3, no, ow, ow), BF16),
        compiler_params=_cparams(("arbitrary", "arbitrary", "arbitrary")),
        name="s5_expand",
    )(src, spread.astype(BF16), row_grp, col_grp)
    return out, al.reshape(1, G_B * S5_W)


def _s5_expand_body(src_ref, spread_ref, rg_ref, cg_ref, o_ref):
    wide = jnp.dot(src_ref[...], spread_ref[...], preferred_element_type=F32)
    o_ref[...] = jnp.where(rg_ref[...] == cg_ref[...], wide, 0.0).astype(BF16)


def _s5_fill_lhs(u_ref, lhs_s):
    tc, _, nb, lanes = u_ref.shape
    for s in range(S5_L):
        lhs_s[:, s * lanes:(s + 1) * lanes] = u_ref[:, s].reshape(tc * nb, lanes).astype(BF16)


def _s5_state_body(u_ref, g_ref, f_ref, lhs_s):
    _s5_fill_lhs(u_ref, lhs_s)
    f_ref[...] = jnp.dot(lhs_s[...], g_ref[...], preferred_element_type=F32).reshape(f_ref.shape)


def _s5_scan_body(f_ref, a_ref, s0_ref, sin_ref, sfin_ref, *, n_c):
    nb, width = s0_ref.shape
    hw = 2 * P_B
    n_g = width // S5_W
    a = a_ref[...]
    a_re = [a[:, k * S5_W:k * S5_W + hw] for k in range(n_g)]
    a_im = [a[:, k * S5_W + hw:(k + 1) * S5_W] for k in range(n_g)]
    is_fwd = (lax.broadcasted_iota(I32, (nb, width), 1) % hw) < P_B

    def split(x):
        return tuple(x[:, k * hw:(k + 1) * hw] for k in range(2 * n_g))

    def merge(parts):
        return jnp.concatenate(parts, axis=-1)

    def advance(c, parts):
        f = split(f_ref[c])
        out = []
        for k in range(n_g):
            s_re, s_im = parts[2 * k], parts[2 * k + 1]
            out.append(a_re[k] * s_re - a_im[k] * s_im + f[2 * k])
            out.append(a_re[k] * s_im + a_im[k] * s_re + f[2 * k + 1])
        return tuple(out)

    def fwd(c, parts):
        sin_ref[c] = merge(parts)
        return advance(c, parts)

    init = split(s0_ref[...])
    fin_f = lax.fori_loop(0, n_c, fwd, init)

    def bwd(k, parts):
        c = n_c - 1 - k
        sin_ref[c] = jnp.where(is_fwd, sin_ref[c], merge(parts))
        return advance(c, parts)

    fin_b = lax.fori_loop(0, n_c, bwd, init)
    sfin_ref[...] = jnp.where(is_fwd, merge(fin_f), merge(fin_b))


def _s5_out_body(u_ref, sin_ref, m_ref, e_ref, y_ref, lhs_s):
    _s5_fill_lhs(u_ref, lhs_s)
    tc, _, nb, lanes = u_ref.shape
    sin = sin_ref[...].reshape(tc * nb, sin_ref.shape[-1]).astype(BF16)
    y = (jnp.dot(lhs_s[...], m_ref[...], preferred_element_type=F32)
         + jnp.dot(sin, e_ref[...], preferred_element_type=F32))
    for s in range(S5_L):
        y_ref[:, s] = y[:, s * lanes:(s + 1) * lanes].reshape(tc, nb, lanes)


def _s5(ub, mats, s0):
    mge, al = mats
    T, B, _ = ub.shape
    n_c = T // S5_L
    no = G_B // S5_OCT
    lanes = S5_OCT * S5_GROUP
    ow = S5_OCT * S5_W
    tc = min(n_c, TOKEN_TILE // B)
    u4 = ub.reshape(n_c, S5_L, B, D_B)
    u_spec = pl.BlockSpec((tc, S5_L, B, lanes), lambda q, i: (i, 0, 0, q))
    w_spec = lambda kind: pl.BlockSpec((None, None, ow, ow), lambda q, i: (kind, q, 0, 0))
    st_spec = pl.BlockSpec((tc, B, ow), lambda q, i: (i, 0, q))
    f_loc = pl.pallas_call(
        _s5_state_body,
        grid=(no, n_c // tc),
        in_specs=[u_spec, w_spec(1)],
        out_specs=st_spec,
        out_shape=jax.ShapeDtypeStruct((n_c, B, G_B * S5_W), F32),
        scratch_shapes=[pltpu.VMEM((tc * B, S5_L * lanes), BF16)],
        compiler_params=_cparams(("arbitrary", "arbitrary")),
        name="s5_state",
    )(u4, mge)
    sw = 4 * S5_W
    sin, sfin = pl.pallas_call(
        functools.partial(_s5_scan_body, n_c=n_c),
        grid=(G_B * S5_W // sw,),
        in_specs=[pl.BlockSpec((n_c, B, sw), lambda g: (0, 0, g)),
                  pl.BlockSpec((1, sw), lambda g: (0, g)),
                  pl.BlockSpec((B, sw), lambda g: (0, g))],
        out_specs=[pl.BlockSpec((n_c, B, sw), lambda g: (0, 0, g)),
                   pl.BlockSpec((B, sw), lambda g: (0, g))],
        out_shape=[jax.ShapeDtypeStruct((n_c, B, G_B * S5_W), F32),
                   jax.ShapeDtypeStruct((B, G_B * S5_W), F32)],
        compiler_params=_cparams(("arbitrary",)),
        name="s5_scan",
    )(f_loc, al, s0)
    y = pl.pallas_call(
        _s5_out_body,
        grid=(no, n_c // tc),
        in_specs=[u_spec, st_spec, w_spec(0), w_spec(2)],
        out_specs=u_spec,
        out_shape=jax.ShapeDtypeStruct((n_c, S5_L, B, D_B), F32),
        scratch_shapes=[pltpu.VMEM((tc * B, S5_L * lanes), BF16)],
        compiler_params=_cparams(("arbitrary", "arbitrary")),
        name="s5_out",
    )(u4, sin, mge, mge)
    return y.reshape(T, B, D_B), sfin


def _s5_state_to_lanes(s_re, s_im):
    parts = [s_re[:, 0], s_re[:, 1], s_im[:, 0], s_im[:, 1]]
    return jnp.concatenate(parts, axis=-1).reshape(s_re.shape[0], G_B * S5_W)


def _s5_lanes_to_state(s):
    s = s.reshape(s.shape[0], G_B, 4, P_B)
    return jnp.stack([s[:, :, 0], s[:, :, 1]], axis=1), jnp.stack([s[:, :, 2], s[:, :, 3]], axis=1)


def _outproj0_body(x_ref, h_ref, ga_ref, y_ref, ub_ref, g1_ref, d_ref, gw_ref, gb_ref, wo_ref,
                   lg_ref, lb_ref, o_ref):
    nb, tq, d = x_ref.shape
    rows = tq * nb

    def flat(v):
        return v.reshape(rows, v.shape[-1])

    ya = flat(h_ref[0] + h_ref[1]) * jax.nn.gelu(flat(ga_ref[...]))
    yb = flat(y_ref[...]) + d_ref[...] * flat(ub_ref[...])
    g = jax.nn.gelu(yb)
    gate = jax.nn.sigmoid(jnp.dot(g.astype(BF16), gw_ref[...], preferred_element_type=F32) + gb_ref[...])
    cat = jnp.concatenate([ya, g * gate], axis=1).astype(BF16)
    cat = jnp.dot(_row_permutation(nb, tq), cat, preferred_element_type=F32).astype(BF16)
    out = jnp.dot(cat, wo_ref[...], preferred_element_type=F32).reshape(nb, tq, d)
    v = ALPHA_DN * x_ref[...] + g1_ref[...] * out
    o_ref[...] = _layer_norm(v, lg_ref[...], lb_ref[...])


def _outproj0(x, h, ga, y5, ub, pmods, s5_d, glu_w, glu_b, w_out0, ln_g, ln_b):
    B, T, D = x.shape
    tq = TOKEN_TILE // B
    tmaj = pl.BlockSpec((tq, B, D_A), lambda i: (i, 0, 0))
    vec = lambda w: pl.BlockSpec((1, w), lambda i: (0, 0))
    return pl.pallas_call(
        _outproj0_body,
        grid=(T // tq,),
        in_specs=[pl.BlockSpec((B, tq, D), lambda i: (0, i, 0)),
                  pl.BlockSpec((2, tq, B, D_A), lambda i: (0, i, 0, 0)),
                  tmaj, tmaj, tmaj,
                  _mod_spec_all(0, 2, B),
                  vec(D_B),
                  pl.BlockSpec((D_B, D_B), lambda i: (0, 0)),
                  vec(D_B),
                  pl.BlockSpec((D, D), lambda i: (0, 0)),
                  vec(D), vec(D)],
        out_specs=pl.BlockSpec((B, tq, D), lambda i: (0, i, 0)),
        out_shape=jax.ShapeDtypeStruct((B, T, D), F32),
        compiler_params=_cparams(("arbitrary",)),
        name="outproj0",
    )(x, h, ga, y5, ub, pmods, s5_d.reshape(1, D_B), glu_w.astype(BF16), glu_b.reshape(1, D_B),
      w_out0.astype(BF16), ln_g.reshape(1, D), ln_b.reshape(1, D))


def _router_body(x_ref, sc_ref, sh_ref, wr_ref, br_ref, u_ref, ldest_ref, gate_ref, cnt_ref):
    u = x_ref[...] * (1.0 + sc_ref[...]) + sh_ref[...]
    ub = u.astype(BF16)
    u_ref[...] = ub
    tm = u.shape[0]
    u_lo = (u - ub.astype(F32)).astype(BF16)
    nt = (((1,), (1,)), ((), ()))
    both = lax.dot_general(wr_ref[...], ub, nt, preferred_element_type=F32)
    logits = (both[:N_EXP] + both[N_EXP:] + lax.dot_general(wr_ref[:N_EXP], u_lo, nt, preferred_element_type=F32)
              + br_ref[...])
    e_iota = lax.broadcasted_iota(I32, logits.shape, 0)
    work = logits
    vals, hots = [], []
    for _ in range(TOP_K):
        m = jnp.max(work, axis=0, keepdims=True)
        idx = jnp.min(jnp.where(work == m, e_iota, N_EXP), axis=0, keepdims=True)
        hot = e_iota == idx
        vals.append(m)
        hots.append(hot)
        work = jnp.where(hot, -jnp.inf, work)
    ex = [jnp.exp(v - vals[0]) for v in vals]
    den = ex[0] + ex[1] + ex[2] + ex[3]
    gate_ref[...] = jnp.concatenate([e / den for e in ex], axis=0)
    hot_sum = jnp.zeros(logits.shape, F32)
    for hot in hots:
        hot_sum = hot_sum + hot.astype(F32)
    hot_b = hot_sum.astype(BF16)
    before = lax.broadcasted_iota(I32, (tm, tm), 0) < lax.broadcasted_iota(I32, (tm, tm), 1)
    excl = jnp.dot(hot_b, jnp.where(before, 1.0, 0.0).astype(BF16), preferred_element_type=F32)
    cnt_row = lax.dot_general(jnp.ones((8, tm), BF16), hot_b, (((1,), (1,)), ((), ())),
                              preferred_element_type=F32)[0:1]
    cnt_ref[...] = cnt_row.astype(I32)
    run_len = jnp.ceil(cnt_row * (1.0 / RUN_ALIGN)) * RUN_ALIGN
    lower = lax.broadcasted_iota(I32, (N_EXP, N_EXP), 1) < lax.broadcasted_iota(I32, (N_EXP, N_EXP), 0)
    run_off = jnp.sum(jnp.where(lower, run_len, 0.0), axis=1, keepdims=True)
    base = excl + run_off
    rows = [jnp.sum(jnp.where(hot, base, 0.0), axis=0, keepdims=True) for hot in hots]
    ldest_ref[...] = jnp.concatenate(rows, axis=0).astype(I32)


def _router(x2, pmods, layer, rows_per_mod, tm, w_router, b_router):
    N, D = x2.shape
    n_t = N // tm
    wr_hi = w_router.T.astype(BF16)
    wr_lo = (w_router.T - wr_hi.astype(F32)).astype(BF16)
    mod = lambda which: pl.BlockSpec((None, None, None, 1, D),
                                     lambda i: (layer, which, (i * tm) // rows_per_mod, 0, 0))
    lane_spec = pl.BlockSpec((TOP_K, tm), lambda i: (0, i))
    return pl.pallas_call(
        _router_body,
        grid=(n_t,),
        in_specs=[pl.BlockSpec((tm, D), lambda i: (i, 0)),
                  mod(4), mod(3),
                  pl.BlockSpec((2 * N_EXP, D), lambda i: (0, 0)),
                  pl.BlockSpec((N_EXP, 1), lambda i: (0, 0))],
        out_specs=[pl.BlockSpec((tm, D), lambda i: (i, 0)),
                   lane_spec, lane_spec,
                   pl.BlockSpec((None, 1, N_EXP), lambda i: (i, 0, 0))],
        out_shape=[jax.ShapeDtypeStruct((N, D), BF16),
                   jax.ShapeDtypeStruct((TOP_K, N), I32),
                   jax.ShapeDtypeStruct((TOP_K, N), F32),
                   jax.ShapeDtypeStruct((n_t, 1, N_EXP), I32)],
        compiler_params=_cparams(("arbitrary",)),
        name="router",
    )(x2, pmods, pmods, jnp.concatenate([wr_hi, wr_lo], axis=0), b_router.reshape(N_EXP, 1))


def _expert_body(blk_e_ref, n_used_ref, xs_ref, wgu_ref, bgu_ref, wdn_ref, bdn_ref, o_ref, wgu_s, wdn_s):
    i = pl.program_id(0)
    prev = blk_e_ref[jnp.maximum(i - 1, 0)]
    changed = jnp.logical_or(i == 0, blk_e_ref[i] != prev)

    @pl.when(changed)
    def _():
        wgu_s[...] = wgu_ref[...].astype(BF16)
        wdn_s[...] = wdn_ref[...].astype(BF16)

    @pl.when(i < n_used_ref[0])
    def _():
        h = jnp.dot(xs_ref[...], wgu_s[...], preferred_element_type=F32) + bgu_ref[...]
        gt = jnp.minimum(h[:, :D_FF], SWIGLU_LIMIT)
        up = jnp.clip(h[:, D_FF:], -SWIGLU_LIMIT, SWIGLU_LIMIT)
        act = (up + 1.0) * gt * jax.nn.sigmoid(SWIGLU_ALPHA * gt)
        y = jnp.dot(act.astype(BF16), wdn_s[...], preferred_element_type=F32) + bdn_ref[...]
        o_ref[...] = y.astype(o_ref.dtype)

    @pl.when(i >= n_used_ref[0])
    def _():
        o_ref[...] = jnp.zeros_like(o_ref)


def _experts(xs, blk_e, n_used, layer, w_gu, b_gu, w_dn, b_dn):
    slots, D = xs.shape
    te = EXPERT_TILE
    n_blk = slots // te
    grid_spec = pltpu.PrefetchScalarGridSpec(
        num_scalar_prefetch=2,
        grid=(n_blk,),
        in_specs=[pl.BlockSpec((te, D), lambda i, be, nu: (jnp.minimum(i, nu[0] - 1), 0)),
                  pl.BlockSpec((None, None, D, 2 * D_FF), lambda i, be, nu: (layer, be[i], 0, 0)),
                  pl.BlockSpec((None, None, 1, 2 * D_FF), lambda i, be, nu: (layer, be[i], 0, 0)),
                  pl.BlockSpec((None, None, D_FF, D), lambda i, be, nu: (layer, be[i], 0, 0)),
                  pl.BlockSpec((None, None, 1, D), lambda i, be, nu: (layer, be[i], 0, 0))],
        out_specs=pl.BlockSpec((te, D), lambda i, be, nu: (i, 0)),
        scratch_shapes=[pltpu.VMEM((D, 2 * D_FF), BF16), pltpu.VMEM((D_FF, D), BF16)],
    )
    return pl.pallas_call(
        _expert_body,
        grid_spec=grid_spec,
        out_shape=jax.ShapeDtypeStruct((slots, D), BF16),
        compiler_params=_cparams(("arbitrary",)),
        name="experts",
    )(blk_e, n_used, xs, w_gu, b_gu.reshape(DEPTH, N_EXP, 1, 2 * D_FF), w_dn, b_dn.reshape(DEPTH, N_EXP, 1, D))


def _chunk_copy(vmem_ref, hbm_ref, sem, k, dst_chunk, to_hbm):
    row = k * RUN_ALIGN if isinstance(k, int) else pl.multiple_of(k * RUN_ALIGN, RUN_ALIGN)
    local = vmem_ref.at[pl.ds(row, RUN_ALIGN)]
    remote = hbm_ref.at[pl.ds(pl.multiple_of(dst_chunk * RUN_ALIGN, RUN_ALIGN), RUN_ALIGN)]
    return pltpu.make_async_copy(local, remote, sem) if to_hbm else pltpu.make_async_copy(remote, local, sem)


def _dispatch_body(dst_ref, tail_ref, nu_ref, *rest, n_chunks, tiles, n_blk):
    n_p = len(tiles)
    xs_ref, buf_s, zero_s, sem = rest[2 * n_p:]
    i = pl.program_id(0)
    n_t = pl.num_programs(0)
    slot = i % 2
    lb = n_chunks * RUN_ALIGN
    te = zero_s.shape[0]
    rb = 256

    def sort_rows(u_ref, ld_ref):
        tm = u_ref.shape[0]
        ld = ld_ref[...]
        u = u_ref[...]
        for j in range(lb // rb):
            r = lax.broadcasted_iota(I32, (rb, tm), 0) + j * rb
            p = jnp.where(r == ld[0:1], 1.0, jnp.where(r == ld[1:2], 1.0, jnp.where(
                r == ld[2:3], 1.0, jnp.where(r == ld[3:4], 1.0, 0.0))))
            buf_s[slot, j * rb:(j + 1) * rb, :] = jnp.dot(p.astype(BF16), u,
                                                          preferred_element_type=F32).astype(BF16)
            for k in range(j * rb // RUN_ALIGN, (j + 1) * rb // RUN_ALIGN):
                _chunk_copy(buf_s.at[slot], xs_ref, sem.at[slot], k, dst_ref[i, k], True).start()

    for p, (first, count) in enumerate(tiles):
        @pl.when(jnp.logical_and(i >= first, i < first + count))
        def _(p=p):
            sort_rows(rest[2 * p], rest[2 * p + 1])

    def drain(s):
        pltpu.make_async_copy(buf_s.at[s], xs_ref.at[pl.ds(0, lb)], sem.at[s]).wait()

    @pl.when(i > 0)
    def _():
        drain(1 - slot)

    @pl.when(i == n_t - 1)
    def _():
        drain(slot)
        zero_s[...] = jnp.zeros_like(zero_s)
        n_tail = tail_ref.shape[0]

        def tail_copy(t):
            return _chunk_copy(zero_s, xs_ref, sem.at[2], 0, tail_ref[t], True)

        def block_copy(k):
            return pltpu.make_async_copy(zero_s, xs_ref.at[pl.ds(pl.multiple_of(k * te, te), te)], sem.at[2])

        def each(n, pred, copy, wait):
            def body(t, _):
                @pl.when(pred(t))
                def _():
                    copy(t).wait() if wait else copy(t).start()
                return 0
            lax.fori_loop(0, n, body, 0)

        for wait in (False, True):
            each(n_tail, lambda t: tail_ref[t] >= 0, tail_copy, wait)
            each(n_blk, lambda k: k >= nu_ref[0], block_copy, wait)


def _dispatch(u2s, ldests, dst, tails, n_used, slots, tm, te):
    D = u2s[0].shape[1]
    n_t, n_chunks = dst.shape
    lb = n_chunks * RUN_ALIGN
    tiles, first = [], 0
    for u2 in u2s:
        tiles.append((first, u2.shape[0] // tm))
        first += u2.shape[0] // tm
    assert first == n_t

    def own_tile(first, count):
        return lambda i, d, t, nu: jnp.clip(i - first, 0, count - 1)

    in_specs, operands = [], []
    for (first, count), u2, ld in zip(tiles, u2s, ldests):
        tile = own_tile(first, count)
        in_specs += [pl.BlockSpec((tm, D), lambda i, d, t, nu, tile=tile: (tile(i, d, t, nu), 0)),
                     pl.BlockSpec((TOP_K, tm), lambda i, d, t, nu, tile=tile: (0, tile(i, d, t, nu)))]
        operands += [u2, ld]
    grid_spec = pltpu.PrefetchScalarGridSpec(
        num_scalar_prefetch=3,
        grid=(n_t,),
        in_specs=in_specs,
        out_specs=pl.BlockSpec(memory_space=pl.ANY),
        scratch_shapes=[pltpu.VMEM((2, lb, D), BF16), pltpu.VMEM((te, D), BF16), pltpu.SemaphoreType.DMA((3,))],
    )
    return pl.pallas_call(
        functools.partial(_dispatch_body, n_chunks=n_chunks, tiles=tuple(tiles), n_blk=slots // te),
        grid_spec=grid_spec,
        out_shape=jax.ShapeDtypeStruct((slots, D), BF16),
        compiler_params=_cparams(("arbitrary",)),
        name="moe_dispatch",
    )(dst, tails, n_used, *operands)


def _combine_body(dst_ref, x_ref, ld_ref, gate_ref, g2_ref, lg_ref, lb_ref, ys_ref, o_ref, buf_s, sem, *, n_chunks):
    i = pl.program_id(0)
    n_t = pl.num_programs(0)
    slot = i % 2
    tm = x_ref.shape[0]
    lb = n_chunks * RUN_ALIGN

    def wait_all(s):
        pltpu.make_async_copy(ys_ref.at[pl.ds(0, lb)], buf_s.at[s], sem.at[s]).wait()

    @pl.when(i == 0)
    def _():
        def body(k, _):
            _chunk_copy(buf_s.at[0], ys_ref, sem.at[0], k, dst_ref[0, k], False).start()
            return 0
        lax.fori_loop(0, n_chunks, body, 0, unroll=8)

    wait_all(slot)
    nxt = jnp.minimum(i + 1, n_t - 1)
    ld = ld_ref[...]
    gates = gate_ref[...]
    cb = 512
    y = jnp.zeros((tm, x_ref.shape[1]), F32)
    for j in range(lb // cb):
        for k in range(j * cb // RUN_ALIGN, (j + 1) * cb // RUN_ALIGN):
            _chunk_copy(buf_s.at[1 - slot], ys_ref, sem.at[1 - slot], k, dst_ref[nxt, k], False).start()
        c = lax.broadcasted_iota(I32, (tm, cb), 1) + j * cb
        w = jnp.zeros((tm, cb), F32)
        for k in range(TOP_K):
            w = jnp.where(c == ld[:, k:k + 1], gates[:, k:k + 1], w)
        y = y + jnp.dot(w.astype(BF16), buf_s[slot, j * cb:(j + 1) * cb, :], preferred_element_type=F32)
    v = ALPHA_DN * x_ref[...] + g2_ref[...] * y
    o_ref[...] = _layer_norm(v, lg_ref[...], lb_ref[...])

    @pl.when(i == n_t - 1)
    def _():
        wait_all(1 - slot)


def _combine(x2, ldest_col, gates_col, dst, ys, pmods, layer, rows_per_mod, tm, ln_g, ln_b):
    N, D = x2.shape
    n_t, n_chunks = dst.shape
    lb = n_chunks * RUN_ALIGN
    vec = pl.BlockSpec((1, D), lambda i, d: (0, 0))
    grid_spec = pltpu.PrefetchScalarGridSpec(
        num_scalar_prefetch=1,
        grid=(n_t,),
        in_specs=[pl.BlockSpec((tm, D), lambda i, d: (i, 0)),
                  pl.BlockSpec((tm, TOP_K), lambda i, d: (i, 0)),
                  pl.BlockSpec((tm, TOP_K), lambda i, d: (i, 0)),
                  pl.BlockSpec((None, None, None, 1, D), lambda i, d: (layer, 5, (i * tm) // rows_per_mod, 0, 0)),
                  vec, vec,
                  pl.BlockSpec(memory_space=pl.ANY)],
        out_specs=pl.BlockSpec((tm, D), lambda i, d: (i, 0)),
        scratch_shapes=[pltpu.VMEM((2, lb, D), BF16), pltpu.SemaphoreType.DMA((2,))],
    )
    return pl.pallas_call(
        functools.partial(_combine_body, n_chunks=n_chunks),
        grid_spec=grid_spec,
        out_shape=jax.ShapeDtypeStruct((N, D), F32),
        compiler_params=_cparams(("arbitrary",)),
        name="moe_combine",
    )(dst, x2, ldest_col, gates_col, pmods, ln_g.reshape(1, D), ln_b.reshape(1, D), ys)


def _moe_plan(cnts, n_chunks, te, spare_chunk):
    a = RUN_ALIGN
    cnt = jnp.concatenate(cnts, axis=0)
    pc = (cnt + a - 1) // a * a
    seg = (jnp.sum(pc, axis=0) + te - 1) // te * te
    pad_end = jnp.cumsum(seg)
    run_start = (pad_end - seg)[None, :] + jnp.cumsum(pc, axis=0) - pc
    lo = jnp.cumsum(pc, axis=1) - pc
    tables, t0 = [], 0
    for c, nc in zip(cnts, n_chunks):
        sl = slice(t0, t0 + c.shape[0])
        parity = ((t0 + jnp.arange(c.shape[0], dtype=I32)) % 2)[:, None]
        t0 += c.shape[0]
        pos = jnp.arange(nc, dtype=I32) * a
        owner = jnp.sum(((lo[sl] + pc[sl])[:, None, :] <= pos[None, :, None]).astype(I32), axis=-1)
        mine = owner[..., None] == jnp.arange(N_EXP, dtype=I32)
        base = jnp.sum(jnp.where(mine, (run_start[sl] - lo[sl])[:, None, :], 0), axis=-1)
        valid = pos[None, :] < jnp.sum(pc[sl], axis=1, keepdims=True)
        spare = spare_chunk + parity * nc + jnp.arange(nc, dtype=I32)[None, :]
        tables.append(jnp.where(valid, (base + pos[None, :]) // a, spare).astype(I32))
    tail_row = (pad_end - seg + jnp.sum(pc, axis=0))[:, None] + jnp.arange(te // a, dtype=I32)[None, :] * a
    tails = jnp.where(tail_row < pad_end[:, None], tail_row // a, -1).astype(I32).reshape(-1)
    return tables, tails, pad_end


def _moe(passes, layer, P):
    te = EXPERT_TILE
    routed = []
    for x, pmods, shared_mod in passes:
        B, T, D = x.shape
        N = B * T
        tm = min(MOE_TILE, N if shared_mod else T)
        x2 = x.reshape(N, D)
        u2, ldest, gates, cnt = _router(x2, pmods, layer, T, tm, P['w_router'][layer], P['b_router'][layer])
        routed.append((x2, pmods, T, tm, u2, ldest, gates, cnt[:, 0, :], x.shape))
    n_chunks = [r[3] * TOP_K // RUN_ALIGN + N_EXP for r in routed]
    n_tiles = [r[7].shape[0] for r in routed]
    n_assign = sum(r[0].shape[0] for r in routed) * TOP_K
    seg_rows = -(-(n_assign + sum(n_tiles) * N_EXP * RUN_ALIGN + N_EXP * te) // te) * te
    slots = seg_rows + -(-(2 * max(n_chunks) * RUN_ALIGN) // te) * te
    dsts, tails, pad_end = _moe_plan([r[7] for r in routed], n_chunks, te, seg_rows // RUN_ALIGN)
    n_blk = slots // te
    blk_pos = jnp.arange(n_blk, dtype=I32) * te
    blk_e = jnp.minimum(jnp.sum((pad_end[None, :] <= blk_pos[:, None]).astype(I32), axis=1), N_EXP - 1)
    n_used = (pad_end[-1] // te).astype(I32).reshape(1)
    assert len(set(r[3] for r in routed)) == 1, "all passes must use the same MoE tile"
    xs = _dispatch([r[4] for r in routed], [r[5] for r in routed], jnp.concatenate(dsts, axis=0), tails, n_used,
                   slots, routed[0][3], te)
    ys = _experts(xs, blk_e, n_used, layer, P['w_gu'], P['b_gu'], P['w_down'], P['b_down'])
    outs = []
    for r, dst in zip(routed, dsts):
        x2, pmods, T, tm, _, ldest, gates, _, shape = r
        y = _combine(x2, ldest.T, gates.T, dst, ys, pmods, layer, T, tm, P['ln_g'][layer, 1], P['ln_b'][layer, 1])
        outs.append(y.reshape(shape))
    return outs


def _split3(x):
    hi = x.astype(BF16)
    r1 = x - hi.astype(F32)
    mid = r1.astype(BF16)
    lo = (r1 - mid.astype(F32)).astype(BF16)
    return jnp.concatenate([hi, mid, lo], axis=1)


def _inproj1_body(x_ref, xp_ref, xn_ref, sc_ref, sh_ref, w_ref, wvt_ref, cw_ref, cb_ref, wg2_ref, wgh_ref,
                  bg_ref, qk_ref, vt_ref, o_ref, gcol_ref, grow_ref):
    j = pl.program_id(1)
    n_j = pl.num_programs(1)
    halo = xp_ref.shape[0]
    tm = x_ref.shape[0]
    nh = 4 * H_C

    def adaln(v):
        return v * (1.0 + sc_ref[...]) + sh_ref[...]

    u = adaln(x_ref[...])
    ub = u.astype(BF16)
    u_prev = jnp.where(j > 0, adaln(xp_ref[...]), 0.0).astype(BF16)
    u_next = jnp.where(j < n_j - 1, adaln(xn_ref[...]), 0.0).astype(BF16)
    u_ext = jnp.concatenate([u_prev, ub, u_next], axis=0)
    n_ext = tm + 2 * halo
    cblk = 256
    n_blk = 2 * D_MODEL // cblk

    def matmuls(b):
        c0 = b * cblk
        zqk = jnp.dot(u_ext, w_ref[:, c0:c0 + cblk], preferred_element_type=F32)
        if c0 < D_MODEL:
            vt = lax.dot_general(wvt_ref[c0:c0 + cblk, :], ub, (((1,), (1,)), ((), ())), preferred_element_type=F32)
            vt_ref[c0:c0 + cblk, :] = vt.astype(BF16)
        else:
            o0 = c0 - D_MODEL
            o_ref[:, o0:o0 + cblk] = jnp.dot(ub, w_ref[:, 3 * D_MODEL + o0:3 * D_MODEL + o0 + cblk],
                                             preferred_element_type=F32)
        return zqk

    z_next = matmuls(0)
    for b in range(n_blk):
        zqk = z_next
        if b + 1 < n_blk:
            z_next = matmuls(b + 1)
        cols = slice(b * cblk, (b + 1) * cblk)
        acc = (pltpu.roll(zqk, 2, 0) * cw_ref[0:1, cols] + pltpu.roll(zqk, 1, 0) * cw_ref[1:2, cols]
               + zqk * cw_ref[2:3, cols] + pltpu.roll(zqk, n_ext - 1, 0) * cw_ref[3:4, cols])[halo:tm + halo]
        act = jax.nn.silu(acc + cb_ref[:, cols])
        if b * cblk >= D_MODEL:
            act = act * (DH_C ** -0.5)
        qk_ref[:, cols] = act.astype(BF16)
    u_lo = (u - ub.astype(F32)).astype(BF16)
    g2 = jnp.dot(ub, wg2_ref[...], preferred_element_type=F32)
    gc = g2[:, :nh] + g2[:, nh:] + jnp.dot(u_lo, wgh_ref[...], preferred_element_type=F32) + bg_ref[...]
    r_i = lax.broadcasted_iota(I32, (tm, tm), 0)
    c_i = lax.broadcasted_iota(I32, (tm, tm), 1)
    same = (r_i // CHUNK) == (c_i // CHUNK)
    tri_f = jnp.where(jnp.logical_and(same, c_i <= r_i), 1.0, 0.0).astype(BF16)
    tri_b = jnp.where(jnp.logical_and(same, c_i >= r_i), 1.0, 0.0).astype(BF16)
    lf3 = _split3(jax.nn.log_sigmoid(gc))

    def sum3(p, axis):
        if axis == 1:
            return p[:, :nh] + p[:, nh:2 * nh] + p[:, 2 * nh:]
        return p[:nh] + p[nh:2 * nh] + p[2 * nh:]

    col_i = lax.broadcasted_iota(I32, (tm, nh), 1)
    cum = jnp.where(col_i < 2 * H_C,
                    sum3(jnp.dot(tri_f, lf3, preferred_element_type=F32), 1),
                    sum3(jnp.dot(tri_b, lf3, preferred_element_type=F32), 1))
    gcol = jnp.where((col_i // H_C) % 2 == 1, cum, gc)
    gcol_ref[...] = gcol
    eye = jnp.where(r_i == c_i, 1.0, 0.0).astype(BF16)
    gt3 = lax.dot_general(_split3(gcol), eye, (((0,), (0,)), ((), ())), preferred_element_type=F32)
    grow_ref[...] = sum3(gt3, 0)


def _inproj1(x, mods, w_in1, w_gate1, b_gate1, conv_w, conv_b, row_off, row_stride):
    B, T, D = x.shape
    tm = min(2 * TOKEN_TILE, T)
    n_j = T // tm
    nh = 4 * H_C
    halo = 8
    wg_hi = w_gate1.astype(BF16)
    wg_lo = (w_gate1 - wg_hi.astype(F32)).astype(BF16)
    w_bf = w_in1.astype(BF16)
    tok = lambda w: pl.BlockSpec((None, tm, w), lambda b, j: (b, j, 0))
    full = lambda r, c: pl.BlockSpec((r, c), lambda b, j: (0, 0))
    return pl.pallas_call(
        _inproj1_body,
        grid=(B, n_j),
        in_specs=[tok(D),
                  pl.BlockSpec((None, halo, D), lambda b, j: (b, jnp.maximum(j * (tm // halo) - 1, 0), 0)),
                  pl.BlockSpec((None, halo, D), lambda b, j: (b, jnp.minimum((j + 1) * (tm // halo), T // halo - 1), 0)),
                  _mod_spec(1, 1, row_off, row_stride),
                  _mod_spec(1, 0, row_off, row_stride),
                  full(D, 4 * D), full(D, D), full(CONV_W, 2 * D), full(1, 2 * D),
                  full(D, 2 * nh), full(D, nh), full(1, nh)],
        out_specs=[tok(2 * D),
                   pl.BlockSpec((None, D, tm), lambda b, j: (b, 0, j)),
                   tok(D), tok(nh),
                   pl.BlockSpec((nh, tm), lambda b, j: (0, b * n_j + j))],
        out_shape=[jax.ShapeDtypeStruct((B, T, 2 * D), BF16),
                   jax.ShapeDtypeStruct((B, D, T), BF16),
                   jax.ShapeDtypeStruct((B, T, D), F32),
                   jax.ShapeDtypeStruct((B, T, nh), F32),
                   jax.ShapeDtypeStruct((nh, B * T), F32)],
        compiler_params=_cparams(("arbitrary", "arbitrary")),
        name="inproj1",
    )(x, x, x, mods, mods, w_bf, w_bf[:, 2 * D:3 * D].T, conv_w, conv_b.reshape(1, 2 * D),
      jnp.concatenate([wg_hi, wg_lo], axis=1), wg_hi, b_gate1.reshape(1, nh))


def _mlstm_body(q_ref, k_ref, vt_ref, gcol_ref, grow_ref, c0_ref, n0_ref, m0_ref,
                h_ref, cfin_ref, nfin_ref, mfin_ref, hb_s, *, T):
    n_c = T // CHUNK
    L = CHUNK

    rb = min(T, 512)
    t_i = lax.broadcasted_iota(I32, (L, L), 0)
    s_i = lax.broadcasted_iota(I32, (L, L), 1)

    cfin_ref[...] = c0_ref[...]
    nfin_ref[...] = n0_ref[...]
    mfin_ref[...] = m0_ref[...]

    def chunk(d, c):
        mask = (s_i <= t_i) if d == 0 else (s_i >= t_i)
        last = L - 1 if d == 0 else 0
        off = pl.multiple_of(c * L, L)
        qc = q_ref[pl.ds(off, L), :]
        kc = k_ref[pl.ds(off, L), :]
        vt = vt_ref[:, pl.ds(off, L)]
        gcol = gcol_ref[pl.ds(off, L), :]
        grow = grow_ref[:, pl.ds(off, L)]
        b_c = gcol[:, 2 * d + 1:2 * d + 2]
        li_r = grow[2 * d:2 * d + 1, :]
        b_r = grow[2 * d + 1:2 * d + 2, :]
        m_prev = mfin_ref[d]
        log_d = jnp.where(mask, b_c - b_r + li_r, -jnp.inf)
        m_inter = b_c + m_prev
        m_t = jnp.maximum(m_inter, jnp.max(log_d, axis=-1, keepdims=True))
        dmat = jnp.exp(log_d - m_t)
        w_inter = jnp.exp(m_inter - m_t)
        c_old = cfin_ref[d]
        s = lax.dot_general(qc, kc, (((1,), (1,)), ((), ())), preferred_element_type=F32) * dmat
        inter = lax.dot_general(qc, c_old.astype(BF16), (((1,), (1,)), ((), ())), preferred_element_type=F32)
        num = w_inter * inter + lax.dot_general(s.astype(BF16), vt, (((1,), (1,)), ((), ())),
                                                preferred_element_type=F32)
        qn = jnp.sum(qc.astype(F32) * nfin_ref[d], axis=-1, keepdims=True)
        den = w_inter * qn + jnp.sum(s, axis=-1, keepdims=True)
        h = num / jnp.maximum(jnp.abs(den), jnp.exp(-m_t))
        m_new = m_t[last:last + 1, :]
        b_last = b_c[last:last + 1, :]
        w_s = jnp.exp(b_last - b_r + li_r - m_new)
        decay = jnp.exp(b_last + m_prev - m_new)
        wvt = (vt.astype(F32) * w_s).astype(BF16)
        cfin_ref[d] = decay * c_old + jnp.dot(wvt, kc, preferred_element_type=F32)
        wk = jnp.dot(jnp.broadcast_to(w_s, (8, L)).astype(BF16), kc, preferred_element_type=F32)[0:1]
        nfin_ref[d] = decay * nfin_ref[d] + wk
        mfin_ref[d] = m_new
        return off, h

    def both(ci, _):
        off_f, h_f = chunk(0, ci)
        off_b, h_b = chunk(1, n_c - 1 - ci)
        h_ref[pl.ds(off_f, L), :] = h_f
        hb_s[pl.ds(off_b, L), :] = h_b
        return 0

    lax.fori_loop(0, n_c, both, 0)

    def add_bwd(r, _):
        off = pl.multiple_of(r * rb, rb)
        h_ref[pl.ds(off, rb), :] = h_ref[pl.ds(off, rb), :] + hb_s[pl.ds(off, rb), :]
        return 0

    lax.fori_loop(0, T // rb, add_bwd, 0)


def _mlstm(qk, vt, gcol, grow, c0, n0, m0):
    B, _, T = vt.shape
    DH = DH_C
    st = lambda *tail: pl.BlockSpec((None, 2, None) + tail, lambda b, h: (b, 0, h) + (0,) * len(tail))
    return pl.pallas_call(
        functools.partial(_mlstm_body, T=T),
        grid=(B, H_C),
        in_specs=[pl.BlockSpec((None, T, DH), lambda b, h: (b, 0, h)),
                  pl.BlockSpec((None, T, DH), lambda b, h: (b, 0, H_C + h)),
                  pl.BlockSpec((None, DH, T), lambda b, h: (b, h, 0)),
                  pl.BlockSpec((None, None, T, 4), lambda b, h: (b, h, 0, 0)),
                  pl.BlockSpec((None, None, 4, T), lambda b, h: (b, h, 0, 0)),
                  st(DH, DH), st(1, DH), st(1, 1)],
        out_specs=[pl.BlockSpec((None, T, DH), lambda b, h: (b, 0, h)),
                   st(DH, DH), st(1, DH), st(1, 1)],
        out_shape=[jax.ShapeDtypeStruct((B, T, D_MODEL), F32),
                   jax.ShapeDtypeStruct((B, 2, H_C, DH, DH), F32),
                   jax.ShapeDtypeStruct((B, 2, H_C, 1, DH), F32),
                   jax.ShapeDtypeStruct((B, 2, H_C, 1, 1), F32)],
        scratch_shapes=[pltpu.VMEM((T, DH), F32)],
        compiler_params=_cparams(("arbitrary", "arbitrary")),
        name="mlstm",
    )(qk, qk, vt, gcol, grow, c0, n0, m0)


def _outproj1_body(x_ref, h_ref, o_ref, g1_ref, w_ref, lg_ref, lb_ref, out_ref):
    y = jax.nn.sigmoid(o_ref[...]) * h_ref[...]
    out = jnp.dot(y.astype(BF16), w_ref[...], preferred_element_type=F32)
    v = ALPHA_DN * x_ref[...] + g1_ref[...] * out
    out_ref[...] = _layer_norm(v, lg_ref[...], lb_ref[...])


def _outproj1(x, h, o, mods, w_out1, ln_g, ln_b, row_off, row_stride):
    B, T, D = x.shape
    tm = min(TOKEN_TILE, T)
    tok = pl.BlockSpec((None, tm, D), lambda b, j: (b, j, 0))
    vec = pl.BlockSpec((1, D), lambda b, j: (0, 0))
    return pl.pallas_call(
        _outproj1_body,
        grid=(B, T // tm),
        in_specs=[tok, tok, tok, _mod_spec(1, 2, row_off, row_stride),
                  pl.BlockSpec((D, D), lambda b, j: (0, 0)), vec, vec],
        out_specs=tok,
        out_shape=jax.ShapeDtypeStruct((B, T, D), F32),
        compiler_params=_cparams(("arbitrary", "arbitrary")),
        name="outproj1",
    )(x, h, o, mods, w_out1.astype(BF16), ln_g.reshape(1, D), ln_b.reshape(1, D))


def _to_col_major(x):
    B, T, C = x.shape
    rows = T // GRID_W
    return x.reshape(B, rows, GRID_W, C).transpose(0, 2, 1, 3).reshape(B, T, C)


def _to_row_major(x):
    B, T, C = x.shape
    rows = T // GRID_W
    return x.reshape(B, GRID_W, rows, C).transpose(0, 2, 1, 3).reshape(B, T, C)


def _mixer0(x, mods, st, P, s5_mats):
    h_rg, s_re, s_im = st
    xa, ga, ub = _inproj0(x, mods, P['w_in0'][0].astype(BF16))
    h, h_fin = _rglru(xa, P['conv_a_w'][0], P['conv_a_b'][0], P['rg_wa'][0], P['rg_ba'][0],
                      P['rg_wi'][0], P['rg_bi'][0], P['rg_lam'][0], h_rg[:, 0].transpose(1, 0, 2))
    y5, s_fin = _s5(ub, s5_mats, _s5_state_to_lanes(s_re[:, 0], s_im[:, 0]))
    x = _outproj0(x, h, ga, y5, ub, mods, P['s5_d'][0], P['glu_w'][0], P['glu_b'][0], P['w_out0'][0],
                  P['ln_g'][0, 0], P['ln_b'][0, 0])
    new_re, new_im = _s5_lanes_to_state(s_fin)
    return x, (h_fin.transpose(1, 0, 2)[:, None], new_re[:, None], new_im[:, None])


def _mixer1(x, mods, st, P):
    m_c, m_n, m_m = st
    B, T, D = x.shape
    qk, v, o, gcol, grow = _inproj1(x, mods, P['w_in1'][0], P['w_gate1'][0], P['b_gate1'][0],
                                    P['conv_c_w'][0], P['conv_c_b'][0], 0, 1)
    gcol = gcol.reshape(B, T, 2, 2, H_C).transpose(0, 4, 1, 2, 3).reshape(B, H_C, T, 4)
    grow = grow.reshape(2, 2, H_C, B, T).transpose(3, 2, 0, 1, 4).reshape(B, H_C, 4, T)
    h, c_fin, n_fin, m_fin = _mlstm(qk, v, gcol, grow,
                                    m_c[:, 0], m_n[:, 0][:, :, :, None, :], m_m[:, 0][:, :, :, None, None])
    x = _outproj1(x, h, o, mods, P['w_out1'][0], P['ln_g'][1, 0], P['ln_b'][1, 0], 0, 1)
    return x, (c_fin[:, None], n_fin[:, None, :, :, 0, :], m_fin[:, None, :, :, 0, 0])


def _forward(x_prompt, x_sample, c, c_ctx, states, P):
    bp = x_prompt.shape[0]
    bs = x_sample.shape[0]
    rows = 1 + bs
    rpad = -(-rows // 8) * 8
    cv = jnp.concatenate([c_ctx[None, :], c, jnp.zeros((rpad - rows, D_MODEL), F32)], axis=0)
    mods = _modulation(cv, P['w_mod'], P['b_mod'])
    s5_mats = _s5_matrices(P['s5_a_re'][0], P['s5_a_im'][0], P['s5_log_dt'][0], P['s5_b_re'][0], P['s5_b_im'][0],
                           P['s5_c_re'][0], P['s5_c_im'][0])
    zero_state = (jnp.zeros((bp, 1, 2, D_A), F32),
                  jnp.zeros((bp, 1, 2, G_B, P_B), F32),
                  jnp.zeros((bp, 1, 2, G_B, P_B), F32),
                  jnp.zeros((bp, 1, 2, H_C, DH_C, DH_C), F32),
                  jnp.zeros((bp, 1, 2, H_C, DH_C), F32),
                  jnp.zeros((bp, 1, 2, H_C), F32))
    mods_ctx = jnp.broadcast_to(mods[:, :, 0:1], mods.shape[:2] + (bp,) + mods.shape[3:])
    mods_lat = mods[:, :, 1:1 + bs]
    xc, new_even = _mixer0(x_prompt, mods_ctx, zero_state[:3], P, s5_mats)
    xl, _ = _mixer0(x_sample, mods_lat, states[:3], P, s5_mats)
    xc, xl = _moe([(xc, mods_ctx, True), (xl, mods_lat, False)], 0, P)
    xl = _to_col_major(xl)
    xc, new_odd = _mixer1(xc, mods_ctx, zero_state[3:], P)
    xl, _ = _mixer1(xl, mods_lat, states[3:], P)
    xc, xl = _moe([(xc, mods_ctx, True), (xl, mods_lat, False)], 1, P)
    return (xc, _to_row_major(xl)) + tuple(new_even) + tuple(new_odd)


def kernel(x_prompt, x_sample, c, c_ctx, state_rglru, state_s5_re, state_s5_im, state_mlstm_C, state_mlstm_n, state_mlstm_m, w_mod, b_mod, ln_g, ln_b, w_in0, conv_a_w, conv_a_b, rg_wa, rg_ba, rg_wi, rg_bi, rg_lam, s5_a_re, s5_a_im, s5_log_dt, s5_b_re, s5_b_im, s5_c_re, s5_c_im, s5_d, glu_w, glu_b, w_out0, w_in1, w_gate1, b_gate1, conv_c_w, conv_c_b, w_out1, w_router, b_router, w_gu, b_gu, w_down, b_down):
    P = dict(w_mod=w_mod, b_mod=b_mod, ln_g=ln_g, ln_b=ln_b, w_in0=w_in0, conv_a_w=conv_a_w,
             conv_a_b=conv_a_b, rg_wa=rg_wa, rg_ba=rg_ba, rg_wi=rg_wi, rg_bi=rg_bi, rg_lam=rg_lam,
             s5_a_re=s5_a_re, s5_a_im=s5_a_im, s5_log_dt=s5_log_dt, s5_b_re=s5_b_re, s5_b_im=s5_b_im,
             s5_c_re=s5_c_re, s5_c_im=s5_c_im, s5_d=s5_d, glu_w=glu_w, glu_b=glu_b, w_out0=w_out0,
             w_in1=w_in1, w_gate1=w_gate1, b_gate1=b_gate1, conv_c_w=conv_c_w, conv_c_b=conv_c_b,
             w_out1=w_out1, w_router=w_router, b_router=b_router, w_gu=w_gu, b_gu=b_gu,
             w_down=w_down, b_down=b_down)
    states = (state_rglru, state_s5_re, state_s5_im, state_mlstm_C, state_mlstm_n, state_mlstm_m)
    return _forward(x_prompt, x_sample, c, c_ctx, states, P)
```

```python
import functools

import jax
import jax.numpy as jnp
from jax import lax
from jax.experimental import pallas as pl
from jax.experimental.pallas import tpu as pltpu

F32 = jnp.float32
BF16 = jnp.bfloat16
I32 = jnp.int32
HI = lax.Precision.HIGHEST

D_MODEL = 1024
DEPTH = 2
GRID_W = 64
D_A = 512
NB_A = 8
BS_A = D_A // NB_A
CONV_W = 4
LRU_C = 8.0
D_B = 512
S5_GROUP = 16
G_B = D_B // S5_GROUP
P_B = 64
H_C = 4
DH_C = D_MODEL // H_C
CHUNK = 128
N_EXP = 32
TOP_K = 4
D_FF = D_MODEL
SWIGLU_LIMIT = 7.0
SWIGLU_ALPHA = 1.702
ALPHA_DN = (2 * DEPTH) ** 0.25
LN_EPS = 1e-5

S5_L = 16
S5_W = S5_L * S5_GROUP
S5_OCT = 8
TOKEN_TILE = 256
EXPERT_TILE = 512
MOE_TILE = 512
RUN_ALIGN = 16
VMEM_LIMIT = 56 * 1024 * 1024


def _cparams(sem, vmem=VMEM_LIMIT):
    return pltpu.CompilerParams(dimension_semantics=sem, vmem_limit_bytes=vmem)


def _layer_norm(v, g, b):
    mu = jnp.mean(v, axis=-1, keepdims=True)
    c = v - mu
    var = jnp.mean(c * c, axis=-1, keepdims=True)
    return c * lax.rsqrt(var + LN_EPS) * g + b


def _mod_spec(layer, which, row_off, row_stride):
    return pl.BlockSpec((None, None, None, 1, D_MODEL),
                        lambda b, j: (layer, which, row_off + b * row_stride, 0, 0))


def _mod_spec_all(layer, which, nb):
    return pl.BlockSpec((None, None, nb, 1, D_MODEL), lambda i: (layer, which, 0, 0, 0))


def _row_permutation(n_out_major, n_out_minor):
    n = n_out_major * n_out_minor
    r_out = lax.broadcasted_iota(I32, (n, n), 0)
    r_in = lax.broadcasted_iota(I32, (n, n), 1)
    hit = jnp.logical_and(r_out // n_out_minor == r_in % n_out_major, r_out % n_out_minor == r_in // n_out_major)
    return jnp.where(hit, 1.0, 0.0).astype(BF16)


def _mod_body(c_ref, w_ref, b_ref, o_ref):
    s = jax.nn.silu(c_ref[...])
    o_ref[...] = jnp.dot(s, w_ref[...], precision=HI, preferred_element_type=F32) + b_ref[...]


def _modulation(cv, w_mod, b_mod):
    R, D = cv.shape
    L, _, N6 = w_mod.shape
    tn = N6 // 6
    out = pl.pallas_call(
        _mod_body,
        grid=(L, N6 // tn),
        in_specs=[pl.BlockSpec((R, D), lambda l, j: (0, 0)),
                  pl.BlockSpec((None, D, tn), lambda l, j: (l, 0, j)),
                  pl.BlockSpec((None, 1, tn), lambda l, j: (l, 0, j))],
        out_specs=pl.BlockSpec((None, R, tn), lambda l, j: (l, 0, j)),
        out_shape=jax.ShapeDtypeStruct((L, R, N6), F32),
        compiler_params=_cparams(("arbitrary", "arbitrary")),
        name="modulation",
    )(cv, w_mod, b_mod.reshape(L, 1, N6))
    return out.reshape(L, R, 6, D).transpose(0, 2, 1, 3)[:, :, :, None, :]


def _inproj0_body(x_ref, sc_ref, sh_ref, w_ref, xa_ref, ga_ref, ub_ref):
    nb, tq, d = x_ref.shape
    u = x_ref[...] * (1.0 + sc_ref[...]) + sh_ref[...]
    ub = u.reshape(nb * tq, d).astype(BF16)
    ut = jnp.dot(_row_permutation(tq, nb), ub, preferred_element_type=F32).astype(BF16)
    z = jnp.dot(ut, w_ref[...], preferred_element_type=F32).reshape(tq, nb, w_ref.shape[1])
    xa_ref[...] = z[:, :, :D_A]
    ga_ref[...] = z[:, :, D_A:2 * D_A]
    ub_ref[...] = z[:, :, 2 * D_A:]


def _inproj0(x, pmods, w_in0):
    B, T, D = x.shape
    tq = TOKEN_TILE // B
    nz = w_in0.shape[1]
    tmaj = pl.BlockSpec((tq, B, D_A), lambda i: (i, 0, 0))
    return pl.pallas_call(
        _inproj0_body,
        grid=(T // tq,),
        in_specs=[pl.BlockSpec((B, tq, D), lambda i: (0, i, 0)),
                  _mod_spec_all(0, 1, B), _mod_spec_all(0, 0, B),
                  pl.BlockSpec((D, nz), lambda i: (0, 0))],
        out_specs=[tmaj, tmaj, tmaj],
        out_shape=[jax.ShapeDtypeStruct((T, B, D_A), F32)] * 3,
        compiler_params=_cparams(("arbitrary",)),
        name="inproj0",
    )(x, pmods, pmods, w_in0)


def _rglru_body(cur_ref, prev_ref, next_ref, cw_ref, cb_ref, wa_ref, wi_ref, ba_ref, bi_ref, lam_ref, h0_ref,
                h_ref, hfin_ref, ext_s, a_s, b_s, carry_s, *, tt, n_t):
    d = pl.program_id(0)
    j = pl.program_id(1)
    jj = j + d * (n_t - 1 - 2 * j)
    nb = cur_ref.shape[1]
    half = D_A // 2
    ext_s[0:2] = jnp.where(jj == 0, 0.0, prev_ref[...])
    ext_s[2:tt + 2] = cur_ref[...]
    ext_s[tt + 2:tt + 3] = jnp.where(jj == n_t - 1, 0.0, next_ref[...])
    xc = ext_s[0:tt] * cw_ref[0]
    for k in range(1, CONV_W):
        xc = xc + ext_s[k:k + tt] * cw_ref[k]
    xc = xc + cb_ref[...]
    x2 = xc.reshape(tt * nb, D_A)
    xb = x2.astype(BF16)

    def gate(w_ref, bias_ref):
        lo = jnp.dot(xb[:, :half], w_ref[0], preferred_element_type=F32)
        hi = jnp.dot(xb[:, half:], w_ref[1], preferred_element_type=F32)
        return jax.nn.sigmoid(jnp.concatenate([lo, hi], axis=1) + bias_ref[...])

    r = gate(wa_ref, ba_ref)
    i = gate(wi_ref, bi_ref)
    log_a = LRU_C * r * jax.nn.log_sigmoid(lam_ref[...])
    a = jnp.exp(log_a)
    one_minus_a2 = -jnp.tanh(log_a) * (a * a + 1.0)
    a_s[...] = a.reshape(tt, nb, D_A)
    b_s[...] = (jnp.sqrt(one_minus_a2) * (i * x2)).reshape(tt, nb, D_A)

    @pl.when(j == 0)
    def _():
        carry_s[...] = h0_ref[...]

    def step(t, h):
        h = a_s[t] * h + b_s[t]
        h_ref[t] = h
        return h

    @pl.when(d == 0)
    def _():
        carry_s[...] = lax.fori_loop(0, tt, step, carry_s[...], unroll=8)

    @pl.when(d == 1)
    def _():
        carry_s[...] = lax.fori_loop(0, tt, lambda t, h: step(tt - 1 - t, h), carry_s[...], unroll=8)

    hfin_ref[...] = carry_s[...]


def _block_diag_halves(w):
    nd = w.shape[0]
    per = NB_A // 2
    w = w.reshape(nd, 2, per, BS_A, BS_A)
    eye = jnp.eye(per, dtype=w.dtype)
    out = jnp.einsum('dhncz,nm->dhncmz', w, eye)
    return out.reshape(nd, 2, per * BS_A, per * BS_A)


def _rglru(x3, conv_w, conv_b, wa, ba, wi, bi, lam, h0):
    T, B, _ = x3.shape
    tt = min(T, 1024 // B)
    n_t = T // tt

    def cur_map(d, j):
        return (j + d * (n_t - 1 - 2 * j), 0, 0)

    def prev_map(d, j):
        jj = j + d * (n_t - 1 - 2 * j)
        return (jnp.maximum(jj * (tt // 2) - 1, 0), 0, 0)

    def next_map(d, j):
        jj = j + d * (n_t - 1 - 2 * j)
        return (jnp.minimum((jj + 1) * tt, T - 1), 0, 0)

    dir_spec3 = pl.BlockSpec((None, 1, D_A), lambda d, j: (d, 0, 0))
    h, hfin = pl.pallas_call(
        functools.partial(_rglru_body, tt=tt, n_t=n_t),
        grid=(2, n_t),
        in_specs=[pl.BlockSpec((tt, B, D_A), cur_map),
                  pl.BlockSpec((2, B, D_A), prev_map),
                  pl.BlockSpec((1, B, D_A), next_map),
                  pl.BlockSpec((CONV_W, 1, D_A), lambda d, j: (0, 0, 0)),
                  pl.BlockSpec((1, 1, D_A), lambda d, j: (0, 0, 0)),
                  pl.BlockSpec((None, 2, D_A // 2, D_A // 2), lambda d, j: (d, 0, 0, 0)),
                  pl.BlockSpec((None, 2, D_A // 2, D_A // 2), lambda d, j: (d, 0, 0, 0)),
                  dir_spec3, dir_spec3, dir_spec3,
                  pl.BlockSpec((None, B, D_A), lambda d, j: (d, 0, 0))],
        out_specs=[pl.BlockSpec((None, tt, B, D_A), lambda d, j: (d,) + cur_map(d, j)),
                   pl.BlockSpec((None, B, D_A), lambda d, j: (d, 0, 0))],
        out_shape=[jax.ShapeDtypeStruct((2, T, B, D_A), F32),
                   jax.ShapeDtypeStruct((2, B, D_A), F32)],
        scratch_shapes=[pltpu.VMEM((tt + 3, B, D_A), F32),
                        pltpu.VMEM((tt, B, D_A), F32),
                        pltpu.VMEM((tt, B, D_A), F32),
                        pltpu.VMEM((B, D_A), F32)],
        compiler_params=_cparams(("arbitrary", "arbitrary")),
        name="rglru",
    )(x3, x3, x3, conv_w.reshape(CONV_W, 1, D_A), conv_b.reshape(1, 1, D_A),
      _block_diag_halves(wa).astype(BF16), _block_diag_halves(wi).astype(BF16),
      ba.reshape(2, 1, D_A), bi.reshape(2, 1, D_A), lam.reshape(2, 1, D_A), h0)
    return h, hfin


def _s5_matrices(a_re, a_im, log_dt, b_re, b_im, c_re, c_im):
    L = S5_L
    lam = lax.complex(a_re.astype(F32), a_im.astype(F32))
    dt = jnp.exp(log_dt.astype(F32))[..., None]
    ldt = lam * dt
    a_bar = jnp.exp(ldt)
    b_bar = ((a_bar - 1.0) / lam)[..., None] * lax.complex(b_re.astype(F32), b_im.astype(F32))
    cc = lax.complex(c_re.astype(F32), c_im.astype(F32))
    ks = jnp.arange(L + 1, dtype=F32)
    pw = jnp.exp(ldt[:, :, None, :] * ks[None, None, :, None].astype(jnp.complex64))
    kern = jnp.real(jnp.einsum('dgjp,dgkp,dgpi->dgkji', cc, pw[:, :, :L], b_bar, precision=HI))
    s_idx = jnp.arange(L)[:, None]
    t_idx = jnp.arange(L)[None, :]
    lag_f = jnp.clip(t_idx - s_idx, 0, L - 1)
    lag_b = jnp.clip(s_idx - t_idx, 0, L - 1)
    m_f = jnp.where((t_idx >= s_idx)[None, :, :, None, None], kern[0][:, lag_f], 0.0)
    m_b = jnp.where((s_idx >= t_idx)[None, :, :, None, None], kern[1][:, lag_b], 0.0)
    m = (m_f + m_b).transpose(0, 1, 4, 2, 3).reshape(G_B, S5_W, S5_W)
    down = (L - ks[:L])[None, :, None].astype(jnp.complex64)
    pw_down_b = jnp.exp(ldt[1][:, None, :] * down)
    g_f = jnp.exp(ldt[0][:, None, :] * (down - 1.0))[..., None] * b_bar[0][:, None]
    g_b = pw[1][:, :L, :, None] * b_bar[1][:, None]

    def g_cols(x):
        return x.transpose(0, 1, 3, 2).reshape(G_B, S5_W, P_B)

    gs = jnp.concatenate([g_cols(jnp.real(g_f)), g_cols(jnp.real(g_b)),
                          g_cols(jnp.imag(g_f)), g_cols(jnp.imag(g_b))], axis=-1)
    e_f = cc[0][:, None] * pw[0][:, 1:, None, :]
    e_b = cc[1][:, None] * pw_down_b[:, :, None, :]

    def e_rows(x):
        return x.transpose(0, 3, 1, 2).reshape(G_B, P_B, S5_W)

    e = jnp.concatenate([e_rows(jnp.real(e_f)), e_rows(jnp.real(e_b)),
                         -e_rows(jnp.imag(e_f)), -e_rows(jnp.imag(e_b))], axis=1)
    a_l = pw[:, :, L]
    al = jnp.concatenate([jnp.real(a_l[0]), jnp.real(a_l[1]), jnp.imag(a_l[0]), jnp.imag(a_l[1])], axis=-1)
    no = G_B // S5_OCT
    ow = S5_OCT * S5_W
    rows_sgi = lambda x: x.reshape(no, S5_OCT, L, S5_GROUP, S5_W).transpose(0, 2, 1, 3, 4).reshape(no, ow, S5_W)
    src = jnp.stack([rows_sgi(m), rows_sgi(gs), e.reshape(no, ow, S5_W)]).astype(BF16)
    r = jnp.arange(ow, dtype=I32)
    c = jnp.arange(S5_W, dtype=I32)
    grp_sgi = (r // S5_GROUP) % S5_OCT
    grp_gl = r // S5_W
    src_sgi = (r // (S5_OCT * S5_GROUP)) * S5_GROUP + r % S5_GROUP
    src_gl = r % S5_W
    spread = jnp.stack([src_sgi, src_gl, src_sgi])[:, None, :] == c[None, :, None]
    row_grp = jnp.stack([grp_sgi, grp_sgi, grp_gl])[:, :, None]
    col_grp = jnp.stack([grp_sgi, grp_gl, grp_sgi])[:, None, :]
    tr = 512
    out = pl.pallas_call(
        _s5_expand_body,
        grid=(3, no, ow // tr),
        in_specs=[pl.BlockSpec((None, None, tr, S5_W), lambda k, q, i: (k, q, i, 0)),
                  pl.BlockSpec((None, S5_W, ow), lambda k, q, i: (k, 0, 0)),
                  pl.BlockSpec((None, tr, 1), lambda k, q, i: (k, i, 0)),
                  pl.BlockSpec((None, 1, ow), lambda k, q, i: (k, 0, 0))],
        out_specs=pl.BlockSpec((None, None, tr, ow), lambda k, q, i: (k, q, i, 0)),
        out_shape=jax.ShapeDtypeStruct((3, no, ow, ow), BF16),
        compiler_params=_cparams(("arbitrary", "arbitrary", "arbitrary")),
        name="s5_expand",
    )(src, spread.astype(BF16), row_grp, col_grp)
    return out, al.reshape(1, G_B * S5_W)


def _s5_expand_body(src_ref, spread_ref, rg_ref, cg_ref, o_ref):
    wide = jnp.dot(src_ref[...], spread_ref[...], preferred_element_type=F32)
    o_ref[...] = jnp.where(rg_ref[...] == cg_ref[...], wide, 0.0).astype(BF16)


def _s5_fill_lhs(u_ref, lhs_s):
    tc, _, nb, lanes = u_ref.shape
    for s in range(S5_L):
        lhs_s[:, s * lanes:(s + 1) * lanes] = u_ref[:, s].reshape(tc * nb, lanes).astype(BF16)


def _s5_state_body(u_ref, g_ref, f_ref, lhs_s):
    _s5_fill_lhs(u_ref, lhs_s)
    f_ref[...] = jnp.dot(lhs_s[...], g_ref[...], preferred_element_type=F32).reshape(f_ref.shape)


def _s5_scan_body(f_ref, a_ref, s0_ref, sin_ref, sfin_ref, *, n_c):
    nb, width = s0_ref.shape
    hw = 2 * P_B
    n_g = width // S5_W
    a = a_ref[...]
    a_re = [a[:, k * S5_W:k * S5_W + hw] for k in range(n_g)]
    a_im = [a[:, k * S5_W + hw:(k + 1) * S5_W] for k in range(n_g)]
    is_fwd = (lax.broadcasted_iota(I32, (nb, width), 1) % hw) < P_B

    def split(x):
        return tuple(x[:, k * hw:(k + 1) * hw] for k in range(2 * n_g))

    def merge(parts):
        return jnp.concatenate(parts, axis=-1)

    def advance(c, parts):
        f = split(f_ref[c])
        out = []
        for k in range(n_g):
            s_re, s_im = parts[2 * k], parts[2 * k + 1]
            out.append(a_re[k] * s_re - a_im[k] * s_im + f[2 * k])
            out.append(a_re[k] * s_im + a_im[k] * s_re + f[2 * k + 1])
        return tuple(out)

    def fwd(c, parts):
        sin_ref[c] = merge(parts)
        return advance(c, parts)

    init = split(s0_ref[...])
    fin_f = lax.fori_loop(0, n_c, fwd, init)

    def bwd(k, parts):
        c = n_c - 1 - k
        sin_ref[c] = jnp.where(is_fwd, sin_ref[c], merge(parts))
        return advance(c, parts)

    fin_b = lax.fori_loop(0, n_c, bwd, init)
    sfin_ref[...] = jnp.where(is_fwd, merge(fin_f), merge(fin_b))


def _s5_out_body(u_ref, sin_ref, m_ref, e_ref, y_ref, lhs_s):
    _s5_fill_lhs(u_ref, lhs_s)
    tc, _, nb, lanes = u_ref.shape
    sin = sin_ref[...].reshape(tc * nb, sin_ref.shape[-1]).astype(BF16)
    y = (jnp.dot(lhs_s[...], m_ref[...], preferred_element_type=F32)
         + jnp.dot(sin, e_ref[...], preferred_element_type=F32))
    for s in range(S5_L):
        y_ref[:, s] = y[:, s * lanes:(s + 1) * lanes].reshape(tc, nb, lanes)


def _s5(ub, mats, s0):
    mge, al = mats
    T, B, _ = ub.shape
    n_c = T // S5_L
    no = G_B // S5_OCT
    lanes = S5_OCT * S5_GROUP
    ow = S5_OCT * S5_W
    tc = min(n_c, TOKEN_TILE // B)
    u4 = ub.reshape(n_c, S5_L, B, D_B)
    u_spec = pl.BlockSpec((tc, S5_L, B, lanes), lambda q, i: (i, 0, 0, q))
    w_spec = lambda kind: pl.BlockSpec((None, None, ow, ow), lambda q, i: (kind, q, 0, 0))
    st_spec = pl.BlockSpec((tc, B, ow), lambda q, i: (i, 0, q))
    f_loc = pl.pallas_call(
        _s5_state_body,
        grid=(no, n_c // tc),
        in_specs=[u_spec, w_spec(1)],
        out_specs=st_spec,
        out_shape=jax.ShapeDtypeStruct((n_c, B, G_B * S5_W), F32),
        scratch_shapes=[pltpu.VMEM((tc * B, S5_L * lanes), BF16)],
        compiler_params=_cparams(("arbitrary", "arbitrary")),
        name="s5_state",
    )(u4, mge)
    sw = 4 * S5_W
    sin, sfin = pl.pallas_call(
        functools.partial(_s5_scan_body, n_c=n_c),
        grid=(G_B * S5_W // sw,),
        in_specs=[pl.BlockSpec((n_c, B, sw), lambda g: (0, 0, g)),
                  pl.BlockSpec((1, sw), lambda g: (0, g)),
                  pl.BlockSpec((B, sw), lambda g: (0, g))],
        out_specs=[pl.BlockSpec((n_c, B, sw), lambda g: (0, 0, g)),
                   pl.BlockSpec((B, sw), lambda g: (0, g))],
        out_shape=[jax.ShapeDtypeStruct((n_c, B, G_B * S5_W), F32),
                   jax.ShapeDtypeStruct((B, G_B * S5_W), F32)],
        compiler_params=_cparams(("arbitrary",)),
        name="s5_scan",
    )(f_loc, al, s0)
    y = pl.pallas_call(
        _s5_out_body,
        grid=(no, n_c // tc),
        in_specs=[u_spec, st_spec, w_spec(0), w_spec(2)],
        out_specs=u_spec,
        out_shape=jax.ShapeDtypeStruct((n_c, S5_L, B, D_B), F32),
        scratch_shapes=[pltpu.VMEM((tc * B, S5_L * lanes), BF16)],
        compiler_params=_cparams(("arbitrary", "arbitrary")),
        name="s5_out",
    )(u4, sin, mge, mge)
    return y.reshape(T, B, D_B), sfin


def _s5_state_to_lanes(s_re, s_im):
    parts = [s_re[:, 0], s_re[:, 1], s_im[:, 0], s_im[:, 1]]
    return jnp.concatenate(parts, axis=-1).reshape(s_re.shape[0], G_B * S5_W)


def _s5_lanes_to_state(s):
    s = s.reshape(s.shape[0], G_B, 4, P_B)
    return jnp.stack([s[:, :, 0], s[:, :, 1]], axis=1), jnp.stack([s[:, :, 2], s[:, :, 3]], axis=1)


def _outproj0_body(x_ref, h_ref, ga_ref, y_ref, ub_ref, g1_ref, d_ref, gw_ref, gb_ref, wo_ref,
                   lg_ref, lb_ref, o_ref):
    nb, tq, d = x_ref.shape
    rows = tq * nb

    def flat(v):
        return v.reshape(rows, v.shape[-1])

    ya = flat(h_ref[0] + h_ref[1]) * jax.nn.gelu(flat(ga_ref[...]))
    yb = flat(y_ref[...]) + d_ref[...] * flat(ub_ref[...])
    g = jax.nn.gelu(yb)
    gate = jax.nn.sigmoid(jnp.dot(g.astype(BF16), gw_ref[...], preferred_element_type=F32) + gb_ref[...])
    cat = jnp.concatenate([ya, g * gate], axis=1).astype(BF16)
    cat = jnp.dot(_row_permutation(nb, tq), cat, preferred_element_type=F32).astype(BF16)
    out = jnp.dot(cat, wo_ref[...], preferred_element_type=F32).reshape(nb, tq, d)
    v = ALPHA_DN * x_ref[...] + g1_ref[...] * out
    o_ref[...] = _layer_norm(v, lg_ref[...], lb_ref[...])


def _outproj0(x, h, ga, y5, ub, pmods, s5_d, glu_w, glu_b, w_out0, ln_g, ln_b):
    B, T, D = x.shape
    tq = TOKEN_TILE // B
    tmaj = pl.BlockSpec((tq, B, D_A), lambda i: (i, 0, 0))
    vec = lambda w: pl.BlockSpec((1, w), lambda i: (0, 0))
    return pl.pallas_call(
        _outproj0_body,
        grid=(T // tq,),
        in_specs=[pl.BlockSpec((B, tq, D), lambda i: (0, i, 0)),
                  pl.BlockSpec((2, tq, B, D_A), lambda i: (0, i, 0, 0)),
                  tmaj, tmaj, tmaj,
                  _mod_spec_all(0, 2, B),
                  vec(D_B),
                  pl.BlockSpec((D_B, D_B), lambda i: (0, 0)),
                  vec(D_B),
                  pl.BlockSpec((D, D), lambda i: (0, 0)),
                  vec(D), vec(D)],
        out_specs=pl.BlockSpec((B, tq, D), lambda i: (0, i, 0)),
        out_shape=jax.ShapeDtypeStruct((B, T, D), F32),
        compiler_params=_cparams(("arbitrary",)),
        name="outproj0",
    )(x, h, ga, y5, ub, pmods, s5_d.reshape(1, D_B), glu_w.astype(BF16), glu_b.reshape(1, D_B),
      w_out0.astype(BF16), ln_g.reshape(1, D), ln_b.reshape(1, D))


def _router_body(x_ref, sc_ref, sh_ref, wr_ref, br_ref, u_ref, ldest_ref, gate_ref, cnt_ref):
    u = x_ref[...] * (1.0 + sc_ref[...]) + sh_ref[...]
    ub = u.astype(BF16)
    u_ref[...] = ub
    tm = u.shape[0]
    u_lo = (u - ub.astype(F32)).astype(BF16)
    nt = (((1,), (1,)), ((), ()))
    both = lax.dot_general(wr_ref[...], ub, nt, preferred_element_type=F32)
    logits = (both[:N_EXP] + both[N_EXP:] + lax.dot_general(wr_ref[:N_EXP], u_lo, nt, preferred_element_type=F32)
              + br_ref[...])
    e_iota = lax.broadcasted_iota(I32, logits.shape, 0)
    work = logits
    vals, hots = [], []
    for _ in range(TOP_K):
        m = jnp.max(work, axis=0, keepdims=True)
        idx = jnp.min(jnp.where(work == m, e_iota, N_EXP), axis=0, keepdims=True)
        hot = e_iota == idx
        vals.append(m)
        hots.append(hot)
        work = jnp.where(hot, -jnp.inf, work)
    ex = [jnp.exp(v - vals[0]) for v in vals]
    den = ex[0] + ex[1] + ex[2] + ex[3]
    gate_ref[...] = jnp.concatenate([e / den for e in ex], axis=0)
    hot_sum = jnp.zeros(logits.shape, F32)
    for hot in hots:
        hot_sum = hot_sum + hot.astype(F32)
    hot_b = hot_sum.astype(BF16)
    before = lax.broadcasted_iota(I32, (tm, tm), 0) < lax.broadcasted_iota(I32, (tm, tm), 1)
    excl = jnp.dot(hot_b, jnp.where(before, 1.0, 0.0).astype(BF16), preferred_element_type=F32)
    cnt_row = lax.dot_general(jnp.ones((8, tm), BF16), hot_b, (((1,), (1,)), ((), ())),
                              preferred_element_type=F32)[0:1]
    cnt_ref[...] = cnt_row.astype(I32)
    run_len = jnp.ceil(cnt_row * (1.0 / RUN_ALIGN)) * RUN_ALIGN
    lower = lax.broadcasted_iota(I32, (N_EXP, N_EXP), 1) < lax.broadcasted_iota(I32, (N_EXP, N_EXP), 0)
    run_off = jnp.sum(jnp.where(lower, run_len, 0.0), axis=1, keepdims=True)
    base = excl + run_off
    rows = [jnp.sum(jnp.where(hot, base, 0.0), axis=0, keepdims=True) for hot in hots]
    ldest_ref[...] = jnp.concatenate(rows, axis=0).astype(I32)


def _router(x2, pmods, layer, rows_per_mod, tm, w_router, b_router):
    N, D = x2.shape
    n_t = N // tm
    wr_hi = w_router.T.astype(BF16)
    wr_lo = (w_router.T - wr_hi.astype(F32)).astype(BF16)
    mod = lambda which: pl.BlockSpec((None, None, None, 1, D),
                                     lambda i: (layer, which, (i * tm) // rows_per_mod, 0, 0))
    lane_spec = pl.BlockSpec((TOP_K, tm), lambda i: (0, i))
    return pl.pallas_call(
        _router_body,
        grid=(n_t,),
        in_specs=[pl.BlockSpec((tm, D), lambda i: (i, 0)),
                  mod(4), mod(3),
                  pl.BlockSpec((2 * N_EXP, D), lambda i: (0, 0)),
                  pl.BlockSpec((N_EXP, 1), lambda i: (0, 0))],
        out_specs=[pl.BlockSpec((tm, D), lambda i: (i, 0)),
                   lane_spec, lane_spec,
                   pl.BlockSpec((None, 1, N_EXP), lambda i: (i, 0, 0))],
        out_shape=[jax.ShapeDtypeStruct((N, D), BF16),
                   jax.ShapeDtypeStruct((TOP_K, N), I32),
                   jax.ShapeDtypeStruct((TOP_K, N), F32),
                   jax.ShapeDtypeStruct((n_t, 1, N_EXP), I32)],
        compiler_params=_cparams(("arbitrary",)),
        name="router",
    )(x2, pmods, pmods, jnp.concatenate([wr_hi, wr_lo], axis=0), b_router.reshape(N_EXP, 1))


def _expert_body(blk_e_ref, n_used_ref, xs_ref, wgu_ref, bgu_ref, wdn_ref, bdn_ref, o_ref, wgu_s, wdn_s):
    i = pl.program_id(0)
    prev = blk_e_ref[jnp.maximum(i - 1, 0)]
    changed = jnp.logical_or(i == 0, blk_e_ref[i] != prev)

    @pl.when(changed)
    def _():
        wgu_s[...] = wgu_ref[...].astype(BF16)
        wdn_s[...] = wdn_ref[...].astype(BF16)

    @pl.when(i < n_used_ref[0])
    def _():
        x = xs_ref[...]
        half = D_FF // 2
        y = bdn_ref[...]
        for c0 in (0, half):
            gt = jnp.dot(x, wgu_s[:, c0:c0 + half], preferred_element_type=F32) + bgu_ref[:, c0:c0 + half]
            up = (jnp.dot(x, wgu_s[:, D_FF + c0:D_FF + c0 + half], preferred_element_type=F32)
                  + bgu_ref[:, D_FF + c0:D_FF + c0 + half])
            gt = jnp.minimum(gt, SWIGLU_LIMIT)
            up = jnp.clip(up, -SWIGLU_LIMIT, SWIGLU_LIMIT)
            act = (up + 1.0) * gt * jax.nn.sigmoid(SWIGLU_ALPHA * gt)
            y = y + jnp.dot(act.astype(BF16), wdn_s[c0:c0 + half, :], preferred_element_type=F32)
        o_ref[...] = y.astype(o_ref.dtype)

    @pl.when(i >= n_used_ref[0])
    def _():
        o_ref[...] = jnp.zeros_like(o_ref)


def _experts(xs, blk_e, n_used, layer, w_gu, b_gu, w_dn, b_dn):
    slots, D = xs.shape
    te = EXPERT_TILE
    n_blk = slots // te
    grid_spec = pltpu.PrefetchScalarGridSpec(
        num_scalar_prefetch=2,
        grid=(n_blk,),
        in_specs=[pl.BlockSpec((te, D), lambda i, be, nu: (jnp.minimum(i, nu[0] - 1), 0)),
                  pl.BlockSpec((None, None, D, 2 * D_FF), lambda i, be, nu: (layer, be[i], 0, 0)),
                  pl.BlockSpec((None, None, 1, 2 * D_FF), lambda i, be, nu: (layer, be[i], 0, 0)),
                  pl.BlockSpec((None, None, D_FF, D), lambda i, be, nu: (layer, be[i], 0, 0)),
                  pl.BlockSpec((None, None, 1, D), lambda i, be, nu: (layer, be[i], 0, 0))],
        out_specs=pl.BlockSpec((te, D), lambda i, be, nu: (i, 0)),
        scratch_shapes=[pltpu.VMEM((D, 2 * D_FF), BF16), pltpu.VMEM((D_FF, D), BF16)],
    )
    return pl.pallas_call(
        _expert_body,
        grid_spec=grid_spec,
        out_shape=jax.ShapeDtypeStruct((slots, D), BF16),
        compiler_params=_cparams(("arbitrary",)),
        name="experts",
    )(blk_e, n_used, xs, w_gu, b_gu.reshape(DEPTH, N_EXP, 1, 2 * D_FF), w_dn, b_dn.reshape(DEPTH, N_EXP, 1, D))


def _chunk_copy(vmem_ref, hbm_ref, sem, k, dst_chunk, to_hbm):
    row = k * RUN_ALIGN if isinstance(k, int) else pl.multiple_of(k * RUN_ALIGN, RUN_ALIGN)
    local = vmem_ref.at[pl.ds(row, RUN_ALIGN)]
    remote = hbm_ref.at[pl.ds(pl.multiple_of(dst_chunk * RUN_ALIGN, RUN_ALIGN), RUN_ALIGN)]
    return pltpu.make_async_copy(local, remote, sem) if to_hbm else pltpu.make_async_copy(remote, local, sem)


def _dispatch_body(dst_ref, tail_ref, nu_ref, *rest, n_chunks, tiles, n_blk):
    n_p = len(tiles)
    xs_ref, buf_s, zero_s, sem = rest[2 * n_p:]
    i = pl.program_id(0)
    n_t = pl.num_programs(0)
    slot = i % 2
    lb = n_chunks * RUN_ALIGN
    te = zero_s.shape[0]
    rb = 256

    def sort_rows(u_ref, ld_ref):
        tm = u_ref.shape[0]
        ld = ld_ref[...]
        u = u_ref[...]
        for j in range(lb // rb):
            r = lax.broadcasted_iota(I32, (rb, tm), 0) + j * rb
            p = jnp.where(r == ld[0:1], 1.0, jnp.where(r == ld[1:2], 1.0, jnp.where(
                r == ld[2:3], 1.0, jnp.where(r == ld[3:4], 1.0, 0.0))))
            buf_s[slot, j * rb:(j + 1) * rb, :] = jnp.dot(p.astype(BF16), u,
                                                          preferred_element_type=F32).astype(BF16)
            for k in range(j * rb // RUN_ALIGN, (j + 1) * rb // RUN_ALIGN):
                _chunk_copy(buf_s.at[slot], xs_ref, sem.at[slot], k, dst_ref[i, k], True).start()

    for p, (first, count) in enumerate(tiles):
        @pl.when(jnp.logical_and(i >= first, i < first + count))
        def _(p=p):
            sort_rows(rest[2 * p], rest[2 * p + 1])

    def drain(s):
        pltpu.make_async_copy(buf_s.at[s], xs_ref.at[pl.ds(0, lb)], sem.at[s]).wait()

    @pl.when(i > 0)
    def _():
        drain(1 - slot)

    @pl.when(i == n_t - 1)
    def _():
        drain(slot)
        zero_s[...] = jnp.zeros_like(zero_s)
        n_tail = tail_ref.shape[0]

        def tail_copy(t):
            return _chunk_copy(zero_s, xs_ref, sem.at[2], 0, tail_ref[t], True)

        def block_copy(k):
            return pltpu.make_async_copy(zero_s, xs_ref.at[pl.ds(pl.multiple_of(k * te, te), te)], sem.at[2])

        def each(n, pred, copy, wait):
            def body(t, _):
                @pl.when(pred(t))
                def _():
                    copy(t).wait() if wait else copy(t).start()
                return 0
            lax.fori_loop(0, n, body, 0)

        for wait in (False, True):
            each(n_tail, lambda t: tail_ref[t] >= 0, tail_copy, wait)
            each(n_blk, lambda k: k >= nu_ref[0], block_copy, wait)


def _dispatch(u2s, ldests, dst, tails, n_used, slots, tm, te):
    D = u2s[0].shape[1]
    n_t, n_chunks = dst.shape
    lb = n_chunks * RUN_ALIGN
    tiles, first = [], 0
    for u2 in u2s:
        tiles.append((first, u2.shape[0] // tm))
        first += u2.shape[0] // tm
    assert first == n_t

    def own_tile(first, count):
        return lambda i, d, t, nu: jnp.clip(i - first, 0, count - 1)

    in_specs, operands = [], []
    for (first, count), u2, ld in zip(tiles, u2s, ldests):
        tile = own_tile(first, count)
        in_specs += [pl.BlockSpec((tm, D), lambda i, d, t, nu, tile=tile: (tile(i, d, t, nu), 0)),
                     pl.BlockSpec((TOP_K, tm), lambda i, d, t, nu, tile=tile: (0, tile(i, d, t, nu)))]
        operands += [u2, ld]
    grid_spec = pltpu.PrefetchScalarGridSpec(
        num_scalar_prefetch=3,
        grid=(n_t,),
        in_specs=in_specs,
        out_specs=pl.BlockSpec(memory_space=pl.ANY),
        scratch_shapes=[pltpu.VMEM((2, lb, D), BF16), pltpu.VMEM((te, D), BF16), pltpu.SemaphoreType.DMA((3,))],
    )
    return pl.pallas_call(
        functools.partial(_dispatch_body, n_chunks=n_chunks, tiles=tuple(tiles), n_blk=slots // te),
        grid_spec=grid_spec,
        out_shape=jax.ShapeDtypeStruct((slots, D), BF16),
        compiler_params=_cparams(("arbitrary",)),
        name="moe_dispatch",
    )(dst, tails, n_used, *operands)


def _combine_body(dst_ref, x_ref, ld_ref, gate_ref, g2_ref, lg_ref, lb_ref, ys_ref, o_ref, buf_s, sem, *, n_chunks):
    i = pl.program_id(0)
    n_t = pl.num_programs(0)
    slot = i % 2
    tm = x_ref.shape[0]
    lb = n_chunks * RUN_ALIGN

    def wait_all(s):
        pltpu.make_async_copy(ys_ref.at[pl.ds(0, lb)], buf_s.at[s], sem.at[s]).wait()

    @pl.when(i == 0)
    def _():
        def body(k, _):
            _chunk_copy(buf_s.at[0], ys_ref, sem.at[0], k, dst_ref[0, k], False).start()
            return 0
        lax.fori_loop(0, n_chunks, body, 0, unroll=8)

    wait_all(slot)
    nxt = jnp.minimum(i + 1, n_t - 1)
    ld = ld_ref[...]
    gates = gate_ref[...]
    cb = 512
    y = jnp.zeros((tm, x_ref.shape[1]), F32)
    for j in range(lb // cb):
        for k in range(j * cb // RUN_ALIGN, (j + 1) * cb // RUN_ALIGN):
            _chunk_copy(buf_s.at[1 - slot], ys_ref, sem.at[1 - slot], k, dst_ref[nxt, k], False).start()
        c = lax.broadcasted_iota(I32, (tm, cb), 1) + j * cb
        w = jnp.zeros((tm, cb), F32)
        for k in range(TOP_K):
            w = jnp.where(c == ld[:, k:k + 1], gates[:, k:k + 1], w)
        y = y + jnp.dot(w.astype(BF16), buf_s[slot, j * cb:(j + 1) * cb, :], preferred_element_type=F32)
    v = ALPHA_DN * x_ref[...] + g2_ref[...] * y
    o_ref[...] = _layer_norm(v, lg_ref[...], lb_ref[...])

    @pl.when(i == n_t - 1)
    def _():
        wait_all(1 - slot)


def _combine(x2, ldest_col, gates_col, dst, ys, pmods, layer, rows_per_mod, tm, ln_g, ln_b):
    N, D = x2.shape
    n_t, n_chunks = dst.shape
    lb = n_chunks * RUN_ALIGN
    vec = pl.BlockSpec((1, D), lambda i, d: (0, 0))
    grid_spec = pltpu.PrefetchScalarGridSpec(
        num_scalar_prefetch=1,
        grid=(n_t,),
        in_specs=[pl.BlockSpec((tm, D), lambda i, d: (i, 0)),
                  pl.BlockSpec((tm, TOP_K), lambda i, d: (i, 0)),
                  pl.BlockSpec((tm, TOP_K), lambda i, d: (i, 0)),
                  pl.BlockSpec((None, None, None, 1, D), lambda i, d: (layer, 5, (i * tm) // rows_per_mod, 0, 0)),
                  vec, vec,
                  pl.BlockSpec(memory_space=pl.ANY)],
        out_specs=pl.BlockSpec((tm, D), lambda i, d: (i, 0)),
        scratch_shapes=[pltpu.VMEM((2, lb, D), BF16), pltpu.SemaphoreType.DMA((2,))],
    )
    return pl.pallas_call(
        functools.partial(_combine_body, n_chunks=n_chunks),
        grid_spec=grid_spec,
        out_shape=jax.ShapeDtypeStruct((N, D), F32),
        compiler_params=_cparams(("arbitrary",)),
        name="moe_combine",
    )(dst, x2, ldest_col, gates_col, pmods, ln_g.reshape(1, D), ln_b.reshape(1, D), ys)


def _moe_plan(cnts, n_chunks, te, spare_chunk):
    a = RUN_ALIGN
    cnt = jnp.concatenate(cnts, axis=0)
    pc = (cnt + a - 1) // a * a
    seg = (jnp.sum(pc, axis=0) + te - 1) // te * te
    pad_end = jnp.cumsum(seg)
    run_start = (pad_end - seg)[None, :] + jnp.cumsum(pc, axis=0) - pc
    lo = jnp.cumsum(pc, axis=1) - pc
    tables, t0 = [], 0
    for c, nc in zip(cnts, n_chunks):
        sl = slice(t0, t0 + c.shape[0])
        parity = ((t0 + jnp.arange(c.shape[0], dtype=I32)) % 2)[:, None]
        t0 += c.shape[0]
        pos = jnp.arange(nc, dtype=I32) * a
        owner = jnp.sum(((lo[sl] + pc[sl])[:, None, :] <= pos[None, :, None]).astype(I32), axis=-1)
        mine = owner[..., None] == jnp.arange(N_EXP, dtype=I32)
        base = jnp.sum(jnp.where(mine, (run_start[sl] - lo[sl])[:, None, :], 0), axis=-1)
        valid = pos[None, :] < jnp.sum(pc[sl], axis=1, keepdims=True)
        spare = spare_chunk + parity * nc + jnp.arange(nc, dtype=I32)[None, :]
        tables.append(jnp.where(valid, (base + pos[None, :]) // a, spare).astype(I32))
    tail_row = (pad_end - seg + jnp.sum(pc, axis=0))[:, None] + jnp.arange(te // a, dtype=I32)[None, :] * a
    tails = jnp.where(tail_row < pad_end[:, None], tail_row // a, -1).astype(I32).reshape(-1)
    return tables, tails, pad_end


def _moe(passes, layer, P):
    te = EXPERT_TILE
    routed = []
    for x, pmods, shared_mod in passes:
        B, T, D = x.shape
        N = B * T
        tm = min(MOE_TILE, N if shared_mod else T)
        x2 = x.reshape(N, D)
        u2, ldest, gates, cnt = _router(x2, pmods, layer, T, tm, P['w_router'][layer], P['b_router'][layer])
        routed.append((x2, pmods, T, tm, u2, ldest, gates, cnt[:, 0, :], x.shape))
    n_chunks = [r[3] * TOP_K // RUN_ALIGN + N_EXP for r in routed]
    n_tiles = [r[7].shape[0] for r in routed]
    n_assign = sum(r[0].shape[0] for r in routed) * TOP_K
    seg_rows = -(-(n_assign + sum(n_tiles) * N_EXP * RUN_ALIGN + N_EXP * te) // te) * te
    slots = seg_rows + -(-(2 * max(n_chunks) * RUN_ALIGN) // te) * te
    dsts, tails, pad_end = _moe_plan([r[7] for r in routed], n_chunks, te, seg_rows // RUN_ALIGN)
    n_blk = slots // te
    blk_pos = jnp.arange(n_blk, dtype=I32) * te
    blk_e = jnp.minimum(jnp.sum((pad_end[None, :] <= blk_pos[:, None]).astype(I32), axis=1), N_EXP - 1)
    n_used = (pad_end[-1] // te).astype(I32).reshape(1)
    assert len(set(r[3] for r in routed)) == 1, "all passes must use the same MoE tile"
    xs = _dispatch([r[4] for r in routed], [r[5] for r in routed], jnp.concatenate(dsts, axis=0), tails, n_used,
                   slots, routed[0][3], te)
    ys = _experts(xs, blk_e, n_used, layer, P['w_gu'], P['b_gu'], P['w_down'], P['b_down'])
    outs = []
    for r, dst in zip(routed, dsts):
        x2, pmods, T, tm, _, ldest, gates, _, shape = r
        y = _combine(x2, ldest.T, gates.T, dst, ys, pmods, layer, T, tm, P['ln_g'][layer, 1], P['ln_b'][layer, 1])
        outs.append(y.reshape(shape))
    return outs


def _split3(x):
    hi = x.astype(BF16)
    r1 = x - hi.astype(F32)
    mid = r1.astype(BF16)
    lo = (r1 - mid.astype(F32)).astype(BF16)
    return jnp.concatenate([hi, mid, lo], axis=1)


def _inproj1_body(x_ref, xp_ref, xn_ref, sc_ref, sh_ref, w_ref, wvt_ref, cw_ref, cb_ref, wg2_ref, wgh_ref,
                  bg_ref, qk_ref, vt_ref, o_ref, gcol_ref, grow_ref):
    j = pl.program_id(1)
    n_j = pl.num_programs(1)
    halo = xp_ref.shape[0]
    tm = x_ref.shape[0]
    nh = 4 * H_C

    def adaln(v):
        return v * (1.0 + sc_ref[...]) + sh_ref[...]

    u = adaln(x_ref[...])
    ub = u.astype(BF16)
    u_prev = jnp.where(j > 0, adaln(xp_ref[...]), 0.0).astype(BF16)
    u_next = jnp.where(j < n_j - 1, adaln(xn_ref[...]), 0.0).astype(BF16)
    u_ext = jnp.concatenate([u_prev, ub, u_next], axis=0)
    n_ext = tm + 2 * halo
    cblk = 256
    n_blk = 2 * D_MODEL // cblk

    def matmuls(b):
        c0 = b * cblk
        zqk = jnp.dot(u_ext, w_ref[:, c0:c0 + cblk], preferred_element_type=F32)
        if c0 < D_MODEL:
            vt = lax.dot_general(wvt_ref[c0:c0 + cblk, :], ub, (((1,), (1,)), ((), ())), preferred_element_type=F32)
            vt_ref[c0:c0 + cblk, :] = vt.astype(BF16)
        else:
            o0 = c0 - D_MODEL
            o_ref[:, o0:o0 + cblk] = jnp.dot(ub, w_ref[:, 3 * D_MODEL + o0:3 * D_MODEL + o0 + cblk],
                                             preferred_element_type=F32)
        return zqk

    z_next = matmuls(0)
    for b in range(n_blk):
        zqk = z_next
        if b + 1 < n_blk:
            z_next = matmuls(b + 1)
        cols = slice(b * cblk, (b + 1) * cblk)
        acc = (pltpu.roll(zqk, 2, 0) * cw_ref[0:1, cols] + pltpu.roll(zqk, 1, 0) * cw_ref[1:2, cols]
               + zqk * cw_ref[2:3, cols] + pltpu.roll(zqk, n_ext - 1, 0) * cw_ref[3:4, cols])[halo:tm + halo]
        act = jax.nn.silu(acc + cb_ref[:, cols])
        if b * cblk >= D_MODEL:
            act = act * (DH_C ** -0.5)
        qk_ref[:, cols] = act.astype(BF16)
    u_lo = (u - ub.astype(F32)).astype(BF16)
    g2 = jnp.dot(ub, wg2_ref[...], preferred_element_type=F32)
    gc = g2[:, :nh] + g2[:, nh:] + jnp.dot(u_lo, wgh_ref[...], preferred_element_type=F32) + bg_ref[...]
    r_i = lax.broadcasted_iota(I32, (tm, tm), 0)
    c_i = lax.broadcasted_iota(I32, (tm, tm), 1)
    same = (r_i // CHUNK) == (c_i // CHUNK)
    tri_f = jnp.where(jnp.logical_and(same, c_i <= r_i), 1.0, 0.0).astype(BF16)
    tri_b = jnp.where(jnp.logical_and(same, c_i >= r_i), 1.0, 0.0).astype(BF16)
    lf3 = _split3(jax.nn.log_sigmoid(gc))

    def sum3(p, axis):
        if axis == 1:
            return p[:, :nh] + p[:, nh:2 * nh] + p[:, 2 * nh:]
        return p[:nh] + p[nh:2 * nh] + p[2 * nh:]

    col_i = lax.broadcasted_iota(I32, (tm, nh), 1)
    cum = jnp.where(col_i < 2 * H_C,
                    sum3(jnp.dot(tri_f, lf3, preferred_element_type=F32), 1),
                    sum3(jnp.dot(tri_b, lf3, preferred_element_type=F32), 1))
    gcol = jnp.where((col_i // H_C) % 2 == 1, cum, gc)
    gcol_ref[...] = gcol
    eye = jnp.where(r_i == c_i, 1.0, 0.0).astype(BF16)
    gt3 = lax.dot_general(_split3(gcol), eye, (((0,), (0,)), ((), ())), preferred_element_type=F32)
    grow_ref[...] = sum3(gt3, 0)


def _inproj1(x, mods, w_in1, w_gate1, b_gate1, conv_w, conv_b, row_off, row_stride):
    B, T, D = x.shape
    tm = min(2 * TOKEN_TILE, T)
    n_j = T // tm
    nh = 4 * H_C
    halo = 8
    wg_hi = w_gate1.astype(BF16)
    wg_lo = (w_gate1 - wg_hi.astype(F32)).astype(BF16)
    w_bf = w_in1.astype(BF16)
    tok = lambda w: pl.BlockSpec((None, tm, w), lambda b, j: (b, j, 0))
    full = lambda r, c: pl.BlockSpec((r, c), lambda b, j: (0, 0))
    return pl.pallas_call(
        _inproj1_body,
        grid=(B, n_j),
        in_specs=[tok(D),
                  pl.BlockSpec((None, halo, D), lambda b, j: (b, jnp.maximum(j * (tm // halo) - 1, 0), 0)),
                  pl.BlockSpec((None, halo, D), lambda b, j: (b, jnp.minimum((j + 1) * (tm // halo), T // halo - 1), 0)),
                  _mod_spec(1, 1, row_off, row_stride),
                  _mod_spec(1, 0, row_off, row_stride),
                  full(D, 4 * D), full(D, D), full(CONV_W, 2 * D), full(1, 2 * D),
                  full(D, 2 * nh), full(D, nh), full(1, nh)],
        out_specs=[tok(2 * D),
                   pl.BlockSpec((None, D, tm), lambda b, j: (b, 0, j)),
                   tok(D), tok(nh),
                   pl.BlockSpec((nh, tm), lambda b, j: (0, b * n_j + j))],
        out_shape=[jax.ShapeDtypeStruct((B, T, 2 * D), BF16),
                   jax.ShapeDtypeStruct((B, D, T), BF16),
                   jax.ShapeDtypeStruct((B, T, D), F32),
                   jax.ShapeDtypeStruct((B, T, nh), F32),
                   jax.ShapeDtypeStruct((nh, B * T), F32)],
        compiler_params=_cparams(("arbitrary", "arbitrary")),
        name="inproj1",
    )(x, x, x, mods, mods, w_bf, w_bf[:, 2 * D:3 * D].T, conv_w, conv_b.reshape(1, 2 * D),
      jnp.concatenate([wg_hi, wg_lo], axis=1), wg_hi, b_gate1.reshape(1, nh))


def _mlstm_body(q_ref, k_ref, vt_ref, gcol_ref, grow_ref, c0_ref, n0_ref, m0_ref,
                h_ref, cfin_ref, nfin_ref, mfin_ref, hb_s, *, T):
    n_c = T // CHUNK
    L = CHUNK

    rb = min(T, 512)
    t_i = lax.broadcasted_iota(I32, (L, L), 0)
    s_i = lax.broadcasted_iota(I32, (L, L), 1)

    cfin_ref[...] = c0_ref[...]
    nfin_ref[...] = n0_ref[...]
    mfin_ref[...] = m0_ref[...]

    def chunk(d, c):
        mask = (s_i <= t_i) if d == 0 else (s_i >= t_i)
        last = L - 1 if d == 0 else 0
        off = pl.multiple_of(c * L, L)
        qc = q_ref[pl.ds(off, L), :]
        kc = k_ref[pl.ds(off, L), :]
        vt = vt_ref[:, pl.ds(off, L)]
        gcol = gcol_ref[pl.ds(off, L), :]
        grow = grow_ref[:, pl.ds(off, L)]
        b_c = gcol[:, 2 * d + 1:2 * d + 2]
        li_r = grow[2 * d:2 * d + 1, :]
        b_r = grow[2 * d + 1:2 * d + 2, :]
        m_prev = mfin_ref[d]
        log_d = jnp.where(mask, b_c - b_r + li_r, -jnp.inf)
        m_inter = b_c + m_prev
        m_t = jnp.maximum(m_inter, jnp.max(log_d, axis=-1, keepdims=True))
        dmat = jnp.exp(log_d - m_t)
        w_inter = jnp.exp(m_inter - m_t)
        c_old = cfin_ref[d]
        s = lax.dot_general(qc, kc, (((1,), (1,)), ((), ())), preferred_element_type=F32) * dmat
        inter = lax.dot_general(qc, c_old.astype(BF16), (((1,), (1,)), ((), ())), preferred_element_type=F32)
        num = w_inter * inter + lax.dot_general(s.astype(BF16), vt, (((1,), (1,)), ((), ())),
                                                preferred_element_type=F32)
        qn = jnp.sum(qc.astype(F32) * nfin_ref[d], axis=-1, keepdims=True)
        den = w_inter * qn + jnp.sum(s, axis=-1, keepdims=True)
        h = num / jnp.maximum(jnp.abs(den), jnp.exp(-m_t))
        m_new = m_t[last:last + 1, :]
        b_last = b_c[last:last + 1, :]
        w_s = jnp.exp(b_last - b_r + li_r - m_new)
        decay = jnp.exp(b_last + m_prev - m_new)
        wvt = (vt.astype(F32) * w_s).astype(BF16)
        cfin_ref[d] = decay * c_old + jnp.dot(wvt, kc, preferred_element_type=F32)
        wk = jnp.dot(jnp.broadcast_to(w_s, (8, L)).astype(BF16), kc, preferred_element_type=F32)[0:1]
        nfin_ref[d] = decay * nfin_ref[d] + wk
        mfin_ref[d] = m_new
        return off, h

    def both(ci, _):
        off_f, h_f = chunk(0, ci)
        off_b, h_b = chunk(1, n_c - 1 - ci)
        h_ref[pl.ds(off_f, L), :] = h_f
        hb_s[pl.ds(off_b, L), :] = h_b
        return 0

    lax.fori_loop(0, n_c, both, 0)

    def add_bwd(r, _):
        off = pl.multiple_of(r * rb, rb)
        h_ref[pl.ds(off, rb), :] = h_ref[pl.ds(off, rb), :] + hb_s[pl.ds(off, rb), :]
        return 0

    lax.fori_loop(0, T // rb, add_bwd, 0)


def _mlstm(qk, vt, gcol, grow, c0, n0, m0):
    B, _, T = vt.shape
    DH = DH_C
    st = lambda *tail: pl.BlockSpec((None, 2, None) + tail, lambda b, h: (b, 0, h) + (0,) * len(tail))
    return pl.pallas_call(
        functools.partial(_mlstm_body, T=T),
        grid=(B, H_C),
        in_specs=[pl.BlockSpec((None, T, DH), lambda b, h: (b, 0, h)),
                  pl.BlockSpec((None, T, DH), lambda b, h: (b, 0, H_C + h)),
                  pl.BlockSpec((None, DH, T), lambda b, h: (b, h, 0)),
                  pl.BlockSpec((None, None, T, 4), lambda b, h: (b, h, 0, 0)),
                  pl.BlockSpec((None, None, 4, T), lambda b, h: (b, h, 0, 0)),
                  st(DH, DH), st(1, DH), st(1, 1)],
        out_specs=[pl.BlockSpec((None, T, DH), lambda b, h: (b, 0, h)),
                   st(DH, DH), st(1, DH), st(1, 1)],
        out_shape=[jax.ShapeDtypeStruct((B, T, D_MODEL), F32),
                   jax.ShapeDtypeStruct((B, 2, H_C, DH, DH), F32),
                   jax.ShapeDtypeStruct((B, 2, H_C, 1, DH), F32),
                   jax.ShapeDtypeStruct((B, 2, H_C, 1, 1), F32)],
        scratch_shapes=[pltpu.VMEM((T, DH), F32)],
        compiler_params=_cparams(("arbitrary", "arbitrary")),
        name="mlstm",
    )(qk, qk, vt, gcol, grow, c0, n0, m0)


def _outproj1_body(x_ref, h_ref, o_ref, g1_ref, w_ref, lg_ref, lb_ref, out_ref):
    y = jax.nn.sigmoid(o_ref[...]) * h_ref[...]
    out = jnp.dot(y.astype(BF16), w_ref[...], preferred_element_type=F32)
    v = ALPHA_DN * x_ref[...] + g1_ref[...] * out
    out_ref[...] = _layer_norm(v, lg_ref[...], lb_ref[...])


def _outproj1(x, h, o, mods, w_out1, ln_g, ln_b, row_off, row_stride):
    B, T, D = x.shape
    tm = min(TOKEN_TILE, T)
    tok = pl.BlockSpec((None, tm, D), lambda b, j: (b, j, 0))
    vec = pl.BlockSpec((1, D), lambda b, j: (0, 0))
    return pl.pallas_call(
        _outproj1_body,
        grid=(B, T // tm),
        in_specs=[tok, tok, tok, _mod_spec(1, 2, row_off, row_stride),
                  pl.BlockSpec((D, D), lambda b, j: (0, 0)), vec, vec],
        out_specs=tok,
        out_shape=jax.ShapeDtypeStruct((B, T, D), F32),
        compiler_params=_cparams(("arbitrary", "arbitrary")),
        name="outproj1",
    )(x, h, o, mods, w_out1.astype(BF16), ln_g.reshape(1, D), ln_b.reshape(1, D))


def _to_col_major(x):
    B, T, C = x.shape
    rows = T // GRID_W
    return x.reshape(B, rows, GRID_W, C).transpose(0, 2, 1, 3).reshape(B, T, C)


def _to_row_major(x):
    B, T, C = x.shape
    rows = T // GRID_W
    return x.reshape(B, GRID_W, rows, C).transpose(0, 2, 1, 3).reshape(B, T, C)


def _mixer0(x, mods, st, P, s5_mats):
    h_rg, s_re, s_im = st
    xa, ga, ub = _inproj0(x, mods, P['w_in0'][0].astype(BF16))
    h, h_fin = _rglru(xa, P['conv_a_w'][0], P['conv_a_b'][0], P['rg_wa'][0], P['rg_ba'][0],
                      P['rg_wi'][0], P['rg_bi'][0], P['rg_lam'][0], h_rg[:, 0].transpose(1, 0, 2))
    y5, s_fin = _s5(ub, s5_mats, _s5_state_to_lanes(s_re[:, 0], s_im[:, 0]))
    x = _outproj0(x, h, ga, y5, ub, mods, P['s5_d'][0], P['glu_w'][0], P['glu_b'][0], P['w_out0'][0],
                  P['ln_g'][0, 0], P['ln_b'][0, 0])
    new_re, new_im = _s5_lanes_to_state(s_fin)
    return x, (h_fin.transpose(1, 0, 2)[:, None], new_re[:, None], new_im[:, None])


def _mixer1(x, mods, st, P):
    m_c, m_n, m_m = st
    B, T, D = x.shape
    qk, v, o, gcol, grow = _inproj1(x, mods, P['w_in1'][0], P['w_gate1'][0], P['b_gate1'][0],
                                    P['conv_c_w'][0], P['conv_c_b'][0], 0, 1)
    gcol = gcol.reshape(B, T, 2, 2, H_C).transpose(0, 4, 1, 2, 3).reshape(B, H_C, T, 4)
    grow = grow.reshape(2, 2, H_C, B, T).transpose(3, 2, 0, 1, 4).reshape(B, H_C, 4, T)
    h, c_fin, n_fin, m_fin = _mlstm(qk, v, gcol, grow,
                                    m_c[:, 0], m_n[:, 0][:, :, :, None, :], m_m[:, 0][:, :, :, None, None])
    x = _outproj1(x, h, o, mods, P['w_out1'][0], P['ln_g'][1, 0], P['ln_b'][1, 0], 0, 1)
    return x, (c_fin[:, None], n_fin[:, None, :, :, 0, :], m_fin[:, None, :, :, 0, 0])


def _forward(x_prompt, x_sample, c, c_ctx, states, P):
    bp = x_prompt.shape[0]
    bs = x_sample.shape[0]
    rows = 1 + bs
    rpad = -(-rows // 8) * 8
    cv = jnp.concatenate([c_ctx[None, :], c, jnp.zeros((rpad - rows, D_MODEL), F32)], axis=0)
    mods = _modulation(cv, P['w_mod'], P['b_mod'])
    s5_mats = _s5_matrices(P['s5_a_re'][0], P['s5_a_im'][0], P['s5_log_dt'][0], P['s5_b_re'][0], P['s5_b_im'][0],
                           P['s5_c_re'][0], P['s5_c_im'][0])
    zero_state = (jnp.zeros((bp, 1, 2, D_A), F32),
                  jnp.zeros((bp, 1, 2, G_B, P_B), F32),
                  jnp.zeros((bp, 1, 2, G_B, P_B), F32),
                  jnp.zeros((bp, 1, 2, H_C, DH_C, DH_C), F32),
                  jnp.zeros((bp, 1, 2, H_C, DH_C), F32),
                  jnp.zeros((bp, 1, 2, H_C), F32))
    mods_ctx = jnp.broadcast_to(mods[:, :, 0:1], mods.shape[:2] + (bp,) + mods.shape[3:])
    mods_lat = mods[:, :, 1:1 + bs]
    xc, new_even = _mixer0(x_prompt, mods_ctx, zero_state[:3], P, s5_mats)
    xl, _ = _mixer0(x_sample, mods_lat, states[:3], P, s5_mats)
    xc, xl = _moe([(xc, mods_ctx, True), (xl, mods_lat, False)], 0, P)
    xl = _to_col_major(xl)
    xc, new_odd = _mixer1(xc, mods_ctx, zero_state[3:], P)
    xl, _ = _mixer1(xl, mods_lat, states[3:], P)
    xc, xl = _moe([(xc, mods_ctx, True), (xl, mods_lat, False)], 1, P)
    return (xc, _to_row_major(xl)) + tuple(new_even) + tuple(new_odd)


def kernel(x_prompt, x_sample, c, c_ctx, state_rglru, state_s5_re, state_s5_im, state_mlstm_C, state_mlstm_n, state_mlstm_m, w_mod, b_mod, ln_g, ln_b, w_in0, conv_a_w, conv_a_b, rg_wa, rg_ba, rg_wi, rg_bi, rg_lam, s5_a_re, s5_a_im, s5_log_dt, s5_b_re, s5_b_im, s5_c_re, s5_c_im, s5_d, glu_w, glu_b, w_out0, w_in1, w_gate1, b_gate1, conv_c_w, conv_c_b, w_out1, w_router, b_router, w_gu, b_gu, w_down, b_down):
    P = dict(w_mod=w_mod, b_mod=b_mod, ln_g=ln_g, ln_b=ln_b, w_in0=w_in0, conv_a_w=conv_a_w,
             conv_a_b=conv_a_b, rg_wa=rg_wa, rg_ba=rg_ba, rg_wi=rg_wi, rg_bi=rg_bi, rg_lam=rg_lam,
             s5_a_re=s5_a_re, s5_a_im=s5_a_im, s5_log_dt=s5_log_dt, s5_b_re=s5_b_re, s5_b_im=s5_b_im,
             s5_c_re=s5_c_re, s5_c_im=s5_c_im, s5_d=s5_d, glu_w=glu_w, glu_b=glu_b, w_out0=w_out0,
             w_in1=w_in1, w_gate1=w_gate1, b_gate1=b_gate1, conv_c_w=conv_c_w, conv_c_b=conv_c_b,
             w_out1=w_out1, w_router=w_router, b_router=b_router, w_gu=w_gu, b_gu=b_gu,
             w_down=w_down, b_down=b_down)
    states = (state_rglru, state_s5_re, state_s5_im, state_mlstm_C, state_mlstm_n, state_mlstm_m)
    return _forward(x_prompt, x_sample, c, c_ctx, states, P)
```
